```python
import jax, jax.numpy as jnp
from jax import lax
import numpy as np

D_MODEL = 1024
BATCH = 8
SEQ = 16384
DEPTH = 1

CHUNK = 64
Q_BLOCK = 2 * CHUNK
ATTN_WIDTH = D_MODEL // 2
CONV_WIDTH = D_MODEL - ATTN_WIDTH
HEAD_DIM = 64
N_HEADS = ATTN_WIDTH // HEAD_DIM
CONV_KERNEL = 31
D_FF = 4 * D_MODEL
IN_WIDTH = 3 * ATTN_WIDTH + N_HEADS + 2 * CONV_WIDTH
EPS = 1e-6

kernel_name = "hybrid_fox_conformer_conv_adaln_block"


def _rmsnorm(x, g):
    xf = x.astype(jnp.float32)
    y = xf * lax.rsqrt(jnp.mean(xf * xf, axis=-1, keepdims=True) + EPS)
    return (y * g.astype(jnp.float32)).astype(x.dtype)


def _layernorm(x, g, b):
    xf = x.astype(jnp.float32)
    mu = jnp.mean(xf, axis=-1, keepdims=True)
    var = jnp.mean(jnp.square(xf - mu), axis=-1, keepdims=True)
    y = (xf - mu) * lax.rsqrt(var + EPS)
    return (y * g.astype(jnp.float32) + b.astype(jnp.float32)).astype(x.dtype)


def _fox_attention(q, k, v, log_f):
    B, S, H, Dh = q.shape
    nb = S // Q_BLOCK
    F = jnp.cumsum(log_f, axis=1).transpose(0, 2, 1)
    qh = q.transpose(0, 2, 1, 3)
    kh = k.transpose(0, 2, 1, 3)
    vh = v.transpose(0, 2, 1, 3)
    q_blocks = qh.reshape(B, H, nb, Q_BLOCK, Dh).transpose(2, 0, 1, 3, 4)
    F_blocks = F.reshape(B, H, nb, Q_BLOCK).transpose(2, 0, 1, 3)
    k_pos = jnp.arange(S)
    scale = HEAD_DIM ** -0.5

    def block(args):
        qb, Fb, i = args
        logits = jnp.einsum('bhqd,bhkd->bhqk', qb, kh).astype(jnp.float32) * scale
        logits = logits + Fb[..., None] - F[:, :, None, :]
        q_pos = i * Q_BLOCK + jnp.arange(Q_BLOCK)
        mask = k_pos[None, :] <= q_pos[:, None]
        logits = jnp.where(mask[None, None], logits, -jnp.inf)
        p = jax.nn.softmax(logits, axis=-1)
        return jnp.einsum('bhqk,bhkd->bhqd', p.astype(vh.dtype), vh)

    out = lax.map(block, (q_blocks, F_blocks, jnp.arange(nb)))
    return out.transpose(1, 0, 3, 2, 4).reshape(B, S, H * Dh)


def _causal_depthwise_conv(u, w, b):
    K, C = w.shape
    u_pad = jnp.pad(u, ((0, 0), (K - 1, 0), (0, 0)))
    y = lax.conv_general_dilated(u_pad, w[:, None, :], window_strides=(1,), padding='VALID',
                                 dimension_numbers=('NWC', 'WIO', 'NWC'),
                                 feature_group_count=C)
    return y + b


def _fwd_setup_inputs(seed: int = 0) -> dict:
    key = jax.random.key(seed)
    ks = jax.random.split(key, 20)
    f32 = jnp.float32
    L, D, A, Cw, H = DEPTH, D_MODEL, ATTN_WIDTH, CONV_WIDTH, N_HEADS
    nrm = lambda k, shape, s: jax.random.normal(k, shape, f32) * s
    return {
        "x": jax.random.normal(ks[0], (BATCH, SEQ, D), f32),
        "c": jax.random.normal(ks[1], (BATCH, D), f32),
        "w_ada": nrm(ks[2], (L, D, 6 * D), 0.5 * D ** -0.5),
        "b_ada": nrm(ks[3], (L, 6 * D), 0.02),
        "norm1_g": 1.0 + nrm(ks[4], (L, D), 0.02),
        "w_in": nrm(ks[5], (L, D, IN_WIDTH), D ** -0.5),
        "q_norm_g": 1.0 + nrm(ks[6], (L, HEAD_DIM), 0.02),
        "k_norm_g": 1.0 + nrm(ks[7], (L, HEAD_DIM), 0.02),
        "b_f": jax.random.uniform(ks[8], (L, H), f32, minval=1.0, maxval=6.0),
        "conv_w": nrm(ks[9], (L, CONV_KERNEL, Cw), CONV_KERNEL ** -0.5),
        "conv_b": nrm(ks[10], (L, Cw), 0.02),
        "conv_ln_g": 1.0 + nrm(ks[11], (L, Cw), 0.02),
        "conv_ln_b": nrm(ks[12], (L, Cw), 0.02),
        "beta_attn": 1.0 + nrm(ks[13], (L, A), 0.02),
        "beta_conv": 1.0 + nrm(ks[14], (L, Cw), 0.02),
        "w_out": nrm(ks[15], (L, D, D), D ** -0.5),
        "norm2_g": 1.0 + nrm(ks[16], (L, D), 0.02),
        "w_ff1": nrm(ks[17], (L, D, D_FF), D ** -0.5),
        "w_ff2": nrm(ks[18], (L, D_FF, D), D_FF ** -0.5),
    }


def _fwd_reference(x, c, w_ada, b_ada, norm1_g, w_in, q_norm_g, k_norm_g, b_f, conv_w, conv_b,
              conv_ln_g, conv_ln_b, beta_attn, beta_conv, w_out, norm2_g, w_ff1, w_ff2):
    B, S, D = x.shape
    A, H = ATTN_WIDTH, N_HEADS
    split_pts = [A, 2 * A, 3 * A, 3 * A + H]
    for l in range(DEPTH):
        mod = jax.nn.silu(c) @ w_ada[l] + b_ada[l]
        sh1, sc1, g1, sh2, sc2, g2 = [m[:, None, :] for m in jnp.split(mod, 6, axis=-1)]

        h = _rmsnorm(x, norm1_g[l]) * (1 + sc1) + sh1
        z = h @ w_in[l]
        q, k, v, fg, conv_in = jnp.split(z, split_pts, axis=-1)

        q = _rmsnorm(q.reshape(B, S, H, HEAD_DIM), q_norm_g[l])
        k = _rmsnorm(k.reshape(B, S, H, HEAD_DIM), k_norm_g[l])
        v = v.reshape(B, S, H, HEAD_DIM)
        log_f = jax.nn.log_sigmoid(fg.astype(jnp.float32) + b_f[l].astype(jnp.float32))
        attn = _fox_attention(q, k, v, log_f)

        a_lin, a_gate = jnp.split(conv_in, 2, axis=-1)
        u = a_lin * jax.nn.sigmoid(a_gate)
        u = _causal_depthwise_conv(u, conv_w[l], conv_b[l])
        u = jax.nn.silu(_layernorm(u, conv_ln_g[l], conv_ln_b[l]))

        merged = jnp.concatenate([_rmsnorm(attn, beta_attn[l]), _rmsnorm(u, beta_conv[l])], axis=-1)
        x = x + g1 * (merged @ w_out[l])

        h = _rmsnorm(x, norm2_g[l]) * (1 + sc2) + sh2
        x = x + g2 * (jnp.square(jax.nn.relu(h @ w_ff1[l])) @ w_ff2[l])
    return x


import jax as _jax
import jax.numpy as _jnp

TWIN_FORMAT = 'train_step'
FWD_PARAMS = ['x', 'c', 'w_ada', 'b_ada', 'norm1_g', 'w_in', 'q_norm_g', 'k_norm_g', 'b_f', 'conv_w', 'conv_b', 'conv_ln_g', 'conv_ln_b', 'beta_attn', 'beta_conv', 'w_out', 'norm2_g', 'w_ff1', 'w_ff2']
TWIN_WEIGHTS = ['w_ada', 'b_ada', 'norm1_g', 'w_in', 'q_norm_g', 'k_norm_g', 'b_f', 'conv_w', 'conv_b', 'conv_ln_g', 'conv_ln_b', 'beta_attn', 'beta_conv', 'w_out', 'norm2_g', 'w_ff1', 'w_ff2']
TWIN_DIFF_INPUT = 'x'
TWIN_INPUTS = ['x', 'c', 'w_ada', 'b_ada', 'norm1_g', 'w_in', 'q_norm_g', 'k_norm_g', 'b_f', 'conv_w', 'conv_b', 'conv_ln_g', 'conv_ln_b', 'beta_attn', 'beta_conv', 'w_out', 'norm2_g', 'w_ff1', 'w_ff2', 'loss_target', 'm_w_ada', 'm_b_ada', 'm_norm1_g', 'm_w_in', 'm_q_norm_g', 'm_k_norm_g', 'm_b_f', 'm_conv_w', 'm_conv_b', 'm_conv_ln_g', 'm_conv_ln_b', 'm_beta_attn', 'm_beta_conv', 'm_w_out', 'm_norm2_g', 'm_w_ff1', 'm_w_ff2', 'v_w_ada', 'v_b_ada', 'v_norm1_g', 'v_w_in', 'v_q_norm_g', 'v_k_norm_g', 'v_b_f', 'v_conv_w', 'v_conv_b', 'v_conv_ln_g', 'v_conv_ln_b', 'v_beta_attn', 'v_beta_conv', 'v_w_out', 'v_norm2_g', 'v_w_ff1', 'v_w_ff2']
TWIN_OUTPUTS = ['loss', 'grad_x', 'grad_w_ada', 'grad_b_ada', 'grad_norm1_g', 'grad_w_in', 'grad_q_norm_g', 'grad_k_norm_g', 'grad_b_f', 'grad_conv_w', 'grad_conv_b', 'grad_conv_ln_g', 'grad_conv_ln_b', 'grad_beta_attn', 'grad_beta_conv', 'grad_w_out', 'grad_norm2_g', 'grad_w_ff1', 'grad_w_ff2', 'delta_w_ada', 'delta_b_ada', 'delta_norm1_g', 'delta_w_in', 'delta_q_norm_g', 'delta_k_norm_g', 'delta_b_f', 'delta_conv_w', 'delta_conv_b', 'delta_conv_ln_g', 'delta_conv_ln_b', 'delta_beta_attn', 'delta_beta_conv', 'delta_w_out', 'delta_norm2_g', 'delta_w_ff1', 'delta_w_ff2', 'new_m_w_ada', 'new_m_b_ada', 'new_m_norm1_g', 'new_m_w_in', 'new_m_q_norm_g', 'new_m_k_norm_g', 'new_m_b_f', 'new_m_conv_w', 'new_m_conv_b', 'new_m_conv_ln_g', 'new_m_conv_ln_b', 'new_m_beta_attn', 'new_m_beta_conv', 'new_m_w_out', 'new_m_norm2_g', 'new_m_w_ff1', 'new_m_w_ff2', 'new_v_w_ada', 'new_v_b_ada', 'new_v_norm1_g', 'new_v_w_in', 'new_v_q_norm_g', 'new_v_k_norm_g', 'new_v_b_f', 'new_v_conv_w', 'new_v_conv_b', 'new_v_conv_ln_g', 'new_v_conv_ln_b', 'new_v_beta_attn', 'new_v_beta_conv', 'new_v_w_out', 'new_v_norm2_g', 'new_v_w_ff1', 'new_v_w_ff2']
TWIN_LEAF_KINDS = {'loss': 'loss', 'grad_x': 'grad_x', 'grad_w_ada': 'grad_w', 'grad_b_ada': 'grad_w', 'grad_norm1_g': 'grad_w', 'grad_w_in': 'grad_w', 'grad_q_norm_g': 'grad_w', 'grad_k_norm_g': 'grad_w', 'grad_b_f': 'grad_w', 'grad_conv_w': 'grad_w', 'grad_conv_b': 'grad_w', 'grad_conv_ln_g': 'grad_w', 'grad_conv_ln_b': 'grad_w', 'grad_beta_attn': 'grad_w', 'grad_beta_conv': 'grad_w', 'grad_w_out': 'grad_w', 'grad_norm2_g': 'grad_w', 'grad_w_ff1': 'grad_w', 'grad_w_ff2': 'grad_w', 'delta_w_ada': 'delta_w', 'delta_b_ada': 'delta_w', 'delta_norm1_g': 'delta_w', 'delta_w_in': 'delta_w', 'delta_q_norm_g': 'delta_w', 'delta_k_norm_g': 'delta_w', 'delta_b_f': 'delta_w', 'delta_conv_w': 'delta_w', 'delta_conv_b': 'delta_w', 'delta_conv_ln_g': 'delta_w', 'delta_conv_ln_b': 'delta_w', 'delta_beta_attn': 'delta_w', 'delta_beta_conv': 'delta_w', 'delta_w_out': 'delta_w', 'delta_norm2_g': 'delta_w', 'delta_w_ff1': 'delta_w', 'delta_w_ff2': 'delta_w', 'new_m_w_ada': 'new_m', 'new_m_b_ada': 'new_m', 'new_m_norm1_g': 'new_m', 'new_m_w_in': 'new_m', 'new_m_q_norm_g': 'new_m', 'new_m_k_norm_g': 'new_m', 'new_m_b_f': 'new_m', 'new_m_conv_w': 'new_m', 'new_m_conv_b': 'new_m', 'new_m_conv_ln_g': 'new_m', 'new_m_conv_ln_b': 'new_m', 'new_m_beta_attn': 'new_m', 'new_m_beta_conv': 'new_m', 'new_m_w_out': 'new_m', 'new_m_norm2_g': 'new_m', 'new_m_w_ff1': 'new_m', 'new_m_w_ff2': 'new_m', 'new_v_w_ada': 'new_v', 'new_v_b_ada': 'new_v', 'new_v_norm1_g': 'new_v', 'new_v_w_in': 'new_v', 'new_v_q_norm_g': 'new_v', 'new_v_k_norm_g': 'new_v', 'new_v_b_f': 'new_v', 'new_v_conv_w': 'new_v', 'new_v_conv_b': 'new_v', 'new_v_conv_ln_g': 'new_v', 'new_v_conv_ln_b': 'new_v', 'new_v_beta_attn': 'new_v', 'new_v_beta_conv': 'new_v', 'new_v_w_out': 'new_v', 'new_v_norm2_g': 'new_v', 'new_v_w_ff1': 'new_v', 'new_v_w_ff2': 'new_v'}


def _forward(args):
    return _fwd_reference(*[args[k] for k in FWD_PARAMS])


def _output_shape():
    def fwd():
        inp = _fwd_setup_inputs(0)
        return _fwd_reference(*[inp[k] for k in FWD_PARAMS])
    out = _jax.eval_shape(fwd)
    return out.shape, out.dtype

N_MICROBATCH = 1
ADAM_LR = 0.001
ADAM_B1 = 0.9
ADAM_B2 = 0.999
ADAM_EPS = 1e-08
ADAM_WD = 0.01
ADAM_STEP = 10
PER_EXAMPLE_BATCH_AXIS = {'x': 0, 'c': 0, 'loss_target': 0}
SHARED_INPUTS = []
_WEIGHT_DTYPES = {'w_ada': _jnp.float32, 'b_ada': _jnp.float32, 'norm1_g': _jnp.float32, 'w_in': _jnp.float32, 'q_norm_g': _jnp.float32, 'k_norm_g': _jnp.float32, 'b_f': _jnp.float32, 'conv_w': _jnp.float32, 'conv_b': _jnp.float32, 'conv_ln_g': _jnp.float32, 'conv_ln_b': _jnp.float32, 'beta_attn': _jnp.float32, 'beta_conv': _jnp.float32, 'w_out': _jnp.float32, 'norm2_g': _jnp.float32, 'w_ff1': _jnp.float32, 'w_ff2': _jnp.float32}
MOMENT_SCALE = {'w_ada': 1.076182e+01, 'b_ada': 2.729960e+01, 'norm1_g': 2.524762e-01, 'w_in': 1.386954e+00, 'q_norm_g': 3.708534e-01, 'k_norm_g': 3.688859e-01, 'b_f': 2.382742e+00, 'conv_w': 9.473011e-01, 'conv_b': 7.230833e+00, 'conv_ln_g': 3.333732e+00, 'conv_ln_b': 4.865189e+00, 'beta_attn': 1.382811e+01, 'beta_conv': 1.246305e+01, 'w_out': 2.784628e+00, 'norm2_g': 4.837034e+01, 'w_ff1': 1.204310e+00, 'w_ff2': 5.019634e+00}


def _to_microbatches(a, axis):
    t = _jnp.moveaxis(a, axis, 0)
    t = t.reshape((N_MICROBATCH, t.shape[0] // N_MICROBATCH) + t.shape[1:])
    return _jnp.moveaxis(t, 1, axis + 1)


def setup_inputs(seed: int = 0) -> dict:
    inp = _fwd_setup_inputs(seed)
    key = _jax.random.fold_in(_jax.random.key(seed), 7919)
    shape, _ = _output_shape()
    out = dict(inp)
    out["loss_target"] = _jax.random.normal(_jax.random.fold_in(key, 0), shape, _jnp.float32)
    for i, name in enumerate(TWIN_WEIGHTS):
        w = inp[name].astype(_jnp.float32)
        if MOMENT_SCALE is None:
            s = _jnp.sqrt(_jnp.mean(_jnp.square(w)) + 1e-30)
        else:
            s = MOMENT_SCALE[name]
        km, kv = _jax.random.split(_jax.random.fold_in(key, i + 1))
        out[name] = w
        out["m_" + name] = s * _jax.random.normal(km, w.shape, _jnp.float32)
        out["v_" + name] = (s * s) * _jax.random.uniform(kv, w.shape, _jnp.float32, 0.5, 1.5)
    if N_MICROBATCH > 1:
        for name, axis in PER_EXAMPLE_BATCH_AXIS.items():
            out[name] = _to_microbatches(out[name], axis)
    return {'x': out['x'], 'c': out['c'], 'w_ada': out['w_ada'], 'b_ada': out['b_ada'], 'norm1_g': out['norm1_g'], 'w_in': out['w_in'], 'q_norm_g': out['q_norm_g'], 'k_norm_g': out['k_norm_g'], 'b_f': out['b_f'], 'conv_w': out['conv_w'], 'conv_b': out['conv_b'], 'conv_ln_g': out['conv_ln_g'], 'conv_ln_b': out['conv_ln_b'], 'beta_attn': out['beta_attn'], 'beta_conv': out['beta_conv'], 'w_out': out['w_out'], 'norm2_g': out['norm2_g'], 'w_ff1': out['w_ff1'], 'w_ff2': out['w_ff2'], 'loss_target': out['loss_target'], 'm_w_ada': out['m_w_ada'], 'm_b_ada': out['m_b_ada'], 'm_norm1_g': out['m_norm1_g'], 'm_w_in': out['m_w_in'], 'm_q_norm_g': out['m_q_norm_g'], 'm_k_norm_g': out['m_k_norm_g'], 'm_b_f': out['m_b_f'], 'm_conv_w': out['m_conv_w'], 'm_conv_b': out['m_conv_b'], 'm_conv_ln_g': out['m_conv_ln_g'], 'm_conv_ln_b': out['m_conv_ln_b'], 'm_beta_attn': out['m_beta_attn'], 'm_beta_conv': out['m_beta_conv'], 'm_w_out': out['m_w_out'], 'm_norm2_g': out['m_norm2_g'], 'm_w_ff1': out['m_w_ff1'], 'm_w_ff2': out['m_w_ff2'], 'v_w_ada': out['v_w_ada'], 'v_b_ada': out['v_b_ada'], 'v_norm1_g': out['v_norm1_g'], 'v_w_in': out['v_w_in'], 'v_q_norm_g': out['v_q_norm_g'], 'v_k_norm_g': out['v_k_norm_g'], 'v_b_f': out['v_b_f'], 'v_conv_w': out['v_conv_w'], 'v_conv_b': out['v_conv_b'], 'v_conv_ln_g': out['v_conv_ln_g'], 'v_conv_ln_b': out['v_conv_ln_b'], 'v_beta_attn': out['v_beta_attn'], 'v_beta_conv': out['v_beta_conv'], 'v_w_out': out['v_w_out'], 'v_norm2_g': out['v_norm2_g'], 'v_w_ff1': out['v_w_ff1'], 'v_w_ff2': out['v_w_ff2']}


def _loss(weights, diff, rest, loss_target):
    with _jax.named_scope("forward"):
        args = {**rest, TWIN_DIFF_INPUT: diff, **{k: w.astype(_WEIGHT_DTYPES[k]) for k, w in weights.items()}}
        y = _forward(args)
    with _jax.named_scope("loss_head"):
        err = _jnp.square(y.astype(_jnp.float32) - loss_target)
        return 0.5 * _jnp.sum(_jnp.mean(err, axis=-1)) if err.ndim else 0.5 * err


def _adamw(w, g, m, v):
    m = ADAM_B1 * m + (1.0 - ADAM_B1) * g
    v = ADAM_B2 * v + (1.0 - ADAM_B2) * _jnp.square(g)
    m_hat = m / (1.0 - ADAM_B1 ** ADAM_STEP)
    v_hat = v / (1.0 - ADAM_B2 ** ADAM_STEP)
    delta = -ADAM_LR * (m_hat / (_jnp.sqrt(v_hat) + ADAM_EPS) + ADAM_WD * w)
    return delta, m, v


def reference(x, c, w_ada, b_ada, norm1_g, w_in, q_norm_g, k_norm_g, b_f, conv_w, conv_b, conv_ln_g, conv_ln_b, beta_attn, beta_conv, w_out, norm2_g, w_ff1, w_ff2, loss_target, m_w_ada, m_b_ada, m_norm1_g, m_w_in, m_q_norm_g, m_k_norm_g, m_b_f, m_conv_w, m_conv_b, m_conv_ln_g, m_conv_ln_b, m_beta_attn, m_beta_conv, m_w_out, m_norm2_g, m_w_ff1, m_w_ff2, v_w_ada, v_b_ada, v_norm1_g, v_w_in, v_q_norm_g, v_k_norm_g, v_b_f, v_conv_w, v_conv_b, v_conv_ln_g, v_conv_ln_b, v_beta_attn, v_beta_conv, v_w_out, v_norm2_g, v_w_ff1, v_w_ff2):
    given = dict(x=x, c=c, w_ada=w_ada, b_ada=b_ada, norm1_g=norm1_g, w_in=w_in, q_norm_g=q_norm_g, k_norm_g=k_norm_g, b_f=b_f, conv_w=conv_w, conv_b=conv_b, conv_ln_g=conv_ln_g, conv_ln_b=conv_ln_b, beta_attn=beta_attn, beta_conv=beta_conv, w_out=w_out, norm2_g=norm2_g, w_ff1=w_ff1, w_ff2=w_ff2, loss_target=loss_target, m_w_ada=m_w_ada, m_b_ada=m_b_ada, m_norm1_g=m_norm1_g, m_w_in=m_w_in, m_q_norm_g=m_q_norm_g, m_k_norm_g=m_k_norm_g, m_b_f=m_b_f, m_conv_w=m_conv_w, m_conv_b=m_conv_b, m_conv_ln_g=m_conv_ln_g, m_conv_ln_b=m_conv_ln_b, m_beta_attn=m_beta_attn, m_beta_conv=m_beta_conv, m_w_out=m_w_out, m_norm2_g=m_norm2_g, m_w_ff1=m_w_ff1, m_w_ff2=m_w_ff2, v_w_ada=v_w_ada, v_b_ada=v_b_ada, v_norm1_g=v_norm1_g, v_w_in=v_w_in, v_q_norm_g=v_q_norm_g, v_k_norm_g=v_k_norm_g, v_b_f=v_b_f, v_conv_w=v_conv_w, v_conv_b=v_conv_b, v_conv_ln_g=v_conv_ln_g, v_conv_ln_b=v_conv_ln_b, v_beta_attn=v_beta_attn, v_beta_conv=v_beta_conv, v_w_out=v_w_out, v_norm2_g=v_norm2_g, v_w_ff1=v_w_ff1, v_w_ff2=v_w_ff2)
    weights = {n: given[n] for n in TWIN_WEIGHTS}
    shared = {n: given[n] for n in SHARED_INPUTS}
    per_example = {n: given[n] for n in ['x', 'c']}
    grad_fn = _jax.value_and_grad(_loss, argnums=(0, 1))

    def one_microbatch(ex, loss_target):
        ex = dict(ex)
        diff = ex.pop(TWIN_DIFF_INPUT)
        return grad_fn(weights, diff, {**shared, **ex}, loss_target)

    if N_MICROBATCH == 1:
        loss, (grad_w, grad_x) = one_microbatch(per_example, given["loss_target"])
    else:
        def body(carry, xs):
            loss_sum, grad_sum = carry
            l_k, (gw_k, gx_k) = one_microbatch(xs[0], xs[1])
            with _jax.named_scope("update"):
                return (loss_sum + l_k, _jax.tree.map(_jnp.add, grad_sum, gw_k)), gx_k

        init = (_jnp.zeros((), _jnp.float32), _jax.tree.map(_jnp.zeros_like, weights))
        (loss, grad_w), grad_x = _jax.lax.scan(body, init, (per_example, given["loss_target"]))
    with _jax.named_scope("update"):
        delta_w, new_m, new_v = {}, {}, {}
        for n in TWIN_WEIGHTS:
            delta_w[n], new_m[n], new_v[n] = _adamw(weights[n], grad_w[n], given["m_" + n], given["v_" + n])
    return (loss, grad_x, *[grad_w[n] for n in TWIN_WEIGHTS], *[delta_w[n] for n in TWIN_WEIGHTS],
            *[new_m[n] for n in TWIN_WEIGHTS], *[new_v[n] for n in TWIN_WEIGHTS])
```

```python
import functools

import jax
import jax.numpy as jnp
from jax import lax
from jax.experimental import pallas as pl
from jax.experimental.pallas import tpu as pltpu

F32 = jnp.float32
BF16 = jnp.bfloat16

D = 1024
A = 512
CW = 512
H = 8
DH = 64
FF = 4096
KC = 31
HALO = 32
KR = 80
ZT_ROWS = 1552
FCH = 1024
EPS = 1e-6
BLK = 512
N_DEV = 8
VMEM_LIMIT = 56 * 1024 * 1024

ADAM_LR = 0.001
ADAM_B1 = 0.9
ADAM_B2 = 0.999
ADAM_EPS = 1e-08
ADAM_WD = 0.01
ADAM_STEP = 10

MESH = pl.DeviceIdType.MESH
ANY = pl.BlockSpec(memory_space=pl.ANY)


def _pc(body, **kw):
    return pl.pallas_call(body, **kw)


def _cp(sem=None):
    return pltpu.CompilerParams(dimension_semantics=sem, vmem_limit_bytes=VMEM_LIMIT)


def _sds(shape, dtype=F32):
    return jax.ShapeDtypeStruct(shape, dtype)


def _const(shape):
    n = len(shape)
    return pl.BlockSpec(shape, lambda *a: (0,) * n)


def _nt(a, b):
    return lax.dot_general(a, b, (((1,), (1,)), ((), ())), preferred_element_type=F32)


def _nn(a, b):
    return jnp.dot(a, b, preferred_element_type=F32)


def _sigmoid(v):
    return 1.0 / (1.0 + jnp.exp(-v))


def _split3(v):
    a1 = v.astype(BF16)
    r1 = v - a1.astype(F32)
    a2 = r1.astype(BF16)
    a3 = (r1 - a2.astype(F32)).astype(BF16)
    return a1, a2, a3


def _rowmean(v):
    return jnp.mean(v, axis=-1, keepdims=True)


def _colsum(v):
    return jnp.sum(v, axis=0, keepdims=True)


def _lanesum(v):
    return jnp.sum(v, axis=-1, keepdims=True)


def _coords():
    return lax.axis_index("x"), lax.axis_index("y"), lax.axis_index("c")


def all_gather(x, name):
    R, C = x.shape

    def body(x_ref, out_ref, send_sems, recv_sems, local_sem):
        mx, my, mc = _coords()
        me, sibling = (mx, my, mc), (mx, my, 1 - mc)
        chips = [(1 - mx, my), (mx, 1 - my), (1 - mx, 1 - my)]

        def rows(px, py, pc):
            return out_ref.at[4 * px + 2 * py + pc]

        def copy(k, block, to, src=None):
            return pltpu.make_async_remote_copy(
                src_ref=rows(*block) if src is None else src, dst_ref=rows(*block),
                send_sem=send_sems.at[k], recv_sem=recv_sems.at[k],
                device_id=to, device_id_type=MESH)

        mine = pltpu.make_async_copy(x_ref, rows(*me), local_sem)
        mine.start()
        first = [copy(0, me, sibling, src=x_ref)]
        first += [copy(1 + j, me, (*chip, mc), src=x_ref) for j, chip in enumerate(chips)]
        for cp in first:
            cp.start()
        passed = [copy(4 + j, (*chip, mc), sibling) for j, chip in enumerate(chips)]
        for j, chip in enumerate(chips):
            copy(1 + j, (*chip, mc), me).wait_recv()
            passed[j].start()
        copy(0, sibling, me).wait_recv()
        for j, chip in enumerate(chips):
            copy(4 + j, (*chip, 1 - mc), me).wait_recv()
        for cp in first + passed:
            cp.wait_send()
        mine.wait()

    return _pc(
        body, name=name, out_shape=_sds((N_DEV, R, C), x.dtype),
        in_specs=[ANY], out_specs=ANY,
        scratch_shapes=[pltpu.SemaphoreType.DMA((7,)), pltpu.SemaphoreType.DMA((7,)),
                        pltpu.SemaphoreType.DMA(())],
    )(x)


def reduce_scatter_exchange(g, name):
    _, R, C = g.shape

    def body(g_ref, recv_ref, send_sems, recv_sems, local_sem):
        mx, my, mc = _coords()
        me = 4 * mx + 2 * my + mc
        local = pltpu.make_async_copy(g_ref.at[me], recv_ref.at[0], local_sem)
        local.start()
        copies = []
        for r in range(1, N_DEV):
            px = 1 - mx if r & 4 else mx
            py = 1 - my if r & 2 else my
            pcc = 1 - mc if r & 1 else mc
            cp = pltpu.make_async_remote_copy(
                src_ref=g_ref.at[4 * px + 2 * py + pcc], dst_ref=recv_ref.at[r],
                send_sem=send_sems.at[r - 1], recv_sem=recv_sems.at[r - 1],
                device_id=(px, py, pcc), device_id_type=MESH)
            cp.start()
            copies.append(cp)
        for cp in copies:
            cp.wait()
        local.wait()

    return _pc(
        body, name=name, out_shape=_sds(g.shape, g.dtype),
        in_specs=[ANY], out_specs=ANY,
        scratch_shapes=[pltpu.SemaphoreType.DMA((7,)), pltpu.SemaphoreType.DMA((7,)),
                        pltpu.SemaphoreType.DMA(())],
    )(g)


def mod_kernel(c, w_ada_b, b_ada):
    def body(c_ref, w_ref, b_ref, o_ref):
        cv = c_ref[...]
        sc = jnp.broadcast_to(cv * _sigmoid(cv), (8, D)).astype(BF16)
        o_ref[...] = _nn(sc, w_ref[...])[0:1, :] + b_ref[...]

    return _pc(body, name="mod", out_shape=_sds((1, 6 * D)),
               in_specs=[_const((1, D)), _const((D, 6 * D)), _const((1, 6 * D))],
               out_specs=_const((1, 6 * D)), grid=(1,), compiler_params=_cp())(c, w_ada_b, b_ada)


def _norm_mod(xv, g, sc, sh):
    r = lax.rsqrt(_rowmean(xv * xv) + EPS)
    xh = xv * r
    return r, xh, xh * (g * (1.0 + sc)) + sh


def _log_sigmoid(v):
    e = jnp.exp(-jnp.abs(v))
    l1p = jnp.where(e < 1e-4, e * (1.0 - 0.5 * e), jnp.log(1.0 + e))
    return jnp.minimum(v, 0.0) - l1p


def fwd_in(x, n1g, sc1, sh1, wT, wr, qg_col, kg_col, bf_col, tri):
    S = x.shape[0]
    B = BLK
    NB = S // B

    def body(x_ref, g_ref, sc_ref, sh_ref, wT_ref, wr_ref, qg_ref, kg_ref, bf_ref, tri_ref,
             qTa_ref, kT_ref, kaug_ref, vT_ref, vaug_ref, zqk_ref, fgT_ref, alg_ref, u0_ref,
             carry, tbuf, fs):
        i = pl.program_id(0)

        @pl.when(i == 0)
        def _():
            carry[...] = jnp.zeros_like(carry)

        _, _, h = _norm_mod(x_ref[...], g_ref[...], sc_ref[...], sh_ref[...])
        hb = h.astype(BF16)
        zT = _nt(wT_ref[...], hb)
        zr = _nn(hb, wr_ref[...])
        zqk_ref[...] = zT[0:2 * A]
        fgT = zT[3 * A:3 * A + 8]
        fgT_ref[...] = fgT
        alg_ref[...] = zr
        u0_ref[...] = zr[:, 0:CW] * _sigmoid(zr[:, CW:2 * CW])

        logf = _log_sigmoid(fgT + bf_ref[...])
        a1, a2, a3 = _split3(logf)
        tr = tri_ref[...]
        F = _nn(a1, tr) + _nn(a2, tr) + _nn(a3, tr) + carry[...]
        carry[...] = carry[...] + _lanesum(logf)
        p1, p2, p3 = _split3(F)
        n1, n2, n3 = _split3(-F)
        for k, v in enumerate((p1, p2, p3, n1, n2, n3)):
            fs[k] = v.astype(F32)

        rowi = lax.broadcasted_iota(jnp.int32, (8, B), 0)
        zeros_tail = jnp.zeros((128 - DH - 8, B), F32)
        ones_row = jnp.where(lax.broadcasted_iota(jnp.int32, (KR - DH, B), 0) == 0, 1.0, 0.0).astype(BF16)
        for hh in range(H):
            sl = slice(hh * DH, (hh + 1) * DH)
            q = zT[sl]
            k = zT[A + hh * DH:A + (hh + 1) * DH]
            v = zT[2 * A + hh * DH:2 * A + (hh + 1) * DH]
            qh = q * lax.rsqrt(jnp.mean(q * q, axis=0, keepdims=True) + EPS) * qg_ref[...] * 0.125
            kh = k * lax.rsqrt(jnp.mean(k * k, axis=0, keepdims=True) + EPS) * kg_ref[...]
            frow = [fs[kk, hh:hh + 1, :] for kk in range(6)]
            qx = jnp.where(rowi < 3, 1.0, jnp.where(rowi == 3, frow[0], jnp.where(
                rowi == 4, frow[1], jnp.where(rowi == 5, frow[2], 0.0))))
            kx = jnp.where(rowi == 0, frow[3], jnp.where(rowi == 1, frow[4], jnp.where(
                rowi == 2, frow[5], jnp.where(rowi < 6, 1.0, 0.0))))
            tbuf[0:DH, :] = qh
            tbuf[DH:DH + 8, :] = qx
            tbuf[DH + 8:128, :] = zeros_tail
            qTa_ref[0, hh * 128:(hh + 1) * 128, :] = tbuf[...].astype(BF16)
            tbuf[0:DH, :] = kh
            tbuf[DH:DH + 8, :] = kx
            kaug_ref[hh] = tbuf[...].T.astype(BF16)
            kT_ref[0, hh * KR:hh * KR + DH, :] = kh.astype(BF16)
            kT_ref[0, hh * KR + DH:(hh + 1) * KR, :] = ones_row
            tbuf[0:DH, :] = v
            tbuf[DH:DH + 8, :] = jnp.zeros((8, B), F32)
            vaug_ref[hh] = tbuf[...].T.astype(BF16)
            vT_ref[0, sl, :] = v.astype(BF16)

    row = lambda w: pl.BlockSpec((B, w), lambda i: (i, 0))
    tsp = lambda r: pl.BlockSpec((1, r, B), lambda i: (i, 0, 0))
    return _pc(
        body, name="fwd_in", grid=(NB,),
        in_specs=[row(D), _const((1, D)), _const((1, D)), _const((1, D)), _const((ZT_ROWS, D)),
                  _const((D, 2 * CW)), _const((DH, 1)), _const((DH, 1)), _const((8, 1)), _const((B, B))],
        out_specs=[tsp(H * 128), tsp(H * KR), pl.BlockSpec((H, B, 128), lambda i: (0, i, 0)), tsp(A),
                   pl.BlockSpec((H, B, 128), lambda i: (0, i, 0)),
                   pl.BlockSpec((2 * A, B), lambda i: (0, i)), pl.BlockSpec((8, B), lambda i: (0, i)),
                   row(2 * CW), row(CW)],
        out_shape=[_sds((NB, H * 128, B), BF16), _sds((NB, H * KR, B), BF16), _sds((H, S, 128), BF16),
                   _sds((NB, A, B), BF16), _sds((H, S, 128), BF16), _sds((2 * A, S)), _sds((8, S)),
                   _sds((S, 2 * CW)), _sds((S, CW))],
        scratch_shapes=[pltpu.VMEM((8, 1), F32), pltpu.VMEM((128, B), F32), pltpu.VMEM((6, 8, B), F32)],
        compiler_params=_cp(("arbitrary",)),
    )(x, n1g, sc1, sh1, wT, wr, qg_col, kg_col, bf_col, tri)


def _causal_keep(B):
    return lax.broadcasted_iota(jnp.int32, (B, B), 0) <= lax.broadcasted_iota(jnp.int32, (B, B), 1)


def attn_fwd(qTa, kaug, vT):
    NB, _, B = qTa.shape
    S = NB * B

    def body(q_ref, k_ref, v_ref, o_ref, lse_ref):
        i = pl.program_id(1)
        q = q_ref[0]

        def step(j, carry, masked):
            m, l, acc = carry
            kj = k_ref[0, pl.ds(pl.multiple_of(j * B, B), B), :]
            s = _nn(kj, q)
            if masked:
                s = jnp.where(_causal_keep(B), s, -jnp.inf)
            mn = jnp.maximum(m, jnp.max(s, axis=0, keepdims=True))
            a = jnp.exp(m - mn)
            p = jnp.exp(s - mn)
            l = a * l + _colsum(p)
            acc = a * acc + _nn(v_ref[j], p.astype(BF16))
            return mn, l, acc

        init = (jnp.full((1, B), -jnp.inf, F32), jnp.zeros((1, B), F32), jnp.zeros((DH, B), F32))
        carry = lax.fori_loop(0, i, lambda j, c: step(j, c, False), init)
        m, l, acc = step(i, carry, True)
        o_ref[0] = acc / l
        lse_ref[0, 0] = m + jnp.log(l)

    return _pc(
        body, name="attn_fwd", grid=(H, NB),
        in_specs=[pl.BlockSpec((1, 128, B), lambda h, i: (i, h, 0)),
                  pl.BlockSpec((1, S, 128), lambda h, i: (h, 0, 0)),
                  pl.BlockSpec((NB, DH, B), lambda h, i: (0, h, 0))],
        out_specs=[pl.BlockSpec((1, DH, B), lambda h, i: (i, h, 0)),
                   pl.BlockSpec((1, 1, 1, B), lambda h, i: (h, i, 0, 0))],
        out_shape=[_sds((NB, A, B)), _sds((H, NB, 1, B))],
        compiler_params=_cp(("arbitrary", "arbitrary")),
    )(qTa, kaug, vT)


def attn_bwd(qTa, kaug, kT, vaug, doTa, lse, delta):
    NB, _, B = qTa.shape
    QR = 80

    def body(q_ref, ka_ref, kt_ref, va_ref, do_ref, lse_ref, dl_ref, dq_ref, dk_ref, dv_ref, dfk_ref):
        j = pl.program_id(1)

        @pl.when(j == 0)
        def _():
            dq_ref[...] = jnp.zeros_like(dq_ref)

        ka = ka_ref[0]
        kt = kt_ref[0]
        va = va_ref[0]

        def step(i, carry, masked):
            dkacc, dvacc = carry
            qa = q_ref[i]
            doa = do_ref[i]
            s = _nn(ka, qa)
            p = jnp.exp(s - lse_ref[0, i])
            if masked:
                p = jnp.where(_causal_keep(B), p, 0.0)
            dp = _nn(va, doa)
            ds = p * (dp - dl_ref[0, i])
            pb = p.astype(BF16)
            dsb = ds.astype(BF16)
            dvacc = dvacc + _nt(doa[0:DH], pb)
            dkacc = dkacc + _nt(qa[0:QR], dsb)
            dq_ref[i] = dq_ref[i] + _nn(kt, dsb)
            return dkacc, dvacc

        init = (jnp.zeros((QR, B), F32), jnp.zeros((DH, B), F32))
        carry = step(j, init, True)
        dkacc, dvacc = lax.fori_loop(j + 1, NB, lambda i, c: step(i, c, False), carry)
        dk_ref[0] = dkacc[0:DH]
        dfk_ref[0, 0] = dkacc[DH:DH + 8]
        dv_ref[0] = dvacc

    per_kv = lambda r: pl.BlockSpec((1, r, B), lambda h, j: (j, h, 0))
    head_all = lambda r: pl.BlockSpec((NB, r, B), lambda h, j: (0, h, 0))
    aug = pl.BlockSpec((1, B, 128), lambda h, j: (h, j, 0))
    stat = pl.BlockSpec((1, NB, 1, B), lambda h, j: (h, 0, 0, 0))
    return _pc(
        body, name="attn_bwd", grid=(H, NB),
        in_specs=[head_all(128), aug, per_kv(KR), aug, head_all(128), stat, stat],
        out_specs=[head_all(KR), per_kv(DH), per_kv(DH),
                   pl.BlockSpec((1, 1, 8, B), lambda h, j: (h, j, 0, 0))],
        out_shape=[_sds((NB, H * KR, B)), _sds((NB, A, B)), _sds((NB, A, B)), _sds((H, NB, 8, B))],
        compiler_params=_cp(("arbitrary", "arbitrary")),
    )(qTa, kaug, kT, vaug, doTa, lse, delta)


def _conv_tail(u1, lng, lnb, beta_c):
    mu = _rowmean(u1)
    d = u1 - mu
    rstd = lax.rsqrt(_rowmean(d * d) + EPS)
    xhat = d * rstd
    u2 = xhat * lng + lnb
    sg = _sigmoid(u2)
    u3 = u2 * sg
    rc = lax.rsqrt(_rowmean(u3 * u3) + EPS)
    n3 = u3 * rc
    return rstd, xhat, u2, sg, rc, n3, n3 * beta_c


def _attn_tail(oT, beta_a_col):
    ra = lax.rsqrt(jnp.mean(oT * oT, axis=0, keepdims=True) + EPS)
    ohat = oT * ra
    return ra, ohat, ohat * beta_a_col


def conv_merge_out(u0, x, oT, conv_w, conv_b, lng, lnb, beta_c, beta_a_col, wo, g1):
    S = x.shape[0]
    B = BLK
    NB = S // B

    def body(uc_ref, up_ref, x_ref, oT_ref, w_ref, cb_ref, lng_ref, lnb_ref, bc_ref, ba_ref, wo_ref,
             g1_ref, x1_ref, o_ref, u1_ref, ubuf):
        i = pl.program_id(0)
        ubuf[0:HALO, :] = jnp.where(i > 0, up_ref[B - HALO:B, :], 0.0)
        ubuf[HALO:HALO + B, :] = uc_ref[...]
        acc = jnp.broadcast_to(cb_ref[...], (B, CW))
        for k in range(KC):
            acc = acc + w_ref[k:k + 1, :] * ubuf[pl.ds(HALO - (KC - 1) + k, B), :]
        u1_ref[...] = acc
        mc = _conv_tail(acc, lng_ref[...], lnb_ref[...], bc_ref[...])[-1]
        maT = _attn_tail(oT_ref[0], ba_ref[...])[-1]
        ma = maT.T
        o = _nn(ma.astype(BF16), wo_ref[0:A, :]) + _nn(mc.astype(BF16), wo_ref[A:D, :])
        o_ref[...] = o
        x1_ref[...] = x_ref[...] + g1_ref[...] * o

    row = lambda w: pl.BlockSpec((B, w), lambda i: (i, 0))
    return _pc(
        body, name="conv_merge_out", grid=(NB,),
        in_specs=[row(CW), pl.BlockSpec((B, CW), lambda i: (jnp.maximum(i - 1, 0), 0)), row(D),
                  pl.BlockSpec((1, A, B), lambda i: (i, 0, 0)), _const((32, CW)), _const((1, CW)),
                  _const((1, CW)), _const((1, CW)), _const((1, CW)), _const((A, 1)), _const((D, D)),
                  _const((1, D))],
        out_specs=[row(D), row(D), row(CW)],
        out_shape=[_sds((S, D)), _sds((S, D)), _sds((S, CW))],
        scratch_shapes=[pltpu.VMEM((B + HALO, CW), F32)],
        compiler_params=_cp(("arbitrary",)),
    )(u0, u0, x, oT, conv_w, conv_b, lng, lnb, beta_c, beta_a_col, wo, g1)


def mlp_fwd_loss(x1, tgt, n2g, sc2, sh2, g2, w1, w2):
    S = x1.shape[0]
    B = BLK
    NB = S // B
    NC = FF // FCH

    def body(x1_ref, t_ref, g_ref, sc_ref, sh_ref, g2_ref, w1_ref, w2_ref, dy_ref, loss_ref, dg2_ref,
             hs, acc):
        i = pl.program_id(0)
        c = pl.program_id(1)

        @pl.when(jnp.logical_and(i == 0, c == 0))
        def _():
            loss_ref[...] = jnp.zeros_like(loss_ref)
            dg2_ref[...] = jnp.zeros_like(dg2_ref)

        @pl.when(c == 0)
        def _():
            hs[...] = _norm_mod(x1_ref[...], g_ref[...], sc_ref[...], sh_ref[...])[2].astype(BF16)
            acc[...] = jnp.zeros_like(acc)

        a = jnp.maximum(_nn(hs[...], w1_ref[...]), 0.0)
        acc[...] = acc[...] + _nn((a * a).astype(BF16), w2_ref[...])

        @pl.when(c == NC - 1)
        def _():
            y2 = acc[...]
            e = x1_ref[...] + g2_ref[...] * y2 - t_ref[...]
            dy = e * (1.0 / D)
            dy_ref[...] = dy
            loss_ref[...] = loss_ref[...] + 0.5 * _colsum(_rowmean(e * e))
            dg2_ref[...] = dg2_ref[...] + _colsum(dy * y2)

    row = pl.BlockSpec((B, D), lambda i, c: (i, 0))
    vec = pl.BlockSpec((1, D), lambda i, c: (0, 0))
    return _pc(
        body, name="mlp_fwd_loss", grid=(NB, NC),
        in_specs=[row, row, vec, vec, vec, vec, pl.BlockSpec((D, FCH), lambda i, c: (0, c)),
                  pl.BlockSpec((FCH, D), lambda i, c: (c, 0))],
        out_specs=[row, pl.BlockSpec((1, 1), lambda i, c: (0, 0)), vec],
        out_shape=[_sds((S, D)), _sds((1, 1)), _sds((1, D))],
        scratch_shapes=[pltpu.VMEM((B, D), BF16), pltpu.VMEM((B, D), F32)],
        compiler_params=_cp(("arbitrary", "arbitrary")),
    )(x1, tgt, n2g, sc2, sh2, g2, w1, w2)


def mlp_bwd(x1, dy, n2g, sc2, sh2, g2, w1, w2):
    S = x1.shape[0]
    B = BLK
    NB = S // B
    NC = FF // FCH

    def body(x1_ref, dy_ref, g_ref, sc_ref, sh_ref, g2_ref, w1_ref, w2_ref, dw1_ref, dw2t_ref, dh_ref):
        i = pl.program_id(1)

        @pl.when(i == 0)
        def _():
            dw1_ref[...] = jnp.zeros_like(dw1_ref)
            dw2t_ref[...] = jnp.zeros_like(dw2t_ref)

        h2 = _norm_mod(x1_ref[...], g_ref[...], sc_ref[...], sh_ref[...])[2]
        w1c = w1_ref[...]
        w2c = w2_ref[...]
        ra = jnp.maximum(_nn(h2.astype(BF16), w1c), 0.0)
        dy2 = dy_ref[...] * g2_ref[...]
        db = _nt(dy2.astype(BF16), w2c)
        dab = (db * (2.0 * ra)).astype(BF16)
        dh_ref[0] = _nt(dab, w1c)
        dw1_ref[...] = dw1_ref[...] + _nn(h2.T.astype(BF16), dab)
        dw2t_ref[...] = dw2t_ref[...] + _nn(dy2.T.astype(BF16), (ra * ra).astype(BF16))

    row = pl.BlockSpec((B, D), lambda c, i: (i, 0))
    vec = pl.BlockSpec((1, D), lambda c, i: (0, 0))
    wcol = pl.BlockSpec((D, FCH), lambda c, i: (0, c))
    return _pc(
        body, name="mlp_bwd", grid=(NC, NB),
        in_specs=[row, row, vec, vec, vec, vec, wcol, pl.BlockSpec((FCH, D), lambda c, i: (c, 0))],
        out_specs=[wcol, wcol, pl.BlockSpec((1, B, D), lambda c, i: (c, i, 0))],
        out_shape=[_sds((D, FF)), _sds((D, FF)), _sds((NC, S, D))],
        compiler_params=_cp(("arbitrary", "arbitrary")),
    )(x1, dy, n2g, sc2, sh2, g2, w1, w2)


def merge_bwd(dh2p, x1, dy, o, oT, u1, n2g, sc2, g1, lng, lnb, beta_c, beta_a_col, wo):
    S = x1.shape[0]
    B = BLK
    NB = S // B
    NC = dh2p.shape[0]

    def body(dh_ref, x1_ref, dy_ref, o_ref, oT_ref, u1_ref, g_ref, sc_ref, g1_ref, lng_ref, lnb_ref,
             bc_ref, ba_ref, wo_ref,
             dx1_ref, doTa_ref, dl_ref, du1_ref, dwo_ref, v1_ref, v2_ref, dba_ref):
        i = pl.program_id(0)

        @pl.when(i == 0)
        def _():
            dwo_ref[...] = jnp.zeros_like(dwo_ref)
            v1_ref[...] = jnp.zeros_like(v1_ref)
            v2_ref[...] = jnp.zeros_like(v2_ref)
            dba_ref[...] = jnp.zeros_like(dba_ref)

        dh2 = dh_ref[0]
        for cc in range(1, NC):
            dh2 = dh2 + dh_ref[cc]
        x1 = x1_ref[...]
        g = g_ref[...]
        sc = sc_ref[...]
        r2 = lax.rsqrt(_rowmean(x1 * x1) + EPS)
        xh = x1 * r2
        dhx = dh2 * xh
        v1_ref[0:1, :] = v1_ref[0:1, :] + _colsum(dh2)
        v1_ref[1:2, :] = v1_ref[1:2, :] + _colsum(dhx) * g
        v1_ref[2:3, :] = v1_ref[2:3, :] + _colsum(dhx) * (1.0 + sc)
        dxh = dh2 * (g * (1.0 + sc))
        dx1 = dy_ref[...] + r2 * (dxh - xh * _rowmean(dxh * xh))
        dx1_ref[...] = dx1
        v1_ref[3:4, :] = v1_ref[3:4, :] + _colsum(dx1 * o_ref[...])
        dob = (dx1 * g1_ref[...]).astype(BF16)

        lng = lng_ref[...]
        bc = bc_ref[...]
        rstd, xhat, u2, sg, rc, n3, mc = _conv_tail(u1_ref[...], lng, lnb_ref[...], bc)
        ba = ba_ref[...]
        oT = oT_ref[0]
        ra, ohat, maT = _attn_tail(oT, ba)
        dwo_ref[0:A, :] = dwo_ref[0:A, :] + _nn(maT.astype(BF16), dob)
        dwo_ref[A:D, :] = dwo_ref[A:D, :] + _nn(mc.T.astype(BF16), dob)
        dmaT = _nt(wo_ref[0:A, :], dob)
        dmc = _nt(dob, wo_ref[A:D, :])

        dba_ref[...] = dba_ref[...] + _lanesum(dmaT * ohat)
        dohat = dmaT * ba
        doT = ra * (dohat - ohat * jnp.mean(dohat * ohat, axis=0, keepdims=True))
        prod = doT * oT
        zpad = jnp.zeros((128 - DH, B), BF16)
        for hh in range(H):
            sl = slice(hh * DH, (hh + 1) * DH)
            dl_ref[hh, 0] = _colsum(prod[sl])
            doTa_ref[0, hh * 128:hh * 128 + DH, :] = doT[sl].astype(BF16)
            doTa_ref[0, hh * 128 + DH:(hh + 1) * 128, :] = zpad

        v2_ref[0:1, :] = v2_ref[0:1, :] + _colsum(dmc * n3)
        dn3 = dmc * bc
        du3 = rc * (dn3 - n3 * _rowmean(dn3 * n3))
        du2 = du3 * (sg * (1.0 + u2 * (1.0 - sg)))
        v2_ref[1:2, :] = v2_ref[1:2, :] + _colsum(du2 * xhat)
        v2_ref[2:3, :] = v2_ref[2:3, :] + _colsum(du2)
        dxhat = du2 * lng
        du1_ref[...] = rstd * (dxhat - _rowmean(dxhat) - xhat * _rowmean(dxhat * xhat))

    row = lambda w: pl.BlockSpec((B, w), lambda i: (i, 0))
    return _pc(
        body, name="merge_bwd", grid=(NB,),
        in_specs=[pl.BlockSpec((NC, B, D), lambda i: (0, i, 0)), row(D), row(D), row(D),
                  pl.BlockSpec((1, A, B), lambda i: (i, 0, 0)), row(CW), _const((1, D)), _const((1, D)),
                  _const((1, D)), _const((1, CW)), _const((1, CW)), _const((1, CW)), _const((A, 1)),
                  _const((D, D))],
        out_specs=[row(D), pl.BlockSpec((1, H * 128, B), lambda i: (i, 0, 0)),
                   pl.BlockSpec((H, 1, 1, B), lambda i: (0, i, 0, 0)), row(CW), _const((D, D)),
                   _const((8, D)), _const((8, CW)), _const((A, 1))],
        out_shape=[_sds((S, D)), _sds((NB, H * 128, B), BF16), _sds((H, NB, 1, B)), _sds((S, CW)),
                   _sds((D, D)), _sds((8, D)), _sds((8, CW)), _sds((A, 1))],
        compiler_params=_cp(("arbitrary",)),
    )(dh2p, x1, dy, o, oT, u1, n2g, sc2, g1, lng, lnb, beta_c, beta_a_col, wo)


def head_bwd(du1, u0, alg, zqk, fgT, dqT, dkT, dvT, dfk, conv_w, qg_col, kg_col, bf_col, tri_lo):
    S = u0.shape[0]
    B = BLK
    NB = S // B

    def body(dc_ref, dn_ref, uc_ref, up_ref, alg_ref, zqk_ref, fg_ref, dq_ref, dk_ref, dv_ref, dfk_ref,
             w_ref, qg_ref, kg_ref, bf_ref, tri_ref,
             dzr_ref, dzT_ref, dcw_ref, vc_ref, dqg_ref, dkg_ref, dbf_ref,
             dbuf, ubuf, carry, fbuf):
        pid = pl.program_id(0)
        ri = NB - 1 - pid

        @pl.when(pid == 0)
        def _():
            carry[...] = jnp.zeros_like(carry)
            dcw_ref[...] = jnp.zeros_like(dcw_ref)
            vc_ref[...] = jnp.zeros_like(vc_ref)
            dqg_ref[...] = jnp.zeros_like(dqg_ref)
            dkg_ref[...] = jnp.zeros_like(dkg_ref)
            dbf_ref[...] = jnp.zeros_like(dbf_ref)

        du1 = dc_ref[...]
        dbuf[0:B, :] = du1
        dbuf[B:B + HALO, :] = jnp.where(ri < NB - 1, dn_ref[0:HALO, :], 0.0)
        ubuf[0:HALO, :] = jnp.where(ri > 0, up_ref[B - HALO:B, :], 0.0)
        ubuf[HALO:HALO + B, :] = uc_ref[...]
        du0 = jnp.zeros((B, CW), F32)
        for k in range(KC):
            du0 = du0 + w_ref[k:k + 1, :] * dbuf[pl.ds(KC - 1 - k, B), :]
            dcw_ref[k:k + 1, :] = dcw_ref[k:k + 1, :] + _colsum(
                du1 * ubuf[pl.ds(HALO - (KC - 1) + k, B), :])
        vc_ref[0:1, :] = vc_ref[0:1, :] + _colsum(du1)

        al = alg_ref[:, 0:CW]
        sg = _sigmoid(alg_ref[:, CW:2 * CW])
        dzr_ref[:, 0:CW] = (du0 * sg).astype(BF16)
        dzr_ref[:, CW:2 * CW] = (du0 * al * sg * (1.0 - sg)).astype(BF16)

        dqg = jnp.zeros((DH, B), F32)
        dkg = jnp.zeros((DH, B), F32)
        qg = qg_ref[...]
        kg = kg_ref[...]
        for hh in range(H):
            sl = slice(hh * DH, (hh + 1) * DH)
            q = zqk_ref[sl, :]
            rq = lax.rsqrt(jnp.mean(q * q, axis=0, keepdims=True) + EPS)
            qn = q * rq
            dqh = dq_ref[0, hh * KR:hh * KR + DH, :] * 0.125
            dqg = dqg + dqh * qn
            dqn = dqh * qg
            dzT_ref[0, sl, :] = (rq * (dqn - qn * jnp.mean(dqn * qn, axis=0, keepdims=True))).astype(BF16)
            k = zqk_ref[A + hh * DH:A + (hh + 1) * DH, :]
            rk = lax.rsqrt(jnp.mean(k * k, axis=0, keepdims=True) + EPS)
            kn = k * rk
            dkh = dk_ref[0, sl, :]
            dkg = dkg + dkh * kn
            dkn = dkh * kg
            dzT_ref[0, A + hh * DH:A + (hh + 1) * DH, :] = (
                rk * (dkn - kn * jnp.mean(dkn * kn, axis=0, keepdims=True))).astype(BF16)
            fbuf[hh:hh + 1, :] = dq_ref[0, hh * KR + DH:hh * KR + DH + 1, :] - dfk_ref[hh, 0, 0:1, :]
        dqg_ref[...] = dqg_ref[...] + _lanesum(dqg)
        dkg_ref[...] = dkg_ref[...] + _lanesum(dkg)
        dzT_ref[0, 2 * A:3 * A, :] = dv_ref[0].astype(BF16)

        dF = fbuf[...]
        a1, a2, a3 = _split3(dF)
        tr = tri_ref[...]
        dlogf = _nn(a1, tr) + _nn(a2, tr) + _nn(a3, tr) + carry[...]
        carry[...] = carry[...] + _lanesum(dF)
        dfg = dlogf * _sigmoid(-(fg_ref[...] + bf_ref[...]))
        dbf_ref[...] = dbf_ref[...] + _lanesum(dfg)
        dzT_ref[0, 3 * A:ZT_ROWS, :] = jnp.concatenate([dfg, jnp.zeros((8, B), F32)], axis=0).astype(BF16)

    rrow = lambda w: pl.BlockSpec((B, w), lambda p: (NB - 1 - p, 0))
    rts = lambda r: pl.BlockSpec((1, r, B), lambda p: (NB - 1 - p, 0, 0))
    return _pc(
        body, name="head_bwd", grid=(NB,),
        in_specs=[rrow(CW), pl.BlockSpec((B, CW), lambda p: (jnp.minimum(NB - p, NB - 1), 0)),
                  rrow(CW), pl.BlockSpec((B, CW), lambda p: (jnp.maximum(NB - 2 - p, 0), 0)),
                  rrow(2 * CW), pl.BlockSpec((2 * A, B), lambda p: (0, NB - 1 - p)),
                  pl.BlockSpec((8, B), lambda p: (0, NB - 1 - p)), rts(H * KR), rts(A), rts(A),
                  pl.BlockSpec((H, 1, 8, B), lambda p: (0, NB - 1 - p, 0, 0)),
                  _const((32, CW)), _const((DH, 1)), _const((DH, 1)), _const((8, 1)), _const((B, B))],
        out_specs=[rrow(2 * CW), rts(ZT_ROWS), _const((32, CW)), _const((8, CW)), _const((DH, 1)),
                   _const((DH, 1)), _const((8, 1))],
        out_shape=[_sds((S, 2 * CW), BF16), _sds((NB, ZT_ROWS, B), BF16), _sds((32, CW)), _sds((8, CW)),
                   _sds((DH, 1)), _sds((DH, 1)), _sds((8, 1))],
        scratch_shapes=[pltpu.VMEM((B + HALO, CW), F32), pltpu.VMEM((B + HALO, CW), F32),
                        pltpu.VMEM((8, 1), F32), pltpu.VMEM((8, B), F32)],
        compiler_params=_cp(("arbitrary",)),
    )(du1, du1, u0, u0, alg, zqk, fgT, dqT, dkT, dvT, dfk, conv_w, qg_col, kg_col, bf_col, tri_lo)


def in_bwd(x, dx1, dzr, dzT, n1g, sc1, sh1, wT, w_qkvf, wr):
    S = x.shape[0]
    B = BLK
    NB = S // B

    def body(x_ref, dx1_ref, dzr_ref, dzT_ref, g_ref, sc_ref, sh_ref, wT_hbm, w_hbm, wr_hbm,
             gx_ref, dwT_hbm, dwr_hbm, v_ref, wT_v, w_v, wr_v, dwT_acc, dwr_acc, sem):
        i = pl.program_id(0)

        @pl.when(i == 0)
        def _():
            for src, dst in ((wT_hbm, wT_v), (w_hbm, w_v), (wr_hbm, wr_v)):
                cp = pltpu.make_async_copy(src, dst, sem)
                cp.start()
                cp.wait()
            dwT_acc[...] = jnp.zeros_like(dwT_acc)
            dwr_acc[...] = jnp.zeros_like(dwr_acc)
            v_ref[...] = jnp.zeros_like(v_ref)

        g = g_ref[...]
        sc = sc_ref[...]
        r1, xh, h1 = _norm_mod(x_ref[...], g, sc, sh_ref[...])
        hb = h1.astype(BF16)
        dzr = dzr_ref[...]
        dzT = dzT_ref[0]
        dwT_acc[...] = dwT_acc[...] + _nn(dzT, hb)
        dwr_acc[...] = dwr_acc[...] + _nn(h1.T.astype(BF16), dzr)
        dh1 = _nt(dzr, wr_v[...]) + _nn(w_v[...], dzT).T
        dhx = dh1 * xh
        v_ref[0:1, :] = v_ref[0:1, :] + _colsum(dh1)
        v_ref[1:2, :] = v_ref[1:2, :] + _colsum(dhx) * g
        v_ref[2:3, :] = v_ref[2:3, :] + _colsum(dhx) * (1.0 + sc)
        dxh = dh1 * (g * (1.0 + sc))
        gx_ref[...] = dx1_ref[...] + r1 * (dxh - xh * _rowmean(dxh * xh))

        @pl.when(i == NB - 1)
        def _():
            for src, dst in ((dwT_acc, dwT_hbm), (dwr_acc, dwr_hbm)):
                cp = pltpu.make_async_copy(src, dst, sem)
                cp.start()
                cp.wait()

    row = lambda w: pl.BlockSpec((B, w), lambda i: (i, 0))
    return _pc(
        body, name="in_bwd", grid=(NB,),
        in_specs=[row(D), row(D), row(2 * CW), pl.BlockSpec((1, ZT_ROWS, B), lambda i: (i, 0, 0)),
                  _const((1, D)), _const((1, D)), _const((1, D)), ANY, ANY, ANY],
        out_specs=[row(D), ANY, ANY, _const((8, D))],
        out_shape=[_sds((S, D)), _sds((ZT_ROWS, D)), _sds((D, 2 * CW)), _sds((8, D))],
        scratch_shapes=[pltpu.VMEM((ZT_ROWS, D), BF16), pltpu.VMEM((D, ZT_ROWS), BF16),
                        pltpu.VMEM((D, 2 * CW), BF16), pltpu.VMEM((ZT_ROWS, D), F32),
                        pltpu.VMEM((D, 2 * CW), F32), pltpu.SemaphoreType.DMA(())],
        compiler_params=_cp(("arbitrary",)),
    )(x, dx1, dzr, dzT, n1g, sc1, sh1, wT, w_qkvf, wr)


def _adam_math(g, w, m, v):
    m = ADAM_B1 * m + (1.0 - ADAM_B1) * g
    v = ADAM_B2 * v + (1.0 - ADAM_B2) * (g * g)
    m_hat = m / (1.0 - ADAM_B1 ** ADAM_STEP)
    v_hat = v / (1.0 - ADAM_B2 ** ADAM_STEP)
    delta = -ADAM_LR * (m_hat / (jnp.sqrt(v_hat) + ADAM_EPS) + ADAM_WD * w)
    return delta, m, v


def _row_tile(R):
    for t in (1024, 512, 256, 128, 64, 32, 16, 8):
        if R % t == 0:
            return t
    return R


def sum_slots(parts, name):
    K, R, C = parts.shape
    T = _row_tile(R)

    def body(p_ref, o_ref):
        s = p_ref[0]
        for k in range(1, K):
            s = s + p_ref[k]
        o_ref[...] = s

    return _pc(body, name=name, grid=(R // T,),
               in_specs=[pl.BlockSpec((K, T, C), lambda i: (0, i, 0))],
               out_specs=pl.BlockSpec((T, C), lambda i: (i, 0)), out_shape=_sds((R, C)),
               compiler_params=_cp(("arbitrary",)))(parts)


def adamw_slots(parts, w, m, v, name):
    K, R, C = parts.shape
    T = _row_tile(R)

    def body(p_ref, w_ref, m_ref, v_ref, g_ref, d_ref, nm_ref, nv_ref):
        g = p_ref[0]
        for k in range(1, K):
            g = g + p_ref[k]
        g_ref[...] = g
        d_ref[...], nm_ref[...], nv_ref[...] = _adam_math(g, w_ref[...], m_ref[...], v_ref[...])

    t2 = pl.BlockSpec((T, C), lambda i: (i, 0))
    return _pc(body, name=name, grid=(R // T,),
               in_specs=[pl.BlockSpec((K, T, C), lambda i: (0, i, 0)), t2, t2, t2],
               out_specs=[t2, t2, t2, t2], out_shape=[_sds((R, C))] * 4,
               compiler_params=_cp(("arbitrary",)))(parts, w, m, v)


def ada_grad_adamw(cT, dmod_cols, w, m, v):
    NCOL = w.shape[1]

    def body(cT_ref, dm_ref, w_ref, m_ref, v_ref, g_ref, d_ref, nm_ref, nv_ref):
        def term(b):
            cv = cT_ref[b]
            return (cv * _sigmoid(cv)) * dm_ref[b:b + 1, :]

        g = term(0)
        for b in range(1, N_DEV):
            g = g + term(b)
        g_ref[...] = g
        d_ref[...], nm_ref[...], nv_ref[...] = _adam_math(g, w_ref[...], m_ref[...], v_ref[...])

    full = _const((D, NCOL))
    return _pc(body, name="ada_grad_adamw", grid=(1,),
               in_specs=[_const((N_DEV, D, 1)), _const((N_DEV, NCOL)), full, full, full],
               out_specs=[full, full, full, full], out_shape=[_sds((D, NCOL))] * 4,
               compiler_params=_cp())(cT, dmod_cols, w, m, v)


def _pack_rows(vecs, rows=None):
    flat = jnp.concatenate([jnp.ravel(v) for v in vecs])
    n = flat.shape[0]
    r = -(-n // 1024) * 8 if rows is None else rows
    return jnp.pad(flat, (0, r * 128 - n)).reshape(r, 128)


def _unpack_rows(packed, shapes):
    flat = packed.reshape(-1)
    out, off = [], 0
    for s in shapes:
        n = 1
        for d in s:
            n *= d
        out.append(flat[off:off + n].reshape(s))
        off += n
    return out


def _cols_from_shards(g, rows, cols):
    return g.reshape(N_DEV, rows, cols).transpose(1, 0, 2).reshape(rows, N_DEV * cols)


def _cols_to_shards(w, cols):
    rows = w.shape[0]
    return w.reshape(rows, N_DEV, cols).transpose(1, 0, 2).reshape(N_DEV, rows * cols)


def kernel(x, c, w_ada, b_ada, norm1_g, w_in, q_norm_g, k_norm_g, b_f, conv_w, conv_b, conv_ln_g, conv_ln_b, beta_attn, beta_conv, w_out, norm2_g, w_ff1, w_ff2, loss_target, m_w_ada, m_b_ada, m_norm1_g, m_w_in, m_q_norm_g, m_k_norm_g, m_b_f, m_conv_w, m_conv_b, m_conv_ln_g, m_conv_ln_b, m_beta_attn, m_beta_conv, m_w_out, m_norm2_g, m_w_ff1, m_w_ff2, v_w_ada, v_b_ada, v_norm1_g, v_w_in, v_q_norm_g, v_k_norm_g, v_b_f, v_conv_w, v_conv_b, v_conv_ln_g, v_conv_ln_b, v_beta_attn, v_beta_conv, v_w_out, v_norm2_g, v_w_ff1, v_w_ff2):
    S = x.shape[1]
    B = BLK
    NB = S // B
    me = 4 * lax.axis_index("x") + 2 * lax.axis_index("y") + lax.axis_index("c")
    xs = x[0]
    tgt = loss_target[0]

    ADA_C, IN_C, FF_C, CV_C = w_ada.shape[2], w_in.shape[2], w_ff1.shape[2], conv_w.shape[2]
    OUT_R, FF_R = w_out.shape[1], w_ff2.shape[1]
    IN_W = N_DEV * IN_C

    n_ada, n_in, n_out, n_f1, n_f2 = D * ADA_C, D * IN_C, OUT_R * D, D * FF_C, FF_R * D
    big = jnp.concatenate([w_ada.reshape(-1), w_in.reshape(-1), w_out.reshape(-1), w_ff1.reshape(-1),
                           w_ff2.reshape(-1)]).astype(BF16)
    rows_big = -(-big.shape[0] // (128 * 16)) * 16
    big = jnp.pad(big, (0, rows_big * 128 - big.shape[0])).reshape(rows_big, 128)
    gw = all_gather(big, "ag_weights").reshape(N_DEV, rows_big * 128)
    small0 = _pack_rows([c, conv_w], rows=24)
    gs0 = all_gather(small0, "ag_small_fwd").reshape(N_DEV, 24 * 128)

    off = 0
    w_ada_full = _cols_from_shards(gw[:, off:off + n_ada], D, ADA_C); off += n_ada
    w_in_full = _cols_from_shards(gw[:, off:off + n_in], D, IN_C); off += n_in
    wo_full = gw[:, off:off + n_out].reshape(D, D); off += n_out
    w1_full = _cols_from_shards(gw[:, off:off + n_f1], D, FF_C); off += n_f1
    w2_full = gw[:, off:off + n_f2].reshape(FF, D)
    c_all = gs0[:, 0:D]
    conv_w_full = _cols_from_shards(gs0[:, D:D + KC * CV_C], KC, CV_C)
    conv_w_pad = jnp.pad(conv_w_full, ((0, 32 - KC), (0, 0)))

    w_qkvf = jnp.pad(w_in_full[:, 0:3 * A + H], ((0, 0), (0, ZT_ROWS - 3 * A - H)))
    wT = w_qkvf.T
    wr = w_in_full[:, 3 * A + H:]

    qg_col = q_norm_g.reshape(DH, 1)
    kg_col = k_norm_g.reshape(DH, 1)
    bf_col = b_f.reshape(H, 1)
    beta_a_col = beta_attn.reshape(A, 1)
    ii = lax.broadcasted_iota(jnp.int32, (B, B), 0)
    jj = lax.broadcasted_iota(jnp.int32, (B, B), 1)
    tri_up = (ii <= jj).astype(BF16)
    tri_lo = (ii >= jj).astype(BF16)

    mod = mod_kernel(c, w_ada_full, b_ada)
    sh1, sc1, g1, sh2, sc2, g2 = [mod[:, k * D:(k + 1) * D] for k in range(6)]
    qTa, kT, kaug, vT, vaug, zqk, fgT, alg, u0 = fwd_in(xs, norm1_g, sc1, sh1, wT, wr, qg_col, kg_col,
                                                       bf_col, tri_up)
    oT, lse = attn_fwd(qTa, kaug, vT)
    x1, o, u1 = conv_merge_out(u0, xs, oT, conv_w_pad, conv_b, conv_ln_g, conv_ln_b, beta_conv,
                               beta_a_col, wo_full, g1)
    dy, loss_part, dg2 = mlp_fwd_loss(x1, tgt, norm2_g, sc2, sh2, g2, w1_full, w2_full)

    dw1, dw2t, dh2p = mlp_bwd(x1, dy, norm2_g, sc2, sh2, g2, w1_full, w2_full)
    dx1, doTa, delta, du1, dwo, v1, v2, dba = merge_bwd(dh2p, x1, dy, o, oT, u1, norm2_g, sc2, g1,
                                                         conv_ln_g, conv_ln_b, beta_conv, beta_a_col, wo_full)
    dqT, dkT, dvT, dfk = attn_bwd(qTa, kaug, kT, vaug, doTa, lse, delta)
    dzr, dzT, dcw, vc, dqg, dkg, dbf = head_bwd(du1, u0, alg, zqk, fgT, dqT, dkT, dvT, dfk, conv_w_pad,
                                                qg_col, kg_col, bf_col, tri_lo)
    grad_x, dwT, dwr, v0 = in_bwd(xs, dx1, dzr, dzT, norm1_g, sc1, sh1, wT, w_qkvf, wr)

    dw_in = jnp.concatenate([dwT.T[:, 0:3 * A + H], dwr], axis=1)
    gbig = jnp.concatenate([_cols_to_shards(dw_in, IN_C), dwo.reshape(N_DEV, n_out),
                            _cols_to_shards(dw1, FF_C), dw2t.T.reshape(N_DEV, n_f2)], axis=1)
    n_g = gbig.shape[1]
    rows_g = -(-n_g // (128 * 512)) * 512
    gbig = jnp.pad(gbig, ((0, 0), (0, rows_g * 128 - n_g)))
    recv =reduce_scatter_exchange(gbig.reshape(N_DEV, rows_g, 128), "rs_grads")

    dmod = jnp.concatenate([v0[0], v0[1], v1[3], v1[0], v1[1], dg2[0]])
    small1 = _pack_rows([dmod, v0[2], v1[2], dcw, vc[0], v2[1], v2[2], v2[0], dba, dqg, dkg,
                         jnp.pad(dbf.reshape(-1), (0, 120)), jnp.pad(loss_part.reshape(-1), (0, 127))])
    gs1 = all_gather(small1, "ag_small_bwd")
    tot = sum_slots(gs1, "sum_small")
    (g_b_ada, g_n1, g_n2, g_cw, g_cb, g_lng, g_lnb, g_bc, g_ba, g_qg, g_kg, g_bf, loss_v) = _unpack_rows(
        tot, [(1, 6 * D), (1, D), (1, D), (32, CW), (1, CW), (1, CW), (1, CW), (1, CW), (1, A),
              (1, DH), (1, DH), (1, 128), (1, 128)])
    loss = loss_v[0, 0]
    g_bf = g_bf[:, 0:H]
    g_cw_mine = lax.dynamic_slice(g_cw[0:KC], (0, me * CV_C), (KC, CV_C)).reshape(1, KC, CV_C)

    small_names = [(b_ada, m_b_ada, v_b_ada, g_b_ada), (norm1_g, m_norm1_g, v_norm1_g, g_n1),
                   (q_norm_g, m_q_norm_g, v_q_norm_g, g_qg), (k_norm_g, m_k_norm_g, v_k_norm_g, g_kg),
                   (b_f, m_b_f, v_b_f, g_bf), (conv_w, m_conv_w, v_conv_w, g_cw_mine),
                   (conv_b, m_conv_b, v_conv_b, g_cb), (conv_ln_g, m_conv_ln_g, v_conv_ln_g, g_lng),
                   (conv_ln_b, m_conv_ln_b, v_conv_ln_b, g_lnb), (beta_attn, m_beta_attn, v_beta_attn, g_ba),
                   (beta_conv, m_beta_conv, v_beta_conv, g_bc), (norm2_g, m_norm2_g, v_norm2_g, g_n2)]
    shapes_s = [t[0].shape for t in small_names]
    pw, pm, pv, pg = [_pack_rows([t[k] for t in small_names]) for k in range(4)]
    sg_, sd_, sm_, sv_ = adamw_slots(pg[None], pw, pm, pv, "adamw_small")
    sgs, sds, sms, svs = [_unpack_rows(a, shapes_s) for a in (sg_, sd_, sm_, sv_)]

    dmod_all = gs1[:, 0:48, :].reshape(N_DEV, N_DEV, ADA_C)
    dmod_cols = lax.dynamic_index_in_dim(dmod_all, me, axis=1, keepdims=False)
    ga, da, ma_, va_ = ada_grad_adamw(c_all.reshape(N_DEV, D, 1), dmod_cols, w_ada[0], m_w_ada[0], v_w_ada[0])

    big_shapes = [(1, D, IN_C), (1, OUT_R, D), (1, D, FF_C), (1, FF_R, D)]
    pack_big = lambda ws: _pack_rows(ws, rows=rows_g)
    bw = pack_big([w_in, w_out, w_ff1, w_ff2])
    bm = pack_big([m_w_in, m_w_out, m_w_ff1, m_w_ff2])
    bv = pack_big([v_w_in, v_w_out, v_w_ff1, v_w_ff2])
    bg_, bd_, bm_, bv_ = adamw_slots(recv, bw, bm, bv, "adamw_big")
    bgs, bds, bms, bvs = [_unpack_rows(a, big_shapes) for a in (bg_, bd_, bm_, bv_)]

    def assemble(small, ada, bigs):
        (b_ada_, n1_, qg_, kg_, bf_, cw_, cb_, lng_, lnb_, ba_, bc_, n2_) = small
        return [ada.reshape(1, D, ADA_C), b_ada_, n1_, bigs[0], qg_, kg_, bf_, cw_, cb_, lng_, lnb_, ba_, bc_,
                bigs[1], n2_, bigs[2], bigs[3]]

    return (loss, grad_x.reshape(1, S, D), *assemble(sgs, ga, bgs), *assemble(sds, da, bds),
            *assemble(sms, ma_, bms), *assemble(svs, va_, bvs))
```

```python
import functools

import jax
import jax.numpy as jnp
from jax import lax
from jax.experimental import pallas as pl
from jax.experimental.pallas import tpu as pltpu

F32 = jnp.float32
BF16 = jnp.bfloat16

D = 1024
A = 512
CW = 512
H = 8
DH = 64
FF = 4096
KC = 31
HALO = 32
KR = 80
ZT_ROWS = 1552
FCH = 1024
EPS = 1e-6
BLK = 512
N_DEV = 8
VMEM_LIMIT = 56 * 1024 * 1024

ADAM_LR = 0.001
ADAM_B1 = 0.9
ADAM_B2 = 0.999
ADAM_EPS = 1e-08
ADAM_WD = 0.01
ADAM_STEP = 10

MESH = pl.DeviceIdType.MESH
ANY = pl.BlockSpec(memory_space=pl.ANY)


def _pc(body, **kw):
    return pl.pallas_call(body, **kw)


def _cp(sem=None):
    return pltpu.CompilerParams(dimension_semantics=sem, vmem_limit_bytes=VMEM_LIMIT)


def _sds(shape, dtype=F32):
    return jax.ShapeDtypeStruct(shape, dtype)


def _const(shape):
    n = len(shape)
    return pl.BlockSpec(shape, lambda *a: (0,) * n)


def _nt(a, b):
    return lax.dot_general(a, b, (((1,), (1,)), ((), ())), preferred_element_type=F32)


def _nn(a, b):
    return jnp.dot(a, b, preferred_element_type=F32)


def _sigmoid(v):
    return 1.0 / (1.0 + jnp.exp(-v))


def _split3(v):
    a1 = v.astype(BF16)
    r1 = v - a1.astype(F32)
    a2 = r1.astype(BF16)
    a3 = (r1 - a2.astype(F32)).astype(BF16)
    return a1, a2, a3


def _rowmean(v):
    return jnp.mean(v, axis=-1, keepdims=True)


def _colsum(v):
    return jnp.sum(v, axis=0, keepdims=True)


def _lanesum(v):
    return jnp.sum(v, axis=-1, keepdims=True)


def _coords():
    return lax.axis_index("x"), lax.axis_index("y"), lax.axis_index("c")


def all_gather(x, name):
    R, C = x.shape

    def body(x_ref, out_ref, send_sems, recv_sems, local_sem):
        mx, my, mc = _coords()
        me, sibling = (mx, my, mc), (mx, my, 1 - mc)
        chips = [(1 - mx, my), (mx, 1 - my), (1 - mx, 1 - my)]

        def rows(px, py, pc):
            return out_ref.at[4 * px + 2 * py + pc]

        def copy(k, block, to, src=None):
            return pltpu.make_async_remote_copy(
                src_ref=rows(*block) if src is None else src, dst_ref=rows(*block),
                send_sem=send_sems.at[k], recv_sem=recv_sems.at[k],
                device_id=to, device_id_type=MESH)

        mine = pltpu.make_async_copy(x_ref, rows(*me), local_sem)
        mine.start()
        first = [copy(0, me, sibling, src=x_ref)]
        first += [copy(1 + j, me, (*chip, mc), src=x_ref) for j, chip in enumerate(chips)]
        for cp in first:
            cp.start()
        passed = [copy(4 + j, (*chip, mc), sibling) for j, chip in enumerate(chips)]
        for j, chip in enumerate(chips):
            copy(1 + j, (*chip, mc), me).wait_recv()
            passed[j].start()
        copy(0, sibling, me).wait_recv()
        for j, chip in enumerate(chips):
            copy(4 + j, (*chip, 1 - mc), me).wait_recv()
        for cp in first + passed:
            cp.wait_send()
        mine.wait()

    return _pc(
        body, name=name, out_shape=_sds((N_DEV, R, C), x.dtype),
        in_specs=[ANY], out_specs=ANY,
        scratch_shapes=[pltpu.SemaphoreType.DMA((7,)), pltpu.SemaphoreType.DMA((7,)),
                        pltpu.SemaphoreType.DMA(())],
    )(x)


def reduce_scatter_exchange(g, name):
    _, R, C = g.shape

    def body(g_ref, recv_ref, send_sems, recv_sems, local_sem):
        mx, my, mc = _coords()
        me = 4 * mx + 2 * my + mc
        local = pltpu.make_async_copy(g_ref.at[me], recv_ref.at[0], local_sem)
        local.start()
        copies = []
        for r in range(1, N_DEV):
            px = 1 - mx if r & 4 else mx
            py = 1 - my if r & 2 else my
            pcc = 1 - mc if r & 1 else mc
            cp = pltpu.make_async_remote_copy(
                src_ref=g_ref.at[4 * px + 2 * py + pcc], dst_ref=recv_ref.at[r],
                send_sem=send_sems.at[r - 1], recv_sem=recv_sems.at[r - 1],
                device_id=(px, py, pcc), device_id_type=MESH)
            cp.start()
            copies.append(cp)
        for cp in copies:
            cp.wait()
        local.wait()

    return _pc(
        body, name=name, out_shape=_sds(g.shape, g.dtype),
        in_specs=[ANY], out_specs=ANY,
        scratch_shapes=[pltpu.SemaphoreType.DMA((7,)), pltpu.SemaphoreType.DMA((7,)),
                        pltpu.SemaphoreType.DMA(())],
    )(g)


def mod_kernel(c, w_ada_b, b_ada):
    def body(c_ref, w_ref, b_ref, o_ref):
        cv = c_ref[...]
        sc = jnp.broadcast_to(cv * _sigmoid(cv), (8, D)).astype(BF16)
        o_ref[...] = _nn(sc, w_ref[...])[0:1, :] + b_ref[...]

    return _pc(body, name="mod", out_shape=_sds((1, 6 * D)),
               in_specs=[_const((1, D)), _const((D, 6 * D)), _const((1, 6 * D))],
               out_specs=_const((1, 6 * D)), grid=(1,), compiler_params=_cp())(c, w_ada_b, b_ada)


def _norm_mod(xv, g, sc, sh):
    r = lax.rsqrt(_rowmean(xv * xv) + EPS)
    xh = xv * r
    return r, xh, xh * (g * (1.0 + sc)) + sh


def _log_sigmoid(v):
    e = jnp.exp(-jnp.abs(v))
    l1p = jnp.where(e < 1e-4, e * (1.0 - 0.5 * e), jnp.log(1.0 + e))
    return jnp.minimum(v, 0.0) - l1p


def fwd_in(x, n1g, sc1, sh1, wT, wr, qg_col, kg_col, bf_col, tri):
    S = x.shape[0]
    B = BLK
    NB = S // B

    def body(x_ref, g_ref, sc_ref, sh_ref, wT_ref, wr_ref, qg_ref, kg_ref, bf_ref, tri_ref,
             qTa_ref, kT_ref, kaug_ref, vT_ref, vaug_ref, zqk_ref, fgT_ref, alg_ref, u0_ref,
             carry, tbuf, fs):
        i = pl.program_id(0)

        @pl.when(i == 0)
        def _():
            carry[...] = jnp.zeros_like(carry)

        _, _, h = _norm_mod(x_ref[...], g_ref[...], sc_ref[...], sh_ref[...])
        hb = h.astype(BF16)
        zT = _nt(wT_ref[...], hb)
        zr = _nn(hb, wr_ref[...])
        zqk_ref[...] = zT[0:2 * A]
        fgT = zT[3 * A:3 * A + 8]
        fgT_ref[...] = fgT
        alg_ref[...] = zr
        u0_ref[...] = zr[:, 0:CW] * _sigmoid(zr[:, CW:2 * CW])

        logf = _log_sigmoid(fgT + bf_ref[...])
        a1, a2, a3 = _split3(logf)
        tr = tri_ref[...]
        F = _nn(a1, tr) + _nn(a2, tr) + _nn(a3, tr) + carry[...]
        carry[...] = carry[...] + _lanesum(logf)
        p1, p2, p3 = _split3(F)
        n1, n2, n3 = _split3(-F)
        for k, v in enumerate((p1, p2, p3, n1, n2, n3)):
            fs[k] = v.astype(F32)

        rowi = lax.broadcasted_iota(jnp.int32, (8, B), 0)
        zeros_tail = jnp.zeros((128 - DH - 8, B), F32)
        ones_row = jnp.where(lax.broadcasted_iota(jnp.int32, (KR - DH, B), 0) == 0, 1.0, 0.0).astype(BF16)
        for hh in range(H):
            sl = slice(hh * DH, (hh + 1) * DH)
            q = zT[sl]
            k = zT[A + hh * DH:A + (hh + 1) * DH]
            v = zT[2 * A + hh * DH:2 * A + (hh + 1) * DH]
            qh = q * lax.rsqrt(jnp.mean(q * q, axis=0, keepdims=True) + EPS) * qg_ref[...] * 0.125
            kh = k * lax.rsqrt(jnp.mean(k * k, axis=0, keepdims=True) + EPS) * kg_ref[...]
            frow = [fs[kk, hh:hh + 1, :] for kk in range(6)]
            qx = jnp.where(rowi < 3, 1.0, jnp.where(rowi == 3, frow[0], jnp.where(
                rowi == 4, frow[1], jnp.where(rowi == 5, frow[2], 0.0))))
            kx = jnp.where(rowi == 0, frow[3], jnp.where(rowi == 1, frow[4], jnp.where(
                rowi == 2, frow[5], jnp.where(rowi < 6, 1.0, 0.0))))
            tbuf[0:DH, :] = qh
            tbuf[DH:DH + 8, :] = qx
            tbuf[DH + 8:128, :] = zeros_tail
            qTa_ref[0, hh * 128:(hh + 1) * 128, :] = tbuf[...].astype(BF16)
            tbuf[0:DH, :] = kh
            tbuf[DH:DH + 8, :] = kx
            kaug_ref[hh] = tbuf[...].T.astype(BF16)
            kT_ref[0, hh * KR:hh * KR + DH, :] = kh.astype(BF16)
            kT_ref[0, hh * KR + DH:(hh + 1) * KR, :] = ones_row
            tbuf[0:DH, :] = v
            tbuf[DH:DH + 8, :] = jnp.zeros((8, B), F32)
            vaug_ref[hh] = tbuf[...].T.astype(BF16)
            vT_ref[0, sl, :] = v.astype(BF16)

    row = lambda w: pl.BlockSpec((B, w), lambda i: (i, 0))
    tsp = lambda r: pl.BlockSpec((1, r, B), lambda i: (i, 0, 0))
    return _pc(
        body, name="fwd_in", grid=(NB,),
        in_specs=[row(D), _const((1, D)), _const((1, D)), _const((1, D)), _const((ZT_ROWS, D)),
                  _const((D, 2 * CW)), _const((DH, 1)), _const((DH, 1)), _const((8, 1)), _const((B, B))],
        out_specs=[tsp(H * 128), tsp(H * KR), pl.BlockSpec((H, B, 128), lambda i: (0, i, 0)), tsp(A),
                   pl.BlockSpec((H, B, 128), lambda i: (0, i, 0)),
                   pl.BlockSpec((2 * A, B), lambda i: (0, i)), pl.BlockSpec((8, B), lambda i: (0, i)),
                   row(2 * CW), row(CW)],
        out_shape=[_sds((NB, H * 128, B), BF16), _sds((NB, H * KR, B), BF16), _sds((H, S, 128), BF16),
                   _sds((NB, A, B), BF16), _sds((H, S, 128), BF16), _sds((2 * A, S)), _sds((8, S)),
                   _sds((S, 2 * CW)), _sds((S, CW))],
        scratch_shapes=[pltpu.VMEM((8, 1), F32), pltpu.VMEM((128, B), F32), pltpu.VMEM((6, 8, B), F32)],
        compiler_params=_cp(("arbitrary",)),
    )(x, n1g, sc1, sh1, wT, wr, qg_col, kg_col, bf_col, tri)


def _causal_keep(B):
    return lax.broadcasted_iota(jnp.int32, (B, B), 0) <= lax.broadcasted_iota(jnp.int32, (B, B), 1)


def attn_fwd(qTa, kaug, vT):
    NB, _, B = qTa.shape
    S = NB * B

    def body(q_ref, k_ref, v_ref, o_ref, lse_ref, s0, s1, m_ref, l_ref, acc_ref):
        i = pl.program_id(1)

        def scores(j, s_ref):
            s_ref[...] = _nn(k_ref[0, pl.ds(pl.multiple_of(j * B, B), B), :], q_ref[0])

        def softmax_step(s_ref, j, masked):
            s = s_ref[...]
            if masked:
                s = jnp.where(_causal_keep(B), s, -jnp.inf)
            m = m_ref[...]
            mn = jnp.maximum(m, jnp.max(s, axis=0, keepdims=True))
            a = jnp.exp(m - mn)
            p = jnp.exp(s - mn)
            m_ref[...] = mn
            l_ref[...] = a * l_ref[...] + _colsum(p)
            acc_ref[...] = a * acc_ref[...] + _nn(v_ref[j], p.astype(BF16))

        m_ref[...] = jnp.full((1, B), -jnp.inf, F32)
        l_ref[...] = jnp.zeros((1, B), F32)
        acc_ref[...] = jnp.zeros((DH, B), F32)
        scores(0, s0)

        def pair(t, carry):
            j = 2 * t
            scores(j + 1, s1)
            softmax_step(s0, j, False)
            scores(j + 2, s0)
            softmax_step(s1, j + 1, False)
            return carry

        lax.fori_loop(0, i // 2, pair, 0)

        @pl.when(i % 2 == 1)
        def _():
            scores(i, s1)
            softmax_step(s0, i - 1, False)
            softmax_step(s1, i, True)

        @pl.when(i % 2 == 0)
        def _():
            softmax_step(s0, i, True)

        l = l_ref[...]
        o_ref[0] = acc_ref[...] / l
        lse_ref[0, 0] = m_ref[...] + jnp.log(l)

    return _pc(
        body, name="attn_fwd", grid=(H, NB),
        in_specs=[pl.BlockSpec((1, 128, B), lambda h, i: (i, h, 0)),
                  pl.BlockSpec((1, S, 128), lambda h, i: (h, 0, 0)),
                  pl.BlockSpec((NB, DH, B), lambda h, i: (0, h, 0))],
        out_specs=[pl.BlockSpec((1, DH, B), lambda h, i: (i, h, 0)),
                   pl.BlockSpec((1, 1, 1, B), lambda h, i: (h, i, 0, 0))],
        out_shape=[_sds((NB, A, B)), _sds((H, NB, 1, B))],
        scratch_shapes=[pltpu.VMEM((B, B), F32), pltpu.VMEM((B, B), F32), pltpu.VMEM((1, B), F32),
                        pltpu.VMEM((1, B), F32), pltpu.VMEM((DH, B), F32)],
        compiler_params=_cp(("arbitrary", "arbitrary")),
    )(qTa, kaug, vT)


def attn_bwd(qTa, kaug, kT, vaug, doTa, lse, delta):
    NB, _, B = qTa.shape
    QR = 80

    def body(q_ref, ka_ref, kt_ref, va_ref, do_ref, lse_ref, dl_ref, dq_ref, dk_ref, dv_ref, dfk_ref,
             s0, d0, s1, d1, dk_acc, dv_acc):
        j = pl.program_id(1)

        @pl.when(j == 0)
        def _():
            dq_ref[...] = jnp.zeros_like(dq_ref)

        dk_acc[...] = jnp.zeros_like(dk_acc)
        dv_acc[...] = jnp.zeros_like(dv_acc)

        def products(i, s_ref, d_ref):
            s_ref[...] = _nn(ka_ref[0], q_ref[i])
            d_ref[...] = _nn(va_ref[0], do_ref[i])

        def grad_step(s_ref, d_ref, i, masked):
            p = jnp.exp(s_ref[...] - lse_ref[0, i])
            if masked:
                p = jnp.where(_causal_keep(B), p, 0.0)
            ds = p * (d_ref[...] - dl_ref[0, i])
            pb = p.astype(BF16)
            dsb = ds.astype(BF16)
            dv_acc[...] = dv_acc[...] + _nt(do_ref[i][0:DH], pb)
            dk_acc[...] = dk_acc[...] + _nt(q_ref[i][0:QR], dsb)
            dq_ref[i] = dq_ref[i] + _nn(kt_ref[0], dsb)

        n_rest = NB - 1 - j
        products(j, s1, d1)
        products(jnp.minimum(j + 1, NB - 1), s0, d0)
        grad_step(s1, d1, j, True)

        def pair(t, carry):
            i = j + 1 + 2 * t
            products(i + 1, s1, d1)
            grad_step(s0, d0, i, False)
            products(jnp.minimum(i + 2, NB - 1), s0, d0)
            grad_step(s1, d1, i + 1, False)
            return carry

        lax.fori_loop(0, n_rest // 2, pair, 0)

        @pl.when(n_rest % 2 == 1)
        def _():
            grad_step(s0, d0, NB - 1, False)

        dk_ref[0] = dk_acc[0:DH, :]
        dfk_ref[0, 0] = dk_acc[DH:DH + 8, :]
        dv_ref[0] = dv_acc[...]

    per_kv = lambda r: pl.BlockSpec((1, r, B), lambda h, j: (j, h, 0))
    head_all = lambda r: pl.BlockSpec((NB, r, B), lambda h, j: (0, h, 0))
    aug = pl.BlockSpec((1, B, 128), lambda h, j: (h, j, 0))
    stat = pl.BlockSpec((1, NB, 1, B), lambda h, j: (h, 0, 0, 0))
    return _pc(
        body, name="attn_bwd", grid=(H, NB),
        in_specs=[head_all(128), aug, per_kv(KR), aug, head_all(128), stat, stat],
        out_specs=[head_all(KR), per_kv(DH), per_kv(DH),
                   pl.BlockSpec((1, 1, 8, B), lambda h, j: (h, j, 0, 0))],
        out_shape=[_sds((NB, H * KR, B)), _sds((NB, A, B)), _sds((NB, A, B)), _sds((H, NB, 8, B))],
        scratch_shapes=[pltpu.VMEM((B, B), F32)] * 4 + [pltpu.VMEM((QR, B), F32), pltpu.VMEM((DH, B), F32)],
        compiler_params=_cp(("arbitrary", "arbitrary")),
    )(qTa, kaug, kT, vaug, doTa, lse, delta)


def _conv_tail(u1, lng, lnb, beta_c):
    mu = _rowmean(u1)
    d = u1 - mu
    rstd = lax.rsqrt(_rowmean(d * d) + EPS)
    xhat = d * rstd
    u2 = xhat * lng + lnb
    sg = _sigmoid(u2)
    u3 = u2 * sg
    rc = lax.rsqrt(_rowmean(u3 * u3) + EPS)
    n3 = u3 * rc
    return rstd, xhat, u2, sg, rc, n3, n3 * beta_c


def _attn_tail(oT, beta_a_col):
    ra = lax.rsqrt(jnp.mean(oT * oT, axis=0, keepdims=True) + EPS)
    ohat = oT * ra
    return ra, ohat, ohat * beta_a_col


def conv_merge_out(u0, x, oT, conv_w, conv_b, lng, lnb, beta_c, beta_a_col, wo, g1):
    S = x.shape[0]
    B = BLK
    NB = S // B

    def body(uc_ref, up_ref, x_ref, oT_ref, w_ref, cb_ref, lng_ref, lnb_ref, bc_ref, ba_ref, wo_ref,
             g1_ref, x1_ref, o_ref, u1_ref, ubuf):
        i = pl.program_id(0)
        ubuf[0:HALO, :] = jnp.where(i > 0, up_ref[B - HALO:B, :], 0.0)
        ubuf[HALO:HALO + B, :] = uc_ref[...]
        acc = jnp.broadcast_to(cb_ref[...], (B, CW))
        for k in range(KC):
            acc = acc + w_ref[k:k + 1, :] * ubuf[pl.ds(HALO - (KC - 1) + k, B), :]
        u1_ref[...] = acc
        mc = _conv_tail(acc, lng_ref[...], lnb_ref[...], bc_ref[...])[-1]
        maT = _attn_tail(oT_ref[0], ba_ref[...])[-1]
        ma = maT.T
        o = _nn(ma.astype(BF16), wo_ref[0:A, :]) + _nn(mc.astype(BF16), wo_ref[A:D, :])
        o_ref[...] = o
        x1_ref[...] = x_ref[...] + g1_ref[...] * o

    row = lambda w: pl.BlockSpec((B, w), lambda i: (i, 0))
    return _pc(
        body, name="conv_merge_out", grid=(NB,),
        in_specs=[row(CW), pl.BlockSpec((B, CW), lambda i: (jnp.maximum(i - 1, 0), 0)), row(D),
                  pl.BlockSpec((1, A, B), lambda i: (i, 0, 0)), _const((32, CW)), _const((1, CW)),
                  _const((1, CW)), _const((1, CW)), _const((1, CW)), _const((A, 1)), _const((D, D)),
                  _const((1, D))],
        out_specs=[row(D), row(D), row(CW)],
        out_shape=[_sds((S, D)), _sds((S, D)), _sds((S, CW))],
        scratch_shapes=[pltpu.VMEM((B + HALO, CW), F32)],
        compiler_params=_cp(("arbitrary",)),
    )(u0, u0, x, oT, conv_w, conv_b, lng, lnb, beta_c, beta_a_col, wo, g1)


def mlp_fwd_loss(x1, tgt, n2g, sc2, sh2, g2, w1, w2):
    S = x1.shape[0]
    B = BLK
    NB = S // B
    NC = FF // FCH

    def body(x1_ref, t_ref, g_ref, sc_ref, sh_ref, g2_ref, w1_ref, w2_ref, dy_ref, loss_ref, dg2_ref,
             hs, acc):
        i = pl.program_id(0)
        c = pl.program_id(1)

        @pl.when(jnp.logical_and(i == 0, c == 0))
        def _():
            loss_ref[...] = jnp.zeros_like(loss_ref)
            dg2_ref[...] = jnp.zeros_like(dg2_ref)

        @pl.when(c == 0)
        def _():
            hs[...] = _norm_mod(x1_ref[...], g_ref[...], sc_ref[...], sh_ref[...])[2].astype(BF16)
            acc[...] = jnp.zeros_like(acc)

        a = jnp.maximum(_nn(hs[...], w1_ref[...]), 0.0)
        acc[...] = acc[...] + _nn((a * a).astype(BF16), w2_ref[...])

        @pl.when(c == NC - 1)
        def _():
            y2 = acc[...]
            e = x1_ref[...] + g2_ref[...] * y2 - t_ref[...]
            dy = e * (1.0 / D)
            dy_ref[...] = dy
            loss_ref[...] = loss_ref[...] + 0.5 * _colsum(_rowmean(e * e))
            dg2_ref[...] = dg2_ref[...] + _colsum(dy * y2)

    row = pl.BlockSpec((B, D), lambda i, c: (i, 0))
    vec = pl.BlockSpec((1, D), lambda i, c: (0, 0))
    return _pc(
        body, name="mlp_fwd_loss", grid=(NB, NC),
        in_specs=[row, row, vec, vec, vec, vec, pl.BlockSpec((D, FCH), lambda i, c: (0, c)),
                  pl.BlockSpec((FCH, D), lambda i, c: (c, 0))],
        out_specs=[row, pl.BlockSpec((1, 1), lambda i, c: (0, 0)), vec],
        out_shape=[_sds((S, D)), _sds((1, 1)), _sds((1, D))],
        scratch_shapes=[pltpu.VMEM((B, D), BF16), pltpu.VMEM((B, D), F32)],
        compiler_params=_cp(("arbitrary", "arbitrary")),
    )(x1, tgt, n2g, sc2, sh2, g2, w1, w2)


def mlp_bwd(x1, dy, n2g, sc2, sh2, g2, w1, w2):
    S = x1.shape[0]
    B = BLK
    NB = S // B
    NC = FF // FCH

    def body(x1_ref, dy_ref, g_ref, sc_ref, sh_ref, g2_ref, w1_ref, w2_ref, dw1_ref, dw2t_ref, dh_ref):
        i = pl.program_id(1)

        @pl.when(i == 0)
        def _():
            dw1_ref[...] = jnp.zeros_like(dw1_ref)
            dw2t_ref[...] = jnp.zeros_like(dw2t_ref)

        h2 = _norm_mod(x1_ref[...], g_ref[...], sc_ref[...], sh_ref[...])[2]
        w1c = w1_ref[...]
        w2c = w2_ref[...]
        ra = jnp.maximum(_nn(h2.astype(BF16), w1c), 0.0)
        dy2 = dy_ref[...] * g2_ref[...]
        db = _nt(dy2.astype(BF16), w2c)
        dab = (db * (2.0 * ra)).astype(BF16)
        dh_ref[0] = _nt(dab, w1c)
        dw1_ref[...] = dw1_ref[...] + _nn(h2.T.astype(BF16), dab)
        dw2t_ref[...] = dw2t_ref[...] + _nn(dy2.T.astype(BF16), (ra * ra).astype(BF16))

    row = pl.BlockSpec((B, D), lambda c, i: (i, 0))
    vec = pl.BlockSpec((1, D), lambda c, i: (0, 0))
    wcol = pl.BlockSpec((D, FCH), lambda c, i: (0, c))
    return _pc(
        body, name="mlp_bwd", grid=(NC, NB),
        in_specs=[row, row, vec, vec, vec, vec, wcol, pl.BlockSpec((FCH, D), lambda c, i: (c, 0))],
        out_specs=[wcol, wcol, pl.BlockSpec((1, B, D), lambda c, i: (c, i, 0))],
        out_shape=[_sds((D, FF)), _sds((D, FF)), _sds((NC, S, D))],
        compiler_params=_cp(("arbitrary", "arbitrary")),
    )(x1, dy, n2g, sc2, sh2, g2, w1, w2)


def merge_bwd(dh2p, x1, dy, o, oT, u1, n2g, sc2, g1, lng, lnb, beta_c, beta_a_col, wo):
    S = x1.shape[0]
    B = BLK
    NB = S // B
    NC = dh2p.shape[0]

    def body(dh_ref, x1_ref, dy_ref, o_ref, oT_ref, u1_ref, g_ref, sc_ref, g1_ref, lng_ref, lnb_ref,
             bc_ref, ba_ref, wo_ref,
             dx1_ref, doTa_ref, dl_ref, du1_ref, dwo_ref, v1_ref, v2_ref, dba_ref):
        i = pl.program_id(0)

        @pl.when(i == 0)
        def _():
            dwo_ref[...] = jnp.zeros_like(dwo_ref)
            v1_ref[...] = jnp.zeros_like(v1_ref)
            v2_ref[...] = jnp.zeros_like(v2_ref)
            dba_ref[...] = jnp.zeros_like(dba_ref)

        dh2 = dh_ref[0]
        for cc in range(1, NC):
            dh2 = dh2 + dh_ref[cc]
        x1 = x1_ref[...]
        g = g_ref[...]
        sc = sc_ref[...]
        r2 = lax.rsqrt(_rowmean(x1 * x1) + EPS)
        xh = x1 * r2
        dhx = dh2 * xh
        v1_ref[0:1, :] = v1_ref[0:1, :] + _colsum(dh2)
        v1_ref[1:2, :] = v1_ref[1:2, :] + _colsum(dhx) * g
        v1_ref[2:3, :] = v1_ref[2:3, :] + _colsum(dhx) * (1.0 + sc)
        dxh = dh2 * (g * (1.0 + sc))
        dx1 = dy_ref[...] + r2 * (dxh - xh * _rowmean(dxh * xh))
        dx1_ref[...] = dx1
        v1_ref[3:4, :] = v1_ref[3:4, :] + _colsum(dx1 * o_ref[...])
        dob = (dx1 * g1_ref[...]).astype(BF16)

        lng = lng_ref[...]
        bc = bc_ref[...]
        rstd, xhat, u2, sg, rc, n3, mc = _conv_tail(u1_ref[...], lng, lnb_ref[...], bc)
        ba = ba_ref[...]
        oT = oT_ref[0]
        ra, ohat, maT = _attn_tail(oT, ba)
        dwo_ref[0:A, :] = dwo_ref[0:A, :] + _nn(maT.astype(BF16), dob)
        dwo_ref[A:D, :] = dwo_ref[A:D, :] + _nn(mc.T.astype(BF16), dob)
        dmaT = _nt(wo_ref[0:A, :], dob)
        dmc = _nt(dob, wo_ref[A:D, :])

        dba_ref[...] = dba_ref[...] + _lanesum(dmaT * ohat)
        dohat = dmaT * ba
        doT = ra * (dohat - ohat * jnp.mean(dohat * ohat, axis=0, keepdims=True))
        prod = doT * oT
        zpad = jnp.zeros((128 - DH, B), BF16)
        for hh in range(H):
            sl = slice(hh * DH, (hh + 1) * DH)
            dl_ref[hh, 0] = _colsum(prod[sl])
            doTa_ref[0, hh * 128:hh * 128 + DH, :] = doT[sl].astype(BF16)
            doTa_ref[0, hh * 128 + DH:(hh + 1) * 128, :] = zpad

        v2_ref[0:1, :] = v2_ref[0:1, :] + _colsum(dmc * n3)
        dn3 = dmc * bc
        du3 = rc * (dn3 - n3 * _rowmean(dn3 * n3))
        du2 = du3 * (sg * (1.0 + u2 * (1.0 - sg)))
        v2_ref[1:2, :] = v2_ref[1:2, :] + _colsum(du2 * xhat)
        v2_ref[2:3, :] = v2_ref[2:3, :] + _colsum(du2)
        dxhat = du2 * lng
        du1_ref[...] = rstd * (dxhat - _rowmean(dxhat) - xhat * _rowmean(dxhat * xhat))

    row = lambda w: pl.BlockSpec((B, w), lambda i: (i, 0))
    return _pc(
        body, name="merge_bwd", grid=(NB,),
        in_specs=[pl.BlockSpec((NC, B, D), lambda i: (0, i, 0)), row(D), row(D), row(D),
                  pl.BlockSpec((1, A, B), lambda i: (i, 0, 0)), row(CW), _const((1, D)), _const((1, D)),
                  _const((1, D)), _const((1, CW)), _const((1, CW)), _const((1, CW)), _const((A, 1)),
                  _const((D, D))],
        out_specs=[row(D), pl.BlockSpec((1, H * 128, B), lambda i: (i, 0, 0)),
                   pl.BlockSpec((H, 1, 1, B), lambda i: (0, i, 0, 0)), row(CW), _const((D, D)),
                   _const((8, D)), _const((8, CW)), _const((A, 1))],
        out_shape=[_sds((S, D)), _sds((NB, H * 128, B), BF16), _sds((H, NB, 1, B)), _sds((S, CW)),
                   _sds((D, D)), _sds((8, D)), _sds((8, CW)), _sds((A, 1))],
        compiler_params=_cp(("arbitrary",)),
    )(dh2p, x1, dy, o, oT, u1, n2g, sc2, g1, lng, lnb, beta_c, beta_a_col, wo)


def head_bwd(du1, u0, alg, zqk, fgT, dqT, dkT, dvT, dfk, conv_w, qg_col, kg_col, bf_col, tri_lo):
    S = u0.shape[0]
    B = BLK
    NB = S // B

    def body(dc_ref, dn_ref, uc_ref, up_ref, alg_ref, zqk_ref, fg_ref, dq_ref, dk_ref, dv_ref, dfk_ref,
             w_ref, qg_ref, kg_ref, bf_ref, tri_ref,
             dzr_ref, dzT_ref, dcw_ref, vc_ref, dqg_ref, dkg_ref, dbf_ref,
             dbuf, ubuf, carry, fbuf):
        pid = pl.program_id(0)
        ri = NB - 1 - pid

        @pl.when(pid == 0)
        def _():
            carry[...] = jnp.zeros_like(carry)
            dcw_ref[...] = jnp.zeros_like(dcw_ref)
            vc_ref[...] = jnp.zeros_like(vc_ref)
            dqg_ref[...] = jnp.zeros_like(dqg_ref)
            dkg_ref[...] = jnp.zeros_like(dkg_ref)
            dbf_ref[...] = jnp.zeros_like(dbf_ref)

        du1 = dc_ref[...]
        dbuf[0:B, :] = du1
        dbuf[B:B + HALO, :] = jnp.where(ri < NB - 1, dn_ref[0:HALO, :], 0.0)
        ubuf[0:HALO, :] = jnp.where(ri > 0, up_ref[B - HALO:B, :], 0.0)
        ubuf[HALO:HALO + B, :] = uc_ref[...]
        du0 = jnp.zeros((B, CW), F32)
        for k in range(KC):
            du0 = du0 + w_ref[k:k + 1, :] * dbuf[pl.ds(KC - 1 - k, B), :]
            dcw_ref[k:k + 1, :] = dcw_ref[k:k + 1, :] + _colsum(
                du1 * ubuf[pl.ds(HALO - (KC - 1) + k, B), :])
        vc_ref[0:1, :] = vc_ref[0:1, :] + _colsum(du1)

        al = alg_ref[:, 0:CW]
        sg = _sigmoid(alg_ref[:, CW:2 * CW])
        dzr_ref[:, 0:CW] = (du0 * sg).astype(BF16)
        dzr_ref[:, CW:2 * CW] = (du0 * al * sg * (1.0 - sg)).astype(BF16)

        dqg = jnp.zeros((DH, B), F32)
        dkg = jnp.zeros((DH, B), F32)
        qg = qg_ref[...]
        kg = kg_ref[...]
        for hh in range(H):
            sl = slice(hh * DH, (hh + 1) * DH)
            q = zqk_ref[sl, :]
            rq = lax.rsqrt(jnp.mean(q * q, axis=0, keepdims=True) + EPS)
            qn = q * rq
            dqh = dq_ref[0, hh * KR:hh * KR + DH, :] * 0.125
            dqg = dqg + dqh * qn
            dqn = dqh * qg
            dzT_ref[0, sl, :] = (rq * (dqn - qn * jnp.mean(dqn * qn, axis=0, keepdims=True))).astype(BF16)
            k = zqk_ref[A + hh * DH:A + (hh + 1) * DH, :]
            rk = lax.rsqrt(jnp.mean(k * k, axis=0, keepdims=True) + EPS)
            kn = k * rk
            dkh = dk_ref[0, sl, :]
            dkg = dkg + dkh * kn
            dkn = dkh * kg
            dzT_ref[0, A + hh * DH:A + (hh + 1) * DH, :] = (
                rk * (dkn - kn * jnp.mean(dkn * kn, axis=0, keepdims=True))).astype(BF16)
            fbuf[hh:hh + 1, :] = dq_ref[0, hh * KR + DH:hh * KR + DH + 1, :] - dfk_ref[hh, 0, 0:1, :]
        dqg_ref[...] = dqg_ref[...] + _lanesum(dqg)
        dkg_ref[...] = dkg_ref[...] + _lanesum(dkg)
        dzT_ref[0, 2 * A:3 * A, :] = dv_ref[0].astype(BF16)

        dF = fbuf[...]
        a1, a2, a3 = _split3(dF)
        tr = tri_ref[...]
        dlogf = _nn(a1, tr) + _nn(a2, tr) + _nn(a3, tr) + carry[...]
        carry[...] = carry[...] + _lanesum(dF)
        dfg = dlogf * _sigmoid(-(fg_ref[...] + bf_ref[...]))
        dbf_ref[...] = dbf_ref[...] + _lanesum(dfg)
        dzT_ref[0, 3 * A:ZT_ROWS, :] = jnp.concatenate([dfg, jnp.zeros((8, B), F32)], axis=0).astype(BF16)

    rrow = lambda w: pl.BlockSpec((B, w), lambda p: (NB - 1 - p, 0))
    rts = lambda r: pl.BlockSpec((1, r, B), lambda p: (NB - 1 - p, 0, 0))
    return _pc(
        body, name="head_bwd", grid=(NB,),
        in_specs=[rrow(CW), pl.BlockSpec((B, CW), lambda p: (jnp.minimum(NB - p, NB - 1), 0)),
                  rrow(CW), pl.BlockSpec((B, CW), lambda p: (jnp.maximum(NB - 2 - p, 0), 0)),
                  rrow(2 * CW), pl.BlockSpec((2 * A, B), lambda p: (0, NB - 1 - p)),
                  pl.BlockSpec((8, B), lambda p: (0, NB - 1 - p)), rts(H * KR), rts(A), rts(A),
                  pl.BlockSpec((H, 1, 8, B), lambda p: (0, NB - 1 - p, 0, 0)),
                  _const((32, CW)), _const((DH, 1)), _const((DH, 1)), _const((8, 1)), _const((B, B))],
        out_specs=[rrow(2 * CW), rts(ZT_ROWS), _const((32, CW)), _const((8, CW)), _const((DH, 1)),
                   _const((DH, 1)), _const((8, 1))],
        out_shape=[_sds((S, 2 * CW), BF16), _sds((NB, ZT_ROWS, B), BF16), _sds((32, CW)), _sds((8, CW)),
                   _sds((DH, 1)), _sds((DH, 1)), _sds((8, 1))],
        scratch_shapes=[pltpu.VMEM((B + HALO, CW), F32), pltpu.VMEM((B + HALO, CW), F32),
                        pltpu.VMEM((8, 1), F32), pltpu.VMEM((8, B), F32)],
        compiler_params=_cp(("arbitrary",)),
    )(du1, du1, u0, u0, alg, zqk, fgT, dqT, dkT, dvT, dfk, conv_w, qg_col, kg_col, bf_col, tri_lo)


def in_bwd(x, dx1, dzr, dzT, n1g, sc1, sh1, wT, w_qkvf, wr):
    S = x.shape[0]
    B = BLK
    NB = S // B

    def body(x_ref, dx1_ref, dzr_ref, dzT_ref, g_ref, sc_ref, sh_ref, wT_hbm, w_hbm, wr_hbm,
             gx_ref, dwT_hbm, dwr_hbm, v_ref, wT_v, w_v, wr_v, dwT_acc, dwr_acc, sem):
        i = pl.program_id(0)

        @pl.when(i == 0)
        def _():
            for src, dst in ((wT_hbm, wT_v), (w_hbm, w_v), (wr_hbm, wr_v)):
                cp = pltpu.make_async_copy(src, dst, sem)
                cp.start()
                cp.wait()
            dwT_acc[...] = jnp.zeros_like(dwT_acc)
            dwr_acc[...] = jnp.zeros_like(dwr_acc)
            v_ref[...] = jnp.zeros_like(v_ref)

        g = g_ref[...]
        sc = sc_ref[...]
        r1, xh, h1 = _norm_mod(x_ref[...], g, sc, sh_ref[...])
        hb = h1.astype(BF16)
        dzr = dzr_ref[...]
        dzT = dzT_ref[0]
        dwT_acc[...] = dwT_acc[...] + _nn(dzT, hb)
        dwr_acc[...] = dwr_acc[...] + _nn(h1.T.astype(BF16), dzr)
        dh1 = _nt(dzr, wr_v[...]) + _nn(w_v[...], dzT).T
        dhx = dh1 * xh
        v_ref[0:1, :] = v_ref[0:1, :] + _colsum(dh1)
        v_ref[1:2, :] = v_ref[1:2, :] + _colsum(dhx) * g
        v_ref[2:3, :] = v_ref[2:3, :] + _colsum(dhx) * (1.0 + sc)
        dxh = dh1 * (g * (1.0 + sc))
        gx_ref[...] = dx1_ref[...] + r1 * (dxh - xh * _rowmean(dxh * xh))

        @pl.when(i == NB - 1)
        def _():
            for src, dst in ((dwT_acc, dwT_hbm), (dwr_acc, dwr_hbm)):
                cp = pltpu.make_async_copy(src, dst, sem)
                cp.start()
                cp.wait()

    row = lambda w: pl.BlockSpec((B, w), lambda i: (i, 0))
    return _pc(
        body, name="in_bwd", grid=(NB,),
        in_specs=[row(D), row(D), row(2 * CW), pl.BlockSpec((1, ZT_ROWS, B), lambda i: (i, 0, 0)),
                  _const((1, D)), _const((1, D)), _const((1, D)), ANY, ANY, ANY],
        out_specs=[row(D), ANY, ANY, _const((8, D))],
        out_shape=[_sds((S, D)), _sds((ZT_ROWS, D)), _sds((D, 2 * CW)), _sds((8, D))],
        scratch_shapes=[pltpu.VMEM((ZT_ROWS, D), BF16), pltpu.VMEM((D, ZT_ROWS), BF16),
                        pltpu.VMEM((D, 2 * CW), BF16), pltpu.VMEM((ZT_ROWS, D), F32),
                        pltpu.VMEM((D, 2 * CW), F32), pltpu.SemaphoreType.DMA(())],
        compiler_params=_cp(("arbitrary",)),
    )(x, dx1, dzr, dzT, n1g, sc1, sh1, wT, w_qkvf, wr)


def _adam_math(g, w, m, v):
    m = ADAM_B1 * m + (1.0 - ADAM_B1) * g
    v = ADAM_B2 * v + (1.0 - ADAM_B2) * (g * g)
    m_hat = m / (1.0 - ADAM_B1 ** ADAM_STEP)
    v_hat = v / (1.0 - ADAM_B2 ** ADAM_STEP)
    delta = -ADAM_LR * (m_hat / (jnp.sqrt(v_hat) + ADAM_EPS) + ADAM_WD * w)
    return delta, m, v


def _row_tile(R):
    for t in (1024, 512, 256, 128, 64, 32, 16, 8):
        if R % t == 0:
            return t
    return R


def sum_slots(parts, name):
    K, R, C = parts.shape
    T = _row_tile(R)

    def body(p_ref, o_ref):
        s = p_ref[0]
        for k in range(1, K):
            s = s + p_ref[k]
        o_ref[...] = s

    return _pc(body, name=name, grid=(R // T,),
               in_specs=[pl.BlockSpec((K, T, C), lambda i: (0, i, 0))],
               out_specs=pl.BlockSpec((T, C), lambda i: (i, 0)), out_shape=_sds((R, C)),
               compiler_params=_cp(("arbitrary",)))(parts)


def adamw_slots(parts, w, m, v, name):
    K, R, C = parts.shape
    T = _row_tile(R)

    def body(p_ref, w_ref, m_ref, v_ref, g_ref, d_ref, nm_ref, nv_ref):
        g = p_ref[0]
        for k in range(1, K):
            g = g + p_ref[k]
        g_ref[...] = g
        d_ref[...], nm_ref[...], nv_ref[...] = _adam_math(g, w_ref[...], m_ref[...], v_ref[...])

    t2 = pl.BlockSpec((T, C), lambda i: (i, 0))
    return _pc(body, name=name, grid=(R // T,),
               in_specs=[pl.BlockSpec((K, T, C), lambda i: (0, i, 0)), t2, t2, t2],
               out_specs=[t2, t2, t2, t2], out_shape=[_sds((R, C))] * 4,
               compiler_params=_cp(("arbitrary",)))(parts, w, m, v)


def ada_grad_adamw(cT, dmod_cols, w, m, v):
    NCOL = w.shape[1]

    def body(cT_ref, dm_ref, w_ref, m_ref, v_ref, g_ref, d_ref, nm_ref, nv_ref):
        def term(b):
            cv = cT_ref[b]
            return (cv * _sigmoid(cv)) * dm_ref[b:b + 1, :]

        g = term(0)
        for b in range(1, N_DEV):
            g = g + term(b)
        g_ref[...] = g
        d_ref[...], nm_ref[...], nv_ref[...] = _adam_math(g, w_ref[...], m_ref[...], v_ref[...])

    full = _const((D, NCOL))
    return _pc(body, name="ada_grad_adamw", grid=(1,),
               in_specs=[_const((N_DEV, D, 1)), _const((N_DEV, NCOL)), full, full, full],
               out_specs=[full, full, full, full], out_shape=[_sds((D, NCOL))] * 4,
               compiler_params=_cp())(cT, dmod_cols, w, m, v)


def _pack_rows(vecs, rows=None):
    flat = jnp.concatenate([jnp.ravel(v) for v in vecs])
    n = flat.shape[0]
    r = -(-n // 1024) * 8 if rows is None else rows
    return jnp.pad(flat, (0, r * 128 - n)).reshape(r, 128)


def _unpack_rows(packed, shapes):
    flat = packed.reshape(-1)
    out, off = [], 0
    for s in shapes:
        n = 1
        for d in s:
            n *= d
        out.append(flat[off:off + n].reshape(s))
        off += n
    return out


def _cols_from_shards(g, rows, cols):
    return g.reshape(N_DEV, rows, cols).transpose(1, 0, 2).reshape(rows, N_DEV * cols)


def _cols_to_shards(w, cols):
    rows = w.shape[0]
    return w.reshape(rows, N_DEV, cols).transpose(1, 0, 2).reshape(N_DEV, rows * cols)


def kernel(x, c, w_ada, b_ada, norm1_g, w_in, q_norm_g, k_norm_g, b_f, conv_w, conv_b, conv_ln_g, conv_ln_b, beta_attn, beta_conv, w_out, norm2_g, w_ff1, w_ff2, loss_target, m_w_ada, m_b_ada, m_norm1_g, m_w_in, m_q_norm_g, m_k_norm_g, m_b_f, m_conv_w, m_conv_b, m_conv_ln_g, m_conv_ln_b, m_beta_attn, m_beta_conv, m_w_out, m_norm2_g, m_w_ff1, m_w_ff2, v_w_ada, v_b_ada, v_norm1_g, v_w_in, v_q_norm_g, v_k_norm_g, v_b_f, v_conv_w, v_conv_b, v_conv_ln_g, v_conv_ln_b, v_beta_attn, v_beta_conv, v_w_out, v_norm2_g, v_w_ff1, v_w_ff2):
    S = x.shape[1]
    B = BLK
    NB = S // B
    me = 4 * lax.axis_index("x") + 2 * lax.axis_index("y") + lax.axis_index("c")
    xs = x[0]
    tgt = loss_target[0]

    ADA_C, IN_C, FF_C, CV_C = w_ada.shape[2], w_in.shape[2], w_ff1.shape[2], conv_w.shape[2]
    OUT_R, FF_R = w_out.shape[1], w_ff2.shape[1]
    IN_W = N_DEV * IN_C

    n_ada, n_in, n_out, n_f1, n_f2 = D * ADA_C, D * IN_C, OUT_R * D, D * FF_C, FF_R * D
    big = jnp.concatenate([w_ada.reshape(-1), w_in.reshape(-1), w_out.reshape(-1), w_ff1.reshape(-1),
                           w_ff2.reshape(-1)]).astype(BF16)
    rows_big = -(-big.shape[0] // (128 * 16)) * 16
    big = jnp.pad(big, (0, rows_big * 128 - big.shape[0])).reshape(rows_big, 128)
    gw = all_gather(big, "ag_weights").reshape(N_DEV, rows_big * 128)
    small0 = _pack_rows([c, conv_w], rows=24)
    gs0 = all_gather(small0, "ag_small_fwd").reshape(N_DEV, 24 * 128)

    off = 0
    w_ada_full = _cols_from_shards(gw[:, off:off + n_ada], D, ADA_C); off += n_ada
    w_in_full = _cols_from_shards(gw[:, off:off + n_in], D, IN_C); off += n_in
    wo_full = gw[:, off:off + n_out].reshape(D, D); off += n_out
    w1_full = _cols_from_shards(gw[:, off:off + n_f1], D, FF_C); off += n_f1
    w2_full = gw[:, off:off + n_f2].reshape(FF, D)
    c_all = gs0[:, 0:D]
    conv_w_full = _cols_from_shards(gs0[:, D:D + KC * CV_C], KC, CV_C)
    conv_w_pad = jnp.pad(conv_w_full, ((0, 32 - KC), (0, 0)))

    w_qkvf = jnp.pad(w_in_full[:, 0:3 * A + H], ((0, 0), (0, ZT_ROWS - 3 * A - H)))
    wT = w_qkvf.T
    wr = w_in_full[:, 3 * A + H:]

    qg_col = q_norm_g.reshape(DH, 1)
    kg_col = k_norm_g.reshape(DH, 1)
    bf_col = b_f.reshape(H, 1)
    beta_a_col = beta_attn.reshape(A, 1)
    ii = lax.broadcasted_iota(jnp.int32, (B, B), 0)
    jj = lax.broadcasted_iota(jnp.int32, (B, B), 1)
    tri_up = (ii <= jj).astype(BF16)
    tri_lo = (ii >= jj).astype(BF16)

    mod = mod_kernel(c, w_ada_full, b_ada)
    sh1, sc1, g1, sh2, sc2, g2 = [mod[:, k * D:(k + 1) * D] for k in range(6)]
    qTa, kT, kaug, vT, vaug, zqk, fgT, alg, u0 = fwd_in(xs, norm1_g, sc1, sh1, wT, wr, qg_col, kg_col,
                                                       bf_col, tri_up)
    oT, lse = attn_fwd(qTa, kaug, vT)
    x1, o, u1 = conv_merge_out(u0, xs, oT, conv_w_pad, conv_b, conv_ln_g, conv_ln_b, beta_conv,
                               beta_a_col, wo_full, g1)
    dy, loss_part, dg2 = mlp_fwd_loss(x1, tgt, norm2_g, sc2, sh2, g2, w1_full, w2_full)

    dw1, dw2t, dh2p = mlp_bwd(x1, dy, norm2_g, sc2, sh2, g2, w1_full, w2_full)
    dx1, doTa, delta, du1, dwo, v1, v2, dba = merge_bwd(dh2p, x1, dy, o, oT, u1, norm2_g, sc2, g1,
                                                         conv_ln_g, conv_ln_b, beta_conv, beta_a_col, wo_full)
    dqT, dkT, dvT, dfk = attn_bwd(qTa, kaug, kT, vaug, doTa, lse, delta)
    dzr, dzT, dcw, vc, dqg, dkg, dbf = head_bwd(du1, u0, alg, zqk, fgT, dqT, dkT, dvT, dfk, conv_w_pad,
                                                qg_col, kg_col, bf_col, tri_lo)
    grad_x, dwT, dwr, v0 = in_bwd(xs, dx1, dzr, dzT, norm1_g, sc1, sh1, wT, w_qkvf, wr)

    dw_in = jnp.concatenate([dwT.T[:, 0:3 * A + H], dwr], axis=1)
    gbig = jnp.concatenate([_cols_to_shards(dw_in, IN_C), dwo.reshape(N_DEV, n_out),
                            _cols_to_shards(dw1, FF_C), dw2t.T.reshape(N_DEV, n_f2)], axis=1)
    n_g = gbig.shape[1]
    rows_g = -(-n_g // (128 * 512)) * 512
    gbig = jnp.pad(gbig, ((0, 0), (0, rows_g * 128 - n_g)))
    recv =reduce_scatter_exchange(gbig.reshape(N_DEV, rows_g, 128), "rs_grads")

    dmod = jnp.concatenate([v0[0], v0[1], v1[3], v1[0], v1[1], dg2[0]])
    small1 = _pack_rows([dmod, v0[2], v1[2], dcw, vc[0], v2[1], v2[2], v2[0], dba, dqg, dkg,
                         jnp.pad(dbf.reshape(-1), (0, 120)), jnp.pad(loss_part.reshape(-1), (0, 127))])
    gs1 = all_gather(small1, "ag_small_bwd")
    tot = sum_slots(gs1, "sum_small")
    (g_b_ada, g_n1, g_n2, g_cw, g_cb, g_lng, g_lnb, g_bc, g_ba, g_qg, g_kg, g_bf, loss_v) = _unpack_rows(
        tot, [(1, 6 * D), (1, D), (1, D), (32, CW), (1, CW), (1, CW), (1, CW), (1, CW), (1, A),
              (1, DH), (1, DH), (1, 128), (1, 128)])
    loss = loss_v[0, 0]
    g_bf = g_bf[:, 0:H]
    g_cw_mine = lax.dynamic_slice(g_cw[0:KC], (0, me * CV_C), (KC, CV_C)).reshape(1, KC, CV_C)

    small_names = [(b_ada, m_b_ada, v_b_ada, g_b_ada), (norm1_g, m_norm1_g, v_norm1_g, g_n1),
                   (q_norm_g, m_q_norm_g, v_q_norm_g, g_qg), (k_norm_g, m_k_norm_g, v_k_norm_g, g_kg),
                   (b_f, m_b_f, v_b_f, g_bf), (conv_w, m_conv_w, v_conv_w, g_cw_mine),
                   (conv_b, m_conv_b, v_conv_b, g_cb), (conv_ln_g, m_conv_ln_g, v_conv_ln_g, g_lng),
                   (conv_ln_b, m_conv_ln_b, v_conv_ln_b, g_lnb), (beta_attn, m_beta_attn, v_beta_attn, g_ba),
                   (beta_conv, m_beta_conv, v_beta_conv, g_bc), (norm2_g, m_norm2_g, v_norm2_g, g_n2)]
    shapes_s = [t[0].shape for t in small_names]
    pw, pm, pv, pg = [_pack_rows([t[k] for t in small_names]) for k in range(4)]
    sg_, sd_, sm_, sv_ = adamw_slots(pg[None], pw, pm, pv, "adamw_small")
    sgs, sds, sms, svs = [_unpack_rows(a, shapes_s) for a in (sg_, sd_, sm_, sv_)]

    dmod_all = gs1[:, 0:48, :].reshape(N_DEV, N_DEV, ADA_C)
    dmod_cols = lax.dynamic_index_in_dim(dmod_all, me, axis=1, keepdims=False)
    ga, da, ma_, va_ = ada_grad_adamw(c_all.reshape(N_DEV, D, 1), dmod_cols, w_ada[0], m_w_ada[0], v_w_ada[0])

    big_shapes = [(1, D, IN_C), (1, OUT_R, D), (1, D, FF_C), (1, FF_R, D)]
    pack_big = lambda ws: _pack_rows(ws, rows=rows_g)
    bw = pack_big([w_in, w_out, w_ff1, w_ff2])
    bm = pack_big([m_w_in, m_w_out, m_w_ff1, m_w_ff2])
    bv = pack_big([v_w_in, v_w_out, v_w_ff1, v_w_ff2])
    bg_, bd_, bm_, bv_ = adamw_slots(recv, bw, bm, bv, "adamw_big")
    bgs, bds, bms, bvs = [_unpack_rows(a, big_shapes) for a in (bg_, bd_, bm_, bv_)]

    def assemble(small, ada, bigs):
        (b_ada_, n1_, qg_, kg_, bf_, cw_, cb_, lng_, lnb_, ba_, bc_, n2_) = small
        return [ada.reshape(1, D, ADA_C), b_ada_, n1_, bigs[0], qg_, kg_, bf_, cw_, cb_, lng_, lnb_, ba_, bc_,
                bigs[1], n2_, bigs[2], bigs[3]]

    return (loss, grad_x.reshape(1, S, D), *assemble(sgs, ga, bgs), *assemble(sds, da, bds),
            *assemble(sms, ma_, bms), *assemble(svs, va_, bvs))
```

```python
import functools

import jax
import jax.numpy as jnp
from jax import lax
from jax.experimental import pallas as pl
from jax.experimental.pallas import tpu as pltpu

F32 = jnp.float32
BF16 = jnp.bfloat16

D = 1024
A = 512
CW = 512
H = 8
DH = 64
FF = 4096
KC = 31
HALO = 32
KR = 80
ZT_ROWS = 1552
FCH = 1024
EPS = 1e-6
BLK = 512
N_DEV = 8
VMEM_LIMIT = 56 * 1024 * 1024

ADAM_LR = 0.001
ADAM_B1 = 0.9
ADAM_B2 = 0.999
ADAM_EPS = 1e-08
ADAM_WD = 0.01
ADAM_STEP = 10

MESH = pl.DeviceIdType.MESH
ANY = pl.BlockSpec(memory_space=pl.ANY)
SMEM = pl.BlockSpec(memory_space=pltpu.SMEM)
PRUNE = 110.0


def _pc(body, **kw):
    return pl.pallas_call(body, **kw)


def _cp(sem=None):
    return pltpu.CompilerParams(dimension_semantics=sem, vmem_limit_bytes=VMEM_LIMIT)


def _sds(shape, dtype=F32):
    return jax.ShapeDtypeStruct(shape, dtype)


def _const(shape):
    n = len(shape)
    return pl.BlockSpec(shape, lambda *a: (0,) * n)


def _nt(a, b):
    return lax.dot_general(a, b, (((1,), (1,)), ((), ())), preferred_element_type=F32)


def _nn(a, b):
    return jnp.dot(a, b, preferred_element_type=F32)


def _sigmoid(v):
    return 1.0 / (1.0 + jnp.exp(-v))


def _split3(v):
    a1 = v.astype(BF16)
    r1 = v - a1.astype(F32)
    a2 = r1.astype(BF16)
    a3 = (r1 - a2.astype(F32)).astype(BF16)
    return a1, a2, a3


def _rowmean(v):
    return jnp.mean(v, axis=-1, keepdims=True)


def _colsum(v):
    return jnp.sum(v, axis=0, keepdims=True)


def _lanesum(v):
    return jnp.sum(v, axis=-1, keepdims=True)


def _coords():
    return lax.axis_index("x"), lax.axis_index("y"), lax.axis_index("c")


def all_gather(x, name):
    R, C = x.shape

    def body(x_ref, out_ref, send_sems, recv_sems, local_sem):
        mx, my, mc = _coords()
        me, sibling = (mx, my, mc), (mx, my, 1 - mc)
        chips = [(1 - mx, my), (mx, 1 - my), (1 - mx, 1 - my)]

        def rows(px, py, pc):
            return out_ref.at[4 * px + 2 * py + pc]

        def copy(k, block, to, src=None):
            return pltpu.make_async_remote_copy(
                src_ref=rows(*block) if src is None else src, dst_ref=rows(*block),
                send_sem=send_sems.at[k], recv_sem=recv_sems.at[k],
                device_id=to, device_id_type=MESH)

        mine = pltpu.make_async_copy(x_ref, rows(*me), local_sem)
        mine.start()
        first = [copy(0, me, sibling, src=x_ref)]
        first += [copy(1 + j, me, (*chip, mc), src=x_ref) for j, chip in enumerate(chips)]
        for cp in first:
            cp.start()
        passed = [copy(4 + j, (*chip, mc), sibling) for j, chip in enumerate(chips)]
        for j, chip in enumerate(chips):
            copy(1 + j, (*chip, mc), me).wait_recv()
            passed[j].start()
        copy(0, sibling, me).wait_recv()
        for j, chip in enumerate(chips):
            copy(4 + j, (*chip, 1 - mc), me).wait_recv()
        for cp in first + passed:
            cp.wait_send()
        mine.wait()

    return _pc(
        body, name=name, out_shape=_sds((N_DEV, R, C), x.dtype),
        in_specs=[ANY], out_specs=ANY,
        scratch_shapes=[pltpu.SemaphoreType.DMA((7,)), pltpu.SemaphoreType.DMA((7,)),
                        pltpu.SemaphoreType.DMA(())],
    )(x)


def reduce_scatter_exchange(g, name):
    _, R, C = g.shape

    def body(g_ref, recv_ref, send_sems, recv_sems, local_sem):
        mx, my, mc = _coords()
        me = 4 * mx + 2 * my + mc
        local = pltpu.make_async_copy(g_ref.at[me], recv_ref.at[0], local_sem)
        local.start()
        copies = []
        for r in range(1, N_DEV):
            px = 1 - mx if r & 4 else mx
            py = 1 - my if r & 2 else my
            pcc = 1 - mc if r & 1 else mc
            cp = pltpu.make_async_remote_copy(
                src_ref=g_ref.at[4 * px + 2 * py + pcc], dst_ref=recv_ref.at[r],
                send_sem=send_sems.at[r - 1], recv_sem=recv_sems.at[r - 1],
                device_id=(px, py, pcc), device_id_type=MESH)
            cp.start()
            copies.append(cp)
        for cp in copies:
            cp.wait()
        local.wait()

    return _pc(
        body, name=name, out_shape=_sds(g.shape, g.dtype),
        in_specs=[ANY], out_specs=ANY,
        scratch_shapes=[pltpu.SemaphoreType.DMA((7,)), pltpu.SemaphoreType.DMA((7,)),
                        pltpu.SemaphoreType.DMA(())],
    )(g)


def mod_kernel(c, w_ada_b, b_ada):
    def body(c_ref, w_ref, b_ref, o_ref):
        cv = c_ref[...]
        sc = jnp.broadcast_to(cv * _sigmoid(cv), (8, D)).astype(BF16)
        o_ref[...] = _nn(sc, w_ref[...])[0:1, :] + b_ref[...]

    return _pc(body, name="mod", out_shape=_sds((1, 6 * D)),
               in_specs=[_const((1, D)), _const((D, 6 * D)), _const((1, 6 * D))],
               out_specs=_const((1, 6 * D)), grid=(1,), compiler_params=_cp())(c, w_ada_b, b_ada)


def _norm_mod(xv, g, sc, sh):
    r = lax.rsqrt(_rowmean(xv * xv) + EPS)
    xh = xv * r
    return r, xh, xh * (g * (1.0 + sc)) + sh


def _log_sigmoid(v):
    e = jnp.exp(-jnp.abs(v))
    l1p = jnp.where(e < 1e-4, e * (1.0 - 0.5 * e), jnp.log(1.0 + e))
    return jnp.minimum(v, 0.0) - l1p


def fwd_in(x, n1g, sc1, sh1, wT, wr, qg_col, kg_col, bf_col, tri):
    S = x.shape[0]
    B = BLK
    NB = S // B

    def body(x_ref, g_ref, sc_ref, sh_ref, wT_ref, wr_ref, qg_ref, kg_ref, bf_ref, tri_ref,
             qTa_ref, kT_ref, kaug_ref, vT_ref, vaug_ref, zqk_ref, fgT_ref, alg_ref, u0_ref, stat_ref,
             carry, tbuf, fs, nq, nk):
        i = pl.program_id(0)

        @pl.when(i == 0)
        def _():
            carry[...] = jnp.zeros_like(carry)

        _, _, h = _norm_mod(x_ref[...], g_ref[...], sc_ref[...], sh_ref[...])
        hb = h.astype(BF16)
        zT = _nt(wT_ref[...], hb)
        zr = _nn(hb, wr_ref[...])
        zqk_ref[...] = zT[0:2 * A]
        fgT = zT[3 * A:3 * A + 8]
        fgT_ref[...] = fgT
        alg_ref[...] = zr
        u0_ref[...] = zr[:, 0:CW] * _sigmoid(zr[:, CW:2 * CW])

        logf = _log_sigmoid(fgT + bf_ref[...])
        a1, a2, a3 = _split3(logf)
        tr = tri_ref[...]
        F = _nn(a1, tr) + _nn(a2, tr) + _nn(a3, tr) + carry[...]
        carry[...] = carry[...] + _lanesum(logf)
        p1, p2, p3 = _split3(F)
        n1, n2, n3 = _split3(-F)
        for k, v in enumerate((p1, p2, p3, n1, n2, n3)):
            fs[k] = v.astype(F32)

        rowi = lax.broadcasted_iota(jnp.int32, (8, B), 0)
        zeros_tail = jnp.zeros((128 - DH - 8, B), F32)
        ones_row = jnp.where(lax.broadcasted_iota(jnp.int32, (KR - DH, B), 0) == 0, 1.0, 0.0).astype(BF16)
        for hh in range(H):
            sl = slice(hh * DH, (hh + 1) * DH)
            q = zT[sl]
            k = zT[A + hh * DH:A + (hh + 1) * DH]
            v = zT[2 * A + hh * DH:2 * A + (hh + 1) * DH]
            qh = q * lax.rsqrt(jnp.mean(q * q, axis=0, keepdims=True) + EPS) * qg_ref[...] * 0.125
            kh = k * lax.rsqrt(jnp.mean(k * k, axis=0, keepdims=True) + EPS) * kg_ref[...]
            frow = [fs[kk, hh:hh + 1, :] for kk in range(6)]
            qx = jnp.where(rowi < 3, 1.0, jnp.where(rowi == 3, frow[0], jnp.where(
                rowi == 4, frow[1], jnp.where(rowi == 5, frow[2], 0.0))))
            kx = jnp.where(rowi == 0, frow[3], jnp.where(rowi == 1, frow[4], jnp.where(
                rowi == 2, frow[5], jnp.where(rowi < 6, 1.0, 0.0))))
            tbuf[0:DH, :] = qh
            tbuf[DH:DH + 8, :] = qx
            tbuf[DH + 8:128, :] = zeros_tail
            qTa_ref[0, hh * 128:(hh + 1) * 128, :] = tbuf[...].astype(BF16)
            tbuf[0:DH, :] = kh
            tbuf[DH:DH + 8, :] = kx
            kaug_ref[hh] = tbuf[...].T.astype(BF16)
            kT_ref[0, hh * KR:hh * KR + DH, :] = kh.astype(BF16)
            kT_ref[0, hh * KR + DH:(hh + 1) * KR, :] = ones_row
            tbuf[0:DH, :] = v
            tbuf[DH:DH + 8, :] = jnp.zeros((8, B), F32)
            vaug_ref[hh] = tbuf[...].T.astype(BF16)
            vT_ref[0, sl, :] = v.astype(BF16)
            nq[hh:hh + 1, :] = jnp.max(_colsum(qh * qh), axis=1, keepdims=True)
            nk[hh:hh + 1, :] = jnp.max(_colsum(kh * kh), axis=1, keepdims=True)

        lane = lax.broadcasted_iota(jnp.int32, (8, 128), 1)
        stat_ref[0] = jnp.where(lane == 0, jnp.max(F, axis=1, keepdims=True), jnp.where(
            lane == 1, jnp.min(F, axis=1, keepdims=True), jnp.where(
                lane == 2, nq[...], jnp.where(lane == 3, nk[...], 0.0))))

    row = lambda w: pl.BlockSpec((B, w), lambda i: (i, 0))
    tsp = lambda r: pl.BlockSpec((1, r, B), lambda i: (i, 0, 0))
    return _pc(
        body, name="fwd_in", grid=(NB,),
        in_specs=[row(D), _const((1, D)), _const((1, D)), _const((1, D)), _const((ZT_ROWS, D)),
                  _const((D, 2 * CW)), _const((DH, 1)), _const((DH, 1)), _const((8, 1)), _const((B, B))],
        out_specs=[tsp(H * 128), tsp(H * KR), pl.BlockSpec((H, B, 128), lambda i: (0, i, 0)), tsp(A),
                   pl.BlockSpec((H, B, 128), lambda i: (0, i, 0)),
                   pl.BlockSpec((2 * A, B), lambda i: (0, i)), pl.BlockSpec((8, B), lambda i: (0, i)),
                   row(2 * CW), row(CW), pl.BlockSpec((1, 8, 128), lambda i: (i, 0, 0))],
        out_shape=[_sds((NB, H * 128, B), BF16), _sds((NB, H * KR, B), BF16), _sds((H, S, 128), BF16),
                   _sds((NB, A, B), BF16), _sds((H, S, 128), BF16), _sds((2 * A, S)), _sds((8, S)),
                   _sds((S, 2 * CW)), _sds((S, CW)), _sds((NB, 8, 128))],
        scratch_shapes=[pltpu.VMEM((8, 1), F32), pltpu.VMEM((128, B), F32), pltpu.VMEM((6, 8, B), F32),
                        pltpu.VMEM((8, 1), F32), pltpu.VMEM((8, 1), F32)],
        compiler_params=_cp(("arbitrary",)),
    )(x, n1g, sc1, sh1, wT, wr, qg_col, kg_col, bf_col, tri)


def _first_key_block(top, fmin_ref, cut, h, i):
    return lax.while_loop(
        lambda j: jnp.logical_and(j > 0, top - fmin_ref[h, jnp.maximum(j - 1, 0)] >= cut),
        lambda j: j - 1, i)


def _causal_keep(B):
    return lax.broadcasted_iota(jnp.int32, (B, B), 0) <= lax.broadcasted_iota(jnp.int32, (B, B), 1)


def attn_fwd(fmax, fmin, thr, qTa, kaug, vT):
    NB, _, B = qTa.shape
    S = NB * B

    def body(fmax_ref, fmin_ref, thr_ref, q_ref, k_ref, v_ref, o_ref, lse_ref, s0, s1, m_ref, l_ref, acc_ref):
        h = pl.program_id(0)
        i = pl.program_id(1)
        jlo = _first_key_block(fmax_ref[h, i], fmin_ref, thr_ref[h], h, i)
        n = i - jlo

        def scores(j, s_ref):
            s_ref[...] = _nn(k_ref[0, pl.ds(pl.multiple_of(j * B, B), B), :], q_ref[0])

        def softmax_step(s_ref, j, masked):
            s = s_ref[...]
            if masked:
                s = jnp.where(_causal_keep(B), s, -jnp.inf)
            m = m_ref[...]
            mn = jnp.maximum(m, jnp.max(s, axis=0, keepdims=True))
            a = jnp.exp(m - mn)
            p = jnp.exp(s - mn)
            m_ref[...] = mn
            l_ref[...] = a * l_ref[...] + _colsum(p)
            acc_ref[...] = a * acc_ref[...] + _nn(v_ref[j], p.astype(BF16))

        m_ref[...] = jnp.full((1, B), -jnp.inf, F32)
        l_ref[...] = jnp.zeros((1, B), F32)
        acc_ref[...] = jnp.zeros((DH, B), F32)
        scores(jlo, s0)

        def pair(t, carry):
            j = jlo + 2 * t
            scores(j + 1, s1)
            softmax_step(s0, j, False)
            scores(j + 2, s0)
            softmax_step(s1, j + 1, False)
            return carry

        lax.fori_loop(0, n // 2, pair, 0)

        @pl.when(n % 2 == 1)
        def _():
            scores(i, s1)
            softmax_step(s0, i - 1, False)
            softmax_step(s1, i, True)

        @pl.when(n % 2 == 0)
        def _():
            softmax_step(s0, i, True)

        l = l_ref[...]
        o_ref[0] = acc_ref[...] / l
        lse_ref[0, 0] = m_ref[...] + jnp.log(l)

    return _pc(
        body, name="attn_fwd", grid=(H, NB),
        in_specs=[SMEM, SMEM, SMEM, pl.BlockSpec((1, 128, B), lambda h, i: (i, h, 0)),
                  pl.BlockSpec((1, S, 128), lambda h, i: (h, 0, 0)),
                  pl.BlockSpec((NB, DH, B), lambda h, i: (0, h, 0))],
        out_specs=[pl.BlockSpec((1, DH, B), lambda h, i: (i, h, 0)),
                   pl.BlockSpec((1, 1, 1, B), lambda h, i: (h, i, 0, 0))],
        out_shape=[_sds((NB, A, B)), _sds((H, NB, 1, B))],
        scratch_shapes=[pltpu.VMEM((B, B), F32), pltpu.VMEM((B, B), F32), pltpu.VMEM((1, B), F32),
                        pltpu.VMEM((1, B), F32), pltpu.VMEM((DH, B), F32)],
        compiler_params=_cp(("arbitrary", "arbitrary")),
    )(fmax, fmin, thr, qTa, kaug, vT)


def attn_bwd(fmax, fmin, thr, qTa, kaug, kT, vaug, doTa, lse, delta):
    NB, _, B = qTa.shape
    QR = 80

    def body(fmax_ref, fmin_ref, thr_ref, q_ref, ka_ref, kt_ref, va_ref, do_ref, lse_ref, dl_ref,
             dq_ref, dk_ref, dv_ref, dfk_ref, s0, d0, s1, d1, dk_acc, dv_acc):
        h = pl.program_id(0)
        j = pl.program_id(1)
        bottom = fmin_ref[h, j]
        cut = thr_ref[h]
        ihi = lax.while_loop(
            lambda i: jnp.logical_and(i < NB - 1, fmax_ref[h, jnp.minimum(i + 1, NB - 1)] - bottom >= cut),
            lambda i: i + 1, j)

        @pl.when(j == 0)
        def _():
            dq_ref[...] = jnp.zeros_like(dq_ref)

        dk_acc[...] = jnp.zeros_like(dk_acc)
        dv_acc[...] = jnp.zeros_like(dv_acc)

        def products(i, s_ref, d_ref):
            s_ref[...] = _nn(ka_ref[0], q_ref[i])
            d_ref[...] = _nn(va_ref[0], do_ref[i])

        def grad_step(s_ref, d_ref, i, masked):
            p = jnp.exp(s_ref[...] - lse_ref[0, i])
            if masked:
                p = jnp.where(_causal_keep(B), p, 0.0)
            ds = p * (d_ref[...] - dl_ref[0, i])
            pb = p.astype(BF16)
            dsb = ds.astype(BF16)
            dv_acc[...] = dv_acc[...] + _nt(do_ref[i][0:DH], pb)
            dk_acc[...] = dk_acc[...] + _nt(q_ref[i][0:QR], dsb)
            dq_ref[i] = dq_ref[i] + _nn(kt_ref[0], dsb)

        n_rest = ihi - j
        products(j, s1, d1)
        products(jnp.minimum(j + 1, NB - 1), s0, d0)
        grad_step(s1, d1, j, True)

        def pair(t, carry):
            i = j + 1 + 2 * t
            products(i + 1, s1, d1)
            grad_step(s0, d0, i, False)
            products(jnp.minimum(i + 2, NB - 1), s0, d0)
            grad_step(s1, d1, i + 1, False)
            return carry

        lax.fori_loop(0, n_rest // 2, pair, 0)

        @pl.when(n_rest % 2 == 1)
        def _():
            grad_step(s0, d0, ihi, False)

        dk_ref[0] = dk_acc[0:DH, :]
        dfk_ref[0, 0] = dk_acc[DH:DH + 8, :]
        dv_ref[0] = dv_acc[...]

    per_kv = lambda r: pl.BlockSpec((1, r, B), lambda h, j: (j, h, 0))
    head_all = lambda r: pl.BlockSpec((NB, r, B), lambda h, j: (0, h, 0))
    aug = pl.BlockSpec((1, B, 128), lambda h, j: (h, j, 0))
    stat = pl.BlockSpec((1, NB, 1, B), lambda h, j: (h, 0, 0, 0))
    return _pc(
        body, name="attn_bwd", grid=(H, NB),
        in_specs=[SMEM, SMEM, SMEM, head_all(128), aug, per_kv(KR), aug, head_all(128), stat, stat],
        out_specs=[head_all(KR), per_kv(DH), per_kv(DH),
                   pl.BlockSpec((1, 1, 8, B), lambda h, j: (h, j, 0, 0))],
        out_shape=[_sds((NB, H * KR, B)), _sds((NB, A, B)), _sds((NB, A, B)), _sds((H, NB, 8, B))],
        scratch_shapes=[pltpu.VMEM((B, B), F32)] * 4 + [pltpu.VMEM((QR, B), F32), pltpu.VMEM((DH, B), F32)],
        compiler_params=_cp(("arbitrary", "arbitrary")),
    )(fmax, fmin, thr, qTa, kaug, kT, vaug, doTa, lse, delta)


def _conv_tail(u1, lng, lnb, beta_c):
    mu = _rowmean(u1)
    d = u1 - mu
    rstd = lax.rsqrt(_rowmean(d * d) + EPS)
    xhat = d * rstd
    u2 = xhat * lng + lnb
    sg = _sigmoid(u2)
    u3 = u2 * sg
    rc = lax.rsqrt(_rowmean(u3 * u3) + EPS)
    n3 = u3 * rc
    return rstd, xhat, u2, sg, rc, n3, n3 * beta_c


def _attn_tail(oT, beta_a_col):
    ra = lax.rsqrt(jnp.mean(oT * oT, axis=0, keepdims=True) + EPS)
    ohat = oT * ra
    return ra, ohat, ohat * beta_a_col


def conv_merge_out(u0, x, oT, conv_w, conv_b, lng, lnb, beta_c, beta_a_col, wo, g1):
    S = x.shape[0]
    B = BLK
    NB = S // B

    def body(uc_ref, up_ref, x_ref, oT_ref, w_ref, cb_ref, lng_ref, lnb_ref, bc_ref, ba_ref, wo_ref,
             g1_ref, x1_ref, o_ref, u1_ref, ubuf):
        i = pl.program_id(0)
        ubuf[0:HALO, :] = jnp.where(i > 0, up_ref[B - HALO:B, :], 0.0)
        ubuf[HALO:HALO + B, :] = uc_ref[...]
        acc = jnp.broadcast_to(cb_ref[...], (B, CW))
        for k in range(KC):
            acc = acc + w_ref[k:k + 1, :] * ubuf[pl.ds(HALO - (KC - 1) + k, B), :]
        u1_ref[...] = acc
        mc = _conv_tail(acc, lng_ref[...], lnb_ref[...], bc_ref[...])[-1]
        maT = _attn_tail(oT_ref[0], ba_ref[...])[-1]
        ma = maT.T
        o = _nn(ma.astype(BF16), wo_ref[0:A, :]) + _nn(mc.astype(BF16), wo_ref[A:D, :])
        o_ref[...] = o
        x1_ref[...] = x_ref[...] + g1_ref[...] * o

    row = lambda w: pl.BlockSpec((B, w), lambda i: (i, 0))
    return _pc(
        body, name="conv_merge_out", grid=(NB,),
        in_specs=[row(CW), pl.BlockSpec((B, CW), lambda i: (jnp.maximum(i - 1, 0), 0)), row(D),
                  pl.BlockSpec((1, A, B), lambda i: (i, 0, 0)), _const((32, CW)), _const((1, CW)),
                  _const((1, CW)), _const((1, CW)), _const((1, CW)), _const((A, 1)), _const((D, D)),
                  _const((1, D))],
        out_specs=[row(D), row(D), row(CW)],
        out_shape=[_sds((S, D)), _sds((S, D)), _sds((S, CW))],
        scratch_shapes=[pltpu.VMEM((B + HALO, CW), F32)],
        compiler_params=_cp(("arbitrary",)),
    )(u0, u0, x, oT, conv_w, conv_b, lng, lnb, beta_c, beta_a_col, wo, g1)


def mlp_fwd_loss(x1, tgt, n2g, sc2, sh2, g2, w1, w2):
    S = x1.shape[0]
    B = BLK
    NB = S // B
    NC = FF // FCH

    def body(x1_ref, t_ref, g_ref, sc_ref, sh_ref, g2_ref, w1_ref, w2_ref, dy_ref, loss_ref, dg2_ref,
             hs, acc):
        i = pl.program_id(0)
        c = pl.program_id(1)

        @pl.when(jnp.logical_and(i == 0, c == 0))
        def _():
            loss_ref[...] = jnp.zeros_like(loss_ref)
            dg2_ref[...] = jnp.zeros_like(dg2_ref)

        @pl.when(c == 0)
        def _():
            hs[...] = _norm_mod(x1_ref[...], g_ref[...], sc_ref[...], sh_ref[...])[2].astype(BF16)
            acc[...] = jnp.zeros_like(acc)

        a = jnp.maximum(_nn(hs[...], w1_ref[...]), 0.0)
        acc[...] = acc[...] + _nn((a * a).astype(BF16), w2_ref[...])

        @pl.when(c == NC - 1)
        def _():
            y2 = acc[...]
            e = x1_ref[...] + g2_ref[...] * y2 - t_ref[...]
            dy = e * (1.0 / D)
            dy_ref[...] = dy
            loss_ref[...] = loss_ref[...] + 0.5 * _colsum(_rowmean(e * e))
            dg2_ref[...] = dg2_ref[...] + _colsum(dy * y2)

    row = pl.BlockSpec((B, D), lambda i, c: (i, 0))
    vec = pl.BlockSpec((1, D), lambda i, c: (0, 0))
    return _pc(
        body, name="mlp_fwd_loss", grid=(NB, NC),
        in_specs=[row, row, vec, vec, vec, vec, pl.BlockSpec((D, FCH), lambda i, c: (0, c)),
                  pl.BlockSpec((FCH, D), lambda i, c: (c, 0))],
        out_specs=[row, pl.BlockSpec((1, 1), lambda i, c: (0, 0)), vec],
        out_shape=[_sds((S, D)), _sds((1, 1)), _sds((1, D))],
        scratch_shapes=[pltpu.VMEM((B, D), BF16), pltpu.VMEM((B, D), F32)],
        compiler_params=_cp(("arbitrary", "arbitrary")),
    )(x1, tgt, n2g, sc2, sh2, g2, w1, w2)


def mlp_bwd(x1, dy, n2g, sc2, sh2, g2, w1, w2):
    S = x1.shape[0]
    B = BLK
    NB = S // B
    NC = FF // FCH

    def body(x1_ref, dy_ref, g_ref, sc_ref, sh_ref, g2_ref, w1_ref, w2_ref, dw1_ref, dw2t_ref, dh_ref):
        i = pl.program_id(1)

        @pl.when(i == 0)
        def _():
            dw1_ref[...] = jnp.zeros_like(dw1_ref)
            dw2t_ref[...] = jnp.zeros_like(dw2t_ref)

        h2 = _norm_mod(x1_ref[...], g_ref[...], sc_ref[...], sh_ref[...])[2]
        w1c = w1_ref[...]
        w2c = w2_ref[...]
        ra = jnp.maximum(_nn(h2.astype(BF16), w1c), 0.0)
        dy2 = dy_ref[...] * g2_ref[...]
        db = _nt(dy2.astype(BF16), w2c)
        dab = (db * (2.0 * ra)).astype(BF16)
        dh_ref[0] = _nt(dab, w1c)
        dw1_ref[...] = dw1_ref[...] + _nn(h2.T.astype(BF16), dab)
        dw2t_ref[...] = dw2t_ref[...] + _nn(dy2.T.astype(BF16), (ra * ra).astype(BF16))

    row = pl.BlockSpec((B, D), lambda c, i: (i, 0))
    vec = pl.BlockSpec((1, D), lambda c, i: (0, 0))
    wcol = pl.BlockSpec((D, FCH), lambda c, i: (0, c))
    return _pc(
        body, name="mlp_bwd", grid=(NC, NB),
        in_specs=[row, row, vec, vec, vec, vec, wcol, pl.BlockSpec((FCH, D), lambda c, i: (c, 0))],
        out_specs=[wcol, wcol, pl.BlockSpec((1, B, D), lambda c, i: (c, i, 0))],
        out_shape=[_sds((D, FF)), _sds((D, FF)), _sds((NC, S, D))],
        compiler_params=_cp(("arbitrary", "arbitrary")),
    )(x1, dy, n2g, sc2, sh2, g2, w1, w2)


def merge_bwd(dh2p, x1, dy, o, oT, u1, n2g, sc2, g1, lng, lnb, beta_c, beta_a_col, wo):
    S = x1.shape[0]
    B = BLK
    NB = S // B
    NC = dh2p.shape[0]

    def body(dh_ref, x1_ref, dy_ref, o_ref, oT_ref, u1_ref, g_ref, sc_ref, g1_ref, lng_ref, lnb_ref,
             bc_ref, ba_ref, wo_ref,
             dx1_ref, doTa_ref, dl_ref, du1_ref, dwo_ref, v1_ref, v2_ref, dba_ref):
        i = pl.program_id(0)

        @pl.when(i == 0)
        def _():
            dwo_ref[...] = jnp.zeros_like(dwo_ref)
            v1_ref[...] = jnp.zeros_like(v1_ref)
            v2_ref[...] = jnp.zeros_like(v2_ref)
            dba_ref[...] = jnp.zeros_like(dba_ref)

        dh2 = dh_ref[0]
        for cc in range(1, NC):
            dh2 = dh2 + dh_ref[cc]
        x1 = x1_ref[...]
        g = g_ref[...]
        sc = sc_ref[...]
        r2 = lax.rsqrt(_rowmean(x1 * x1) + EPS)
        xh = x1 * r2
        dhx = dh2 * xh
        v1_ref[0:1, :] = v1_ref[0:1, :] + _colsum(dh2)
        v1_ref[1:2, :] = v1_ref[1:2, :] + _colsum(dhx) * g
        v1_ref[2:3, :] = v1_ref[2:3, :] + _colsum(dhx) * (1.0 + sc)
        dxh = dh2 * (g * (1.0 + sc))
        dx1 = dy_ref[...] + r2 * (dxh - xh * _rowmean(dxh * xh))
        dx1_ref[...] = dx1
        v1_ref[3:4, :] = v1_ref[3:4, :] + _colsum(dx1 * o_ref[...])
        dob = (dx1 * g1_ref[...]).astype(BF16)

        lng = lng_ref[...]
        bc = bc_ref[...]
        rstd, xhat, u2, sg, rc, n3, mc = _conv_tail(u1_ref[...], lng, lnb_ref[...], bc)
        ba = ba_ref[...]
        oT = oT_ref[0]
        ra, ohat, maT = _attn_tail(oT, ba)
        dwo_ref[0:A, :] = dwo_ref[0:A, :] + _nn(maT.astype(BF16), dob)
        dwo_ref[A:D, :] = dwo_ref[A:D, :] + _nn(mc.T.astype(BF16), dob)
        dmaT = _nt(wo_ref[0:A, :], dob)
        dmc = _nt(dob, wo_ref[A:D, :])

        dba_ref[...] = dba_ref[...] + _lanesum(dmaT * ohat)
        dohat = dmaT * ba
        doT = ra * (dohat - ohat * jnp.mean(dohat * ohat, axis=0, keepdims=True))
        prod = doT * oT
        zpad = jnp.zeros((128 - DH, B), BF16)
        for hh in range(H):
            sl = slice(hh * DH, (hh + 1) * DH)
            dl_ref[hh, 0] = _colsum(prod[sl])
            doTa_ref[0, hh * 128:hh * 128 + DH, :] = doT[sl].astype(BF16)
            doTa_ref[0, hh * 128 + DH:(hh + 1) * 128, :] = zpad

        v2_ref[0:1, :] = v2_ref[0:1, :] + _colsum(dmc * n3)
        dn3 = dmc * bc
        du3 = rc * (dn3 - n3 * _rowmean(dn3 * n3))
        du2 = du3 * (sg * (1.0 + u2 * (1.0 - sg)))
        v2_ref[1:2, :] = v2_ref[1:2, :] + _colsum(du2 * xhat)
        v2_ref[2:3, :] = v2_ref[2:3, :] + _colsum(du2)
        dxhat = du2 * lng
        du1_ref[...] = rstd * (dxhat - _rowmean(dxhat) - xhat * _rowmean(dxhat * xhat))

    row = lambda w: pl.BlockSpec((B, w), lambda i: (i, 0))
    return _pc(
        body, name="merge_bwd", grid=(NB,),
        in_specs=[pl.BlockSpec((NC, B, D), lambda i: (0, i, 0)), row(D), row(D), row(D),
                  pl.BlockSpec((1, A, B), lambda i: (i, 0, 0)), row(CW), _const((1, D)), _const((1, D)),
                  _const((1, D)), _const((1, CW)), _const((1, CW)), _const((1, CW)), _const((A, 1)),
                  _const((D, D))],
        out_specs=[row(D), pl.BlockSpec((1, H * 128, B), lambda i: (i, 0, 0)),
                   pl.BlockSpec((H, 1, 1, B), lambda i: (0, i, 0, 0)), row(CW), _const((D, D)),
                   _const((8, D)), _const((8, CW)), _const((A, 1))],
        out_shape=[_sds((S, D)), _sds((NB, H * 128, B), BF16), _sds((H, NB, 1, B)), _sds((S, CW)),
                   _sds((D, D)), _sds((8, D)), _sds((8, CW)), _sds((A, 1))],
        compiler_params=_cp(("arbitrary",)),
    )(dh2p, x1, dy, o, oT, u1, n2g, sc2, g1, lng, lnb, beta_c, beta_a_col, wo)


def head_bwd(du1, u0, alg, zqk, fgT, dqT, dkT, dvT, dfk, conv_w, qg_col, kg_col, bf_col, tri_lo):
    S = u0.shape[0]
    B = BLK
    NB = S // B

    def body(dc_ref, dn_ref, uc_ref, up_ref, alg_ref, zqk_ref, fg_ref, dq_ref, dk_ref, dv_ref, dfk_ref,
             w_ref, qg_ref, kg_ref, bf_ref, tri_ref,
             dzr_ref, dzT_ref, dcw_ref, vc_ref, dqg_ref, dkg_ref, dbf_ref,
             dbuf, ubuf, carry, fbuf):
        pid = pl.program_id(0)
        ri = NB - 1 - pid

        @pl.when(pid == 0)
        def _():
            carry[...] = jnp.zeros_like(carry)
            dcw_ref[...] = jnp.zeros_like(dcw_ref)
            vc_ref[...] = jnp.zeros_like(vc_ref)
            dqg_ref[...] = jnp.zeros_like(dqg_ref)
            dkg_ref[...] = jnp.zeros_like(dkg_ref)
            dbf_ref[...] = jnp.zeros_like(dbf_ref)

        du1 = dc_ref[...]
        dbuf[0:B, :] = du1
        dbuf[B:B + HALO, :] = jnp.where(ri < NB - 1, dn_ref[0:HALO, :], 0.0)
        ubuf[0:HALO, :] = jnp.where(ri > 0, up_ref[B - HALO:B, :], 0.0)
        ubuf[HALO:HALO + B, :] = uc_ref[...]
        du0 = jnp.zeros((B, CW), F32)
        for k in range(KC):
            du0 = du0 + w_ref[k:k + 1, :] * dbuf[pl.ds(KC - 1 - k, B), :]
            dcw_ref[k:k + 1, :] = dcw_ref[k:k + 1, :] + _colsum(
                du1 * ubuf[pl.ds(HALO - (KC - 1) + k, B), :])
        vc_ref[0:1, :] = vc_ref[0:1, :] + _colsum(du1)

        al = alg_ref[:, 0:CW]
        sg = _sigmoid(alg_ref[:, CW:2 * CW])
        dzr_ref[:, 0:CW] = (du0 * sg).astype(BF16)
        dzr_ref[:, CW:2 * CW] = (du0 * al * sg * (1.0 - sg)).astype(BF16)

        dqg = jnp.zeros((DH, B), F32)
        dkg = jnp.zeros((DH, B), F32)
        qg = qg_ref[...]
        kg = kg_ref[...]
        for hh in range(H):
            sl = slice(hh * DH, (hh + 1) * DH)
            q = zqk_ref[sl, :]
            rq = lax.rsqrt(jnp.mean(q * q, axis=0, keepdims=True) + EPS)
            qn = q * rq
            dqh = dq_ref[0, hh * KR:hh * KR + DH, :] * 0.125
            dqg = dqg + dqh * qn
            dqn = dqh * qg
            dzT_ref[0, sl, :] = (rq * (dqn - qn * jnp.mean(dqn * qn, axis=0, keepdims=True))).astype(BF16)
            k = zqk_ref[A + hh * DH:A + (hh + 1) * DH, :]
            rk = lax.rsqrt(jnp.mean(k * k, axis=0, keepdims=True) + EPS)
            kn = k * rk
            dkh = dk_ref[0, sl, :]
            dkg = dkg + dkh * kn
            dkn = dkh * kg
            dzT_ref[0, A + hh * DH:A + (hh + 1) * DH, :] = (
                rk * (dkn - kn * jnp.mean(dkn * kn, axis=0, keepdims=True))).astype(BF16)
            fbuf[hh:hh + 1, :] = dq_ref[0, hh * KR + DH:hh * KR + DH + 1, :] - dfk_ref[hh, 0, 0:1, :]
        dqg_ref[...] = dqg_ref[...] + _lanesum(dqg)
        dkg_ref[...] = dkg_ref[...] + _lanesum(dkg)
        dzT_ref[0, 2 * A:3 * A, :] = dv_ref[0].astype(BF16)

        dF = fbuf[...]
        a1, a2, a3 = _split3(dF)
        tr = tri_ref[...]
        dlogf = _nn(a1, tr) + _nn(a2, tr) + _nn(a3, tr) + carry[...]
        carry[...] = carry[...] + _lanesum(dF)
        dfg = dlogf * _sigmoid(-(fg_ref[...] + bf_ref[...]))
        dbf_ref[...] = dbf_ref[...] + _lanesum(dfg)
        dzT_ref[0, 3 * A:ZT_ROWS, :] = jnp.concatenate([dfg, jnp.zeros((8, B), F32)], axis=0).astype(BF16)

    rrow = lambda w: pl.BlockSpec((B, w), lambda p: (NB - 1 - p, 0))
    rts = lambda r: pl.BlockSpec((1, r, B), lambda p: (NB - 1 - p, 0, 0))
    return _pc(
        body, name="head_bwd", grid=(NB,),
        in_specs=[rrow(CW), pl.BlockSpec((B, CW), lambda p: (jnp.minimum(NB - p, NB - 1), 0)),
                  rrow(CW), pl.BlockSpec((B, CW), lambda p: (jnp.maximum(NB - 2 - p, 0), 0)),
                  rrow(2 * CW), pl.BlockSpec((2 * A, B), lambda p: (0, NB - 1 - p)),
                  pl.BlockSpec((8, B), lambda p: (0, NB - 1 - p)), rts(H * KR), rts(A), rts(A),
                  pl.BlockSpec((H, 1, 8, B), lambda p: (0, NB - 1 - p, 0, 0)),
                  _const((32, CW)), _const((DH, 1)), _const((DH, 1)), _const((8, 1)), _const((B, B))],
        out_specs=[rrow(2 * CW), rts(ZT_ROWS), _const((32, CW)), _const((8, CW)), _const((DH, 1)),
                   _const((DH, 1)), _const((8, 1))],
        out_shape=[_sds((S, 2 * CW), BF16), _sds((NB, ZT_ROWS, B), BF16), _sds((32, CW)), _sds((8, CW)),
                   _sds((DH, 1)), _sds((DH, 1)), _sds((8, 1))],
        scratch_shapes=[pltpu.VMEM((B + HALO, CW), F32), pltpu.VMEM((B + HALO, CW), F32),
                        pltpu.VMEM((8, 1), F32), pltpu.VMEM((8, B), F32)],
        compiler_params=_cp(("arbitrary",)),
    )(du1, du1, u0, u0, alg, zqk, fgT, dqT, dkT, dvT, dfk, conv_w, qg_col, kg_col, bf_col, tri_lo)


def in_bwd(x, dx1, dzr, dzT, n1g, sc1, sh1, wT, w_qkvf, wr):
    S = x.shape[0]
    B = BLK
    NB = S // B

    def body(x_ref, dx1_ref, dzr_ref, dzT_ref, g_ref, sc_ref, sh_ref, wT_hbm, w_hbm, wr_hbm,
             gx_ref, dwT_hbm, dwr_hbm, v_ref, wT_v, w_v, wr_v, dwT_acc, dwr_acc, sem):
        i = pl.program_id(0)

        @pl.when(i == 0)
        def _():
            for src, dst in ((wT_hbm, wT_v), (w_hbm, w_v), (wr_hbm, wr_v)):
                cp = pltpu.make_async_copy(src, dst, sem)
                cp.start()
                cp.wait()
            dwT_acc[...] = jnp.zeros_like(dwT_acc)
            dwr_acc[...] = jnp.zeros_like(dwr_acc)
            v_ref[...] = jnp.zeros_like(v_ref)

        g = g_ref[...]
        sc = sc_ref[...]
        r1, xh, h1 = _norm_mod(x_ref[...], g, sc, sh_ref[...])
        hb = h1.astype(BF16)
        dzr = dzr_ref[...]
        dzT = dzT_ref[0]
        dwT_acc[...] = dwT_acc[...] + _nn(dzT, hb)
        dwr_acc[...] = dwr_acc[...] + _nn(h1.T.astype(BF16), dzr)
        dh1 = _nt(dzr, wr_v[...]) + _nn(w_v[...], dzT).T
        dhx = dh1 * xh
        v_ref[0:1, :] = v_ref[0:1, :] + _colsum(dh1)
        v_ref[1:2, :] = v_ref[1:2, :] + _colsum(dhx) * g
        v_ref[2:3, :] = v_ref[2:3, :] + _colsum(dhx) * (1.0 + sc)
        dxh = dh1 * (g * (1.0 + sc))
        gx_ref[...] = dx1_ref[...] + r1 * (dxh - xh * _rowmean(dxh * xh))

        @pl.when(i == NB - 1)
        def _():
            for src, dst in ((dwT_acc, dwT_hbm), (dwr_acc, dwr_hbm)):
                cp = pltpu.make_async_copy(src, dst, sem)
                cp.start()
                cp.wait()

    row = lambda w: pl.BlockSpec((B, w), lambda i: (i, 0))
    return _pc(
        body, name="in_bwd", grid=(NB,),
        in_specs=[row(D), row(D), row(2 * CW), pl.BlockSpec((1, ZT_ROWS, B), lambda i: (i, 0, 0)),
                  _const((1, D)), _const((1, D)), _const((1, D)), ANY, ANY, ANY],
        out_specs=[row(D), ANY, ANY, _const((8, D))],
        out_shape=[_sds((S, D)), _sds((ZT_ROWS, D)), _sds((D, 2 * CW)), _sds((8, D))],
        scratch_shapes=[pltpu.VMEM((ZT_ROWS, D), BF16), pltpu.VMEM((D, ZT_ROWS), BF16),
                        pltpu.VMEM((D, 2 * CW), BF16), pltpu.VMEM((ZT_ROWS, D), F32),
                        pltpu.VMEM((D, 2 * CW), F32), pltpu.SemaphoreType.DMA(())],
        compiler_params=_cp(("arbitrary",)),
    )(x, dx1, dzr, dzT, n1g, sc1, sh1, wT, w_qkvf, wr)


def _adam_math(g, w, m, v):
    m = ADAM_B1 * m + (1.0 - ADAM_B1) * g
    v = ADAM_B2 * v + (1.0 - ADAM_B2) * (g * g)
    m_hat = m / (1.0 - ADAM_B1 ** ADAM_STEP)
    v_hat = v / (1.0 - ADAM_B2 ** ADAM_STEP)
    delta = -ADAM_LR * (m_hat / (jnp.sqrt(v_hat) + ADAM_EPS) + ADAM_WD * w)
    return delta, m, v


def _row_tile(R):
    for t in (1024, 512, 256, 128, 64, 32, 16, 8):
        if R % t == 0:
            return t
    return R


def sum_slots(parts, name):
    K, R, C = parts.shape
    T = _row_tile(R)

    def body(p_ref, o_ref):
        s = p_ref[0]
        for k in range(1, K):
            s = s + p_ref[k]
        o_ref[...] = s

    return _pc(body, name=name, grid=(R // T,),
               in_specs=[pl.BlockSpec((K, T, C), lambda i: (0, i, 0))],
               out_specs=pl.BlockSpec((T, C), lambda i: (i, 0)), out_shape=_sds((R, C)),
               compiler_params=_cp(("arbitrary",)))(parts)


def adamw_slots(parts, w, m, v, name):
    K, R, C = parts.shape
    T = _row_tile(R)

    def body(p_ref, w_ref, m_ref, v_ref, g_ref, d_ref, nm_ref, nv_ref):
        g = p_ref[0]
        for k in range(1, K):
            g = g + p_ref[k]
        g_ref[...] = g
        d_ref[...], nm_ref[...], nv_ref[...] = _adam_math(g, w_ref[...], m_ref[...], v_ref[...])

    t2 = pl.BlockSpec((T, C), lambda i: (i, 0))
    return _pc(body, name=name, grid=(R // T,),
               in_specs=[pl.BlockSpec((K, T, C), lambda i: (0, i, 0)), t2, t2, t2],
               out_specs=[t2, t2, t2, t2], out_shape=[_sds((R, C))] * 4,
               compiler_params=_cp(("arbitrary",)))(parts, w, m, v)


def ada_grad_adamw(cT, dmod_cols, w, m, v):
    NCOL = w.shape[1]

    def body(cT_ref, dm_ref, w_ref, m_ref, v_ref, g_ref, d_ref, nm_ref, nv_ref):
        def term(b):
            cv = cT_ref[b]
            return (cv * _sigmoid(cv)) * dm_ref[b:b + 1, :]

        g = term(0)
        for b in range(1, N_DEV):
            g = g + term(b)
        g_ref[...] = g
        d_ref[...], nm_ref[...], nv_ref[...] = _adam_math(g, w_ref[...], m_ref[...], v_ref[...])

    full = _const((D, NCOL))
    return _pc(body, name="ada_grad_adamw", grid=(1,),
               in_specs=[_const((N_DEV, D, 1)), _const((N_DEV, NCOL)), full, full, full],
               out_specs=[full, full, full, full], out_shape=[_sds((D, NCOL))] * 4,
               compiler_params=_cp())(cT, dmod_cols, w, m, v)


def _pack_rows(vecs, rows=None):
    flat = jnp.concatenate([jnp.ravel(v) for v in vecs])
    n = flat.shape[0]
    r = -(-n // 1024) * 8 if rows is None else rows
    return jnp.pad(flat, (0, r * 128 - n)).reshape(r, 128)


def _unpack_rows(packed, shapes):
    flat = packed.reshape(-1)
    out, off = [], 0
    for s in shapes:
        n = 1
        for d in s:
            n *= d
        out.append(flat[off:off + n].reshape(s))
        off += n
    return out


def _cols_from_shards(g, rows, cols):
    return g.reshape(N_DEV, rows, cols).transpose(1, 0, 2).reshape(rows, N_DEV * cols)


def _cols_to_shards(w, cols):
    rows = w.shape[0]
    return w.reshape(rows, N_DEV, cols).transpose(1, 0, 2).reshape(N_DEV, rows * cols)


def kernel(x, c, w_ada, b_ada, norm1_g, w_in, q_norm_g, k_norm_g, b_f, conv_w, conv_b, conv_ln_g, conv_ln_b, beta_attn, beta_conv, w_out, norm2_g, w_ff1, w_ff2, loss_target, m_w_ada, m_b_ada, m_norm1_g, m_w_in, m_q_norm_g, m_k_norm_g, m_b_f, m_conv_w, m_conv_b, m_conv_ln_g, m_conv_ln_b, m_beta_attn, m_beta_conv, m_w_out, m_norm2_g, m_w_ff1, m_w_ff2, v_w_ada, v_b_ada, v_norm1_g, v_w_in, v_q_norm_g, v_k_norm_g, v_b_f, v_conv_w, v_conv_b, v_conv_ln_g, v_conv_ln_b, v_beta_attn, v_beta_conv, v_w_out, v_norm2_g, v_w_ff1, v_w_ff2):
    S = x.shape[1]
    B = BLK
    NB = S // B
    me = 4 * lax.axis_index("x") + 2 * lax.axis_index("y") + lax.axis_index("c")
    xs = x[0]
    tgt = loss_target[0]

    ADA_C, IN_C, FF_C, CV_C = w_ada.shape[2], w_in.shape[2], w_ff1.shape[2], conv_w.shape[2]
    OUT_R, FF_R = w_out.shape[1], w_ff2.shape[1]
    IN_W = N_DEV * IN_C

    n_ada, n_in, n_out, n_f1, n_f2 = D * ADA_C, D * IN_C, OUT_R * D, D * FF_C, FF_R * D
    big = jnp.concatenate([w_ada.reshape(-1), w_in.reshape(-1), w_out.reshape(-1), w_ff1.reshape(-1),
                           w_ff2.reshape(-1)]).astype(BF16)
    rows_big = -(-big.shape[0] // (128 * 16)) * 16
    big = jnp.pad(big, (0, rows_big * 128 - big.shape[0])).reshape(rows_big, 128)
    gw = all_gather(big, "ag_weights").reshape(N_DEV, rows_big * 128)
    small0 = _pack_rows([c, conv_w], rows=24)
    gs0 = all_gather(small0, "ag_small_fwd").reshape(N_DEV, 24 * 128)

    off = 0
    w_ada_full = _cols_from_shards(gw[:, off:off + n_ada], D, ADA_C); off += n_ada
    w_in_full = _cols_from_shards(gw[:, off:off + n_in], D, IN_C); off += n_in
    wo_full = gw[:, off:off + n_out].reshape(D, D); off += n_out
    w1_full = _cols_from_shards(gw[:, off:off + n_f1], D, FF_C); off += n_f1
    w2_full = gw[:, off:off + n_f2].reshape(FF, D)
    c_all = gs0[:, 0:D]
    conv_w_full = _cols_from_shards(gs0[:, D:D + KC * CV_C], KC, CV_C)
    conv_w_pad = jnp.pad(conv_w_full, ((0, 32 - KC), (0, 0)))

    w_qkvf = jnp.pad(w_in_full[:, 0:3 * A + H], ((0, 0), (0, ZT_ROWS - 3 * A - H)))
    wT = w_qkvf.T
    wr = w_in_full[:, 3 * A + H:]

    qg_col = q_norm_g.reshape(DH, 1)
    kg_col = k_norm_g.reshape(DH, 1)
    bf_col = b_f.reshape(H, 1)
    beta_a_col = beta_attn.reshape(A, 1)
    ii = lax.broadcasted_iota(jnp.int32, (B, B), 0)
    jj = lax.broadcasted_iota(jnp.int32, (B, B), 1)
    tri_up = (ii <= jj).astype(BF16)
    tri_lo = (ii >= jj).astype(BF16)

    mod = mod_kernel(c, w_ada_full, b_ada)
    sh1, sc1, g1, sh2, sc2, g2 = [mod[:, k * D:(k + 1) * D] for k in range(6)]
    qTa, kT, kaug, vT, vaug, zqk, fgT, alg, u0, stat = fwd_in(xs, norm1_g, sc1, sh1, wT, wr, qg_col,
                                                             kg_col, bf_col, tri_up)
    fmax = stat[:, :, 0].T
    fmin = stat[:, :, 1].T
    qk_max = jnp.sqrt(jnp.max(stat[:, :, 2], axis=0) * jnp.max(stat[:, :, 3], axis=0))
    thr = -(PRUNE + 2.02 * qk_max)
    oT, lse = attn_fwd(fmax, fmin, thr, qTa, kaug, vT)
    x1, o, u1 = conv_merge_out(u0, xs, oT, conv_w_pad, conv_b, conv_ln_g, conv_ln_b, beta_conv,
                               beta_a_col, wo_full, g1)
    dy, loss_part, dg2 = mlp_fwd_loss(x1, tgt, norm2_g, sc2, sh2, g2, w1_full, w2_full)

    dw1, dw2t, dh2p = mlp_bwd(x1, dy, norm2_g, sc2, sh2, g2, w1_full, w2_full)
    dx1, doTa, delta, du1, dwo, v1, v2, dba = merge_bwd(dh2p, x1, dy, o, oT, u1, norm2_g, sc2, g1,
                                                         conv_ln_g, conv_ln_b, beta_conv, beta_a_col, wo_full)
    dqT, dkT, dvT, dfk = attn_bwd(fmax, fmin, thr, qTa, kaug, kT, vaug, doTa, lse, delta)
    dzr, dzT, dcw, vc, dqg, dkg, dbf = head_bwd(du1, u0, alg, zqk, fgT, dqT, dkT, dvT, dfk, conv_w_pad,
                                                qg_col, kg_col, bf_col, tri_lo)
    grad_x, dwT, dwr, v0 = in_bwd(xs, dx1, dzr, dzT, norm1_g, sc1, sh1, wT, w_qkvf, wr)

    dw_in = jnp.concatenate([dwT.T[:, 0:3 * A + H], dwr], axis=1)
    gbig = jnp.concatenate([_cols_to_shards(dw_in, IN_C), dwo.reshape(N_DEV, n_out),
                            _cols_to_shards(dw1, FF_C), dw2t.T.reshape(N_DEV, n_f2)], axis=1)
    n_g = gbig.shape[1]
    rows_g = -(-n_g // (128 * 512)) * 512
    gbig = jnp.pad(gbig, ((0, 0), (0, rows_g * 128 - n_g)))
    recv =reduce_scatter_exchange(gbig.reshape(N_DEV, rows_g, 128), "rs_grads")

    dmod = jnp.concatenate([v0[0], v0[1], v1[3], v1[0], v1[1], dg2[0]])
    small1 = _pack_rows([dmod, v0[2], v1[2], dcw, vc[0], v2[1], v2[2], v2[0], dba, dqg, dkg,
                         jnp.pad(dbf.reshape(-1), (0, 120)), jnp.pad(loss_part.reshape(-1), (0, 127))])
    gs1 = all_gather(small1, "ag_small_bwd")
    tot = sum_slots(gs1, "sum_small")
    (g_b_ada, g_n1, g_n2, g_cw, g_cb, g_lng, g_lnb, g_bc, g_ba, g_qg, g_kg, g_bf, loss_v) = _unpack_rows(
        tot, [(1, 6 * D), (1, D), (1, D), (32, CW), (1, CW), (1, CW), (1, CW), (1, CW), (1, A),
              (1, DH), (1, DH), (1, 128), (1, 128)])
    loss = loss_v[0, 0]
    g_bf = g_bf[:, 0:H]
    g_cw_mine = lax.dynamic_slice(g_cw[0:KC], (0, me * CV_C), (KC, CV_C)).reshape(1, KC, CV_C)

    small_names = [(b_ada, m_b_ada, v_b_ada, g_b_ada), (norm1_g, m_norm1_g, v_norm1_g, g_n1),
                   (q_norm_g, m_q_norm_g, v_q_norm_g, g_qg), (k_norm_g, m_k_norm_g, v_k_norm_g, g_kg),
                   (b_f, m_b_f, v_b_f, g_bf), (conv_w, m_conv_w, v_conv_w, g_cw_mine),
                   (conv_b, m_conv_b, v_conv_b, g_cb), (conv_ln_g, m_conv_ln_g, v_conv_ln_g, g_lng),
                   (conv_ln_b, m_conv_ln_b, v_conv_ln_b, g_lnb), (beta_attn, m_beta_attn, v_beta_attn, g_ba),
                   (beta_conv, m_beta_conv, v_beta_conv, g_bc), (norm2_g, m_norm2_g, v_norm2_g, g_n2)]
    shapes_s = [t[0].shape for t in small_names]
    pw, pm, pv, pg = [_pack_rows([t[k] for t in small_names]) for k in range(4)]
    sg_, sd_, sm_, sv_ = adamw_slots(pg[None], pw, pm, pv, "adamw_small")
    sgs, sds, sms, svs = [_unpack_rows(a, shapes_s) for a in (sg_, sd_, sm_, sv_)]

    dmod_all = gs1[:, 0:48, :].reshape(N_DEV, N_DEV, ADA_C)
    dmod_cols = lax.dynamic_index_in_dim(dmod_all, me, axis=1, keepdims=False)
    ga, da, ma_, va_ = ada_grad_adamw(c_all.reshape(N_DEV, D, 1), dmod_cols, w_ada[0], m_w_ada[0], v_w_ada[0])

    big_shapes = [(1, D, IN_C), (1, OUT_R, D), (1, D, FF_C), (1, FF_R, D)]
    pack_big = lambda ws: _pack_rows(ws, rows=rows_g)
    bw = pack_big([w_in, w_out, w_ff1, w_ff2])
    bm = pack_big([m_w_in, m_w_out, m_w_ff1, m_w_ff2])
    bv = pack_big([v_w_in, v_w_out, v_w_ff1, v_w_ff2])
    bg_, bd_, bm_, bv_ = adamw_slots(recv, bw, bm, bv, "adamw_big")
    bgs, bds, bms, bvs = [_unpack_rows(a, big_shapes) for a in (bg_, bd_, bm_, bv_)]

    def assemble(small, ada, bigs):
        (b_ada_, n1_, qg_, kg_, bf_, cw_, cb_, lng_, lnb_, ba_, bc_, n2_) = small
        return [ada.reshape(1, D, ADA_C), b_ada_, n1_, bigs[0], qg_, kg_, bf_, cw_, cb_, lng_, lnb_, ba_, bc_,
                bigs[1], n2_, bigs[2], bigs[3]]

    return (loss, grad_x.reshape(1, S, D), *assemble(sgs, ga, bgs), *assemble(sds, da, bds),
            *assemble(sms, ma_, bms), *assemble(svs, va_, bvs))
```

```python
import functools

import jax
import jax.numpy as jnp
from jax import lax
from jax.experimental import pallas as pl
from jax.experimental.pallas import tpu as pltpu

F32 = jnp.float32
BF16 = jnp.bfloat16

D = 1024
A = 512
CW = 512
H = 8
DH = 64
FF = 4096
KC = 31
HALO = 32
KR = 80
ZT_ROWS = 1552
FCH = 1024
EPS = 1e-6
BLK = 512
N_DEV = 8
VMEM_LIMIT = 56 * 1024 * 1024

ADAM_LR = 0.001
ADAM_B1 = 0.9
ADAM_B2 = 0.999
ADAM_EPS = 1e-08
ADAM_WD = 0.01
ADAM_STEP = 10

MESH = pl.DeviceIdType.MESH
ANY = pl.BlockSpec(memory_space=pl.ANY)
SMEM = pl.BlockSpec(memory_space=pltpu.SMEM)
PRUNE = 110.0
LOG2E = 1.4426950408889634


def _pc(body, **kw):
    return pl.pallas_call(body, **kw)


def _cp(sem=None):
    return pltpu.CompilerParams(dimension_semantics=sem, vmem_limit_bytes=VMEM_LIMIT)


def _sds(shape, dtype=F32):
    return jax.ShapeDtypeStruct(shape, dtype)


def _const(shape):
    n = len(shape)
    return pl.BlockSpec(shape, lambda *a: (0,) * n)


def _nt(a, b):
    return lax.dot_general(a, b, (((1,), (1,)), ((), ())), preferred_element_type=F32)


def _nn(a, b):
    return jnp.dot(a, b, preferred_element_type=F32)


def _sigmoid(v):
    return 1.0 / (1.0 + jnp.exp(-v))


def _split3(v):
    a1 = v.astype(BF16)
    r1 = v - a1.astype(F32)
    a2 = r1.astype(BF16)
    a3 = (r1 - a2.astype(F32)).astype(BF16)
    return a1, a2, a3


def _rowmean(v):
    return jnp.mean(v, axis=-1, keepdims=True)


def _colsum(v):
    return jnp.sum(v, axis=0, keepdims=True)


def _lanesum(v):
    return jnp.sum(v, axis=-1, keepdims=True)


def _coords():
    return lax.axis_index("x"), lax.axis_index("y"), lax.axis_index("c")


def all_gather(x, name):
    R, C = x.shape

    def body(x_ref, out_ref, send_sems, recv_sems, local_sem):
        mx, my, mc = _coords()
        me, sibling = (mx, my, mc), (mx, my, 1 - mc)
        chips = [(1 - mx, my), (mx, 1 - my), (1 - mx, 1 - my)]

        def rows(px, py, pc):
            return out_ref.at[4 * px + 2 * py + pc]

        def copy(k, block, to, src=None):
            return pltpu.make_async_remote_copy(
                src_ref=rows(*block) if src is None else src, dst_ref=rows(*block),
                send_sem=send_sems.at[k], recv_sem=recv_sems.at[k],
                device_id=to, device_id_type=MESH)

        mine = pltpu.make_async_copy(x_ref, rows(*me), local_sem)
        mine.start()
        first = [copy(0, me, sibling, src=x_ref)]
        first += [copy(1 + j, me, (*chip, mc), src=x_ref) for j, chip in enumerate(chips)]
        for cp in first:
            cp.start()
        passed = [copy(4 + j, (*chip, mc), sibling) for j, chip in enumerate(chips)]
        for j, chip in enumerate(chips):
            copy(1 + j, (*chip, mc), me).wait_recv()
            passed[j].start()
        copy(0, sibling, me).wait_recv()
        for j, chip in enumerate(chips):
            copy(4 + j, (*chip, 1 - mc), me).wait_recv()
        for cp in first + passed:
            cp.wait_send()
        mine.wait()

    return _pc(
        body, name=name, out_shape=_sds((N_DEV, R, C), x.dtype),
        in_specs=[ANY], out_specs=ANY,
        scratch_shapes=[pltpu.SemaphoreType.DMA((7,)), pltpu.SemaphoreType.DMA((7,)),
                        pltpu.SemaphoreType.DMA(())],
    )(x)


def _direct_copies(scatter, src_ref, dst_ref, send_sems, recv_sems, local_sem):
    mx, my, mc = _coords()
    me = 4 * mx + 2 * my + mc
    if scatter:
        local = pltpu.make_async_copy(src_ref.at[me], dst_ref.at[0], local_sem)
    else:
        local = pltpu.make_async_copy(src_ref, dst_ref.at[me], local_sem)
    remote = []
    for r in range(1, N_DEV):
        px = 1 - mx if r & 4 else mx
        py = 1 - my if r & 2 else my
        pcc = 1 - mc if r & 1 else mc
        remote.append(pltpu.make_async_remote_copy(
            src_ref=src_ref.at[4 * px + 2 * py + pcc] if scatter else src_ref,
            dst_ref=dst_ref.at[r] if scatter else dst_ref.at[me],
            send_sem=send_sems.at[r - 1], recv_sem=recv_sems.at[r - 1],
            device_id=(px, py, pcc), device_id_type=MESH))
    return [local] + remote


def _start_all(copies):
    for cp in copies:
        cp.start()


def _wait_all(copies):
    for cp in copies[1:]:
        cp.wait()
    copies[0].wait()


COMM_SEMS = [pltpu.SemaphoreType.DMA((7,)), pltpu.SemaphoreType.DMA((7,)), pltpu.SemaphoreType.DMA(())]


def reduce_scatter_exchange(g, name):
    def body(g_ref, recv_ref, send_sems, recv_sems, local_sem):
        copies = _direct_copies(True, g_ref, recv_ref, send_sems, recv_sems, local_sem)
        _start_all(copies)
        _wait_all(copies)

    return _pc(body, name=name, out_shape=_sds(g.shape, g.dtype), in_specs=[ANY], out_specs=ANY,
               scratch_shapes=COMM_SEMS)(g)


def mod_kernel(c, w_ada_b, b_ada):
    def body(c_ref, w_ref, b_ref, o_ref):
        cv = c_ref[...]
        sc = jnp.broadcast_to(cv * _sigmoid(cv), (8, D)).astype(BF16)
        o_ref[...] = _nn(sc, w_ref[...])[0:1, :] + b_ref[...]

    return _pc(body, name="mod", out_shape=_sds((1, 6 * D)),
               in_specs=[_const((1, D)), _const((D, 6 * D)), _const((1, 6 * D))],
               out_specs=_const((1, 6 * D)), grid=(1,), compiler_params=_cp())(c, w_ada_b, b_ada)


def _norm_mod(xv, g, sc, sh):
    r = lax.rsqrt(_rowmean(xv * xv) + EPS)
    xh = xv * r
    return r, xh, xh * (g * (1.0 + sc)) + sh


def _log_sigmoid(v):
    e = jnp.exp(-jnp.abs(v))
    l1p = jnp.where(e < 1e-4, e * (1.0 - 0.5 * e), jnp.log(1.0 + e))
    return jnp.minimum(v, 0.0) - l1p


def fwd_in(x, n1g, sc1, sh1, wT, wr, qg_col, kg_col, bf_col, tri):
    S = x.shape[0]
    B = BLK
    NB = S // B

    def body(x_ref, g_ref, sc_ref, sh_ref, wT_ref, wr_ref, qg_ref, kg_ref, bf_ref, tri_ref,
             qTa_ref, kT_ref, kaug_ref, vT_ref, vaug_ref, zqk_ref, fgT_ref, alg_ref, u0_ref, stat_ref,
             carry, tbuf, fs, nq, nk):
        i = pl.program_id(0)

        @pl.when(i == 0)
        def _():
            carry[...] = jnp.zeros_like(carry)

        _, _, h = _norm_mod(x_ref[...], g_ref[...], sc_ref[...], sh_ref[...])
        hb = h.astype(BF16)
        zT = _nt(wT_ref[...], hb)
        zr = _nn(hb, wr_ref[...])
        zqk_ref[...] = zT[0:2 * A]
        fgT = zT[3 * A:3 * A + 8]
        fgT_ref[...] = fgT
        alg_ref[...] = zr
        u0_ref[...] = zr[:, 0:CW] * _sigmoid(zr[:, CW:2 * CW])

        logf = _log_sigmoid(fgT + bf_ref[...])
        a1, a2, a3 = _split3(logf)
        tr = tri_ref[...]
        F = _nn(a1, tr) + _nn(a2, tr) + _nn(a3, tr) + carry[...]
        carry[...] = carry[...] + _lanesum(logf)
        p1, p2, p3 = _split3(F * LOG2E)
        n1, n2, n3 = _split3(F * (-LOG2E))
        for k, v in enumerate((p1, p2, p3, n1, n2, n3)):
            fs[k] = v.astype(F32)

        rowi = lax.broadcasted_iota(jnp.int32, (8, B), 0)
        zeros_tail = jnp.zeros((128 - DH - 8, B), F32)
        ones_row = jnp.where(lax.broadcasted_iota(jnp.int32, (KR - DH, B), 0) == 0, 1.0, 0.0).astype(BF16)
        for hh in range(H):
            sl = slice(hh * DH, (hh + 1) * DH)
            q = zT[sl]
            k = zT[A + hh * DH:A + (hh + 1) * DH]
            v = zT[2 * A + hh * DH:2 * A + (hh + 1) * DH]
            qh = q * lax.rsqrt(jnp.mean(q * q, axis=0, keepdims=True) + EPS) * qg_ref[...] * (0.125 * LOG2E)
            kh = k * lax.rsqrt(jnp.mean(k * k, axis=0, keepdims=True) + EPS) * kg_ref[...]
            frow = [fs[kk, hh:hh + 1, :] for kk in range(6)]
            qx = jnp.where(rowi < 3, 1.0, jnp.where(rowi == 3, frow[0], jnp.where(
                rowi == 4, frow[1], jnp.where(rowi == 5, frow[2], 0.0))))
            kx = jnp.where(rowi == 0, frow[3], jnp.where(rowi == 1, frow[4], jnp.where(
                rowi == 2, frow[5], jnp.where(rowi < 6, 1.0, 0.0))))
            tbuf[0:DH, :] = qh
            tbuf[DH:DH + 8, :] = qx
            tbuf[DH + 8:128, :] = zeros_tail
            qTa_ref[0, hh * 128:(hh + 1) * 128, :] = tbuf[...].astype(BF16)
            tbuf[0:DH, :] = kh
            tbuf[DH:DH + 8, :] = kx
            kaug_ref[hh] = tbuf[...].T.astype(BF16)
            kT_ref[0, hh * KR:hh * KR + DH, :] = kh.astype(BF16)
            kT_ref[0, hh * KR + DH:(hh + 1) * KR, :] = ones_row
            tbuf[0:DH, :] = v
            tbuf[DH:DH + 8, :] = jnp.zeros((8, B), F32)
            vaug_ref[hh] = tbuf[...].T.astype(BF16)
            vT_ref[0, sl, :] = v.astype(BF16)
            nq[hh:hh + 1, :] = jnp.max(_colsum(qh * qh), axis=1, keepdims=True)
            nk[hh:hh + 1, :] = jnp.max(_colsum(kh * kh), axis=1, keepdims=True)

        lane = lax.broadcasted_iota(jnp.int32, (8, 128), 1)
        stat_ref[0] = jnp.where(lane == 0, jnp.max(F, axis=1, keepdims=True), jnp.where(
            lane == 1, jnp.min(F, axis=1, keepdims=True), jnp.where(
                lane == 2, nq[...], jnp.where(lane == 3, nk[...], 0.0))))

    row = lambda w: pl.BlockSpec((B, w), lambda i: (i, 0))
    tsp = lambda r: pl.BlockSpec((1, r, B), lambda i: (i, 0, 0))
    return _pc(
        body, name="fwd_in", grid=(NB,),
        in_specs=[row(D), _const((1, D)), _const((1, D)), _const((1, D)), _const((ZT_ROWS, D)),
                  _const((D, 2 * CW)), _const((DH, 1)), _const((DH, 1)), _const((8, 1)), _const((B, B))],
        out_specs=[tsp(H * 128), tsp(H * KR), pl.BlockSpec((H, B, 128), lambda i: (0, i, 0)), tsp(A),
                   pl.BlockSpec((H, B, 128), lambda i: (0, i, 0)),
                   pl.BlockSpec((2 * A, B), lambda i: (0, i)), pl.BlockSpec((8, B), lambda i: (0, i)),
                   row(2 * CW), row(CW), pl.BlockSpec((1, 8, 128), lambda i: (i, 0, 0))],
        out_shape=[_sds((NB, H * 128, B), BF16), _sds((NB, H * KR, B), BF16), _sds((H, S, 128), BF16),
                   _sds((NB, A, B), BF16), _sds((H, S, 128), BF16), _sds((2 * A, S)), _sds((8, S)),
                   _sds((S, 2 * CW)), _sds((S, CW)), _sds((NB, 8, 128))],
        scratch_shapes=[pltpu.VMEM((8, 1), F32), pltpu.VMEM((128, B), F32), pltpu.VMEM((6, 8, B), F32),
                        pltpu.VMEM((8, 1), F32), pltpu.VMEM((8, 1), F32)],
        compiler_params=_cp(("arbitrary",)),
    )(x, n1g, sc1, sh1, wT, wr, qg_col, kg_col, bf_col, tri)


def _first_key_block(top, fmin_ref, cut, h, i):
    return lax.while_loop(
        lambda j: jnp.logical_and(j > 0, top - fmin_ref[h, jnp.maximum(j - 1, 0)] >= cut),
        lambda j: j - 1, i)


def _causal_keep(B):
    return lax.broadcasted_iota(jnp.int32, (B, B), 0) <= lax.broadcasted_iota(jnp.int32, (B, B), 1)


def attn_fwd(fmax, fmin, thr, qTa, kaug, vT, late_w):
    NB, _, B = qTa.shape
    S = NB * B

    def body(fmax_ref, fmin_ref, thr_ref, q_ref, k_ref, v_ref, w_ref, o_ref, lse_ref, gw_ref,
             s0, s1, m_ref, l_ref, acc_ref, send_sems, recv_sems, local_sem):
        h = pl.program_id(0)
        i = pl.program_id(1)

        @pl.when(jnp.logical_and(h == 0, i == 0))
        def _():
            _start_all(_direct_copies(False, w_ref, gw_ref, send_sems, recv_sems, local_sem))

        jlo = _first_key_block(fmax_ref[h, i], fmin_ref, thr_ref[h], h, i)
        n = i - jlo

        def scores(j, s_ref):
            s_ref[...] = _nn(k_ref[0, pl.ds(pl.multiple_of(j * B, B), B), :], q_ref[0])

        def softmax_step(s_ref, j, masked):
            s = s_ref[...]
            if masked:
                s = jnp.where(_causal_keep(B), s, -jnp.inf)
            m = m_ref[...]
            mn = jnp.maximum(m, jnp.max(s, axis=0, keepdims=True))
            a = jnp.exp2(m - mn)
            p = jnp.exp2(s - mn)
            m_ref[...] = mn
            l_ref[...] = a * l_ref[...] + _colsum(p)
            acc_ref[...] = a * acc_ref[...] + _nn(v_ref[j], p.astype(BF16))

        m_ref[...] = jnp.full((1, B), -jnp.inf, F32)
        l_ref[...] = jnp.zeros((1, B), F32)
        acc_ref[...] = jnp.zeros((DH, B), F32)
        scores(jlo, s0)

        def pair(t, carry):
            j = jlo + 2 * t
            scores(j + 1, s1)
            softmax_step(s0, j, False)
            scores(j + 2, s0)
            softmax_step(s1, j + 1, False)
            return carry

        lax.fori_loop(0, n // 2, pair, 0)

        @pl.when(n % 2 == 1)
        def _():
            scores(i, s1)
            softmax_step(s0, i - 1, False)
            softmax_step(s1, i, True)

        @pl.when(n % 2 == 0)
        def _():
            softmax_step(s0, i, True)

        l = l_ref[...]
        o_ref[0] = acc_ref[...] / l
        lse_ref[0, 0] = m_ref[...] + jnp.log2(l)

        @pl.when(jnp.logical_and(h == H - 1, i == NB - 1))
        def _():
            _wait_all(_direct_copies(False, w_ref, gw_ref, send_sems, recv_sems, local_sem))

    return _pc(
        body, name="attn_fwd", grid=(H, NB),
        in_specs=[SMEM, SMEM, SMEM, pl.BlockSpec((1, 128, B), lambda h, i: (i, h, 0)),
                  pl.BlockSpec((1, S, 128), lambda h, i: (h, 0, 0)),
                  pl.BlockSpec((NB, DH, B), lambda h, i: (0, h, 0)), ANY],
        out_specs=[pl.BlockSpec((1, DH, B), lambda h, i: (i, h, 0)),
                   pl.BlockSpec((1, 1, 1, B), lambda h, i: (h, i, 0, 0)), ANY],
        out_shape=[_sds((NB, A, B)), _sds((H, NB, 1, B)), _sds((N_DEV,) + late_w.shape, late_w.dtype)],
        scratch_shapes=[pltpu.VMEM((B, B), F32), pltpu.VMEM((B, B), F32), pltpu.VMEM((1, B), F32),
                        pltpu.VMEM((1, B), F32), pltpu.VMEM((DH, B), F32)] + COMM_SEMS,
        compiler_params=_cp(("arbitrary", "arbitrary")),
    )(fmax, fmin, thr, qTa, kaug, vT, late_w)


def attn_bwd(fmax, fmin, thr, qTa, kaug, kT, vaug, doTa, lse, delta, early_g):
    NB, _, B = qTa.shape
    QR = 80

    def body(fmax_ref, fmin_ref, thr_ref, q_ref, ka_ref, kt_ref, va_ref, do_ref, lse_ref, dl_ref, g_ref,
             dq_ref, dk_ref, dv_ref, dfk_ref, recv_ref, s0, d0, s1, d1, dk_acc, dv_acc,
             send_sems, recv_sems, local_sem):
        h = pl.program_id(0)
        j = pl.program_id(1)

        @pl.when(jnp.logical_and(h == 0, j == 0))
        def _():
            _start_all(_direct_copies(True, g_ref, recv_ref, send_sems, recv_sems, local_sem))

        bottom = fmin_ref[h, j]
        cut = thr_ref[h]
        ihi = lax.while_loop(
            lambda i: jnp.logical_and(i < NB - 1, fmax_ref[h, jnp.minimum(i + 1, NB - 1)] - bottom >= cut),
            lambda i: i + 1, j)

        @pl.when(j == 0)
        def _():
            dq_ref[...] = jnp.zeros_like(dq_ref)

        dk_acc[...] = jnp.zeros_like(dk_acc)
        dv_acc[...] = jnp.zeros_like(dv_acc)

        def products(i, s_ref, d_ref):
            s_ref[...] = _nn(ka_ref[0], q_ref[i])
            d_ref[...] = _nn(va_ref[0], do_ref[i])

        def grad_step(s_ref, d_ref, i, masked):
            p = jnp.exp2(s_ref[...] - lse_ref[0, i])
            if masked:
                p = jnp.where(_causal_keep(B), p, 0.0)
            ds = p * (d_ref[...] - dl_ref[0, i])
            pb = p.astype(BF16)
            dsb = ds.astype(BF16)
            dv_acc[...] = dv_acc[...] + _nt(do_ref[i][0:DH], pb)
            dk_acc[...] = dk_acc[...] + _nt(q_ref[i][0:QR], dsb)
            dq_ref[i] = dq_ref[i] + _nn(kt_ref[0], dsb)

        n_rest = ihi - j
        products(j, s1, d1)
        products(jnp.minimum(j + 1, NB - 1), s0, d0)
        grad_step(s1, d1, j, True)

        def pair(t, carry):
            i = j + 1 + 2 * t
            products(i + 1, s1, d1)
            grad_step(s0, d0, i, False)
            products(jnp.minimum(i + 2, NB - 1), s0, d0)
            grad_step(s1, d1, i + 1, False)
            return carry

        lax.fori_loop(0, n_rest // 2, pair, 0)

        @pl.when(n_rest % 2 == 1)
        def _():
            grad_step(s0, d0, ihi, False)

        dk_ref[0] = dk_acc[0:DH, :]
        dfk_ref[0, 0] = dk_acc[DH:DH + 8, :]
        dv_ref[0] = dv_acc[...]

        @pl.when(jnp.logical_and(h == H - 1, j == NB - 1))
        def _():
            _wait_all(_direct_copies(True, g_ref, recv_ref, send_sems, recv_sems, local_sem))

    per_kv = lambda r: pl.BlockSpec((1, r, B), lambda h, j: (j, h, 0))
    head_all = lambda r: pl.BlockSpec((NB, r, B), lambda h, j: (0, h, 0))
    aug = pl.BlockSpec((1, B, 128), lambda h, j: (h, j, 0))
    stat = pl.BlockSpec((1, NB, 1, B), lambda h, j: (h, 0, 0, 0))
    return _pc(
        body, name="attn_bwd", grid=(H, NB),
        in_specs=[SMEM, SMEM, SMEM, head_all(128), aug, per_kv(KR), aug, head_all(128), stat, stat, ANY],
        out_specs=[head_all(KR), per_kv(DH), per_kv(DH),
                   pl.BlockSpec((1, 1, 8, B), lambda h, j: (h, j, 0, 0)), ANY],
        out_shape=[_sds((NB, H * KR, B)), _sds((NB, A, B)), _sds((NB, A, B)), _sds((H, NB, 8, B)),
                   _sds(early_g.shape, early_g.dtype)],
        scratch_shapes=[pltpu.VMEM((B, B), F32)] * 4 + [pltpu.VMEM((QR, B), F32), pltpu.VMEM((DH, B), F32)]
        + COMM_SEMS,
        compiler_params=_cp(("arbitrary", "arbitrary")),
    )(fmax, fmin, thr, qTa, kaug, kT, vaug, doTa, lse, delta, early_g)


def _conv_tail(u1, lng, lnb, beta_c):
    mu = _rowmean(u1)
    d = u1 - mu
    rstd = lax.rsqrt(_rowmean(d * d) + EPS)
    xhat = d * rstd
    u2 = xhat * lng + lnb
    sg = _sigmoid(u2)
    u3 = u2 * sg
    rc = lax.rsqrt(_rowmean(u3 * u3) + EPS)
    n3 = u3 * rc
    return rstd, xhat, u2, sg, rc, n3, n3 * beta_c


def _attn_tail(oT, beta_a_col):
    ra = lax.rsqrt(jnp.mean(oT * oT, axis=0, keepdims=True) + EPS)
    ohat = oT * ra
    return ra, ohat, ohat * beta_a_col


def conv_merge_out(u0, x, oT, conv_w, conv_b, lng, lnb, beta_c, beta_a_col, wo, g1):
    S = x.shape[0]
    B = BLK
    NB = S // B

    def body(uc_ref, up_ref, x_ref, oT_ref, w_ref, cb_ref, lng_ref, lnb_ref, bc_ref, ba_ref, wo_ref,
             g1_ref, x1_ref, o_ref, u1_ref, ubuf):
        i = pl.program_id(0)
        ubuf[0:HALO, :] = jnp.where(i > 0, up_ref[B - HALO:B, :], 0.0)
        ubuf[HALO:HALO + B, :] = uc_ref[...]
        acc = jnp.broadcast_to(cb_ref[...], (B, CW))
        for k in range(KC):
            acc = acc + w_ref[k:k + 1, :] * ubuf[pl.ds(HALO - (KC - 1) + k, B), :]
        u1_ref[...] = acc
        mc = _conv_tail(acc, lng_ref[...], lnb_ref[...], bc_ref[...])[-1]
        maT = _attn_tail(oT_ref[0], ba_ref[...])[-1]
        ma = maT.T
        o = _nn(ma.astype(BF16), wo_ref[0:A, :]) + _nn(mc.astype(BF16), wo_ref[A:D, :])
        o_ref[...] = o
        x1_ref[...] = x_ref[...] + g1_ref[...] * o

    row = lambda w: pl.BlockSpec((B, w), lambda i: (i, 0))
    return _pc(
        body, name="conv_merge_out", grid=(NB,),
        in_specs=[row(CW), pl.BlockSpec((B, CW), lambda i: (jnp.maximum(i - 1, 0), 0)), row(D),
                  pl.BlockSpec((1, A, B), lambda i: (i, 0, 0)), _const((32, CW)), _const((1, CW)),
                  _const((1, CW)), _const((1, CW)), _const((1, CW)), _const((A, 1)), _const((D, D)),
                  _const((1, D))],
        out_specs=[row(D), row(D), row(CW)],
        out_shape=[_sds((S, D)), _sds((S, D)), _sds((S, CW))],
        scratch_shapes=[pltpu.VMEM((B + HALO, CW), F32)],
        compiler_params=_cp(("arbitrary",)),
    )(u0, u0, x, oT, conv_w, conv_b, lng, lnb, beta_c, beta_a_col, wo, g1)


def mlp_fwd_loss(x1, tgt, n2g, sc2, sh2, g2, w1, w2):
    S = x1.shape[0]
    B = BLK
    NB = S // B
    NC = FF // FCH

    def body(x1_ref, t_ref, g_ref, sc_ref, sh_ref, g2_ref, w1_ref, w2_ref, dy_ref, loss_ref, dg2_ref,
             hs, acc):
        i = pl.program_id(0)
        c = pl.program_id(1)

        @pl.when(jnp.logical_and(i == 0, c == 0))
        def _():
            loss_ref[...] = jnp.zeros_like(loss_ref)
            dg2_ref[...] = jnp.zeros_like(dg2_ref)

        @pl.when(c == 0)
        def _():
            hs[...] = _norm_mod(x1_ref[...], g_ref[...], sc_ref[...], sh_ref[...])[2].astype(BF16)
            acc[...] = jnp.zeros_like(acc)

        a = jnp.maximum(_nn(hs[...], w1_ref[...]), 0.0)
        acc[...] = acc[...] + _nn((a * a).astype(BF16), w2_ref[...])

        @pl.when(c == NC - 1)
        def _():
            y2 = acc[...]
            e = x1_ref[...] + g2_ref[...] * y2 - t_ref[...]
            dy = e * (1.0 / D)
            dy_ref[...] = dy
            loss_ref[...] = loss_ref[...] + 0.5 * _colsum(_rowmean(e * e))
            dg2_ref[...] = dg2_ref[...] + _colsum(dy * y2)

    row = pl.BlockSpec((B, D), lambda i, c: (i, 0))
    vec = pl.BlockSpec((1, D), lambda i, c: (0, 0))
    return _pc(
        body, name="mlp_fwd_loss", grid=(NB, NC),
        in_specs=[row, row, vec, vec, vec, vec, pl.BlockSpec((D, FCH), lambda i, c: (0, c)),
                  pl.BlockSpec((FCH, D), lambda i, c: (c, 0))],
        out_specs=[row, pl.BlockSpec((1, 1), lambda i, c: (0, 0)), vec],
        out_shape=[_sds((S, D)), _sds((1, 1)), _sds((1, D))],
        scratch_shapes=[pltpu.VMEM((B, D), BF16), pltpu.VMEM((B, D), F32)],
        compiler_params=_cp(("arbitrary", "arbitrary")),
    )(x1, tgt, n2g, sc2, sh2, g2, w1, w2)


def mlp_bwd(x1, dy, n2g, sc2, sh2, g2, w1, w2):
    S = x1.shape[0]
    B = BLK
    NB = S // B
    NC = FF // FCH

    def body(x1_ref, dy_ref, g_ref, sc_ref, sh_ref, g2_ref, w1_ref, w2_ref, dw1_ref, dw2t_ref, dh_ref):
        i = pl.program_id(1)

        @pl.when(i == 0)
        def _():
            dw1_ref[...] = jnp.zeros_like(dw1_ref)
            dw2t_ref[...] = jnp.zeros_like(dw2t_ref)

        h2 = _norm_mod(x1_ref[...], g_ref[...], sc_ref[...], sh_ref[...])[2]
        w1c = w1_ref[...]
        w2c = w2_ref[...]
        ra = jnp.maximum(_nn(h2.astype(BF16), w1c), 0.0)
        dy2 = dy_ref[...] * g2_ref[...]
        db = _nt(dy2.astype(BF16), w2c)
        dab = (db * (2.0 * ra)).astype(BF16)
        dh_ref[0] = _nt(dab, w1c)
        dw1_ref[...] = dw1_ref[...] + _nn(h2.T.astype(BF16), dab)
        dw2t_ref[...] = dw2t_ref[...] + _nn(dy2.T.astype(BF16), (ra * ra).astype(BF16))

    row = pl.BlockSpec((B, D), lambda c, i: (i, 0))
    vec = pl.BlockSpec((1, D), lambda c, i: (0, 0))
    wcol = pl.BlockSpec((D, FCH), lambda c, i: (0, c))
    return _pc(
        body, name="mlp_bwd", grid=(NC, NB),
        in_specs=[row, row, vec, vec, vec, vec, wcol, pl.BlockSpec((FCH, D), lambda c, i: (c, 0))],
        out_specs=[wcol, wcol, pl.BlockSpec((1, B, D), lambda c, i: (c, i, 0))],
        out_shape=[_sds((D, FF)), _sds((D, FF)), _sds((NC, S, D))],
        compiler_params=_cp(("arbitrary", "arbitrary")),
    )(x1, dy, n2g, sc2, sh2, g2, w1, w2)


def merge_bwd(dh2p, x1, dy, o, oT, u1, n2g, sc2, g1, lng, lnb, beta_c, beta_a_col, wo):
    S = x1.shape[0]
    B = BLK
    NB = S // B
    NC = dh2p.shape[0]

    def body(dh_ref, x1_ref, dy_ref, o_ref, oT_ref, u1_ref, g_ref, sc_ref, g1_ref, lng_ref, lnb_ref,
             bc_ref, ba_ref, wo_ref,
             dx1_ref, doTa_ref, dl_ref, du1_ref, dwo_ref, v1_ref, v2_ref, dba_ref):
        i = pl.program_id(0)

        @pl.when(i == 0)
        def _():
            dwo_ref[...] = jnp.zeros_like(dwo_ref)
            v1_ref[...] = jnp.zeros_like(v1_ref)
            v2_ref[...] = jnp.zeros_like(v2_ref)
            dba_ref[...] = jnp.zeros_like(dba_ref)

        dh2 = dh_ref[0]
        for cc in range(1, NC):
            dh2 = dh2 + dh_ref[cc]
        x1 = x1_ref[...]
        g = g_ref[...]
        sc = sc_ref[...]
        r2 = lax.rsqrt(_rowmean(x1 * x1) + EPS)
        xh = x1 * r2
        dhx = dh2 * xh
        v1_ref[0:1, :] = v1_ref[0:1, :] + _colsum(dh2)
        v1_ref[1:2, :] = v1_ref[1:2, :] + _colsum(dhx) * g
        v1_ref[2:3, :] = v1_ref[2:3, :] + _colsum(dhx) * (1.0 + sc)
        dxh = dh2 * (g * (1.0 + sc))
        dx1 = dy_ref[...] + r2 * (dxh - xh * _rowmean(dxh * xh))
        dx1_ref[...] = dx1
        v1_ref[3:4, :] = v1_ref[3:4, :] + _colsum(dx1 * o_ref[...])
        dob = (dx1 * g1_ref[...]).astype(BF16)

        lng = lng_ref[...]
        bc = bc_ref[...]
        rstd, xhat, u2, sg, rc, n3, mc = _conv_tail(u1_ref[...], lng, lnb_ref[...], bc)
        ba = ba_ref[...]
        oT = oT_ref[0]
        ra, ohat, maT = _attn_tail(oT, ba)
        dwo_ref[0:A, :] = dwo_ref[0:A, :] + _nn(maT.astype(BF16), dob)
        dwo_ref[A:D, :] = dwo_ref[A:D, :] + _nn(mc.T.astype(BF16), dob)
        dmaT = _nt(wo_ref[0:A, :], dob)
        dmc = _nt(dob, wo_ref[A:D, :])

        dba_ref[...] = dba_ref[...] + _lanesum(dmaT * ohat)
        dohat = dmaT * ba
        doT = ra * (dohat - ohat * jnp.mean(dohat * ohat, axis=0, keepdims=True))
        prod = doT * oT
        zpad = jnp.zeros((128 - DH, B), BF16)
        for hh in range(H):
            sl = slice(hh * DH, (hh + 1) * DH)
            dl_ref[hh, 0] = _colsum(prod[sl])
            doTa_ref[0, hh * 128:hh * 128 + DH, :] = doT[sl].astype(BF16)
            doTa_ref[0, hh * 128 + DH:(hh + 1) * 128, :] = zpad

        v2_ref[0:1, :] = v2_ref[0:1, :] + _colsum(dmc * n3)
        dn3 = dmc * bc
        du3 = rc * (dn3 - n3 * _rowmean(dn3 * n3))
        du2 = du3 * (sg * (1.0 + u2 * (1.0 - sg)))
        v2_ref[1:2, :] = v2_ref[1:2, :] + _colsum(du2 * xhat)
        v2_ref[2:3, :] = v2_ref[2:3, :] + _colsum(du2)
        dxhat = du2 * lng
        du1_ref[...] = rstd * (dxhat - _rowmean(dxhat) - xhat * _rowmean(dxhat * xhat))

    row = lambda w: pl.BlockSpec((B, w), lambda i: (i, 0))
    return _pc(
        body, name="merge_bwd", grid=(NB,),
        in_specs=[pl.BlockSpec((NC, B, D), lambda i: (0, i, 0)), row(D), row(D), row(D),
                  pl.BlockSpec((1, A, B), lambda i: (i, 0, 0)), row(CW), _const((1, D)), _const((1, D)),
                  _const((1, D)), _const((1, CW)), _const((1, CW)), _const((1, CW)), _const((A, 1)),
                  _const((D, D))],
        out_specs=[row(D), pl.BlockSpec((1, H * 128, B), lambda i: (i, 0, 0)),
                   pl.BlockSpec((H, 1, 1, B), lambda i: (0, i, 0, 0)), row(CW), _const((D, D)),
                   _const((8, D)), _const((8, CW)), _const((A, 1))],
        out_shape=[_sds((S, D)), _sds((NB, H * 128, B), BF16), _sds((H, NB, 1, B)), _sds((S, CW)),
                   _sds((D, D)), _sds((8, D)), _sds((8, CW)), _sds((A, 1))],
        compiler_params=_cp(("arbitrary",)),
    )(dh2p, x1, dy, o, oT, u1, n2g, sc2, g1, lng, lnb, beta_c, beta_a_col, wo)


def head_bwd(du1, u0, alg, zqk, fgT, dqT, dkT, dvT, dfk, conv_w, qg_col, kg_col, bf_col, tri_lo):
    S = u0.shape[0]
    B = BLK
    NB = S // B

    def body(dc_ref, dn_ref, uc_ref, up_ref, alg_ref, zqk_ref, fg_ref, dq_ref, dk_ref, dv_ref, dfk_ref,
             w_ref, qg_ref, kg_ref, bf_ref, tri_ref,
             dzr_ref, dzT_ref, dcw_ref, vc_ref, dqg_ref, dkg_ref, dbf_ref,
             dbuf, ubuf, carry, fbuf):
        pid = pl.program_id(0)
        ri = NB - 1 - pid

        @pl.when(pid == 0)
        def _():
            carry[...] = jnp.zeros_like(carry)
            dcw_ref[...] = jnp.zeros_like(dcw_ref)
            vc_ref[...] = jnp.zeros_like(vc_ref)
            dqg_ref[...] = jnp.zeros_like(dqg_ref)
            dkg_ref[...] = jnp.zeros_like(dkg_ref)
            dbf_ref[...] = jnp.zeros_like(dbf_ref)

        du1 = dc_ref[...]
        dbuf[0:B, :] = du1
        dbuf[B:B + HALO, :] = jnp.where(ri < NB - 1, dn_ref[0:HALO, :], 0.0)
        ubuf[0:HALO, :] = jnp.where(ri > 0, up_ref[B - HALO:B, :], 0.0)
        ubuf[HALO:HALO + B, :] = uc_ref[...]
        du0 = jnp.zeros((B, CW), F32)
        for k in range(KC):
            du0 = du0 + w_ref[k:k + 1, :] * dbuf[pl.ds(KC - 1 - k, B), :]
            dcw_ref[k:k + 1, :] = dcw_ref[k:k + 1, :] + _colsum(
                du1 * ubuf[pl.ds(HALO - (KC - 1) + k, B), :])
        vc_ref[0:1, :] = vc_ref[0:1, :] + _colsum(du1)

        al = alg_ref[:, 0:CW]
        sg = _sigmoid(alg_ref[:, CW:2 * CW])
        dzr_ref[:, 0:CW] = (du0 * sg).astype(BF16)
        dzr_ref[:, CW:2 * CW] = (du0 * al * sg * (1.0 - sg)).astype(BF16)

        dqg = jnp.zeros((DH, B), F32)
        dkg = jnp.zeros((DH, B), F32)
        qg = qg_ref[...]
        kg = kg_ref[...]
        for hh in range(H):
            sl = slice(hh * DH, (hh + 1) * DH)
            q = zqk_ref[sl, :]
            rq = lax.rsqrt(jnp.mean(q * q, axis=0, keepdims=True) + EPS)
            qn = q * rq
            dqh = dq_ref[0, hh * KR:hh * KR + DH, :] * 0.125
            dqg = dqg + dqh * qn
            dqn = dqh * qg
            dzT_ref[0, sl, :] = (rq * (dqn - qn * jnp.mean(dqn * qn, axis=0, keepdims=True))).astype(BF16)
            k = zqk_ref[A + hh * DH:A + (hh + 1) * DH, :]
            rk = lax.rsqrt(jnp.mean(k * k, axis=0, keepdims=True) + EPS)
            kn = k * rk
            dkh = dk_ref[0, sl, :] * (1.0 / LOG2E)
            dkg = dkg + dkh * kn
            dkn = dkh * kg
            dzT_ref[0, A + hh * DH:A + (hh + 1) * DH, :] = (
                rk * (dkn - kn * jnp.mean(dkn * kn, axis=0, keepdims=True))).astype(BF16)
            fbuf[hh:hh + 1, :] = dq_ref[0, hh * KR + DH:hh * KR + DH + 1, :] - dfk_ref[hh, 0, 0:1, :]
        dqg_ref[...] = dqg_ref[...] + _lanesum(dqg)
        dkg_ref[...] = dkg_ref[...] + _lanesum(dkg)
        dzT_ref[0, 2 * A:3 * A, :] = dv_ref[0].astype(BF16)

        dF = fbuf[...]
        a1, a2, a3 = _split3(dF)
        tr = tri_ref[...]
        dlogf = _nn(a1, tr) + _nn(a2, tr) + _nn(a3, tr) + carry[...]
        carry[...] = carry[...] + _lanesum(dF)
        dfg = dlogf * _sigmoid(-(fg_ref[...] + bf_ref[...]))
        dbf_ref[...] = dbf_ref[...] + _lanesum(dfg)
        dzT_ref[0, 3 * A:ZT_ROWS, :] = jnp.concatenate([dfg, jnp.zeros((8, B), F32)], axis=0).astype(BF16)

    rrow = lambda w: pl.BlockSpec((B, w), lambda p: (NB - 1 - p, 0))
    rts = lambda r: pl.BlockSpec((1, r, B), lambda p: (NB - 1 - p, 0, 0))
    return _pc(
        body, name="head_bwd", grid=(NB,),
        in_specs=[rrow(CW), pl.BlockSpec((B, CW), lambda p: (jnp.minimum(NB - p, NB - 1), 0)),
                  rrow(CW), pl.BlockSpec((B, CW), lambda p: (jnp.maximum(NB - 2 - p, 0), 0)),
                  rrow(2 * CW), pl.BlockSpec((2 * A, B), lambda p: (0, NB - 1 - p)),
                  pl.BlockSpec((8, B), lambda p: (0, NB - 1 - p)), rts(H * KR), rts(A), rts(A),
                  pl.BlockSpec((H, 1, 8, B), lambda p: (0, NB - 1 - p, 0, 0)),
                  _const((32, CW)), _const((DH, 1)), _const((DH, 1)), _const((8, 1)), _const((B, B))],
        out_specs=[rrow(2 * CW), rts(ZT_ROWS), _const((32, CW)), _const((8, CW)), _const((DH, 1)),
                   _const((DH, 1)), _const((8, 1))],
        out_shape=[_sds((S, 2 * CW), BF16), _sds((NB, ZT_ROWS, B), BF16), _sds((32, CW)), _sds((8, CW)),
                   _sds((DH, 1)), _sds((DH, 1)), _sds((8, 1))],
        scratch_shapes=[pltpu.VMEM((B + HALO, CW), F32), pltpu.VMEM((B + HALO, CW), F32),
                        pltpu.VMEM((8, 1), F32), pltpu.VMEM((8, B), F32)],
        compiler_params=_cp(("arbitrary",)),
    )(du1, du1, u0, u0, alg, zqk, fgT, dqT, dkT, dvT, dfk, conv_w, qg_col, kg_col, bf_col, tri_lo)


def in_bwd(x, dx1, dzr, dzT, n1g, sc1, sh1, wT, w_qkvf, wr):
    S = x.shape[0]
    B = BLK
    NB = S // B

    def body(x_ref, dx1_ref, dzr_ref, dzT_ref, g_ref, sc_ref, sh_ref, wT_hbm, w_hbm, wr_hbm,
             gx_ref, dwT_hbm, dwr_hbm, v_ref, wT_v, w_v, wr_v, dwT_acc, dwr_acc, sem):
        i = pl.program_id(0)

        @pl.when(i == 0)
        def _():
            for src, dst in ((wT_hbm, wT_v), (w_hbm, w_v), (wr_hbm, wr_v)):
                cp = pltpu.make_async_copy(src, dst, sem)
                cp.start()
                cp.wait()
            dwT_acc[...] = jnp.zeros_like(dwT_acc)
            dwr_acc[...] = jnp.zeros_like(dwr_acc)
            v_ref[...] = jnp.zeros_like(v_ref)

        g = g_ref[...]
        sc = sc_ref[...]
        r1, xh, h1 = _norm_mod(x_ref[...], g, sc, sh_ref[...])
        hb = h1.astype(BF16)
        dzr = dzr_ref[...]
        dzT = dzT_ref[0]
        dwT_acc[...] = dwT_acc[...] + _nn(dzT, hb)
        dwr_acc[...] = dwr_acc[...] + _nn(h1.T.astype(BF16), dzr)
        dh1 = _nt(dzr, wr_v[...]) + _nn(w_v[...], dzT).T
        dhx = dh1 * xh
        v_ref[0:1, :] = v_ref[0:1, :] + _colsum(dh1)
        v_ref[1:2, :] = v_ref[1:2, :] + _colsum(dhx) * g
        v_ref[2:3, :] = v_ref[2:3, :] + _colsum(dhx) * (1.0 + sc)
        dxh = dh1 * (g * (1.0 + sc))
        gx_ref[...] = dx1_ref[...] + r1 * (dxh - xh * _rowmean(dxh * xh))

        @pl.when(i == NB - 1)
        def _():
            for src, dst in ((dwT_acc, dwT_hbm), (dwr_acc, dwr_hbm)):
                cp = pltpu.make_async_copy(src, dst, sem)
                cp.start()
                cp.wait()

    row = lambda w: pl.BlockSpec((B, w), lambda i: (i, 0))
    return _pc(
        body, name="in_bwd", grid=(NB,),
        in_specs=[row(D), row(D), row(2 * CW), pl.BlockSpec((1, ZT_ROWS, B), lambda i: (i, 0, 0)),
                  _const((1, D)), _const((1, D)), _const((1, D)), ANY, ANY, ANY],
        out_specs=[row(D), ANY, ANY, _const((8, D))],
        out_shape=[_sds((S, D)), _sds((ZT_ROWS, D)), _sds((D, 2 * CW)), _sds((8, D))],
        scratch_shapes=[pltpu.VMEM((ZT_ROWS, D), BF16), pltpu.VMEM((D, ZT_ROWS), BF16),
                        pltpu.VMEM((D, 2 * CW), BF16), pltpu.VMEM((ZT_ROWS, D), F32),
                        pltpu.VMEM((D, 2 * CW), F32), pltpu.SemaphoreType.DMA(())],
        compiler_params=_cp(("arbitrary",)),
    )(x, dx1, dzr, dzT, n1g, sc1, sh1, wT, w_qkvf, wr)


def _adam_math(g, w, m, v):
    m = ADAM_B1 * m + (1.0 - ADAM_B1) * g
    v = ADAM_B2 * v + (1.0 - ADAM_B2) * (g * g)
    m_hat = m / (1.0 - ADAM_B1 ** ADAM_STEP)
    v_hat = v / (1.0 - ADAM_B2 ** ADAM_STEP)
    delta = -ADAM_LR * (m_hat / (jnp.sqrt(v_hat) + ADAM_EPS) + ADAM_WD * w)
    return delta, m, v


def _row_tile(R):
    for t in (1024, 512, 256, 128, 64, 32, 16, 8):
        if R % t == 0:
            return t
    return R


def sum_slots(parts, name):
    K, R, C = parts.shape
    T = _row_tile(R)

    def body(p_ref, o_ref):
        s = p_ref[0]
        for k in range(1, K):
            s = s + p_ref[k]
        o_ref[...] = s

    return _pc(body, name=name, grid=(R // T,),
               in_specs=[pl.BlockSpec((K, T, C), lambda i: (0, i, 0))],
               out_specs=pl.BlockSpec((T, C), lambda i: (i, 0)), out_shape=_sds((R, C)),
               compiler_params=_cp(("arbitrary",)))(parts)


def adamw_slots(parts, w, m, v, name):
    K, R, C = parts.shape
    T = _row_tile(R)

    def body(p_ref, w_ref, m_ref, v_ref, g_ref, d_ref, nm_ref, nv_ref):
        g = p_ref[0]
        for k in range(1, K):
            g = g + p_ref[k]
        g_ref[...] = g
        d_ref[...], nm_ref[...], nv_ref[...] = _adam_math(g, w_ref[...], m_ref[...], v_ref[...])

    t2 = pl.BlockSpec((T, C), lambda i: (i, 0))
    return _pc(body, name=name, grid=(R // T,),
               in_specs=[pl.BlockSpec((K, T, C), lambda i: (0, i, 0)), t2, t2, t2],
               out_specs=[t2, t2, t2, t2], out_shape=[_sds((R, C))] * 4,
               compiler_params=_cp(("arbitrary",)))(parts, w, m, v)


def ada_grad_adamw(cT, dmod_cols, w, m, v):
    NCOL = w.shape[1]

    def body(cT_ref, dm_ref, w_ref, m_ref, v_ref, g_ref, d_ref, nm_ref, nv_ref):
        def term(b):
            cv = cT_ref[b]
            return (cv * _sigmoid(cv)) * dm_ref[b:b + 1, :]

        g = term(0)
        for b in range(1, N_DEV):
            g = g + term(b)
        g_ref[...] = g
        d_ref[...], nm_ref[...], nv_ref[...] = _adam_math(g, w_ref[...], m_ref[...], v_ref[...])

    full = _const((D, NCOL))
    return _pc(body, name="ada_grad_adamw", grid=(1,),
               in_specs=[_const((N_DEV, D, 1)), _const((N_DEV, NCOL)), full, full, full],
               out_specs=[full, full, full, full], out_shape=[_sds((D, NCOL))] * 4,
               compiler_params=_cp())(cT, dmod_cols, w, m, v)


def _pack_rows(vecs, rows=None):
    flat = jnp.concatenate([jnp.ravel(v) for v in vecs])
    n = flat.shape[0]
    r = -(-n // 1024) * 8 if rows is None else rows
    return jnp.pad(flat, (0, r * 128 - n)).reshape(r, 128)


def _unpack_rows(packed, shapes):
    flat = packed.reshape(-1)
    out, off = [], 0
    for s in shapes:
        n = 1
        for d in s:
            n *= d
        out.append(flat[off:off + n].reshape(s))
        off += n
    return out


def _cols_from_shards(g, rows, cols):
    return g.reshape(N_DEV, rows, cols).transpose(1, 0, 2).reshape(rows, N_DEV * cols)


def _cols_to_shards(w, cols):
    rows = w.shape[0]
    return w.reshape(rows, N_DEV, cols).transpose(1, 0, 2).reshape(N_DEV, rows * cols)


def kernel(x, c, w_ada, b_ada, norm1_g, w_in, q_norm_g, k_norm_g, b_f, conv_w, conv_b, conv_ln_g, conv_ln_b, beta_attn, beta_conv, w_out, norm2_g, w_ff1, w_ff2, loss_target, m_w_ada, m_b_ada, m_norm1_g, m_w_in, m_q_norm_g, m_k_norm_g, m_b_f, m_conv_w, m_conv_b, m_conv_ln_g, m_conv_ln_b, m_beta_attn, m_beta_conv, m_w_out, m_norm2_g, m_w_ff1, m_w_ff2, v_w_ada, v_b_ada, v_norm1_g, v_w_in, v_q_norm_g, v_k_norm_g, v_b_f, v_conv_w, v_conv_b, v_conv_ln_g, v_conv_ln_b, v_beta_attn, v_beta_conv, v_w_out, v_norm2_g, v_w_ff1, v_w_ff2):
    S = x.shape[1]
    B = BLK
    NB = S // B
    me = 4 * lax.axis_index("x") + 2 * lax.axis_index("y") + lax.axis_index("c")
    xs = x[0]
    tgt = loss_target[0]

    ADA_C, IN_C, FF_C, CV_C = w_ada.shape[2], w_in.shape[2], w_ff1.shape[2], conv_w.shape[2]
    OUT_R, FF_R = w_out.shape[1], w_ff2.shape[1]
    IN_W = N_DEV * IN_C

    n_ada, n_in, n_out, n_f1, n_f2 = D * ADA_C, D * IN_C, OUT_R * D, D * FF_C, FF_R * D
    early = jnp.concatenate([w_ada.reshape(-1), w_in.reshape(-1)]).astype(BF16)
    rows_e = -(-early.shape[0] // (128 * 16)) * 16
    early = jnp.pad(early, (0, rows_e * 128 - early.shape[0])).reshape(rows_e, 128)
    small0 = lax.bitcast_convert_type(_pack_rows([c, conv_w], rows=24), BF16).reshape(48, 128)
    ge = all_gather(jnp.concatenate([early, small0], axis=0), "ag_weights_early")
    gw = ge[:, 0:rows_e].reshape(N_DEV, rows_e * 128)
    gs0 = lax.bitcast_convert_type(ge[:, rows_e:].reshape(N_DEV, 24, 128, 2), F32).reshape(N_DEV, 24 * 128)
    late_w = jnp.concatenate([w_out.reshape(-1), w_ff1.reshape(-1), w_ff2.reshape(-1)]).astype(BF16)
    late_w = late_w.reshape(-1, 128)

    w_ada_full = _cols_from_shards(gw[:, 0:n_ada], D, ADA_C)
    w_in_full = _cols_from_shards(gw[:, n_ada:n_ada + n_in], D, IN_C)
    c_all = gs0[:, 0:D]
    conv_w_full = _cols_from_shards(gs0[:, D:D + KC * CV_C], KC, CV_C)
    conv_w_pad = jnp.pad(conv_w_full, ((0, 32 - KC), (0, 0)))

    w_qkvf = jnp.pad(w_in_full[:, 0:3 * A + H], ((0, 0), (0, ZT_ROWS - 3 * A - H)))
    wT = w_qkvf.T
    wr = w_in_full[:, 3 * A + H:]

    qg_col = q_norm_g.reshape(DH, 1)
    kg_col = k_norm_g.reshape(DH, 1)
    bf_col = b_f.reshape(H, 1)
    beta_a_col = beta_attn.reshape(A, 1)
    ii = lax.broadcasted_iota(jnp.int32, (B, B), 0)
    jj = lax.broadcasted_iota(jnp.int32, (B, B), 1)
    tri_up = (ii <= jj).astype(BF16)
    tri_lo = (ii >= jj).astype(BF16)

    mod = mod_kernel(c, w_ada_full, b_ada)
    sh1, sc1, g1, sh2, sc2, g2 = [mod[:, k * D:(k + 1) * D] for k in range(6)]
    qTa, kT, kaug, vT, vaug, zqk, fgT, alg, u0, stat = fwd_in(xs, norm1_g, sc1, sh1, wT, wr, qg_col,
                                                             kg_col, bf_col, tri_up)
    fmax = stat[:, :, 0].T
    fmin = stat[:, :, 1].T
    qk_max = jnp.sqrt(jnp.max(stat[:, :, 2], axis=0) * jnp.max(stat[:, :, 3], axis=0))
    thr = -(PRUNE + (2.02 / LOG2E) * qk_max)
    oT, lse, gl = attn_fwd(fmax, fmin, thr, qTa, kaug, vT, late_w)
    gl = gl.reshape(N_DEV, -1)
    wo_full = gl[:, 0:n_out].reshape(D, D)
    w1_full = _cols_from_shards(gl[:, n_out:n_out + n_f1], D, FF_C)
    w2_full = gl[:, n_out + n_f1:].reshape(FF, D)
    x1, o, u1 = conv_merge_out(u0, xs, oT, conv_w_pad, conv_b, conv_ln_g, conv_ln_b, beta_conv,
                               beta_a_col, wo_full, g1)
    dy, loss_part, dg2 = mlp_fwd_loss(x1, tgt, norm2_g, sc2, sh2, g2, w1_full, w2_full)

    dw1, dw2t, dh2p = mlp_bwd(x1, dy, norm2_g, sc2, sh2, g2, w1_full, w2_full)
    dx1, doTa, delta, du1, dwo, v1, v2, dba = merge_bwd(dh2p, x1, dy, o, oT, u1, norm2_g, sc2, g1,
                                                         conv_ln_g, conv_ln_b, beta_conv, beta_a_col, wo_full)
    early_g = jnp.concatenate([dwo.reshape(N_DEV, n_out), _cols_to_shards(dw1, FF_C),
                               dw2t.T.reshape(N_DEV, n_f2)], axis=1)
    rows_a = early_g.shape[1] // 128
    dqT, dkT, dvT, dfk, recv_a = attn_bwd(fmax, fmin, thr, qTa, kaug, kT, vaug, doTa, lse, delta,
                                          early_g.reshape(N_DEV, rows_a, 128))
    dzr, dzT, dcw, vc, dqg, dkg, dbf = head_bwd(du1, u0, alg, zqk, fgT, dqT, dkT, dvT, dfk, conv_w_pad,
                                                qg_col, kg_col, bf_col, tri_lo)
    grad_x, dwT, dwr, v0 = in_bwd(xs, dx1, dzr, dzT, norm1_g, sc1, sh1, wT, w_qkvf, wr)

    dw_in = jnp.concatenate([dwT.T[:, 0:3 * A + H], dwr], axis=1)
    late_g = _cols_to_shards(dw_in, IN_C)
    rows_b = -(-late_g.shape[1] // (128 * 256)) * 256
    late_g = jnp.pad(late_g, ((0, 0), (0, rows_b * 128 - late_g.shape[1])))
    recv_b = reduce_scatter_exchange(late_g.reshape(N_DEV, rows_b, 128), "rs_grads_in")

    dmod = jnp.concatenate([v0[0], v0[1], v1[3], v1[0], v1[1], dg2[0]])
    small1 = _pack_rows([dmod, v0[2], v1[2], dcw, vc[0], v2[1], v2[2], v2[0], dba, dqg, dkg,
                         jnp.pad(dbf.reshape(-1), (0, 120)), jnp.pad(loss_part.reshape(-1), (0, 127))])
    gs1 = all_gather(small1, "ag_small_bwd")
    tot = sum_slots(gs1, "sum_small")
    (g_b_ada, g_n1, g_n2, g_cw, g_cb, g_lng, g_lnb, g_bc, g_ba, g_qg, g_kg, g_bf, loss_v) = _unpack_rows(
        tot, [(1, 6 * D), (1, D), (1, D), (32, CW), (1, CW), (1, CW), (1, CW), (1, CW), (1, A),
              (1, DH), (1, DH), (1, 128), (1, 128)])
    loss = loss_v[0, 0]
    g_bf = g_bf[:, 0:H]
    g_cw_mine = lax.dynamic_slice(g_cw[0:KC], (0, me * CV_C), (KC, CV_C)).reshape(1, KC, CV_C)

    small_names = [(b_ada, m_b_ada, v_b_ada, g_b_ada), (norm1_g, m_norm1_g, v_norm1_g, g_n1),
                   (q_norm_g, m_q_norm_g, v_q_norm_g, g_qg), (k_norm_g, m_k_norm_g, v_k_norm_g, g_kg),
                   (b_f, m_b_f, v_b_f, g_bf), (conv_w, m_conv_w, v_conv_w, g_cw_mine),
                   (conv_b, m_conv_b, v_conv_b, g_cb), (conv_ln_g, m_conv_ln_g, v_conv_ln_g, g_lng),
                   (conv_ln_b, m_conv_ln_b, v_conv_ln_b, g_lnb), (beta_attn, m_beta_attn, v_beta_attn, g_ba),
                   (beta_conv, m_beta_conv, v_beta_conv, g_bc), (norm2_g, m_norm2_g, v_norm2_g, g_n2)]
    shapes_s = [t[0].shape for t in small_names]
    pw, pm, pv, pg = [_pack_rows([t[k] for t in small_names]) for k in range(4)]
    sg_, sd_, sm_, sv_ = adamw_slots(pg[None], pw, pm, pv, "adamw_small")
    sgs, sds, sms, svs = [_unpack_rows(a, shapes_s) for a in (sg_, sd_, sm_, sv_)]

    dmod_all = gs1[:, 0:48, :].reshape(N_DEV, N_DEV, ADA_C)
    dmod_cols = lax.dynamic_index_in_dim(dmod_all, me, axis=1, keepdims=False)
    ga, da, ma_, va_ = ada_grad_adamw(c_all.reshape(N_DEV, D, 1), dmod_cols, w_ada[0], m_w_ada[0], v_w_ada[0])

    res_a = adamw_slots(recv_a, *[_pack_rows(ws, rows=rows_a) for ws in (
        [w_out, w_ff1, w_ff2], [m_w_out, m_w_ff1, m_w_ff2], [v_w_out, v_w_ff1, v_w_ff2])], "adamw_late")
    res_b = adamw_slots(recv_b, *[_pack_rows(ws, rows=rows_b) for ws in ([w_in], [m_w_in], [v_w_in])],
                        "adamw_in")
    bgs, bds, bms, bvs = [
        _unpack_rows(b, [(1, D, IN_C)]) + _unpack_rows(a, [(1, OUT_R, D), (1, D, FF_C), (1, FF_R, D)])
        for a, b in zip(res_a, res_b)]

    def assemble(small, ada, bigs):
        (b_ada_, n1_, qg_, kg_, bf_, cw_, cb_, lng_, lnb_, ba_, bc_, n2_) = small
        return [ada.reshape(1, D, ADA_C), b_ada_, n1_, bigs[0], qg_, kg_, bf_, cw_, cb_, lng_, lnb_, ba_, bc_,
                bigs[1], n2_, bigs[2], bigs[3]]

    return (loss, grad_x.reshape(1, S, D), *assemble(sgs, ga, bgs), *assemble(sds, da, bds),
            *assemble(sms, ma_, bms), *assemble(svs, va_, bvs))
```

```python
import functools

import jax
import jax.numpy as jnp
from jax import lax
from jax.experimental import pallas as pl
from jax.experimental.pallas import tpu as pltpu

F32 = jnp.float32
BF16 = jnp.bfloat16

D = 1024
A = 512
CW = 512
H = 8
DH = 64
FF = 4096
KC = 31
HALO = 32
KR = 80
ZT_ROWS = 1552
FCH = 1024
EPS = 1e-6
BLK = 512
N_DEV = 8
VMEM_LIMIT = 56 * 1024 * 1024

ADAM_LR = 0.001
ADAM_B1 = 0.9
ADAM_B2 = 0.999
ADAM_EPS = 1e-08
ADAM_WD = 0.01
ADAM_STEP = 10

MESH = pl.DeviceIdType.MESH
ANY = pl.BlockSpec(memory_space=pl.ANY)
SMEM = pl.BlockSpec(memory_space=pltpu.SMEM)
PRUNE = 110.0
LOG2E = 1.4426950408889634


def _pc(body, **kw):
    return pl.pallas_call(body, **kw)


def _cp(sem=None):
    return pltpu.CompilerParams(dimension_semantics=sem, vmem_limit_bytes=VMEM_LIMIT)


def _sds(shape, dtype=F32):
    return jax.ShapeDtypeStruct(shape, dtype)


def _const(shape):
    n = len(shape)
    return pl.BlockSpec(shape, lambda *a: (0,) * n)


def _nt(a, b):
    return lax.dot_general(a, b, (((1,), (1,)), ((), ())), preferred_element_type=F32)


def _nn(a, b):
    return jnp.dot(a, b, preferred_element_type=F32)


def _sigmoid(v):
    return 1.0 / (1.0 + jnp.exp(-v))


def _split3(v):
    a1 = v.astype(BF16)
    r1 = v - a1.astype(F32)
    a2 = r1.astype(BF16)
    a3 = (r1 - a2.astype(F32)).astype(BF16)
    return a1, a2, a3


def _rowmean(v):
    return jnp.mean(v, axis=-1, keepdims=True)


def _colsum(v):
    return jnp.sum(v, axis=0, keepdims=True)


def _lanesum(v):
    return jnp.sum(v, axis=-1, keepdims=True)


def _coords():
    return lax.axis_index("x"), lax.axis_index("y"), lax.axis_index("c")


def all_gather(x, name):
    R, C = x.shape

    def body(x_ref, out_ref, send_sems, recv_sems, local_sem):
        mx, my, mc = _coords()
        me, sibling = (mx, my, mc), (mx, my, 1 - mc)
        chips = [(1 - mx, my), (mx, 1 - my), (1 - mx, 1 - my)]

        def rows(px, py, pc):
            return out_ref.at[4 * px + 2 * py + pc]

        def copy(k, block, to, src=None):
            return pltpu.make_async_remote_copy(
                src_ref=rows(*block) if src is None else src, dst_ref=rows(*block),
                send_sem=send_sems.at[k], recv_sem=recv_sems.at[k],
                device_id=to, device_id_type=MESH)

        mine = pltpu.make_async_copy(x_ref, rows(*me), local_sem)
        mine.start()
        first = [copy(0, me, sibling, src=x_ref)]
        first += [copy(1 + j, me, (*chip, mc), src=x_ref) for j, chip in enumerate(chips)]
        for cp in first:
            cp.start()
        passed = [copy(4 + j, (*chip, mc), sibling) for j, chip in enumerate(chips)]
        for j, chip in enumerate(chips):
            copy(1 + j, (*chip, mc), me).wait_recv()
            passed[j].start()
        copy(0, sibling, me).wait_recv()
        for j, chip in enumerate(chips):
            copy(4 + j, (*chip, 1 - mc), me).wait_recv()
        for cp in first + passed:
            cp.wait_send()
        mine.wait()

    return _pc(
        body, name=name, out_shape=_sds((N_DEV, R, C), x.dtype),
        in_specs=[ANY], out_specs=ANY,
        scratch_shapes=[pltpu.SemaphoreType.DMA((7,)), pltpu.SemaphoreType.DMA((7,)),
                        pltpu.SemaphoreType.DMA(())],
    )(x)


def _direct_copies(scatter, src_ref, dst_ref, send_sems, recv_sems, local_sem):
    mx, my, mc = _coords()
    me = 4 * mx + 2 * my + mc
    if scatter:
        local = pltpu.make_async_copy(src_ref.at[me], dst_ref.at[0], local_sem)
    else:
        local = pltpu.make_async_copy(src_ref, dst_ref.at[me], local_sem)
    remote = []
    for r in range(1, N_DEV):
        px = 1 - mx if r & 4 else mx
        py = 1 - my if r & 2 else my
        pcc = 1 - mc if r & 1 else mc
        remote.append(pltpu.make_async_remote_copy(
            src_ref=src_ref.at[4 * px + 2 * py + pcc] if scatter else src_ref,
            dst_ref=dst_ref.at[r] if scatter else dst_ref.at[me],
            send_sem=send_sems.at[r - 1], recv_sem=recv_sems.at[r - 1],
            device_id=(px, py, pcc), device_id_type=MESH))
    return [local] + remote


def _start_all(copies):
    for cp in copies:
        cp.start()


def _wait_all(copies):
    for cp in copies[1:]:
        cp.wait()
    copies[0].wait()


COMM_SEMS = [pltpu.SemaphoreType.DMA((7,)), pltpu.SemaphoreType.DMA((7,)), pltpu.SemaphoreType.DMA(())]


def reduce_scatter_exchange(g, name):
    def body(g_ref, recv_ref, send_sems, recv_sems, local_sem):
        copies = _direct_copies(True, g_ref, recv_ref, send_sems, recv_sems, local_sem)
        _start_all(copies)
        _wait_all(copies)

    return _pc(body, name=name, out_shape=_sds(g.shape, g.dtype), in_specs=[ANY], out_specs=ANY,
               scratch_shapes=COMM_SEMS)(g)


def mod_kernel(c, w_ada_b, b_ada):
    def body(c_ref, w_ref, b_ref, o_ref):
        cv = c_ref[...]
        sc = jnp.broadcast_to(cv * _sigmoid(cv), (8, D)).astype(BF16)
        o_ref[...] = _nn(sc, w_ref[...])[0:1, :] + b_ref[...]

    return _pc(body, name="mod", out_shape=_sds((1, 6 * D)),
               in_specs=[_const((1, D)), _const((D, 6 * D)), _const((1, 6 * D))],
               out_specs=_const((1, 6 * D)), grid=(1,), compiler_params=_cp())(c, w_ada_b, b_ada)


def _norm_mod(xv, g, sc, sh):
    r = lax.rsqrt(_rowmean(xv * xv) + EPS)
    xh = xv * r
    return r, xh, xh * (g * (1.0 + sc)) + sh


def _log_sigmoid(v):
    e = jnp.exp(-jnp.abs(v))
    l1p = jnp.where(e < 1e-4, e * (1.0 - 0.5 * e), jnp.log(1.0 + e))
    return jnp.minimum(v, 0.0) - l1p


def fwd_in(x, n1g, sc1, sh1, wT, wr, qg_col, kg_col, bf_col, tri):
    S = x.shape[0]
    B = BLK
    NB = S // B

    def body(x_ref, g_ref, sc_ref, sh_ref, wT_ref, wr_ref, qg_ref, kg_ref, bf_ref, tri_ref,
             qTa_ref, kT_ref, kaug_ref, vT_ref, vaug_ref, zqk_ref, fgT_ref, alg_ref, u0_ref, stat_ref,
             carry, tbuf, fs, nq, nk):
        i = pl.program_id(0)

        @pl.when(i == 0)
        def _():
            carry[...] = jnp.zeros_like(carry)

        _, _, h = _norm_mod(x_ref[...], g_ref[...], sc_ref[...], sh_ref[...])
        hb = h.astype(BF16)
        zT = _nt(wT_ref[...], hb)
        zr = _nn(hb, wr_ref[...])
        zqk_ref[...] = zT[0:2 * A]
        fgT = zT[3 * A:3 * A + 8]
        fgT_ref[...] = fgT
        alg_ref[...] = zr
        u0_ref[...] = zr[:, 0:CW] * _sigmoid(zr[:, CW:2 * CW])

        logf = _log_sigmoid(fgT + bf_ref[...])
        a1, a2, a3 = _split3(logf)
        tr = tri_ref[...]
        F = _nn(a1, tr) + _nn(a2, tr) + _nn(a3, tr) + carry[...]
        carry[...] = carry[...] + _lanesum(logf)
        p1, p2, p3 = _split3(F * LOG2E)
        n1, n2, n3 = _split3(F * (-LOG2E))
        for k, v in enumerate((p1, p2, p3, n1, n2, n3)):
            fs[k] = v.astype(F32)

        rowi = lax.broadcasted_iota(jnp.int32, (8, B), 0)
        zeros_tail = jnp.zeros((128 - DH - 8, B), F32)
        ones_row = jnp.where(lax.broadcasted_iota(jnp.int32, (KR - DH, B), 0) == 0, 1.0, 0.0).astype(BF16)
        for hh in range(H):
            sl = slice(hh * DH, (hh + 1) * DH)
            q = zT[sl]
            k = zT[A + hh * DH:A + (hh + 1) * DH]
            v = zT[2 * A + hh * DH:2 * A + (hh + 1) * DH]
            qh = q * lax.rsqrt(jnp.mean(q * q, axis=0, keepdims=True) + EPS) * qg_ref[...] * (0.125 * LOG2E)
            kh = k * lax.rsqrt(jnp.mean(k * k, axis=0, keepdims=True) + EPS) * kg_ref[...]
            frow = [fs[kk, hh:hh + 1, :] for kk in range(6)]
            qx = jnp.where(rowi < 3, 1.0, jnp.where(rowi == 3, frow[0], jnp.where(
                rowi == 4, frow[1], jnp.where(rowi == 5, frow[2], 0.0))))
            kx = jnp.where(rowi == 0, frow[3], jnp.where(rowi == 1, frow[4], jnp.where(
                rowi == 2, frow[5], jnp.where(rowi < 6, 1.0, 0.0))))
            tbuf[0:DH, :] = qh
            tbuf[DH:DH + 8, :] = qx
            tbuf[DH + 8:128, :] = zeros_tail
            qTa_ref[0, hh * 128:(hh + 1) * 128, :] = tbuf[...].astype(BF16)
            tbuf[0:DH, :] = kh
            tbuf[DH:DH + 8, :] = kx
            kaug_ref[hh] = tbuf[...].T.astype(BF16)
            kT_ref[0, hh * KR:hh * KR + DH, :] = kh.astype(BF16)
            kT_ref[0, hh * KR + DH:(hh + 1) * KR, :] = ones_row
            tbuf[0:DH, :] = v
            tbuf[DH:DH + 8, :] = jnp.zeros((8, B), F32)
            vaug_ref[hh] = tbuf[...].T.astype(BF16)
            vT_ref[0, sl, :] = v.astype(BF16)
            nq[hh:hh + 1, :] = jnp.max(_colsum(qh * qh), axis=1, keepdims=True)
            nk[hh:hh + 1, :] = jnp.max(_colsum(kh * kh), axis=1, keepdims=True)

        lane = lax.broadcasted_iota(jnp.int32, (8, 128), 1)
        stat_ref[0] = jnp.where(lane == 0, jnp.max(F, axis=1, keepdims=True), jnp.where(
            lane == 1, jnp.min(F, axis=1, keepdims=True), jnp.where(
                lane == 2, nq[...], jnp.where(lane == 3, nk[...], 0.0))))

    row = lambda w: pl.BlockSpec((B, w), lambda i: (i, 0))
    tsp = lambda r: pl.BlockSpec((1, r, B), lambda i: (i, 0, 0))
    return _pc(
        body, name="fwd_in", grid=(NB,),
        in_specs=[row(D), _const((1, D)), _const((1, D)), _const((1, D)), _const((ZT_ROWS, D)),
                  _const((D, 2 * CW)), _const((DH, 1)), _const((DH, 1)), _const((8, 1)), _const((B, B))],
        out_specs=[tsp(H * 128), tsp(H * KR), pl.BlockSpec((H, B, 128), lambda i: (0, i, 0)), tsp(A),
                   pl.BlockSpec((H, B, 128), lambda i: (0, i, 0)),
                   pl.BlockSpec((2 * A, B), lambda i: (0, i)), pl.BlockSpec((8, B), lambda i: (0, i)),
                   row(2 * CW), row(CW), pl.BlockSpec((1, 8, 128), lambda i: (i, 0, 0))],
        out_shape=[_sds((NB, H * 128, B), BF16), _sds((NB, H * KR, B), BF16), _sds((H, S, 128), BF16),
                   _sds((NB, A, B), BF16), _sds((H, S, 128), BF16), _sds((2 * A, S)), _sds((8, S)),
                   _sds((S, 2 * CW)), _sds((S, CW)), _sds((NB, 8, 128))],
        scratch_shapes=[pltpu.VMEM((8, 1), F32), pltpu.VMEM((128, B), F32), pltpu.VMEM((6, 8, B), F32),
                        pltpu.VMEM((8, 1), F32), pltpu.VMEM((8, 1), F32)],
        compiler_params=_cp(("arbitrary",)),
    )(x, n1g, sc1, sh1, wT, wr, qg_col, kg_col, bf_col, tri)


def _first_key_block(top, fmin_ref, cut, h, i):
    return lax.while_loop(
        lambda j: jnp.logical_and(j > 0, top - fmin_ref[h, jnp.maximum(j - 1, 0)] >= cut),
        lambda j: j - 1, i)


def _causal_keep(B):
    return lax.broadcasted_iota(jnp.int32, (B, B), 0) <= lax.broadcasted_iota(jnp.int32, (B, B), 1)


def attn_fwd(fmax, fmin, thr, qTa, kaug, vT, late_w):
    NB, _, B = qTa.shape
    S = NB * B

    def body(fmax_ref, fmin_ref, thr_ref, q_ref, k_ref, v_ref, w_ref, o_ref, lse_ref, gw_ref,
             s0, s1, m_ref, l_ref, acc_ref, send_sems, recv_sems, local_sem):
        h = pl.program_id(0)
        i = pl.program_id(1)

        @pl.when(jnp.logical_and(h == 0, i == 0))
        def _():
            _start_all(_direct_copies(False, w_ref, gw_ref, send_sems, recv_sems, local_sem))

        jlo = _first_key_block(fmax_ref[h, i], fmin_ref, thr_ref[h], h, i)
        n = i - jlo

        def scores(j, s_ref):
            s_ref[...] = _nn(k_ref[0, pl.ds(pl.multiple_of(j * B, B), B), :], q_ref[0])

        def softmax_step(s_ref, j, masked):
            s = s_ref[...]
            if masked:
                s = jnp.where(_causal_keep(B), s, -jnp.inf)
            m = m_ref[...]
            mn = jnp.maximum(m, jnp.max(s, axis=0, keepdims=True))
            a = jnp.exp2(m - mn)
            p = jnp.exp2(s - mn)
            m_ref[...] = mn
            l_ref[...] = a * l_ref[...] + _colsum(p)
            acc_ref[...] = a * acc_ref[...] + _nn(v_ref[j], p.astype(BF16))

        m_ref[...] = jnp.full((1, B), -jnp.inf, F32)
        l_ref[...] = jnp.zeros((1, B), F32)
        acc_ref[...] = jnp.zeros((DH, B), F32)
        scores(jlo, s0)

        def pair(t, carry):
            j = jlo + 2 * t
            scores(j + 1, s1)
            softmax_step(s0, j, False)
            scores(j + 2, s0)
            softmax_step(s1, j + 1, False)
            return carry

        lax.fori_loop(0, n // 2, pair, 0)

        @pl.when(n % 2 == 1)
        def _():
            scores(i, s1)
            softmax_step(s0, i - 1, False)
            softmax_step(s1, i, True)

        @pl.when(n % 2 == 0)
        def _():
            softmax_step(s0, i, True)

        l = l_ref[...]
        o_ref[0] = acc_ref[...] / l
        lse_ref[0, 0] = m_ref[...] + jnp.log2(l)

        @pl.when(jnp.logical_and(h == H - 1, i == NB - 1))
        def _():
            _wait_all(_direct_copies(False, w_ref, gw_ref, send_sems, recv_sems, local_sem))

    return _pc(
        body, name="attn_fwd", grid=(H, NB),
        in_specs=[SMEM, SMEM, SMEM, pl.BlockSpec((1, 128, B), lambda h, i: (i, h, 0)),
                  pl.BlockSpec((1, S, 128), lambda h, i: (h, 0, 0)),
                  pl.BlockSpec((NB, DH, B), lambda h, i: (0, h, 0)), ANY],
        out_specs=[pl.BlockSpec((1, DH, B), lambda h, i: (i, h, 0)),
                   pl.BlockSpec((1, 1, 1, B), lambda h, i: (h, i, 0, 0)), ANY],
        out_shape=[_sds((NB, A, B)), _sds((H, NB, 1, B)), _sds((N_DEV,) + late_w.shape, late_w.dtype)],
        scratch_shapes=[pltpu.VMEM((B, B), F32), pltpu.VMEM((B, B), F32), pltpu.VMEM((1, B), F32),
                        pltpu.VMEM((1, B), F32), pltpu.VMEM((DH, B), F32)] + COMM_SEMS,
        compiler_params=_cp(("arbitrary", "arbitrary")),
    )(fmax, fmin, thr, qTa, kaug, vT, late_w)


def attn_bwd(fmax, fmin, thr, qTa, kaug, kT, vaug, doTa, lse, delta, early_g):
    NB, _, B = qTa.shape
    QR = 80
    CH = min(256, B)

    def body(fmax_ref, fmin_ref, thr_ref, q_ref, ka_ref, kt_ref, va_ref, do_ref, lse_ref, dl_ref, g_ref,
             dq_ref, dk_ref, dv_ref, dfk_ref, recv_ref, s0, d0, s1, d1, p0, e0, p1, e1, dk_acc, dv_acc,
             send_sems, recv_sems, local_sem):
        h = pl.program_id(0)
        j = pl.program_id(1)

        @pl.when(jnp.logical_and(h == 0, j == 0))
        def _():
            _start_all(_direct_copies(True, g_ref, recv_ref, send_sems, recv_sems, local_sem))

        bottom = fmin_ref[h, j]
        cut = thr_ref[h]
        ihi = lax.while_loop(
            lambda i: jnp.logical_and(i < NB - 1, fmax_ref[h, jnp.minimum(i + 1, NB - 1)] - bottom >= cut),
            lambda i: i + 1, j)

        @pl.when(j == 0)
        def _():
            dq_ref[...] = jnp.zeros_like(dq_ref)

        dk_acc[...] = jnp.zeros_like(dk_acc)
        dv_acc[...] = jnp.zeros_like(dv_acc)

        n = ihi - j + 1

        def blk(k):
            return jnp.minimum(j + k, ihi)

        def products(k, s_ref, d_ref):
            i = blk(k)
            s_ref[...] = _nn(ka_ref[0], q_ref[i])
            d_ref[...] = _nn(va_ref[0], do_ref[i])

        def elementwise(k, s_ref, d_ref, p_ref, ds_ref, masked):
            i = blk(k)
            p = jnp.exp2(s_ref[...] - lse_ref[0, i])
            if masked:
                p = jnp.where(_causal_keep(B), p, 0.0)
            p_ref[...] = p.astype(BF16)
            ds_ref[...] = (p * (d_ref[...] - dl_ref[0, i])).astype(BF16)

        def grads(k, p_ref, ds_ref):
            i = j + k
            dsb = ds_ref[...]
            dv_acc[...] = dv_acc[...] + _nt(do_ref[i][0:DH], p_ref[...])
            dk_acc[...] = dk_acc[...] + _nt(q_ref[i][0:QR], dsb)
            dq_ref[i] = dq_ref[i] + _nn(kt_ref[0], dsb)

        def trip(kp, s_p, d_p, ke, s_e, d_e, p_e, e_e, kg, p_g, e_g):
            ip, ie, ig = blk(kp), blk(ke), j + kg
            lse_e = lse_ref[0, ie]
            dl_e = dl_ref[0, ie]
            dq_part = jnp.zeros((KR, B), F32)
            for c in range(B // CH):
                rows = slice(c * CH, (c + 1) * CH)
                s_p[rows, :] = _nn(ka_ref[0, rows, :], q_ref[ip])
                d_p[rows, :] = _nn(va_ref[0, rows, :], do_ref[ip])
                p = jnp.exp2(s_e[rows, :] - lse_e)
                p_e[rows, :] = p.astype(BF16)
                e_e[rows, :] = (p * (d_e[rows, :] - dl_e)).astype(BF16)
                dsb = e_g[rows, :]
                dv_acc[:, rows] = dv_acc[:, rows] + _nt(do_ref[ig][0:DH], p_g[rows, :])
                dk_acc[:, rows] = dk_acc[:, rows] + _nt(q_ref[ig][0:QR], dsb)
                dq_part = dq_part + _nn(kt_ref[0, :, rows], dsb)
            dq_ref[ig] = dq_ref[ig] + dq_part

        products(0, s0, d0)
        products(1, s1, d1)
        elementwise(0, s0, d0, p0, e0, True)

        def pair(t, carry):
            k = 2 * t
            trip(k + 2, s0, d0, k + 1, s1, d1, p1, e1, k, p0, e0)
            trip(k + 3, s1, d1, k + 2, s0, d0, p0, e0, k + 1, p1, e1)
            return carry

        lax.fori_loop(0, n // 2, pair, 0)

        @pl.when(n % 2 == 1)
        def _():
            grads(n - 1, p0, e0)

        dk_ref[0] = dk_acc[0:DH, :]
        dfk_ref[0, 0] = dk_acc[DH:DH + 8, :]
        dv_ref[0] = dv_acc[...]

        @pl.when(jnp.logical_and(h == H - 1, j == NB - 1))
        def _():
            _wait_all(_direct_copies(True, g_ref, recv_ref, send_sems, recv_sems, local_sem))

    per_kv = lambda r: pl.BlockSpec((1, r, B), lambda h, j: (j, h, 0))
    head_all = lambda r: pl.BlockSpec((NB, r, B), lambda h, j: (0, h, 0))
    aug = pl.BlockSpec((1, B, 128), lambda h, j: (h, j, 0))
    stat = pl.BlockSpec((1, NB, 1, B), lambda h, j: (h, 0, 0, 0))
    return _pc(
        body, name="attn_bwd", grid=(H, NB),
        in_specs=[SMEM, SMEM, SMEM, head_all(128), aug, per_kv(KR), aug, head_all(128), stat, stat, ANY],
        out_specs=[head_all(KR), per_kv(DH), per_kv(DH),
                   pl.BlockSpec((1, 1, 8, B), lambda h, j: (h, j, 0, 0)), ANY],
        out_shape=[_sds((NB, H * KR, B)), _sds((NB, A, B)), _sds((NB, A, B)), _sds((H, NB, 8, B)),
                   _sds(early_g.shape, early_g.dtype)],
        scratch_shapes=[pltpu.VMEM((B, B), F32)] * 4 + [pltpu.VMEM((B, B), BF16)] * 4
        + [pltpu.VMEM((QR, B), F32), pltpu.VMEM((DH, B), F32)] + COMM_SEMS,
        compiler_params=_cp(("arbitrary", "arbitrary")),
    )(fmax, fmin, thr, qTa, kaug, kT, vaug, doTa, lse, delta, early_g)


def _conv_tail(u1, lng, lnb, beta_c):
    mu = _rowmean(u1)
    d = u1 - mu
    rstd = lax.rsqrt(_rowmean(d * d) + EPS)
    xhat = d * rstd
    u2 = xhat * lng + lnb
    sg = _sigmoid(u2)
    u3 = u2 * sg
    rc = lax.rsqrt(_rowmean(u3 * u3) + EPS)
    n3 = u3 * rc
    return rstd, xhat, u2, sg, rc, n3, n3 * beta_c


def _attn_tail(oT, beta_a_col):
    ra = lax.rsqrt(jnp.mean(oT * oT, axis=0, keepdims=True) + EPS)
    ohat = oT * ra
    return ra, ohat, ohat * beta_a_col


CONV_ROWS = 32
CORR_ROWS = 16


def _fill_shifted(buf, sh):
    rows = sh.shape[1]
    for ph in range(8):
        sh[ph] = buf[pl.ds(ph, rows), :]


def _tap_rows(sh, r0, o, rows):
    return sh[o % 8, pl.ds(pl.multiple_of(r0 + 8 * (o // 8), 8), rows), :]


def _depthwise(sh, w_ref, offs, bias, out_ref, B):
    def chunk(ci, carry):
        r0 = pl.multiple_of(ci * CONV_ROWS, CONV_ROWS)
        acc = jnp.broadcast_to(bias, (CONV_ROWS, CW))
        for k, o in enumerate(offs):
            acc = acc + w_ref[k:k + 1, :] * _tap_rows(sh, r0, o, CONV_ROWS)
        out_ref[pl.ds(r0, CONV_ROWS), :] = acc
        return carry

    lax.fori_loop(0, B // CONV_ROWS, chunk, 0)


def _tap_gradients(sh, d_ref, offs, dw_ref, B):
    for g0 in range(0, len(offs), 8):
        ks = list(range(g0, min(g0 + 8, len(offs))))

        def chunk(ci, accs, ks=ks):
            r0 = pl.multiple_of(ci * CORR_ROWS, CORR_ROWS)
            d = d_ref[pl.ds(r0, CORR_ROWS), :]
            out = []
            for a, k in zip(accs, ks):
                pr = d * _tap_rows(sh, r0, offs[k], CORR_ROWS)
                out.append(a + pr[0:8] + pr[8:16])
            return tuple(out)

        accs = lax.fori_loop(0, B // CORR_ROWS, chunk, tuple(jnp.zeros((8, CW), F32) for _ in ks))
        for a, k in zip(accs, ks):
            dw_ref[k:k + 1, :] = dw_ref[k:k + 1, :] + _colsum(a)


def conv_merge_out(u0, x, oT, conv_w, conv_b, lng, lnb, beta_c, beta_a_col, wo, g1):
    S = x.shape[0]
    B = BLK
    NB = S // B

    def body(uc_ref, up_ref, x_ref, oT_ref, w_ref, cb_ref, lng_ref, lnb_ref, bc_ref, ba_ref, wo_ref,
             g1_ref, x1_ref, o_ref, u1_ref, ubuf, sh):
        i = pl.program_id(0)
        ubuf[0:HALO, :] = jnp.where(i > 0, up_ref[B - HALO:B, :], 0.0)
        ubuf[HALO:HALO + B, :] = uc_ref[...]
        ubuf[HALO + B:HALO + B + 8, :] = jnp.zeros((8, CW), F32)
        _fill_shifted(ubuf, sh)
        _depthwise(sh, w_ref, [HALO - (KC - 1) + k for k in range(KC)], cb_ref[...], u1_ref, B)
        mc = _conv_tail(u1_ref[...], lng_ref[...], lnb_ref[...], bc_ref[...])[-1]
        maT = _attn_tail(oT_ref[0], ba_ref[...])[-1]
        ma = maT.T
        o = _nn(ma.astype(BF16), wo_ref[0:A, :]) + _nn(mc.astype(BF16), wo_ref[A:D, :])
        o_ref[...] = o
        x1_ref[...] = x_ref[...] + g1_ref[...] * o

    row = lambda w: pl.BlockSpec((B, w), lambda i: (i, 0))
    return _pc(
        body, name="conv_merge_out", grid=(NB,),
        in_specs=[row(CW), pl.BlockSpec((B, CW), lambda i: (jnp.maximum(i - 1, 0), 0)), row(D),
                  pl.BlockSpec((1, A, B), lambda i: (i, 0, 0)), _const((32, CW)), _const((1, CW)),
                  _const((1, CW)), _const((1, CW)), _const((1, CW)), _const((A, 1)), _const((D, D)),
                  _const((1, D))],
        out_specs=[row(D), row(D), row(CW)],
        out_shape=[_sds((S, D)), _sds((S, D)), _sds((S, CW))],
        scratch_shapes=[pltpu.VMEM((B + HALO + 8, CW), F32), pltpu.VMEM((8, B + HALO, CW), F32)],
        compiler_params=_cp(("arbitrary",)),
    )(u0, u0, x, oT, conv_w, conv_b, lng, lnb, beta_c, beta_a_col, wo, g1)


def mlp_fwd_loss(x1, tgt, n2g, sc2, sh2, g2, w1, w2):
    S = x1.shape[0]
    B = BLK
    NB = S // B
    NC = FF // FCH

    def body(x1_ref, t_ref, g_ref, sc_ref, sh_ref, g2_ref, w1_ref, w2_ref, dy_ref, loss_ref, dg2_ref,
             hs, acc):
        i = pl.program_id(0)
        c = pl.program_id(1)

        @pl.when(jnp.logical_and(i == 0, c == 0))
        def _():
            loss_ref[...] = jnp.zeros_like(loss_ref)
            dg2_ref[...] = jnp.zeros_like(dg2_ref)

        @pl.when(c == 0)
        def _():
            hs[...] = _norm_mod(x1_ref[...], g_ref[...], sc_ref[...], sh_ref[...])[2].astype(BF16)
            acc[...] = jnp.zeros_like(acc)

        a = jnp.maximum(_nn(hs[...], w1_ref[...]), 0.0)
        acc[...] = acc[...] + _nn((a * a).astype(BF16), w2_ref[...])

        @pl.when(c == NC - 1)
        def _():
            y2 = acc[...]
            e = x1_ref[...] + g2_ref[...] * y2 - t_ref[...]
            dy = e * (1.0 / D)
            dy_ref[...] = dy
            loss_ref[...] = loss_ref[...] + 0.5 * _colsum(_rowmean(e * e))
            dg2_ref[...] = dg2_ref[...] + _colsum(dy * y2)

    row = pl.BlockSpec((B, D), lambda i, c: (i, 0))
    vec = pl.BlockSpec((1, D), lambda i, c: (0, 0))
    return _pc(
        body, name="mlp_fwd_loss", grid=(NB, NC),
        in_specs=[row, row, vec, vec, vec, vec, pl.BlockSpec((D, FCH), lambda i, c: (0, c)),
                  pl.BlockSpec((FCH, D), lambda i, c: (c, 0))],
        out_specs=[row, pl.BlockSpec((1, 1), lambda i, c: (0, 0)), vec],
        out_shape=[_sds((S, D)), _sds((1, 1)), _sds((1, D))],
        scratch_shapes=[pltpu.VMEM((B, D), BF16), pltpu.VMEM((B, D), F32)],
        compiler_params=_cp(("arbitrary", "arbitrary")),
    )(x1, tgt, n2g, sc2, sh2, g2, w1, w2)


def mlp_bwd(x1, dy, n2g, sc2, sh2, g2, w1, w2):
    S = x1.shape[0]
    B = BLK
    NB = S // B
    NC = FF // FCH

    def body(x1_ref, dy_ref, g_ref, sc_ref, sh_ref, g2_ref, w1_ref, w2_ref, dw1_ref, dw2t_ref, dh_ref):
        i = pl.program_id(1)

        @pl.when(i == 0)
        def _():
            dw1_ref[...] = jnp.zeros_like(dw1_ref)
            dw2t_ref[...] = jnp.zeros_like(dw2t_ref)

        h2 = _norm_mod(x1_ref[...], g_ref[...], sc_ref[...], sh_ref[...])[2]
        w1c = w1_ref[...]
        w2c = w2_ref[...]
        ra = jnp.maximum(_nn(h2.astype(BF16), w1c), 0.0)
        dy2 = dy_ref[...] * g2_ref[...]
        db = _nt(dy2.astype(BF16), w2c)
        dab = (db * (2.0 * ra)).astype(BF16)
        dh_ref[0] = _nt(dab, w1c)
        dw1_ref[...] = dw1_ref[...] + _nn(h2.T.astype(BF16), dab)
        dw2t_ref[...] = dw2t_ref[...] + _nn(dy2.T.astype(BF16), (ra * ra).astype(BF16))

    row = pl.BlockSpec((B, D), lambda c, i: (i, 0))
    vec = pl.BlockSpec((1, D), lambda c, i: (0, 0))
    wcol = pl.BlockSpec((D, FCH), lambda c, i: (0, c))
    return _pc(
        body, name="mlp_bwd", grid=(NC, NB),
        in_specs=[row, row, vec, vec, vec, vec, wcol, pl.BlockSpec((FCH, D), lambda c, i: (c, 0))],
        out_specs=[wcol, wcol, pl.BlockSpec((1, B, D), lambda c, i: (c, i, 0))],
        out_shape=[_sds((D, FF)), _sds((D, FF)), _sds((NC, S, D))],
        compiler_params=_cp(("arbitrary", "arbitrary")),
    )(x1, dy, n2g, sc2, sh2, g2, w1, w2)


def merge_bwd(dh2p, x1, dy, o, oT, u1, n2g, sc2, g1, lng, lnb, beta_c, beta_a_col, wo):
    S = x1.shape[0]
    B = BLK
    NB = S // B
    NC = dh2p.shape[0]

    def body(dh_ref, x1_ref, dy_ref, o_ref, oT_ref, u1_ref, g_ref, sc_ref, g1_ref, lng_ref, lnb_ref,
             bc_ref, ba_ref, wo_ref,
             dx1_ref, doTa_ref, dl_ref, du1_ref, dwo_ref, v1_ref, v2_ref, dba_ref):
        i = pl.program_id(0)

        @pl.when(i == 0)
        def _():
            dwo_ref[...] = jnp.zeros_like(dwo_ref)
            v1_ref[...] = jnp.zeros_like(v1_ref)
            v2_ref[...] = jnp.zeros_like(v2_ref)
            dba_ref[...] = jnp.zeros_like(dba_ref)

        dh2 = dh_ref[0]
        for cc in range(1, NC):
            dh2 = dh2 + dh_ref[cc]
        x1 = x1_ref[...]
        g = g_ref[...]
        sc = sc_ref[...]
        r2 = lax.rsqrt(_rowmean(x1 * x1) + EPS)
        xh = x1 * r2
        dhx = dh2 * xh
        v1_ref[0:1, :] = v1_ref[0:1, :] + _colsum(dh2)
        v1_ref[1:2, :] = v1_ref[1:2, :] + _colsum(dhx) * g
        v1_ref[2:3, :] = v1_ref[2:3, :] + _colsum(dhx) * (1.0 + sc)
        dxh = dh2 * (g * (1.0 + sc))
        dx1 = dy_ref[...] + r2 * (dxh - xh * _rowmean(dxh * xh))
        dx1_ref[...] = dx1
        v1_ref[3:4, :] = v1_ref[3:4, :] + _colsum(dx1 * o_ref[...])
        dob = (dx1 * g1_ref[...]).astype(BF16)

        lng = lng_ref[...]
        bc = bc_ref[...]
        rstd, xhat, u2, sg, rc, n3, mc = _conv_tail(u1_ref[...], lng, lnb_ref[...], bc)
        ba = ba_ref[...]
        oT = oT_ref[0]
        ra, ohat, maT = _attn_tail(oT, ba)
        dwo_ref[0:A, :] = dwo_ref[0:A, :] + _nn(maT.astype(BF16), dob)
        dwo_ref[A:D, :] = dwo_ref[A:D, :] + _nn(mc.T.astype(BF16), dob)
        dmaT = _nt(wo_ref[0:A, :], dob)
        dmc = _nt(dob, wo_ref[A:D, :])

        dba_ref[...] = dba_ref[...] + _lanesum(dmaT * ohat)
        dohat = dmaT * ba
        doT = ra * (dohat - ohat * jnp.mean(dohat * ohat, axis=0, keepdims=True))
        prod = doT * oT
        zpad = jnp.zeros((128 - DH, B), BF16)
        for hh in range(H):
            sl = slice(hh * DH, (hh + 1) * DH)
            dl_ref[hh, 0] = _colsum(prod[sl])
            doTa_ref[0, hh * 128:hh * 128 + DH, :] = doT[sl].astype(BF16)
            doTa_ref[0, hh * 128 + DH:(hh + 1) * 128, :] = zpad

        v2_ref[0:1, :] = v2_ref[0:1, :] + _colsum(dmc * n3)
        dn3 = dmc * bc
        du3 = rc * (dn3 - n3 * _rowmean(dn3 * n3))
        du2 = du3 * (sg * (1.0 + u2 * (1.0 - sg)))
        v2_ref[1:2, :] = v2_ref[1:2, :] + _colsum(du2 * xhat)
        v2_ref[2:3, :] = v2_ref[2:3, :] + _colsum(du2)
        dxhat = du2 * lng
        du1_ref[...] = rstd * (dxhat - _rowmean(dxhat) - xhat * _rowmean(dxhat * xhat))

    row = lambda w: pl.BlockSpec((B, w), lambda i: (i, 0))
    return _pc(
        body, name="merge_bwd", grid=(NB,),
        in_specs=[pl.BlockSpec((NC, B, D), lambda i: (0, i, 0)), row(D), row(D), row(D),
                  pl.BlockSpec((1, A, B), lambda i: (i, 0, 0)), row(CW), _const((1, D)), _const((1, D)),
                  _const((1, D)), _const((1, CW)), _const((1, CW)), _const((1, CW)), _const((A, 1)),
                  _const((D, D))],
        out_specs=[row(D), pl.BlockSpec((1, H * 128, B), lambda i: (i, 0, 0)),
                   pl.BlockSpec((H, 1, 1, B), lambda i: (0, i, 0, 0)), row(CW), _const((D, D)),
                   _const((8, D)), _const((8, CW)), _const((A, 1))],
        out_shape=[_sds((S, D)), _sds((NB, H * 128, B), BF16), _sds((H, NB, 1, B)), _sds((S, CW)),
                   _sds((D, D)), _sds((8, D)), _sds((8, CW)), _sds((A, 1))],
        compiler_params=_cp(("arbitrary",)),
    )(dh2p, x1, dy, o, oT, u1, n2g, sc2, g1, lng, lnb, beta_c, beta_a_col, wo)


def head_bwd(du1, u0, alg, zqk, fgT, dqT, dkT, dvT, dfk, conv_w, qg_col, kg_col, bf_col, tri_lo):
    S = u0.shape[0]
    B = BLK
    NB = S // B

    def body(dc_ref, dn_ref, uc_ref, up_ref, alg_ref, zqk_ref, fg_ref, dq_ref, dk_ref, dv_ref, dfk_ref,
             w_ref, qg_ref, kg_ref, bf_ref, tri_ref,
             dzr_ref, dzT_ref, dcw_ref, vc_ref, dqg_ref, dkg_ref, dbf_ref,
             buf, sh, du0_ref, carry, fbuf):
        pid = pl.program_id(0)
        ri = NB - 1 - pid

        @pl.when(pid == 0)
        def _():
            carry[...] = jnp.zeros_like(carry)
            dcw_ref[...] = jnp.zeros_like(dcw_ref)
            vc_ref[...] = jnp.zeros_like(vc_ref)
            dqg_ref[...] = jnp.zeros_like(dqg_ref)
            dkg_ref[...] = jnp.zeros_like(dkg_ref)
            dbf_ref[...] = jnp.zeros_like(dbf_ref)

        zero8 = jnp.zeros((8, CW), F32)
        buf[0:B, :] = dc_ref[...]
        buf[B:B + HALO, :] = jnp.where(ri < NB - 1, dn_ref[0:HALO, :], 0.0)
        buf[B + HALO:B + HALO + 8, :] = zero8
        _fill_shifted(buf, sh)
        _depthwise(sh, w_ref, [KC - 1 - k for k in range(KC)], jnp.zeros((1, CW), F32), du0_ref, B)
        buf[0:HALO, :] = jnp.where(ri > 0, up_ref[B - HALO:B, :], 0.0)
        buf[HALO:HALO + B, :] = uc_ref[...]
        buf[HALO + B:HALO + B + 8, :] = zero8
        _fill_shifted(buf, sh)
        _tap_gradients(sh, dc_ref, [HALO - (KC - 1) + k for k in range(KC)], dcw_ref, B)
        vc_ref[0:1, :] = vc_ref[0:1, :] + _colsum(dc_ref[...])

        du0 = du0_ref[...]
        al = alg_ref[:, 0:CW]
        sg = _sigmoid(alg_ref[:, CW:2 * CW])
        dzr_ref[:, 0:CW] = (du0 * sg).astype(BF16)
        dzr_ref[:, CW:2 * CW] = (du0 * al * sg * (1.0 - sg)).astype(BF16)

        dqg = jnp.zeros((DH, B), F32)
        dkg = jnp.zeros((DH, B), F32)
        qg = qg_ref[...]
        kg = kg_ref[...]
        for hh in range(H):
            sl = slice(hh * DH, (hh + 1) * DH)
            q = zqk_ref[sl, :]
            rq = lax.rsqrt(jnp.mean(q * q, axis=0, keepdims=True) + EPS)
            qn = q * rq
            dqh = dq_ref[0, hh * KR:hh * KR + DH, :] * 0.125
            dqg = dqg + dqh * qn
            dqn = dqh * qg
            dzT_ref[0, sl, :] = (rq * (dqn - qn * jnp.mean(dqn * qn, axis=0, keepdims=True))).astype(BF16)
            k = zqk_ref[A + hh * DH:A + (hh + 1) * DH, :]
            rk = lax.rsqrt(jnp.mean(k * k, axis=0, keepdims=True) + EPS)
            kn = k * rk
            dkh = dk_ref[0, sl, :] * (1.0 / LOG2E)
            dkg = dkg + dkh * kn
            dkn = dkh * kg
            dzT_ref[0, A + hh * DH:A + (hh + 1) * DH, :] = (
                rk * (dkn - kn * jnp.mean(dkn * kn, axis=0, keepdims=True))).astype(BF16)
            fbuf[hh:hh + 1, :] = dq_ref[0, hh * KR + DH:hh * KR + DH + 1, :] - dfk_ref[hh, 0, 0:1, :]
        dqg_ref[...] = dqg_ref[...] + _lanesum(dqg)
        dkg_ref[...] = dkg_ref[...] + _lanesum(dkg)
        dzT_ref[0, 2 * A:3 * A, :] = dv_ref[0].astype(BF16)

        dF = fbuf[...]
        a1, a2, a3 = _split3(dF)
        tr = tri_ref[...]
        dlogf = _nn(a1, tr) + _nn(a2, tr) + _nn(a3, tr) + carry[...]
        carry[...] = carry[...] + _lanesum(dF)
        dfg = dlogf * _sigmoid(-(fg_ref[...] + bf_ref[...]))
        dbf_ref[...] = dbf_ref[...] + _lanesum(dfg)
        dzT_ref[0, 3 * A:ZT_ROWS, :] = jnp.concatenate([dfg, jnp.zeros((8, B), F32)], axis=0).astype(BF16)

    rrow = lambda w: pl.BlockSpec((B, w), lambda p: (NB - 1 - p, 0))
    rts = lambda r: pl.BlockSpec((1, r, B), lambda p: (NB - 1 - p, 0, 0))
    return _pc(
        body, name="head_bwd", grid=(NB,),
        in_specs=[rrow(CW), pl.BlockSpec((B, CW), lambda p: (jnp.minimum(NB - p, NB - 1), 0)),
                  rrow(CW), pl.BlockSpec((B, CW), lambda p: (jnp.maximum(NB - 2 - p, 0), 0)),
                  rrow(2 * CW), pl.BlockSpec((2 * A, B), lambda p: (0, NB - 1 - p)),
                  pl.BlockSpec((8, B), lambda p: (0, NB - 1 - p)), rts(H * KR), rts(A), rts(A),
                  pl.BlockSpec((H, 1, 8, B), lambda p: (0, NB - 1 - p, 0, 0)),
                  _const((32, CW)), _const((DH, 1)), _const((DH, 1)), _const((8, 1)), _const((B, B))],
        out_specs=[rrow(2 * CW), rts(ZT_ROWS), _const((32, CW)), _const((8, CW)), _const((DH, 1)),
                   _const((DH, 1)), _const((8, 1))],
        out_shape=[_sds((S, 2 * CW), BF16), _sds((NB, ZT_ROWS, B), BF16), _sds((32, CW)), _sds((8, CW)),
                   _sds((DH, 1)), _sds((DH, 1)), _sds((8, 1))],
        scratch_shapes=[pltpu.VMEM((B + HALO + 8, CW), F32), pltpu.VMEM((8, B + HALO, CW), F32),
                        pltpu.VMEM((B, CW), F32), pltpu.VMEM((8, 1), F32), pltpu.VMEM((8, B), F32)],
        compiler_params=_cp(("arbitrary",)),
    )(du1, du1, u0, u0, alg, zqk, fgT, dqT, dkT, dvT, dfk, conv_w, qg_col, kg_col, bf_col, tri_lo)


def in_bwd(x, dx1, dzr, dzT, n1g, sc1, sh1, wT, w_qkvf, wr):
    S = x.shape[0]
    B = BLK
    NB = S // B

    def body(x_ref, dx1_ref, dzr_ref, dzT_ref, g_ref, sc_ref, sh_ref, wT_hbm, w_hbm, wr_hbm,
             gx_ref, dwT_hbm, dwr_hbm, v_ref, wT_v, w_v, wr_v, dwT_acc, dwr_acc, sem):
        i = pl.program_id(0)

        @pl.when(i == 0)
        def _():
            for src, dst in ((wT_hbm, wT_v), (w_hbm, w_v), (wr_hbm, wr_v)):
                cp = pltpu.make_async_copy(src, dst, sem)
                cp.start()
                cp.wait()
            dwT_acc[...] = jnp.zeros_like(dwT_acc)
            dwr_acc[...] = jnp.zeros_like(dwr_acc)
            v_ref[...] = jnp.zeros_like(v_ref)

        g = g_ref[...]
        sc = sc_ref[...]
        r1, xh, h1 = _norm_mod(x_ref[...], g, sc, sh_ref[...])
        hb = h1.astype(BF16)
        dzr = dzr_ref[...]
        dzT = dzT_ref[0]
        dwT_acc[...] = dwT_acc[...] + _nn(dzT, hb)
        dwr_acc[...] = dwr_acc[...] + _nn(h1.T.astype(BF16), dzr)
        dh1 = _nt(dzr, wr_v[...]) + _nn(w_v[...], dzT).T
        dhx = dh1 * xh
        v_ref[0:1, :] = v_ref[0:1, :] + _colsum(dh1)
        v_ref[1:2, :] = v_ref[1:2, :] + _colsum(dhx) * g
        v_ref[2:3, :] = v_ref[2:3, :] + _colsum(dhx) * (1.0 + sc)
        dxh = dh1 * (g * (1.0 + sc))
        gx_ref[...] = dx1_ref[...] + r1 * (dxh - xh * _rowmean(dxh * xh))

        @pl.when(i == NB - 1)
        def _():
            for src, dst in ((dwT_acc, dwT_hbm), (dwr_acc, dwr_hbm)):
                cp = pltpu.make_async_copy(src, dst, sem)
                cp.start()
                cp.wait()

    row = lambda w: pl.BlockSpec((B, w), lambda i: (i, 0))
    return _pc(
        body, name="in_bwd", grid=(NB,),
        in_specs=[row(D), row(D), row(2 * CW), pl.BlockSpec((1, ZT_ROWS, B), lambda i: (i, 0, 0)),
                  _const((1, D)), _const((1, D)), _const((1, D)), ANY, ANY, ANY],
        out_specs=[row(D), ANY, ANY, _const((8, D))],
        out_shape=[_sds((S, D)), _sds((ZT_ROWS, D)), _sds((D, 2 * CW)), _sds((8, D))],
        scratch_shapes=[pltpu.VMEM((ZT_ROWS, D), BF16), pltpu.VMEM((D, ZT_ROWS), BF16),
                        pltpu.VMEM((D, 2 * CW), BF16), pltpu.VMEM((ZT_ROWS, D), F32),
                        pltpu.VMEM((D, 2 * CW), F32), pltpu.SemaphoreType.DMA(())],
        compiler_params=_cp(("arbitrary",)),
    )(x, dx1, dzr, dzT, n1g, sc1, sh1, wT, w_qkvf, wr)


def _adam_math(g, w, m, v):
    m = ADAM_B1 * m + (1.0 - ADAM_B1) * g
    v = ADAM_B2 * v + (1.0 - ADAM_B2) * (g * g)
    m_hat = m / (1.0 - ADAM_B1 ** ADAM_STEP)
    v_hat = v / (1.0 - ADAM_B2 ** ADAM_STEP)
    delta = -ADAM_LR * (m_hat / (jnp.sqrt(v_hat) + ADAM_EPS) + ADAM_WD * w)
    return delta, m, v


def _row_tile(R):
    for t in (1024, 512, 256, 128, 64, 32, 16, 8):
        if R % t == 0:
            return t
    return R


def sum_slots(parts, name):
    K, R, C = parts.shape
    T = _row_tile(R)

    def body(p_ref, o_ref):
        s = p_ref[0]
        for k in range(1, K):
            s = s + p_ref[k]
        o_ref[...] = s

    return _pc(body, name=name, grid=(R // T,),
               in_specs=[pl.BlockSpec((K, T, C), lambda i: (0, i, 0))],
               out_specs=pl.BlockSpec((T, C), lambda i: (i, 0)), out_shape=_sds((R, C)),
               compiler_params=_cp(("arbitrary",)))(parts)


def adamw_slots(parts, w, m, v, name):
    K, R, C = parts.shape
    T = _row_tile(R)

    def body(p_ref, w_ref, m_ref, v_ref, g_ref, d_ref, nm_ref, nv_ref):
        g = p_ref[0]
        for k in range(1, K):
            g = g + p_ref[k]
        g_ref[...] = g
        d_ref[...], nm_ref[...], nv_ref[...] = _adam_math(g, w_ref[...], m_ref[...], v_ref[...])

    t2 = pl.BlockSpec((T, C), lambda i: (i, 0))
    return _pc(body, name=name, grid=(R // T,),
               in_specs=[pl.BlockSpec((K, T, C), lambda i: (0, i, 0)), t2, t2, t2],
               out_specs=[t2, t2, t2, t2], out_shape=[_sds((R, C))] * 4,
               compiler_params=_cp(("arbitrary",)))(parts, w, m, v)


def ada_grad_adamw(cT, dmod_cols, w, m, v):
    NCOL = w.shape[1]

    def body(cT_ref, dm_ref, w_ref, m_ref, v_ref, g_ref, d_ref, nm_ref, nv_ref):
        def term(b):
            cv = cT_ref[b]
            return (cv * _sigmoid(cv)) * dm_ref[b:b + 1, :]

        g = term(0)
        for b in range(1, N_DEV):
            g = g + term(b)
        g_ref[...] = g
        d_ref[...], nm_ref[...], nv_ref[...] = _adam_math(g, w_ref[...], m_ref[...], v_ref[...])

    full = _const((D, NCOL))
    return _pc(body, name="ada_grad_adamw", grid=(1,),
               in_specs=[_const((N_DEV, D, 1)), _const((N_DEV, NCOL)), full, full, full],
               out_specs=[full, full, full, full], out_shape=[_sds((D, NCOL))] * 4,
               compiler_params=_cp())(cT, dmod_cols, w, m, v)


def _pack_rows(vecs, rows=None):
    flat = jnp.concatenate([jnp.ravel(v) for v in vecs])
    n = flat.shape[0]
    r = -(-n // 1024) * 8 if rows is None else rows
    return jnp.pad(flat, (0, r * 128 - n)).reshape(r, 128)


def _unpack_rows(packed, shapes):
    flat = packed.reshape(-1)
    out, off = [], 0
    for s in shapes:
        n = 1
        for d in s:
            n *= d
        out.append(flat[off:off + n].reshape(s))
        off += n
    return out


def _cols_from_shards(g, rows, cols):
    return g.reshape(N_DEV, rows, cols).transpose(1, 0, 2).reshape(rows, N_DEV * cols)


def _cols_to_shards(w, cols):
    rows = w.shape[0]
    return w.reshape(rows, N_DEV, cols).transpose(1, 0, 2).reshape(N_DEV, rows * cols)


def kernel(x, c, w_ada, b_ada, norm1_g, w_in, q_norm_g, k_norm_g, b_f, conv_w, conv_b, conv_ln_g, conv_ln_b, beta_attn, beta_conv, w_out, norm2_g, w_ff1, w_ff2, loss_target, m_w_ada, m_b_ada, m_norm1_g, m_w_in, m_q_norm_g, m_k_norm_g, m_b_f, m_conv_w, m_conv_b, m_conv_ln_g, m_conv_ln_b, m_beta_attn, m_beta_conv, m_w_out, m_norm2_g, m_w_ff1, m_w_ff2, v_w_ada, v_b_ada, v_norm1_g, v_w_in, v_q_norm_g, v_k_norm_g, v_b_f, v_conv_w, v_conv_b, v_conv_ln_g, v_conv_ln_b, v_beta_attn, v_beta_conv, v_w_out, v_norm2_g, v_w_ff1, v_w_ff2):
    S = x.shape[1]
    B = BLK
    NB = S // B
    me = 4 * lax.axis_index("x") + 2 * lax.axis_index("y") + lax.axis_index("c")
    xs = x[0]
    tgt = loss_target[0]

    ADA_C, IN_C, FF_C, CV_C = w_ada.shape[2], w_in.shape[2], w_ff1.shape[2], conv_w.shape[2]
    OUT_R, FF_R = w_out.shape[1], w_ff2.shape[1]
    IN_W = N_DEV * IN_C

    n_ada, n_in, n_out, n_f1, n_f2 = D * ADA_C, D * IN_C, OUT_R * D, D * FF_C, FF_R * D
    early = jnp.concatenate([w_ada.reshape(-1), w_in.reshape(-1)]).astype(BF16)
    rows_e = -(-early.shape[0] // (128 * 16)) * 16
    early = jnp.pad(early, (0, rows_e * 128 - early.shape[0])).reshape(rows_e, 128)
    small0 = lax.bitcast_convert_type(_pack_rows([c, conv_w], rows=24), BF16).reshape(48, 128)
    ge = all_gather(jnp.concatenate([early, small0], axis=0), "ag_weights_early")
    gw = ge[:, 0:rows_e].reshape(N_DEV, rows_e * 128)
    gs0 = lax.bitcast_convert_type(ge[:, rows_e:].reshape(N_DEV, 24, 128, 2), F32).reshape(N_DEV, 24 * 128)
    late_w = jnp.concatenate([w_out.reshape(-1), w_ff1.reshape(-1), w_ff2.reshape(-1)]).astype(BF16)
    late_w = late_w.reshape(-1, 128)

    w_ada_full = _cols_from_shards(gw[:, 0:n_ada], D, ADA_C)
    w_in_full = _cols_from_shards(gw[:, n_ada:n_ada + n_in], D, IN_C)
    c_all = gs0[:, 0:D]
    conv_w_full = _cols_from_shards(gs0[:, D:D + KC * CV_C], KC, CV_C)
    conv_w_pad = jnp.pad(conv_w_full, ((0, 32 - KC), (0, 0)))

    w_qkvf = jnp.pad(w_in_full[:, 0:3 * A + H], ((0, 0), (0, ZT_ROWS - 3 * A - H)))
    wT = w_qkvf.T
    wr = w_in_full[:, 3 * A + H:]

    qg_col = q_norm_g.reshape(DH, 1)
    kg_col = k_norm_g.reshape(DH, 1)
    bf_col = b_f.reshape(H, 1)
    beta_a_col = beta_attn.reshape(A, 1)
    ii = lax.broadcasted_iota(jnp.int32, (B, B), 0)
    jj = lax.broadcasted_iota(jnp.int32, (B, B), 1)
    tri_up = (ii <= jj).astype(BF16)
    tri_lo = (ii >= jj).astype(BF16)

    mod = mod_kernel(c, w_ada_full, b_ada)
    sh1, sc1, g1, sh2, sc2, g2 = [mod[:, k * D:(k + 1) * D] for k in range(6)]
    qTa, kT, kaug, vT, vaug, zqk, fgT, alg, u0, stat = fwd_in(xs, norm1_g, sc1, sh1, wT, wr, qg_col,
                                                             kg_col, bf_col, tri_up)
    fmax = stat[:, :, 0].T
    fmin = stat[:, :, 1].T
    qk_max = jnp.sqrt(jnp.max(stat[:, :, 2], axis=0) * jnp.max(stat[:, :, 3], axis=0))
    thr = -(PRUNE + (2.02 / LOG2E) * qk_max)
    oT, lse, gl = attn_fwd(fmax, fmin, thr, qTa, kaug, vT, late_w)
    gl = gl.reshape(N_DEV, -1)
    wo_full = gl[:, 0:n_out].reshape(D, D)
    w1_full = _cols_from_shards(gl[:, n_out:n_out + n_f1], D, FF_C)
    w2_full = gl[:, n_out + n_f1:].reshape(FF, D)
    x1, o, u1 = conv_merge_out(u0, xs, oT, conv_w_pad, conv_b, conv_ln_g, conv_ln_b, beta_conv,
                               beta_a_col, wo_full, g1)
    dy, loss_part, dg2 = mlp_fwd_loss(x1, tgt, norm2_g, sc2, sh2, g2, w1_full, w2_full)

    dw1, dw2t, dh2p = mlp_bwd(x1, dy, norm2_g, sc2, sh2, g2, w1_full, w2_full)
    dx1, doTa, delta, du1, dwo, v1, v2, dba = merge_bwd(dh2p, x1, dy, o, oT, u1, norm2_g, sc2, g1,
                                                         conv_ln_g, conv_ln_b, beta_conv, beta_a_col, wo_full)
    early_g = jnp.concatenate([dwo.reshape(N_DEV, n_out), _cols_to_shards(dw1, FF_C),
                               dw2t.T.reshape(N_DEV, n_f2)], axis=1)
    rows_a = early_g.shape[1] // 128
    dqT, dkT, dvT, dfk, recv_a = attn_bwd(fmax, fmin, thr, qTa, kaug, kT, vaug, doTa, lse, delta,
                                          early_g.reshape(N_DEV, rows_a, 128))
    dzr, dzT, dcw, vc, dqg, dkg, dbf = head_bwd(du1, u0, alg, zqk, fgT, dqT, dkT, dvT, dfk, conv_w_pad,
                                                qg_col, kg_col, bf_col, tri_lo)
    grad_x, dwT, dwr, v0 = in_bwd(xs, dx1, dzr, dzT, norm1_g, sc1, sh1, wT, w_qkvf, wr)

    dw_in = jnp.concatenate([dwT.T[:, 0:3 * A + H], dwr], axis=1)
    late_g = _cols_to_shards(dw_in, IN_C)
    rows_b = -(-late_g.shape[1] // (128 * 256)) * 256
    late_g = jnp.pad(late_g, ((0, 0), (0, rows_b * 128 - late_g.shape[1])))
    recv_b = reduce_scatter_exchange(late_g.reshape(N_DEV, rows_b, 128), "rs_grads_in")

    dmod = jnp.concatenate([v0[0], v0[1], v1[3], v1[0], v1[1], dg2[0]])
    small1 = _pack_rows([dmod, v0[2], v1[2], dcw, vc[0], v2[1], v2[2], v2[0], dba, dqg, dkg,
                         jnp.pad(dbf.reshape(-1), (0, 120)), jnp.pad(loss_part.reshape(-1), (0, 127))])
    gs1 = all_gather(small1, "ag_small_bwd")
    tot = sum_slots(gs1, "sum_small")
    (g_b_ada, g_n1, g_n2, g_cw, g_cb, g_lng, g_lnb, g_bc, g_ba, g_qg, g_kg, g_bf, loss_v) = _unpack_rows(
        tot, [(1, 6 * D), (1, D), (1, D), (32, CW), (1, CW), (1, CW), (1, CW), (1, CW), (1, A),
              (1, DH), (1, DH), (1, 128), (1, 128)])
    loss = loss_v[0, 0]
    g_bf = g_bf[:, 0:H]
    g_cw_mine = lax.dynamic_slice(g_cw[0:KC], (0, me * CV_C), (KC, CV_C)).reshape(1, KC, CV_C)

    small_names = [(b_ada, m_b_ada, v_b_ada, g_b_ada), (norm1_g, m_norm1_g, v_norm1_g, g_n1),
                   (q_norm_g, m_q_norm_g, v_q_norm_g, g_qg), (k_norm_g, m_k_norm_g, v_k_norm_g, g_kg),
                   (b_f, m_b_f, v_b_f, g_bf), (conv_w, m_conv_w, v_conv_w, g_cw_mine),
                   (conv_b, m_conv_b, v_conv_b, g_cb), (conv_ln_g, m_conv_ln_g, v_conv_ln_g, g_lng),
                   (conv_ln_b, m_conv_ln_b, v_conv_ln_b, g_lnb), (beta_attn, m_beta_attn, v_beta_attn, g_ba),
                   (beta_conv, m_beta_conv, v_beta_conv, g_bc), (norm2_g, m_norm2_g, v_norm2_g, g_n2)]
    shapes_s = [t[0].shape for t in small_names]
    pw, pm, pv, pg = [_pack_rows([t[k] for t in small_names]) for k in range(4)]
    sg_, sd_, sm_, sv_ = adamw_slots(pg[None], pw, pm, pv, "adamw_small")
    sgs, sds, sms, svs = [_unpack_rows(a, shapes_s) for a in (sg_, sd_, sm_, sv_)]

    dmod_all = gs1[:, 0:48, :].reshape(N_DEV, N_DEV, ADA_C)
    dmod_cols = lax.dynamic_index_in_dim(dmod_all, me, axis=1, keepdims=False)
    ga, da, ma_, va_ = ada_grad_adamw(c_all.reshape(N_DEV, D, 1), dmod_cols, w_ada[0], m_w_ada[0], v_w_ada[0])

    res_a = adamw_slots(recv_a, *[_pack_rows(ws, rows=rows_a) for ws in (
        [w_out, w_ff1, w_ff2], [m_w_out, m_w_ff1, m_w_ff2], [v_w_out, v_w_ff1, v_w_ff2])], "adamw_late")
    res_b = adamw_slots(recv_b, *[_pack_rows(ws, rows=rows_b) for ws in ([w_in], [m_w_in], [v_w_in])],
                        "adamw_in")
    bgs, bds, bms, bvs = [
        _unpack_rows(b, [(1, D, IN_C)]) + _unpack_rows(a, [(1, OUT_R, D), (1, D, FF_C), (1, FF_R, D)])
        for a, b in zip(res_a, res_b)]

    def assemble(small, ada, bigs):
        (b_ada_, n1_, qg_, kg_, bf_, cw_, cb_, lng_, lnb_, ba_, bc_, n2_) = small
        return [ada.reshape(1, D, ADA_C), b_ada_, n1_, bigs[0], qg_, kg_, bf_, cw_, cb_, lng_, lnb_, ba_, bc_,
                bigs[1], n2_, bigs[2], bigs[3]]

    return (loss, grad_x.reshape(1, S, D), *assemble(sgs, ga, bgs), *assemble(sds, da, bds),
            *assemble(sms, ma_, bms), *assemble(svs, va_, bvs))
```

```python
import functools

import jax
import jax.numpy as jnp
from jax import lax
from jax.experimental import pallas as pl
from jax.experimental.pallas import tpu as pltpu

F32 = jnp.float32
BF16 = jnp.bfloat16

D = 1024
A = 512
CW = 512
H = 8
DH = 64
FF = 4096
KC = 31
HALO = 32
KR = 80
ZT_ROWS = 1552
FCH = 1024
EPS = 1e-6
BLK = 512
N_DEV = 8
VMEM_LIMIT = 56 * 1024 * 1024

ADAM_LR = 0.001
ADAM_B1 = 0.9
ADAM_B2 = 0.999
ADAM_EPS = 1e-08
ADAM_WD = 0.01
ADAM_STEP = 10

MESH = pl.DeviceIdType.MESH
ANY = pl.BlockSpec(memory_space=pl.ANY)
SMEM = pl.BlockSpec(memory_space=pltpu.SMEM)
PRUNE = 110.0
LOG2E = 1.4426950408889634


def _pc(body, **kw):
    return pl.pallas_call(body, **kw)


def _cp(sem=None):
    return pltpu.CompilerParams(dimension_semantics=sem, vmem_limit_bytes=VMEM_LIMIT)


def _sds(shape, dtype=F32):
    return jax.ShapeDtypeStruct(shape, dtype)


def _const(shape):
    n = len(shape)
    return pl.BlockSpec(shape, lambda *a: (0,) * n)


def _nt(a, b):
    return lax.dot_general(a, b, (((1,), (1,)), ((), ())), preferred_element_type=F32)


def _nn(a, b):
    return jnp.dot(a, b, preferred_element_type=F32)


def _sigmoid(v):
    return 1.0 / (1.0 + jnp.exp(-v))


def _split3(v):
    a1 = v.astype(BF16)
    r1 = v - a1.astype(F32)
    a2 = r1.astype(BF16)
    a3 = (r1 - a2.astype(F32)).astype(BF16)
    return a1, a2, a3


def _rowmean(v):
    return jnp.mean(v, axis=-1, keepdims=True)


def _colsum(v):
    return jnp.sum(v, axis=0, keepdims=True)


def _lanesum(v):
    return jnp.sum(v, axis=-1, keepdims=True)


def _coords():
    return lax.axis_index("x"), lax.axis_index("y"), lax.axis_index("c")


def all_gather(x, name):
    R, C = x.shape

    def body(x_ref, out_ref, send_sems, recv_sems, local_sem):
        mx, my, mc = _coords()
        me, sibling = (mx, my, mc), (mx, my, 1 - mc)
        chips = [(1 - mx, my), (mx, 1 - my), (1 - mx, 1 - my)]

        def rows(px, py, pc):
            return out_ref.at[4 * px + 2 * py + pc]

        def copy(k, block, to, src=None):
            return pltpu.make_async_remote_copy(
                src_ref=rows(*block) if src is None else src, dst_ref=rows(*block),
                send_sem=send_sems.at[k], recv_sem=recv_sems.at[k],
                device_id=to, device_id_type=MESH)

        mine = pltpu.make_async_copy(x_ref, rows(*me), local_sem)
        mine.start()
        first = [copy(0, me, sibling, src=x_ref)]
        first += [copy(1 + j, me, (*chip, mc), src=x_ref) for j, chip in enumerate(chips)]
        for cp in first:
            cp.start()
        passed = [copy(4 + j, (*chip, mc), sibling) for j, chip in enumerate(chips)]
        for j, chip in enumerate(chips):
            copy(1 + j, (*chip, mc), me).wait_recv()
            passed[j].start()
        copy(0, sibling, me).wait_recv()
        for j, chip in enumerate(chips):
            copy(4 + j, (*chip, 1 - mc), me).wait_recv()
        for cp in first + passed:
            cp.wait_send()
        mine.wait()

    return _pc(
        body, name=name, out_shape=_sds((N_DEV, R, C), x.dtype),
        in_specs=[ANY], out_specs=ANY,
        scratch_shapes=[pltpu.SemaphoreType.DMA((7,)), pltpu.SemaphoreType.DMA((7,)),
                        pltpu.SemaphoreType.DMA(())],
    )(x)


def _direct_copies(scatter, src_ref, dst_ref, send_sems, recv_sems, local_sem):
    mx, my, mc = _coords()
    me = 4 * mx + 2 * my + mc
    if scatter:
        local = pltpu.make_async_copy(src_ref.at[me], dst_ref.at[0], local_sem)
    else:
        local = pltpu.make_async_copy(src_ref, dst_ref.at[me], local_sem)
    remote = []
    for r in range(1, N_DEV):
        px = 1 - mx if r & 4 else mx
        py = 1 - my if r & 2 else my
        pcc = 1 - mc if r & 1 else mc
        remote.append(pltpu.make_async_remote_copy(
            src_ref=src_ref.at[4 * px + 2 * py + pcc] if scatter else src_ref,
            dst_ref=dst_ref.at[r] if scatter else dst_ref.at[me],
            send_sem=send_sems.at[r - 1], recv_sem=recv_sems.at[r - 1],
            device_id=(px, py, pcc), device_id_type=MESH))
    return [local] + remote


def _start_all(copies):
    for cp in copies:
        cp.start()


def _wait_all(copies):
    for cp in copies[1:]:
        cp.wait()
    copies[0].wait()


COMM_SEMS = [pltpu.SemaphoreType.DMA((7,)), pltpu.SemaphoreType.DMA((7,)), pltpu.SemaphoreType.DMA(())]


def reduce_scatter_exchange(g, name):
    def body(g_ref, recv_ref, send_sems, recv_sems, local_sem):
        copies = _direct_copies(True, g_ref, recv_ref, send_sems, recv_sems, local_sem)
        _start_all(copies)
        _wait_all(copies)

    return _pc(body, name=name, out_shape=_sds(g.shape, g.dtype), in_specs=[ANY], out_specs=ANY,
               scratch_shapes=COMM_SEMS)(g)


def mod_kernel(c, w_ada_b, b_ada):
    def body(c_ref, w_ref, b_ref, o_ref):
        cv = c_ref[...]
        sc = jnp.broadcast_to(cv * _sigmoid(cv), (8, D)).astype(BF16)
        o_ref[...] = _nn(sc, w_ref[...])[0:1, :] + b_ref[...]

    return _pc(body, name="mod", out_shape=_sds((1, 6 * D)),
               in_specs=[_const((1, D)), _const((D, 6 * D)), _const((1, 6 * D))],
               out_specs=_const((1, 6 * D)), grid=(1,), compiler_params=_cp())(c, w_ada_b, b_ada)


def _norm_mod(xv, g, sc, sh):
    r = lax.rsqrt(_rowmean(xv * xv) + EPS)
    xh = xv * r
    return r, xh, xh * (g * (1.0 + sc)) + sh


def _log_sigmoid(v):
    e = jnp.exp(-jnp.abs(v))
    l1p = jnp.where(e < 1e-4, e * (1.0 - 0.5 * e), jnp.log(1.0 + e))
    return jnp.minimum(v, 0.0) - l1p


def fwd_in(x, n1g, sc1, sh1, wT, wr, qg_col, kg_col, bf_col, tri):
    S = x.shape[0]
    B = BLK
    NB = S // B

    def body(x_ref, g_ref, sc_ref, sh_ref, wT_ref, wr_ref, qg_ref, kg_ref, bf_ref, tri_ref,
             qTa_ref, kT_ref, kaug_ref, vT_ref, vaug_ref, zqk_ref, fgT_ref, alg_ref, u0_ref, stat_ref,
             carry, tbuf, fs, nq, nk):
        i = pl.program_id(0)

        @pl.when(i == 0)
        def _():
            carry[...] = jnp.zeros_like(carry)

        _, _, h = _norm_mod(x_ref[...], g_ref[...], sc_ref[...], sh_ref[...])
        hb = h.astype(BF16)
        zT = _nt(wT_ref[...], hb)
        zr = _nn(hb, wr_ref[...])
        zqk_ref[...] = zT[0:2 * A]
        fgT = zT[3 * A:3 * A + 8]
        fgT_ref[...] = fgT
        alg_ref[...] = zr
        u0_ref[...] = zr[:, 0:CW] * _sigmoid(zr[:, CW:2 * CW])

        logf = _log_sigmoid(fgT + bf_ref[...])
        a1, a2, a3 = _split3(logf)
        tr = tri_ref[...]
        F = _nn(a1, tr) + _nn(a2, tr) + _nn(a3, tr) + carry[...]
        carry[...] = carry[...] + _lanesum(logf)
        p1, p2, p3 = _split3(F * LOG2E)
        n1, n2, n3 = _split3(F * (-LOG2E))
        for k, v in enumerate((p1, p2, p3, n1, n2, n3)):
            fs[k] = v.astype(F32)

        rowi = lax.broadcasted_iota(jnp.int32, (8, B), 0)
        zeros_tail = jnp.zeros((128 - DH - 8, B), F32)
        ones_row = jnp.where(lax.broadcasted_iota(jnp.int32, (KR - DH, B), 0) == 0, 1.0, 0.0).astype(BF16)
        for hh in range(H):
            sl = slice(hh * DH, (hh + 1) * DH)
            q = zT[sl]
            k = zT[A + hh * DH:A + (hh + 1) * DH]
            v = zT[2 * A + hh * DH:2 * A + (hh + 1) * DH]
            qh = q * lax.rsqrt(jnp.mean(q * q, axis=0, keepdims=True) + EPS) * qg_ref[...] * (0.125 * LOG2E)
            kh = k * lax.rsqrt(jnp.mean(k * k, axis=0, keepdims=True) + EPS) * kg_ref[...]
            frow = [fs[kk, hh:hh + 1, :] for kk in range(6)]
            qx = jnp.where(rowi < 3, 1.0, jnp.where(rowi == 3, frow[0], jnp.where(
                rowi == 4, frow[1], jnp.where(rowi == 5, frow[2], 0.0))))
            kx = jnp.where(rowi == 0, frow[3], jnp.where(rowi == 1, frow[4], jnp.where(
                rowi == 2, frow[5], jnp.where(rowi < 6, 1.0, 0.0))))
            tbuf[0:DH, :] = qh
            tbuf[DH:DH + 8, :] = qx
            tbuf[DH + 8:128, :] = zeros_tail
            qTa_ref[0, hh * 128:(hh + 1) * 128, :] = tbuf[...].astype(BF16)
            tbuf[0:DH, :] = kh
            tbuf[DH:DH + 8, :] = kx
            kaug_ref[hh] = tbuf[...].T.astype(BF16)
            kT_ref[0, hh * KR:hh * KR + DH, :] = kh.astype(BF16)
            kT_ref[0, hh * KR + DH:(hh + 1) * KR, :] = ones_row
            tbuf[0:DH, :] = v
            tbuf[DH:DH + 8, :] = jnp.zeros((8, B), F32)
            vaug_ref[hh] = tbuf[...].T.astype(BF16)
            vT_ref[0, sl, :] = v.astype(BF16)
            nq[hh:hh + 1, :] = jnp.max(_colsum(qh * qh), axis=1, keepdims=True)
            nk[hh:hh + 1, :] = jnp.max(_colsum(kh * kh), axis=1, keepdims=True)

        lane = lax.broadcasted_iota(jnp.int32, (8, 128), 1)
        stat_ref[0] = jnp.where(lane == 0, jnp.max(F, axis=1, keepdims=True), jnp.where(
            lane == 1, jnp.min(F, axis=1, keepdims=True), jnp.where(
                lane == 2, nq[...], jnp.where(lane == 3, nk[...], 0.0))))

    row = lambda w: pl.BlockSpec((B, w), lambda i: (i, 0))
    tsp = lambda r: pl.BlockSpec((1, r, B), lambda i: (i, 0, 0))
    return _pc(
        body, name="fwd_in", grid=(NB,),
        in_specs=[row(D), _const((1, D)), _const((1, D)), _const((1, D)), _const((ZT_ROWS, D)),
                  _const((D, 2 * CW)), _const((DH, 1)), _const((DH, 1)), _const((8, 1)), _const((B, B))],
        out_specs=[tsp(H * 128), tsp(H * KR), pl.BlockSpec((H, B, 128), lambda i: (0, i, 0)), tsp(A),
                   pl.BlockSpec((H, B, 128), lambda i: (0, i, 0)),
                   pl.BlockSpec((2 * A, B), lambda i: (0, i)), pl.BlockSpec((8, B), lambda i: (0, i)),
                   row(2 * CW), row(CW), pl.BlockSpec((1, 8, 128), lambda i: (i, 0, 0))],
        out_shape=[_sds((NB, H * 128, B), BF16), _sds((NB, H * KR, B), BF16), _sds((H, S, 128), BF16),
                   _sds((NB, A, B), BF16), _sds((H, S, 128), BF16), _sds((2 * A, S)), _sds((8, S)),
                   _sds((S, 2 * CW)), _sds((S, CW)), _sds((NB, 8, 128))],
        scratch_shapes=[pltpu.VMEM((8, 1), F32), pltpu.VMEM((128, B), F32), pltpu.VMEM((6, 8, B), F32),
                        pltpu.VMEM((8, 1), F32), pltpu.VMEM((8, 1), F32)],
        compiler_params=_cp(("arbitrary",)),
    )(x, n1g, sc1, sh1, wT, wr, qg_col, kg_col, bf_col, tri)


def _first_key_block(top, fmin_ref, cut, h, i):
    return lax.while_loop(
        lambda j: jnp.logical_and(j > 0, top - fmin_ref[h, jnp.maximum(j - 1, 0)] >= cut),
        lambda j: j - 1, i)


def _causal_keep(B):
    return lax.broadcasted_iota(jnp.int32, (B, B), 0) <= lax.broadcasted_iota(jnp.int32, (B, B), 1)


def attn_fwd(fmax, fmin, thr, qTa, kaug, vT, late_w):
    NB, _, B = qTa.shape
    S = NB * B

    def body(fmax_ref, fmin_ref, thr_ref, q_ref, k_ref, v_ref, w_ref, o_ref, lse_ref, gw_ref,
             s0, s1, m_ref, l_ref, acc_ref, send_sems, recv_sems, local_sem):
        h = pl.program_id(0)
        i = pl.program_id(1)

        @pl.when(jnp.logical_and(h == 0, i == 0))
        def _():
            _start_all(_direct_copies(False, w_ref, gw_ref, send_sems, recv_sems, local_sem))

        jlo = _first_key_block(fmax_ref[h, i], fmin_ref, thr_ref[h], h, i)
        n = i - jlo

        def scores(j, s_ref):
            s_ref[...] = _nn(k_ref[0, pl.ds(pl.multiple_of(j * B, B), B), :], q_ref[0])

        def softmax_step(s_ref, j, masked):
            s = s_ref[...]
            if masked:
                s = jnp.where(_causal_keep(B), s, -jnp.inf)
            m = m_ref[...]
            mn = jnp.maximum(m, jnp.max(s, axis=0, keepdims=True))
            a = jnp.exp2(m - mn)
            p = jnp.exp2(s - mn)
            m_ref[...] = mn
            l_ref[...] = a * l_ref[...] + _colsum(p)
            acc_ref[...] = a * acc_ref[...] + _nn(v_ref[j], p.astype(BF16))

        m_ref[...] = jnp.full((1, B), -jnp.inf, F32)
        l_ref[...] = jnp.zeros((1, B), F32)
        acc_ref[...] = jnp.zeros((DH, B), F32)
        scores(jlo, s0)

        def pair(t, carry):
            j = jlo + 2 * t
            scores(j + 1, s1)
            softmax_step(s0, j, False)
            scores(j + 2, s0)
            softmax_step(s1, j + 1, False)
            return carry

        lax.fori_loop(0, n // 2, pair, 0)

        @pl.when(n % 2 == 1)
        def _():
            scores(i, s1)
            softmax_step(s0, i - 1, False)
            softmax_step(s1, i, True)

        @pl.when(n % 2 == 0)
        def _():
            softmax_step(s0, i, True)

        l = l_ref[...]
        o_ref[0] = acc_ref[...] / l
        lse_ref[0, 0] = m_ref[...] + jnp.log2(l)

        @pl.when(jnp.logical_and(h == H - 1, i == NB - 1))
        def _():
            _wait_all(_direct_copies(False, w_ref, gw_ref, send_sems, recv_sems, local_sem))

    return _pc(
        body, name="attn_fwd", grid=(H, NB),
        in_specs=[SMEM, SMEM, SMEM, pl.BlockSpec((1, 128, B), lambda h, i: (i, h, 0)),
                  pl.BlockSpec((1, S, 128), lambda h, i: (h, 0, 0)),
                  pl.BlockSpec((NB, DH, B), lambda h, i: (0, h, 0)), ANY],
        out_specs=[pl.BlockSpec((1, DH, B), lambda h, i: (i, h, 0)),
                   pl.BlockSpec((1, 1, 1, B), lambda h, i: (h, i, 0, 0)), ANY],
        out_shape=[_sds((NB, A, B)), _sds((H, NB, 1, B)), _sds((N_DEV,) + late_w.shape, late_w.dtype)],
        scratch_shapes=[pltpu.VMEM((B, B), F32), pltpu.VMEM((B, B), F32), pltpu.VMEM((1, B), F32),
                        pltpu.VMEM((1, B), F32), pltpu.VMEM((DH, B), F32)] + COMM_SEMS,
        compiler_params=_cp(("arbitrary", "arbitrary")),
    )(fmax, fmin, thr, qTa, kaug, vT, late_w)


def attn_bwd(fmax, fmin, thr, qTa, kaug, kT, vaug, doTa, lse, delta, early_g):
    NB, _, B = qTa.shape
    QR = 80
    CH = min(256, B)

    def body(fmax_ref, fmin_ref, thr_ref, q_ref, ka_ref, kt_ref, va_ref, do_ref, lse_ref, dl_ref, g_ref,
             dq_ref, dk_ref, dv_ref, dfk_ref, recv_ref, s0, d0, s1, d1, p0, e0, p1, e1, dk_acc, dv_acc,
             send_sems, recv_sems, local_sem):
        h = pl.program_id(0)
        j = pl.program_id(1)

        @pl.when(jnp.logical_and(h == 0, j == 0))
        def _():
            _start_all(_direct_copies(True, g_ref, recv_ref, send_sems, recv_sems, local_sem))

        bottom = fmin_ref[h, j]
        cut = thr_ref[h]
        ihi = lax.while_loop(
            lambda i: jnp.logical_and(i < NB - 1, fmax_ref[h, jnp.minimum(i + 1, NB - 1)] - bottom >= cut),
            lambda i: i + 1, j)

        @pl.when(j == 0)
        def _():
            dq_ref[...] = jnp.zeros_like(dq_ref)

        dk_acc[...] = jnp.zeros_like(dk_acc)
        dv_acc[...] = jnp.zeros_like(dv_acc)

        n = ihi - j + 1

        def blk(k):
            return jnp.minimum(j + k, ihi)

        def products(k, s_ref, d_ref):
            i = blk(k)
            s_ref[...] = _nn(ka_ref[0], q_ref[i])
            d_ref[...] = _nn(va_ref[0], do_ref[i])

        def elementwise(k, s_ref, d_ref, p_ref, ds_ref, masked):
            i = blk(k)
            p = jnp.exp2(s_ref[...] - lse_ref[0, i])
            if masked:
                p = jnp.where(_causal_keep(B), p, 0.0)
            p_ref[...] = p.astype(BF16)
            ds_ref[...] = (p * (d_ref[...] - dl_ref[0, i])).astype(BF16)

        def grads(k, p_ref, ds_ref):
            i = j + k
            dsb = ds_ref[...]
            dv_acc[...] = dv_acc[...] + _nt(do_ref[i][0:DH], p_ref[...])
            dk_acc[...] = dk_acc[...] + _nt(q_ref[i][0:QR], dsb)
            dq_ref[i] = dq_ref[i] + _nn(kt_ref[0], dsb)

        def trip(kp, s_p, d_p, ke, s_e, d_e, p_e, e_e, kg, p_g, e_g):
            ip, ie, ig = blk(kp), blk(ke), j + kg
            lse_e = lse_ref[0, ie]
            dl_e = dl_ref[0, ie]
            dq_part = jnp.zeros((KR, B), F32)
            for c in range(B // CH):
                rows = slice(c * CH, (c + 1) * CH)
                s_p[rows, :] = _nn(ka_ref[0, rows, :], q_ref[ip])
                d_p[rows, :] = _nn(va_ref[0, rows, :], do_ref[ip])
                p = jnp.exp2(s_e[rows, :] - lse_e)
                p_e[rows, :] = p.astype(BF16)
                e_e[rows, :] = (p * (d_e[rows, :] - dl_e)).astype(BF16)
                dsb = e_g[rows, :]
                dv_acc[:, rows] = dv_acc[:, rows] + _nt(do_ref[ig][0:DH], p_g[rows, :])
                dk_acc[:, rows] = dk_acc[:, rows] + _nt(q_ref[ig][0:QR], dsb)
                dq_part = dq_part + _nn(kt_ref[0, :, rows], dsb)
            dq_ref[ig] = dq_ref[ig] + dq_part

        products(0, s0, d0)
        products(1, s1, d1)
        elementwise(0, s0, d0, p0, e0, True)

        def pair(t, carry):
            k = 2 * t
            trip(k + 2, s0, d0, k + 1, s1, d1, p1, e1, k, p0, e0)
            trip(k + 3, s1, d1, k + 2, s0, d0, p0, e0, k + 1, p1, e1)
            return carry

        n_pairs = n // 2
        lax.fori_loop(0, n_pairs - 1, pair, 0)
        k_last = 2 * (n_pairs - 1)

        @pl.when(jnp.logical_and(n_pairs >= 1, n % 2 == 0))
        def _():
            elementwise(k_last + 1, s1, d1, p1, e1, False)
            grads(k_last, p0, e0)
            grads(k_last + 1, p1, e1)

        @pl.when(jnp.logical_and(n_pairs >= 1, n % 2 == 1))
        def _():
            trip(k_last + 2, s0, d0, k_last + 1, s1, d1, p1, e1, k_last, p0, e0)
            elementwise(k_last + 2, s0, d0, p0, e0, False)
            grads(k_last + 1, p1, e1)
            grads(k_last + 2, p0, e0)

        @pl.when(n == 1)
        def _():
            grads(0, p0, e0)

        dk_ref[0] = dk_acc[0:DH, :]
        dfk_ref[0, 0] = dk_acc[DH:DH + 8, :]
        dv_ref[0] = dv_acc[...]

        @pl.when(jnp.logical_and(h == H - 1, j == NB - 1))
        def _():
            _wait_all(_direct_copies(True, g_ref, recv_ref, send_sems, recv_sems, local_sem))

    per_kv = lambda r: pl.BlockSpec((1, r, B), lambda h, j: (j, h, 0))
    head_all = lambda r: pl.BlockSpec((NB, r, B), lambda h, j: (0, h, 0))
    aug = pl.BlockSpec((1, B, 128), lambda h, j: (h, j, 0))
    stat = pl.BlockSpec((1, NB, 1, B), lambda h, j: (h, 0, 0, 0))
    return _pc(
        body, name="attn_bwd", grid=(H, NB),
        in_specs=[SMEM, SMEM, SMEM, head_all(128), aug, per_kv(KR), aug, head_all(128), stat, stat, ANY],
        out_specs=[head_all(KR), per_kv(DH), per_kv(DH),
                   pl.BlockSpec((1, 1, 8, B), lambda h, j: (h, j, 0, 0)), ANY],
        out_shape=[_sds((NB, H * KR, B)), _sds((NB, A, B)), _sds((NB, A, B)), _sds((H, NB, 8, B)),
                   _sds(early_g.shape, early_g.dtype)],
        scratch_shapes=[pltpu.VMEM((B, B), F32)] * 4 + [pltpu.VMEM((B, B), BF16)] * 4
        + [pltpu.VMEM((QR, B), F32), pltpu.VMEM((DH, B), F32)] + COMM_SEMS,
        compiler_params=_cp(("arbitrary", "arbitrary")),
    )(fmax, fmin, thr, qTa, kaug, kT, vaug, doTa, lse, delta, early_g)


def _conv_tail(u1, lng, lnb, beta_c):
    mu = _rowmean(u1)
    d = u1 - mu
    rstd = lax.rsqrt(_rowmean(d * d) + EPS)
    xhat = d * rstd
    u2 = xhat * lng + lnb
    sg = _sigmoid(u2)
    u3 = u2 * sg
    rc = lax.rsqrt(_rowmean(u3 * u3) + EPS)
    n3 = u3 * rc
    return rstd, xhat, u2, sg, rc, n3, n3 * beta_c


def _attn_tail(oT, beta_a_col):
    ra = lax.rsqrt(jnp.mean(oT * oT, axis=0, keepdims=True) + EPS)
    ohat = oT * ra
    return ra, ohat, ohat * beta_a_col


CONV_ROWS = 32
CORR_ROWS = 16


def _fill_shifted(buf, sh):
    rows = sh.shape[1]
    for ph in range(8):
        sh[ph] = buf[pl.ds(ph, rows), :]


def _tap_rows(sh, r0, o, rows):
    return sh[o % 8, pl.ds(pl.multiple_of(r0 + 8 * (o // 8), 8), rows), :]


def _depthwise(sh, w_ref, offs, bias, out_ref, B):
    def chunk(ci, carry):
        r0 = pl.multiple_of(ci * CONV_ROWS, CONV_ROWS)
        acc = jnp.broadcast_to(bias, (CONV_ROWS, CW))
        for k, o in enumerate(offs):
            acc = acc + w_ref[k:k + 1, :] * _tap_rows(sh, r0, o, CONV_ROWS)
        out_ref[pl.ds(r0, CONV_ROWS), :] = acc
        return carry

    lax.fori_loop(0, B // CONV_ROWS, chunk, 0)


def _tap_gradients(sh, d_ref, offs, dw_ref, B):
    for g0 in range(0, len(offs), 8):
        ks = list(range(g0, min(g0 + 8, len(offs))))

        def chunk(ci, accs, ks=ks):
            r0 = pl.multiple_of(ci * CORR_ROWS, CORR_ROWS)
            d = d_ref[pl.ds(r0, CORR_ROWS), :]
            out = []
            for a, k in zip(accs, ks):
                pr = d * _tap_rows(sh, r0, offs[k], CORR_ROWS)
                out.append(a + pr[0:8] + pr[8:16])
            return tuple(out)

        accs = lax.fori_loop(0, B // CORR_ROWS, chunk, tuple(jnp.zeros((8, CW), F32) for _ in ks))
        for a, k in zip(accs, ks):
            dw_ref[k:k + 1, :] = dw_ref[k:k + 1, :] + _colsum(a)


def conv_merge_out(u0, x, oT, conv_w, conv_b, lng, lnb, beta_c, beta_a_col, wo, g1):
    S = x.shape[0]
    B = BLK
    NB = S // B

    def body(uc_ref, up_ref, x_ref, oT_ref, w_ref, cb_ref, lng_ref, lnb_ref, bc_ref, ba_ref, wo_ref,
             g1_ref, x1_ref, o_ref, u1_ref, ubuf, sh):
        i = pl.program_id(0)
        ubuf[0:HALO, :] = jnp.where(i > 0, up_ref[B - HALO:B, :], 0.0)
        ubuf[HALO:HALO + B, :] = uc_ref[...]
        ubuf[HALO + B:HALO + B + 8, :] = jnp.zeros((8, CW), F32)
        _fill_shifted(ubuf, sh)
        _depthwise(sh, w_ref, [HALO - (KC - 1) + k for k in range(KC)], cb_ref[...], u1_ref, B)
        mc = _conv_tail(u1_ref[...], lng_ref[...], lnb_ref[...], bc_ref[...])[-1]
        maT = _attn_tail(oT_ref[0], ba_ref[...])[-1]
        ma = maT.T
        o = _nn(ma.astype(BF16), wo_ref[0:A, :]) + _nn(mc.astype(BF16), wo_ref[A:D, :])
        o_ref[...] = o
        x1_ref[...] = x_ref[...] + g1_ref[...] * o

    row = lambda w: pl.BlockSpec((B, w), lambda i: (i, 0))
    return _pc(
        body, name="conv_merge_out", grid=(NB,),
        in_specs=[row(CW), pl.BlockSpec((B, CW), lambda i: (jnp.maximum(i - 1, 0), 0)), row(D),
                  pl.BlockSpec((1, A, B), lambda i: (i, 0, 0)), _const((32, CW)), _const((1, CW)),
                  _const((1, CW)), _const((1, CW)), _const((1, CW)), _const((A, 1)), _const((D, D)),
                  _const((1, D))],
        out_specs=[row(D), row(D), row(CW)],
        out_shape=[_sds((S, D)), _sds((S, D)), _sds((S, CW))],
        scratch_shapes=[pltpu.VMEM((B + HALO + 8, CW), F32), pltpu.VMEM((8, B + HALO, CW), F32)],
        compiler_params=_cp(("arbitrary",)),
    )(u0, u0, x, oT, conv_w, conv_b, lng, lnb, beta_c, beta_a_col, wo, g1)


def mlp_fwd_loss(x1, tgt, n2g, sc2, sh2, g2, w1, w2):
    S = x1.shape[0]
    B = BLK
    NB = S // B
    NC = FF // FCH

    def body(x1_ref, t_ref, g_ref, sc_ref, sh_ref, g2_ref, w1_ref, w2_ref, dy_ref, loss_ref, dg2_ref, ra_ref,
             hs, acc):
        i = pl.program_id(0)
        c = pl.program_id(1)

        @pl.when(jnp.logical_and(i == 0, c == 0))
        def _():
            loss_ref[...] = jnp.zeros_like(loss_ref)
            dg2_ref[...] = jnp.zeros_like(dg2_ref)

        @pl.when(c == 0)
        def _():
            hs[...] = _norm_mod(x1_ref[...], g_ref[...], sc_ref[...], sh_ref[...])[2].astype(BF16)
            acc[...] = jnp.zeros_like(acc)

        a = jnp.maximum(_nn(hs[...], w1_ref[...]), 0.0)
        ra_ref[...] = a.astype(BF16)
        acc[...] = acc[...] + _nn((a * a).astype(BF16), w2_ref[...])

        @pl.when(c == NC - 1)
        def _():
            y2 = acc[...]
            e = x1_ref[...] + g2_ref[...] * y2 - t_ref[...]
            dy = e * (1.0 / D)
            dy_ref[...] = dy
            loss_ref[...] = loss_ref[...] + 0.5 * _colsum(_rowmean(e * e))
            dg2_ref[...] = dg2_ref[...] + _colsum(dy * y2)

    row = pl.BlockSpec((B, D), lambda i, c: (i, 0))
    vec = pl.BlockSpec((1, D), lambda i, c: (0, 0))
    return _pc(
        body, name="mlp_fwd_loss", grid=(NB, NC),
        in_specs=[row, row, vec, vec, vec, vec, pl.BlockSpec((D, FCH), lambda i, c: (0, c)),
                  pl.BlockSpec((FCH, D), lambda i, c: (c, 0))],
        out_specs=[row, pl.BlockSpec((1, 1), lambda i, c: (0, 0)), vec,
                   pl.BlockSpec((B, FCH), lambda i, c: (i, c))],
        out_shape=[_sds((S, D)), _sds((1, 1)), _sds((1, D)), _sds((S, FF), BF16)],
        scratch_shapes=[pltpu.VMEM((B, D), BF16), pltpu.VMEM((B, D), F32)],
        compiler_params=_cp(("arbitrary", "arbitrary")),
    )(x1, tgt, n2g, sc2, sh2, g2, w1, w2)


def mlp_bwd(x1, dy, ra, n2g, sc2, sh2, g2, w1, w2):
    S = x1.shape[0]
    B = BLK
    NB = S // B
    NC = FF // FCH

    def body(x1_ref, dy_ref, ra_ref, g_ref, sc_ref, sh_ref, g2_ref, w1_ref, w2_ref, dw1_ref, dw2t_ref, dh_ref):
        i = pl.program_id(1)

        @pl.when(i == 0)
        def _():
            dw1_ref[...] = jnp.zeros_like(dw1_ref)
            dw2t_ref[...] = jnp.zeros_like(dw2t_ref)

        h2 = _norm_mod(x1_ref[...], g_ref[...], sc_ref[...], sh_ref[...])[2]
        w1c = w1_ref[...]
        w2c = w2_ref[...]
        ra = ra_ref[...].astype(F32)
        dy2 = dy_ref[...] * g2_ref[...]
        db = _nt(dy2.astype(BF16), w2c)
        dab = (db * (2.0 * ra)).astype(BF16)
        dh_ref[0] = _nt(dab, w1c)
        dw1_ref[...] = dw1_ref[...] + _nn(h2.T.astype(BF16), dab)
        dw2t_ref[...] = dw2t_ref[...] + _nn(dy2.T.astype(BF16), (ra * ra).astype(BF16))

    row = pl.BlockSpec((B, D), lambda c, i: (i, 0))
    vec = pl.BlockSpec((1, D), lambda c, i: (0, 0))
    wcol = pl.BlockSpec((D, FCH), lambda c, i: (0, c))
    return _pc(
        body, name="mlp_bwd", grid=(NC, NB),
        in_specs=[row, row, pl.BlockSpec((B, FCH), lambda c, i: (i, c)), vec, vec, vec, vec, wcol,
                  pl.BlockSpec((FCH, D), lambda c, i: (c, 0))],
        out_specs=[wcol, wcol, pl.BlockSpec((1, B, D), lambda c, i: (c, i, 0))],
        out_shape=[_sds((D, FF)), _sds((D, FF)), _sds((NC, S, D))],
        compiler_params=_cp(("arbitrary", "arbitrary")),
    )(x1, dy, ra, n2g, sc2, sh2, g2, w1, w2)


def merge_bwd(dh2p, x1, dy, o, oT, u1, n2g, sc2, g1, lng, lnb, beta_c, beta_a_col, wo):
    S = x1.shape[0]
    B = BLK
    NB = S // B
    NC = dh2p.shape[0]

    def body(dh_ref, x1_ref, dy_ref, o_ref, oT_ref, u1_ref, g_ref, sc_ref, g1_ref, lng_ref, lnb_ref,
             bc_ref, ba_ref, wo_ref,
             dx1_ref, doTa_ref, dl_ref, du1_ref, dwo_ref, v1_ref, v2_ref, dba_ref):
        i = pl.program_id(0)

        @pl.when(i == 0)
        def _():
            dwo_ref[...] = jnp.zeros_like(dwo_ref)
            v1_ref[...] = jnp.zeros_like(v1_ref)
            v2_ref[...] = jnp.zeros_like(v2_ref)
            dba_ref[...] = jnp.zeros_like(dba_ref)

        dh2 = dh_ref[0]
        for cc in range(1, NC):
            dh2 = dh2 + dh_ref[cc]
        x1 = x1_ref[...]
        g = g_ref[...]
        sc = sc_ref[...]
        r2 = lax.rsqrt(_rowmean(x1 * x1) + EPS)
        xh = x1 * r2
        dhx = dh2 * xh
        v1_ref[0:1, :] = v1_ref[0:1, :] + _colsum(dh2)
        v1_ref[1:2, :] = v1_ref[1:2, :] + _colsum(dhx) * g
        v1_ref[2:3, :] = v1_ref[2:3, :] + _colsum(dhx) * (1.0 + sc)
        dxh = dh2 * (g * (1.0 + sc))
        dx1 = dy_ref[...] + r2 * (dxh - xh * _rowmean(dxh * xh))
        dx1_ref[...] = dx1
        v1_ref[3:4, :] = v1_ref[3:4, :] + _colsum(dx1 * o_ref[...])
        dob = (dx1 * g1_ref[...]).astype(BF16)

        lng = lng_ref[...]
        bc = bc_ref[...]
        rstd, xhat, u2, sg, rc, n3, mc = _conv_tail(u1_ref[...], lng, lnb_ref[...], bc)
        ba = ba_ref[...]
        oT = oT_ref[0]
        ra, ohat, maT = _attn_tail(oT, ba)
        dwo_ref[0:A, :] = dwo_ref[0:A, :] + _nn(maT.astype(BF16), dob)
        dwo_ref[A:D, :] = dwo_ref[A:D, :] + _nn(mc.T.astype(BF16), dob)
        dmaT = _nt(wo_ref[0:A, :], dob)
        dmc = _nt(dob, wo_ref[A:D, :])

        dba_ref[...] = dba_ref[...] + _lanesum(dmaT * ohat)
        dohat = dmaT * ba
        doT = ra * (dohat - ohat * jnp.mean(dohat * ohat, axis=0, keepdims=True))
        prod = doT * oT
        zpad = jnp.zeros((128 - DH, B), BF16)
        for hh in range(H):
            sl = slice(hh * DH, (hh + 1) * DH)
            dl_ref[hh, 0] = _colsum(prod[sl])
            doTa_ref[0, hh * 128:hh * 128 + DH, :] = doT[sl].astype(BF16)
            doTa_ref[0, hh * 128 + DH:(hh + 1) * 128, :] = zpad

        v2_ref[0:1, :] = v2_ref[0:1, :] + _colsum(dmc * n3)
        dn3 = dmc * bc
        du3 = rc * (dn3 - n3 * _rowmean(dn3 * n3))
        du2 = du3 * (sg * (1.0 + u2 * (1.0 - sg)))
        v2_ref[1:2, :] = v2_ref[1:2, :] + _colsum(du2 * xhat)
        v2_ref[2:3, :] = v2_ref[2:3, :] + _colsum(du2)
        dxhat = du2 * lng
        du1_ref[...] = rstd * (dxhat - _rowmean(dxhat) - xhat * _rowmean(dxhat * xhat))

    row = lambda w: pl.BlockSpec((B, w), lambda i: (i, 0))
    return _pc(
        body, name="merge_bwd", grid=(NB,),
        in_specs=[pl.BlockSpec((NC, B, D), lambda i: (0, i, 0)), row(D), row(D), row(D),
                  pl.BlockSpec((1, A, B), lambda i: (i, 0, 0)), row(CW), _const((1, D)), _const((1, D)),
                  _const((1, D)), _const((1, CW)), _const((1, CW)), _const((1, CW)), _const((A, 1)),
                  _const((D, D))],
        out_specs=[row(D), pl.BlockSpec((1, H * 128, B), lambda i: (i, 0, 0)),
                   pl.BlockSpec((H, 1, 1, B), lambda i: (0, i, 0, 0)), row(CW), _const((D, D)),
                   _const((8, D)), _const((8, CW)), _const((A, 1))],
        out_shape=[_sds((S, D)), _sds((NB, H * 128, B), BF16), _sds((H, NB, 1, B)), _sds((S, CW)),
                   _sds((D, D)), _sds((8, D)), _sds((8, CW)), _sds((A, 1))],
        compiler_params=_cp(("arbitrary",)),
    )(dh2p, x1, dy, o, oT, u1, n2g, sc2, g1, lng, lnb, beta_c, beta_a_col, wo)


def head_bwd(du1, u0, alg, zqk, fgT, dqT, dkT, dvT, dfk, conv_w, qg_col, kg_col, bf_col, tri_lo):
    S = u0.shape[0]
    B = BLK
    NB = S // B

    def body(dc_ref, dn_ref, uc_ref, up_ref, alg_ref, zqk_ref, fg_ref, dq_ref, dk_ref, dv_ref, dfk_ref,
             w_ref, qg_ref, kg_ref, bf_ref, tri_ref,
             dzr_ref, dzT_ref, dcw_ref, vc_ref, dqg_ref, dkg_ref, dbf_ref,
             buf, sh, du0_ref, carry, fbuf):
        pid = pl.program_id(0)
        ri = NB - 1 - pid

        @pl.when(pid == 0)
        def _():
            carry[...] = jnp.zeros_like(carry)
            dcw_ref[...] = jnp.zeros_like(dcw_ref)
            vc_ref[...] = jnp.zeros_like(vc_ref)
            dqg_ref[...] = jnp.zeros_like(dqg_ref)
            dkg_ref[...] = jnp.zeros_like(dkg_ref)
            dbf_ref[...] = jnp.zeros_like(dbf_ref)

        zero8 = jnp.zeros((8, CW), F32)
        buf[0:B, :] = dc_ref[...]
        buf[B:B + HALO, :] = jnp.where(ri < NB - 1, dn_ref[0:HALO, :], 0.0)
        buf[B + HALO:B + HALO + 8, :] = zero8
        _fill_shifted(buf, sh)
        _depthwise(sh, w_ref, [KC - 1 - k for k in range(KC)], jnp.zeros((1, CW), F32), du0_ref, B)
        buf[0:HALO, :] = jnp.where(ri > 0, up_ref[B - HALO:B, :], 0.0)
        buf[HALO:HALO + B, :] = uc_ref[...]
        buf[HALO + B:HALO + B + 8, :] = zero8
        _fill_shifted(buf, sh)
        _tap_gradients(sh, dc_ref, [HALO - (KC - 1) + k for k in range(KC)], dcw_ref, B)
        vc_ref[0:1, :] = vc_ref[0:1, :] + _colsum(dc_ref[...])

        du0 = du0_ref[...]
        al = alg_ref[:, 0:CW]
        sg = _sigmoid(alg_ref[:, CW:2 * CW])
        dzr_ref[:, 0:CW] = (du0 * sg).astype(BF16)
        dzr_ref[:, CW:2 * CW] = (du0 * al * sg * (1.0 - sg)).astype(BF16)

        dqg = jnp.zeros((DH, B), F32)
        dkg = jnp.zeros((DH, B), F32)
        qg = qg_ref[...]
        kg = kg_ref[...]
        for hh in range(H):
            sl = slice(hh * DH, (hh + 1) * DH)
            q = zqk_ref[sl, :]
            rq = lax.rsqrt(jnp.mean(q * q, axis=0, keepdims=True) + EPS)
            qn = q * rq
            dqh = dq_ref[0, hh * KR:hh * KR + DH, :] * 0.125
            dqg = dqg + dqh * qn
            dqn = dqh * qg
            dzT_ref[0, sl, :] = (rq * (dqn - qn * jnp.mean(dqn * qn, axis=0, keepdims=True))).astype(BF16)
            k = zqk_ref[A + hh * DH:A + (hh + 1) * DH, :]
            rk = lax.rsqrt(jnp.mean(k * k, axis=0, keepdims=True) + EPS)
            kn = k * rk
            dkh = dk_ref[0, sl, :] * (1.0 / LOG2E)
            dkg = dkg + dkh * kn
            dkn = dkh * kg
            dzT_ref[0, A + hh * DH:A + (hh + 1) * DH, :] = (
                rk * (dkn - kn * jnp.mean(dkn * kn, axis=0, keepdims=True))).astype(BF16)
            fbuf[hh:hh + 1, :] = dq_ref[0, hh * KR + DH:hh * KR + DH + 1, :] - dfk_ref[hh, 0, 0:1, :]
        dqg_ref[...] = dqg_ref[...] + _lanesum(dqg)
        dkg_ref[...] = dkg_ref[...] + _lanesum(dkg)
        dzT_ref[0, 2 * A:3 * A, :] = dv_ref[0].astype(BF16)

        dF = fbuf[...]
        a1, a2, a3 = _split3(dF)
        tr = tri_ref[...]
        dlogf = _nn(a1, tr) + _nn(a2, tr) + _nn(a3, tr) + carry[...]
        carry[...] = carry[...] + _lanesum(dF)
        dfg = dlogf * _sigmoid(-(fg_ref[...] + bf_ref[...]))
        dbf_ref[...] = dbf_ref[...] + _lanesum(dfg)
        dzT_ref[0, 3 * A:ZT_ROWS, :] = jnp.concatenate([dfg, jnp.zeros((8, B), F32)], axis=0).astype(BF16)

    rrow = lambda w: pl.BlockSpec((B, w), lambda p: (NB - 1 - p, 0))
    rts = lambda r: pl.BlockSpec((1, r, B), lambda p: (NB - 1 - p, 0, 0))
    return _pc(
        body, name="head_bwd", grid=(NB,),
        in_specs=[rrow(CW), pl.BlockSpec((B, CW), lambda p: (jnp.minimum(NB - p, NB - 1), 0)),
                  rrow(CW), pl.BlockSpec((B, CW), lambda p: (jnp.maximum(NB - 2 - p, 0), 0)),
                  rrow(2 * CW), pl.BlockSpec((2 * A, B), lambda p: (0, NB - 1 - p)),
                  pl.BlockSpec((8, B), lambda p: (0, NB - 1 - p)), rts(H * KR), rts(A), rts(A),
                  pl.BlockSpec((H, 1, 8, B), lambda p: (0, NB - 1 - p, 0, 0)),
                  _const((32, CW)), _const((DH, 1)), _const((DH, 1)), _const((8, 1)), _const((B, B))],
        out_specs=[rrow(2 * CW), rts(ZT_ROWS), _const((32, CW)), _const((8, CW)), _const((DH, 1)),
                   _const((DH, 1)), _const((8, 1))],
        out_shape=[_sds((S, 2 * CW), BF16), _sds((NB, ZT_ROWS, B), BF16), _sds((32, CW)), _sds((8, CW)),
                   _sds((DH, 1)), _sds((DH, 1)), _sds((8, 1))],
        scratch_shapes=[pltpu.VMEM((B + HALO + 8, CW), F32), pltpu.VMEM((8, B + HALO, CW), F32),
                        pltpu.VMEM((B, CW), F32), pltpu.VMEM((8, 1), F32), pltpu.VMEM((8, B), F32)],
        compiler_params=_cp(("arbitrary",)),
    )(du1, du1, u0, u0, alg, zqk, fgT, dqT, dkT, dvT, dfk, conv_w, qg_col, kg_col, bf_col, tri_lo)


def in_bwd(x, dx1, dzr, dzT, n1g, sc1, sh1, wT, w_qkvf, wr):
    S = x.shape[0]
    B = BLK
    NB = S // B

    def body(x_ref, dx1_ref, dzr_ref, dzT_ref, g_ref, sc_ref, sh_ref, wT_hbm, w_hbm, wr_hbm,
             gx_ref, dwT_hbm, dwr_hbm, v_ref, wT_v, w_v, wr_v, dwT_acc, dwr_acc, sem):
        i = pl.program_id(0)

        @pl.when(i == 0)
        def _():
            for src, dst in ((wT_hbm, wT_v), (w_hbm, w_v), (wr_hbm, wr_v)):
                cp = pltpu.make_async_copy(src, dst, sem)
                cp.start()
                cp.wait()
            dwT_acc[...] = jnp.zeros_like(dwT_acc)
            dwr_acc[...] = jnp.zeros_like(dwr_acc)
            v_ref[...] = jnp.zeros_like(v_ref)

        g = g_ref[...]
        sc = sc_ref[...]
        r1, xh, h1 = _norm_mod(x_ref[...], g, sc, sh_ref[...])
        hb = h1.astype(BF16)
        dzr = dzr_ref[...]
        dzT = dzT_ref[0]
        dwT_acc[...] = dwT_acc[...] + _nn(dzT, hb)
        dwr_acc[...] = dwr_acc[...] + _nn(h1.T.astype(BF16), dzr)
        dh1 = _nt(dzr, wr_v[...]) + _nn(w_v[...], dzT).T
        dhx = dh1 * xh
        v_ref[0:1, :] = v_ref[0:1, :] + _colsum(dh1)
        v_ref[1:2, :] = v_ref[1:2, :] + _colsum(dhx) * g
        v_ref[2:3, :] = v_ref[2:3, :] + _colsum(dhx) * (1.0 + sc)
        dxh = dh1 * (g * (1.0 + sc))
        gx_ref[...] = dx1_ref[...] + r1 * (dxh - xh * _rowmean(dxh * xh))

        @pl.when(i == NB - 1)
        def _():
            for src, dst in ((dwT_acc, dwT_hbm), (dwr_acc, dwr_hbm)):
                cp = pltpu.make_async_copy(src, dst, sem)
                cp.start()
                cp.wait()

    row = lambda w: pl.BlockSpec((B, w), lambda i: (i, 0))
    return _pc(
        body, name="in_bwd", grid=(NB,),
        in_specs=[row(D), row(D), row(2 * CW), pl.BlockSpec((1, ZT_ROWS, B), lambda i: (i, 0, 0)),
                  _const((1, D)), _const((1, D)), _const((1, D)), ANY, ANY, ANY],
        out_specs=[row(D), ANY, ANY, _const((8, D))],
        out_shape=[_sds((S, D)), _sds((ZT_ROWS, D)), _sds((D, 2 * CW)), _sds((8, D))],
        scratch_shapes=[pltpu.VMEM((ZT_ROWS, D), BF16), pltpu.VMEM((D, ZT_ROWS), BF16),
                        pltpu.VMEM((D, 2 * CW), BF16), pltpu.VMEM((ZT_ROWS, D), F32),
                        pltpu.VMEM((D, 2 * CW), F32), pltpu.SemaphoreType.DMA(())],
        compiler_params=_cp(("arbitrary",)),
    )(x, dx1, dzr, dzT, n1g, sc1, sh1, wT, w_qkvf, wr)


def _adam_math(g, w, m, v):
    m = ADAM_B1 * m + (1.0 - ADAM_B1) * g
    v = ADAM_B2 * v + (1.0 - ADAM_B2) * (g * g)
    m_hat = m / (1.0 - ADAM_B1 ** ADAM_STEP)
    v_hat = v / (1.0 - ADAM_B2 ** ADAM_STEP)
    delta = -ADAM_LR * (m_hat / (jnp.sqrt(v_hat) + ADAM_EPS) + ADAM_WD * w)
    return delta, m, v


def _row_tile(R):
    for t in (1024, 512, 256, 128, 64, 32, 16, 8):
        if R % t == 0:
            return t
    return R


def sum_slots(parts, name):
    K, R, C = parts.shape
    T = _row_tile(R)

    def body(p_ref, o_ref):
        s = p_ref[0]
        for k in range(1, K):
            s = s + p_ref[k]
        o_ref[...] = s

    return _pc(body, name=name, grid=(R // T,),
               in_specs=[pl.BlockSpec((K, T, C), lambda i: (0, i, 0))],
               out_specs=pl.BlockSpec((T, C), lambda i: (i, 0)), out_shape=_sds((R, C)),
               compiler_params=_cp(("arbitrary",)))(parts)


def adamw_slots(parts, w, m, v, name, row_off=0):
    K, _, C = parts.shape
    R = w.shape[0]
    T = _row_tile(R)
    assert row_off % T == 0
    blk_off = row_off // T

    def body(p_ref, w_ref, m_ref, v_ref, g_ref, d_ref, nm_ref, nv_ref):
        g = p_ref[0].astype(F32)
        for k in range(1, K):
            g = g + p_ref[k].astype(F32)
        g_ref[...] = g
        d_ref[...], nm_ref[...], nv_ref[...] = _adam_math(g, w_ref[...], m_ref[...], v_ref[...])

    t2 = pl.BlockSpec((T, C), lambda i: (i, 0))
    return _pc(body, name=name, grid=(R // T,),
               in_specs=[pl.BlockSpec((K, T, C), lambda i: (0, i + blk_off, 0)), t2, t2, t2],
               out_specs=[t2, t2, t2, t2], out_shape=[_sds((R, C))] * 4,
               compiler_params=_cp(("arbitrary",)))(parts, w, m, v)


def ada_grad_adamw(cT, dmod_cols, w, m, v):
    NCOL = w.shape[1]

    def body(cT_ref, dm_ref, w_ref, m_ref, v_ref, g_ref, d_ref, nm_ref, nv_ref):
        def term(b):
            cv = cT_ref[b]
            return (cv * _sigmoid(cv)) * dm_ref[b:b + 1, :]

        g = term(0)
        for b in range(1, N_DEV):
            g = g + term(b)
        g_ref[...] = g
        d_ref[...], nm_ref[...], nv_ref[...] = _adam_math(g, w_ref[...], m_ref[...], v_ref[...])

    full = _const((D, NCOL))
    return _pc(body, name="ada_grad_adamw", grid=(1,),
               in_specs=[_const((N_DEV, D, 1)), _const((N_DEV, NCOL)), full, full, full],
               out_specs=[full, full, full, full], out_shape=[_sds((D, NCOL))] * 4,
               compiler_params=_cp())(cT, dmod_cols, w, m, v)


def _pack_rows(vecs, rows=None):
    flat = jnp.concatenate([jnp.ravel(v) for v in vecs])
    n = flat.shape[0]
    r = -(-n // 1024) * 8 if rows is None else rows
    return jnp.pad(flat, (0, r * 128 - n)).reshape(r, 128)


def _unpack_rows(packed, shapes):
    flat = packed.reshape(-1)
    out, off = [], 0
    for s in shapes:
        n = 1
        for d in s:
            n *= d
        out.append(flat[off:off + n].reshape(s))
        off += n
    return out


def _cols_from_shards(g, rows, cols):
    return g.reshape(N_DEV, rows, cols).transpose(1, 0, 2).reshape(rows, N_DEV * cols)


def _cols_to_shards(w, cols):
    rows = w.shape[0]
    return w.reshape(rows, N_DEV, cols).transpose(1, 0, 2).reshape(N_DEV, rows * cols)


def kernel(x, c, w_ada, b_ada, norm1_g, w_in, q_norm_g, k_norm_g, b_f, conv_w, conv_b, conv_ln_g, conv_ln_b, beta_attn, beta_conv, w_out, norm2_g, w_ff1, w_ff2, loss_target, m_w_ada, m_b_ada, m_norm1_g, m_w_in, m_q_norm_g, m_k_norm_g, m_b_f, m_conv_w, m_conv_b, m_conv_ln_g, m_conv_ln_b, m_beta_attn, m_beta_conv, m_w_out, m_norm2_g, m_w_ff1, m_w_ff2, v_w_ada, v_b_ada, v_norm1_g, v_w_in, v_q_norm_g, v_k_norm_g, v_b_f, v_conv_w, v_conv_b, v_conv_ln_g, v_conv_ln_b, v_beta_attn, v_beta_conv, v_w_out, v_norm2_g, v_w_ff1, v_w_ff2):
    S = x.shape[1]
    B = BLK
    NB = S // B
    me = 4 * lax.axis_index("x") + 2 * lax.axis_index("y") + lax.axis_index("c")
    xs = x[0]
    tgt = loss_target[0]

    ADA_C, IN_C, FF_C, CV_C = w_ada.shape[2], w_in.shape[2], w_ff1.shape[2], conv_w.shape[2]
    OUT_R, FF_R = w_out.shape[1], w_ff2.shape[1]
    IN_W = N_DEV * IN_C

    n_ada, n_in, n_out, n_f1, n_f2 = D * ADA_C, D * IN_C, OUT_R * D, D * FF_C, FF_R * D
    early = jnp.concatenate([w_ada.reshape(-1), w_in.reshape(-1)]).astype(BF16)
    rows_e = -(-early.shape[0] // (128 * 16)) * 16
    early = jnp.pad(early, (0, rows_e * 128 - early.shape[0])).reshape(rows_e, 128)
    small0 = lax.bitcast_convert_type(_pack_rows([c, conv_w], rows=24), BF16).reshape(48, 128)
    ge = all_gather(jnp.concatenate([early, small0], axis=0), "ag_weights_early")
    gw = ge[:, 0:rows_e].reshape(N_DEV, rows_e * 128)
    gs0 = lax.bitcast_convert_type(ge[:, rows_e:].reshape(N_DEV, 24, 128, 2), F32).reshape(N_DEV, 24 * 128)
    late_w = jnp.concatenate([w_out.reshape(-1), w_ff1.reshape(-1), w_ff2.reshape(-1)]).astype(BF16)
    late_w = late_w.reshape(-1, 128)

    w_ada_full = _cols_from_shards(gw[:, 0:n_ada], D, ADA_C)
    w_in_full = _cols_from_shards(gw[:, n_ada:n_ada + n_in], D, IN_C)
    c_all = gs0[:, 0:D]
    conv_w_full = _cols_from_shards(gs0[:, D:D + KC * CV_C], KC, CV_C)
    conv_w_pad = jnp.pad(conv_w_full, ((0, 32 - KC), (0, 0)))

    w_qkvf = jnp.pad(w_in_full[:, 0:3 * A + H], ((0, 0), (0, ZT_ROWS - 3 * A - H)))
    wT = w_qkvf.T
    wr = w_in_full[:, 3 * A + H:]

    qg_col = q_norm_g.reshape(DH, 1)
    kg_col = k_norm_g.reshape(DH, 1)
    bf_col = b_f.reshape(H, 1)
    beta_a_col = beta_attn.reshape(A, 1)
    ii = lax.broadcasted_iota(jnp.int32, (B, B), 0)
    jj = lax.broadcasted_iota(jnp.int32, (B, B), 1)
    tri_up = (ii <= jj).astype(BF16)
    tri_lo = (ii >= jj).astype(BF16)

    mod = mod_kernel(c, w_ada_full, b_ada)
    sh1, sc1, g1, sh2, sc2, g2 = [mod[:, k * D:(k + 1) * D] for k in range(6)]
    qTa, kT, kaug, vT, vaug, zqk, fgT, alg, u0, stat = fwd_in(xs, norm1_g, sc1, sh1, wT, wr, qg_col,
                                                             kg_col, bf_col, tri_up)
    fmax = stat[:, :, 0].T
    fmin = stat[:, :, 1].T
    qk_max = jnp.sqrt(jnp.max(stat[:, :, 2], axis=0) * jnp.max(stat[:, :, 3], axis=0))
    thr = -(PRUNE + (2.02 / LOG2E) * qk_max)
    oT, lse, gl = attn_fwd(fmax, fmin, thr, qTa, kaug, vT, late_w)
    gl = gl.reshape(N_DEV, -1)
    wo_full = gl[:, 0:n_out].reshape(D, D)
    w1_full = _cols_from_shards(gl[:, n_out:n_out + n_f1], D, FF_C)
    w2_full = gl[:, n_out + n_f1:].reshape(FF, D)
    x1, o, u1 = conv_merge_out(u0, xs, oT, conv_w_pad, conv_b, conv_ln_g, conv_ln_b, beta_conv,
                               beta_a_col, wo_full, g1)
    dy, loss_part, dg2, ra = mlp_fwd_loss(x1, tgt, norm2_g, sc2, sh2, g2, w1_full, w2_full)

    dw1, dw2t, dh2p = mlp_bwd(x1, dy, ra, norm2_g, sc2, sh2, g2, w1_full, w2_full)
    dx1, doTa, delta, du1, dwo, v1, v2, dba = merge_bwd(dh2p, x1, dy, o, oT, u1, norm2_g, sc2, g1,
                                                         conv_ln_g, conv_ln_b, beta_conv, beta_a_col, wo_full)
    early_g = jnp.concatenate([dwo.reshape(N_DEV, n_out), _cols_to_shards(dw1, FF_C),
                               dw2t.T.reshape(N_DEV, n_f2)], axis=1)
    rows_a = early_g.shape[1] // 128
    dqT, dkT, dvT, dfk, recv_a = attn_bwd(fmax, fmin, thr, qTa, kaug, kT, vaug, doTa, lse, delta,
                                          early_g.reshape(N_DEV, rows_a, 128))
    dzr, dzT, dcw, vc, dqg, dkg, dbf = head_bwd(du1, u0, alg, zqk, fgT, dqT, dkT, dvT, dfk, conv_w_pad,
                                                qg_col, kg_col, bf_col, tri_lo)
    grad_x, dwT, dwr, v0 = in_bwd(xs, dx1, dzr, dzT, norm1_g, sc1, sh1, wT, w_qkvf, wr)

    dw_in = jnp.concatenate([dwT.T[:, 0:3 * A + H], dwr], axis=1)
    late_g = _cols_to_shards(dw_in, IN_C)
    rows_b = -(-late_g.shape[1] // (128 * 256)) * 256
    late_g = jnp.pad(late_g, ((0, 0), (0, rows_b * 128 - late_g.shape[1]))).astype(BF16)
    recv_b = reduce_scatter_exchange(late_g.reshape(N_DEV, rows_b, 128), "rs_grads_in")

    dmod = jnp.concatenate([v0[0], v0[1], v1[3], v1[0], v1[1], dg2[0]])
    small1 = _pack_rows([dmod, v0[2], v1[2], dcw, vc[0], v2[1], v2[2], v2[0], dba, dqg, dkg,
                         jnp.pad(dbf.reshape(-1), (0, 120)), jnp.pad(loss_part.reshape(-1), (0, 127))])
    gs1 = all_gather(small1, "ag_small_bwd")
    tot = sum_slots(gs1, "sum_small")
    (g_b_ada, g_n1, g_n2, g_cw, g_cb, g_lng, g_lnb, g_bc, g_ba, g_qg, g_kg, g_bf, loss_v) = _unpack_rows(
        tot, [(1, 6 * D), (1, D), (1, D), (32, CW), (1, CW), (1, CW), (1, CW), (1, CW), (1, A),
              (1, DH), (1, DH), (1, 128), (1, 128)])
    loss = loss_v[0, 0]
    g_bf = g_bf[:, 0:H]
    g_cw_mine = lax.dynamic_slice(g_cw[0:KC], (0, me * CV_C), (KC, CV_C)).reshape(1, KC, CV_C)

    small_names = [(b_ada, m_b_ada, v_b_ada, g_b_ada), (norm1_g, m_norm1_g, v_norm1_g, g_n1),
                   (q_norm_g, m_q_norm_g, v_q_norm_g, g_qg), (k_norm_g, m_k_norm_g, v_k_norm_g, g_kg),
                   (b_f, m_b_f, v_b_f, g_bf), (conv_w, m_conv_w, v_conv_w, g_cw_mine),
                   (conv_b, m_conv_b, v_conv_b, g_cb), (conv_ln_g, m_conv_ln_g, v_conv_ln_g, g_lng),
                   (conv_ln_b, m_conv_ln_b, v_conv_ln_b, g_lnb), (beta_attn, m_beta_attn, v_beta_attn, g_ba),
                   (beta_conv, m_beta_conv, v_beta_conv, g_bc), (norm2_g, m_norm2_g, v_norm2_g, g_n2)]
    shapes_s = [t[0].shape for t in small_names]
    pw, pm, pv, pg = [_pack_rows([t[k] for t in small_names]) for k in range(4)]
    sg_, sd_, sm_, sv_ = adamw_slots(pg[None], pw, pm, pv, "adamw_small")
    sgs, sds, sms, svs = [_unpack_rows(a, shapes_s) for a in (sg_, sd_, sm_, sv_)]

    dmod_all = gs1[:, 0:48, :].reshape(N_DEV, N_DEV, ADA_C)
    dmod_cols = lax.dynamic_index_in_dim(dmod_all, me, axis=1, keepdims=False)
    ga, da, ma_, va_ = ada_grad_adamw(c_all.reshape(N_DEV, D, 1), dmod_cols, w_ada[0], m_w_ada[0], v_w_ada[0])

    res_b = adamw_slots(recv_b, *[_pack_rows(ws, rows=rows_b) for ws in ([w_in], [m_w_in], [v_w_in])],
                        "adamw_in")
    late = []
    for nm, off, (w_, m_, v_) in (("adamw_out", 0, (w_out, m_w_out, v_w_out)),
                                  ("adamw_ff1", n_out // 128, (w_ff1, m_w_ff1, v_w_ff1)),
                                  ("adamw_ff2", (n_out + n_f1) // 128, (w_ff2, m_w_ff2, v_w_ff2))):
        res = adamw_slots(recv_a, w_.reshape(-1, 128), m_.reshape(-1, 128), v_.reshape(-1, 128), nm, off)
        late.append([r.reshape(w_.shape) for r in res])
    bgs, bds, bms, bvs = [_unpack_rows(res_b[q], [(1, D, IN_C)]) + [late[0][q], late[1][q], late[2][q]]
                          for q in range(4)]

    def assemble(small, ada, bigs):
        (b_ada_, n1_, qg_, kg_, bf_, cw_, cb_, lng_, lnb_, ba_, bc_, n2_) = small
        return [ada.reshape(1, D, ADA_C), b_ada_, n1_, bigs[0], qg_, kg_, bf_, cw_, cb_, lng_, lnb_, ba_, bc_,
                bigs[1], n2_, bigs[2], bigs[3]]

    return (loss, grad_x.reshape(1, S, D), *assemble(sgs, ga, bgs), *assemble(sds, da, bds),
            *assemble(sms, ma_, bms), *assemble(svs, va_, bvs))
```

```python
import functools

import jax
import jax.numpy as jnp
from jax import lax
from jax.experimental import pallas as pl
from jax.experimental.pallas import tpu as pltpu

F32 = jnp.float32
BF16 = jnp.bfloat16

D = 1024
A = 512
CW = 512
H = 8
DH = 64
FF = 4096
KC = 31
HALO = 32
KR = 80
ZT_ROWS = 1552
FCH = 1024
EPS = 1e-6
BLK = 512
N_DEV = 8
VMEM_LIMIT = 56 * 1024 * 1024

ADAM_LR = 0.001
ADAM_B1 = 0.9
ADAM_B2 = 0.999
ADAM_EPS = 1e-08
ADAM_WD = 0.01
ADAM_STEP = 10

MESH = pl.DeviceIdType.MESH
ANY = pl.BlockSpec(memory_space=pl.ANY)
SMEM = pl.BlockSpec(memory_space=pltpu.SMEM)
PRUNE = 110.0
LOG2E = 1.4426950408889634


def _pc(body, **kw):
    return pl.pallas_call(body, **kw)


def _cp(sem=None):
    return pltpu.CompilerParams(dimension_semantics=sem, vmem_limit_bytes=VMEM_LIMIT)


def _sds(shape, dtype=F32):
    return jax.ShapeDtypeStruct(shape, dtype)


def _const(shape):
    n = len(shape)
    return pl.BlockSpec(shape, lambda *a: (0,) * n)


def _nt(a, b):
    return lax.dot_general(a, b, (((1,), (1,)), ((), ())), preferred_element_type=F32)


def _nn(a, b):
    return jnp.dot(a, b, preferred_element_type=F32)


def _sigmoid(v):
    return 1.0 / (1.0 + jnp.exp(-v))


def _split3(v):
    a1 = v.astype(BF16)
    r1 = v - a1.astype(F32)
    a2 = r1.astype(BF16)
    a3 = (r1 - a2.astype(F32)).astype(BF16)
    return a1, a2, a3


def _rowmean(v):
    return jnp.mean(v, axis=-1, keepdims=True)


def _colsum(v):
    return jnp.sum(v, axis=0, keepdims=True)


def _lanesum(v):
    return jnp.sum(v, axis=-1, keepdims=True)


def _coords():
    return lax.axis_index("x"), lax.axis_index("y"), lax.axis_index("c")


def all_gather(x, name):
    R, C = x.shape

    def body(x_ref, out_ref, send_sems, recv_sems, local_sem):
        mx, my, mc = _coords()
        me, sibling = (mx, my, mc), (mx, my, 1 - mc)
        chips = [(1 - mx, my), (mx, 1 - my), (1 - mx, 1 - my)]

        def rows(px, py, pc):
            return out_ref.at[4 * px + 2 * py + pc]

        def copy(k, block, to, src=None):
            return pltpu.make_async_remote_copy(
                src_ref=rows(*block) if src is None else src, dst_ref=rows(*block),
                send_sem=send_sems.at[k], recv_sem=recv_sems.at[k],
                device_id=to, device_id_type=MESH)

        mine = pltpu.make_async_copy(x_ref, rows(*me), local_sem)
        mine.start()
        first = [copy(0, me, sibling, src=x_ref)]
        first += [copy(1 + j, me, (*chip, mc), src=x_ref) for j, chip in enumerate(chips)]
        for cp in first:
            cp.start()
        passed = [copy(4 + j, (*chip, mc), sibling) for j, chip in enumerate(chips)]
        for j, chip in enumerate(chips):
            copy(1 + j, (*chip, mc), me).wait_recv()
            passed[j].start()
        copy(0, sibling, me).wait_recv()
        for j, chip in enumerate(chips):
            copy(4 + j, (*chip, 1 - mc), me).wait_recv()
        for cp in first + passed:
            cp.wait_send()
        mine.wait()

    return _pc(
        body, name=name, out_shape=_sds((N_DEV, R, C), x.dtype),
        in_specs=[ANY], out_specs=ANY,
        scratch_shapes=[pltpu.SemaphoreType.DMA((7,)), pltpu.SemaphoreType.DMA((7,)),
                        pltpu.SemaphoreType.DMA(())],
    )(x)


def _direct_copies(scatter, src_ref, dst_ref, send_sems, recv_sems, local_sem):
    mx, my, mc = _coords()
    me = 4 * mx + 2 * my + mc
    if scatter:
        local = pltpu.make_async_copy(src_ref.at[me], dst_ref.at[0], local_sem)
    else:
        local = pltpu.make_async_copy(src_ref, dst_ref.at[me], local_sem)
    remote = []
    for r in range(1, N_DEV):
        px = 1 - mx if r & 4 else mx
        py = 1 - my if r & 2 else my
        pcc = 1 - mc if r & 1 else mc
        remote.append(pltpu.make_async_remote_copy(
            src_ref=src_ref.at[4 * px + 2 * py + pcc] if scatter else src_ref,
            dst_ref=dst_ref.at[r] if scatter else dst_ref.at[me],
            send_sem=send_sems.at[r - 1], recv_sem=recv_sems.at[r - 1],
            device_id=(px, py, pcc), device_id_type=MESH))
    return [local] + remote


def _start_all(copies):
    for cp in copies:
        cp.start()


def _wait_all(copies):
    for cp in copies[1:]:
        cp.wait()
    copies[0].wait()


COMM_SEMS = [pltpu.SemaphoreType.DMA((7,)), pltpu.SemaphoreType.DMA((7,)), pltpu.SemaphoreType.DMA(())]


def mod_columns(c_all, w_shard, b_cols):
    NCOL = w_shard.shape[1]

    def body(c_ref, w_ref, b_ref, o_ref):
        cv = c_ref[...]
        o_ref[...] = _nn((cv * _sigmoid(cv)).astype(BF16), w_ref[...].astype(BF16)) + b_ref[...]

    return _pc(body, name="mod_columns", out_shape=_sds((N_DEV, NCOL)),
               in_specs=[_const((N_DEV, D)), _const((D, NCOL)), _const((1, NCOL))],
               out_specs=_const((N_DEV, NCOL)), grid=(1,), compiler_params=_cp())(c_all, w_shard, b_cols)


def _norm_mod(xv, g, sc, sh):
    r = lax.rsqrt(_rowmean(xv * xv) + EPS)
    xh = xv * r
    return r, xh, xh * (g * (1.0 + sc)) + sh


def _log_sigmoid(v):
    e = jnp.exp(-jnp.abs(v))
    l1p = jnp.where(e < 1e-4, e * (1.0 - 0.5 * e), jnp.log(1.0 + e))
    return jnp.minimum(v, 0.0) - l1p


def fwd_in(x, n1g, sc1, sh1, wT, wr, qg_col, kg_col, bf_col, tri):
    S = x.shape[0]
    B = BLK
    NB = S // B

    def body(x_ref, g_ref, sc_ref, sh_ref, wT_ref, wr_ref, qg_ref, kg_ref, bf_ref, tri_ref,
             qTa_ref, kT_ref, kaug_ref, vT_ref, vaug_ref, zqk_ref, fgT_ref, alg_ref, u0_ref, stat_ref,
             carry, tbuf, fs, nq, nk):
        i = pl.program_id(0)

        @pl.when(i == 0)
        def _():
            carry[...] = jnp.zeros_like(carry)

        _, _, h = _norm_mod(x_ref[...], g_ref[...], sc_ref[...], sh_ref[...])
        hb = h.astype(BF16)
        zT = _nt(wT_ref[...], hb)
        zr = _nn(hb, wr_ref[...])
        zqk_ref[...] = zT[0:2 * A]
        fgT = zT[3 * A:3 * A + 8]
        fgT_ref[...] = fgT
        alg_ref[...] = zr
        u0_ref[...] = zr[:, 0:CW] * _sigmoid(zr[:, CW:2 * CW])

        logf = _log_sigmoid(fgT + bf_ref[...])
        a1, a2, a3 = _split3(logf)
        tr = tri_ref[...]
        F = _nn(a1, tr) + _nn(a2, tr) + _nn(a3, tr) + carry[...]
        carry[...] = carry[...] + _lanesum(logf)
        p1, p2, p3 = _split3(F * LOG2E)
        n1, n2, n3 = _split3(F * (-LOG2E))
        for k, v in enumerate((p1, p2, p3, n1, n2, n3)):
            fs[k] = v.astype(F32)

        rowi = lax.broadcasted_iota(jnp.int32, (8, B), 0)
        zeros_tail = jnp.zeros((128 - DH - 8, B), F32)
        ones_row = jnp.where(lax.broadcasted_iota(jnp.int32, (KR - DH, B), 0) == 0, 1.0, 0.0).astype(BF16)
        for hh in range(H):
            sl = slice(hh * DH, (hh + 1) * DH)
            q = zT[sl]
            k = zT[A + hh * DH:A + (hh + 1) * DH]
            v = zT[2 * A + hh * DH:2 * A + (hh + 1) * DH]
            qh = q * lax.rsqrt(jnp.mean(q * q, axis=0, keepdims=True) + EPS) * qg_ref[...] * (0.125 * LOG2E)
            kh = k * lax.rsqrt(jnp.mean(k * k, axis=0, keepdims=True) + EPS) * kg_ref[...]
            frow = [fs[kk, hh:hh + 1, :] for kk in range(6)]
            qx = jnp.where(rowi < 3, 1.0, jnp.where(rowi == 3, frow[0], jnp.where(
                rowi == 4, frow[1], jnp.where(rowi == 5, frow[2], 0.0))))
            kx = jnp.where(rowi == 0, frow[3], jnp.where(rowi == 1, frow[4], jnp.where(
                rowi == 2, frow[5], jnp.where(rowi < 6, 1.0, 0.0))))
            tbuf[0:DH, :] = qh
            tbuf[DH:DH + 8, :] = qx
            tbuf[DH + 8:128, :] = zeros_tail
            qTa_ref[0, hh * 128:(hh + 1) * 128, :] = tbuf[...].astype(BF16)
            tbuf[0:DH, :] = kh
            tbuf[DH:DH + 8, :] = kx
            kaug_ref[hh] = tbuf[...].T.astype(BF16)
            kT_ref[0, hh * KR:hh * KR + DH, :] = kh.astype(BF16)
            kT_ref[0, hh * KR + DH:(hh + 1) * KR, :] = ones_row
            tbuf[0:DH, :] = v
            tbuf[DH:DH + 8, :] = jnp.zeros((8, B), F32)
            vaug_ref[hh] = tbuf[...].T.astype(BF16)
            vT_ref[0, sl, :] = v.astype(BF16)
            nq[hh:hh + 1, :] = jnp.max(_colsum(qh * qh), axis=1, keepdims=True)
            nk[hh:hh + 1, :] = jnp.max(_colsum(kh * kh), axis=1, keepdims=True)

        lane = lax.broadcasted_iota(jnp.int32, (8, 128), 1)
        stat_ref[0] = jnp.where(lane == 0, jnp.max(F, axis=1, keepdims=True), jnp.where(
            lane == 1, jnp.min(F, axis=1, keepdims=True), jnp.where(
                lane == 2, nq[...], jnp.where(lane == 3, nk[...], 0.0))))

    row = lambda w: pl.BlockSpec((B, w), lambda i: (i, 0))
    tsp = lambda r: pl.BlockSpec((1, r, B), lambda i: (i, 0, 0))
    return _pc(
        body, name="fwd_in", grid=(NB,),
        in_specs=[row(D), _const((1, D)), _const((1, D)), _const((1, D)), _const((ZT_ROWS, D)),
                  _const((D, 2 * CW)), _const((DH, 1)), _const((DH, 1)), _const((8, 1)), _const((B, B))],
        out_specs=[tsp(H * 128), tsp(H * KR), pl.BlockSpec((H, B, 128), lambda i: (0, i, 0)), tsp(A),
                   pl.BlockSpec((H, B, 128), lambda i: (0, i, 0)),
                   pl.BlockSpec((2 * A, B), lambda i: (0, i)), pl.BlockSpec((8, B), lambda i: (0, i)),
                   row(2 * CW), row(CW), pl.BlockSpec((1, 8, 128), lambda i: (i, 0, 0))],
        out_shape=[_sds((NB, H * 128, B), BF16), _sds((NB, H * KR, B), BF16), _sds((H, S, 128), BF16),
                   _sds((NB, A, B), BF16), _sds((H, S, 128), BF16), _sds((2 * A, S)), _sds((8, S)),
                   _sds((S, 2 * CW)), _sds((S, CW)), _sds((NB, 8, 128))],
        scratch_shapes=[pltpu.VMEM((8, 1), F32), pltpu.VMEM((128, B), F32), pltpu.VMEM((6, 8, B), F32),
                        pltpu.VMEM((8, 1), F32), pltpu.VMEM((8, 1), F32)],
        compiler_params=_cp(("arbitrary",)),
    )(x, n1g, sc1, sh1, wT, wr, qg_col, kg_col, bf_col, tri)


def _first_key_block(top, fmin_ref, cut, h, i):
    return lax.while_loop(
        lambda j: jnp.logical_and(j > 0, top - fmin_ref[h, jnp.maximum(j - 1, 0)] >= cut),
        lambda j: j - 1, i)


def _causal_keep(B):
    return lax.broadcasted_iota(jnp.int32, (B, B), 0) <= lax.broadcasted_iota(jnp.int32, (B, B), 1)


def attn_fwd(fmax, fmin, thr, qTa, kaug, vT, late_w):
    NB, _, B = qTa.shape
    S = NB * B

    def body(fmax_ref, fmin_ref, thr_ref, q_ref, k_ref, v_ref, w_ref, o_ref, lse_ref, gw_ref,
             s0, s1, m_ref, l_ref, acc_ref, send_sems, recv_sems, local_sem):
        h = pl.program_id(0)
        i = pl.program_id(1)

        @pl.when(jnp.logical_and(h == 0, i == 0))
        def _():
            _start_all(_direct_copies(False, w_ref, gw_ref, send_sems, recv_sems, local_sem))

        jlo = _first_key_block(fmax_ref[h, i], fmin_ref, thr_ref[h], h, i)
        n = i - jlo

        def scores(j, s_ref):
            s_ref[...] = _nn(k_ref[0, pl.ds(pl.multiple_of(j * B, B), B), :], q_ref[0])

        def softmax_step(s_ref, j, masked):
            s = s_ref[...]
            if masked:
                s = jnp.where(_causal_keep(B), s, -jnp.inf)
            m = m_ref[...]
            mn = jnp.maximum(m, jnp.max(s, axis=0, keepdims=True))
            a = jnp.exp2(m - mn)
            p = jnp.exp2(s - mn)
            m_ref[...] = mn
            l_ref[...] = a * l_ref[...] + _colsum(p)
            acc_ref[...] = a * acc_ref[...] + _nn(v_ref[j], p.astype(BF16))

        m_ref[...] = jnp.full((1, B), -jnp.inf, F32)
        l_ref[...] = jnp.zeros((1, B), F32)
        acc_ref[...] = jnp.zeros((DH, B), F32)
        scores(jlo, s0)

        def pair(t, carry):
            j = jlo + 2 * t
            scores(j + 1, s1)
            softmax_step(s0, j, False)
            scores(j + 2, s0)
            softmax_step(s1, j + 1, False)
            return carry

        lax.fori_loop(0, n // 2, pair, 0)

        @pl.when(n % 2 == 1)
        def _():
            scores(i, s1)
            softmax_step(s0, i - 1, False)
            softmax_step(s1, i, True)

        @pl.when(n % 2 == 0)
        def _():
            softmax_step(s0, i, True)

        l = l_ref[...]
        o_ref[0] = acc_ref[...] / l
        lse_ref[0, 0] = m_ref[...] + jnp.log2(l)

        @pl.when(jnp.logical_and(h == H - 1, i == NB - 1))
        def _():
            _wait_all(_direct_copies(False, w_ref, gw_ref, send_sems, recv_sems, local_sem))

    return _pc(
        body, name="attn_fwd", grid=(H, NB),
        in_specs=[SMEM, SMEM, SMEM, pl.BlockSpec((1, 128, B), lambda h, i: (i, h, 0)),
                  pl.BlockSpec((1, S, 128), lambda h, i: (h, 0, 0)),
                  pl.BlockSpec((NB, DH, B), lambda h, i: (0, h, 0)), ANY],
        out_specs=[pl.BlockSpec((1, DH, B), lambda h, i: (i, h, 0)),
                   pl.BlockSpec((1, 1, 1, B), lambda h, i: (h, i, 0, 0)), ANY],
        out_shape=[_sds((NB, A, B)), _sds((H, NB, 1, B)), _sds((N_DEV,) + late_w.shape, late_w.dtype)],
        scratch_shapes=[pltpu.VMEM((B, B), F32), pltpu.VMEM((B, B), F32), pltpu.VMEM((1, B), F32),
                        pltpu.VMEM((1, B), F32), pltpu.VMEM((DH, B), F32)] + COMM_SEMS,
        compiler_params=_cp(("arbitrary", "arbitrary")),
    )(fmax, fmin, thr, qTa, kaug, vT, late_w)


def attn_bwd(fmax, fmin, thr, qTa, kaug, kT, vaug, doTa, lse, delta, early_g):
    NG = len(early_g)
    NB, _, B = qTa.shape
    QR = 80
    CH = min(256, B)

    def body(fmax_ref, fmin_ref, thr_ref, q_ref, ka_ref, kt_ref, va_ref, do_ref, lse_ref, dl_ref, *rest):
        g_refs = rest[0:NG]
        dq_ref, dk_ref, dv_ref, dfk_ref = rest[NG:NG + 4]
        recv_refs = rest[NG + 4:2 * NG + 4]
        s0, d0, s1, d1, p0, e0, p1, e1, dk_acc, dv_acc = rest[2 * NG + 4:2 * NG + 14]
        sems = rest[2 * NG + 14:]
        h = pl.program_id(0)
        j = pl.program_id(1)

        def exchanges():
            return [_direct_copies(True, g_refs[q], recv_refs[q], *sems[3 * q:3 * q + 3]) for q in range(NG)]

        @pl.when(jnp.logical_and(h == 0, j == 0))
        def _():
            for copies in exchanges():
                _start_all(copies)

        bottom = fmin_ref[h, j]
        cut = thr_ref[h]
        ihi = lax.while_loop(
            lambda i: jnp.logical_and(i < NB - 1, fmax_ref[h, jnp.minimum(i + 1, NB - 1)] - bottom >= cut),
            lambda i: i + 1, j)

        @pl.when(j == 0)
        def _():
            dq_ref[...] = jnp.zeros_like(dq_ref)

        dk_acc[...] = jnp.zeros_like(dk_acc)
        dv_acc[...] = jnp.zeros_like(dv_acc)

        n = ihi - j + 1

        def blk(k):
            return jnp.minimum(j + k, ihi)

        def products(k, s_ref, d_ref):
            i = blk(k)
            s_ref[...] = _nn(ka_ref[0], q_ref[i])
            d_ref[...] = _nn(va_ref[0], do_ref[i])

        def elementwise(k, s_ref, d_ref, p_ref, ds_ref, masked):
            i = blk(k)
            p = jnp.exp2(s_ref[...] - lse_ref[0, i])
            if masked:
                p = jnp.where(_causal_keep(B), p, 0.0)
            p_ref[...] = p.astype(BF16)
            ds_ref[...] = (p * (d_ref[...] - dl_ref[0, i])).astype(BF16)

        def grads(k, p_ref, ds_ref):
            i = j + k
            dsb = ds_ref[...]
            dv_acc[...] = dv_acc[...] + _nt(do_ref[i][0:DH], p_ref[...])
            dk_acc[...] = dk_acc[...] + _nt(q_ref[i][0:QR], dsb)
            dq_ref[i] = dq_ref[i] + _nn(kt_ref[0], dsb)

        def trip(kp, s_p, d_p, ke, s_e, d_e, p_e, e_e, kg, p_g, e_g):
            ip, ie, ig = blk(kp), blk(ke), j + kg
            lse_e = lse_ref[0, ie]
            dl_e = dl_ref[0, ie]
            dq_part = jnp.zeros((KR, B), F32)
            for c in range(B // CH):
                rows = slice(c * CH, (c + 1) * CH)
                s_p[rows, :] = _nn(ka_ref[0, rows, :], q_ref[ip])
                d_p[rows, :] = _nn(va_ref[0, rows, :], do_ref[ip])
                p = jnp.exp2(s_e[rows, :] - lse_e)
                p_e[rows, :] = p.astype(BF16)
                e_e[rows, :] = (p * (d_e[rows, :] - dl_e)).astype(BF16)
                dsb = e_g[rows, :]
                dv_acc[:, rows] = dv_acc[:, rows] + _nt(do_ref[ig][0:DH], p_g[rows, :])
                dk_acc[:, rows] = dk_acc[:, rows] + _nt(q_ref[ig][0:QR], dsb)
                dq_part = dq_part + _nn(kt_ref[0, :, rows], dsb)
            dq_ref[ig] = dq_ref[ig] + dq_part

        products(0, s0, d0)
        products(1, s1, d1)
        elementwise(0, s0, d0, p0, e0, True)

        def pair(t, carry):
            k = 2 * t
            trip(k + 2, s0, d0, k + 1, s1, d1, p1, e1, k, p0, e0)
            trip(k + 3, s1, d1, k + 2, s0, d0, p0, e0, k + 1, p1, e1)
            return carry

        n_pairs = n // 2
        lax.fori_loop(0, n_pairs - 1, pair, 0)
        k_last = 2 * (n_pairs - 1)

        @pl.when(jnp.logical_and(n_pairs >= 1, n % 2 == 0))
        def _():
            elementwise(k_last + 1, s1, d1, p1, e1, False)
            grads(k_last, p0, e0)
            grads(k_last + 1, p1, e1)

        @pl.when(jnp.logical_and(n_pairs >= 1, n % 2 == 1))
        def _():
            trip(k_last + 2, s0, d0, k_last + 1, s1, d1, p1, e1, k_last, p0, e0)
            elementwise(k_last + 2, s0, d0, p0, e0, False)
            grads(k_last + 1, p1, e1)
            grads(k_last + 2, p0, e0)

        @pl.when(n == 1)
        def _():
            grads(0, p0, e0)

        dk_ref[0] = dk_acc[0:DH, :]
        dfk_ref[0, 0] = dk_acc[DH:DH + 8, :]
        dv_ref[0] = dv_acc[...]

        @pl.when(jnp.logical_and(h == H - 1, j == NB - 1))
        def _():
            for copies in exchanges():
                _wait_all(copies)

    per_kv = lambda r: pl.BlockSpec((1, r, B), lambda h, j: (j, h, 0))
    head_all = lambda r: pl.BlockSpec((NB, r, B), lambda h, j: (0, h, 0))
    aug = pl.BlockSpec((1, B, 128), lambda h, j: (h, j, 0))
    stat = pl.BlockSpec((1, NB, 1, B), lambda h, j: (h, 0, 0, 0))
    return _pc(
        body, name="attn_bwd", grid=(H, NB),
        in_specs=[SMEM, SMEM, SMEM, head_all(128), aug, per_kv(KR), aug, head_all(128), stat, stat]
        + [ANY] * NG,
        out_specs=[head_all(KR), per_kv(DH), per_kv(DH),
                   pl.BlockSpec((1, 1, 8, B), lambda h, j: (h, j, 0, 0))] + [ANY] * NG,
        out_shape=[_sds((NB, H * KR, B)), _sds((NB, A, B)), _sds((NB, A, B)), _sds((H, NB, 8, B))]
        + [_sds(g.shape, g.dtype) for g in early_g],
        scratch_shapes=[pltpu.VMEM((B, B), F32)] * 4 + [pltpu.VMEM((B, B), BF16)] * 4
        + [pltpu.VMEM((QR, B), F32), pltpu.VMEM((DH, B), F32)] + COMM_SEMS * NG,
        compiler_params=_cp(("arbitrary", "arbitrary")),
    )(fmax, fmin, thr, qTa, kaug, kT, vaug, doTa, lse, delta, *early_g)


def _conv_tail(u1, lng, lnb, beta_c):
    mu = _rowmean(u1)
    d = u1 - mu
    rstd = lax.rsqrt(_rowmean(d * d) + EPS)
    xhat = d * rstd
    u2 = xhat * lng + lnb
    sg = _sigmoid(u2)
    u3 = u2 * sg
    rc = lax.rsqrt(_rowmean(u3 * u3) + EPS)
    n3 = u3 * rc
    return rstd, xhat, u2, sg, rc, n3, n3 * beta_c


def _attn_tail(oT, beta_a_col):
    ra = lax.rsqrt(jnp.mean(oT * oT, axis=0, keepdims=True) + EPS)
    ohat = oT * ra
    return ra, ohat, ohat * beta_a_col


CONV_ROWS = 32
CORR_ROWS = 16


def _fill_shifted(buf, sh):
    rows = sh.shape[1]
    for ph in range(8):
        sh[ph] = buf[pl.ds(ph, rows), :]


def _tap_rows(sh, r0, o, rows):
    return sh[o % 8, pl.ds(pl.multiple_of(r0 + 8 * (o // 8), 8), rows), :]


def _depthwise(sh, w_ref, offs, bias, out_ref, B):
    def chunk(ci, carry):
        r0 = pl.multiple_of(ci * CONV_ROWS, CONV_ROWS)
        acc = jnp.broadcast_to(bias, (CONV_ROWS, CW))
        for k, o in enumerate(offs):
            acc = acc + w_ref[k:k + 1, :] * _tap_rows(sh, r0, o, CONV_ROWS)
        out_ref[pl.ds(r0, CONV_ROWS), :] = acc
        return carry

    lax.fori_loop(0, B // CONV_ROWS, chunk, 0)


def _tap_gradients(sh, d_ref, offs, dw_ref, B):
    for g0 in range(0, len(offs), 8):
        ks = list(range(g0, min(g0 + 8, len(offs))))

        def chunk(ci, accs, ks=ks):
            r0 = pl.multiple_of(ci * CORR_ROWS, CORR_ROWS)
            d = d_ref[pl.ds(r0, CORR_ROWS), :]
            out = []
            for a, k in zip(accs, ks):
                pr = d * _tap_rows(sh, r0, offs[k], CORR_ROWS)
                out.append(a + pr[0:8] + pr[8:16])
            return tuple(out)

        accs = lax.fori_loop(0, B // CORR_ROWS, chunk, tuple(jnp.zeros((8, CW), F32) for _ in ks))
        for a, k in zip(accs, ks):
            dw_ref[k:k + 1, :] = dw_ref[k:k + 1, :] + _colsum(a)


def conv_merge_out(u0, x, oT, conv_w, conv_b, lng, lnb, beta_c, beta_a_col, wo, g1):
    S = x.shape[0]
    B = BLK
    NB = S // B

    def body(uc_ref, up_ref, x_ref, oT_ref, w_ref, cb_ref, lng_ref, lnb_ref, bc_ref, ba_ref, wo_ref,
             g1_ref, x1_ref, o_ref, u1_ref, ubuf, sh):
        i = pl.program_id(0)
        ubuf[0:HALO, :] = jnp.where(i > 0, up_ref[B - HALO:B, :], 0.0)
        ubuf[HALO:HALO + B, :] = uc_ref[...]
        ubuf[HALO + B:HALO + B + 8, :] = jnp.zeros((8, CW), F32)
        _fill_shifted(ubuf, sh)
        _depthwise(sh, w_ref, [HALO - (KC - 1) + k for k in range(KC)], cb_ref[...], u1_ref, B)
        mc = _conv_tail(u1_ref[...], lng_ref[...], lnb_ref[...], bc_ref[...])[-1]
        maT = _attn_tail(oT_ref[0], ba_ref[...])[-1]
        ma = maT.T
        o = _nn(ma.astype(BF16), wo_ref[0:A, :]) + _nn(mc.astype(BF16), wo_ref[A:D, :])
        o_ref[...] = o
        x1_ref[...] = x_ref[...] + g1_ref[...] * o

    row = lambda w: pl.BlockSpec((B, w), lambda i: (i, 0))
    return _pc(
        body, name="conv_merge_out", grid=(NB,),
        in_specs=[row(CW), pl.BlockSpec((B, CW), lambda i: (jnp.maximum(i - 1, 0), 0)), row(D),
                  pl.BlockSpec((1, A, B), lambda i: (i, 0, 0)), _const((32, CW)), _const((1, CW)),
                  _const((1, CW)), _const((1, CW)), _const((1, CW)), _const((A, 1)), _const((D, D)),
                  _const((1, D))],
        out_specs=[row(D), row(D), row(CW)],
        out_shape=[_sds((S, D)), _sds((S, D)), _sds((S, CW))],
        scratch_shapes=[pltpu.VMEM((B + HALO + 8, CW), F32), pltpu.VMEM((8, B + HALO, CW), F32)],
        compiler_params=_cp(("arbitrary",)),
    )(u0, u0, x, oT, conv_w, conv_b, lng, lnb, beta_c, beta_a_col, wo, g1)


def mlp_fwd_loss(x1, tgt, n2g, sc2, sh2, g2, w1, w2):
    S = x1.shape[0]
    B = BLK
    NB = S // B
    NC = FF // FCH

    def body(x1_ref, t_ref, g_ref, sc_ref, sh_ref, g2_ref, w1_ref, w2_ref, dy_ref, loss_ref, dg2_ref, ra_ref,
             hs, acc):
        i = pl.program_id(0)
        c = pl.program_id(1)

        @pl.when(jnp.logical_and(i == 0, c == 0))
        def _():
            loss_ref[...] = jnp.zeros_like(loss_ref)
            dg2_ref[...] = jnp.zeros_like(dg2_ref)

        @pl.when(c == 0)
        def _():
            hs[...] = _norm_mod(x1_ref[...], g_ref[...], sc_ref[...], sh_ref[...])[2].astype(BF16)
            acc[...] = jnp.zeros_like(acc)

        a = jnp.maximum(_nn(hs[...], w1_ref[...]), 0.0)
        ra_ref[...] = a.astype(BF16)
        acc[...] = acc[...] + _nn((a * a).astype(BF16), w2_ref[...])

        @pl.when(c == NC - 1)
        def _():
            y2 = acc[...]
            e = x1_ref[...] + g2_ref[...] * y2 - t_ref[...]
            dy = e * (1.0 / D)
            dy_ref[...] = dy
            loss_ref[...] = loss_ref[...] + 0.5 * _colsum(_rowmean(e * e))
            dg2_ref[...] = dg2_ref[...] + _colsum(dy * y2)

    row = pl.BlockSpec((B, D), lambda i, c: (i, 0))
    vec = pl.BlockSpec((1, D), lambda i, c: (0, 0))
    return _pc(
        body, name="mlp_fwd_loss", grid=(NB, NC),
        in_specs=[row, row, vec, vec, vec, vec, pl.BlockSpec((D, FCH), lambda i, c: (0, c)),
                  pl.BlockSpec((FCH, D), lambda i, c: (c, 0))],
        out_specs=[row, pl.BlockSpec((1, 1), lambda i, c: (0, 0)), vec,
                   pl.BlockSpec((B, FCH), lambda i, c: (i, c))],
        out_shape=[_sds((S, D)), _sds((1, 1)), _sds((1, D)), _sds((S, FF), BF16)],
        scratch_shapes=[pltpu.VMEM((B, D), BF16), pltpu.VMEM((B, D), F32)],
        compiler_params=_cp(("arbitrary", "arbitrary")),
    )(x1, tgt, n2g, sc2, sh2, g2, w1, w2)


def mlp_bwd(x1, dy, ra, n2g, sc2, sh2, g2, w1, w2):
    S = x1.shape[0]
    B = BLK
    NB = S // B
    NC = FF // FCH

    def body(x1_ref, dy_ref, ra_ref, g_ref, sc_ref, sh_ref, g2_ref, w1_ref, w2_ref, dw1_ref, dw2t_ref, dh_ref):
        i = pl.program_id(1)

        @pl.when(i == 0)
        def _():
            dw1_ref[...] = jnp.zeros_like(dw1_ref)
            dw2t_ref[...] = jnp.zeros_like(dw2t_ref)

        h2 = _norm_mod(x1_ref[...], g_ref[...], sc_ref[...], sh_ref[...])[2]
        w1c = w1_ref[...]
        w2c = w2_ref[...]
        ra = ra_ref[...].astype(F32)
        dy2 = dy_ref[...] * g2_ref[...]
        db = _nt(dy2.astype(BF16), w2c)
        dab = (db * (2.0 * ra)).astype(BF16)
        dh_ref[0] = _nt(dab, w1c)
        dw1_ref[...] = dw1_ref[...] + _nn(h2.T.astype(BF16), dab)
        dw2t_ref[...] = dw2t_ref[...] + _nn(dy2.T.astype(BF16), (ra * ra).astype(BF16))

    row = pl.BlockSpec((B, D), lambda c, i: (i, 0))
    vec = pl.BlockSpec((1, D), lambda c, i: (0, 0))
    wcol = pl.BlockSpec((D, FCH), lambda c, i: (0, c))
    return _pc(
        body, name="mlp_bwd", grid=(NC, NB),
        in_specs=[row, row, pl.BlockSpec((B, FCH), lambda c, i: (i, c)), vec, vec, vec, vec, wcol,
                  pl.BlockSpec((FCH, D), lambda c, i: (c, 0))],
        out_specs=[wcol, wcol, pl.BlockSpec((1, B, D), lambda c, i: (c, i, 0))],
        out_shape=[_sds((D, FF)), _sds((D, FF)), _sds((NC, S, D))],
        compiler_params=_cp(("arbitrary", "arbitrary")),
    )(x1, dy, ra, n2g, sc2, sh2, g2, w1, w2)


def merge_bwd(dh2p, x1, dy, o, oT, u1, n2g, sc2, g1, lng, lnb, beta_c, beta_a_col, wo):
    S = x1.shape[0]
    B = BLK
    NB = S // B
    NC = dh2p.shape[0]

    def body(dh_ref, x1_ref, dy_ref, o_ref, oT_ref, u1_ref, g_ref, sc_ref, g1_ref, lng_ref, lnb_ref,
             bc_ref, ba_ref, wo_ref,
             dx1_ref, doTa_ref, dl_ref, du1_ref, dwo_ref, v1_ref, v2_ref, dba_ref):
        i = pl.program_id(0)

        @pl.when(i == 0)
        def _():
            dwo_ref[...] = jnp.zeros_like(dwo_ref)
            v1_ref[...] = jnp.zeros_like(v1_ref)
            v2_ref[...] = jnp.zeros_like(v2_ref)
            dba_ref[...] = jnp.zeros_like(dba_ref)

        dh2 = dh_ref[0]
        for cc in range(1, NC):
            dh2 = dh2 + dh_ref[cc]
        x1 = x1_ref[...]
        g = g_ref[...]
        sc = sc_ref[...]
        r2 = lax.rsqrt(_rowmean(x1 * x1) + EPS)
        xh = x1 * r2
        dhx = dh2 * xh
        v1_ref[0:1, :] = v1_ref[0:1, :] + _colsum(dh2)
        v1_ref[1:2, :] = v1_ref[1:2, :] + _colsum(dhx) * g
        v1_ref[2:3, :] = v1_ref[2:3, :] + _colsum(dhx) * (1.0 + sc)
        dxh = dh2 * (g * (1.0 + sc))
        dx1 = dy_ref[...] + r2 * (dxh - xh * _rowmean(dxh * xh))
        dx1_ref[...] = dx1
        v1_ref[3:4, :] = v1_ref[3:4, :] + _colsum(dx1 * o_ref[...])
        dob = (dx1 * g1_ref[...]).astype(BF16)

        lng = lng_ref[...]
        bc = bc_ref[...]
        rstd, xhat, u2, sg, rc, n3, mc = _conv_tail(u1_ref[...], lng, lnb_ref[...], bc)
        ba = ba_ref[...]
        oT = oT_ref[0]
        ra, ohat, maT = _attn_tail(oT, ba)
        dwo_ref[0:A, :] = dwo_ref[0:A, :] + _nn(maT.astype(BF16), dob)
        dwo_ref[A:D, :] = dwo_ref[A:D, :] + _nn(mc.T.astype(BF16), dob)
        dmaT = _nt(wo_ref[0:A, :], dob)
        dmc = _nt(dob, wo_ref[A:D, :])

        dba_ref[...] = dba_ref[...] + _lanesum(dmaT * ohat)
        dohat = dmaT * ba
        doT = ra * (dohat - ohat * jnp.mean(dohat * ohat, axis=0, keepdims=True))
        prod = doT * oT
        zpad = jnp.zeros((128 - DH, B), BF16)
        for hh in range(H):
            sl = slice(hh * DH, (hh + 1) * DH)
            dl_ref[hh, 0] = _colsum(prod[sl])
            doTa_ref[0, hh * 128:hh * 128 + DH, :] = doT[sl].astype(BF16)
            doTa_ref[0, hh * 128 + DH:(hh + 1) * 128, :] = zpad

        v2_ref[0:1, :] = v2_ref[0:1, :] + _colsum(dmc * n3)
        dn3 = dmc * bc
        du3 = rc * (dn3 - n3 * _rowmean(dn3 * n3))
        du2 = du3 * (sg * (1.0 + u2 * (1.0 - sg)))
        v2_ref[1:2, :] = v2_ref[1:2, :] + _colsum(du2 * xhat)
        v2_ref[2:3, :] = v2_ref[2:3, :] + _colsum(du2)
        dxhat = du2 * lng
        du1_ref[...] = rstd * (dxhat - _rowmean(dxhat) - xhat * _rowmean(dxhat * xhat))

    row = lambda w: pl.BlockSpec((B, w), lambda i: (i, 0))
    return _pc(
        body, name="merge_bwd", grid=(NB,),
        in_specs=[pl.BlockSpec((NC, B, D), lambda i: (0, i, 0)), row(D), row(D), row(D),
                  pl.BlockSpec((1, A, B), lambda i: (i, 0, 0)), row(CW), _const((1, D)), _const((1, D)),
                  _const((1, D)), _const((1, CW)), _const((1, CW)), _const((1, CW)), _const((A, 1)),
                  _const((D, D))],
        out_specs=[row(D), pl.BlockSpec((1, H * 128, B), lambda i: (i, 0, 0)),
                   pl.BlockSpec((H, 1, 1, B), lambda i: (0, i, 0, 0)), row(CW), _const((D, D)),
                   _const((8, D)), _const((8, CW)), _const((A, 1))],
        out_shape=[_sds((S, D)), _sds((NB, H * 128, B), BF16), _sds((H, NB, 1, B)), _sds((S, CW)),
                   _sds((D, D)), _sds((8, D)), _sds((8, CW)), _sds((A, 1))],
        compiler_params=_cp(("arbitrary",)),
    )(dh2p, x1, dy, o, oT, u1, n2g, sc2, g1, lng, lnb, beta_c, beta_a_col, wo)


def head_bwd(du1, u0, alg, zqk, fgT, dqT, dkT, dvT, dfk, conv_w, qg_col, kg_col, bf_col, tri_lo):
    S = u0.shape[0]
    B = BLK
    NB = S // B

    def body(dc_ref, dn_ref, uc_ref, up_ref, alg_ref, zqk_ref, fg_ref, dq_ref, dk_ref, dv_ref, dfk_ref,
             w_ref, qg_ref, kg_ref, bf_ref, tri_ref,
             dzr_ref, dzT_ref, dcw_ref, vc_ref, dqg_ref, dkg_ref, dbf_ref,
             buf, sh, du0_ref, carry, fbuf):
        pid = pl.program_id(0)
        ri = NB - 1 - pid

        @pl.when(pid == 0)
        def _():
            carry[...] = jnp.zeros_like(carry)
            dcw_ref[...] = jnp.zeros_like(dcw_ref)
            vc_ref[...] = jnp.zeros_like(vc_ref)
            dqg_ref[...] = jnp.zeros_like(dqg_ref)
            dkg_ref[...] = jnp.zeros_like(dkg_ref)
            dbf_ref[...] = jnp.zeros_like(dbf_ref)

        zero8 = jnp.zeros((8, CW), F32)
        buf[0:B, :] = dc_ref[...]
        buf[B:B + HALO, :] = jnp.where(ri < NB - 1, dn_ref[0:HALO, :], 0.0)
        buf[B + HALO:B + HALO + 8, :] = zero8
        _fill_shifted(buf, sh)
        _depthwise(sh, w_ref, [KC - 1 - k for k in range(KC)], jnp.zeros((1, CW), F32), du0_ref, B)
        buf[0:HALO, :] = jnp.where(ri > 0, up_ref[B - HALO:B, :], 0.0)
        buf[HALO:HALO + B, :] = uc_ref[...]
        buf[HALO + B:HALO + B + 8, :] = zero8
        _fill_shifted(buf, sh)
        _tap_gradients(sh, dc_ref, [HALO - (KC - 1) + k for k in range(KC)], dcw_ref, B)
        vc_ref[0:1, :] = vc_ref[0:1, :] + _colsum(dc_ref[...])

        du0 = du0_ref[...]
        al = alg_ref[:, 0:CW]
        sg = _sigmoid(alg_ref[:, CW:2 * CW])
        dzr_ref[:, 0:CW] = (du0 * sg).astype(BF16)
        dzr_ref[:, CW:2 * CW] = (du0 * al * sg * (1.0 - sg)).astype(BF16)

        dqg = jnp.zeros((DH, B), F32)
        dkg = jnp.zeros((DH, B), F32)
        qg = qg_ref[...]
        kg = kg_ref[...]
        for hh in range(H):
            sl = slice(hh * DH, (hh + 1) * DH)
            q = zqk_ref[sl, :]
            rq = lax.rsqrt(jnp.mean(q * q, axis=0, keepdims=True) + EPS)
            qn = q * rq
            dqh = dq_ref[0, hh * KR:hh * KR + DH, :] * 0.125
            dqg = dqg + dqh * qn
            dqn = dqh * qg
            dzT_ref[0, sl, :] = (rq * (dqn - qn * jnp.mean(dqn * qn, axis=0, keepdims=True))).astype(BF16)
            k = zqk_ref[A + hh * DH:A + (hh + 1) * DH, :]
            rk = lax.rsqrt(jnp.mean(k * k, axis=0, keepdims=True) + EPS)
            kn = k * rk
            dkh = dk_ref[0, sl, :] * (1.0 / LOG2E)
            dkg = dkg + dkh * kn
            dkn = dkh * kg
            dzT_ref[0, A + hh * DH:A + (hh + 1) * DH, :] = (
                rk * (dkn - kn * jnp.mean(dkn * kn, axis=0, keepdims=True))).astype(BF16)
            fbuf[hh:hh + 1, :] = dq_ref[0, hh * KR + DH:hh * KR + DH + 1, :] - dfk_ref[hh, 0, 0:1, :]
        dqg_ref[...] = dqg_ref[...] + _lanesum(dqg)
        dkg_ref[...] = dkg_ref[...] + _lanesum(dkg)
        dzT_ref[0, 2 * A:3 * A, :] = dv_ref[0].astype(BF16)

        dF = fbuf[...]
        a1, a2, a3 = _split3(dF)
        tr = tri_ref[...]
        dlogf = _nn(a1, tr) + _nn(a2, tr) + _nn(a3, tr) + carry[...]
        carry[...] = carry[...] + _lanesum(dF)
        dfg = dlogf * _sigmoid(-(fg_ref[...] + bf_ref[...]))
        dbf_ref[...] = dbf_ref[...] + _lanesum(dfg)
        dzT_ref[0, 3 * A:ZT_ROWS, :] = jnp.concatenate([dfg, jnp.zeros((8, B), F32)], axis=0).astype(BF16)

    rrow = lambda w: pl.BlockSpec((B, w), lambda p: (NB - 1 - p, 0))
    rts = lambda r: pl.BlockSpec((1, r, B), lambda p: (NB - 1 - p, 0, 0))
    return _pc(
        body, name="head_bwd", grid=(NB,),
        in_specs=[rrow(CW), pl.BlockSpec((B, CW), lambda p: (jnp.minimum(NB - p, NB - 1), 0)),
                  rrow(CW), pl.BlockSpec((B, CW), lambda p: (jnp.maximum(NB - 2 - p, 0), 0)),
                  rrow(2 * CW), pl.BlockSpec((2 * A, B), lambda p: (0, NB - 1 - p)),
                  pl.BlockSpec((8, B), lambda p: (0, NB - 1 - p)), rts(H * KR), rts(A), rts(A),
                  pl.BlockSpec((H, 1, 8, B), lambda p: (0, NB - 1 - p, 0, 0)),
                  _const((32, CW)), _const((DH, 1)), _const((DH, 1)), _const((8, 1)), _const((B, B))],
        out_specs=[rrow(2 * CW), rts(ZT_ROWS), _const((32, CW)), _const((8, CW)), _const((DH, 1)),
                   _const((DH, 1)), _const((8, 1))],
        out_shape=[_sds((S, 2 * CW), BF16), _sds((NB, ZT_ROWS, B), BF16), _sds((32, CW)), _sds((8, CW)),
                   _sds((DH, 1)), _sds((DH, 1)), _sds((8, 1))],
        scratch_shapes=[pltpu.VMEM((B + HALO + 8, CW), F32), pltpu.VMEM((8, B + HALO, CW), F32),
                        pltpu.VMEM((B, CW), F32), pltpu.VMEM((8, 1), F32), pltpu.VMEM((8, B), F32)],
        compiler_params=_cp(("arbitrary",)),
    )(du1, du1, u0, u0, alg, zqk, fgT, dqT, dkT, dvT, dfk, conv_w, qg_col, kg_col, bf_col, tri_lo)


def in_bwd_w(x, dzr, dzT, n1g, sc1, sh1):
    S = x.shape[0]
    B = BLK
    NB = S // B

    def body(x_ref, dzr_ref, dzT_ref, g_ref, sc_ref, sh_ref, dwT_hbm, dwr_hbm, dwT_acc, dwr_acc, sem):
        i = pl.program_id(0)

        @pl.when(i == 0)
        def _():
            dwT_acc[...] = jnp.zeros_like(dwT_acc)
            dwr_acc[...] = jnp.zeros_like(dwr_acc)

        h1 = _norm_mod(x_ref[...], g_ref[...], sc_ref[...], sh_ref[...])[2]
        dwT_acc[...] = dwT_acc[...] + _nn(dzT_ref[0], h1.astype(BF16))
        dwr_acc[...] = dwr_acc[...] + _nn(h1.T.astype(BF16), dzr_ref[...])

        @pl.when(i == NB - 1)
        def _():
            for src, dst in ((dwT_acc, dwT_hbm), (dwr_acc, dwr_hbm)):
                cp = pltpu.make_async_copy(src, dst, sem)
                cp.start()
                cp.wait()

    row = lambda w: pl.BlockSpec((B, w), lambda i: (i, 0))
    return _pc(
        body, name="in_bwd_w", grid=(NB,),
        in_specs=[row(D), row(2 * CW), pl.BlockSpec((1, ZT_ROWS, B), lambda i: (i, 0, 0)),
                  _const((1, D)), _const((1, D)), _const((1, D))],
        out_specs=[ANY, ANY],
        out_shape=[_sds((ZT_ROWS, D)), _sds((D, 2 * CW))],
        scratch_shapes=[pltpu.VMEM((ZT_ROWS, D), F32), pltpu.VMEM((D, 2 * CW), F32),
                        pltpu.SemaphoreType.DMA(())],
        compiler_params=_cp(("arbitrary",)),
    )(x, dzr, dzT, n1g, sc1, sh1)


def in_bwd_x(x, dx1, dzr, dzT, n1g, sc1, sh1, w_qkvf, wr, late_g):
    S = x.shape[0]
    B = BLK
    NB = S // B

    def body(x_ref, dx1_ref, dzr_ref, dzT_ref, g_ref, sc_ref, sh_ref, w_hbm, wr_hbm, lg_ref,
             gx_ref, v_ref, recv_ref, w_v, wr_v, sem, send_sems, recv_sems, local_sem):
        i = pl.program_id(0)

        @pl.when(i == 0)
        def _():
            _start_all(_direct_copies(True, lg_ref, recv_ref, send_sems, recv_sems, local_sem))
            for src, dst in ((w_hbm, w_v), (wr_hbm, wr_v)):
                cp = pltpu.make_async_copy(src, dst, sem)
                cp.start()
                cp.wait()
            v_ref[...] = jnp.zeros_like(v_ref)

        g = g_ref[...]
        sc = sc_ref[...]
        r1, xh, _ = _norm_mod(x_ref[...], g, sc, sh_ref[...])
        dh1 = _nt(dzr_ref[...], wr_v[...]) + _nn(w_v[...], dzT_ref[0]).T
        dhx = dh1 * xh
        v_ref[0:1, :] = v_ref[0:1, :] + _colsum(dh1)
        v_ref[1:2, :] = v_ref[1:2, :] + _colsum(dhx) * g
        v_ref[2:3, :] = v_ref[2:3, :] + _colsum(dhx) * (1.0 + sc)
        dxh = dh1 * (g * (1.0 + sc))
        gx_ref[...] = dx1_ref[...] + r1 * (dxh - xh * _rowmean(dxh * xh))

        @pl.when(i == NB - 1)
        def _():
            _wait_all(_direct_copies(True, lg_ref, recv_ref, send_sems, recv_sems, local_sem))

    row = lambda w: pl.BlockSpec((B, w), lambda i: (i, 0))
    return _pc(
        body, name="in_bwd_x", grid=(NB,),
        in_specs=[row(D), row(D), row(2 * CW), pl.BlockSpec((1, ZT_ROWS, B), lambda i: (i, 0, 0)),
                  _const((1, D)), _const((1, D)), _const((1, D)), ANY, ANY, ANY],
        out_specs=[row(D), _const((8, D)), ANY],
        out_shape=[_sds((S, D)), _sds((8, D)), _sds(late_g.shape, late_g.dtype)],
        scratch_shapes=[pltpu.VMEM((D, ZT_ROWS), BF16), pltpu.VMEM((D, 2 * CW), BF16),
                        pltpu.SemaphoreType.DMA(())] + COMM_SEMS,
        compiler_params=_cp(("arbitrary",)),
    )(x, dx1, dzr, dzT, n1g, sc1, sh1, w_qkvf, wr, late_g)


def _adam_math(g, w, m, v):
    m = ADAM_B1 * m + (1.0 - ADAM_B1) * g
    v = ADAM_B2 * v + (1.0 - ADAM_B2) * (g * g)
    m_hat = m / (1.0 - ADAM_B1 ** ADAM_STEP)
    v_hat = v / (1.0 - ADAM_B2 ** ADAM_STEP)
    delta = -ADAM_LR * (m_hat / (jnp.sqrt(v_hat) + ADAM_EPS) + ADAM_WD * w)
    return delta, m, v


def _row_tile(R):
    for t in (1024, 512, 256, 128, 64, 32, 16, 8):
        if R % t == 0:
            return t
    return R


def sum_slots(parts, name):
    K, R, C = parts.shape
    T = _row_tile(R)

    def body(p_ref, o_ref):
        s = p_ref[0]
        for k in range(1, K):
            s = s + p_ref[k]
        o_ref[...] = s

    return _pc(body, name=name, grid=(R // T,),
               in_specs=[pl.BlockSpec((K, T, C), lambda i: (0, i, 0))],
               out_specs=pl.BlockSpec((T, C), lambda i: (i, 0)), out_shape=_sds((R, C)),
               compiler_params=_cp(("arbitrary",)))(parts)


def adamw_slots(parts, w, m, v, name):
    K, R, C = parts.shape
    T = R
    while K * T * C * 4 > (8 << 20) and T % 16 == 0:
        T //= 2

    def body(p_ref, w_ref, m_ref, v_ref, g_ref, d_ref, nm_ref, nv_ref):
        g = p_ref[0].astype(F32)
        for k in range(1, K):
            g = g + p_ref[k].astype(F32)
        g_ref[...] = g
        d_ref[...], nm_ref[...], nv_ref[...] = _adam_math(g, w_ref[...], m_ref[...], v_ref[...])

    t2 = pl.BlockSpec((T, C), lambda i: (i, 0))
    return _pc(body, name=name, grid=(R // T,),
               in_specs=[pl.BlockSpec((K, T, C), lambda i: (0, i, 0)), t2, t2, t2],
               out_specs=[t2, t2, t2, t2], out_shape=[_sds((R, C))] * 4,
               compiler_params=_cp(("arbitrary",)))(parts, w, m, v)


def ada_grad_adamw(cT, dmod_cols, w, m, v):
    NCOL = w.shape[1]

    def body(cT_ref, dm_ref, w_ref, m_ref, v_ref, g_ref, d_ref, nm_ref, nv_ref):
        def term(b):
            cv = cT_ref[b]
            return (cv * _sigmoid(cv)) * dm_ref[b:b + 1, :]

        g = term(0)
        for b in range(1, N_DEV):
            g = g + term(b)
        g_ref[...] = g
        d_ref[...], nm_ref[...], nv_ref[...] = _adam_math(g, w_ref[...], m_ref[...], v_ref[...])

    full = _const((D, NCOL))
    return _pc(body, name="ada_grad_adamw", grid=(1,),
               in_specs=[_const((N_DEV, D, 1)), _const((N_DEV, NCOL)), full, full, full],
               out_specs=[full, full, full, full], out_shape=[_sds((D, NCOL))] * 4,
               compiler_params=_cp())(cT, dmod_cols, w, m, v)


def _pack_rows(vecs, rows=None):
    flat = jnp.concatenate([jnp.ravel(v) for v in vecs])
    n = flat.shape[0]
    r = -(-n // 1024) * 8 if rows is None else rows
    return jnp.pad(flat, (0, r * 128 - n)).reshape(r, 128)


def _unpack_rows(packed, shapes):
    flat = packed.reshape(-1)
    out, off = [], 0
    for s in shapes:
        n = 1
        for d in s:
            n *= d
        out.append(flat[off:off + n].reshape(s))
        off += n
    return out


def _cols_from_shards(g, rows, cols):
    return g.reshape(N_DEV, rows, cols).transpose(1, 0, 2).reshape(rows, N_DEV * cols)


def _cols_to_shards(w, cols):
    rows = w.shape[0]
    return w.reshape(rows, N_DEV, cols).transpose(1, 0, 2).reshape(N_DEV, rows * cols)


def kernel(x, c, w_ada, b_ada, norm1_g, w_in, q_norm_g, k_norm_g, b_f, conv_w, conv_b, conv_ln_g, conv_ln_b, beta_attn, beta_conv, w_out, norm2_g, w_ff1, w_ff2, loss_target, m_w_ada, m_b_ada, m_norm1_g, m_w_in, m_q_norm_g, m_k_norm_g, m_b_f, m_conv_w, m_conv_b, m_conv_ln_g, m_conv_ln_b, m_beta_attn, m_beta_conv, m_w_out, m_norm2_g, m_w_ff1, m_w_ff2, v_w_ada, v_b_ada, v_norm1_g, v_w_in, v_q_norm_g, v_k_norm_g, v_b_f, v_conv_w, v_conv_b, v_conv_ln_g, v_conv_ln_b, v_beta_attn, v_beta_conv, v_w_out, v_norm2_g, v_w_ff1, v_w_ff2):
    S = x.shape[1]
    B = BLK
    NB = S // B
    me = 4 * lax.axis_index("x") + 2 * lax.axis_index("y") + lax.axis_index("c")
    xs = x[0]
    tgt = loss_target[0]

    ADA_C, IN_C, FF_C, CV_C = w_ada.shape[2], w_in.shape[2], w_ff1.shape[2], conv_w.shape[2]
    OUT_R, FF_R = w_out.shape[1], w_ff2.shape[1]

    n_in, n_out, n_f1, n_f2 = D * IN_C, OUT_R * D, D * FF_C, FF_R * D
    early = w_in.reshape(-1).astype(BF16)
    rows_e = -(-early.shape[0] // (128 * 16)) * 16
    early = jnp.pad(early, (0, rows_e * 128 - early.shape[0])).reshape(rows_e, 128)
    small0 = lax.bitcast_convert_type(_pack_rows([c, conv_w], rows=24), BF16).reshape(48, 128)
    ge = all_gather(jnp.concatenate([early, small0], axis=0), "ag_weights_early")
    gw = ge[:, 0:rows_e].reshape(N_DEV, rows_e * 128)
    gs0 = lax.bitcast_convert_type(ge[:, rows_e:].reshape(N_DEV, 24, 128, 2), F32).reshape(N_DEV, 24 * 128)
    late_w = jnp.concatenate([w_out.reshape(-1), w_ff1.reshape(-1), w_ff2.reshape(-1)]).astype(BF16)
    late_w = late_w.reshape(-1, 128)

    w_in_full = _cols_from_shards(gw[:, 0:n_in], D, IN_C)
    c_all = gs0[:, 0:D]
    conv_w_full = _cols_from_shards(gs0[:, D:D + KC * CV_C], KC, CV_C)
    conv_w_pad = jnp.pad(conv_w_full, ((0, 32 - KC), (0, 0)))

    w_qkvf = jnp.pad(w_in_full[:, 0:3 * A + H], ((0, 0), (0, ZT_ROWS - 3 * A - H)))
    wT = w_qkvf.T
    wr = w_in_full[:, 3 * A + H:]

    qg_col = q_norm_g.reshape(DH, 1)
    kg_col = k_norm_g.reshape(DH, 1)
    bf_col = b_f.reshape(H, 1)
    beta_a_col = beta_attn.reshape(A, 1)
    ii = lax.broadcasted_iota(jnp.int32, (B, B), 0)
    jj = lax.broadcasted_iota(jnp.int32, (B, B), 1)
    tri_up = (ii <= jj).astype(BF16)
    tri_lo = (ii >= jj).astype(BF16)

    modc = mod_columns(c_all, w_ada[0], lax.dynamic_slice(b_ada, (0, me * ADA_C), (1, ADA_C)))
    gm = all_gather(modc, "ag_mod")
    mod = lax.dynamic_index_in_dim(gm, me, axis=1, keepdims=False).reshape(1, N_DEV * ADA_C)
    sh1, sc1, g1, sh2, sc2, g2 = [mod[:, k * D:(k + 1) * D] for k in range(6)]
    qTa, kT, kaug, vT, vaug, zqk, fgT, alg, u0, stat = fwd_in(xs, norm1_g, sc1, sh1, wT, wr, qg_col,
                                                             kg_col, bf_col, tri_up)
    fmax = stat[:, :, 0].T
    fmin = stat[:, :, 1].T
    qk_max = jnp.sqrt(jnp.max(stat[:, :, 2], axis=0) * jnp.max(stat[:, :, 3], axis=0))
    thr = -(PRUNE + (2.02 / LOG2E) * qk_max)
    oT, lse, gl = attn_fwd(fmax, fmin, thr, qTa, kaug, vT, late_w)
    gl = gl.reshape(N_DEV, -1)
    wo_full = gl[:, 0:n_out].reshape(D, D)
    w1_full = _cols_from_shards(gl[:, n_out:n_out + n_f1], D, FF_C)
    w2_full = gl[:, n_out + n_f1:].reshape(FF, D)
    x1, o, u1 = conv_merge_out(u0, xs, oT, conv_w_pad, conv_b, conv_ln_g, conv_ln_b, beta_conv,
                               beta_a_col, wo_full, g1)
    dy, loss_part, dg2, ra = mlp_fwd_loss(x1, tgt, norm2_g, sc2, sh2, g2, w1_full, w2_full)

    dw1, dw2t, dh2p = mlp_bwd(x1, dy, ra, norm2_g, sc2, sh2, g2, w1_full, w2_full)
    dx1, doTa, delta, du1, dwo, v1, v2, dba = merge_bwd(dh2p, x1, dy, o, oT, u1, norm2_g, sc2, g1,
                                                         conv_ln_g, conv_ln_b, beta_conv, beta_a_col, wo_full)
    early_g = [dwo.reshape(N_DEV, OUT_R, D), dw1.reshape(D, N_DEV, FF_C).transpose(1, 0, 2),
               dw2t.T.reshape(N_DEV, FF_R, D)]
    dqT, dkT, dvT, dfk, r_out, r_f1, r_f2 = attn_bwd(fmax, fmin, thr, qTa, kaug, kT, vaug, doTa, lse,
                                                     delta, early_g)
    dzr, dzT, dcw, vc, dqg, dkg, dbf = head_bwd(du1, u0, alg, zqk, fgT, dqT, dkT, dvT, dfk, conv_w_pad,
                                                qg_col, kg_col, bf_col, tri_lo)
    dwT, dwr = in_bwd_w(xs, dzr, dzT, norm1_g, sc1, sh1)

    dw_in = jnp.concatenate([dwT.T[:, 0:3 * A + H], dwr], axis=1)
    late_g = _cols_to_shards(dw_in, IN_C)
    rows_b = -(-late_g.shape[1] // (128 * 256)) * 256
    late_g = jnp.pad(late_g, ((0, 0), (0, rows_b * 128 - late_g.shape[1]))).astype(BF16)
    grad_x, v0, recv_b = in_bwd_x(xs, dx1, dzr, dzT, norm1_g, sc1, sh1, w_qkvf, wr,
                                  late_g.reshape(N_DEV, rows_b, 128))

    dmod = jnp.concatenate([v0[0], v0[1], v1[3], v1[0], v1[1], dg2[0]])
    small1 = _pack_rows([dmod, v0[2], v1[2], dcw, vc[0], v2[1], v2[2], v2[0], dba, dqg, dkg,
                         jnp.pad(dbf.reshape(-1), (0, 120)), jnp.pad(loss_part.reshape(-1), (0, 127))])
    gs1 = all_gather(small1, "ag_small_bwd")
    tot = sum_slots(gs1, "sum_small")
    (g_b_ada, g_n1, g_n2, g_cw, g_cb, g_lng, g_lnb, g_bc, g_ba, g_qg, g_kg, g_bf, loss_v) = _unpack_rows(
        tot, [(1, 6 * D), (1, D), (1, D), (32, CW), (1, CW), (1, CW), (1, CW), (1, CW), (1, A),
              (1, DH), (1, DH), (1, 128), (1, 128)])
    loss = loss_v[0, 0]
    g_bf = g_bf[:, 0:H]
    g_cw_mine = lax.dynamic_slice(g_cw[0:KC], (0, me * CV_C), (KC, CV_C)).reshape(1, KC, CV_C)

    small_names = [(b_ada, m_b_ada, v_b_ada, g_b_ada), (norm1_g, m_norm1_g, v_norm1_g, g_n1),
                   (q_norm_g, m_q_norm_g, v_q_norm_g, g_qg), (k_norm_g, m_k_norm_g, v_k_norm_g, g_kg),
                   (b_f, m_b_f, v_b_f, g_bf), (conv_w, m_conv_w, v_conv_w, g_cw_mine),
                   (conv_b, m_conv_b, v_conv_b, g_cb), (conv_ln_g, m_conv_ln_g, v_conv_ln_g, g_lng),
                   (conv_ln_b, m_conv_ln_b, v_conv_ln_b, g_lnb), (beta_attn, m_beta_attn, v_beta_attn, g_ba),
                   (beta_conv, m_beta_conv, v_beta_conv, g_bc), (norm2_g, m_norm2_g, v_norm2_g, g_n2)]
    shapes_s = [t[0].shape for t in small_names]
    pw, pm, pv, pg = [_pack_rows([t[k] for t in small_names]) for k in range(4)]
    sg_, sd_, sm_, sv_ = adamw_slots(pg[None], pw, pm, pv, "adamw_small")
    sgs, sds, sms, svs = [_unpack_rows(a, shapes_s) for a in (sg_, sd_, sm_, sv_)]

    dmod_all = gs1[:, 0:48, :].reshape(N_DEV, N_DEV, ADA_C)
    dmod_cols = lax.dynamic_index_in_dim(dmod_all, me, axis=1, keepdims=False)
    ga, da, ma_, va_ = ada_grad_adamw(c_all.reshape(N_DEV, D, 1), dmod_cols, w_ada[0], m_w_ada[0], v_w_ada[0])

    res_b = adamw_slots(recv_b, *[_pack_rows(ws, rows=rows_b) for ws in ([w_in], [m_w_in], [v_w_in])],
                        "adamw_in")
    late = []
    for nm, parts, (w_, m_, v_) in (("adamw_out", r_out, (w_out, m_w_out, v_w_out)),
                                    ("adamw_ff1", r_f1, (w_ff1, m_w_ff1, v_w_ff1)),
                                    ("adamw_ff2", r_f2, (w_ff2, m_w_ff2, v_w_ff2))):
        late.append([r[None] for r in adamw_slots(parts, w_[0], m_[0], v_[0], nm)])
    bgs, bds, bms, bvs = [_unpack_rows(res_b[q], [(1, D, IN_C)]) + [late[0][q], late[1][q], late[2][q]]
                          for q in range(4)]

    def assemble(small, ada, bigs):
        (b_ada_, n1_, qg_, kg_, bf_, cw_, cb_, lng_, lnb_, ba_, bc_, n2_) = small
        return [ada.reshape(1, D, ADA_C), b_ada_, n1_, bigs[0], qg_, kg_, bf_, cw_, cb_, lng_, lnb_, ba_, bc_,
                bigs[1], n2_, bigs[2], bigs[3]]

    return (loss, grad_x.reshape(1, S, D), *assemble(sgs, ga, bgs), *assemble(sds, da, bds),
            *assemble(sms, ma_, bms), *assemble(svs, va_, bvs))
```

```python
import functools

import jax
import jax.numpy as jnp
from jax import lax
from jax.experimental import pallas as pl
from jax.experimental.pallas import tpu as pltpu

F32 = jnp.float32
BF16 = jnp.bfloat16

D = 1024
A = 512
CW = 512
H = 8
DH = 64
FF = 4096
KC = 31
HALO = 32
KR = 80
ZT_ROWS = 1552
FCH = 1024
EPS = 1e-6
BLK = 512
N_DEV = 8
VMEM_LIMIT = 56 * 1024 * 1024

ADAM_LR = 0.001
ADAM_B1 = 0.9
ADAM_B2 = 0.999
ADAM_EPS = 1e-08
ADAM_WD = 0.01
ADAM_STEP = 10

MESH = pl.DeviceIdType.MESH
ANY = pl.BlockSpec(memory_space=pl.ANY)
SMEM = pl.BlockSpec(memory_space=pltpu.SMEM)
PRUNE = 110.0
LOG2E = 1.4426950408889634


def _pc(body, **kw):
    return pl.pallas_call(body, **kw)


def _cp(sem=None):
    return pltpu.CompilerParams(dimension_semantics=sem, vmem_limit_bytes=VMEM_LIMIT)


def _sds(shape, dtype=F32):
    return jax.ShapeDtypeStruct(shape, dtype)


def _const(shape):
    n = len(shape)
    return pl.BlockSpec(shape, lambda *a: (0,) * n)


def _nt(a, b):
    return lax.dot_general(a, b, (((1,), (1,)), ((), ())), preferred_element_type=F32)


def _nn(a, b):
    return jnp.dot(a, b, preferred_element_type=F32)


def _sigmoid(v):
    return 1.0 / (1.0 + jnp.exp(-v))


def _split3(v):
    a1 = v.astype(BF16)
    r1 = v - a1.astype(F32)
    a2 = r1.astype(BF16)
    a3 = (r1 - a2.astype(F32)).astype(BF16)
    return a1, a2, a3


def _rowmean(v):
    return jnp.mean(v, axis=-1, keepdims=True)


def _colsum(v):
    return jnp.sum(v, axis=0, keepdims=True)


def _lanesum(v):
    return jnp.sum(v, axis=-1, keepdims=True)


def _coords():
    return lax.axis_index("x"), lax.axis_index("y"), lax.axis_index("c")


def all_gather(x, name):
    R, C = x.shape

    def body(x_ref, out_ref, send_sems, recv_sems, local_sem):
        mx, my, mc = _coords()
        me, sibling = (mx, my, mc), (mx, my, 1 - mc)
        chips = [(1 - mx, my), (mx, 1 - my), (1 - mx, 1 - my)]

        def rows(px, py, pc):
            return out_ref.at[4 * px + 2 * py + pc]

        def copy(k, block, to, src=None):
            return pltpu.make_async_remote_copy(
                src_ref=rows(*block) if src is None else src, dst_ref=rows(*block),
                send_sem=send_sems.at[k], recv_sem=recv_sems.at[k],
                device_id=to, device_id_type=MESH)

        mine = pltpu.make_async_copy(x_ref, rows(*me), local_sem)
        mine.start()
        first = [copy(0, me, sibling, src=x_ref)]
        first += [copy(1 + j, me, (*chip, mc), src=x_ref) for j, chip in enumerate(chips)]
        for cp in first:
            cp.start()
        passed = [copy(4 + j, (*chip, mc), sibling) for j, chip in enumerate(chips)]
        for j, chip in enumerate(chips):
            copy(1 + j, (*chip, mc), me).wait_recv()
            passed[j].start()
        copy(0, sibling, me).wait_recv()
        for j, chip in enumerate(chips):
            copy(4 + j, (*chip, 1 - mc), me).wait_recv()
        for cp in first + passed:
            cp.wait_send()
        mine.wait()

    return _pc(
        body, name=name, out_shape=_sds((N_DEV, R, C), x.dtype),
        in_specs=[ANY], out_specs=ANY,
        scratch_shapes=[pltpu.SemaphoreType.DMA((7,)), pltpu.SemaphoreType.DMA((7,)),
                        pltpu.SemaphoreType.DMA(())],
    )(x)


def _direct_copies(scatter, src_ref, dst_ref, send_sems, recv_sems, local_sem):
    mx, my, mc = _coords()
    me = 4 * mx + 2 * my + mc
    if scatter:
        local = pltpu.make_async_copy(src_ref.at[me], dst_ref.at[0], local_sem)
    else:
        local = pltpu.make_async_copy(src_ref, dst_ref.at[me], local_sem)
    remote = []
    for r in range(1, N_DEV):
        px = 1 - mx if r & 4 else mx
        py = 1 - my if r & 2 else my
        pcc = 1 - mc if r & 1 else mc
        remote.append(pltpu.make_async_remote_copy(
            src_ref=src_ref.at[4 * px + 2 * py + pcc] if scatter else src_ref,
            dst_ref=dst_ref.at[r] if scatter else dst_ref.at[me],
            send_sem=send_sems.at[r - 1], recv_sem=recv_sems.at[r - 1],
            device_id=(px, py, pcc), device_id_type=MESH))
    return [local] + remote


def _start_all(copies):
    for cp in copies:
        cp.start()


def _wait_all(copies):
    for cp in copies[1:]:
        cp.wait()
    copies[0].wait()


COMM_SEMS = [pltpu.SemaphoreType.DMA((7,)), pltpu.SemaphoreType.DMA((7,)), pltpu.SemaphoreType.DMA(())]


def mod_columns(c_all, w_shard, b_cols):
    NCOL = w_shard.shape[1]

    def body(c_ref, w_ref, b_ref, o_ref):
        cv = c_ref[...]
        o_ref[...] = _nn((cv * _sigmoid(cv)).astype(BF16), w_ref[...].astype(BF16)) + b_ref[...]

    return _pc(body, name="mod_columns", out_shape=_sds((N_DEV, NCOL)),
               in_specs=[_const((N_DEV, D)), _const((D, NCOL)), _const((1, NCOL))],
               out_specs=_const((N_DEV, NCOL)), grid=(1,), compiler_params=_cp())(c_all, w_shard, b_cols)


def _norm_mod(xv, g, sc, sh):
    r = lax.rsqrt(_rowmean(xv * xv) + EPS)
    xh = xv * r
    return r, xh, xh * (g * (1.0 + sc)) + sh


def _log_sigmoid(v):
    e = jnp.exp(-jnp.abs(v))
    l1p = jnp.where(e < 1e-4, e * (1.0 - 0.5 * e), jnp.log(1.0 + e))
    return jnp.minimum(v, 0.0) - l1p


def fwd_in(x, n1g, sc1, sh1, wT, wr, qg_col, kg_col, bf_col, tri):
    S = x.shape[0]
    B = BLK
    NB = S // B

    def body(x_ref, g_ref, sc_ref, sh_ref, wT_ref, wr_ref, qg_ref, kg_ref, bf_ref, tri_ref,
             qTa_ref, kT_ref, kaug_ref, vT_ref, vaug_ref, zqk_ref, fgT_ref, alg_ref, u0_ref, stat_ref,
             carry, tbuf, fs, nq, nk):
        i = pl.program_id(0)

        @pl.when(i == 0)
        def _():
            carry[...] = jnp.zeros_like(carry)

        _, _, h = _norm_mod(x_ref[...], g_ref[...], sc_ref[...], sh_ref[...])
        hb = h.astype(BF16)
        zT = _nt(wT_ref[...], hb)
        zr = _nn(hb, wr_ref[...])
        zqk_ref[...] = zT[0:2 * A]
        fgT = zT[3 * A:3 * A + 8]
        fgT_ref[...] = fgT
        alg_ref[...] = zr
        u0_ref[...] = zr[:, 0:CW] * _sigmoid(zr[:, CW:2 * CW])

        logf = _log_sigmoid(fgT + bf_ref[...])
        a1, a2, a3 = _split3(logf)
        tr = tri_ref[...]
        F = _nn(a1, tr) + _nn(a2, tr) + _nn(a3, tr) + carry[...]
        carry[...] = carry[...] + _lanesum(logf)
        p1, p2, p3 = _split3(F * LOG2E)
        n1, n2, n3 = _split3(F * (-LOG2E))
        for k, v in enumerate((p1, p2, p3, n1, n2, n3)):
            fs[k] = v.astype(F32)

        rowi = lax.broadcasted_iota(jnp.int32, (8, B), 0)
        zeros_tail = jnp.zeros((128 - DH - 8, B), F32)
        ones_row = jnp.where(lax.broadcasted_iota(jnp.int32, (KR - DH, B), 0) == 0, 1.0, 0.0).astype(BF16)
        for hh in range(H):
            sl = slice(hh * DH, (hh + 1) * DH)
            q = zT[sl]
            k = zT[A + hh * DH:A + (hh + 1) * DH]
            v = zT[2 * A + hh * DH:2 * A + (hh + 1) * DH]
            qh = q * lax.rsqrt(jnp.mean(q * q, axis=0, keepdims=True) + EPS) * qg_ref[...] * (0.125 * LOG2E)
            kh = k * lax.rsqrt(jnp.mean(k * k, axis=0, keepdims=True) + EPS) * kg_ref[...]
            frow = [fs[kk, hh:hh + 1, :] for kk in range(6)]
            qx = jnp.where(rowi < 3, 1.0, jnp.where(rowi == 3, frow[0], jnp.where(
                rowi == 4, frow[1], jnp.where(rowi == 5, frow[2], 0.0))))
            kx = jnp.where(rowi == 0, frow[3], jnp.where(rowi == 1, frow[4], jnp.where(
                rowi == 2, frow[5], jnp.where(rowi < 6, 1.0, 0.0))))
            tbuf[0:DH, :] = qh
            tbuf[DH:DH + 8, :] = qx
            tbuf[DH + 8:128, :] = zeros_tail
            qTa_ref[0, hh * 128:(hh + 1) * 128, :] = tbuf[...].astype(BF16)
            tbuf[0:DH, :] = kh
            tbuf[DH:DH + 8, :] = kx
            kaug_ref[hh] = tbuf[...].T.astype(BF16)
            kT_ref[0, hh * KR:hh * KR + DH, :] = kh.astype(BF16)
            kT_ref[0, hh * KR + DH:(hh + 1) * KR, :] = ones_row
            tbuf[0:DH, :] = v
            tbuf[DH:DH + 8, :] = jnp.zeros((8, B), F32)
            vaug_ref[hh] = tbuf[...].T.astype(BF16)
            vT_ref[0, sl, :] = v.astype(BF16)
            nq[hh:hh + 1, :] = jnp.max(_colsum(qh * qh), axis=1, keepdims=True)
            nk[hh:hh + 1, :] = jnp.max(_colsum(kh * kh), axis=1, keepdims=True)

        lane = lax.broadcasted_iota(jnp.int32, (8, 128), 1)
        stat_ref[0] = jnp.where(lane == 0, jnp.max(F, axis=1, keepdims=True), jnp.where(
            lane == 1, jnp.min(F, axis=1, keepdims=True), jnp.where(
                lane == 2, nq[...], jnp.where(lane == 3, nk[...], 0.0))))

    row = lambda w: pl.BlockSpec((B, w), lambda i: (i, 0))
    tsp = lambda r: pl.BlockSpec((1, r, B), lambda i: (i, 0, 0))
    return _pc(
        body, name="fwd_in", grid=(NB,),
        in_specs=[row(D), _const((1, D)), _const((1, D)), _const((1, D)), _const((ZT_ROWS, D)),
                  _const((D, 2 * CW)), _const((DH, 1)), _const((DH, 1)), _const((8, 1)), _const((B, B))],
        out_specs=[tsp(H * 128), tsp(H * KR), pl.BlockSpec((H, B, 128), lambda i: (0, i, 0)), tsp(A),
                   pl.BlockSpec((H, B, 128), lambda i: (0, i, 0)),
                   pl.BlockSpec((2 * A, B), lambda i: (0, i)), pl.BlockSpec((8, B), lambda i: (0, i)),
                   row(2 * CW), row(CW), pl.BlockSpec((1, 8, 128), lambda i: (i, 0, 0))],
        out_shape=[_sds((NB, H * 128, B), BF16), _sds((NB, H * KR, B), BF16), _sds((H, S, 128), BF16),
                   _sds((NB, A, B), BF16), _sds((H, S, 128), BF16), _sds((2 * A, S)), _sds((8, S)),
                   _sds((S, 2 * CW)), _sds((S, CW)), _sds((NB, 8, 128))],
        scratch_shapes=[pltpu.VMEM((8, 1), F32), pltpu.VMEM((128, B), F32), pltpu.VMEM((6, 8, B), F32),
                        pltpu.VMEM((8, 1), F32), pltpu.VMEM((8, 1), F32)],
        compiler_params=_cp(("arbitrary",)),
    )(x, n1g, sc1, sh1, wT, wr, qg_col, kg_col, bf_col, tri)


def _first_key_block(top, fmin_ref, cut, h, i):
    return lax.while_loop(
        lambda j: jnp.logical_and(j > 0, top - fmin_ref[h, jnp.maximum(j - 1, 0)] >= cut),
        lambda j: j - 1, i)


def _causal_keep(B):
    return lax.broadcasted_iota(jnp.int32, (B, B), 0) <= lax.broadcasted_iota(jnp.int32, (B, B), 1)


def attn_fwd(fmax, fmin, thr, qTa, kaug, vT, late_w):
    NB, _, B = qTa.shape
    S = NB * B
    NG = len(late_w)

    def body(fmax_ref, fmin_ref, thr_ref, q_ref, k_ref, v_ref, *rest):
        w_refs = rest[0:NG]
        o_ref, lse_ref = rest[NG:NG + 2]
        gw_refs = rest[NG + 2:2 * NG + 2]
        s0, s1, m_ref, l_ref, acc_ref = rest[2 * NG + 2:2 * NG + 7]
        sems = rest[2 * NG + 7:]
        h = pl.program_id(0)
        i = pl.program_id(1)

        def gathers():
            return [_direct_copies(False, w_refs[q], gw_refs[q], *sems[3 * q:3 * q + 3]) for q in range(NG)]

        @pl.when(jnp.logical_and(h == 0, i == 0))
        def _():
            for copies in gathers():
                _start_all(copies)

        jlo = _first_key_block(fmax_ref[h, i], fmin_ref, thr_ref[h], h, i)
        n = i - jlo

        def scores(j, s_ref):
            s_ref[...] = _nn(k_ref[0, pl.ds(pl.multiple_of(j * B, B), B), :], q_ref[0])

        def softmax_step(s_ref, j, masked):
            s = s_ref[...]
            if masked:
                s = jnp.where(_causal_keep(B), s, -jnp.inf)
            m = m_ref[...]
            mn = jnp.maximum(m, jnp.max(s, axis=0, keepdims=True))
            a = jnp.exp2(m - mn)
            p = jnp.exp2(s - mn)
            m_ref[...] = mn
            l_ref[...] = a * l_ref[...] + _colsum(p)
            acc_ref[...] = a * acc_ref[...] + _nn(v_ref[j], p.astype(BF16))

        m_ref[...] = jnp.full((1, B), -jnp.inf, F32)
        l_ref[...] = jnp.zeros((1, B), F32)
        acc_ref[...] = jnp.zeros((DH, B), F32)
        scores(jlo, s0)

        def pair(t, carry):
            j = jlo + 2 * t
            scores(j + 1, s1)
            softmax_step(s0, j, False)
            scores(j + 2, s0)
            softmax_step(s1, j + 1, False)
            return carry

        lax.fori_loop(0, n // 2, pair, 0)

        @pl.when(n % 2 == 1)
        def _():
            scores(i, s1)
            softmax_step(s0, i - 1, False)
            softmax_step(s1, i, True)

        @pl.when(n % 2 == 0)
        def _():
            softmax_step(s0, i, True)

        l = l_ref[...]
        o_ref[0] = acc_ref[...] / l
        lse_ref[0, 0] = m_ref[...] + jnp.log2(l)

        @pl.when(jnp.logical_and(h == H - 1, i == NB - 1))
        def _():
            for copies in gathers():
                _wait_all(copies)

    return _pc(
        body, name="attn_fwd", grid=(H, NB),
        in_specs=[SMEM, SMEM, SMEM, pl.BlockSpec((1, 128, B), lambda h, i: (i, h, 0)),
                  pl.BlockSpec((1, S, 128), lambda h, i: (h, 0, 0)),
                  pl.BlockSpec((NB, DH, B), lambda h, i: (0, h, 0))] + [ANY] * NG,
        out_specs=[pl.BlockSpec((1, DH, B), lambda h, i: (i, h, 0)),
                   pl.BlockSpec((1, 1, 1, B), lambda h, i: (h, i, 0, 0))] + [ANY] * NG,
        out_shape=[_sds((NB, A, B)), _sds((H, NB, 1, B))]
        + [_sds((N_DEV,) + w.shape, w.dtype) for w in late_w],
        scratch_shapes=[pltpu.VMEM((B, B), F32), pltpu.VMEM((B, B), F32), pltpu.VMEM((1, B), F32),
                        pltpu.VMEM((1, B), F32), pltpu.VMEM((DH, B), F32)] + COMM_SEMS * NG,
        compiler_params=_cp(("arbitrary", "arbitrary")),
    )(fmax, fmin, thr, qTa, kaug, vT, *late_w)


def attn_bwd(fmax, fmin, thr, qTa, kaug, kT, vaug, doTa, lse, delta, early_g):
    NG = len(early_g)
    NB, _, B = qTa.shape
    QR = 80
    CH = min(256, B)

    def body(fmax_ref, fmin_ref, thr_ref, q_ref, ka_ref, kt_ref, va_ref, do_ref, lse_ref, dl_ref, *rest):
        g_refs = rest[0:NG]
        dq_ref, dk_ref, dv_ref, dfk_ref = rest[NG:NG + 4]
        recv_refs = rest[NG + 4:2 * NG + 4]
        s0, d0, s1, d1, p0, e0, p1, e1, dk_acc, dv_acc = rest[2 * NG + 4:2 * NG + 14]
        sems = rest[2 * NG + 14:]
        h = pl.program_id(0)
        j = pl.program_id(1)

        def exchanges():
            return [_direct_copies(True, g_refs[q], recv_refs[q], *sems[3 * q:3 * q + 3]) for q in range(NG)]

        @pl.when(jnp.logical_and(h == 0, j == 0))
        def _():
            for copies in exchanges():
                _start_all(copies)

        bottom = fmin_ref[h, j]
        cut = thr_ref[h]
        ihi = lax.while_loop(
            lambda i: jnp.logical_and(i < NB - 1, fmax_ref[h, jnp.minimum(i + 1, NB - 1)] - bottom >= cut),
            lambda i: i + 1, j)

        @pl.when(j == 0)
        def _():
            dq_ref[...] = jnp.zeros_like(dq_ref)

        dk_acc[...] = jnp.zeros_like(dk_acc)
        dv_acc[...] = jnp.zeros_like(dv_acc)

        n = ihi - j + 1

        def blk(k):
            return jnp.minimum(j + k, ihi)

        def products(k, s_ref, d_ref):
            i = blk(k)
            s_ref[...] = _nn(ka_ref[0], q_ref[i])
            d_ref[...] = _nn(va_ref[0], do_ref[i])

        def elementwise(k, s_ref, d_ref, p_ref, ds_ref, masked):
            i = blk(k)
            p = jnp.exp2(s_ref[...] - lse_ref[0, i])
            if masked:
                p = jnp.where(_causal_keep(B), p, 0.0)
            p_ref[...] = p.astype(BF16)
            ds_ref[...] = (p * (d_ref[...] - dl_ref[0, i])).astype(BF16)

        def grads(k, p_ref, ds_ref):
            i = j + k
            dsb = ds_ref[...]
            dv_acc[...] = dv_acc[...] + _nt(do_ref[i][0:DH], p_ref[...])
            dk_acc[...] = dk_acc[...] + _nt(q_ref[i][0:QR], dsb)
            dq_ref[i] = dq_ref[i] + _nn(kt_ref[0], dsb)

        def trip(kp, s_p, d_p, ke, s_e, d_e, p_e, e_e, kg, p_g, e_g):
            ip, ie, ig = blk(kp), blk(ke), j + kg
            lse_e = lse_ref[0, ie]
            dl_e = dl_ref[0, ie]
            dq_part = jnp.zeros((KR, B), F32)
            for c in range(B // CH):
                rows = slice(c * CH, (c + 1) * CH)
                s_p[rows, :] = _nn(ka_ref[0, rows, :], q_ref[ip])
                d_p[rows, :] = _nn(va_ref[0, rows, :], do_ref[ip])
                p = jnp.exp2(s_e[rows, :] - lse_e)
                p_e[rows, :] = p.astype(BF16)
                e_e[rows, :] = (p * (d_e[rows, :] - dl_e)).astype(BF16)
                dsb = e_g[rows, :]
                dv_acc[:, rows] = dv_acc[:, rows] + _nt(do_ref[ig][0:DH], p_g[rows, :])
                dk_acc[:, rows] = dk_acc[:, rows] + _nt(q_ref[ig][0:QR], dsb)
                dq_part = dq_part + _nn(kt_ref[0, :, rows], dsb)
            dq_ref[ig] = dq_ref[ig] + dq_part

        products(0, s0, d0)
        products(1, s1, d1)
        elementwise(0, s0, d0, p0, e0, True)

        def pair(t, carry):
            k = 2 * t
            trip(k + 2, s0, d0, k + 1, s1, d1, p1, e1, k, p0, e0)
            trip(k + 3, s1, d1, k + 2, s0, d0, p0, e0, k + 1, p1, e1)
            return carry

        n_pairs = n // 2
        lax.fori_loop(0, n_pairs - 1, pair, 0)
        k_last = 2 * (n_pairs - 1)

        @pl.when(jnp.logical_and(n_pairs >= 1, n % 2 == 0))
        def _():
            elementwise(k_last + 1, s1, d1, p1, e1, False)
            grads(k_last, p0, e0)
            grads(k_last + 1, p1, e1)

        @pl.when(jnp.logical_and(n_pairs >= 1, n % 2 == 1))
        def _():
            trip(k_last + 2, s0, d0, k_last + 1, s1, d1, p1, e1, k_last, p0, e0)
            elementwise(k_last + 2, s0, d0, p0, e0, False)
            grads(k_last + 1, p1, e1)
            grads(k_last + 2, p0, e0)

        @pl.when(n == 1)
        def _():
            grads(0, p0, e0)

        dk_ref[0] = dk_acc[0:DH, :]
        dfk_ref[0, 0] = dk_acc[DH:DH + 8, :]
        dv_ref[0] = dv_acc[...]

        @pl.when(jnp.logical_and(h == H - 1, j == NB - 1))
        def _():
            for copies in exchanges():
                _wait_all(copies)

    per_kv = lambda r: pl.BlockSpec((1, r, B), lambda h, j: (j, h, 0))
    head_all = lambda r: pl.BlockSpec((NB, r, B), lambda h, j: (0, h, 0))
    aug = pl.BlockSpec((1, B, 128), lambda h, j: (h, j, 0))
    stat = pl.BlockSpec((1, NB, 1, B), lambda h, j: (h, 0, 0, 0))
    return _pc(
        body, name="attn_bwd", grid=(H, NB),
        in_specs=[SMEM, SMEM, SMEM, head_all(128), aug, per_kv(KR), aug, head_all(128), stat, stat]
        + [ANY] * NG,
        out_specs=[head_all(KR), per_kv(DH), per_kv(DH),
                   pl.BlockSpec((1, 1, 8, B), lambda h, j: (h, j, 0, 0))] + [ANY] * NG,
        out_shape=[_sds((NB, H * KR, B)), _sds((NB, A, B)), _sds((NB, A, B)), _sds((H, NB, 8, B))]
        + [_sds(g.shape, g.dtype) for g in early_g],
        scratch_shapes=[pltpu.VMEM((B, B), F32)] * 4 + [pltpu.VMEM((B, B), BF16)] * 4
        + [pltpu.VMEM((QR, B), F32), pltpu.VMEM((DH, B), F32)] + COMM_SEMS * NG,
        compiler_params=_cp(("arbitrary", "arbitrary")),
    )(fmax, fmin, thr, qTa, kaug, kT, vaug, doTa, lse, delta, *early_g)


def _conv_tail(u1, lng, lnb, beta_c):
    mu = _rowmean(u1)
    d = u1 - mu
    rstd = lax.rsqrt(_rowmean(d * d) + EPS)
    xhat = d * rstd
    u2 = xhat * lng + lnb
    sg = _sigmoid(u2)
    u3 = u2 * sg
    rc = lax.rsqrt(_rowmean(u3 * u3) + EPS)
    n3 = u3 * rc
    return rstd, xhat, u2, sg, rc, n3, n3 * beta_c


def _attn_tail(oT, beta_a_col):
    ra = lax.rsqrt(jnp.mean(oT * oT, axis=0, keepdims=True) + EPS)
    ohat = oT * ra
    return ra, ohat, ohat * beta_a_col


CONV_ROWS = 32
CORR_ROWS = 16


def _fill_shifted(buf, sh):
    rows = sh.shape[1]
    for ph in range(8):
        sh[ph] = buf[pl.ds(ph, rows), :]


def _tap_rows(sh, r0, o, rows):
    return sh[o % 8, pl.ds(pl.multiple_of(r0 + 8 * (o // 8), 8), rows), :]


def _depthwise(sh, w_ref, offs, bias, out_ref, B):
    def chunk(ci, carry):
        r0 = pl.multiple_of(ci * CONV_ROWS, CONV_ROWS)
        acc = jnp.broadcast_to(bias, (CONV_ROWS, CW))
        for k, o in enumerate(offs):
            acc = acc + w_ref[k:k + 1, :] * _tap_rows(sh, r0, o, CONV_ROWS)
        out_ref[pl.ds(r0, CONV_ROWS), :] = acc
        return carry

    lax.fori_loop(0, B // CONV_ROWS, chunk, 0)


def _tap_gradients(sh, d_ref, offs, dw_ref, B):
    for g0 in range(0, len(offs), 8):
        ks = list(range(g0, min(g0 + 8, len(offs))))

        def chunk(ci, accs, ks=ks):
            r0 = pl.multiple_of(ci * CORR_ROWS, CORR_ROWS)
            d = d_ref[pl.ds(r0, CORR_ROWS), :]
            out = []
            for a, k in zip(accs, ks):
                pr = d * _tap_rows(sh, r0, offs[k], CORR_ROWS)
                out.append(a + pr[0:8] + pr[8:16])
            return tuple(out)

        accs = lax.fori_loop(0, B // CORR_ROWS, chunk, tuple(jnp.zeros((8, CW), F32) for _ in ks))
        for a, k in zip(accs, ks):
            dw_ref[k:k + 1, :] = dw_ref[k:k + 1, :] + _colsum(a)


def conv_merge_out(u0, x, oT, conv_w, conv_b, lng, lnb, beta_c, beta_a_col, wo, g1):
    S = x.shape[0]
    B = BLK
    NB = S // B

    def body(uc_ref, up_ref, x_ref, oT_ref, w_ref, cb_ref, lng_ref, lnb_ref, bc_ref, ba_ref, wo_ref,
             g1_ref, x1_ref, o_ref, u1_ref, ubuf, sh):
        i = pl.program_id(0)
        ubuf[0:HALO, :] = jnp.where(i > 0, up_ref[B - HALO:B, :], 0.0)
        ubuf[HALO:HALO + B, :] = uc_ref[...]
        ubuf[HALO + B:HALO + B + 8, :] = jnp.zeros((8, CW), F32)
        _fill_shifted(ubuf, sh)
        _depthwise(sh, w_ref, [HALO - (KC - 1) + k for k in range(KC)], cb_ref[...], u1_ref, B)
        mc = _conv_tail(u1_ref[...], lng_ref[...], lnb_ref[...], bc_ref[...])[-1]
        maT = _attn_tail(oT_ref[0], ba_ref[...])[-1]
        ma = maT.T
        o = _nn(ma.astype(BF16), wo_ref[0:A, :]) + _nn(mc.astype(BF16), wo_ref[A:D, :])
        o_ref[...] = o
        x1_ref[...] = x_ref[...] + g1_ref[...] * o

    row = lambda w: pl.BlockSpec((B, w), lambda i: (i, 0))
    return _pc(
        body, name="conv_merge_out", grid=(NB,),
        in_specs=[row(CW), pl.BlockSpec((B, CW), lambda i: (jnp.maximum(i - 1, 0), 0)), row(D),
                  pl.BlockSpec((1, A, B), lambda i: (i, 0, 0)), _const((32, CW)), _const((1, CW)),
                  _const((1, CW)), _const((1, CW)), _const((1, CW)), _const((A, 1)), _const((D, D)),
                  _const((1, D))],
        out_specs=[row(D), row(D), row(CW)],
        out_shape=[_sds((S, D)), _sds((S, D)), _sds((S, CW))],
        scratch_shapes=[pltpu.VMEM((B + HALO + 8, CW), F32), pltpu.VMEM((8, B + HALO, CW), F32)],
        compiler_params=_cp(("arbitrary",)),
    )(u0, u0, x, oT, conv_w, conv_b, lng, lnb, beta_c, beta_a_col, wo, g1)


def mlp_fwd_loss(x1, tgt, n2g, sc2, sh2, g2, w1s, w2):
    S = x1.shape[0]
    B = BLK
    NB = S // B
    NS, _, FS = w1s.shape

    def body(x1_ref, t_ref, g_ref, sc_ref, sh_ref, g2_ref, w1_hbm, w2_hbm, dy_ref, loss_ref, dg2_ref, ra_ref,
             w1_v, w2_v, sem):
        i = pl.program_id(0)

        @pl.when(i == 0)
        def _():
            for src, dst in ((w1_hbm, w1_v), (w2_hbm, w2_v)):
                cp = pltpu.make_async_copy(src, dst, sem)
                cp.start()
                cp.wait()
            loss_ref[...] = jnp.zeros_like(loss_ref)
            dg2_ref[...] = jnp.zeros_like(dg2_ref)

        x1 = x1_ref[...]
        hs = _norm_mod(x1, g_ref[...], sc_ref[...], sh_ref[...])[2].astype(BF16)
        y2 = jnp.zeros((B, D), F32)
        for q in range(NS):
            a = jnp.maximum(_nn(hs, w1_v[q]), 0.0)
            ra_ref[:, q * FS:(q + 1) * FS] = a.astype(BF16)
            y2 = y2 + _nn((a * a).astype(BF16), w2_v[q * FS:(q + 1) * FS, :])
        e = x1 + g2_ref[...] * y2 - t_ref[...]
        dy = e * (1.0 / D)
        dy_ref[...] = dy
        loss_ref[...] = loss_ref[...] + 0.5 * _colsum(_rowmean(e * e))
        dg2_ref[...] = dg2_ref[...] + _colsum(dy * y2)

    row = pl.BlockSpec((B, D), lambda i: (i, 0))
    return _pc(
        body, name="mlp_fwd_loss", grid=(NB,),
        in_specs=[row, row, _const((1, D)), _const((1, D)), _const((1, D)), _const((1, D)), ANY, ANY],
        out_specs=[row, _const((1, 1)), _const((1, D)), pl.BlockSpec((B, FF), lambda i: (i, 0))],
        out_shape=[_sds((S, D)), _sds((1, 1)), _sds((1, D)), _sds((S, FF), BF16)],
        scratch_shapes=[pltpu.VMEM(w1s.shape, BF16), pltpu.VMEM(w2.shape, BF16), pltpu.SemaphoreType.DMA(())],
        compiler_params=_cp(("arbitrary",)),
    )(x1, tgt, n2g, sc2, sh2, g2, w1s, w2)


def mlp_bwd(x1, dy, ra, n2g, sc2, sh2, g2, w1s, w2):
    S = x1.shape[0]
    B = BLK
    NB = S // B
    NC = FF // FCH
    NS, _, FS = w1s.shape
    SPC = NS // NC

    def body(x1_ref, dy_ref, ra_ref, g_ref, sc_ref, sh_ref, g2_ref, w1_ref, w2_ref, dw1_ref, dw2_ref, dh_ref):
        i = pl.program_id(1)

        @pl.when(i == 0)
        def _():
            dw1_ref[...] = jnp.zeros_like(dw1_ref)
            dw2_ref[...] = jnp.zeros_like(dw2_ref)

        h2 = _norm_mod(x1_ref[...], g_ref[...], sc_ref[...], sh_ref[...])[2]
        h2t = h2.T.astype(BF16)
        ra = ra_ref[...].astype(F32)
        dyb = (dy_ref[...] * g2_ref[...]).astype(BF16)
        db = _nt(dyb, w2_ref[...])
        dab = (db * (2.0 * ra)).astype(BF16)
        dh = jnp.zeros((B, D), F32)
        for q in range(SPC):
            part = dab[:, q * FS:(q + 1) * FS]
            dh = dh + _nt(part, w1_ref[q])
            dw1_ref[q] = dw1_ref[q] + _nn(h2t, part)
        dh_ref[0] = dh
        dw2_ref[...] = dw2_ref[...] + _nn((ra * ra).T.astype(BF16), dyb)

    row = pl.BlockSpec((B, D), lambda c, i: (i, 0))
    vec = pl.BlockSpec((1, D), lambda c, i: (0, 0))
    wsh = pl.BlockSpec((SPC, D, FS), lambda c, i: (c, 0, 0))
    w2b = pl.BlockSpec((FCH, D), lambda c, i: (c, 0))
    return _pc(
        body, name="mlp_bwd", grid=(NC, NB),
        in_specs=[row, row, pl.BlockSpec((B, FCH), lambda c, i: (i, c)), vec, vec, vec, vec, wsh, w2b],
        out_specs=[wsh, w2b, pl.BlockSpec((1, B, D), lambda c, i: (c, i, 0))],
        out_shape=[_sds(w1s.shape), _sds((FF, D)), _sds((NC, S, D))],
        compiler_params=_cp(("arbitrary", "arbitrary")),
    )(x1, dy, ra, n2g, sc2, sh2, g2, w1s, w2)


def merge_bwd(dh2p, x1, dy, o, oT, u1, n2g, sc2, g1, lng, lnb, beta_c, beta_a_col, wo):
    S = x1.shape[0]
    B = BLK
    NB = S // B
    NC = dh2p.shape[0]

    def body(dh_ref, x1_ref, dy_ref, o_ref, oT_ref, u1_ref, g_ref, sc_ref, g1_ref, lng_ref, lnb_ref,
             bc_ref, ba_ref, wo_ref,
             dx1_ref, doTa_ref, dl_ref, du1_ref, dwo_ref, v1_ref, v2_ref, dba_ref):
        i = pl.program_id(0)

        @pl.when(i == 0)
        def _():
            dwo_ref[...] = jnp.zeros_like(dwo_ref)
            v1_ref[...] = jnp.zeros_like(v1_ref)
            v2_ref[...] = jnp.zeros_like(v2_ref)
            dba_ref[...] = jnp.zeros_like(dba_ref)

        dh2 = dh_ref[0]
        for cc in range(1, NC):
            dh2 = dh2 + dh_ref[cc]
        x1 = x1_ref[...]
        g = g_ref[...]
        sc = sc_ref[...]
        r2 = lax.rsqrt(_rowmean(x1 * x1) + EPS)
        xh = x1 * r2
        dhx = dh2 * xh
        v1_ref[0:1, :] = v1_ref[0:1, :] + _colsum(dh2)
        v1_ref[1:2, :] = v1_ref[1:2, :] + _colsum(dhx) * g
        v1_ref[2:3, :] = v1_ref[2:3, :] + _colsum(dhx) * (1.0 + sc)
        dxh = dh2 * (g * (1.0 + sc))
        dx1 = dy_ref[...] + r2 * (dxh - xh * _rowmean(dxh * xh))
        dx1_ref[...] = dx1
        v1_ref[3:4, :] = v1_ref[3:4, :] + _colsum(dx1 * o_ref[...])
        dob = (dx1 * g1_ref[...]).astype(BF16)

        lng = lng_ref[...]
        bc = bc_ref[...]
        rstd, xhat, u2, sg, rc, n3, mc = _conv_tail(u1_ref[...], lng, lnb_ref[...], bc)
        ba = ba_ref[...]
        oT = oT_ref[0]
        ra, ohat, maT = _attn_tail(oT, ba)
        dwo_ref[0:A, :] = dwo_ref[0:A, :] + _nn(maT.astype(BF16), dob)
        dwo_ref[A:D, :] = dwo_ref[A:D, :] + _nn(mc.T.astype(BF16), dob)
        dmaT = _nt(wo_ref[0:A, :], dob)
        dmc = _nt(dob, wo_ref[A:D, :])

        dba_ref[...] = dba_ref[...] + _lanesum(dmaT * ohat)
        dohat = dmaT * ba
        doT = ra * (dohat - ohat * jnp.mean(dohat * ohat, axis=0, keepdims=True))
        prod = doT * oT
        zpad = jnp.zeros((128 - DH, B), BF16)
        for hh in range(H):
            sl = slice(hh * DH, (hh + 1) * DH)
            dl_ref[hh, 0] = _colsum(prod[sl])
            doTa_ref[0, hh * 128:hh * 128 + DH, :] = doT[sl].astype(BF16)
            doTa_ref[0, hh * 128 + DH:(hh + 1) * 128, :] = zpad

        v2_ref[0:1, :] = v2_ref[0:1, :] + _colsum(dmc * n3)
        dn3 = dmc * bc
        du3 = rc * (dn3 - n3 * _rowmean(dn3 * n3))
        du2 = du3 * (sg * (1.0 + u2 * (1.0 - sg)))
        v2_ref[1:2, :] = v2_ref[1:2, :] + _colsum(du2 * xhat)
        v2_ref[2:3, :] = v2_ref[2:3, :] + _colsum(du2)
        dxhat = du2 * lng
        du1_ref[...] = rstd * (dxhat - _rowmean(dxhat) - xhat * _rowmean(dxhat * xhat))

    row = lambda w: pl.BlockSpec((B, w), lambda i: (i, 0))
    return _pc(
        body, name="merge_bwd", grid=(NB,),
        in_specs=[pl.BlockSpec((NC, B, D), lambda i: (0, i, 0)), row(D), row(D), row(D),
                  pl.BlockSpec((1, A, B), lambda i: (i, 0, 0)), row(CW), _const((1, D)), _const((1, D)),
                  _const((1, D)), _const((1, CW)), _const((1, CW)), _const((1, CW)), _const((A, 1)),
                  _const((D, D))],
        out_specs=[row(D), pl.BlockSpec((1, H * 128, B), lambda i: (i, 0, 0)),
                   pl.BlockSpec((H, 1, 1, B), lambda i: (0, i, 0, 0)), row(CW), _const((D, D)),
                   _const((8, D)), _const((8, CW)), _const((A, 1))],
        out_shape=[_sds((S, D)), _sds((NB, H * 128, B), BF16), _sds((H, NB, 1, B)), _sds((S, CW)),
                   _sds((D, D)), _sds((8, D)), _sds((8, CW)), _sds((A, 1))],
        compiler_params=_cp(("arbitrary",)),
    )(dh2p, x1, dy, o, oT, u1, n2g, sc2, g1, lng, lnb, beta_c, beta_a_col, wo)


def head_bwd(du1, u0, alg, zqk, fgT, dqT, dkT, dvT, dfk, conv_w, qg_col, kg_col, bf_col, tri_lo):
    S = u0.shape[0]
    B = BLK
    NB = S // B

    def body(dc_ref, dn_ref, uc_ref, up_ref, alg_ref, zqk_ref, fg_ref, dq_ref, dk_ref, dv_ref, dfk_ref,
             w_ref, qg_ref, kg_ref, bf_ref, tri_ref,
             dzr_ref, dzT_ref, dcw_ref, vc_ref, dqg_ref, dkg_ref, dbf_ref,
             buf, sh, du0_ref, carry, fbuf):
        pid = pl.program_id(0)
        ri = NB - 1 - pid

        @pl.when(pid == 0)
        def _():
            carry[...] = jnp.zeros_like(carry)
            dcw_ref[...] = jnp.zeros_like(dcw_ref)
            vc_ref[...] = jnp.zeros_like(vc_ref)
            dqg_ref[...] = jnp.zeros_like(dqg_ref)
            dkg_ref[...] = jnp.zeros_like(dkg_ref)
            dbf_ref[...] = jnp.zeros_like(dbf_ref)

        zero8 = jnp.zeros((8, CW), F32)
        buf[0:B, :] = dc_ref[...]
        buf[B:B + HALO, :] = jnp.where(ri < NB - 1, dn_ref[0:HALO, :], 0.0)
        buf[B + HALO:B + HALO + 8, :] = zero8
        _fill_shifted(buf, sh)
        _depthwise(sh, w_ref, [KC - 1 - k for k in range(KC)], jnp.zeros((1, CW), F32), du0_ref, B)
        buf[0:HALO, :] = jnp.where(ri > 0, up_ref[B - HALO:B, :], 0.0)
        buf[HALO:HALO + B, :] = uc_ref[...]
        buf[HALO + B:HALO + B + 8, :] = zero8
        _fill_shifted(buf, sh)
        _tap_gradients(sh, dc_ref, [HALO - (KC - 1) + k for k in range(KC)], dcw_ref, B)
        vc_ref[0:1, :] = vc_ref[0:1, :] + _colsum(dc_ref[...])

        du0 = du0_ref[...]
        al = alg_ref[:, 0:CW]
        sg = _sigmoid(alg_ref[:, CW:2 * CW])
        dzr_ref[:, 0:CW] = (du0 * sg).astype(BF16)
        dzr_ref[:, CW:2 * CW] = (du0 * al * sg * (1.0 - sg)).astype(BF16)

        dqg = jnp.zeros((DH, B), F32)
        dkg = jnp.zeros((DH, B), F32)
        qg = qg_ref[...]
        kg = kg_ref[...]
        for hh in range(H):
            sl = slice(hh * DH, (hh + 1) * DH)
            q = zqk_ref[sl, :]
            rq = lax.rsqrt(jnp.mean(q * q, axis=0, keepdims=True) + EPS)
            qn = q * rq
            dqh = dq_ref[0, hh * KR:hh * KR + DH, :] * 0.125
            dqg = dqg + dqh * qn
            dqn = dqh * qg
            dzT_ref[0, sl, :] = (rq * (dqn - qn * jnp.mean(dqn * qn, axis=0, keepdims=True))).astype(BF16)
            k = zqk_ref[A + hh * DH:A + (hh + 1) * DH, :]
            rk = lax.rsqrt(jnp.mean(k * k, axis=0, keepdims=True) + EPS)
            kn = k * rk
            dkh = dk_ref[0, sl, :] * (1.0 / LOG2E)
            dkg = dkg + dkh * kn
            dkn = dkh * kg
            dzT_ref[0, A + hh * DH:A + (hh + 1) * DH, :] = (
                rk * (dkn - kn * jnp.mean(dkn * kn, axis=0, keepdims=True))).astype(BF16)
            fbuf[hh:hh + 1, :] = dq_ref[0, hh * KR + DH:hh * KR + DH + 1, :] - dfk_ref[hh, 0, 0:1, :]
        dqg_ref[...] = dqg_ref[...] + _lanesum(dqg)
        dkg_ref[...] = dkg_ref[...] + _lanesum(dkg)
        dzT_ref[0, 2 * A:3 * A, :] = dv_ref[0].astype(BF16)

        dF = fbuf[...]
        a1, a2, a3 = _split3(dF)
        tr = tri_ref[...]
        dlogf = _nn(a1, tr) + _nn(a2, tr) + _nn(a3, tr) + carry[...]
        carry[...] = carry[...] + _lanesum(dF)
        dfg = dlogf * _sigmoid(-(fg_ref[...] + bf_ref[...]))
        dbf_ref[...] = dbf_ref[...] + _lanesum(dfg)
        dzT_ref[0, 3 * A:ZT_ROWS, :] = jnp.concatenate([dfg, jnp.zeros((8, B), F32)], axis=0).astype(BF16)

    rrow = lambda w: pl.BlockSpec((B, w), lambda p: (NB - 1 - p, 0))
    rts = lambda r: pl.BlockSpec((1, r, B), lambda p: (NB - 1 - p, 0, 0))
    return _pc(
        body, name="head_bwd", grid=(NB,),
        in_specs=[rrow(CW), pl.BlockSpec((B, CW), lambda p: (jnp.minimum(NB - p, NB - 1), 0)),
                  rrow(CW), pl.BlockSpec((B, CW), lambda p: (jnp.maximum(NB - 2 - p, 0), 0)),
                  rrow(2 * CW), pl.BlockSpec((2 * A, B), lambda p: (0, NB - 1 - p)),
                  pl.BlockSpec((8, B), lambda p: (0, NB - 1 - p)), rts(H * KR), rts(A), rts(A),
                  pl.BlockSpec((H, 1, 8, B), lambda p: (0, NB - 1 - p, 0, 0)),
                  _const((32, CW)), _const((DH, 1)), _const((DH, 1)), _const((8, 1)), _const((B, B))],
        out_specs=[rrow(2 * CW), rts(ZT_ROWS), _const((32, CW)), _const((8, CW)), _const((DH, 1)),
                   _const((DH, 1)), _const((8, 1))],
        out_shape=[_sds((S, 2 * CW), BF16), _sds((NB, ZT_ROWS, B), BF16), _sds((32, CW)), _sds((8, CW)),
                   _sds((DH, 1)), _sds((DH, 1)), _sds((8, 1))],
        scratch_shapes=[pltpu.VMEM((B + HALO + 8, CW), F32), pltpu.VMEM((8, B + HALO, CW), F32),
                        pltpu.VMEM((B, CW), F32), pltpu.VMEM((8, 1), F32), pltpu.VMEM((8, B), F32)],
        compiler_params=_cp(("arbitrary",)),
    )(du1, du1, u0, u0, alg, zqk, fgT, dqT, dkT, dvT, dfk, conv_w, qg_col, kg_col, bf_col, tri_lo)


def in_bwd_w(x, dzr, dzT, n1g, sc1, sh1):
    S = x.shape[0]
    B = BLK
    NB = S // B

    def body(x_ref, dzr_ref, dzT_ref, g_ref, sc_ref, sh_ref, dwT_hbm, dwr_hbm, dwT_acc, dwr_acc, sem):
        i = pl.program_id(0)

        @pl.when(i == 0)
        def _():
            dwT_acc[...] = jnp.zeros_like(dwT_acc)
            dwr_acc[...] = jnp.zeros_like(dwr_acc)

        h1 = _norm_mod(x_ref[...], g_ref[...], sc_ref[...], sh_ref[...])[2]
        dwT_acc[...] = dwT_acc[...] + _nn(dzT_ref[0], h1.astype(BF16))
        dwr_acc[...] = dwr_acc[...] + _nn(h1.T.astype(BF16), dzr_ref[...])

        @pl.when(i == NB - 1)
        def _():
            for src, dst in ((dwT_acc, dwT_hbm), (dwr_acc, dwr_hbm)):
                cp = pltpu.make_async_copy(src, dst, sem)
                cp.start()
                cp.wait()

    row = lambda w: pl.BlockSpec((B, w), lambda i: (i, 0))
    return _pc(
        body, name="in_bwd_w", grid=(NB,),
        in_specs=[row(D), row(2 * CW), pl.BlockSpec((1, ZT_ROWS, B), lambda i: (i, 0, 0)),
                  _const((1, D)), _const((1, D)), _const((1, D))],
        out_specs=[ANY, ANY],
        out_shape=[_sds((ZT_ROWS, D)), _sds((D, 2 * CW))],
        scratch_shapes=[pltpu.VMEM((ZT_ROWS, D), F32), pltpu.VMEM((D, 2 * CW), F32),
                        pltpu.SemaphoreType.DMA(())],
        compiler_params=_cp(("arbitrary",)),
    )(x, dzr, dzT, n1g, sc1, sh1)


def in_bwd_x(x, dx1, dzr, dzT, n1g, sc1, sh1, w_qkvf, wr, late_g):
    S = x.shape[0]
    B = BLK
    NB = S // B

    def body(x_ref, dx1_ref, dzr_ref, dzT_ref, g_ref, sc_ref, sh_ref, w_hbm, wr_hbm, lg_ref,
             gx_ref, v_ref, recv_ref, w_v, wr_v, sem, send_sems, recv_sems, local_sem):
        i = pl.program_id(0)

        @pl.when(i == 0)
        def _():
            _start_all(_direct_copies(True, lg_ref, recv_ref, send_sems, recv_sems, local_sem))
            for src, dst in ((w_hbm, w_v), (wr_hbm, wr_v)):
                cp = pltpu.make_async_copy(src, dst, sem)
                cp.start()
                cp.wait()
            v_ref[...] = jnp.zeros_like(v_ref)

        g = g_ref[...]
        sc = sc_ref[...]
        r1, xh, _ = _norm_mod(x_ref[...], g, sc, sh_ref[...])
        dh1 = _nt(dzr_ref[...], wr_v[...]) + _nn(w_v[...], dzT_ref[0]).T
        dhx = dh1 * xh
        v_ref[0:1, :] = v_ref[0:1, :] + _colsum(dh1)
        v_ref[1:2, :] = v_ref[1:2, :] + _colsum(dhx) * g
        v_ref[2:3, :] = v_ref[2:3, :] + _colsum(dhx) * (1.0 + sc)
        dxh = dh1 * (g * (1.0 + sc))
        gx_ref[...] = dx1_ref[...] + r1 * (dxh - xh * _rowmean(dxh * xh))

        @pl.when(i == NB - 1)
        def _():
            _wait_all(_direct_copies(True, lg_ref, recv_ref, send_sems, recv_sems, local_sem))

    row = lambda w: pl.BlockSpec((B, w), lambda i: (i, 0))
    return _pc(
        body, name="in_bwd_x", grid=(NB,),
        in_specs=[row(D), row(D), row(2 * CW), pl.BlockSpec((1, ZT_ROWS, B), lambda i: (i, 0, 0)),
                  _const((1, D)), _const((1, D)), _const((1, D)), ANY, ANY, ANY],
        out_specs=[row(D), _const((8, D)), ANY],
        out_shape=[_sds((S, D)), _sds((8, D)), _sds(late_g.shape, late_g.dtype)],
        scratch_shapes=[pltpu.VMEM((D, ZT_ROWS), BF16), pltpu.VMEM((D, 2 * CW), BF16),
                        pltpu.SemaphoreType.DMA(())] + COMM_SEMS,
        compiler_params=_cp(("arbitrary",)),
    )(x, dx1, dzr, dzT, n1g, sc1, sh1, w_qkvf, wr, late_g)


def _adam_math(g, w, m, v):
    m = ADAM_B1 * m + (1.0 - ADAM_B1) * g
    v = ADAM_B2 * v + (1.0 - ADAM_B2) * (g * g)
    m_hat = m / (1.0 - ADAM_B1 ** ADAM_STEP)
    v_hat = v / (1.0 - ADAM_B2 ** ADAM_STEP)
    delta = -ADAM_LR * (m_hat / (jnp.sqrt(v_hat) + ADAM_EPS) + ADAM_WD * w)
    return delta, m, v


def _row_tile(R):
    for t in (1024, 512, 256, 128, 64, 32, 16, 8):
        if R % t == 0:
            return t
    return R


def sum_slots(parts, name):
    K, R, C = parts.shape
    T = _row_tile(R)

    def body(p_ref, o_ref):
        s = p_ref[0]
        for k in range(1, K):
            s = s + p_ref[k]
        o_ref[...] = s

    return _pc(body, name=name, grid=(R // T,),
               in_specs=[pl.BlockSpec((K, T, C), lambda i: (0, i, 0))],
               out_specs=pl.BlockSpec((T, C), lambda i: (i, 0)), out_shape=_sds((R, C)),
               compiler_params=_cp(("arbitrary",)))(parts)


def adamw_slots(parts, w, m, v, name):
    K, R, C = parts.shape
    T = R
    while K * T * C * 4 > (8 << 20) and T % 16 == 0:
        T //= 2

    def body(p_ref, w_ref, m_ref, v_ref, g_ref, d_ref, nm_ref, nv_ref):
        g = p_ref[0].astype(F32)
        for k in range(1, K):
            g = g + p_ref[k].astype(F32)
        g_ref[...] = g
        d_ref[...], nm_ref[...], nv_ref[...] = _adam_math(g, w_ref[...], m_ref[...], v_ref[...])

    t2 = pl.BlockSpec((T, C), lambda i: (i, 0))
    return _pc(body, name=name, grid=(R // T,),
               in_specs=[pl.BlockSpec((K, T, C), lambda i: (0, i, 0)), t2, t2, t2],
               out_specs=[t2, t2, t2, t2], out_shape=[_sds((R, C))] * 4,
               compiler_params=_cp(("arbitrary",)))(parts, w, m, v)


def ada_grad_adamw(cT, dmod_cols, w, m, v):
    NCOL = w.shape[1]

    def body(cT_ref, dm_ref, w_ref, m_ref, v_ref, g_ref, d_ref, nm_ref, nv_ref):
        def term(b):
            cv = cT_ref[b]
            return (cv * _sigmoid(cv)) * dm_ref[b:b + 1, :]

        g = term(0)
        for b in range(1, N_DEV):
            g = g + term(b)
        g_ref[...] = g
        d_ref[...], nm_ref[...], nv_ref[...] = _adam_math(g, w_ref[...], m_ref[...], v_ref[...])

    full = _const((D, NCOL))
    return _pc(body, name="ada_grad_adamw", grid=(1,),
               in_specs=[_const((N_DEV, D, 1)), _const((N_DEV, NCOL)), full, full, full],
               out_specs=[full, full, full, full], out_shape=[_sds((D, NCOL))] * 4,
               compiler_params=_cp())(cT, dmod_cols, w, m, v)


def _pack_rows(vecs, rows=None):
    flat = jnp.concatenate([jnp.ravel(v) for v in vecs])
    n = flat.shape[0]
    r = -(-n // 1024) * 8 if rows is None else rows
    return jnp.pad(flat, (0, r * 128 - n)).reshape(r, 128)


def _unpack_rows(packed, shapes):
    flat = packed.reshape(-1)
    out, off = [], 0
    for s in shapes:
        n = 1
        for d in s:
            n *= d
        out.append(flat[off:off + n].reshape(s))
        off += n
    return out


def _cols_from_shards(g, rows, cols):
    return g.reshape(N_DEV, rows, cols).transpose(1, 0, 2).reshape(rows, N_DEV * cols)


def _cols_to_shards(w, cols):
    rows = w.shape[0]
    return w.reshape(rows, N_DEV, cols).transpose(1, 0, 2).reshape(N_DEV, rows * cols)


def kernel(x, c, w_ada, b_ada, norm1_g, w_in, q_norm_g, k_norm_g, b_f, conv_w, conv_b, conv_ln_g, conv_ln_b, beta_attn, beta_conv, w_out, norm2_g, w_ff1, w_ff2, loss_target, m_w_ada, m_b_ada, m_norm1_g, m_w_in, m_q_norm_g, m_k_norm_g, m_b_f, m_conv_w, m_conv_b, m_conv_ln_g, m_conv_ln_b, m_beta_attn, m_beta_conv, m_w_out, m_norm2_g, m_w_ff1, m_w_ff2, v_w_ada, v_b_ada, v_norm1_g, v_w_in, v_q_norm_g, v_k_norm_g, v_b_f, v_conv_w, v_conv_b, v_conv_ln_g, v_conv_ln_b, v_beta_attn, v_beta_conv, v_w_out, v_norm2_g, v_w_ff1, v_w_ff2):
    S = x.shape[1]
    B = BLK
    NB = S // B
    me = 4 * lax.axis_index("x") + 2 * lax.axis_index("y") + lax.axis_index("c")
    xs = x[0]
    tgt = loss_target[0]

    ADA_C, IN_C, FF_C, CV_C = w_ada.shape[2], w_in.shape[2], w_ff1.shape[2], conv_w.shape[2]
    OUT_R, FF_R = w_out.shape[1], w_ff2.shape[1]

    n_in, n_out, n_f1, n_f2 = D * IN_C, OUT_R * D, D * FF_C, FF_R * D
    early = w_in.reshape(-1).astype(BF16)
    rows_e = -(-early.shape[0] // (128 * 16)) * 16
    early = jnp.pad(early, (0, rows_e * 128 - early.shape[0])).reshape(rows_e, 128)
    small0 = lax.bitcast_convert_type(_pack_rows([c, conv_w], rows=24), BF16).reshape(48, 128)
    ge = all_gather(jnp.concatenate([early, small0], axis=0), "ag_weights_early")
    gw = ge[:, 0:rows_e].reshape(N_DEV, rows_e * 128)
    gs0 = lax.bitcast_convert_type(ge[:, rows_e:].reshape(N_DEV, 24, 128, 2), F32).reshape(N_DEV, 24 * 128)
    late_w = [w_out[0].astype(BF16), w_ff1[0].astype(BF16), w_ff2[0].astype(BF16)]

    w_in_full = _cols_from_shards(gw[:, 0:n_in], D, IN_C)
    c_all = gs0[:, 0:D]
    conv_w_full = _cols_from_shards(gs0[:, D:D + KC * CV_C], KC, CV_C)
    conv_w_pad = jnp.pad(conv_w_full, ((0, 32 - KC), (0, 0)))

    w_qkvf = jnp.pad(w_in_full[:, 0:3 * A + H], ((0, 0), (0, ZT_ROWS - 3 * A - H)))
    wT = w_qkvf.T
    wr = w_in_full[:, 3 * A + H:]

    qg_col = q_norm_g.reshape(DH, 1)
    kg_col = k_norm_g.reshape(DH, 1)
    bf_col = b_f.reshape(H, 1)
    beta_a_col = beta_attn.reshape(A, 1)
    ii = lax.broadcasted_iota(jnp.int32, (B, B), 0)
    jj = lax.broadcasted_iota(jnp.int32, (B, B), 1)
    tri_up = (ii <= jj).astype(BF16)
    tri_lo = (ii >= jj).astype(BF16)

    modc = mod_columns(c_all, w_ada[0], lax.dynamic_slice(b_ada, (0, me * ADA_C), (1, ADA_C)))
    gm = all_gather(modc, "ag_mod")
    mod = lax.dynamic_index_in_dim(gm, me, axis=1, keepdims=False).reshape(1, N_DEV * ADA_C)
    sh1, sc1, g1, sh2, sc2, g2 = [mod[:, k * D:(k + 1) * D] for k in range(6)]
    qTa, kT, kaug, vT, vaug, zqk, fgT, alg, u0, stat = fwd_in(xs, norm1_g, sc1, sh1, wT, wr, qg_col,
                                                             kg_col, bf_col, tri_up)
    fmax = stat[:, :, 0].T
    fmin = stat[:, :, 1].T
    qk_max = jnp.sqrt(jnp.max(stat[:, :, 2], axis=0) * jnp.max(stat[:, :, 3], axis=0))
    thr = -(PRUNE + (2.02 / LOG2E) * qk_max)
    oT, lse, g_wo, w1_shards, g_w2 = attn_fwd(fmax, fmin, thr, qTa, kaug, vT, late_w)
    wo_full = g_wo.reshape(D, D)
    w2_full = g_w2.reshape(FF, D)
    x1, o, u1 = conv_merge_out(u0, xs, oT, conv_w_pad, conv_b, conv_ln_g, conv_ln_b, beta_conv,
                               beta_a_col, wo_full, g1)
    dy, loss_part, dg2, ra = mlp_fwd_loss(x1, tgt, norm2_g, sc2, sh2, g2, w1_shards, w2_full)

    dw1s, dw2, dh2p = mlp_bwd(x1, dy, ra, norm2_g, sc2, sh2, g2, w1_shards, w2_full)
    dx1, doTa, delta, du1, dwo, v1, v2, dba = merge_bwd(dh2p, x1, dy, o, oT, u1, norm2_g, sc2, g1,
                                                         conv_ln_g, conv_ln_b, beta_conv, beta_a_col, wo_full)
    early_g = [dwo.reshape(N_DEV, OUT_R, D), dw1s, dw2.reshape(N_DEV, FF_R, D)]
    dqT, dkT, dvT, dfk, r_out, r_f1, r_f2 = attn_bwd(fmax, fmin, thr, qTa, kaug, kT, vaug, doTa, lse,
                                                     delta, early_g)
    dzr, dzT, dcw, vc, dqg, dkg, dbf = head_bwd(du1, u0, alg, zqk, fgT, dqT, dkT, dvT, dfk, conv_w_pad,
                                                qg_col, kg_col, bf_col, tri_lo)
    dwT, dwr = in_bwd_w(xs, dzr, dzT, norm1_g, sc1, sh1)

    dw_in = jnp.concatenate([dwT.T[:, 0:3 * A + H], dwr], axis=1)
    late_g = _cols_to_shards(dw_in, IN_C)
    rows_b = -(-late_g.shape[1] // (128 * 256)) * 256
    late_g = jnp.pad(late_g, ((0, 0), (0, rows_b * 128 - late_g.shape[1]))).astype(BF16)
    grad_x, v0, recv_b = in_bwd_x(xs, dx1, dzr, dzT, norm1_g, sc1, sh1, w_qkvf, wr,
                                  late_g.reshape(N_DEV, rows_b, 128))

    dmod = jnp.concatenate([v0[0], v0[1], v1[3], v1[0], v1[1], dg2[0]])
    small1 = _pack_rows([dmod, v0[2], v1[2], dcw, vc[0], v2[1], v2[2], v2[0], dba, dqg, dkg,
                         jnp.pad(dbf.reshape(-1), (0, 120)), jnp.pad(loss_part.reshape(-1), (0, 127))])
    gs1 = all_gather(small1, "ag_small_bwd")
    tot = sum_slots(gs1, "sum_small")
    (g_b_ada, g_n1, g_n2, g_cw, g_cb, g_lng, g_lnb, g_bc, g_ba, g_qg, g_kg, g_bf, loss_v) = _unpack_rows(
        tot, [(1, 6 * D), (1, D), (1, D), (32, CW), (1, CW), (1, CW), (1, CW), (1, CW), (1, A),
              (1, DH), (1, DH), (1, 128), (1, 128)])
    loss = loss_v[0, 0]
    g_bf = g_bf[:, 0:H]
    g_cw_mine = lax.dynamic_slice(g_cw[0:KC], (0, me * CV_C), (KC, CV_C)).reshape(1, KC, CV_C)

    small_names = [(b_ada, m_b_ada, v_b_ada, g_b_ada), (norm1_g, m_norm1_g, v_norm1_g, g_n1),
                   (q_norm_g, m_q_norm_g, v_q_norm_g, g_qg), (k_norm_g, m_k_norm_g, v_k_norm_g, g_kg),
                   (b_f, m_b_f, v_b_f, g_bf), (conv_w, m_conv_w, v_conv_w, g_cw_mine),
                   (conv_b, m_conv_b, v_conv_b, g_cb), (conv_ln_g, m_conv_ln_g, v_conv_ln_g, g_lng),
                   (conv_ln_b, m_conv_ln_b, v_conv_ln_b, g_lnb), (beta_attn, m_beta_attn, v_beta_attn, g_ba),
                   (beta_conv, m_beta_conv, v_beta_conv, g_bc), (norm2_g, m_norm2_g, v_norm2_g, g_n2)]
    shapes_s = [t[0].shape for t in small_names]
    pw, pm, pv, pg = [_pack_rows([t[k] for t in small_names]) for k in range(4)]
    sg_, sd_, sm_, sv_ = adamw_slots(pg[None], pw, pm, pv, "adamw_small")
    sgs, sds, sms, svs = [_unpack_rows(a, shapes_s) for a in (sg_, sd_, sm_, sv_)]

    dmod_all = gs1[:, 0:48, :].reshape(N_DEV, N_DEV, ADA_C)
    dmod_cols = lax.dynamic_index_in_dim(dmod_all, me, axis=1, keepdims=False)
    ga, da, ma_, va_ = ada_grad_adamw(c_all.reshape(N_DEV, D, 1), dmod_cols, w_ada[0], m_w_ada[0], v_w_ada[0])

    res_b = adamw_slots(recv_b, *[_pack_rows(ws, rows=rows_b) for ws in ([w_in], [m_w_in], [v_w_in])],
                        "adamw_in")
    late = []
    for nm, parts, (w_, m_, v_) in (("adamw_out", r_out, (w_out, m_w_out, v_w_out)),
                                    ("adamw_ff1", r_f1, (w_ff1, m_w_ff1, v_w_ff1)),
                                    ("adamw_ff2", r_f2, (w_ff2, m_w_ff2, v_w_ff2))):
        late.append([r[None] for r in adamw_slots(parts, w_[0], m_[0], v_[0], nm)])
    bgs, bds, bms, bvs = [_unpack_rows(res_b[q], [(1, D, IN_C)]) + [late[0][q], late[1][q], late[2][q]]
                          for q in range(4)]

    def assemble(small, ada, bigs):
        (b_ada_, n1_, qg_, kg_, bf_, cw_, cb_, lng_, lnb_, ba_, bc_, n2_) = small
        return [ada.reshape(1, D, ADA_C), b_ada_, n1_, bigs[0], qg_, kg_, bf_, cw_, cb_, lng_, lnb_, ba_, bc_,
                bigs[1], n2_, bigs[2], bigs[3]]

    return (loss, grad_x.reshape(1, S, D), *assemble(sgs, ga, bgs), *assemble(sds, da, bds),
            *assemble(sms, ma_, bms), *assemble(svs, va_, bvs))
```

```python
import functools

import jax
import jax.numpy as jnp
from jax import lax
from jax.experimental import pallas as pl
from jax.experimental.pallas import tpu as pltpu

F32 = jnp.float32
BF16 = jnp.bfloat16

D = 1024
A = 512
CW = 512
H = 8
DH = 64
FF = 4096
KC = 31
HALO = 32
KR = 80
ZT_ROWS = 1552
FCH = 1024
EPS = 1e-6
BLK = 512
N_DEV = 8
VMEM_LIMIT = 56 * 1024 * 1024

ADAM_LR = 0.001
ADAM_B1 = 0.9
ADAM_B2 = 0.999
ADAM_EPS = 1e-08
ADAM_WD = 0.01
ADAM_STEP = 10

MESH = pl.DeviceIdType.MESH
ANY = pl.BlockSpec(memory_space=pl.ANY)
SMEM = pl.BlockSpec(memory_space=pltpu.SMEM)
PRUNE = 105.0
LOG2E = 1.4426950408889634


def _pc(body, **kw):
    return pl.pallas_call(body, **kw)


def _cp(sem=None):
    return pltpu.CompilerParams(dimension_semantics=sem, vmem_limit_bytes=VMEM_LIMIT)


def _sds(shape, dtype=F32):
    return jax.ShapeDtypeStruct(shape, dtype)


def _const(shape):
    n = len(shape)
    return pl.BlockSpec(shape, lambda *a: (0,) * n)


def _nt(a, b):
    return lax.dot_general(a, b, (((1,), (1,)), ((), ())), preferred_element_type=F32)


def _nn(a, b):
    return jnp.dot(a, b, preferred_element_type=F32)


def _sigmoid(v):
    return 1.0 / (1.0 + jnp.exp(-v))


def _split3(v):
    a1 = v.astype(BF16)
    r1 = v - a1.astype(F32)
    a2 = r1.astype(BF16)
    a3 = (r1 - a2.astype(F32)).astype(BF16)
    return a1, a2, a3


def _rowmean(v):
    return jnp.mean(v, axis=-1, keepdims=True)


def _colsum(v):
    return jnp.sum(v, axis=0, keepdims=True)


def _lanesum(v):
    return jnp.sum(v, axis=-1, keepdims=True)


def _coords():
    return lax.axis_index("x"), lax.axis_index("y"), lax.axis_index("c")


def _direct_copies(scatter, src_ref, dst_ref, send_sems, recv_sems, local_sem):
    mx, my, mc = _coords()
    me = 4 * mx + 2 * my + mc
    if scatter:
        local = pltpu.make_async_copy(src_ref.at[me], dst_ref.at[0], local_sem)
    else:
        local = pltpu.make_async_copy(src_ref, dst_ref.at[me], local_sem)
    remote = []
    for r in range(1, N_DEV):
        px = 1 - mx if r & 4 else mx
        py = 1 - my if r & 2 else my
        pcc = 1 - mc if r & 1 else mc
        remote.append(pltpu.make_async_remote_copy(
            src_ref=src_ref.at[4 * px + 2 * py + pcc] if scatter else src_ref,
            dst_ref=dst_ref.at[r] if scatter else dst_ref.at[me],
            send_sem=send_sems.at[r - 1], recv_sem=recv_sems.at[r - 1],
            device_id=(px, py, pcc), device_id_type=MESH))
    return [local] + remote


def _start_all(copies):
    for cp in copies:
        cp.start()


def _wait_all(copies):
    for cp in copies[1:]:
        cp.wait()
    copies[0].wait()


COMM_SEMS = [pltpu.SemaphoreType.DMA((7,)), pltpu.SemaphoreType.DMA((7,)), pltpu.SemaphoreType.DMA(())]


def gather_direct(arrays, name):
    NG = len(arrays)

    def body(*refs):
        sems = refs[2 * NG:]
        sets = [_direct_copies(False, refs[q], refs[NG + q], *sems[3 * q:3 * q + 3]) for q in range(NG)]
        for copies in sets:
            _start_all(copies)
        for copies in sets:
            _wait_all(copies)

    return _pc(body, name=name, out_shape=[_sds((N_DEV,) + a.shape, a.dtype) for a in arrays],
               in_specs=[ANY] * NG, out_specs=[ANY] * NG, scratch_shapes=COMM_SEMS * NG)(*arrays)


def mod_columns(c_all, w_shard, b_cols):
    NCOL = w_shard.shape[1]

    def body(c_ref, w_ref, b_ref, o_ref):
        cv = c_ref[...]
        o_ref[...] = _nn((cv * _sigmoid(cv)).astype(BF16), w_ref[...].astype(BF16)) + b_ref[...]

    return _pc(body, name="mod_columns", out_shape=_sds((N_DEV, NCOL)),
               in_specs=[_const((N_DEV, D)), _const((D, NCOL)), _const((1, NCOL))],
               out_specs=_const((N_DEV, NCOL)), grid=(1,), compiler_params=_cp())(c_all, w_shard, b_cols)


def _norm_mod(xv, g, sc, sh):
    r = lax.rsqrt(_rowmean(xv * xv) + EPS)
    xh = xv * r
    return r, xh, xh * (g * (1.0 + sc)) + sh


def _log_sigmoid(v):
    e = jnp.exp(-jnp.abs(v))
    l1p = jnp.where(e < 1e-4, e * (1.0 - 0.5 * e), jnp.log(1.0 + e))
    return jnp.minimum(v, 0.0) - l1p


def fwd_in(x, n1g, sc1, sh1, wT, wr, qg_col, kg_col, bf_col, tri):
    S = x.shape[0]
    B = BLK
    NB = S // B

    def body(x_ref, g_ref, sc_ref, sh_ref, wT_ref, wr_ref, qg_ref, kg_ref, bf_ref, tri_ref,
             qTa_ref, kT_ref, kaug_ref, vT_ref, vaug_ref, zqk_ref, fgT_ref, alg_ref, u0_ref, stat_ref,
             carry, tbuf, fs, nq, nk):
        i = pl.program_id(0)

        @pl.when(i == 0)
        def _():
            carry[...] = jnp.zeros_like(carry)

        _, _, h = _norm_mod(x_ref[...], g_ref[...], sc_ref[...], sh_ref[...])
        hb = h.astype(BF16)
        zT = _nt(wT_ref[...], hb)
        zr = _nn(hb, wr_ref[...])
        zqk_ref[...] = zT[0:2 * A]
        fgT = zT[3 * A:3 * A + 8]
        fgT_ref[...] = fgT
        alg_ref[...] = zr
        u0_ref[...] = zr[:, 0:CW] * _sigmoid(zr[:, CW:2 * CW])

        logf = _log_sigmoid(fgT + bf_ref[...])
        a1, a2, a3 = _split3(logf)
        tr = tri_ref[...]
        F = _nn(a1, tr) + _nn(a2, tr) + _nn(a3, tr) + carry[...]
        carry[...] = carry[...] + _lanesum(logf)
        p1, p2, p3 = _split3(F * LOG2E)
        n1, n2, n3 = _split3(F * (-LOG2E))
        for k, v in enumerate((p1, p2, p3, n1, n2, n3)):
            fs[k] = v.astype(F32)

        rowi = lax.broadcasted_iota(jnp.int32, (8, B), 0)
        zeros_tail = jnp.zeros((128 - DH - 8, B), F32)
        ones_row = jnp.where(lax.broadcasted_iota(jnp.int32, (KR - DH, B), 0) == 0, 1.0, 0.0).astype(BF16)
        for hh in range(H):
            sl = slice(hh * DH, (hh + 1) * DH)
            q = zT[sl]
            k = zT[A + hh * DH:A + (hh + 1) * DH]
            v = zT[2 * A + hh * DH:2 * A + (hh + 1) * DH]
            qh = q * lax.rsqrt(jnp.mean(q * q, axis=0, keepdims=True) + EPS) * qg_ref[...] * (0.125 * LOG2E)
            kh = k * lax.rsqrt(jnp.mean(k * k, axis=0, keepdims=True) + EPS) * kg_ref[...]
            frow = [fs[kk, hh:hh + 1, :] for kk in range(6)]
            qx = jnp.where(rowi < 3, 1.0, jnp.where(rowi == 3, frow[0], jnp.where(
                rowi == 4, frow[1], jnp.where(rowi == 5, frow[2], 0.0))))
            kx = jnp.where(rowi == 0, frow[3], jnp.where(rowi == 1, frow[4], jnp.where(
                rowi == 2, frow[5], jnp.where(rowi < 6, 1.0, 0.0))))
            tbuf[0:DH, :] = qh
            tbuf[DH:DH + 8, :] = qx
            tbuf[DH + 8:128, :] = zeros_tail
            qTa_ref[0, hh * 128:(hh + 1) * 128, :] = tbuf[...].astype(BF16)
            tbuf[0:DH, :] = kh
            tbuf[DH:DH + 8, :] = kx
            kaug_ref[hh] = tbuf[...].T.astype(BF16)
            kT_ref[0, hh * KR:hh * KR + DH, :] = kh.astype(BF16)
            kT_ref[0, hh * KR + DH:(hh + 1) * KR, :] = ones_row
            tbuf[0:DH, :] = v
            tbuf[DH:DH + 8, :] = jnp.zeros((8, B), F32)
            vaug_ref[hh] = tbuf[...].T.astype(BF16)
            vT_ref[0, sl, :] = v.astype(BF16)
            nq[hh:hh + 1, :] = jnp.max(_colsum(qh * qh), axis=1, keepdims=True)
            nk[hh:hh + 1, :] = jnp.max(_colsum(kh * kh), axis=1, keepdims=True)

        lane = lax.broadcasted_iota(jnp.int32, (8, 128), 1)
        stat_ref[0] = jnp.where(lane == 0, jnp.max(F, axis=1, keepdims=True), jnp.where(
            lane == 1, jnp.min(F, axis=1, keepdims=True), jnp.where(
                lane == 2, nq[...], jnp.where(lane == 3, nk[...], 0.0))))

    row = lambda w: pl.BlockSpec((B, w), lambda i: (i, 0))
    tsp = lambda r: pl.BlockSpec((1, r, B), lambda i: (i, 0, 0))
    return _pc(
        body, name="fwd_in", grid=(NB,),
        in_specs=[row(D), _const((1, D)), _const((1, D)), _const((1, D)), _const((ZT_ROWS, D)),
                  _const((D, 2 * CW)), _const((DH, 1)), _const((DH, 1)), _const((8, 1)), _const((B, B))],
        out_specs=[tsp(H * 128), tsp(H * KR), pl.BlockSpec((H, B, 128), lambda i: (0, i, 0)), tsp(A),
                   pl.BlockSpec((H, B, 128), lambda i: (0, i, 0)),
                   pl.BlockSpec((2 * A, B), lambda i: (0, i)), pl.BlockSpec((8, B), lambda i: (0, i)),
                   row(2 * CW), row(CW), pl.BlockSpec((1, 8, 128), lambda i: (i, 0, 0))],
        out_shape=[_sds((NB, H * 128, B), BF16), _sds((NB, H * KR, B), BF16), _sds((H, S, 128), BF16),
                   _sds((NB, A, B), BF16), _sds((H, S, 128), BF16), _sds((2 * A, S)), _sds((8, S)),
                   _sds((S, 2 * CW)), _sds((S, CW)), _sds((NB, 8, 128))],
        scratch_shapes=[pltpu.VMEM((8, 1), F32), pltpu.VMEM((128, B), F32), pltpu.VMEM((6, 8, B), F32),
                        pltpu.VMEM((8, 1), F32), pltpu.VMEM((8, 1), F32)],
        compiler_params=_cp(("arbitrary",)),
    )(x, n1g, sc1, sh1, wT, wr, qg_col, kg_col, bf_col, tri)


def _first_key_block(top, fmin_ref, cut, h, i):
    return lax.while_loop(
        lambda j: jnp.logical_and(j > 0, top - fmin_ref[h, jnp.maximum(j - 1, 0)] >= cut),
        lambda j: j - 1, i)


def _causal_keep(B):
    return lax.broadcasted_iota(jnp.int32, (B, B), 0) <= lax.broadcasted_iota(jnp.int32, (B, B), 1)


def attn_fwd(fmax, fmin, thr, qTa, kaug, vT, late_w):
    NB, _, B = qTa.shape
    S = NB * B
    NG = len(late_w)

    def body(fmax_ref, fmin_ref, thr_ref, q_ref, k_ref, v_ref, *rest):
        w_refs = rest[0:NG]
        o_ref, lse_ref = rest[NG:NG + 2]
        gw_refs = rest[NG + 2:2 * NG + 2]
        s0, s1, m_ref, l_ref, acc_ref = rest[2 * NG + 2:2 * NG + 7]
        sems = rest[2 * NG + 7:]
        h = pl.program_id(0)
        i = pl.program_id(1)

        def gathers():
            return [_direct_copies(False, w_refs[q], gw_refs[q], *sems[3 * q:3 * q + 3]) for q in range(NG)]

        @pl.when(jnp.logical_and(h == 0, i == 0))
        def _():
            for copies in gathers():
                _start_all(copies)

        jlo = _first_key_block(fmax_ref[h, i], fmin_ref, thr_ref[h], h, i)
        n = i - jlo

        def scores(j, s_ref):
            s_ref[...] = _nn(k_ref[0, pl.ds(pl.multiple_of(j * B, B), B), :], q_ref[0])

        def softmax_step(s_ref, j, masked):
            s = s_ref[...]
            if masked:
                s = jnp.where(_causal_keep(B), s, -jnp.inf)
            m = m_ref[...]
            mn = jnp.maximum(m, jnp.max(s, axis=0, keepdims=True))
            a = jnp.exp2(m - mn)
            p = jnp.exp2(s - mn)
            m_ref[...] = mn
            l_ref[...] = a * l_ref[...] + _colsum(p)
            acc_ref[...] = a * acc_ref[...] + _nn(v_ref[j], p.astype(BF16))

        m_ref[...] = jnp.full((1, B), -jnp.inf, F32)
        l_ref[...] = jnp.zeros((1, B), F32)
        acc_ref[...] = jnp.zeros((DH, B), F32)
        scores(jlo, s0)

        def pair(t, carry):
            j = jlo + 2 * t
            scores(j + 1, s1)
            softmax_step(s0, j, False)
            scores(j + 2, s0)
            softmax_step(s1, j + 1, False)
            return carry

        lax.fori_loop(0, n // 2, pair, 0)

        @pl.when(n % 2 == 1)
        def _():
            scores(i, s1)
            softmax_step(s0, i - 1, False)
            softmax_step(s1, i, True)

        @pl.when(n % 2 == 0)
        def _():
            softmax_step(s0, i, True)

        l = l_ref[...]
        o_ref[0] = acc_ref[...] / l
        lse_ref[0, 0] = m_ref[...] + jnp.log2(l)

        @pl.when(jnp.logical_and(h == H - 1, i == NB - 1))
        def _():
            for copies in gathers():
                _wait_all(copies)

    return _pc(
        body, name="attn_fwd", grid=(H, NB),
        in_specs=[SMEM, SMEM, SMEM, pl.BlockSpec((1, 128, B), lambda h, i: (i, h, 0)),
                  pl.BlockSpec((1, S, 128), lambda h, i: (h, 0, 0)),
                  pl.BlockSpec((NB, DH, B), lambda h, i: (0, h, 0))] + [ANY] * NG,
        out_specs=[pl.BlockSpec((1, DH, B), lambda h, i: (i, h, 0)),
                   pl.BlockSpec((1, 1, 1, B), lambda h, i: (h, i, 0, 0))] + [ANY] * NG,
        out_shape=[_sds((NB, A, B)), _sds((H, NB, 1, B))]
        + [_sds((N_DEV,) + w.shape, w.dtype) for w in late_w],
        scratch_shapes=[pltpu.VMEM((B, B), F32), pltpu.VMEM((B, B), F32), pltpu.VMEM((1, B), F32),
                        pltpu.VMEM((1, B), F32), pltpu.VMEM((DH, B), F32)] + COMM_SEMS * NG,
        compiler_params=_cp(("arbitrary", "arbitrary")),
    )(fmax, fmin, thr, qTa, kaug, vT, *late_w)


def attn_bwd(fmax, fmin, thr, qTa, kaug, kT, vaug, doTa, lse, delta, early_g):
    NG = len(early_g)
    NB, _, B = qTa.shape
    QR = 80
    CH = min(256, B)

    def body(fmax_ref, fmin_ref, thr_ref, q_ref, ka_ref, kt_ref, va_ref, do_ref, lse_ref, dl_ref, *rest):
        g_refs = rest[0:NG]
        dq_ref, dk_ref, dv_ref, dfk_ref = rest[NG:NG + 4]
        recv_refs = rest[NG + 4:2 * NG + 4]
        s0, d0, s1, d1, p0, e0, p1, e1, dk_acc, dv_acc = rest[2 * NG + 4:2 * NG + 14]
        sems = rest[2 * NG + 14:]
        h = pl.program_id(0)
        j = pl.program_id(1)

        def exchanges():
            return [_direct_copies(True, g_refs[q], recv_refs[q], *sems[3 * q:3 * q + 3]) for q in range(NG)]

        @pl.when(jnp.logical_and(h == 0, j == 0))
        def _():
            for copies in exchanges():
                _start_all(copies)

        bottom = fmin_ref[h, j]
        cut = thr_ref[h]
        ihi = lax.while_loop(
            lambda i: jnp.logical_and(i < NB - 1, fmax_ref[h, jnp.minimum(i + 1, NB - 1)] - bottom >= cut),
            lambda i: i + 1, j)

        @pl.when(j == 0)
        def _():
            dq_ref[...] = jnp.zeros_like(dq_ref)

        dk_acc[...] = jnp.zeros_like(dk_acc)
        dv_acc[...] = jnp.zeros_like(dv_acc)

        n = ihi - j + 1

        def blk(k):
            return jnp.minimum(j + k, ihi)

        def products(k, s_ref, d_ref):
            i = blk(k)
            s_ref[...] = _nn(ka_ref[0], q_ref[i])
            d_ref[...] = _nn(va_ref[0], do_ref[i])

        def elementwise(k, s_ref, d_ref, p_ref, ds_ref, masked):
            i = blk(k)
            p = jnp.exp2(s_ref[...] - lse_ref[0, i])
            if masked:
                p = jnp.where(_causal_keep(B), p, 0.0)
            p_ref[...] = p.astype(BF16)
            ds_ref[...] = (p * (d_ref[...] - dl_ref[0, i])).astype(BF16)

        def grads(k, p_ref, ds_ref):
            i = j + k
            dsb = ds_ref[...]
            dv_acc[...] = dv_acc[...] + _nt(do_ref[i][0:DH], p_ref[...])
            dk_acc[...] = dk_acc[...] + _nt(q_ref[i][0:QR], dsb)
            dq_ref[i] = dq_ref[i] + _nn(kt_ref[0], dsb)

        def trip(kp, s_p, d_p, ke, s_e, d_e, p_e, e_e, kg, p_g, e_g):
            ip, ie, ig = blk(kp), blk(ke), j + kg
            lse_e = lse_ref[0, ie]
            dl_e = dl_ref[0, ie]
            dq_part = jnp.zeros((KR, B), F32)
            for c in range(B // CH):
                rows = slice(c * CH, (c + 1) * CH)
                s_p[rows, :] = _nn(ka_ref[0, rows, :], q_ref[ip])
                d_p[rows, :] = _nn(va_ref[0, rows, :], do_ref[ip])
                p = jnp.exp2(s_e[rows, :] - lse_e)
                p_e[rows, :] = p.astype(BF16)
                e_e[rows, :] = (p * (d_e[rows, :] - dl_e)).astype(BF16)
                dsb = e_g[rows, :]
                dv_acc[:, rows] = dv_acc[:, rows] + _nt(do_ref[ig][0:DH], p_g[rows, :])
                dk_acc[:, rows] = dk_acc[:, rows] + _nt(q_ref[ig][0:QR], dsb)
                dq_part = dq_part + _nn(kt_ref[0, :, rows], dsb)
            dq_ref[ig] = dq_ref[ig] + dq_part

        products(0, s0, d0)
        products(1, s1, d1)
        elementwise(0, s0, d0, p0, e0, True)

        def pair(t, carry):
            k = 2 * t
            trip(k + 2, s0, d0, k + 1, s1, d1, p1, e1, k, p0, e0)
            trip(k + 3, s1, d1, k + 2, s0, d0, p0, e0, k + 1, p1, e1)
            return carry

        n_pairs = n // 2
        lax.fori_loop(0, n_pairs - 1, pair, 0)
        k_last = 2 * (n_pairs - 1)

        @pl.when(jnp.logical_and(n_pairs >= 1, n % 2 == 0))
        def _():
            elementwise(k_last + 1, s1, d1, p1, e1, False)
            grads(k_last, p0, e0)
            grads(k_last + 1, p1, e1)

        @pl.when(jnp.logical_and(n_pairs >= 1, n % 2 == 1))
        def _():
            trip(k_last + 2, s0, d0, k_last + 1, s1, d1, p1, e1, k_last, p0, e0)
            elementwise(k_last + 2, s0, d0, p0, e0, False)
            grads(k_last + 1, p1, e1)
            grads(k_last + 2, p0, e0)

        @pl.when(n == 1)
        def _():
            grads(0, p0, e0)

        dk_ref[0] = dk_acc[0:DH, :]
        dfk_ref[0, 0] = dk_acc[DH:DH + 8, :]
        dv_ref[0] = dv_acc[...]

        @pl.when(jnp.logical_and(h == H - 1, j == NB - 1))
        def _():
            for copies in exchanges():
                _wait_all(copies)

    per_kv = lambda r: pl.BlockSpec((1, r, B), lambda h, j: (j, h, 0))
    head_all = lambda r: pl.BlockSpec((NB, r, B), lambda h, j: (0, h, 0))
    aug = pl.BlockSpec((1, B, 128), lambda h, j: (h, j, 0))
    stat = pl.BlockSpec((1, NB, 1, B), lambda h, j: (h, 0, 0, 0))
    return _pc(
        body, name="attn_bwd", grid=(H, NB),
        in_specs=[SMEM, SMEM, SMEM, head_all(128), aug, per_kv(KR), aug, head_all(128), stat, stat]
        + [ANY] * NG,
        out_specs=[head_all(KR), per_kv(DH), per_kv(DH),
                   pl.BlockSpec((1, 1, 8, B), lambda h, j: (h, j, 0, 0))] + [ANY] * NG,
        out_shape=[_sds((NB, H * KR, B)), _sds((NB, A, B)), _sds((NB, A, B)), _sds((H, NB, 8, B))]
        + [_sds(g.shape, g.dtype) for g in early_g],
        scratch_shapes=[pltpu.VMEM((B, B), F32)] * 4 + [pltpu.VMEM((B, B), BF16)] * 4
        + [pltpu.VMEM((QR, B), F32), pltpu.VMEM((DH, B), F32)] + COMM_SEMS * NG,
        compiler_params=_cp(("arbitrary", "arbitrary")),
    )(fmax, fmin, thr, qTa, kaug, kT, vaug, doTa, lse, delta, *early_g)


def _conv_tail(u1, lng, lnb, beta_c):
    mu = _rowmean(u1)
    d = u1 - mu
    rstd = lax.rsqrt(_rowmean(d * d) + EPS)
    xhat = d * rstd
    u2 = xhat * lng + lnb
    sg = _sigmoid(u2)
    u3 = u2 * sg
    rc = lax.rsqrt(_rowmean(u3 * u3) + EPS)
    n3 = u3 * rc
    return rstd, xhat, u2, sg, rc, n3, n3 * beta_c


def _attn_tail(oT, beta_a_col):
    ra = lax.rsqrt(jnp.mean(oT * oT, axis=0, keepdims=True) + EPS)
    ohat = oT * ra
    return ra, ohat, ohat * beta_a_col


CONV_ROWS = 32
CORR_ROWS = 16


def _fill_shifted(buf, sh):
    rows = sh.shape[1]
    for ph in range(8):
        sh[ph] = buf[pl.ds(ph, rows), :]


def _tap_rows(sh, r0, o, rows):
    return sh[o % 8, pl.ds(pl.multiple_of(r0 + 8 * (o // 8), 8), rows), :]


def _depthwise(sh, w_ref, offs, bias, out_ref, B):
    def chunk(ci, carry):
        r0 = pl.multiple_of(ci * CONV_ROWS, CONV_ROWS)
        acc = jnp.broadcast_to(bias, (CONV_ROWS, CW))
        for k, o in enumerate(offs):
            acc = acc + w_ref[k:k + 1, :] * _tap_rows(sh, r0, o, CONV_ROWS)
        out_ref[pl.ds(r0, CONV_ROWS), :] = acc
        return carry

    lax.fori_loop(0, B // CONV_ROWS, chunk, 0)


def _tap_gradients(sh, d_ref, offs, dw_ref, B):
    for g0 in range(0, len(offs), 8):
        ks = list(range(g0, min(g0 + 8, len(offs))))

        def chunk(ci, accs, ks=ks):
            r0 = pl.multiple_of(ci * CORR_ROWS, CORR_ROWS)
            d = d_ref[pl.ds(r0, CORR_ROWS), :]
            out = []
            for a, k in zip(accs, ks):
                pr = d * _tap_rows(sh, r0, offs[k], CORR_ROWS)
                out.append(a + pr[0:8] + pr[8:16])
            return tuple(out)

        accs = lax.fori_loop(0, B // CORR_ROWS, chunk, tuple(jnp.zeros((8, CW), F32) for _ in ks))
        for a, k in zip(accs, ks):
            dw_ref[k:k + 1, :] = dw_ref[k:k + 1, :] + _colsum(a)


def conv_merge_out(u0, x, oT, conv_w, conv_b, lng, lnb, beta_c, beta_a_col, wo, g1):
    S = x.shape[0]
    B = BLK
    NB = S // B

    def body(uc_ref, up_ref, x_ref, oT_ref, w_ref, cb_ref, lng_ref, lnb_ref, bc_ref, ba_ref, wo_ref,
             g1_ref, x1_ref, o_ref, u1_ref, ubuf, sh):
        i = pl.program_id(0)
        ubuf[0:HALO, :] = jnp.where(i > 0, up_ref[B - HALO:B, :], 0.0)
        ubuf[HALO:HALO + B, :] = uc_ref[...]
        ubuf[HALO + B:HALO + B + 8, :] = jnp.zeros((8, CW), F32)
        _fill_shifted(ubuf, sh)
        _depthwise(sh, w_ref, [HALO - (KC - 1) + k for k in range(KC)], cb_ref[...], u1_ref, B)
        mc = _conv_tail(u1_ref[...], lng_ref[...], lnb_ref[...], bc_ref[...])[-1]
        maT = _attn_tail(oT_ref[0], ba_ref[...])[-1]
        ma = maT.T
        o = _nn(ma.astype(BF16), wo_ref[0:A, :]) + _nn(mc.astype(BF16), wo_ref[A:D, :])
        o_ref[...] = o
        x1_ref[...] = x_ref[...] + g1_ref[...] * o

    row = lambda w: pl.BlockSpec((B, w), lambda i: (i, 0))
    return _pc(
        body, name="conv_merge_out", grid=(NB,),
        in_specs=[row(CW), pl.BlockSpec((B, CW), lambda i: (jnp.maximum(i - 1, 0), 0)), row(D),
                  pl.BlockSpec((1, A, B), lambda i: (i, 0, 0)), _const((32, CW)), _const((1, CW)),
                  _const((1, CW)), _const((1, CW)), _const((1, CW)), _const((A, 1)), _const((D, D)),
                  _const((1, D))],
        out_specs=[row(D), row(D), row(CW)],
        out_shape=[_sds((S, D)), _sds((S, D)), _sds((S, CW))],
        scratch_shapes=[pltpu.VMEM((B + HALO + 8, CW), F32), pltpu.VMEM((8, B + HALO, CW), F32)],
        compiler_params=_cp(("arbitrary",)),
    )(u0, u0, x, oT, conv_w, conv_b, lng, lnb, beta_c, beta_a_col, wo, g1)


def mlp_fwd_loss(x1, tgt, n2g, sc2, sh2, g2, w1s, w2):
    S = x1.shape[0]
    B = BLK
    NB = S // B
    NS, _, FS = w1s.shape

    def body(x1_ref, t_ref, g_ref, sc_ref, sh_ref, g2_ref, w1_hbm, w2_hbm, dy_ref, loss_ref, dg2_ref, ra_ref,
             w1_v, w2_v, sem):
        i = pl.program_id(0)

        @pl.when(i == 0)
        def _():
            for src, dst in ((w1_hbm, w1_v), (w2_hbm, w2_v)):
                cp = pltpu.make_async_copy(src, dst, sem)
                cp.start()
                cp.wait()
            loss_ref[...] = jnp.zeros_like(loss_ref)
            dg2_ref[...] = jnp.zeros_like(dg2_ref)

        x1 = x1_ref[...]
        hs = _norm_mod(x1, g_ref[...], sc_ref[...], sh_ref[...])[2].astype(BF16)
        y2 = jnp.zeros((B, D), F32)
        for q in range(NS):
            a = jnp.maximum(_nn(hs, w1_v[q]), 0.0)
            ra_ref[:, q * FS:(q + 1) * FS] = a.astype(BF16)
            y2 = y2 + _nn((a * a).astype(BF16), w2_v[q * FS:(q + 1) * FS, :])
        e = x1 + g2_ref[...] * y2 - t_ref[...]
        dy = e * (1.0 / D)
        dy_ref[...] = dy
        loss_ref[...] = loss_ref[...] + 0.5 * _colsum(_rowmean(e * e))
        dg2_ref[...] = dg2_ref[...] + _colsum(dy * y2)

    row = pl.BlockSpec((B, D), lambda i: (i, 0))
    return _pc(
        body, name="mlp_fwd_loss", grid=(NB,),
        in_specs=[row, row, _const((1, D)), _const((1, D)), _const((1, D)), _const((1, D)), ANY, ANY],
        out_specs=[row, _const((1, 1)), _const((1, D)), pl.BlockSpec((B, FF), lambda i: (i, 0))],
        out_shape=[_sds((S, D)), _sds((1, 1)), _sds((1, D)), _sds((S, FF), BF16)],
        scratch_shapes=[pltpu.VMEM(w1s.shape, BF16), pltpu.VMEM(w2.shape, BF16), pltpu.SemaphoreType.DMA(())],
        compiler_params=_cp(("arbitrary",)),
    )(x1, tgt, n2g, sc2, sh2, g2, w1s, w2)


def mlp_bwd(x1, dy, ra, n2g, sc2, sh2, g2, w1s, w2):
    S = x1.shape[0]
    B = BLK
    NB = S // B
    NC = FF // FCH
    NS, _, FS = w1s.shape
    SPC = NS // NC

    def body(x1_ref, dy_ref, ra_ref, g_ref, sc_ref, sh_ref, g2_ref, w1_ref, w2_ref, dw1_ref, dw2_ref, dh_ref):
        i = pl.program_id(1)

        @pl.when(i == 0)
        def _():
            dw1_ref[...] = jnp.zeros_like(dw1_ref)
            dw2_ref[...] = jnp.zeros_like(dw2_ref)

        h2 = _norm_mod(x1_ref[...], g_ref[...], sc_ref[...], sh_ref[...])[2]
        h2t = h2.T.astype(BF16)
        ra = ra_ref[...].astype(F32)
        dyb = (dy_ref[...] * g2_ref[...]).astype(BF16)
        db = _nt(dyb, w2_ref[...])
        dab = (db * (2.0 * ra)).astype(BF16)
        dh = jnp.zeros((B, D), F32)
        for q in range(SPC):
            part = dab[:, q * FS:(q + 1) * FS]
            dh = dh + _nt(part, w1_ref[q])
            dw1_ref[q] = dw1_ref[q] + _nn(h2t, part)
        dh_ref[0] = dh
        dw2_ref[...] = dw2_ref[...] + _nn((ra * ra).T.astype(BF16), dyb)

    row = pl.BlockSpec((B, D), lambda c, i: (i, 0))
    vec = pl.BlockSpec((1, D), lambda c, i: (0, 0))
    wsh = pl.BlockSpec((SPC, D, FS), lambda c, i: (c, 0, 0))
    w2b = pl.BlockSpec((FCH, D), lambda c, i: (c, 0))
    return _pc(
        body, name="mlp_bwd", grid=(NC, NB),
        in_specs=[row, row, pl.BlockSpec((B, FCH), lambda c, i: (i, c)), vec, vec, vec, vec, wsh, w2b],
        out_specs=[wsh, w2b, pl.BlockSpec((1, B, D), lambda c, i: (c, i, 0))],
        out_shape=[_sds(w1s.shape), _sds((FF, D)), _sds((NC, S, D))],
        compiler_params=_cp(("arbitrary", "arbitrary")),
    )(x1, dy, ra, n2g, sc2, sh2, g2, w1s, w2)


def merge_bwd(dh2p, x1, dy, o, oT, u1, n2g, sc2, g1, lng, lnb, beta_c, beta_a_col, wo):
    S = x1.shape[0]
    B = BLK
    NB = S // B
    NC = dh2p.shape[0]

    def body(dh_ref, x1_ref, dy_ref, o_ref, oT_ref, u1_ref, g_ref, sc_ref, g1_ref, lng_ref, lnb_ref,
             bc_ref, ba_ref, wo_ref,
             dx1_ref, doTa_ref, dl_ref, du1_ref, dwo_ref, v1_ref, v2_ref, dba_ref):
        i = pl.program_id(0)

        @pl.when(i == 0)
        def _():
            dwo_ref[...] = jnp.zeros_like(dwo_ref)
            v1_ref[...] = jnp.zeros_like(v1_ref)
            v2_ref[...] = jnp.zeros_like(v2_ref)
            dba_ref[...] = jnp.zeros_like(dba_ref)

        dh2 = dh_ref[0]
        for cc in range(1, NC):
            dh2 = dh2 + dh_ref[cc]
        x1 = x1_ref[...]
        g = g_ref[...]
        sc = sc_ref[...]
        r2 = lax.rsqrt(_rowmean(x1 * x1) + EPS)
        xh = x1 * r2
        dhx = dh2 * xh
        v1_ref[0:1, :] = v1_ref[0:1, :] + _colsum(dh2)
        v1_ref[1:2, :] = v1_ref[1:2, :] + _colsum(dhx) * g
        v1_ref[2:3, :] = v1_ref[2:3, :] + _colsum(dhx) * (1.0 + sc)
        dxh = dh2 * (g * (1.0 + sc))
        dx1 = dy_ref[...] + r2 * (dxh - xh * _rowmean(dxh * xh))
        dx1_ref[...] = dx1
        v1_ref[3:4, :] = v1_ref[3:4, :] + _colsum(dx1 * o_ref[...])
        dob = (dx1 * g1_ref[...]).astype(BF16)

        lng = lng_ref[...]
        bc = bc_ref[...]
        rstd, xhat, u2, sg, rc, n3, mc = _conv_tail(u1_ref[...], lng, lnb_ref[...], bc)
        ba = ba_ref[...]
        oT = oT_ref[0]
        ra, ohat, maT = _attn_tail(oT, ba)
        dwo_ref[0:A, :] = dwo_ref[0:A, :] + _nn(maT.astype(BF16), dob)
        dwo_ref[A:D, :] = dwo_ref[A:D, :] + _nn(mc.T.astype(BF16), dob)
        dmaT = _nt(wo_ref[0:A, :], dob)
        dmc = _nt(dob, wo_ref[A:D, :])

        dba_ref[...] = dba_ref[...] + _lanesum(dmaT * ohat)
        dohat = dmaT * ba
        doT = ra * (dohat - ohat * jnp.mean(dohat * ohat, axis=0, keepdims=True))
        prod = doT * oT
        zpad = jnp.zeros((128 - DH, B), BF16)
        for hh in range(H):
            sl = slice(hh * DH, (hh + 1) * DH)
            dl_ref[hh, 0] = _colsum(prod[sl])
            doTa_ref[0, hh * 128:hh * 128 + DH, :] = doT[sl].astype(BF16)
            doTa_ref[0, hh * 128 + DH:(hh + 1) * 128, :] = zpad

        v2_ref[0:1, :] = v2_ref[0:1, :] + _colsum(dmc * n3)
        dn3 = dmc * bc
        du3 = rc * (dn3 - n3 * _rowmean(dn3 * n3))
        du2 = du3 * (sg * (1.0 + u2 * (1.0 - sg)))
        v2_ref[1:2, :] = v2_ref[1:2, :] + _colsum(du2 * xhat)
        v2_ref[2:3, :] = v2_ref[2:3, :] + _colsum(du2)
        dxhat = du2 * lng
        du1_ref[...] = rstd * (dxhat - _rowmean(dxhat) - xhat * _rowmean(dxhat * xhat))

    row = lambda w: pl.BlockSpec((B, w), lambda i: (i, 0))
    return _pc(
        body, name="merge_bwd", grid=(NB,),
        in_specs=[pl.BlockSpec((NC, B, D), lambda i: (0, i, 0)), row(D), row(D), row(D),
                  pl.BlockSpec((1, A, B), lambda i: (i, 0, 0)), row(CW), _const((1, D)), _const((1, D)),
                  _const((1, D)), _const((1, CW)), _const((1, CW)), _const((1, CW)), _const((A, 1)),
                  _const((D, D))],
        out_specs=[row(D), pl.BlockSpec((1, H * 128, B), lambda i: (i, 0, 0)),
                   pl.BlockSpec((H, 1, 1, B), lambda i: (0, i, 0, 0)), row(CW), _const((D, D)),
                   _const((8, D)), _const((8, CW)), _const((A, 1))],
        out_shape=[_sds((S, D)), _sds((NB, H * 128, B), BF16), _sds((H, NB, 1, B)), _sds((S, CW)),
                   _sds((D, D)), _sds((8, D)), _sds((8, CW)), _sds((A, 1))],
        compiler_params=_cp(("arbitrary",)),
    )(dh2p, x1, dy, o, oT, u1, n2g, sc2, g1, lng, lnb, beta_c, beta_a_col, wo)


def head_bwd(du1, u0, alg, zqk, fgT, dqT, dkT, dvT, dfk, conv_w, qg_col, kg_col, bf_col, tri_lo):
    S = u0.shape[0]
    B = BLK
    NB = S // B

    def body(dc_ref, dn_ref, uc_ref, up_ref, alg_ref, zqk_ref, fg_ref, dq_ref, dk_ref, dv_ref, dfk_ref,
             w_ref, qg_ref, kg_ref, bf_ref, tri_ref,
             dzr_ref, dzT_ref, dcw_ref, vc_ref, dqg_ref, dkg_ref, dbf_ref,
             buf, sh, du0_ref, carry, fbuf):
        pid = pl.program_id(0)
        ri = NB - 1 - pid

        @pl.when(pid == 0)
        def _():
            carry[...] = jnp.zeros_like(carry)
            dcw_ref[...] = jnp.zeros_like(dcw_ref)
            vc_ref[...] = jnp.zeros_like(vc_ref)
            dqg_ref[...] = jnp.zeros_like(dqg_ref)
            dkg_ref[...] = jnp.zeros_like(dkg_ref)
            dbf_ref[...] = jnp.zeros_like(dbf_ref)

        zero8 = jnp.zeros((8, CW), F32)
        buf[0:B, :] = dc_ref[...]
        buf[B:B + HALO, :] = jnp.where(ri < NB - 1, dn_ref[0:HALO, :], 0.0)
        buf[B + HALO:B + HALO + 8, :] = zero8
        _fill_shifted(buf, sh)
        _depthwise(sh, w_ref, [KC - 1 - k for k in range(KC)], jnp.zeros((1, CW), F32), du0_ref, B)
        buf[0:HALO, :] = jnp.where(ri > 0, up_ref[B - HALO:B, :], 0.0)
        buf[HALO:HALO + B, :] = uc_ref[...]
        buf[HALO + B:HALO + B + 8, :] = zero8
        _fill_shifted(buf, sh)
        _tap_gradients(sh, dc_ref, [HALO - (KC - 1) + k for k in range(KC)], dcw_ref, B)
        vc_ref[0:1, :] = vc_ref[0:1, :] + _colsum(dc_ref[...])

        du0 = du0_ref[...]
        al = alg_ref[:, 0:CW]
        sg = _sigmoid(alg_ref[:, CW:2 * CW])
        dzr_ref[:, 0:CW] = (du0 * sg).astype(BF16)
        dzr_ref[:, CW:2 * CW] = (du0 * al * sg * (1.0 - sg)).astype(BF16)

        dqg = jnp.zeros((DH, B), F32)
        dkg = jnp.zeros((DH, B), F32)
        qg = qg_ref[...]
        kg = kg_ref[...]
        for hh in range(H):
            sl = slice(hh * DH, (hh + 1) * DH)
            q = zqk_ref[sl, :]
            rq = lax.rsqrt(jnp.mean(q * q, axis=0, keepdims=True) + EPS)
            qn = q * rq
            dqh = dq_ref[0, hh * KR:hh * KR + DH, :] * 0.125
            dqg = dqg + dqh * qn
            dqn = dqh * qg
            dzT_ref[0, sl, :] = (rq * (dqn - qn * jnp.mean(dqn * qn, axis=0, keepdims=True))).astype(BF16)
            k = zqk_ref[A + hh * DH:A + (hh + 1) * DH, :]
            rk = lax.rsqrt(jnp.mean(k * k, axis=0, keepdims=True) + EPS)
            kn = k * rk
            dkh = dk_ref[0, sl, :] * (1.0 / LOG2E)
            dkg = dkg + dkh * kn
            dkn = dkh * kg
            dzT_ref[0, A + hh * DH:A + (hh + 1) * DH, :] = (
                rk * (dkn - kn * jnp.mean(dkn * kn, axis=0, keepdims=True))).astype(BF16)
            fbuf[hh:hh + 1, :] = dq_ref[0, hh * KR + DH:hh * KR + DH + 1, :] - dfk_ref[hh, 0, 0:1, :]
        dqg_ref[...] = dqg_ref[...] + _lanesum(dqg)
        dkg_ref[...] = dkg_ref[...] + _lanesum(dkg)
        dzT_ref[0, 2 * A:3 * A, :] = dv_ref[0].astype(BF16)

        dF = fbuf[...]
        a1, a2, a3 = _split3(dF)
        tr = tri_ref[...]
        dlogf = _nn(a1, tr) + _nn(a2, tr) + _nn(a3, tr) + carry[...]
        carry[...] = carry[...] + _lanesum(dF)
        dfg = dlogf * _sigmoid(-(fg_ref[...] + bf_ref[...]))
        dbf_ref[...] = dbf_ref[...] + _lanesum(dfg)
        dzT_ref[0, 3 * A:ZT_ROWS, :] = jnp.concatenate([dfg, jnp.zeros((8, B), F32)], axis=0).astype(BF16)

    rrow = lambda w: pl.BlockSpec((B, w), lambda p: (NB - 1 - p, 0))
    rts = lambda r: pl.BlockSpec((1, r, B), lambda p: (NB - 1 - p, 0, 0))
    return _pc(
        body, name="head_bwd", grid=(NB,),
        in_specs=[rrow(CW), pl.BlockSpec((B, CW), lambda p: (jnp.minimum(NB - p, NB - 1), 0)),
                  rrow(CW), pl.BlockSpec((B, CW), lambda p: (jnp.maximum(NB - 2 - p, 0), 0)),
                  rrow(2 * CW), pl.BlockSpec((2 * A, B), lambda p: (0, NB - 1 - p)),
                  pl.BlockSpec((8, B), lambda p: (0, NB - 1 - p)), rts(H * KR), rts(A), rts(A),
                  pl.BlockSpec((H, 1, 8, B), lambda p: (0, NB - 1 - p, 0, 0)),
                  _const((32, CW)), _const((DH, 1)), _const((DH, 1)), _const((8, 1)), _const((B, B))],
        out_specs=[rrow(2 * CW), rts(ZT_ROWS), _const((32, CW)), _const((8, CW)), _const((DH, 1)),
                   _const((DH, 1)), _const((8, 1))],
        out_shape=[_sds((S, 2 * CW), BF16), _sds((NB, ZT_ROWS, B), BF16), _sds((32, CW)), _sds((8, CW)),
                   _sds((DH, 1)), _sds((DH, 1)), _sds((8, 1))],
        scratch_shapes=[pltpu.VMEM((B + HALO + 8, CW), F32), pltpu.VMEM((8, B + HALO, CW), F32),
                        pltpu.VMEM((B, CW), F32), pltpu.VMEM((8, 1), F32), pltpu.VMEM((8, B), F32)],
        compiler_params=_cp(("arbitrary",)),
    )(du1, du1, u0, u0, alg, zqk, fgT, dqT, dkT, dvT, dfk, conv_w, qg_col, kg_col, bf_col, tri_lo)


def in_bwd_w(x, dzr, dzT, n1g, sc1, sh1):
    S = x.shape[0]
    B = BLK
    NB = S // B

    def body(x_ref, dzr_ref, dzT_ref, g_ref, sc_ref, sh_ref, dwT_hbm, dwr_hbm, dwT_acc, dwr_acc, sem):
        i = pl.program_id(0)

        @pl.when(i == 0)
        def _():
            dwT_acc[...] = jnp.zeros_like(dwT_acc)
            dwr_acc[...] = jnp.zeros_like(dwr_acc)

        h1 = _norm_mod(x_ref[...], g_ref[...], sc_ref[...], sh_ref[...])[2]
        dwT_acc[...] = dwT_acc[...] + _nn(dzT_ref[0], h1.astype(BF16))
        dwr_acc[...] = dwr_acc[...] + _nn(h1.T.astype(BF16), dzr_ref[...])

        @pl.when(i == NB - 1)
        def _():
            for src, dst in ((dwT_acc, dwT_hbm), (dwr_acc, dwr_hbm)):
                cp = pltpu.make_async_copy(src, dst, sem)
                cp.start()
                cp.wait()

    row = lambda w: pl.BlockSpec((B, w), lambda i: (i, 0))
    return _pc(
        body, name="in_bwd_w", grid=(NB,),
        in_specs=[row(D), row(2 * CW), pl.BlockSpec((1, ZT_ROWS, B), lambda i: (i, 0, 0)),
                  _const((1, D)), _const((1, D)), _const((1, D))],
        out_specs=[ANY, ANY],
        out_shape=[_sds((ZT_ROWS, D)), _sds((D, 2 * CW))],
        scratch_shapes=[pltpu.VMEM((ZT_ROWS, D), F32), pltpu.VMEM((D, 2 * CW), F32),
                        pltpu.SemaphoreType.DMA(())],
        compiler_params=_cp(("arbitrary",)),
    )(x, dzr, dzT, n1g, sc1, sh1)


def in_bwd_x(x, dx1, dzr, dzT, n1g, sc1, sh1, w_qkvf, wr, late_g):
    S = x.shape[0]
    B = BLK
    NB = S // B

    def body(x_ref, dx1_ref, dzr_ref, dzT_ref, g_ref, sc_ref, sh_ref, w_hbm, wr_hbm, lg_ref,
             gx_ref, v_ref, recv_ref, w_v, wr_v, sem, send_sems, recv_sems, local_sem):
        i = pl.program_id(0)

        @pl.when(i == 0)
        def _():
            _start_all(_direct_copies(True, lg_ref, recv_ref, send_sems, recv_sems, local_sem))
            for src, dst in ((w_hbm, w_v), (wr_hbm, wr_v)):
                cp = pltpu.make_async_copy(src, dst, sem)
                cp.start()
                cp.wait()
            v_ref[...] = jnp.zeros_like(v_ref)

        g = g_ref[...]
        sc = sc_ref[...]
        r1, xh, _ = _norm_mod(x_ref[...], g, sc, sh_ref[...])
        dh1 = _nt(dzr_ref[...], wr_v[...]) + _nn(w_v[...], dzT_ref[0]).T
        dhx = dh1 * xh
        v_ref[0:1, :] = v_ref[0:1, :] + _colsum(dh1)
        v_ref[1:2, :] = v_ref[1:2, :] + _colsum(dhx) * g
        v_ref[2:3, :] = v_ref[2:3, :] + _colsum(dhx) * (1.0 + sc)
        dxh = dh1 * (g * (1.0 + sc))
        gx_ref[...] = dx1_ref[...] + r1 * (dxh - xh * _rowmean(dxh * xh))

        @pl.when(i == NB - 1)
        def _():
            _wait_all(_direct_copies(True, lg_ref, recv_ref, send_sems, recv_sems, local_sem))

    row = lambda w: pl.BlockSpec((B, w), lambda i: (i, 0))
    return _pc(
        body, name="in_bwd_x", grid=(NB,),
        in_specs=[row(D), row(D), row(2 * CW), pl.BlockSpec((1, ZT_ROWS, B), lambda i: (i, 0, 0)),
                  _const((1, D)), _const((1, D)), _const((1, D)), ANY, ANY, ANY],
        out_specs=[row(D), _const((8, D)), ANY],
        out_shape=[_sds((S, D)), _sds((8, D)), _sds(late_g.shape, late_g.dtype)],
        scratch_shapes=[pltpu.VMEM((D, ZT_ROWS), BF16), pltpu.VMEM((D, 2 * CW), BF16),
                        pltpu.SemaphoreType.DMA(())] + COMM_SEMS,
        compiler_params=_cp(("arbitrary",)),
    )(x, dx1, dzr, dzT, n1g, sc1, sh1, w_qkvf, wr, late_g)


def _adam_math(g, w, m, v):
    m = ADAM_B1 * m + (1.0 - ADAM_B1) * g
    v = ADAM_B2 * v + (1.0 - ADAM_B2) * (g * g)
    m_hat = m / (1.0 - ADAM_B1 ** ADAM_STEP)
    v_hat = v / (1.0 - ADAM_B2 ** ADAM_STEP)
    delta = -ADAM_LR * (m_hat / (jnp.sqrt(v_hat) + ADAM_EPS) + ADAM_WD * w)
    return delta, m, v


def _row_tile(R):
    for t in (1024, 512, 256, 128, 64, 32, 16, 8):
        if R % t == 0:
            return t
    return R


def sum_slots(parts, name):
    K, R, C = parts.shape
    T = _row_tile(R)

    def body(p_ref, o_ref):
        s = p_ref[0]
        for k in range(1, K):
            s = s + p_ref[k]
        o_ref[...] = s

    return _pc(body, name=name, grid=(R // T,),
               in_specs=[pl.BlockSpec((K, T, C), lambda i: (0, i, 0))],
               out_specs=pl.BlockSpec((T, C), lambda i: (i, 0)), out_shape=_sds((R, C)),
               compiler_params=_cp(("arbitrary",)))(parts)


def adamw_slots(parts, w, m, v, name):
    K, R, C = parts.shape
    T = R
    while K * T * C * 4 > (8 << 20) and T % 16 == 0:
        T //= 2

    def body(p_ref, w_ref, m_ref, v_ref, g_ref, d_ref, nm_ref, nv_ref):
        g = p_ref[0].astype(F32)
        for k in range(1, K):
            g = g + p_ref[k].astype(F32)
        g_ref[...] = g
        d_ref[...], nm_ref[...], nv_ref[...] = _adam_math(g, w_ref[...], m_ref[...], v_ref[...])

    t2 = pl.BlockSpec((T, C), lambda i: (i, 0))
    return _pc(body, name=name, grid=(R // T,),
               in_specs=[pl.BlockSpec((K, T, C), lambda i: (0, i, 0)), t2, t2, t2],
               out_specs=[t2, t2, t2, t2], out_shape=[_sds((R, C))] * 4,
               compiler_params=_cp(("arbitrary",)))(parts, w, m, v)


def ada_grad_adamw(cT, dmod_cols, w, m, v):
    NCOL = w.shape[1]

    def body(cT_ref, dm_ref, w_ref, m_ref, v_ref, g_ref, d_ref, nm_ref, nv_ref):
        def term(b):
            cv = cT_ref[b]
            return (cv * _sigmoid(cv)) * dm_ref[b:b + 1, :]

        g = term(0)
        for b in range(1, N_DEV):
            g = g + term(b)
        g_ref[...] = g
        d_ref[...], nm_ref[...], nv_ref[...] = _adam_math(g, w_ref[...], m_ref[...], v_ref[...])

    full = _const((D, NCOL))
    return _pc(body, name="ada_grad_adamw", grid=(1,),
               in_specs=[_const((N_DEV, D, 1)), _const((N_DEV, NCOL)), full, full, full],
               out_specs=[full, full, full, full], out_shape=[_sds((D, NCOL))] * 4,
               compiler_params=_cp())(cT, dmod_cols, w, m, v)


def _pack_rows(vecs, rows=None):
    flat = jnp.concatenate([jnp.ravel(v) for v in vecs])
    n = flat.shape[0]
    r = -(-n // 1024) * 8 if rows is None else rows
    return jnp.pad(flat, (0, r * 128 - n)).reshape(r, 128)


def _unpack_rows(packed, shapes):
    flat = packed.reshape(-1)
    out, off = [], 0
    for s in shapes:
        n = 1
        for d in s:
            n *= d
        out.append(flat[off:off + n].reshape(s))
        off += n
    return out


def _cols_to_shards(w, cols):
    rows = w.shape[0]
    return w.reshape(rows, N_DEV, cols).transpose(1, 0, 2).reshape(N_DEV, rows * cols)


def kernel(x, c, w_ada, b_ada, norm1_g, w_in, q_norm_g, k_norm_g, b_f, conv_w, conv_b, conv_ln_g, conv_ln_b, beta_attn, beta_conv, w_out, norm2_g, w_ff1, w_ff2, loss_target, m_w_ada, m_b_ada, m_norm1_g, m_w_in, m_q_norm_g, m_k_norm_g, m_b_f, m_conv_w, m_conv_b, m_conv_ln_g, m_conv_ln_b, m_beta_attn, m_beta_conv, m_w_out, m_norm2_g, m_w_ff1, m_w_ff2, v_w_ada, v_b_ada, v_norm1_g, v_w_in, v_q_norm_g, v_k_norm_g, v_b_f, v_conv_w, v_conv_b, v_conv_ln_g, v_conv_ln_b, v_beta_attn, v_beta_conv, v_w_out, v_norm2_g, v_w_ff1, v_w_ff2):
    S = x.shape[1]
    B = BLK
    NB = S // B
    me = 4 * lax.axis_index("x") + 2 * lax.axis_index("y") + lax.axis_index("c")
    xs = x[0]
    tgt = loss_target[0]

    ADA_C, IN_C, FF_C, CV_C = w_ada.shape[2], w_in.shape[2], w_ff1.shape[2], conv_w.shape[2]
    OUT_R, FF_R = w_out.shape[1], w_ff2.shape[1]

    g_in, g_c, g_cw = gather_direct([w_in[0].astype(BF16), c, conv_w[0]], "ag_weights_early")
    late_w = [w_out[0].astype(BF16), w_ff1[0].astype(BF16), w_ff2[0].astype(BF16)]

    w_in_full = g_in.transpose(1, 0, 2).reshape(D, N_DEV * IN_C)
    c_all = g_c.reshape(N_DEV, D)
    conv_w_full = g_cw.transpose(1, 0, 2).reshape(KC, N_DEV * CV_C)
    conv_w_pad = jnp.pad(conv_w_full, ((0, 32 - KC), (0, 0)))

    w_qkvf = jnp.pad(w_in_full[:, 0:3 * A + H], ((0, 0), (0, ZT_ROWS - 3 * A - H)))
    wT = w_qkvf.T
    wr = w_in_full[:, 3 * A + H:]

    qg_col = q_norm_g.reshape(DH, 1)
    kg_col = k_norm_g.reshape(DH, 1)
    bf_col = b_f.reshape(H, 1)
    beta_a_col = beta_attn.reshape(A, 1)
    ii = lax.broadcasted_iota(jnp.int32, (B, B), 0)
    jj = lax.broadcasted_iota(jnp.int32, (B, B), 1)
    tri_up = (ii <= jj).astype(BF16)
    tri_lo = (ii >= jj).astype(BF16)

    modc = mod_columns(c_all, w_ada[0], lax.dynamic_slice(b_ada, (0, me * ADA_C), (1, ADA_C)))
    gm, = gather_direct([modc], "ag_mod")
    mod = lax.dynamic_index_in_dim(gm, me, axis=1, keepdims=False).reshape(1, N_DEV * ADA_C)
    sh1, sc1, g1, sh2, sc2, g2 = [mod[:, k * D:(k + 1) * D] for k in range(6)]
    qTa, kT, kaug, vT, vaug, zqk, fgT, alg, u0, stat = fwd_in(xs, norm1_g, sc1, sh1, wT, wr, qg_col,
                                                             kg_col, bf_col, tri_up)
    fmax = stat[:, :, 0].T
    fmin = stat[:, :, 1].T
    qk_max = jnp.sqrt(jnp.max(stat[:, :, 2], axis=0) * jnp.max(stat[:, :, 3], axis=0))
    thr = -(PRUNE + (2.02 / LOG2E) * qk_max)
    oT, lse, g_wo, w1_shards, g_w2 = attn_fwd(fmax, fmin, thr, qTa, kaug, vT, late_w)
    wo_full = g_wo.reshape(D, D)
    w2_full = g_w2.reshape(FF, D)
    x1, o, u1 = conv_merge_out(u0, xs, oT, conv_w_pad, conv_b, conv_ln_g, conv_ln_b, beta_conv,
                               beta_a_col, wo_full, g1)
    dy, loss_part, dg2, ra = mlp_fwd_loss(x1, tgt, norm2_g, sc2, sh2, g2, w1_shards, w2_full)

    dw1s, dw2, dh2p = mlp_bwd(x1, dy, ra, norm2_g, sc2, sh2, g2, w1_shards, w2_full)
    dx1, doTa, delta, du1, dwo, v1, v2, dba = merge_bwd(dh2p, x1, dy, o, oT, u1, norm2_g, sc2, g1,
                                                         conv_ln_g, conv_ln_b, beta_conv, beta_a_col, wo_full)
    early_g = [dwo.reshape(N_DEV, OUT_R, D), dw1s, dw2.reshape(N_DEV, FF_R, D)]
    dqT, dkT, dvT, dfk, r_out, r_f1, r_f2 = attn_bwd(fmax, fmin, thr, qTa, kaug, kT, vaug, doTa, lse,
                                                     delta, early_g)
    dzr, dzT, dcw, vc, dqg, dkg, dbf = head_bwd(du1, u0, alg, zqk, fgT, dqT, dkT, dvT, dfk, conv_w_pad,
                                                qg_col, kg_col, bf_col, tri_lo)
    dwT, dwr = in_bwd_w(xs, dzr, dzT, norm1_g, sc1, sh1)

    dw_in = jnp.concatenate([dwT.T[:, 0:3 * A + H], dwr], axis=1)
    late_g = _cols_to_shards(dw_in, IN_C)
    rows_b = -(-late_g.shape[1] // (128 * 256)) * 256
    late_g = jnp.pad(late_g, ((0, 0), (0, rows_b * 128 - late_g.shape[1]))).astype(BF16)
    grad_x, v0, recv_b = in_bwd_x(xs, dx1, dzr, dzT, norm1_g, sc1, sh1, w_qkvf, wr,
                                  late_g.reshape(N_DEV, rows_b, 128))

    dmod = jnp.concatenate([v0[0], v0[1], v1[3], v1[0], v1[1], dg2[0]])
    small1 = _pack_rows([dmod, v0[2], v1[2], dcw, vc[0], v2[1], v2[2], v2[0], dba, dqg, dkg,
                         jnp.pad(dbf.reshape(-1), (0, 120)), jnp.pad(loss_part.reshape(-1), (0, 127))])
    gs1, = gather_direct([small1], "ag_small_bwd")
    tot = sum_slots(gs1, "sum_small")
    (g_b_ada, g_n1, g_n2, g_cw, g_cb, g_lng, g_lnb, g_bc, g_ba, g_qg, g_kg, g_bf, loss_v) = _unpack_rows(
        tot, [(1, 6 * D), (1, D), (1, D), (32, CW), (1, CW), (1, CW), (1, CW), (1, CW), (1, A),
              (1, DH), (1, DH), (1, 128), (1, 128)])
    loss = loss_v[0, 0]
    g_bf = g_bf[:, 0:H]
    g_cw_mine = lax.dynamic_slice(g_cw[0:KC], (0, me * CV_C), (KC, CV_C)).reshape(1, KC, CV_C)

    small_names = [(b_ada, m_b_ada, v_b_ada, g_b_ada), (norm1_g, m_norm1_g, v_norm1_g, g_n1),
                   (q_norm_g, m_q_norm_g, v_q_norm_g, g_qg), (k_norm_g, m_k_norm_g, v_k_norm_g, g_kg),
                   (b_f, m_b_f, v_b_f, g_bf), (conv_w, m_conv_w, v_conv_w, g_cw_mine),
                   (conv_b, m_conv_b, v_conv_b, g_cb), (conv_ln_g, m_conv_ln_g, v_conv_ln_g, g_lng),
                   (conv_ln_b, m_conv_ln_b, v_conv_ln_b, g_lnb), (beta_attn, m_beta_attn, v_beta_attn, g_ba),
                   (beta_conv, m_beta_conv, v_beta_conv, g_bc), (norm2_g, m_norm2_g, v_norm2_g, g_n2)]
    shapes_s = [t[0].shape for t in small_names]
    pw, pm, pv, pg = [_pack_rows([t[k] for t in small_names]) for k in range(4)]
    sg_, sd_, sm_, sv_ = adamw_slots(pg[None], pw, pm, pv, "adamw_small")
    sgs, sds, sms, svs = [_unpack_rows(a, shapes_s) for a in (sg_, sd_, sm_, sv_)]

    dmod_all = gs1[:, 0:48, :].reshape(N_DEV, N_DEV, ADA_C)
    dmod_cols = lax.dynamic_index_in_dim(dmod_all, me, axis=1, keepdims=False)
    ga, da, ma_, va_ = ada_grad_adamw(c_all.reshape(N_DEV, D, 1), dmod_cols, w_ada[0], m_w_ada[0], v_w_ada[0])

    res_b = adamw_slots(recv_b, *[_pack_rows(ws, rows=rows_b) for ws in ([w_in], [m_w_in], [v_w_in])],
                        "adamw_in")
    late = []
    for nm, parts, (w_, m_, v_) in (("adamw_out", r_out, (w_out, m_w_out, v_w_out)),
                                    ("adamw_ff1", r_f1, (w_ff1, m_w_ff1, v_w_ff1)),
                                    ("adamw_ff2", r_f2, (w_ff2, m_w_ff2, v_w_ff2))):
        late.append([r[None] for r in adamw_slots(parts, w_[0], m_[0], v_[0], nm)])
    bgs, bds, bms, bvs = [_unpack_rows(res_b[q], [(1, D, IN_C)]) + [late[0][q], late[1][q], late[2][q]]
                          for q in range(4)]

    def assemble(small, ada, bigs):
        (b_ada_, n1_, qg_, kg_, bf_, cw_, cb_, lng_, lnb_, ba_, bc_, n2_) = small
        return [ada.reshape(1, D, ADA_C), b_ada_, n1_, bigs[0], qg_, kg_, bf_, cw_, cb_, lng_, lnb_, ba_, bc_,
                bigs[1], n2_, bigs[2], bigs[3]]

    return (loss, grad_x.reshape(1, S, D), *assemble(sgs, ga, bgs), *assemble(sds, da, bds),
            *assemble(sms, ma_, bms), *assemble(svs, va_, bvs))
```

```python
import functools

import jax
import jax.numpy as jnp
from jax import lax
from jax.experimental import pallas as pl
from jax.experimental.pallas import tpu as pltpu

F32 = jnp.float32
BF16 = jnp.bfloat16

D = 1024
A = 512
CW = 512
H = 8
DH = 64
FF = 4096
KC = 31
HALO = 32
KR = 80
ZT_ROWS = 1552
FCH = 1024
EPS = 1e-6
BLK = 512
N_DEV = 8
VMEM_LIMIT = 56 * 1024 * 1024

ADAM_LR = 0.001
ADAM_B1 = 0.9
ADAM_B2 = 0.999
ADAM_EPS = 1e-08
ADAM_WD = 0.01
ADAM_STEP = 10

MESH = pl.DeviceIdType.MESH
ANY = pl.BlockSpec(memory_space=pl.ANY)
SMEM = pl.BlockSpec(memory_space=pltpu.SMEM)
PRUNE = 105.0
LOG2E = 1.4426950408889634


def _pc(body, **kw):
    return pl.pallas_call(body, **kw)


def _cp(sem=None):
    return pltpu.CompilerParams(dimension_semantics=sem, vmem_limit_bytes=VMEM_LIMIT)


def _sds(shape, dtype=F32):
    return jax.ShapeDtypeStruct(shape, dtype)


def _const(shape):
    n = len(shape)
    return pl.BlockSpec(shape, lambda *a: (0,) * n)


def _nt(a, b):
    return lax.dot_general(a, b, (((1,), (1,)), ((), ())), preferred_element_type=F32)


def _nn(a, b):
    return jnp.dot(a, b, preferred_element_type=F32)


def _sigmoid(v):
    return 1.0 / (1.0 + jnp.exp(-v))


def _split3(v):
    a1 = v.astype(BF16)
    r1 = v - a1.astype(F32)
    a2 = r1.astype(BF16)
    a3 = (r1 - a2.astype(F32)).astype(BF16)
    return a1, a2, a3


def _rowmean(v):
    return jnp.mean(v, axis=-1, keepdims=True)


def _colsum(v):
    return jnp.sum(v, axis=0, keepdims=True)


def _lanesum(v):
    return jnp.sum(v, axis=-1, keepdims=True)


def _coords():
    return lax.axis_index("x"), lax.axis_index("y"), lax.axis_index("c")


def _direct_copies(scatter, src_ref, dst_ref, send_sems, recv_sems, local_sem):
    mx, my, mc = _coords()
    me = 4 * mx + 2 * my + mc
    if scatter:
        local = pltpu.make_async_copy(src_ref.at[me], dst_ref.at[0], local_sem)
    else:
        local = pltpu.make_async_copy(src_ref, dst_ref.at[me], local_sem)
    remote = []
    for r in range(1, N_DEV):
        px = 1 - mx if r & 4 else mx
        py = 1 - my if r & 2 else my
        pcc = 1 - mc if r & 1 else mc
        remote.append(pltpu.make_async_remote_copy(
            src_ref=src_ref.at[4 * px + 2 * py + pcc] if scatter else src_ref,
            dst_ref=dst_ref.at[r] if scatter else dst_ref.at[me],
            send_sem=send_sems.at[r - 1], recv_sem=recv_sems.at[r - 1],
            device_id=(px, py, pcc), device_id_type=MESH))
    return [local] + remote


def _start_all(copies):
    for cp in copies:
        cp.start()


def _wait_all(copies):
    for cp in copies[1:]:
        cp.wait()
    copies[0].wait()


COMM_SEMS = [pltpu.SemaphoreType.DMA((7,)), pltpu.SemaphoreType.DMA((7,)), pltpu.SemaphoreType.DMA(())]


def gather_direct(arrays, name):
    NG = len(arrays)

    def body(*refs):
        sems = refs[2 * NG:]
        sets = [_direct_copies(False, refs[q], refs[NG + q], *sems[3 * q:3 * q + 3]) for q in range(NG)]
        for copies in sets:
            _start_all(copies)
        for copies in sets:
            _wait_all(copies)

    return _pc(body, name=name, out_shape=[_sds((N_DEV,) + a.shape, a.dtype) for a in arrays],
               in_specs=[ANY] * NG, out_specs=[ANY] * NG, scratch_shapes=COMM_SEMS * NG)(*arrays)


def mod_columns(c_all, w_shard, b_cols):
    NCOL = w_shard.shape[1]

    def body(c_ref, w_ref, b_ref, o_ref):
        cv = c_ref[...]
        o_ref[...] = _nn((cv * _sigmoid(cv)).astype(BF16), w_ref[...].astype(BF16)) + b_ref[...]

    return _pc(body, name="mod_columns", out_shape=_sds((N_DEV, NCOL)),
               in_specs=[_const((N_DEV, D)), _const((D, NCOL)), _const((1, NCOL))],
               out_specs=_const((N_DEV, NCOL)), grid=(1,), compiler_params=_cp())(c_all, w_shard, b_cols)


def _norm_mod(xv, g, sc, sh):
    r = lax.rsqrt(_rowmean(xv * xv) + EPS)
    xh = xv * r
    return r, xh, xh * (g * (1.0 + sc)) + sh


def _log_sigmoid(v):
    e = jnp.exp(-jnp.abs(v))
    l1p = jnp.where(e < 1e-4, e * (1.0 - 0.5 * e), jnp.log(1.0 + e))
    return jnp.minimum(v, 0.0) - l1p


def fwd_in(x, n1g, sc1, sh1, wT, wr, qg_col, kg_col, bf_col, tri):
    S = x.shape[0]
    B = BLK
    NB = S // B

    def body(x_ref, g_ref, sc_ref, sh_ref, wT_ref, wr_ref, qg_ref, kg_ref, bf_ref, tri_ref,
             qTa_ref, kT_ref, kaug_ref, vT_ref, vaug_ref, zqk_ref, fgT_ref, alg_ref, u0_ref, stat_ref,
             carry, tbuf, fs, nq, nk):
        i = pl.program_id(0)

        @pl.when(i == 0)
        def _():
            carry[...] = jnp.zeros_like(carry)

        _, _, h = _norm_mod(x_ref[...], g_ref[...], sc_ref[...], sh_ref[...])
        hb = h.astype(BF16)
        zT = _nt(wT_ref[...], hb)
        zr = _nn(hb, wr_ref[...])
        zqk_ref[...] = zT[0:2 * A]
        fgT = zT[3 * A:3 * A + 8]
        fgT_ref[...] = fgT
        alg_ref[...] = zr
        u0_ref[...] = zr[:, 0:CW] * _sigmoid(zr[:, CW:2 * CW])

        logf = _log_sigmoid(fgT + bf_ref[...])
        a1, a2, a3 = _split3(logf)
        tr = tri_ref[...]
        F = _nn(a1, tr) + _nn(a2, tr) + _nn(a3, tr) + carry[...]
        carry[...] = carry[...] + _lanesum(logf)
        p1, p2, p3 = _split3(F * LOG2E)
        n1, n2, n3 = _split3(F * (-LOG2E))
        for k, v in enumerate((p1, p2, p3, n1, n2, n3)):
            fs[k] = v.astype(F32)

        rowi = lax.broadcasted_iota(jnp.int32, (8, B), 0)
        zeros_tail = jnp.zeros((128 - DH - 8, B), F32)
        ones_row = jnp.where(lax.broadcasted_iota(jnp.int32, (KR - DH, B), 0) == 0, 1.0, 0.0).astype(BF16)
        for hh in range(H):
            sl = slice(hh * DH, (hh + 1) * DH)
            q = zT[sl]
            k = zT[A + hh * DH:A + (hh + 1) * DH]
            v = zT[2 * A + hh * DH:2 * A + (hh + 1) * DH]
            qh = q * lax.rsqrt(jnp.mean(q * q, axis=0, keepdims=True) + EPS) * qg_ref[...] * (0.125 * LOG2E)
            kh = k * lax.rsqrt(jnp.mean(k * k, axis=0, keepdims=True) + EPS) * kg_ref[...]
            frow = [fs[kk, hh:hh + 1, :] for kk in range(6)]
            qx = jnp.where(rowi < 3, 1.0, jnp.where(rowi == 3, frow[0], jnp.where(
                rowi == 4, frow[1], jnp.where(rowi == 5, frow[2], 0.0))))
            kx = jnp.where(rowi == 0, frow[3], jnp.where(rowi == 1, frow[4], jnp.where(
                rowi == 2, frow[5], jnp.where(rowi < 6, 1.0, 0.0))))
            tbuf[0:DH, :] = qh
            tbuf[DH:DH + 8, :] = qx
            tbuf[DH + 8:128, :] = zeros_tail
            qTa_ref[0, hh * 128:(hh + 1) * 128, :] = tbuf[...].astype(BF16)
            tbuf[0:DH, :] = kh
            tbuf[DH:DH + 8, :] = kx
            kaug_ref[hh] = tbuf[...].T.astype(BF16)
            kT_ref[0, hh * KR:hh * KR + DH, :] = kh.astype(BF16)
            kT_ref[0, hh * KR + DH:(hh + 1) * KR, :] = ones_row
            tbuf[0:DH, :] = v
            tbuf[DH:DH + 8, :] = jnp.zeros((8, B), F32)
            vaug_ref[hh] = tbuf[...].T.astype(BF16)
            vT_ref[0, sl, :] = v.astype(BF16)
            nq[hh:hh + 1, :] = jnp.max(_colsum(qh * qh), axis=1, keepdims=True)
            nk[hh:hh + 1, :] = jnp.max(_colsum(kh * kh), axis=1, keepdims=True)

        lane = lax.broadcasted_iota(jnp.int32, (8, 128), 1)
        stat_ref[0] = jnp.where(lane == 0, jnp.max(F, axis=1, keepdims=True), jnp.where(
            lane == 1, jnp.min(F, axis=1, keepdims=True), jnp.where(
                lane == 2, nq[...], jnp.where(lane == 3, nk[...], 0.0))))

    row = lambda w: pl.BlockSpec((B, w), lambda i: (i, 0))
    tsp = lambda r: pl.BlockSpec((1, r, B), lambda i: (i, 0, 0))
    return _pc(
        body, name="fwd_in", grid=(NB,),
        in_specs=[row(D), _const((1, D)), _const((1, D)), _const((1, D)), _const((ZT_ROWS, D)),
                  _const((D, 2 * CW)), _const((DH, 1)), _const((DH, 1)), _const((8, 1)), _const((B, B))],
        out_specs=[tsp(H * 128), tsp(H * KR), pl.BlockSpec((H, B, 128), lambda i: (0, i, 0)), tsp(A),
                   pl.BlockSpec((H, B, 128), lambda i: (0, i, 0)),
                   pl.BlockSpec((2 * A, B), lambda i: (0, i)), pl.BlockSpec((8, B), lambda i: (0, i)),
                   row(2 * CW), row(CW), pl.BlockSpec((1, 8, 128), lambda i: (i, 0, 0))],
        out_shape=[_sds((NB, H * 128, B), BF16), _sds((NB, H * KR, B), BF16), _sds((H, S, 128), BF16),
                   _sds((NB, A, B), BF16), _sds((H, S, 128), BF16), _sds((2 * A, S)), _sds((8, S)),
                   _sds((S, 2 * CW)), _sds((S, CW)), _sds((NB, 8, 128))],
        scratch_shapes=[pltpu.VMEM((8, 1), F32), pltpu.VMEM((128, B), F32), pltpu.VMEM((6, 8, B), F32),
                        pltpu.VMEM((8, 1), F32), pltpu.VMEM((8, 1), F32)],
        compiler_params=_cp(("arbitrary",)),
    )(x, n1g, sc1, sh1, wT, wr, qg_col, kg_col, bf_col, tri)


def _first_key_block(top, fmin_ref, cut, h, i):
    return lax.while_loop(
        lambda j: jnp.logical_and(j > 0, top - fmin_ref[h, jnp.maximum(j - 1, 0)] >= cut),
        lambda j: j - 1, i)


def _causal_keep(B):
    return lax.broadcasted_iota(jnp.int32, (B, B), 0) <= lax.broadcasted_iota(jnp.int32, (B, B), 1)


def _keep_right_half(B):
    hb = B // 2
    return lax.broadcasted_iota(jnp.int32, (B, hb), 0) <= lax.broadcasted_iota(jnp.int32, (B, hb), 1) + hb


def _keep_top_half(B):
    hb = B // 2
    return lax.broadcasted_iota(jnp.int32, (hb, B), 0) <= lax.broadcasted_iota(jnp.int32, (hb, B), 1)


def attn_fwd(fmax, fmin, thr, qTa, kaug, vT, late_w):
    NB, _, B = qTa.shape
    S = NB * B
    NG = len(late_w)
    HB = B // 2

    def body(fmax_ref, fmin_ref, thr_ref, q_ref, k_ref, v_ref, *rest):
        w_refs = rest[0:NG]
        o_ref, lse_ref = rest[NG:NG + 2]
        gw_refs = rest[NG + 2:2 * NG + 2]
        s0, s1, m_ref, l_ref, acc_ref = rest[2 * NG + 2:2 * NG + 7]
        sems = rest[2 * NG + 7:]
        h = pl.program_id(0)
        i = pl.program_id(1)

        def gathers():
            return [_direct_copies(False, w_refs[q], gw_refs[q], *sems[3 * q:3 * q + 3]) for q in range(NG)]

        @pl.when(jnp.logical_and(h == 0, i == 0))
        def _():
            for copies in gathers():
                _start_all(copies)

        jlo = _first_key_block(fmax_ref[h, i], fmin_ref, thr_ref[h], h, i)
        n = i - jlo

        def scores(j, s_ref):
            s_ref[...] = _nn(k_ref[0, pl.ds(pl.multiple_of(j * B, B), B), :], q_ref[0])

        def softmax_cols(s, j, cols, rows):
            m = m_ref[:, cols]
            mn = jnp.maximum(m, jnp.max(s, axis=0, keepdims=True))
            a = jnp.exp2(m - mn)
            p = jnp.exp2(s - mn)
            m_ref[:, cols] = mn
            l_ref[:, cols] = a * l_ref[:, cols] + _colsum(p)
            acc_ref[:, cols] = a * acc_ref[:, cols] + _nn(v_ref[j][:, rows], p.astype(BF16))

        def softmax_step(s_ref, j, masked):
            if not masked:
                softmax_cols(s_ref[...], j, slice(None), slice(None))
                return
            lo, hi = slice(0, HB), slice(HB, B)
            softmax_cols(jnp.where(_causal_keep(HB), s_ref[lo, lo], -jnp.inf), j, lo, lo)
            softmax_cols(jnp.where(_keep_right_half(B), s_ref[:, hi], -jnp.inf), j, hi, slice(None))

        m_ref[...] = jnp.full((1, B), -jnp.inf, F32)
        l_ref[...] = jnp.zeros((1, B), F32)
        acc_ref[...] = jnp.zeros((DH, B), F32)
        scores(jlo, s0)

        def pair(t, carry):
            j = jlo + 2 * t
            scores(j + 1, s1)
            softmax_step(s0, j, False)
            scores(j + 2, s0)
            softmax_step(s1, j + 1, False)
            return carry

        lax.fori_loop(0, n // 2, pair, 0)

        @pl.when(n % 2 == 1)
        def _():
            scores(i, s1)
            softmax_step(s0, i - 1, False)
            softmax_step(s1, i, True)

        @pl.when(n % 2 == 0)
        def _():
            softmax_step(s0, i, True)

        l = l_ref[...]
        o_ref[0] = acc_ref[...] / l
        lse_ref[0, 0] = m_ref[...] + jnp.log2(l)

        @pl.when(jnp.logical_and(h == H - 1, i == NB - 1))
        def _():
            for copies in gathers():
                _wait_all(copies)

    return _pc(
        body, name="attn_fwd", grid=(H, NB),
        in_specs=[SMEM, SMEM, SMEM, pl.BlockSpec((1, 128, B), lambda h, i: (i, h, 0)),
                  pl.BlockSpec((1, S, 128), lambda h, i: (h, 0, 0)),
                  pl.BlockSpec((NB, DH, B), lambda h, i: (0, h, 0))] + [ANY] * NG,
        out_specs=[pl.BlockSpec((1, DH, B), lambda h, i: (i, h, 0)),
                   pl.BlockSpec((1, 1, 1, B), lambda h, i: (h, i, 0, 0))] + [ANY] * NG,
        out_shape=[_sds((NB, A, B)), _sds((H, NB, 1, B))]
        + [_sds((N_DEV,) + w.shape, w.dtype) for w in late_w],
        scratch_shapes=[pltpu.VMEM((B, B), F32), pltpu.VMEM((B, B), F32), pltpu.VMEM((1, B), F32),
                        pltpu.VMEM((1, B), F32), pltpu.VMEM((DH, B), F32)] + COMM_SEMS * NG,
        compiler_params=_cp(("arbitrary", "arbitrary")),
    )(fmax, fmin, thr, qTa, kaug, vT, *late_w)


def attn_bwd(fmax, fmin, thr, qTa, kaug, kT, vaug, doTa, lse, delta, early_g):
    NG = len(early_g)
    NB, _, B = qTa.shape
    QR = 80
    CH = min(256, B)

    def body(fmax_ref, fmin_ref, thr_ref, q_ref, ka_ref, kt_ref, va_ref, do_ref, lse_ref, dl_ref, *rest):
        g_refs = rest[0:NG]
        dq_ref, dk_ref, dv_ref, dfk_ref = rest[NG:NG + 4]
        recv_refs = rest[NG + 4:2 * NG + 4]
        s0, d0, s1, d1, p0, e0, p1, e1, dk_acc, dv_acc = rest[2 * NG + 4:2 * NG + 14]
        sems = rest[2 * NG + 14:]
        h = pl.program_id(0)
        j = pl.program_id(1)

        def exchanges():
            return [_direct_copies(True, g_refs[q], recv_refs[q], *sems[3 * q:3 * q + 3]) for q in range(NG)]

        @pl.when(jnp.logical_and(h == 0, j == 0))
        def _():
            for copies in exchanges():
                _start_all(copies)

        bottom = fmin_ref[h, j]
        cut = thr_ref[h]
        ihi = lax.while_loop(
            lambda i: jnp.logical_and(i < NB - 1, fmax_ref[h, jnp.minimum(i + 1, NB - 1)] - bottom >= cut),
            lambda i: i + 1, j)

        @pl.when(j == 0)
        def _():
            dq_ref[...] = jnp.zeros_like(dq_ref)

        dk_acc[...] = jnp.zeros_like(dk_acc)
        dv_acc[...] = jnp.zeros_like(dv_acc)

        n = ihi - j + 1

        def blk(k):
            return jnp.minimum(j + k, ihi)

        def products(k, s_ref, d_ref):
            i = blk(k)
            s_ref[...] = _nn(ka_ref[0], q_ref[i])
            d_ref[...] = _nn(va_ref[0], do_ref[i])

        def elementwise(k, s_ref, d_ref, p_ref, ds_ref, masked):
            i = blk(k)
            p = jnp.exp2(s_ref[...] - lse_ref[0, i])
            if masked:
                p = jnp.where(_causal_keep(B), p, 0.0)
            p_ref[...] = p.astype(BF16)
            ds_ref[...] = (p * (d_ref[...] - dl_ref[0, i])).astype(BF16)

        def grads(k, p_ref, ds_ref):
            i = j + k
            dsb = ds_ref[...]
            dv_acc[...] = dv_acc[...] + _nt(do_ref[i][0:DH], p_ref[...])
            dk_acc[...] = dk_acc[...] + _nt(q_ref[i][0:QR], dsb)
            dq_ref[i] = dq_ref[i] + _nn(kt_ref[0], dsb)

        def trip(kp, s_p, d_p, ke, s_e, d_e, p_e, e_e, kg, p_g, e_g):
            ip, ie, ig = blk(kp), blk(ke), j + kg
            lse_e = lse_ref[0, ie]
            dl_e = dl_ref[0, ie]
            dq_part = jnp.zeros((KR, B), F32)
            for c in range(B // CH):
                rows = slice(c * CH, (c + 1) * CH)
                s_p[rows, :] = _nn(ka_ref[0, rows, :], q_ref[ip])
                d_p[rows, :] = _nn(va_ref[0, rows, :], do_ref[ip])
                p = jnp.exp2(s_e[rows, :] - lse_e)
                p_e[rows, :] = p.astype(BF16)
                e_e[rows, :] = (p * (d_e[rows, :] - dl_e)).astype(BF16)
                dsb = e_g[rows, :]
                dv_acc[:, rows] = dv_acc[:, rows] + _nt(do_ref[ig][0:DH], p_g[rows, :])
                dk_acc[:, rows] = dk_acc[:, rows] + _nt(q_ref[ig][0:QR], dsb)
                dq_part = dq_part + _nn(kt_ref[0, :, rows], dsb)
            dq_ref[ig] = dq_ref[ig] + dq_part

        HB = B // 2
        lo, hi = slice(0, HB), slice(HB, B)

        def diagonal_products(s_ref, d_ref):
            s_ref[lo, :] = _nn(ka_ref[0, lo, :], q_ref[j])
            d_ref[lo, :] = _nn(va_ref[0, lo, :], do_ref[j])
            s_ref[hi, hi] = _nn(ka_ref[0, hi, :], q_ref[j, :, hi])
            d_ref[hi, hi] = _nn(va_ref[0, hi, :], do_ref[j, :, hi])

        def diagonal_elementwise(s_ref, d_ref, p_ref, ds_ref):
            lse0 = lse_ref[0, j]
            dl0 = dl_ref[0, j]
            p = jnp.where(_keep_top_half(B), jnp.exp2(s_ref[lo, :] - lse0), 0.0)
            p_ref[lo, :] = p.astype(BF16)
            ds_ref[lo, :] = (p * (d_ref[lo, :] - dl0)).astype(BF16)
            p = jnp.where(_causal_keep(HB), jnp.exp2(s_ref[hi, hi] - lse0[:, hi]), 0.0)
            p_ref[hi, hi] = p.astype(BF16)
            ds_ref[hi, hi] = (p * (d_ref[hi, hi] - dl0[:, hi])).astype(BF16)
            p_ref[hi, lo] = jnp.zeros((HB, HB), BF16)
            ds_ref[hi, lo] = jnp.zeros((HB, HB), BF16)

        diagonal_products(s0, d0)
        products(1, s1, d1)
        diagonal_elementwise(s0, d0, p0, e0)

        def pair(t, carry):
            k = 2 * t
            trip(k + 2, s0, d0, k + 1, s1, d1, p1, e1, k, p0, e0)
            trip(k + 3, s1, d1, k + 2, s0, d0, p0, e0, k + 1, p1, e1)
            return carry

        n_pairs = n // 2
        lax.fori_loop(0, n_pairs - 1, pair, 0)
        k_last = 2 * (n_pairs - 1)

        @pl.when(jnp.logical_and(n_pairs >= 1, n % 2 == 0))
        def _():
            elementwise(k_last + 1, s1, d1, p1, e1, False)
            grads(k_last, p0, e0)
            grads(k_last + 1, p1, e1)

        @pl.when(jnp.logical_and(n_pairs >= 1, n % 2 == 1))
        def _():
            trip(k_last + 2, s0, d0, k_last + 1, s1, d1, p1, e1, k_last, p0, e0)
            elementwise(k_last + 2, s0, d0, p0, e0, False)
            grads(k_last + 1, p1, e1)
            grads(k_last + 2, p0, e0)

        @pl.when(n == 1)
        def _():
            grads(0, p0, e0)

        dk_ref[0] = dk_acc[0:DH, :]
        dfk_ref[0, 0] = dk_acc[DH:DH + 8, :]
        dv_ref[0] = dv_acc[...]

        @pl.when(jnp.logical_and(h == H - 1, j == NB - 1))
        def _():
            for copies in exchanges():
                _wait_all(copies)

    per_kv = lambda r: pl.BlockSpec((1, r, B), lambda h, j: (j, h, 0))
    head_all = lambda r: pl.BlockSpec((NB, r, B), lambda h, j: (0, h, 0))
    aug = pl.BlockSpec((1, B, 128), lambda h, j: (h, j, 0))
    stat = pl.BlockSpec((1, NB, 1, B), lambda h, j: (h, 0, 0, 0))
    return _pc(
        body, name="attn_bwd", grid=(H, NB),
        in_specs=[SMEM, SMEM, SMEM, head_all(128), aug, per_kv(KR), aug, head_all(128), stat, stat]
        + [ANY] * NG,
        out_specs=[head_all(KR), per_kv(DH), per_kv(DH),
                   pl.BlockSpec((1, 1, 8, B), lambda h, j: (h, j, 0, 0))] + [ANY] * NG,
        out_shape=[_sds((NB, H * KR, B)), _sds((NB, A, B)), _sds((NB, A, B)), _sds((H, NB, 8, B))]
        + [_sds(g.shape, g.dtype) for g in early_g],
        scratch_shapes=[pltpu.VMEM((B, B), F32)] * 4 + [pltpu.VMEM((B, B), BF16)] * 4
        + [pltpu.VMEM((QR, B), F32), pltpu.VMEM((DH, B), F32)] + COMM_SEMS * NG,
        compiler_params=_cp(("arbitrary", "arbitrary")),
    )(fmax, fmin, thr, qTa, kaug, kT, vaug, doTa, lse, delta, *early_g)


def _conv_tail(u1, lng, lnb, beta_c):
    mu = _rowmean(u1)
    d = u1 - mu
    rstd = lax.rsqrt(_rowmean(d * d) + EPS)
    xhat = d * rstd
    u2 = xhat * lng + lnb
    sg = _sigmoid(u2)
    u3 = u2 * sg
    rc = lax.rsqrt(_rowmean(u3 * u3) + EPS)
    n3 = u3 * rc
    return rstd, xhat, u2, sg, rc, n3, n3 * beta_c


def _attn_tail(oT, beta_a_col):
    ra = lax.rsqrt(jnp.mean(oT * oT, axis=0, keepdims=True) + EPS)
    ohat = oT * ra
    return ra, ohat, ohat * beta_a_col


CONV_ROWS = 32
CORR_ROWS = 16


def _fill_shifted(buf, sh):
    rows = sh.shape[1]
    for ph in range(8):
        sh[ph] = buf[pl.ds(ph, rows), :]


def _tap_rows(sh, r0, o, rows):
    return sh[o % 8, pl.ds(pl.multiple_of(r0 + 8 * (o // 8), 8), rows), :]


def _depthwise(sh, w_ref, offs, bias, out_ref, B):
    def chunk(ci, carry):
        r0 = pl.multiple_of(ci * CONV_ROWS, CONV_ROWS)
        acc = jnp.broadcast_to(bias, (CONV_ROWS, CW))
        for k, o in enumerate(offs):
            acc = acc + w_ref[k:k + 1, :] * _tap_rows(sh, r0, o, CONV_ROWS)
        out_ref[pl.ds(r0, CONV_ROWS), :] = acc
        return carry

    lax.fori_loop(0, B // CONV_ROWS, chunk, 0)


def _tap_gradients(sh, d_ref, offs, dw_ref, B):
    for g0 in range(0, len(offs), 8):
        ks = list(range(g0, min(g0 + 8, len(offs))))

        def chunk(ci, accs, ks=ks):
            r0 = pl.multiple_of(ci * CORR_ROWS, CORR_ROWS)
            d = d_ref[pl.ds(r0, CORR_ROWS), :]
            out = []
            for a, k in zip(accs, ks):
                pr = d * _tap_rows(sh, r0, offs[k], CORR_ROWS)
                out.append(a + pr[0:8] + pr[8:16])
            return tuple(out)

        accs = lax.fori_loop(0, B // CORR_ROWS, chunk, tuple(jnp.zeros((8, CW), F32) for _ in ks))
        for a, k in zip(accs, ks):
            dw_ref[k:k + 1, :] = dw_ref[k:k + 1, :] + _colsum(a)


def conv_merge_out(u0, x, oT, conv_w, conv_b, lng, lnb, beta_c, beta_a_col, wo, g1):
    S = x.shape[0]
    B = BLK
    NB = S // B

    def body(uc_ref, up_ref, x_ref, oT_ref, w_ref, cb_ref, lng_ref, lnb_ref, bc_ref, ba_ref, wo_ref,
             g1_ref, x1_ref, o_ref, u1_ref, ubuf, sh):
        i = pl.program_id(0)
        ubuf[0:HALO, :] = jnp.where(i > 0, up_ref[B - HALO:B, :], 0.0)
        ubuf[HALO:HALO + B, :] = uc_ref[...]
        ubuf[HALO + B:HALO + B + 8, :] = jnp.zeros((8, CW), F32)
        _fill_shifted(ubuf, sh)
        _depthwise(sh, w_ref, [HALO - (KC - 1) + k for k in range(KC)], cb_ref[...], u1_ref, B)
        mc = _conv_tail(u1_ref[...], lng_ref[...], lnb_ref[...], bc_ref[...])[-1]
        maT = _attn_tail(oT_ref[0], ba_ref[...])[-1]
        ma = maT.T
        o = _nn(ma.astype(BF16), wo_ref[0:A, :]) + _nn(mc.astype(BF16), wo_ref[A:D, :])
        o_ref[...] = o
        x1_ref[...] = x_ref[...] + g1_ref[...] * o

    row = lambda w: pl.BlockSpec((B, w), lambda i: (i, 0))
    return _pc(
        body, name="conv_merge_out", grid=(NB,),
        in_specs=[row(CW), pl.BlockSpec((B, CW), lambda i: (jnp.maximum(i - 1, 0), 0)), row(D),
                  pl.BlockSpec((1, A, B), lambda i: (i, 0, 0)), _const((32, CW)), _const((1, CW)),
                  _const((1, CW)), _const((1, CW)), _const((1, CW)), _const((A, 1)), _const((D, D)),
                  _const((1, D))],
        out_specs=[row(D), row(D), row(CW)],
        out_shape=[_sds((S, D)), _sds((S, D)), _sds((S, CW))],
        scratch_shapes=[pltpu.VMEM((B + HALO + 8, CW), F32), pltpu.VMEM((8, B + HALO, CW), F32)],
        compiler_params=_cp(("arbitrary",)),
    )(u0, u0, x, oT, conv_w, conv_b, lng, lnb, beta_c, beta_a_col, wo, g1)


def mlp_fwd_loss(x1, tgt, n2g, sc2, sh2, g2, w1s, w2):
    S = x1.shape[0]
    B = BLK
    NB = S // B
    NS, _, FS = w1s.shape

    def body(x1_ref, t_ref, g_ref, sc_ref, sh_ref, g2_ref, w1_hbm, w2_hbm, dy_ref, loss_ref, dg2_ref, ra_ref,
             w1_v, w2_v, sem):
        i = pl.program_id(0)

        @pl.when(i == 0)
        def _():
            for src, dst in ((w1_hbm, w1_v), (w2_hbm, w2_v)):
                cp = pltpu.make_async_copy(src, dst, sem)
                cp.start()
                cp.wait()
            loss_ref[...] = jnp.zeros_like(loss_ref)
            dg2_ref[...] = jnp.zeros_like(dg2_ref)

        x1 = x1_ref[...]
        hs = _norm_mod(x1, g_ref[...], sc_ref[...], sh_ref[...])[2].astype(BF16)
        y2 = jnp.zeros((B, D), F32)
        for q in range(NS):
            a = jnp.maximum(_nn(hs, w1_v[q]), 0.0)
            ra_ref[:, q * FS:(q + 1) * FS] = a.astype(BF16)
            y2 = y2 + _nn((a * a).astype(BF16), w2_v[q * FS:(q + 1) * FS, :])
        e = x1 + g2_ref[...] * y2 - t_ref[...]
        dy = e * (1.0 / D)
        dy_ref[...] = dy
        loss_ref[...] = loss_ref[...] + 0.5 * _colsum(_rowmean(e * e))
        dg2_ref[...] = dg2_ref[...] + _colsum(dy * y2)

    row = pl.BlockSpec((B, D), lambda i: (i, 0))
    return _pc(
        body, name="mlp_fwd_loss", grid=(NB,),
        in_specs=[row, row, _const((1, D)), _const((1, D)), _const((1, D)), _const((1, D)), ANY, ANY],
        out_specs=[row, _const((1, 1)), _const((1, D)), pl.BlockSpec((B, FF), lambda i: (i, 0))],
        out_shape=[_sds((S, D)), _sds((1, 1)), _sds((1, D)), _sds((S, FF), BF16)],
        scratch_shapes=[pltpu.VMEM(w1s.shape, BF16), pltpu.VMEM(w2.shape, BF16), pltpu.SemaphoreType.DMA(())],
        compiler_params=_cp(("arbitrary",)),
    )(x1, tgt, n2g, sc2, sh2, g2, w1s, w2)


def mlp_bwd(x1, dy, ra, n2g, sc2, sh2, g2, w1s, w2):
    S = x1.shape[0]
    B = BLK
    NB = S // B
    NC = FF // FCH
    NS, _, FS = w1s.shape
    SPC = NS // NC

    def body(x1_ref, dy_ref, ra_ref, g_ref, sc_ref, sh_ref, g2_ref, w1_ref, w2_ref, dw1_ref, dw2_ref, dh_ref):
        i = pl.program_id(1)

        @pl.when(i == 0)
        def _():
            dw1_ref[...] = jnp.zeros_like(dw1_ref)
            dw2_ref[...] = jnp.zeros_like(dw2_ref)

        h2 = _norm_mod(x1_ref[...], g_ref[...], sc_ref[...], sh_ref[...])[2]
        h2t = h2.T.astype(BF16)
        ra = ra_ref[...].astype(F32)
        dyb = (dy_ref[...] * g2_ref[...]).astype(BF16)
        db = _nt(dyb, w2_ref[...])
        dab = (db * (2.0 * ra)).astype(BF16)
        dh = jnp.zeros((B, D), F32)
        for q in range(SPC):
            part = dab[:, q * FS:(q + 1) * FS]
            dh = dh + _nt(part, w1_ref[q])
            dw1_ref[q] = dw1_ref[q] + _nn(h2t, part)
        dh_ref[0] = dh
        dw2_ref[...] = dw2_ref[...] + _nn((ra * ra).T.astype(BF16), dyb)

    row = pl.BlockSpec((B, D), lambda c, i: (i, 0))
    vec = pl.BlockSpec((1, D), lambda c, i: (0, 0))
    wsh = pl.BlockSpec((SPC, D, FS), lambda c, i: (c, 0, 0))
    w2b = pl.BlockSpec((FCH, D), lambda c, i: (c, 0))
    return _pc(
        body, name="mlp_bwd", grid=(NC, NB),
        in_specs=[row, row, pl.BlockSpec((B, FCH), lambda c, i: (i, c)), vec, vec, vec, vec, wsh, w2b],
        out_specs=[wsh, w2b, pl.BlockSpec((1, B, D), lambda c, i: (c, i, 0))],
        out_shape=[_sds(w1s.shape), _sds((FF, D)), _sds((NC, S, D))],
        compiler_params=_cp(("arbitrary", "arbitrary")),
    )(x1, dy, ra, n2g, sc2, sh2, g2, w1s, w2)


def merge_bwd(dh2p, x1, dy, o, oT, u1, n2g, sc2, g1, lng, lnb, beta_c, beta_a_col, wo):
    S = x1.shape[0]
    B = BLK
    NB = S // B
    NC = dh2p.shape[0]

    def body(dh_ref, x1_ref, dy_ref, o_ref, oT_ref, u1_ref, g_ref, sc_ref, g1_ref, lng_ref, lnb_ref,
             bc_ref, ba_ref, wo_ref,
             dx1_ref, doTa_ref, dl_ref, du1_ref, dwo_ref, v1_ref, v2_ref, dba_ref):
        i = pl.program_id(0)

        @pl.when(i == 0)
        def _():
            dwo_ref[...] = jnp.zeros_like(dwo_ref)
            v1_ref[...] = jnp.zeros_like(v1_ref)
            v2_ref[...] = jnp.zeros_like(v2_ref)
            dba_ref[...] = jnp.zeros_like(dba_ref)

        dh2 = dh_ref[0]
        for cc in range(1, NC):
            dh2 = dh2 + dh_ref[cc]
        x1 = x1_ref[...]
        g = g_ref[...]
        sc = sc_ref[...]
        r2 = lax.rsqrt(_rowmean(x1 * x1) + EPS)
        xh = x1 * r2
        dhx = dh2 * xh
        v1_ref[0:1, :] = v1_ref[0:1, :] + _colsum(dh2)
        v1_ref[1:2, :] = v1_ref[1:2, :] + _colsum(dhx) * g
        v1_ref[2:3, :] = v1_ref[2:3, :] + _colsum(dhx) * (1.0 + sc)
        dxh = dh2 * (g * (1.0 + sc))
        dx1 = dy_ref[...] + r2 * (dxh - xh * _rowmean(dxh * xh))
        dx1_ref[...] = dx1
        v1_ref[3:4, :] = v1_ref[3:4, :] + _colsum(dx1 * o_ref[...])
        dob = (dx1 * g1_ref[...]).astype(BF16)

        lng = lng_ref[...]
        bc = bc_ref[...]
        rstd, xhat, u2, sg, rc, n3, mc = _conv_tail(u1_ref[...], lng, lnb_ref[...], bc)
        ba = ba_ref[...]
        oT = oT_ref[0]
        ra, ohat, maT = _attn_tail(oT, ba)
        dwo_ref[0:A, :] = dwo_ref[0:A, :] + _nn(maT.astype(BF16), dob)
        dwo_ref[A:D, :] = dwo_ref[A:D, :] + _nn(mc.T.astype(BF16), dob)
        dmaT = _nt(wo_ref[0:A, :], dob)
        dmc = _nt(dob, wo_ref[A:D, :])

        dba_ref[...] = dba_ref[...] + _lanesum(dmaT * ohat)
        dohat = dmaT * ba
        doT = ra * (dohat - ohat * jnp.mean(dohat * ohat, axis=0, keepdims=True))
        prod = doT * oT
        zpad = jnp.zeros((128 - DH, B), BF16)
        for hh in range(H):
            sl = slice(hh * DH, (hh + 1) * DH)
            dl_ref[hh, 0] = _colsum(prod[sl])
            doTa_ref[0, hh * 128:hh * 128 + DH, :] = doT[sl].astype(BF16)
            doTa_ref[0, hh * 128 + DH:(hh + 1) * 128, :] = zpad

        v2_ref[0:1, :] = v2_ref[0:1, :] + _colsum(dmc * n3)
        dn3 = dmc * bc
        du3 = rc * (dn3 - n3 * _rowmean(dn3 * n3))
        du2 = du3 * (sg * (1.0 + u2 * (1.0 - sg)))
        v2_ref[1:2, :] = v2_ref[1:2, :] + _colsum(du2 * xhat)
        v2_ref[2:3, :] = v2_ref[2:3, :] + _colsum(du2)
        dxhat = du2 * lng
        du1_ref[...] = rstd * (dxhat - _rowmean(dxhat) - xhat * _rowmean(dxhat * xhat))

    row = lambda w: pl.BlockSpec((B, w), lambda i: (i, 0))
    return _pc(
        body, name="merge_bwd", grid=(NB,),
        in_specs=[pl.BlockSpec((NC, B, D), lambda i: (0, i, 0)), row(D), row(D), row(D),
                  pl.BlockSpec((1, A, B), lambda i: (i, 0, 0)), row(CW), _const((1, D)), _const((1, D)),
                  _const((1, D)), _const((1, CW)), _const((1, CW)), _const((1, CW)), _const((A, 1)),
                  _const((D, D))],
        out_specs=[row(D), pl.BlockSpec((1, H * 128, B), lambda i: (i, 0, 0)),
                   pl.BlockSpec((H, 1, 1, B), lambda i: (0, i, 0, 0)), row(CW), _const((D, D)),
                   _const((8, D)), _const((8, CW)), _const((A, 1))],
        out_shape=[_sds((S, D)), _sds((NB, H * 128, B), BF16), _sds((H, NB, 1, B)), _sds((S, CW)),
                   _sds((D, D)), _sds((8, D)), _sds((8, CW)), _sds((A, 1))],
        compiler_params=_cp(("arbitrary",)),
    )(dh2p, x1, dy, o, oT, u1, n2g, sc2, g1, lng, lnb, beta_c, beta_a_col, wo)


def head_bwd(du1, u0, alg, zqk, fgT, dqT, dkT, dvT, dfk, conv_w, qg_col, kg_col, bf_col, tri_lo):
    S = u0.shape[0]
    B = BLK
    NB = S // B

    def body(dc_ref, dn_ref, uc_ref, up_ref, alg_ref, zqk_ref, fg_ref, dq_ref, dk_ref, dv_ref, dfk_ref,
             w_ref, qg_ref, kg_ref, bf_ref, tri_ref,
             dzr_ref, dzT_ref, dcw_ref, vc_ref, dqg_ref, dkg_ref, dbf_ref,
             buf, sh, du0_ref, carry, fbuf):
        pid = pl.program_id(0)
        ri = NB - 1 - pid

        @pl.when(pid == 0)
        def _():
            carry[...] = jnp.zeros_like(carry)
            dcw_ref[...] = jnp.zeros_like(dcw_ref)
            vc_ref[...] = jnp.zeros_like(vc_ref)
            dqg_ref[...] = jnp.zeros_like(dqg_ref)
            dkg_ref[...] = jnp.zeros_like(dkg_ref)
            dbf_ref[...] = jnp.zeros_like(dbf_ref)

        zero8 = jnp.zeros((8, CW), F32)
        buf[0:B, :] = dc_ref[...]
        buf[B:B + HALO, :] = jnp.where(ri < NB - 1, dn_ref[0:HALO, :], 0.0)
        buf[B + HALO:B + HALO + 8, :] = zero8
        _fill_shifted(buf, sh)
        _depthwise(sh, w_ref, [KC - 1 - k for k in range(KC)], jnp.zeros((1, CW), F32), du0_ref, B)
        buf[0:HALO, :] = jnp.where(ri > 0, up_ref[B - HALO:B, :], 0.0)
        buf[HALO:HALO + B, :] = uc_ref[...]
        buf[HALO + B:HALO + B + 8, :] = zero8
        _fill_shifted(buf, sh)
        _tap_gradients(sh, dc_ref, [HALO - (KC - 1) + k for k in range(KC)], dcw_ref, B)
        vc_ref[0:1, :] = vc_ref[0:1, :] + _colsum(dc_ref[...])

        du0 = du0_ref[...]
        al = alg_ref[:, 0:CW]
        sg = _sigmoid(alg_ref[:, CW:2 * CW])
        dzr_ref[:, 0:CW] = (du0 * sg).astype(BF16)
        dzr_ref[:, CW:2 * CW] = (du0 * al * sg * (1.0 - sg)).astype(BF16)

        dqg = jnp.zeros((DH, B), F32)
        dkg = jnp.zeros((DH, B), F32)
        qg = qg_ref[...]
        kg = kg_ref[...]
        for hh in range(H):
            sl = slice(hh * DH, (hh + 1) * DH)
            q = zqk_ref[sl, :]
            rq = lax.rsqrt(jnp.mean(q * q, axis=0, keepdims=True) + EPS)
            qn = q * rq
            dqh = dq_ref[0, hh * KR:hh * KR + DH, :] * 0.125
            dqg = dqg + dqh * qn
            dqn = dqh * qg
            dzT_ref[0, sl, :] = (rq * (dqn - qn * jnp.mean(dqn * qn, axis=0, keepdims=True))).astype(BF16)
            k = zqk_ref[A + hh * DH:A + (hh + 1) * DH, :]
            rk = lax.rsqrt(jnp.mean(k * k, axis=0, keepdims=True) + EPS)
            kn = k * rk
            dkh = dk_ref[0, sl, :] * (1.0 / LOG2E)
            dkg = dkg + dkh * kn
            dkn = dkh * kg
            dzT_ref[0, A + hh * DH:A + (hh + 1) * DH, :] = (
                rk * (dkn - kn * jnp.mean(dkn * kn, axis=0, keepdims=True))).astype(BF16)
            fbuf[hh:hh + 1, :] = dq_ref[0, hh * KR + DH:hh * KR + DH + 1, :] - dfk_ref[hh, 0, 0:1, :]
        dqg_ref[...] = dqg_ref[...] + _lanesum(dqg)
        dkg_ref[...] = dkg_ref[...] + _lanesum(dkg)
        dzT_ref[0, 2 * A:3 * A, :] = dv_ref[0].astype(BF16)

        dF = fbuf[...]
        a1, a2, a3 = _split3(dF)
        tr = tri_ref[...]
        dlogf = _nn(a1, tr) + _nn(a2, tr) + _nn(a3, tr) + carry[...]
        carry[...] = carry[...] + _lanesum(dF)
        dfg = dlogf * _sigmoid(-(fg_ref[...] + bf_ref[...]))
        dbf_ref[...] = dbf_ref[...] + _lanesum(dfg)
        dzT_ref[0, 3 * A:ZT_ROWS, :] = jnp.concatenate([dfg, jnp.zeros((8, B), F32)], axis=0).astype(BF16)

    rrow = lambda w: pl.BlockSpec((B, w), lambda p: (NB - 1 - p, 0))
    rts = lambda r: pl.BlockSpec((1, r, B), lambda p: (NB - 1 - p, 0, 0))
    return _pc(
        body, name="head_bwd", grid=(NB,),
        in_specs=[rrow(CW), pl.BlockSpec((B, CW), lambda p: (jnp.minimum(NB - p, NB - 1), 0)),
                  rrow(CW), pl.BlockSpec((B, CW), lambda p: (jnp.maximum(NB - 2 - p, 0), 0)),
                  rrow(2 * CW), pl.BlockSpec((2 * A, B), lambda p: (0, NB - 1 - p)),
                  pl.BlockSpec((8, B), lambda p: (0, NB - 1 - p)), rts(H * KR), rts(A), rts(A),
                  pl.BlockSpec((H, 1, 8, B), lambda p: (0, NB - 1 - p, 0, 0)),
                  _const((32, CW)), _const((DH, 1)), _const((DH, 1)), _const((8, 1)), _const((B, B))],
        out_specs=[rrow(2 * CW), rts(ZT_ROWS), _const((32, CW)), _const((8, CW)), _const((DH, 1)),
                   _const((DH, 1)), _const((8, 1))],
        out_shape=[_sds((S, 2 * CW), BF16), _sds((NB, ZT_ROWS, B), BF16), _sds((32, CW)), _sds((8, CW)),
                   _sds((DH, 1)), _sds((DH, 1)), _sds((8, 1))],
        scratch_shapes=[pltpu.VMEM((B + HALO + 8, CW), F32), pltpu.VMEM((8, B + HALO, CW), F32),
                        pltpu.VMEM((B, CW), F32), pltpu.VMEM((8, 1), F32), pltpu.VMEM((8, B), F32)],
        compiler_params=_cp(("arbitrary",)),
    )(du1, du1, u0, u0, alg, zqk, fgT, dqT, dkT, dvT, dfk, conv_w, qg_col, kg_col, bf_col, tri_lo)


def in_bwd_w(x, dzr, dzT, n1g, sc1, sh1):
    S = x.shape[0]
    B = BLK
    NB = S // B

    def body(x_ref, dzr_ref, dzT_ref, g_ref, sc_ref, sh_ref, dwT_hbm, dwr_hbm, dwT_acc, dwr_acc, sem):
        i = pl.program_id(0)

        @pl.when(i == 0)
        def _():
            dwT_acc[...] = jnp.zeros_like(dwT_acc)
            dwr_acc[...] = jnp.zeros_like(dwr_acc)

        h1 = _norm_mod(x_ref[...], g_ref[...], sc_ref[...], sh_ref[...])[2]
        dwT_acc[...] = dwT_acc[...] + _nn(dzT_ref[0], h1.astype(BF16))
        dwr_acc[...] = dwr_acc[...] + _nn(h1.T.astype(BF16), dzr_ref[...])

        @pl.when(i == NB - 1)
        def _():
            for src, dst in ((dwT_acc, dwT_hbm), (dwr_acc, dwr_hbm)):
                cp = pltpu.make_async_copy(src, dst, sem)
                cp.start()
                cp.wait()

    row = lambda w: pl.BlockSpec((B, w), lambda i: (i, 0))
    return _pc(
        body, name="in_bwd_w", grid=(NB,),
        in_specs=[row(D), row(2 * CW), pl.BlockSpec((1, ZT_ROWS, B), lambda i: (i, 0, 0)),
                  _const((1, D)), _const((1, D)), _const((1, D))],
        out_specs=[ANY, ANY],
        out_shape=[_sds((ZT_ROWS, D)), _sds((D, 2 * CW))],
        scratch_shapes=[pltpu.VMEM((ZT_ROWS, D), F32), pltpu.VMEM((D, 2 * CW), F32),
                        pltpu.SemaphoreType.DMA(())],
        compiler_params=_cp(("arbitrary",)),
    )(x, dzr, dzT, n1g, sc1, sh1)


def in_bwd_x(x, dx1, dzr, dzT, n1g, sc1, sh1, w_qkvf, wr, late_g):
    S = x.shape[0]
    B = BLK
    NB = S // B

    def body(x_ref, dx1_ref, dzr_ref, dzT_ref, g_ref, sc_ref, sh_ref, w_hbm, wr_hbm, lg_ref,
             gx_ref, v_ref, recv_ref, w_v, wr_v, sem, send_sems, recv_sems, local_sem):
        i = pl.program_id(0)

        @pl.when(i == 0)
        def _():
            _start_all(_direct_copies(True, lg_ref, recv_ref, send_sems, recv_sems, local_sem))
            for src, dst in ((w_hbm, w_v), (wr_hbm, wr_v)):
                cp = pltpu.make_async_copy(src, dst, sem)
                cp.start()
                cp.wait()
            v_ref[...] = jnp.zeros_like(v_ref)

        g = g_ref[...]
        sc = sc_ref[...]
        r1, xh, _ = _norm_mod(x_ref[...], g, sc, sh_ref[...])
        dh1 = _nt(dzr_ref[...], wr_v[...]) + _nn(w_v[...], dzT_ref[0]).T
        dhx = dh1 * xh
        v_ref[0:1, :] = v_ref[0:1, :] + _colsum(dh1)
        v_ref[1:2, :] = v_ref[1:2, :] + _colsum(dhx) * g
        v_ref[2:3, :] = v_ref[2:3, :] + _colsum(dhx) * (1.0 + sc)
        dxh = dh1 * (g * (1.0 + sc))
        gx_ref[...] = dx1_ref[...] + r1 * (dxh - xh * _rowmean(dxh * xh))

        @pl.when(i == NB - 1)
        def _():
            _wait_all(_direct_copies(True, lg_ref, recv_ref, send_sems, recv_sems, local_sem))

    row = lambda w: pl.BlockSpec((B, w), lambda i: (i, 0))
    return _pc(
        body, name="in_bwd_x", grid=(NB,),
        in_specs=[row(D), row(D), row(2 * CW), pl.BlockSpec((1, ZT_ROWS, B), lambda i: (i, 0, 0)),
                  _const((1, D)), _const((1, D)), _const((1, D)), ANY, ANY, ANY],
        out_specs=[row(D), _const((8, D)), ANY],
        out_shape=[_sds((S, D)), _sds((8, D)), _sds(late_g.shape, late_g.dtype)],
        scratch_shapes=[pltpu.VMEM((D, ZT_ROWS), BF16), pltpu.VMEM((D, 2 * CW), BF16),
                        pltpu.SemaphoreType.DMA(())] + COMM_SEMS,
        compiler_params=_cp(("arbitrary",)),
    )(x, dx1, dzr, dzT, n1g, sc1, sh1, w_qkvf, wr, late_g)


def _adam_math(g, w, m, v):
    m = ADAM_B1 * m + (1.0 - ADAM_B1) * g
    v = ADAM_B2 * v + (1.0 - ADAM_B2) * (g * g)
    m_hat = m / (1.0 - ADAM_B1 ** ADAM_STEP)
    v_hat = v / (1.0 - ADAM_B2 ** ADAM_STEP)
    delta = -ADAM_LR * (m_hat / (jnp.sqrt(v_hat) + ADAM_EPS) + ADAM_WD * w)
    return delta, m, v


def _row_tile(R):
    for t in (1024, 512, 256, 128, 64, 32, 16, 8):
        if R % t == 0:
            return t
    return R


def sum_slots(parts, name):
    K, R, C = parts.shape
    T = _row_tile(R)

    def body(p_ref, o_ref):
        s = p_ref[0]
        for k in range(1, K):
            s = s + p_ref[k]
        o_ref[...] = s

    return _pc(body, name=name, grid=(R // T,),
               in_specs=[pl.BlockSpec((K, T, C), lambda i: (0, i, 0))],
               out_specs=pl.BlockSpec((T, C), lambda i: (i, 0)), out_shape=_sds((R, C)),
               compiler_params=_cp(("arbitrary",)))(parts)


def adamw_slots(parts, w, m, v, name):
    K, R, C = parts.shape
    T = R
    while K * T * C * 4 > (8 << 20) and T % 16 == 0:
        T //= 2

    def body(p_ref, w_ref, m_ref, v_ref, g_ref, d_ref, nm_ref, nv_ref):
        g = p_ref[0].astype(F32)
        for k in range(1, K):
            g = g + p_ref[k].astype(F32)
        g_ref[...] = g
        d_ref[...], nm_ref[...], nv_ref[...] = _adam_math(g, w_ref[...], m_ref[...], v_ref[...])

    t2 = pl.BlockSpec((T, C), lambda i: (i, 0))
    return _pc(body, name=name, grid=(R // T,),
               in_specs=[pl.BlockSpec((K, T, C), lambda i: (0, i, 0)), t2, t2, t2],
               out_specs=[t2, t2, t2, t2], out_shape=[_sds((R, C))] * 4,
               compiler_params=_cp(("arbitrary",)))(parts, w, m, v)


def ada_grad_adamw(cT, dmod_cols, w, m, v):
    NCOL = w.shape[1]

    def body(cT_ref, dm_ref, w_ref, m_ref, v_ref, g_ref, d_ref, nm_ref, nv_ref):
        def term(b):
            cv = cT_ref[b]
            return (cv * _sigmoid(cv)) * dm_ref[b:b + 1, :]

        g = term(0)
        for b in range(1, N_DEV):
            g = g + term(b)
        g_ref[...] = g
        d_ref[...], nm_ref[...], nv_ref[...] = _adam_math(g, w_ref[...], m_ref[...], v_ref[...])

    full = _const((D, NCOL))
    return _pc(body, name="ada_grad_adamw", grid=(1,),
               in_specs=[_const((N_DEV, D, 1)), _const((N_DEV, NCOL)), full, full, full],
               out_specs=[full, full, full, full], out_shape=[_sds((D, NCOL))] * 4,
               compiler_params=_cp())(cT, dmod_cols, w, m, v)


def _pack_rows(vecs, rows=None):
    flat = jnp.concatenate([jnp.ravel(v) for v in vecs])
    n = flat.shape[0]
    r = -(-n // 1024) * 8 if rows is None else rows
    return jnp.pad(flat, (0, r * 128 - n)).reshape(r, 128)


def _unpack_rows(packed, shapes):
    flat = packed.reshape(-1)
    out, off = [], 0
    for s in shapes:
        n = 1
        for d in s:
            n *= d
        out.append(flat[off:off + n].reshape(s))
        off += n
    return out


def _cols_to_shards(w, cols):
    rows = w.shape[0]
    return w.reshape(rows, N_DEV, cols).transpose(1, 0, 2).reshape(N_DEV, rows * cols)


def kernel(x, c, w_ada, b_ada, norm1_g, w_in, q_norm_g, k_norm_g, b_f, conv_w, conv_b, conv_ln_g, conv_ln_b, beta_attn, beta_conv, w_out, norm2_g, w_ff1, w_ff2, loss_target, m_w_ada, m_b_ada, m_norm1_g, m_w_in, m_q_norm_g, m_k_norm_g, m_b_f, m_conv_w, m_conv_b, m_conv_ln_g, m_conv_ln_b, m_beta_attn, m_beta_conv, m_w_out, m_norm2_g, m_w_ff1, m_w_ff2, v_w_ada, v_b_ada, v_norm1_g, v_w_in, v_q_norm_g, v_k_norm_g, v_b_f, v_conv_w, v_conv_b, v_conv_ln_g, v_conv_ln_b, v_beta_attn, v_beta_conv, v_w_out, v_norm2_g, v_w_ff1, v_w_ff2):
    S = x.shape[1]
    B = BLK
    NB = S // B
    me = 4 * lax.axis_index("x") + 2 * lax.axis_index("y") + lax.axis_index("c")
    xs = x[0]
    tgt = loss_target[0]

    ADA_C, IN_C, FF_C, CV_C = w_ada.shape[2], w_in.shape[2], w_ff1.shape[2], conv_w.shape[2]
    OUT_R, FF_R = w_out.shape[1], w_ff2.shape[1]

    g_in, g_c, g_cw = gather_direct([w_in[0].astype(BF16), c, conv_w[0]], "ag_weights_early")
    late_w = [w_out[0].astype(BF16), w_ff1[0].astype(BF16), w_ff2[0].astype(BF16)]

    w_in_full = g_in.transpose(1, 0, 2).reshape(D, N_DEV * IN_C)
    c_all = g_c.reshape(N_DEV, D)
    conv_w_full = g_cw.transpose(1, 0, 2).reshape(KC, N_DEV * CV_C)
    conv_w_pad = jnp.pad(conv_w_full, ((0, 32 - KC), (0, 0)))

    w_qkvf = jnp.pad(w_in_full[:, 0:3 * A + H], ((0, 0), (0, ZT_ROWS - 3 * A - H)))
    wT = w_qkvf.T
    wr = w_in_full[:, 3 * A + H:]

    qg_col = q_norm_g.reshape(DH, 1)
    kg_col = k_norm_g.reshape(DH, 1)
    bf_col = b_f.reshape(H, 1)
    beta_a_col = beta_attn.reshape(A, 1)
    ii = lax.broadcasted_iota(jnp.int32, (B, B), 0)
    jj = lax.broadcasted_iota(jnp.int32, (B, B), 1)
    tri_up = (ii <= jj).astype(BF16)
    tri_lo = (ii >= jj).astype(BF16)

    modc = mod_columns(c_all, w_ada[0], lax.dynamic_slice(b_ada, (0, me * ADA_C), (1, ADA_C)))
    gm, = gather_direct([modc], "ag_mod")
    mod = lax.dynamic_index_in_dim(gm, me, axis=1, keepdims=False).reshape(1, N_DEV * ADA_C)
    sh1, sc1, g1, sh2, sc2, g2 = [mod[:, k * D:(k + 1) * D] for k in range(6)]
    qTa, kT, kaug, vT, vaug, zqk, fgT, alg, u0, stat = fwd_in(xs, norm1_g, sc1, sh1, wT, wr, qg_col,
                                                             kg_col, bf_col, tri_up)
    fmax = stat[:, :, 0].T
    fmin = stat[:, :, 1].T
    qk_max = jnp.sqrt(jnp.max(stat[:, :, 2], axis=0) * jnp.max(stat[:, :, 3], axis=0))
    thr = -(PRUNE + (2.02 / LOG2E) * qk_max)
    oT, lse, g_wo, w1_shards, g_w2 = attn_fwd(fmax, fmin, thr, qTa, kaug, vT, late_w)
    wo_full = g_wo.reshape(D, D)
    w2_full = g_w2.reshape(FF, D)
    x1, o, u1 = conv_merge_out(u0, xs, oT, conv_w_pad, conv_b, conv_ln_g, conv_ln_b, beta_conv,
                               beta_a_col, wo_full, g1)
    dy, loss_part, dg2, ra = mlp_fwd_loss(x1, tgt, norm2_g, sc2, sh2, g2, w1_shards, w2_full)

    dw1s, dw2, dh2p = mlp_bwd(x1, dy, ra, norm2_g, sc2, sh2, g2, w1_shards, w2_full)
    dx1, doTa, delta, du1, dwo, v1, v2, dba = merge_bwd(dh2p, x1, dy, o, oT, u1, norm2_g, sc2, g1,
                                                         conv_ln_g, conv_ln_b, beta_conv, beta_a_col, wo_full)
    early_g = [dwo.reshape(N_DEV, OUT_R, D), dw1s, dw2.reshape(N_DEV, FF_R, D)]
    dqT, dkT, dvT, dfk, r_out, r_f1, r_f2 = attn_bwd(fmax, fmin, thr, qTa, kaug, kT, vaug, doTa, lse,
                                                     delta, early_g)
    dzr, dzT, dcw, vc, dqg, dkg, dbf = head_bwd(du1, u0, alg, zqk, fgT, dqT, dkT, dvT, dfk, conv_w_pad,
                                                qg_col, kg_col, bf_col, tri_lo)
    dwT, dwr = in_bwd_w(xs, dzr, dzT, norm1_g, sc1, sh1)

    dw_in = jnp.concatenate([dwT.T[:, 0:3 * A + H], dwr], axis=1)
    late_g = _cols_to_shards(dw_in, IN_C)
    rows_b = -(-late_g.shape[1] // (128 * 256)) * 256
    late_g = jnp.pad(late_g, ((0, 0), (0, rows_b * 128 - late_g.shape[1]))).astype(BF16)
    grad_x, v0, recv_b = in_bwd_x(xs, dx1, dzr, dzT, norm1_g, sc1, sh1, w_qkvf, wr,
                                  late_g.reshape(N_DEV, rows_b, 128))

    dmod = jnp.concatenate([v0[0], v0[1], v1[3], v1[0], v1[1], dg2[0]])
    small1 = _pack_rows([dmod, v0[2], v1[2], dcw, vc[0], v2[1], v2[2], v2[0], dba, dqg, dkg,
                         jnp.pad(dbf.reshape(-1), (0, 120)), jnp.pad(loss_part.reshape(-1), (0, 127))])
    gs1, = gather_direct([small1], "ag_small_bwd")
    tot = sum_slots(gs1, "sum_small")
    (g_b_ada, g_n1, g_n2, g_cw, g_cb, g_lng, g_lnb, g_bc, g_ba, g_qg, g_kg, g_bf, loss_v) = _unpack_rows(
        tot, [(1, 6 * D), (1, D), (1, D), (32, CW), (1, CW), (1, CW), (1, CW), (1, CW), (1, A),
              (1, DH), (1, DH), (1, 128), (1, 128)])
    loss = loss_v[0, 0]
    g_bf = g_bf[:, 0:H]
    g_cw_mine = lax.dynamic_slice(g_cw[0:KC], (0, me * CV_C), (KC, CV_C)).reshape(1, KC, CV_C)

    small_names = [(b_ada, m_b_ada, v_b_ada, g_b_ada), (norm1_g, m_norm1_g, v_norm1_g, g_n1),
                   (q_norm_g, m_q_norm_g, v_q_norm_g, g_qg), (k_norm_g, m_k_norm_g, v_k_norm_g, g_kg),
                   (b_f, m_b_f, v_b_f, g_bf), (conv_w, m_conv_w, v_conv_w, g_cw_mine),
                   (conv_b, m_conv_b, v_conv_b, g_cb), (conv_ln_g, m_conv_ln_g, v_conv_ln_g, g_lng),
                   (conv_ln_b, m_conv_ln_b, v_conv_ln_b, g_lnb), (beta_attn, m_beta_attn, v_beta_attn, g_ba),
                   (beta_conv, m_beta_conv, v_beta_conv, g_bc), (norm2_g, m_norm2_g, v_norm2_g, g_n2)]
    shapes_s = [t[0].shape for t in small_names]
    pw, pm, pv, pg = [_pack_rows([t[k] for t in small_names]) for k in range(4)]
    sg_, sd_, sm_, sv_ = adamw_slots(pg[None], pw, pm, pv, "adamw_small")
    sgs, sds, sms, svs = [_unpack_rows(a, shapes_s) for a in (sg_, sd_, sm_, sv_)]

    dmod_all = gs1[:, 0:48, :].reshape(N_DEV, N_DEV, ADA_C)
    dmod_cols = lax.dynamic_index_in_dim(dmod_all, me, axis=1, keepdims=False)
    ga, da, ma_, va_ = ada_grad_adamw(c_all.reshape(N_DEV, D, 1), dmod_cols, w_ada[0], m_w_ada[0], v_w_ada[0])

    res_b = adamw_slots(recv_b, *[_pack_rows(ws, rows=rows_b) for ws in ([w_in], [m_w_in], [v_w_in])],
                        "adamw_in")
    late = []
    for nm, parts, (w_, m_, v_) in (("adamw_out", r_out, (w_out, m_w_out, v_w_out)),
                                    ("adamw_ff1", r_f1, (w_ff1, m_w_ff1, v_w_ff1)),
                                    ("adamw_ff2", r_f2, (w_ff2, m_w_ff2, v_w_ff2))):
        late.append([r[None] for r in adamw_slots(parts, w_[0], m_[0], v_[0], nm)])
    bgs, bds, bms, bvs = [_unpack_rows(res_b[q], [(1, D, IN_C)]) + [late[0][q], late[1][q], late[2][q]]
                          for q in range(4)]

    def assemble(small, ada, bigs):
        (b_ada_, n1_, qg_, kg_, bf_, cw_, cb_, lng_, lnb_, ba_, bc_, n2_) = small
        return [ada.reshape(1, D, ADA_C), b_ada_, n1_, bigs[0], qg_, kg_, bf_, cw_, cb_, lng_, lnb_, ba_, bc_,
                bigs[1], n2_, bigs[2], bigs[3]]

    return (loss, grad_x.reshape(1, S, D), *assemble(sgs, ga, bgs), *assemble(sds, da, bds),
            *assemble(sms, ma_, bms), *assemble(svs, va_, bvs))
```

```python
import functools

import jax
import jax.numpy as jnp
from jax import lax
from jax.experimental import pallas as pl
from jax.experimental.pallas import tpu as pltpu

F32 = jnp.float32
BF16 = jnp.bfloat16

D = 1024
A = 512
CW = 512
H = 8
DH = 64
FF = 4096
KC = 31
HALO = 32
KR = 80
ZT_ROWS = 1552
FCH = 1024
EPS = 1e-6
BLK = 512
N_DEV = 8
VMEM_LIMIT = 56 * 1024 * 1024

ADAM_LR = 0.001
ADAM_B1 = 0.9
ADAM_B2 = 0.999
ADAM_EPS = 1e-08
ADAM_WD = 0.01
ADAM_STEP = 10

MESH = pl.DeviceIdType.MESH
ANY = pl.BlockSpec(memory_space=pl.ANY)
SMEM = pl.BlockSpec(memory_space=pltpu.SMEM)
PRUNE = 105.0
LOG2E = 1.4426950408889634


def _pc(body, **kw):
    return pl.pallas_call(body, **kw)


def _cp(sem=None):
    return pltpu.CompilerParams(dimension_semantics=sem, vmem_limit_bytes=VMEM_LIMIT)


def _sds(shape, dtype=F32):
    return jax.ShapeDtypeStruct(shape, dtype)


def _const(shape):
    n = len(shape)
    return pl.BlockSpec(shape, lambda *a: (0,) * n)


def _nt(a, b):
    return lax.dot_general(a, b, (((1,), (1,)), ((), ())), preferred_element_type=F32)


def _nn(a, b):
    return jnp.dot(a, b, preferred_element_type=F32)


def _sigmoid(v):
    return 1.0 / (1.0 + jnp.exp(-v))


def _split3(v):
    a1 = v.astype(BF16)
    r1 = v - a1.astype(F32)
    a2 = r1.astype(BF16)
    a3 = (r1 - a2.astype(F32)).astype(BF16)
    return a1, a2, a3


def _rowmean(v):
    return jnp.mean(v, axis=-1, keepdims=True)


def _colsum(v):
    return jnp.sum(v, axis=0, keepdims=True)


def _lanesum(v):
    return jnp.sum(v, axis=-1, keepdims=True)


def _coords():
    return lax.axis_index("x"), lax.axis_index("y"), lax.axis_index("c")


def _direct_copies(scatter, src_ref, dst_ref, send_sems, recv_sems, local_sem):
    mx, my, mc = _coords()
    me = 4 * mx + 2 * my + mc
    if scatter:
        local = pltpu.make_async_copy(src_ref.at[me], dst_ref.at[0], local_sem)
    else:
        local = pltpu.make_async_copy(src_ref, dst_ref.at[me], local_sem)
    remote = []
    for r in range(1, N_DEV):
        px = 1 - mx if r & 4 else mx
        py = 1 - my if r & 2 else my
        pcc = 1 - mc if r & 1 else mc
        remote.append(pltpu.make_async_remote_copy(
            src_ref=src_ref.at[4 * px + 2 * py + pcc] if scatter else src_ref,
            dst_ref=dst_ref.at[r] if scatter else dst_ref.at[me],
            send_sem=send_sems.at[r - 1], recv_sem=recv_sems.at[r - 1],
            device_id=(px, py, pcc), device_id_type=MESH))
    return [local] + remote


def _start_all(copies):
    for cp in copies:
        cp.start()


def _wait_all(copies):
    for cp in copies[1:]:
        cp.wait()
    copies[0].wait()


COMM_SEMS = [pltpu.SemaphoreType.DMA((7,)), pltpu.SemaphoreType.DMA((7,)), pltpu.SemaphoreType.DMA(())]


def gather_direct(arrays, name):
    NG = len(arrays)

    def body(*refs):
        sems = refs[2 * NG:]
        sets = [_direct_copies(False, refs[q], refs[NG + q], *sems[3 * q:3 * q + 3]) for q in range(NG)]
        for copies in sets:
            _start_all(copies)
        for copies in sets:
            _wait_all(copies)

    return _pc(body, name=name, out_shape=[_sds((N_DEV,) + a.shape, a.dtype) for a in arrays],
               in_specs=[ANY] * NG, out_specs=[ANY] * NG, scratch_shapes=COMM_SEMS * NG)(*arrays)


def mod_columns(c_all, w_shard, b_cols):
    NCOL = w_shard.shape[1]

    def body(c_ref, w_ref, b_ref, o_ref):
        cv = c_ref[...]
        o_ref[...] = _nn((cv * _sigmoid(cv)).astype(BF16), w_ref[...].astype(BF16)) + b_ref[...]

    return _pc(body, name="mod_columns", out_shape=_sds((N_DEV, NCOL)),
               in_specs=[_const((N_DEV, D)), _const((D, NCOL)), _const((1, NCOL))],
               out_specs=_const((N_DEV, NCOL)), grid=(1,), compiler_params=_cp())(c_all, w_shard, b_cols)


def _norm_mod(xv, g, sc, sh):
    r = lax.rsqrt(_rowmean(xv * xv) + EPS)
    xh = xv * r
    return r, xh, xh * (g * (1.0 + sc)) + sh


def _log_sigmoid(v):
    e = jnp.exp(-jnp.abs(v))
    l1p = jnp.where(e < 1e-4, e * (1.0 - 0.5 * e), jnp.log(1.0 + e))
    return jnp.minimum(v, 0.0) - l1p


def fwd_in(x, n1g, sc1, sh1, wT, wr, qg_col, kg_col, bf_col, tri):
    S = x.shape[0]
    B = BLK
    NB = S // B

    def body(x_ref, g_ref, sc_ref, sh_ref, wT_ref, wr_ref, qg_ref, kg_ref, bf_ref, tri_ref,
             qTa_ref, kT_ref, kaug_ref, vT_ref, vaug_ref, zqk_ref, fgT_ref, alg_ref, u0_ref, stat_ref,
             carry, tbuf, fs, nq, nk):
        i = pl.program_id(0)

        @pl.when(i == 0)
        def _():
            carry[...] = jnp.zeros_like(carry)

        _, _, h = _norm_mod(x_ref[...], g_ref[...], sc_ref[...], sh_ref[...])
        hb = h.astype(BF16)
        zT = _nt(wT_ref[...], hb)
        zr = _nn(hb, wr_ref[...])
        zqk_ref[...] = zT[0:2 * A]
        fgT = zT[3 * A:3 * A + 8]
        fgT_ref[...] = fgT
        alg_ref[...] = zr
        u0_ref[...] = zr[:, 0:CW] * _sigmoid(zr[:, CW:2 * CW])

        logf = _log_sigmoid(fgT + bf_ref[...])
        a1, a2, a3 = _split3(logf)
        tr = tri_ref[...]
        F = _nn(a1, tr) + _nn(a2, tr) + _nn(a3, tr) + carry[...]
        carry[...] = carry[...] + _lanesum(logf)
        p1, p2, p3 = _split3(F * LOG2E)
        n1, n2, n3 = _split3(F * (-LOG2E))
        for k, v in enumerate((p1, p2, p3, n1, n2, n3)):
            fs[k] = v.astype(F32)

        rowi = lax.broadcasted_iota(jnp.int32, (8, B), 0)
        zeros_tail = jnp.zeros((128 - DH - 8, B), F32)
        ones_row = jnp.where(lax.broadcasted_iota(jnp.int32, (KR - DH, B), 0) == 0, 1.0, 0.0).astype(BF16)
        for hh in range(H):
            sl = slice(hh * DH, (hh + 1) * DH)
            q = zT[sl]
            k = zT[A + hh * DH:A + (hh + 1) * DH]
            v = zT[2 * A + hh * DH:2 * A + (hh + 1) * DH]
            qh = q * lax.rsqrt(jnp.mean(q * q, axis=0, keepdims=True) + EPS) * qg_ref[...] * (0.125 * LOG2E)
            kh = k * lax.rsqrt(jnp.mean(k * k, axis=0, keepdims=True) + EPS) * kg_ref[...]
            frow = [fs[kk, hh:hh + 1, :] for kk in range(6)]
            qx = jnp.where(rowi < 3, 1.0, jnp.where(rowi == 3, frow[0], jnp.where(
                rowi == 4, frow[1], jnp.where(rowi == 5, frow[2], 0.0))))
            kx = jnp.where(rowi == 0, frow[3], jnp.where(rowi == 1, frow[4], jnp.where(
                rowi == 2, frow[5], jnp.where(rowi < 6, 1.0, 0.0))))
            tbuf[0:DH, :] = qh
            tbuf[DH:DH + 8, :] = qx
            tbuf[DH + 8:128, :] = zeros_tail
            qTa_ref[0, hh * 128:(hh + 1) * 128, :] = tbuf[...].astype(BF16)
            tbuf[0:DH, :] = kh
            tbuf[DH:DH + 8, :] = kx
            kaug_ref[hh] = tbuf[...].T.astype(BF16)
            kT_ref[0, hh * KR:hh * KR + DH, :] = kh.astype(BF16)
            kT_ref[0, hh * KR + DH:(hh + 1) * KR, :] = ones_row
            tbuf[0:DH, :] = v
            tbuf[DH:DH + 8, :] = jnp.zeros((8, B), F32)
            vaug_ref[hh] = tbuf[...].T.astype(BF16)
            vT_ref[0, sl, :] = v.astype(BF16)
            nq[hh:hh + 1, :] = jnp.max(_colsum(qh * qh), axis=1, keepdims=True)
            nk[hh:hh + 1, :] = jnp.max(_colsum(kh * kh), axis=1, keepdims=True)

        lane = lax.broadcasted_iota(jnp.int32, (8, 128), 1)
        stat_ref[0] = jnp.where(lane == 0, jnp.max(F, axis=1, keepdims=True), jnp.where(
            lane == 1, jnp.min(F, axis=1, keepdims=True), jnp.where(
                lane == 2, nq[...], jnp.where(lane == 3, nk[...], 0.0))))

    row = lambda w: pl.BlockSpec((B, w), lambda i: (i, 0))
    tsp = lambda r: pl.BlockSpec((1, r, B), lambda i: (i, 0, 0))
    return _pc(
        body, name="fwd_in", grid=(NB,),
        in_specs=[row(D), _const((1, D)), _const((1, D)), _const((1, D)), _const((ZT_ROWS, D)),
                  _const((D, 2 * CW)), _const((DH, 1)), _const((DH, 1)), _const((8, 1)), _const((B, B))],
        out_specs=[tsp(H * 128), tsp(H * KR), pl.BlockSpec((H, B, 128), lambda i: (0, i, 0)), tsp(A),
                   pl.BlockSpec((H, B, 128), lambda i: (0, i, 0)),
                   pl.BlockSpec((2 * A, B), lambda i: (0, i)), pl.BlockSpec((8, B), lambda i: (0, i)),
                   row(2 * CW), row(CW), pl.BlockSpec((1, 8, 128), lambda i: (i, 0, 0))],
        out_shape=[_sds((NB, H * 128, B), BF16), _sds((NB, H * KR, B), BF16), _sds((H, S, 128), BF16),
                   _sds((NB, A, B), BF16), _sds((H, S, 128), BF16), _sds((2 * A, S)), _sds((8, S)),
                   _sds((S, 2 * CW)), _sds((S, CW)), _sds((NB, 8, 128))],
        scratch_shapes=[pltpu.VMEM((8, 1), F32), pltpu.VMEM((128, B), F32), pltpu.VMEM((6, 8, B), F32),
                        pltpu.VMEM((8, 1), F32), pltpu.VMEM((8, 1), F32)],
        compiler_params=_cp(("arbitrary",)),
    )(x, n1g, sc1, sh1, wT, wr, qg_col, kg_col, bf_col, tri)


def _first_key_block(top, fmin_ref, cut, h, i):
    return lax.while_loop(
        lambda j: jnp.logical_and(j > 0, top - fmin_ref[h, jnp.maximum(j - 1, 0)] >= cut),
        lambda j: j - 1, i)


def _causal_keep(B):
    return lax.broadcasted_iota(jnp.int32, (B, B), 0) <= lax.broadcasted_iota(jnp.int32, (B, B), 1)


def _keep_right_half(B):
    hb = B // 2
    return lax.broadcasted_iota(jnp.int32, (B, hb), 0) <= lax.broadcasted_iota(jnp.int32, (B, hb), 1) + hb


def _keep_top_half(B):
    hb = B // 2
    return lax.broadcasted_iota(jnp.int32, (hb, B), 0) <= lax.broadcasted_iota(jnp.int32, (hb, B), 1)


def attn_fwd(fmax, fmin, thr, qTa, kaug, vT, late_w):
    NB, _, B = qTa.shape
    S = NB * B
    NG = len(late_w)
    HB = B // 2

    def body(fmax_ref, fmin_ref, thr_ref, q_ref, k_ref, v_ref, *rest):
        w_refs = rest[0:NG]
        o_ref, lse_ref = rest[NG:NG + 2]
        gw_refs = rest[NG + 2:2 * NG + 2]
        s0, s1, m_ref, l_ref, acc_ref = rest[2 * NG + 2:2 * NG + 7]
        sems = rest[2 * NG + 7:]
        h = pl.program_id(0)
        i = pl.program_id(1)

        def gathers():
            return [_direct_copies(False, w_refs[q], gw_refs[q], *sems[3 * q:3 * q + 3]) for q in range(NG)]

        @pl.when(jnp.logical_and(h == 0, i == 0))
        def _():
            for copies in gathers():
                _start_all(copies)

        jlo = _first_key_block(fmax_ref[h, i], fmin_ref, thr_ref[h], h, i)
        n = i - jlo

        def scores(j, s_ref):
            s_ref[...] = _nn(k_ref[0, pl.ds(pl.multiple_of(j * B, B), B), :], q_ref[0])

        def softmax_cols(s, j, cols, rows):
            m = m_ref[:, cols]
            mn = jnp.maximum(m, jnp.max(s, axis=0, keepdims=True))
            a = jnp.exp2(m - mn)
            p = jnp.exp2(s - mn)
            m_ref[:, cols] = mn
            l_ref[:, cols] = a * l_ref[:, cols] + _colsum(p)
            acc_ref[:, cols] = a * acc_ref[:, cols] + _nn(v_ref[j][:, rows], p.astype(BF16))

        def softmax_step(s_ref, j, masked):
            if not masked:
                softmax_cols(s_ref[...], j, slice(None), slice(None))
                return
            lo, hi = slice(0, HB), slice(HB, B)
            softmax_cols(jnp.where(_causal_keep(HB), s_ref[lo, lo], -jnp.inf), j, lo, lo)
            softmax_cols(jnp.where(_keep_right_half(B), s_ref[:, hi], -jnp.inf), j, hi, slice(None))

        m_ref[...] = jnp.full((1, B), -jnp.inf, F32)
        l_ref[...] = jnp.zeros((1, B), F32)
        acc_ref[...] = jnp.zeros((DH, B), F32)
        scores(jlo, s0)

        def pair(t, carry):
            j = jlo + 2 * t
            scores(j + 1, s1)
            softmax_step(s0, j, False)
            scores(j + 2, s0)
            softmax_step(s1, j + 1, False)
            return carry

        lax.fori_loop(0, n // 2, pair, 0)

        @pl.when(n % 2 == 1)
        def _():
            scores(i, s1)
            softmax_step(s0, i - 1, False)
            softmax_step(s1, i, True)

        @pl.when(n % 2 == 0)
        def _():
            softmax_step(s0, i, True)

        l = l_ref[...]
        o_ref[0] = acc_ref[...] / l
        lse_ref[0, 0] = m_ref[...] + jnp.log2(l)

        @pl.when(jnp.logical_and(h == H - 1, i == NB - 1))
        def _():
            for copies in gathers():
                _wait_all(copies)

    return _pc(
        body, name="attn_fwd", grid=(H, NB),
        in_specs=[SMEM, SMEM, SMEM, pl.BlockSpec((1, 128, B), lambda h, i: (i, h, 0)),
                  pl.BlockSpec((1, S, 128), lambda h, i: (h, 0, 0)),
                  pl.BlockSpec((NB, DH, B), lambda h, i: (0, h, 0))] + [ANY] * NG,
        out_specs=[pl.BlockSpec((1, DH, B), lambda h, i: (i, h, 0)),
                   pl.BlockSpec((1, 1, 1, B), lambda h, i: (h, i, 0, 0))] + [ANY] * NG,
        out_shape=[_sds((NB, A, B)), _sds((H, NB, 1, B))]
        + [_sds((N_DEV,) + w.shape, w.dtype) for w in late_w],
        scratch_shapes=[pltpu.VMEM((B, B), F32), pltpu.VMEM((B, B), F32), pltpu.VMEM((1, B), F32),
                        pltpu.VMEM((1, B), F32), pltpu.VMEM((DH, B), F32)] + COMM_SEMS * NG,
        compiler_params=_cp(("arbitrary", "arbitrary")),
    )(fmax, fmin, thr, qTa, kaug, vT, *late_w)


def attn_bwd(fmax, fmin, thr, qTa, kaug, kT, vaug, doTa, lse, delta, early_g):
    NG = len(early_g)
    NB, _, B = qTa.shape
    QR = 80
    CH = min(256, B)

    def body(fmax_ref, fmin_ref, thr_ref, q_ref, ka_ref, kt_ref, va_ref, do_ref, lse_ref, dl_ref, *rest):
        g_refs = rest[0:NG]
        dq_ref, dk_ref, dv_ref, dfk_ref = rest[NG:NG + 4]
        recv_refs = rest[NG + 4:2 * NG + 4]
        s0, d0, s1, d1, p0, e0, p1, e1, dk_acc, dv_acc = rest[2 * NG + 4:2 * NG + 14]
        sems = rest[2 * NG + 14:]
        h = pl.program_id(0)
        j = pl.program_id(1)

        def exchanges():
            return [_direct_copies(True, g_refs[q], recv_refs[q], *sems[3 * q:3 * q + 3]) for q in range(NG)]

        @pl.when(jnp.logical_and(h == 0, j == 0))
        def _():
            for copies in exchanges():
                _start_all(copies)

        bottom = fmin_ref[h, j]
        cut = thr_ref[h]
        ihi = lax.while_loop(
            lambda i: jnp.logical_and(i < NB - 1, fmax_ref[h, jnp.minimum(i + 1, NB - 1)] - bottom >= cut),
            lambda i: i + 1, j)

        @pl.when(j == 0)
        def _():
            dq_ref[...] = jnp.zeros_like(dq_ref)

        dk_acc[...] = jnp.zeros_like(dk_acc)
        dv_acc[...] = jnp.zeros_like(dv_acc)

        n = ihi - j + 1

        def blk(k):
            return jnp.minimum(j + k, ihi)

        def products(k, s_ref, d_ref):
            i = blk(k)
            s_ref[...] = _nn(ka_ref[0], q_ref[i])
            d_ref[...] = _nn(va_ref[0], do_ref[i])

        def elementwise(k, s_ref, d_ref, p_ref, ds_ref, masked):
            i = blk(k)
            p = jnp.exp2(s_ref[...] - lse_ref[0, i])
            if masked:
                p = jnp.where(_causal_keep(B), p, 0.0)
            p_ref[...] = p.astype(BF16)
            ds_ref[...] = (p * (d_ref[...] - dl_ref[0, i])).astype(BF16)

        def grads(k, p_ref, ds_ref):
            i = j + k
            dsb = ds_ref[...]
            dv_acc[...] = dv_acc[...] + _nt(do_ref[i][0:DH], p_ref[...])
            dk_acc[...] = dk_acc[...] + _nt(q_ref[i][0:QR], dsb)
            dq_ref[i] = dq_ref[i] + _nn(kt_ref[0], dsb)

        def trip(kp, s_p, d_p, ke, s_e, d_e, p_e, e_e, kg, p_g, e_g):
            ip, ie, ig = blk(kp), blk(ke), j + kg
            lse_e = lse_ref[0, ie]
            dl_e = dl_ref[0, ie]
            dq_part = jnp.zeros((KR, B), F32)
            for c in range(B // CH):
                rows = slice(c * CH, (c + 1) * CH)
                s_p[rows, :] = _nn(ka_ref[0, rows, :], q_ref[ip])
                d_p[rows, :] = _nn(va_ref[0, rows, :], do_ref[ip])
                p = jnp.exp2(s_e[rows, :] - lse_e)
                p_e[rows, :] = p.astype(BF16)
                e_e[rows, :] = (p * (d_e[rows, :] - dl_e)).astype(BF16)
                dsb = e_g[rows, :]
                dv_acc[:, rows] = dv_acc[:, rows] + _nt(do_ref[ig][0:DH], p_g[rows, :])
                dk_acc[:, rows] = dk_acc[:, rows] + _nt(q_ref[ig][0:QR], dsb)
                dq_part = dq_part + _nn(kt_ref[0, :, rows], dsb)
            dq_ref[ig] = dq_ref[ig] + dq_part

        HB = B // 2
        lo, hi = slice(0, HB), slice(HB, B)

        def diagonal_products(s_ref, d_ref):
            s_ref[lo, :] = _nn(ka_ref[0, lo, :], q_ref[j])
            d_ref[lo, :] = _nn(va_ref[0, lo, :], do_ref[j])
            s_ref[hi, hi] = _nn(ka_ref[0, hi, :], q_ref[j, :, hi])
            d_ref[hi, hi] = _nn(va_ref[0, hi, :], do_ref[j, :, hi])

        def diagonal_elementwise(s_ref, d_ref, p_ref, ds_ref):
            lse0 = lse_ref[0, j]
            dl0 = dl_ref[0, j]
            p = jnp.where(_keep_top_half(B), jnp.exp2(s_ref[lo, :] - lse0), 0.0)
            p_ref[lo, :] = p.astype(BF16)
            ds_ref[lo, :] = (p * (d_ref[lo, :] - dl0)).astype(BF16)
            p = jnp.where(_causal_keep(HB), jnp.exp2(s_ref[hi, hi] - lse0[:, hi]), 0.0)
            p_ref[hi, hi] = p.astype(BF16)
            ds_ref[hi, hi] = (p * (d_ref[hi, hi] - dl0[:, hi])).astype(BF16)
            p_ref[hi, lo] = jnp.zeros((HB, HB), BF16)
            ds_ref[hi, lo] = jnp.zeros((HB, HB), BF16)

        diagonal_products(s0, d0)
        products(1, s1, d1)
        diagonal_elementwise(s0, d0, p0, e0)

        def pair(t, carry):
            k = 2 * t
            trip(k + 2, s0, d0, k + 1, s1, d1, p1, e1, k, p0, e0)
            trip(k + 3, s1, d1, k + 2, s0, d0, p0, e0, k + 1, p1, e1)
            return carry

        n_pairs = n // 2
        lax.fori_loop(0, n_pairs - 1, pair, 0)
        k_last = 2 * (n_pairs - 1)

        @pl.when(jnp.logical_and(n_pairs >= 1, n % 2 == 0))
        def _():
            elementwise(k_last + 1, s1, d1, p1, e1, False)
            grads(k_last, p0, e0)
            grads(k_last + 1, p1, e1)

        @pl.when(jnp.logical_and(n_pairs >= 1, n % 2 == 1))
        def _():
            trip(k_last + 2, s0, d0, k_last + 1, s1, d1, p1, e1, k_last, p0, e0)
            elementwise(k_last + 2, s0, d0, p0, e0, False)
            grads(k_last + 1, p1, e1)
            grads(k_last + 2, p0, e0)

        @pl.when(n == 1)
        def _():
            grads(0, p0, e0)

        dk_ref[0] = dk_acc[0:DH, :]
        dfk_ref[0, 0] = dk_acc[DH:DH + 8, :]
        dv_ref[0] = dv_acc[...]

        @pl.when(jnp.logical_and(h == H - 1, j == NB - 1))
        def _():
            for copies in exchanges():
                _wait_all(copies)

    per_kv = lambda r: pl.BlockSpec((1, r, B), lambda h, j: (j, h, 0))
    head_all = lambda r: pl.BlockSpec((NB, r, B), lambda h, j: (0, h, 0))
    aug = pl.BlockSpec((1, B, 128), lambda h, j: (h, j, 0))
    stat = pl.BlockSpec((1, NB, 1, B), lambda h, j: (h, 0, 0, 0))
    return _pc(
        body, name="attn_bwd", grid=(H, NB),
        in_specs=[SMEM, SMEM, SMEM, head_all(128), aug, per_kv(KR), aug, head_all(128), stat, stat]
        + [ANY] * NG,
        out_specs=[head_all(KR), per_kv(DH), per_kv(DH),
                   pl.BlockSpec((1, 1, 8, B), lambda h, j: (h, j, 0, 0))] + [ANY] * NG,
        out_shape=[_sds((NB, H * KR, B)), _sds((NB, A, B)), _sds((NB, A, B)), _sds((H, NB, 8, B))]
        + [_sds(g.shape, g.dtype) for g in early_g],
        scratch_shapes=[pltpu.VMEM((B, B), F32)] * 4 + [pltpu.VMEM((B, B), BF16)] * 4
        + [pltpu.VMEM((QR, B), F32), pltpu.VMEM((DH, B), F32)] + COMM_SEMS * NG,
        compiler_params=_cp(("arbitrary", "arbitrary")),
    )(fmax, fmin, thr, qTa, kaug, kT, vaug, doTa, lse, delta, *early_g)


def _conv_tail(u1, lng, lnb, beta_c):
    mu = _rowmean(u1)
    d = u1 - mu
    rstd = lax.rsqrt(_rowmean(d * d) + EPS)
    xhat = d * rstd
    u2 = xhat * lng + lnb
    sg = _sigmoid(u2)
    u3 = u2 * sg
    rc = lax.rsqrt(_rowmean(u3 * u3) + EPS)
    n3 = u3 * rc
    return rstd, xhat, u2, sg, rc, n3, n3 * beta_c


def _attn_tail(oT, beta_a_col):
    ra = lax.rsqrt(jnp.mean(oT * oT, axis=0, keepdims=True) + EPS)
    ohat = oT * ra
    return ra, ohat, ohat * beta_a_col


CONV_ROWS = 32
CORR_ROWS = 16


def _fill_shifted(buf, sh):
    rows = sh.shape[1]
    for ph in range(8):
        sh[ph] = buf[pl.ds(ph, rows), :]


def _tap_rows(sh, r0, o, rows):
    return sh[o % 8, pl.ds(pl.multiple_of(r0 + 8 * (o // 8), 8), rows), :]


def _depthwise(sh, w_ref, offs, bias, out_ref, B):
    def chunk(ci, carry):
        r0 = pl.multiple_of(ci * CONV_ROWS, CONV_ROWS)
        acc = jnp.broadcast_to(bias, (CONV_ROWS, CW))
        for k, o in enumerate(offs):
            acc = acc + w_ref[k:k + 1, :] * _tap_rows(sh, r0, o, CONV_ROWS)
        out_ref[pl.ds(r0, CONV_ROWS), :] = acc
        return carry

    lax.fori_loop(0, B // CONV_ROWS, chunk, 0)


def _tap_gradients(sh, d_ref, offs, dw_ref, B):
    for g0 in range(0, len(offs), 8):
        ks = list(range(g0, min(g0 + 8, len(offs))))

        def chunk(ci, accs, ks=ks):
            r0 = pl.multiple_of(ci * CORR_ROWS, CORR_ROWS)
            d = d_ref[pl.ds(r0, CORR_ROWS), :]
            out = []
            for a, k in zip(accs, ks):
                pr = d * _tap_rows(sh, r0, offs[k], CORR_ROWS)
                out.append(a + pr[0:8] + pr[8:16])
            return tuple(out)

        accs = lax.fori_loop(0, B // CORR_ROWS, chunk, tuple(jnp.zeros((8, CW), F32) for _ in ks))
        for a, k in zip(accs, ks):
            dw_ref[k:k + 1, :] = dw_ref[k:k + 1, :] + _colsum(a)


def conv_merge_out(u0, x, oT, conv_w, conv_b, lng, lnb, beta_c, beta_a_col, wo, g1):
    S = x.shape[0]
    B = BLK
    NB = S // B

    def body(uc_ref, up_ref, x_ref, oT_ref, w_ref, cb_ref, lng_ref, lnb_ref, bc_ref, ba_ref, wo_ref,
             g1_ref, x1_ref, o_ref, u1_ref, ubuf, sh):
        i = pl.program_id(0)
        ubuf[0:HALO, :] = jnp.where(i > 0, up_ref[B - HALO:B, :], 0.0)
        ubuf[HALO:HALO + B, :] = uc_ref[...]
        ubuf[HALO + B:HALO + B + 8, :] = jnp.zeros((8, CW), F32)
        _fill_shifted(ubuf, sh)
        _depthwise(sh, w_ref, [HALO - (KC - 1) + k for k in range(KC)], cb_ref[...], u1_ref, B)
        mc = _conv_tail(u1_ref[...], lng_ref[...], lnb_ref[...], bc_ref[...])[-1]
        maT = _attn_tail(oT_ref[0], ba_ref[...])[-1]
        ma = maT.T
        o = _nn(ma.astype(BF16), wo_ref[0:A, :]) + _nn(mc.astype(BF16), wo_ref[A:D, :])
        o_ref[...] = o
        x1_ref[...] = x_ref[...] + g1_ref[...] * o

    row = lambda w: pl.BlockSpec((B, w), lambda i: (i, 0))
    return _pc(
        body, name="conv_merge_out", grid=(NB,),
        in_specs=[row(CW), pl.BlockSpec((B, CW), lambda i: (jnp.maximum(i - 1, 0), 0)), row(D),
                  pl.BlockSpec((1, A, B), lambda i: (i, 0, 0)), _const((32, CW)), _const((1, CW)),
                  _const((1, CW)), _const((1, CW)), _const((1, CW)), _const((A, 1)), _const((D, D)),
                  _const((1, D))],
        out_specs=[row(D), row(D), row(CW)],
        out_shape=[_sds((S, D)), _sds((S, D)), _sds((S, CW))],
        scratch_shapes=[pltpu.VMEM((B + HALO + 8, CW), F32), pltpu.VMEM((8, B + HALO, CW), F32)],
        compiler_params=_cp(("arbitrary",)),
    )(u0, u0, x, oT, conv_w, conv_b, lng, lnb, beta_c, beta_a_col, wo, g1)


def mlp_fwd_loss(x1, tgt, n2g, sc2, sh2, g2, w1s, w2):
    S = x1.shape[0]
    B = BLK
    NB = S // B
    NS, _, FS = w1s.shape

    def body(x1_ref, t_ref, g_ref, sc_ref, sh_ref, g2_ref, w1_hbm, w2_hbm, dy_ref, loss_ref, dg2_ref, ra_ref,
             w1_v, w2_v, sem):
        i = pl.program_id(0)

        @pl.when(i == 0)
        def _():
            for src, dst in ((w1_hbm, w1_v), (w2_hbm, w2_v)):
                cp = pltpu.make_async_copy(src, dst, sem)
                cp.start()
                cp.wait()
            loss_ref[...] = jnp.zeros_like(loss_ref)
            dg2_ref[...] = jnp.zeros_like(dg2_ref)

        x1 = x1_ref[...]
        hs = _norm_mod(x1, g_ref[...], sc_ref[...], sh_ref[...])[2].astype(BF16)
        y2 = jnp.zeros((B, D), F32)
        for q in range(NS):
            a = jnp.maximum(_nn(hs, w1_v[q]), 0.0)
            ra_ref[:, q * FS:(q + 1) * FS] = a.astype(BF16)
            y2 = y2 + _nn((a * a).astype(BF16), w2_v[q * FS:(q + 1) * FS, :])
        e = x1 + g2_ref[...] * y2 - t_ref[...]
        dy = e * (1.0 / D)
        dy_ref[...] = dy
        loss_ref[...] = loss_ref[...] + 0.5 * _colsum(_rowmean(e * e))
        dg2_ref[...] = dg2_ref[...] + _colsum(dy * y2)

    row = pl.BlockSpec((B, D), lambda i: (i, 0))
    return _pc(
        body, name="mlp_fwd_loss", grid=(NB,),
        in_specs=[row, row, _const((1, D)), _const((1, D)), _const((1, D)), _const((1, D)), ANY, ANY],
        out_specs=[row, _const((1, 1)), _const((1, D)), pl.BlockSpec((B, FF), lambda i: (i, 0))],
        out_shape=[_sds((S, D)), _sds((1, 1)), _sds((1, D)), _sds((S, FF), BF16)],
        scratch_shapes=[pltpu.VMEM(w1s.shape, BF16), pltpu.VMEM(w2.shape, BF16), pltpu.SemaphoreType.DMA(())],
        compiler_params=_cp(("arbitrary",)),
    )(x1, tgt, n2g, sc2, sh2, g2, w1s, w2)


def mlp_bwd(x1, dy, ra, n2g, sc2, sh2, g2, w1s, w2):
    S = x1.shape[0]
    B = BLK
    NB = S // B
    NC = FF // FCH
    NS, _, FS = w1s.shape
    SPC = NS // NC

    def body(x1_ref, dy_ref, ra_ref, g_ref, sc_ref, sh_ref, g2_ref, w1_ref, w2_ref, dw1_ref, dw2_ref, dh_ref):
        i = pl.program_id(1)

        @pl.when(i == 0)
        def _():
            dw1_ref[...] = jnp.zeros_like(dw1_ref)
            dw2_ref[...] = jnp.zeros_like(dw2_ref)

        h2 = _norm_mod(x1_ref[...], g_ref[...], sc_ref[...], sh_ref[...])[2]
        h2t = h2.T.astype(BF16)
        ra = ra_ref[...].astype(F32)
        dyb = (dy_ref[...] * g2_ref[...]).astype(BF16)
        db = _nt(dyb, w2_ref[...])
        dab = (db * (2.0 * ra)).astype(BF16)
        dh = jnp.zeros((B, D), F32)
        for q in range(SPC):
            part = dab[:, q * FS:(q + 1) * FS]
            dh = dh + _nt(part, w1_ref[q])
            dw1_ref[q] = dw1_ref[q] + _nn(h2t, part)
        dh_ref[0] = dh
        dw2_ref[...] = dw2_ref[...] + _nn((ra * ra).T.astype(BF16), dyb)

    row = pl.BlockSpec((B, D), lambda c, i: (i, 0))
    vec = pl.BlockSpec((1, D), lambda c, i: (0, 0))
    wsh = pl.BlockSpec((SPC, D, FS), lambda c, i: (c, 0, 0))
    w2b = pl.BlockSpec((FCH, D), lambda c, i: (c, 0))
    return _pc(
        body, name="mlp_bwd", grid=(NC, NB),
        in_specs=[row, row, pl.BlockSpec((B, FCH), lambda c, i: (i, c)), vec, vec, vec, vec, wsh, w2b],
        out_specs=[wsh, w2b, pl.BlockSpec((1, B, D), lambda c, i: (c, i, 0))],
        out_shape=[_sds(w1s.shape), _sds((FF, D)), _sds((NC, S, D))],
        compiler_params=_cp(("arbitrary", "arbitrary")),
    )(x1, dy, ra, n2g, sc2, sh2, g2, w1s, w2)


def merge_bwd(dh2p, x1, dy, o, oT, u1, n2g, sc2, g1, lng, lnb, beta_c, beta_a_col, wo):
    S = x1.shape[0]
    B = BLK
    NB = S // B
    NC = dh2p.shape[0]

    def body(dh_ref, x1_ref, dy_ref, o_ref, oT_ref, u1_ref, g_ref, sc_ref, g1_ref, lng_ref, lnb_ref,
             bc_ref, ba_ref, wo_ref,
             dx1_ref, doTa_ref, dl_ref, du1_ref, dwo_ref, v1_ref, v2_ref, dba_ref):
        i = pl.program_id(0)

        @pl.when(i == 0)
        def _():
            dwo_ref[...] = jnp.zeros_like(dwo_ref)
            v1_ref[...] = jnp.zeros_like(v1_ref)
            v2_ref[...] = jnp.zeros_like(v2_ref)
            dba_ref[...] = jnp.zeros_like(dba_ref)

        dh2 = dh_ref[0]
        for cc in range(1, NC):
            dh2 = dh2 + dh_ref[cc]
        x1 = x1_ref[...]
        g = g_ref[...]
        sc = sc_ref[...]
        r2 = lax.rsqrt(_rowmean(x1 * x1) + EPS)
        xh = x1 * r2
        dhx = dh2 * xh
        v1_ref[0:1, :] = v1_ref[0:1, :] + _colsum(dh2)
        v1_ref[1:2, :] = v1_ref[1:2, :] + _colsum(dhx) * g
        v1_ref[2:3, :] = v1_ref[2:3, :] + _colsum(dhx) * (1.0 + sc)
        dxh = dh2 * (g * (1.0 + sc))
        dx1 = dy_ref[...] + r2 * (dxh - xh * _rowmean(dxh * xh))
        dx1_ref[...] = dx1
        v1_ref[3:4, :] = v1_ref[3:4, :] + _colsum(dx1 * o_ref[...])
        dob = (dx1 * g1_ref[...]).astype(BF16)

        lng = lng_ref[...]
        bc = bc_ref[...]
        rstd, xhat, u2, sg, rc, n3, mc = _conv_tail(u1_ref[...], lng, lnb_ref[...], bc)
        ba = ba_ref[...]
        oT = oT_ref[0]
        ra, ohat, maT = _attn_tail(oT, ba)
        dwo_ref[0:A, :] = dwo_ref[0:A, :] + _nn(maT.astype(BF16), dob)
        dwo_ref[A:D, :] = dwo_ref[A:D, :] + _nn(mc.T.astype(BF16), dob)
        dmaT = _nt(wo_ref[0:A, :], dob)
        dmc = _nt(dob, wo_ref[A:D, :])

        dba_ref[...] = dba_ref[...] + _lanesum(dmaT * ohat)
        dohat = dmaT * ba
        doT = ra * (dohat - ohat * jnp.mean(dohat * ohat, axis=0, keepdims=True))
        prod = doT * oT
        zpad = jnp.zeros((128 - DH, B), BF16)
        for hh in range(H):
            sl = slice(hh * DH, (hh + 1) * DH)
            dl_ref[hh, 0] = _colsum(prod[sl])
            doTa_ref[0, hh * 128:hh * 128 + DH, :] = doT[sl].astype(BF16)
            doTa_ref[0, hh * 128 + DH:(hh + 1) * 128, :] = zpad

        v2_ref[0:1, :] = v2_ref[0:1, :] + _colsum(dmc * n3)
        dn3 = dmc * bc
        du3 = rc * (dn3 - n3 * _rowmean(dn3 * n3))
        du2 = du3 * (sg * (1.0 + u2 * (1.0 - sg)))
        v2_ref[1:2, :] = v2_ref[1:2, :] + _colsum(du2 * xhat)
        v2_ref[2:3, :] = v2_ref[2:3, :] + _colsum(du2)
        dxhat = du2 * lng
        du1_ref[...] = rstd * (dxhat - _rowmean(dxhat) - xhat * _rowmean(dxhat * xhat))

    row = lambda w: pl.BlockSpec((B, w), lambda i: (i, 0))
    return _pc(
        body, name="merge_bwd", grid=(NB,),
        in_specs=[pl.BlockSpec((NC, B, D), lambda i: (0, i, 0)), row(D), row(D), row(D),
                  pl.BlockSpec((1, A, B), lambda i: (i, 0, 0)), row(CW), _const((1, D)), _const((1, D)),
                  _const((1, D)), _const((1, CW)), _const((1, CW)), _const((1, CW)), _const((A, 1)),
                  _const((D, D))],
        out_specs=[row(D), pl.BlockSpec((1, H * 128, B), lambda i: (i, 0, 0)),
                   pl.BlockSpec((H, 1, 1, B), lambda i: (0, i, 0, 0)), row(CW), _const((D, D)),
                   _const((8, D)), _const((8, CW)), _const((A, 1))],
        out_shape=[_sds((S, D)), _sds((NB, H * 128, B), BF16), _sds((H, NB, 1, B)), _sds((S, CW)),
                   _sds((D, D)), _sds((8, D)), _sds((8, CW)), _sds((A, 1))],
        compiler_params=_cp(("arbitrary",)),
    )(dh2p, x1, dy, o, oT, u1, n2g, sc2, g1, lng, lnb, beta_c, beta_a_col, wo)


def head_bwd(du1, u0, alg, zqk, fgT, dqT, dkT, dvT, dfk, conv_w, qg_col, kg_col, bf_col, tri_lo):
    S = u0.shape[0]
    B = BLK
    NB = S // B

    def body(dc_ref, dn_ref, uc_ref, up_ref, alg_ref, zqk_ref, fg_ref, dq_ref, dk_ref, dv_ref, dfk_ref,
             w_ref, qg_ref, kg_ref, bf_ref, tri_ref,
             dzr_ref, dzT_ref, dcw_ref, vc_ref, dqg_ref, dkg_ref, dbf_ref,
             buf, sh, du0_ref, carry, fbuf):
        pid = pl.program_id(0)
        ri = NB - 1 - pid

        @pl.when(pid == 0)
        def _():
            carry[...] = jnp.zeros_like(carry)
            dcw_ref[...] = jnp.zeros_like(dcw_ref)
            vc_ref[...] = jnp.zeros_like(vc_ref)
            dqg_ref[...] = jnp.zeros_like(dqg_ref)
            dkg_ref[...] = jnp.zeros_like(dkg_ref)
            dbf_ref[...] = jnp.zeros_like(dbf_ref)

        zero8 = jnp.zeros((8, CW), F32)
        buf[0:B, :] = dc_ref[...]
        buf[B:B + HALO, :] = jnp.where(ri < NB - 1, dn_ref[0:HALO, :], 0.0)
        buf[B + HALO:B + HALO + 8, :] = zero8
        _fill_shifted(buf, sh)
        _depthwise(sh, w_ref, [KC - 1 - k for k in range(KC)], jnp.zeros((1, CW), F32), du0_ref, B)
        buf[0:HALO, :] = jnp.where(ri > 0, up_ref[B - HALO:B, :], 0.0)
        buf[HALO:HALO + B, :] = uc_ref[...]
        buf[HALO + B:HALO + B + 8, :] = zero8
        _fill_shifted(buf, sh)
        _tap_gradients(sh, dc_ref, [HALO - (KC - 1) + k for k in range(KC)], dcw_ref, B)
        vc_ref[0:1, :] = vc_ref[0:1, :] + _colsum(dc_ref[...])

        du0 = du0_ref[...]
        al = alg_ref[:, 0:CW]
        sg = _sigmoid(alg_ref[:, CW:2 * CW])
        dzr_ref[:, 0:CW] = (du0 * sg).astype(BF16)
        dzr_ref[:, CW:2 * CW] = (du0 * al * sg * (1.0 - sg)).astype(BF16)

        dqg = jnp.zeros((DH, B), F32)
        dkg = jnp.zeros((DH, B), F32)
        qg = qg_ref[...]
        kg = kg_ref[...]
        for hh in range(H):
            sl = slice(hh * DH, (hh + 1) * DH)
            q = zqk_ref[sl, :]
            rq = lax.rsqrt(jnp.mean(q * q, axis=0, keepdims=True) + EPS)
            qn = q * rq
            dqh = dq_ref[0, hh * KR:hh * KR + DH, :] * 0.125
            dqg = dqg + dqh * qn
            dqn = dqh * qg
            dzT_ref[0, sl, :] = (rq * (dqn - qn * jnp.mean(dqn * qn, axis=0, keepdims=True))).astype(BF16)
            k = zqk_ref[A + hh * DH:A + (hh + 1) * DH, :]
            rk = lax.rsqrt(jnp.mean(k * k, axis=0, keepdims=True) + EPS)
            kn = k * rk
            dkh = dk_ref[0, sl, :] * (1.0 / LOG2E)
            dkg = dkg + dkh * kn
            dkn = dkh * kg
            dzT_ref[0, A + hh * DH:A + (hh + 1) * DH, :] = (
                rk * (dkn - kn * jnp.mean(dkn * kn, axis=0, keepdims=True))).astype(BF16)
            fbuf[hh:hh + 1, :] = dq_ref[0, hh * KR + DH:hh * KR + DH + 1, :] - dfk_ref[hh, 0, 0:1, :]
        dqg_ref[...] = dqg_ref[...] + _lanesum(dqg)
        dkg_ref[...] = dkg_ref[...] + _lanesum(dkg)
        dzT_ref[0, 2 * A:3 * A, :] = dv_ref[0].astype(BF16)

        dF = fbuf[...]
        a1, a2, a3 = _split3(dF)
        tr = tri_ref[...]
        dlogf = _nn(a1, tr) + _nn(a2, tr) + _nn(a3, tr) + carry[...]
        carry[...] = carry[...] + _lanesum(dF)
        dfg = dlogf * _sigmoid(-(fg_ref[...] + bf_ref[...]))
        dbf_ref[...] = dbf_ref[...] + _lanesum(dfg)
        dzT_ref[0, 3 * A:ZT_ROWS, :] = jnp.concatenate([dfg, jnp.zeros((8, B), F32)], axis=0).astype(BF16)

    rrow = lambda w: pl.BlockSpec((B, w), lambda p: (NB - 1 - p, 0))
    rts = lambda r: pl.BlockSpec((1, r, B), lambda p: (NB - 1 - p, 0, 0))
    return _pc(
        body, name="head_bwd", grid=(NB,),
        in_specs=[rrow(CW), pl.BlockSpec((B, CW), lambda p: (jnp.minimum(NB - p, NB - 1), 0)),
                  rrow(CW), pl.BlockSpec((B, CW), lambda p: (jnp.maximum(NB - 2 - p, 0), 0)),
                  rrow(2 * CW), pl.BlockSpec((2 * A, B), lambda p: (0, NB - 1 - p)),
                  pl.BlockSpec((8, B), lambda p: (0, NB - 1 - p)), rts(H * KR), rts(A), rts(A),
                  pl.BlockSpec((H, 1, 8, B), lambda p: (0, NB - 1 - p, 0, 0)),
                  _const((32, CW)), _const((DH, 1)), _const((DH, 1)), _const((8, 1)), _const((B, B))],
        out_specs=[rrow(2 * CW), rts(ZT_ROWS), _const((32, CW)), _const((8, CW)), _const((DH, 1)),
                   _const((DH, 1)), _const((8, 1))],
        out_shape=[_sds((S, 2 * CW), BF16), _sds((NB, ZT_ROWS, B), BF16), _sds((32, CW)), _sds((8, CW)),
                   _sds((DH, 1)), _sds((DH, 1)), _sds((8, 1))],
        scratch_shapes=[pltpu.VMEM((B + HALO + 8, CW), F32), pltpu.VMEM((8, B + HALO, CW), F32),
                        pltpu.VMEM((B, CW), F32), pltpu.VMEM((8, 1), F32), pltpu.VMEM((8, B), F32)],
        compiler_params=_cp(("arbitrary",)),
    )(du1, du1, u0, u0, alg, zqk, fgT, dqT, dkT, dvT, dfk, conv_w, qg_col, kg_col, bf_col, tri_lo)


def in_bwd_w(x, dzr, dzT, n1g, sc1, sh1):
    S = x.shape[0]
    B = BLK
    NB = S // B

    def body(x_ref, dzr_ref, dzT_ref, g_ref, sc_ref, sh_ref, dwT_hbm, dwr_hbm, dwT_acc, dwr_acc, sem):
        i = pl.program_id(0)

        @pl.when(i == 0)
        def _():
            dwT_acc[...] = jnp.zeros_like(dwT_acc)
            dwr_acc[...] = jnp.zeros_like(dwr_acc)

        h1 = _norm_mod(x_ref[...], g_ref[...], sc_ref[...], sh_ref[...])[2]
        dwT_acc[...] = dwT_acc[...] + _nn(dzT_ref[0], h1.astype(BF16))
        dwr_acc[...] = dwr_acc[...] + _nn(h1.T.astype(BF16), dzr_ref[...])

        @pl.when(i == NB - 1)
        def _():
            for src, dst in ((dwT_acc, dwT_hbm), (dwr_acc, dwr_hbm)):
                cp = pltpu.make_async_copy(src, dst, sem)
                cp.start()
                cp.wait()

    row = lambda w: pl.BlockSpec((B, w), lambda i: (i, 0))
    return _pc(
        body, name="in_bwd_w", grid=(NB,),
        in_specs=[row(D), row(2 * CW), pl.BlockSpec((1, ZT_ROWS, B), lambda i: (i, 0, 0)),
                  _const((1, D)), _const((1, D)), _const((1, D))],
        out_specs=[ANY, ANY],
        out_shape=[_sds((ZT_ROWS, D)), _sds((D, 2 * CW))],
        scratch_shapes=[pltpu.VMEM((ZT_ROWS, D), F32), pltpu.VMEM((D, 2 * CW), F32),
                        pltpu.SemaphoreType.DMA(())],
        compiler_params=_cp(("arbitrary",)),
    )(x, dzr, dzT, n1g, sc1, sh1)


def in_bwd_x(x, dx1, dzr, dzT, n1g, sc1, sh1, w_qkvf, wr, late_g):
    S = x.shape[0]
    B = BLK
    NB = S // B

    def body(x_ref, dx1_ref, dzr_ref, dzT_ref, g_ref, sc_ref, sh_ref, w_hbm, wr_hbm, lg_ref,
             gx_ref, v_ref, recv_ref, w_v, wr_v, sem, send_sems, recv_sems, local_sem):
        i = pl.program_id(0)

        @pl.when(i == 0)
        def _():
            _start_all(_direct_copies(True, lg_ref, recv_ref, send_sems, recv_sems, local_sem))
            for src, dst in ((w_hbm, w_v), (wr_hbm, wr_v)):
                cp = pltpu.make_async_copy(src, dst, sem)
                cp.start()
                cp.wait()
            v_ref[...] = jnp.zeros_like(v_ref)

        g = g_ref[...]
        sc = sc_ref[...]
        r1, xh, _ = _norm_mod(x_ref[...], g, sc, sh_ref[...])
        dh1 = _nt(dzr_ref[...], wr_v[...]) + _nn(w_v[...], dzT_ref[0]).T
        dhx = dh1 * xh
        v_ref[0:1, :] = v_ref[0:1, :] + _colsum(dh1)
        v_ref[1:2, :] = v_ref[1:2, :] + _colsum(dhx) * g
        v_ref[2:3, :] = v_ref[2:3, :] + _colsum(dhx) * (1.0 + sc)
        dxh = dh1 * (g * (1.0 + sc))
        gx_ref[...] = dx1_ref[...] + r1 * (dxh - xh * _rowmean(dxh * xh))

        @pl.when(i == NB - 1)
        def _():
            _wait_all(_direct_copies(True, lg_ref, recv_ref, send_sems, recv_sems, local_sem))

    row = lambda w: pl.BlockSpec((B, w), lambda i: (i, 0))
    return _pc(
        body, name="in_bwd_x", grid=(NB,),
        in_specs=[row(D), row(D), row(2 * CW), pl.BlockSpec((1, ZT_ROWS, B), lambda i: (i, 0, 0)),
                  _const((1, D)), _const((1, D)), _const((1, D)), ANY, ANY, ANY],
        out_specs=[row(D), _const((8, D)), ANY],
        out_shape=[_sds((S, D)), _sds((8, D)), _sds(late_g.shape, late_g.dtype)],
        scratch_shapes=[pltpu.VMEM((D, ZT_ROWS), BF16), pltpu.VMEM((D, 2 * CW), BF16),
                        pltpu.SemaphoreType.DMA(())] + COMM_SEMS,
        compiler_params=_cp(("arbitrary",)),
    )(x, dx1, dzr, dzT, n1g, sc1, sh1, w_qkvf, wr, late_g)


def _adam_math(g, w, m, v):
    m = ADAM_B1 * m + (1.0 - ADAM_B1) * g
    v = ADAM_B2 * v + (1.0 - ADAM_B2) * (g * g)
    m_hat = m / (1.0 - ADAM_B1 ** ADAM_STEP)
    v_hat = v / (1.0 - ADAM_B2 ** ADAM_STEP)
    delta = -ADAM_LR * (m_hat / (jnp.sqrt(v_hat) + ADAM_EPS) + ADAM_WD * w)
    return delta, m, v


def _row_tile(R):
    for t in (1024, 512, 256, 128, 64, 32, 16, 8):
        if R % t == 0:
            return t
    return R


def sum_slots(parts, name):
    K, R, C = parts.shape
    T = _row_tile(R)

    def body(p_ref, o_ref):
        s = p_ref[0]
        for k in range(1, K):
            s = s + p_ref[k]
        o_ref[...] = s

    return _pc(body, name=name, grid=(R // T,),
               in_specs=[pl.BlockSpec((K, T, C), lambda i: (0, i, 0))],
               out_specs=pl.BlockSpec((T, C), lambda i: (i, 0)), out_shape=_sds((R, C)),
               compiler_params=_cp(("arbitrary",)))(parts)


def adamw_slots(parts, w, m, v, name):
    K, R, C = parts.shape
    T = R
    while K * T * C * 4 > (8 << 20) and T % 16 == 0:
        T //= 2

    def body(p_ref, w_ref, m_ref, v_ref, g_ref, d_ref, nm_ref, nv_ref):
        g = p_ref[0].astype(F32)
        for k in range(1, K):
            g = g + p_ref[k].astype(F32)
        g_ref[...] = g
        d_ref[...], nm_ref[...], nv_ref[...] = _adam_math(g, w_ref[...], m_ref[...], v_ref[...])

    t2 = pl.BlockSpec((T, C), lambda i: (i, 0))
    return _pc(body, name=name, grid=(R // T,),
               in_specs=[pl.BlockSpec((K, T, C), lambda i: (0, i, 0)), t2, t2, t2],
               out_specs=[t2, t2, t2, t2], out_shape=[_sds((R, C))] * 4,
               compiler_params=_cp(("arbitrary",)))(parts, w, m, v)


def adamw_many(gs, ws, ms, vs, name):
    n = len(ws)

    def body(*refs):
        outs = refs[4 * n:]
        for q in range(n):
            d, nm, nv = _adam_math(refs[q][...], refs[n + q][...], refs[2 * n + q][...], refs[3 * n + q][...])
            outs[q][...] = d
            outs[n + q][...] = nm
            outs[2 * n + q][...] = nv

    res = _pc(body, name=name, grid=(1,),
              in_specs=[_const(a.shape) for a in list(gs) + list(ws) + list(ms) + list(vs)],
              out_specs=[_const(w.shape) for w in ws] * 3, out_shape=[_sds(w.shape) for w in ws] * 3,
              compiler_params=_cp())(*gs, *ws, *ms, *vs)
    return res[0:n], res[n:2 * n], res[2 * n:3 * n]


def ada_grad_adamw(cT, dmod_cols, w, m, v):
    NCOL = w.shape[1]

    def body(cT_ref, dm_ref, w_ref, m_ref, v_ref, g_ref, d_ref, nm_ref, nv_ref):
        def term(b):
            cv = cT_ref[b]
            return (cv * _sigmoid(cv)) * dm_ref[b:b + 1, :]

        g = term(0)
        for b in range(1, N_DEV):
            g = g + term(b)
        g_ref[...] = g
        d_ref[...], nm_ref[...], nv_ref[...] = _adam_math(g, w_ref[...], m_ref[...], v_ref[...])

    full = _const((D, NCOL))
    return _pc(body, name="ada_grad_adamw", grid=(1,),
               in_specs=[_const((N_DEV, D, 1)), _const((N_DEV, NCOL)), full, full, full],
               out_specs=[full, full, full, full], out_shape=[_sds((D, NCOL))] * 4,
               compiler_params=_cp())(cT, dmod_cols, w, m, v)


def _pack_rows(vecs, rows=None):
    flat = jnp.concatenate([jnp.ravel(v) for v in vecs])
    n = flat.shape[0]
    r = -(-n // 1024) * 8 if rows is None else rows
    return jnp.pad(flat, (0, r * 128 - n)).reshape(r, 128)


def _unpack_rows(packed, shapes):
    flat = packed.reshape(-1)
    out, off = [], 0
    for s in shapes:
        n = 1
        for d in s:
            n *= d
        out.append(flat[off:off + n].reshape(s))
        off += n
    return out


def _cols_to_shards(w, cols):
    rows = w.shape[0]
    return w.reshape(rows, N_DEV, cols).transpose(1, 0, 2).reshape(N_DEV, rows * cols)


def kernel(x, c, w_ada, b_ada, norm1_g, w_in, q_norm_g, k_norm_g, b_f, conv_w, conv_b, conv_ln_g, conv_ln_b, beta_attn, beta_conv, w_out, norm2_g, w_ff1, w_ff2, loss_target, m_w_ada, m_b_ada, m_norm1_g, m_w_in, m_q_norm_g, m_k_norm_g, m_b_f, m_conv_w, m_conv_b, m_conv_ln_g, m_conv_ln_b, m_beta_attn, m_beta_conv, m_w_out, m_norm2_g, m_w_ff1, m_w_ff2, v_w_ada, v_b_ada, v_norm1_g, v_w_in, v_q_norm_g, v_k_norm_g, v_b_f, v_conv_w, v_conv_b, v_conv_ln_g, v_conv_ln_b, v_beta_attn, v_beta_conv, v_w_out, v_norm2_g, v_w_ff1, v_w_ff2):
    S = x.shape[1]
    B = BLK
    NB = S // B
    me = 4 * lax.axis_index("x") + 2 * lax.axis_index("y") + lax.axis_index("c")
    xs = x[0]
    tgt = loss_target[0]

    ADA_C, IN_C, FF_C, CV_C = w_ada.shape[2], w_in.shape[2], w_ff1.shape[2], conv_w.shape[2]
    OUT_R, FF_R = w_out.shape[1], w_ff2.shape[1]

    g_in, g_c, g_cw = gather_direct([w_in[0].T.astype(BF16), c, conv_w[0]], "ag_weights_early")
    late_w = [w_out[0].astype(BF16), w_ff1[0].astype(BF16), w_ff2[0].astype(BF16)]

    w_in_t = g_in.reshape(N_DEV * IN_C, D)
    c_all = g_c.reshape(N_DEV, D)
    conv_w_full = g_cw.transpose(1, 0, 2).reshape(KC, N_DEV * CV_C)
    conv_w_pad = jnp.pad(conv_w_full, ((0, 32 - KC), (0, 0)))

    wT = jnp.pad(w_in_t[0:3 * A + H], ((0, ZT_ROWS - 3 * A - H), (0, 0)))
    w_qkvf = wT.T
    wr = w_in_t[3 * A + H:].T

    qg_col = q_norm_g.reshape(DH, 1)
    kg_col = k_norm_g.reshape(DH, 1)
    bf_col = b_f.reshape(H, 1)
    beta_a_col = beta_attn.reshape(A, 1)
    ii = lax.broadcasted_iota(jnp.int32, (B, B), 0)
    jj = lax.broadcasted_iota(jnp.int32, (B, B), 1)
    tri_up = (ii <= jj).astype(BF16)
    tri_lo = (ii >= jj).astype(BF16)

    modc = mod_columns(c_all, w_ada[0], lax.dynamic_slice(b_ada, (0, me * ADA_C), (1, ADA_C)))
    gm, = gather_direct([modc], "ag_mod")
    mod = lax.dynamic_index_in_dim(gm, me, axis=1, keepdims=False).reshape(1, N_DEV * ADA_C)
    sh1, sc1, g1, sh2, sc2, g2 = [mod[:, k * D:(k + 1) * D] for k in range(6)]
    qTa, kT, kaug, vT, vaug, zqk, fgT, alg, u0, stat = fwd_in(xs, norm1_g, sc1, sh1, wT, wr, qg_col,
                                                             kg_col, bf_col, tri_up)
    fmax = stat[:, :, 0].T
    fmin = stat[:, :, 1].T
    qk_max = jnp.sqrt(jnp.max(stat[:, :, 2], axis=0) * jnp.max(stat[:, :, 3], axis=0))
    thr = -(PRUNE + (2.02 / LOG2E) * qk_max)
    oT, lse, g_wo, w1_shards, g_w2 = attn_fwd(fmax, fmin, thr, qTa, kaug, vT, late_w)
    wo_full = g_wo.reshape(D, D)
    w2_full = g_w2.reshape(FF, D)
    x1, o, u1 = conv_merge_out(u0, xs, oT, conv_w_pad, conv_b, conv_ln_g, conv_ln_b, beta_conv,
                               beta_a_col, wo_full, g1)
    dy, loss_part, dg2, ra = mlp_fwd_loss(x1, tgt, norm2_g, sc2, sh2, g2, w1_shards, w2_full)

    dw1s, dw2, dh2p = mlp_bwd(x1, dy, ra, norm2_g, sc2, sh2, g2, w1_shards, w2_full)
    dx1, doTa, delta, du1, dwo, v1, v2, dba = merge_bwd(dh2p, x1, dy, o, oT, u1, norm2_g, sc2, g1,
                                                         conv_ln_g, conv_ln_b, beta_conv, beta_a_col, wo_full)
    early_g = [dwo.reshape(N_DEV, OUT_R, D), dw1s, dw2.reshape(N_DEV, FF_R, D)]
    dqT, dkT, dvT, dfk, r_out, r_f1, r_f2 = attn_bwd(fmax, fmin, thr, qTa, kaug, kT, vaug, doTa, lse,
                                                     delta, early_g)
    dzr, dzT, dcw, vc, dqg, dkg, dbf = head_bwd(du1, u0, alg, zqk, fgT, dqT, dkT, dvT, dfk, conv_w_pad,
                                                qg_col, kg_col, bf_col, tri_lo)
    dwT, dwr = in_bwd_w(xs, dzr, dzT, norm1_g, sc1, sh1)

    dw_in = jnp.concatenate([dwT.T[:, 0:3 * A + H], dwr], axis=1)
    late_g = _cols_to_shards(dw_in, IN_C)
    rows_b = -(-late_g.shape[1] // (128 * 256)) * 256
    late_g = jnp.pad(late_g, ((0, 0), (0, rows_b * 128 - late_g.shape[1]))).astype(BF16)
    grad_x, v0, recv_b = in_bwd_x(xs, dx1, dzr, dzT, norm1_g, sc1, sh1, w_qkvf, wr,
                                  late_g.reshape(N_DEV, rows_b, 128))

    dmod = jnp.concatenate([v0[0], v0[1], v1[3], v1[0], v1[1], dg2[0]])
    small1 = _pack_rows([dmod, v0[2], v1[2], dcw, vc[0], v2[1], v2[2], v2[0], dba, dqg, dkg,
                         jnp.pad(dbf.reshape(-1), (0, 120)), jnp.pad(loss_part.reshape(-1), (0, 127))])
    gs1, = gather_direct([small1], "ag_small_bwd")
    tot = sum_slots(gs1, "sum_small")
    (g_b_ada, g_n1, g_n2, g_cw, g_cb, g_lng, g_lnb, g_bc, g_ba, g_qg, g_kg, g_bf, loss_v) = _unpack_rows(
        tot, [(1, 6 * D), (1, D), (1, D), (32, CW), (1, CW), (1, CW), (1, CW), (1, CW), (1, A),
              (1, DH), (1, DH), (1, 128), (1, 128)])
    loss = loss_v[0, 0]
    g_bf = g_bf[:, 0:H]
    g_cw_mine = lax.dynamic_slice(g_cw[0:KC], (0, me * CV_C), (KC, CV_C)).reshape(1, KC, CV_C)

    small_names = [(b_ada, m_b_ada, v_b_ada, g_b_ada), (norm1_g, m_norm1_g, v_norm1_g, g_n1),
                   (q_norm_g, m_q_norm_g, v_q_norm_g, g_qg), (k_norm_g, m_k_norm_g, v_k_norm_g, g_kg),
                   (b_f, m_b_f, v_b_f, g_bf), (conv_w, m_conv_w, v_conv_w, g_cw_mine),
                   (conv_b, m_conv_b, v_conv_b, g_cb), (conv_ln_g, m_conv_ln_g, v_conv_ln_g, g_lng),
                   (conv_ln_b, m_conv_ln_b, v_conv_ln_b, g_lnb), (beta_attn, m_beta_attn, v_beta_attn, g_ba),
                   (beta_conv, m_beta_conv, v_beta_conv, g_bc), (norm2_g, m_norm2_g, v_norm2_g, g_n2)]
    sgs = [t[3].reshape(t[0].shape) for t in small_names]
    sds, sms, svs = adamw_many(sgs, *[[t[k] for t in small_names] for k in range(3)], "adamw_small")

    dmod_all = gs1[:, 0:48, :].reshape(N_DEV, N_DEV, ADA_C)
    dmod_cols = lax.dynamic_index_in_dim(dmod_all, me, axis=1, keepdims=False)
    ga, da, ma_, va_ = ada_grad_adamw(c_all.reshape(N_DEV, D, 1), dmod_cols, w_ada[0], m_w_ada[0], v_w_ada[0])

    res_b = adamw_slots(recv_b, *[_pack_rows(ws, rows=rows_b) for ws in ([w_in], [m_w_in], [v_w_in])],
                        "adamw_in")
    late = []
    for nm, parts, (w_, m_, v_) in (("adamw_out", r_out, (w_out, m_w_out, v_w_out)),
                                    ("adamw_ff1", r_f1, (w_ff1, m_w_ff1, v_w_ff1)),
                                    ("adamw_ff2", r_f2, (w_ff2, m_w_ff2, v_w_ff2))):
        late.append([r[None] for r in adamw_slots(parts, w_[0], m_[0], v_[0], nm)])
    bgs, bds, bms, bvs = [_unpack_rows(res_b[q], [(1, D, IN_C)]) + [late[0][q], late[1][q], late[2][q]]
                          for q in range(4)]

    def assemble(small, ada, bigs):
        (b_ada_, n1_, qg_, kg_, bf_, cw_, cb_, lng_, lnb_, ba_, bc_, n2_) = small
        return [ada.reshape(1, D, ADA_C), b_ada_, n1_, bigs[0], qg_, kg_, bf_, cw_, cb_, lng_, lnb_, ba_, bc_,
                bigs[1], n2_, bigs[2], bigs[3]]

    return (loss, grad_x.reshape(1, S, D), *assemble(sgs, ga, bgs), *assemble(sds, da, bds),
            *assemble(sms, ma_, bms), *assemble(svs, va_, bvs))
```

```python
import functools

import jax
import jax.numpy as jnp
from jax import lax
from jax.experimental import pallas as pl
from jax.experimental.pallas import tpu as pltpu

F32 = jnp.float32
BF16 = jnp.bfloat16

D = 1024
A = 512
CW = 512
H = 8
DH = 64
FF = 4096
KC = 31
HALO = 32
KR = 80
ZT_ROWS = 1552
FCH = 1024
EPS = 1e-6
BLK = 512
N_DEV = 8
VMEM_LIMIT = 56 * 1024 * 1024

ADAM_LR = 0.001
ADAM_B1 = 0.9
ADAM_B2 = 0.999
ADAM_EPS = 1e-08
ADAM_WD = 0.01
ADAM_STEP = 10

MESH = pl.DeviceIdType.MESH
ANY = pl.BlockSpec(memory_space=pl.ANY)
SMEM = pl.BlockSpec(memory_space=pltpu.SMEM)
PRUNE = 105.0
LOG2E = 1.4426950408889634


def _pc(body, **kw):
    return pl.pallas_call(body, **kw)


def _cp(sem=None):
    return pltpu.CompilerParams(dimension_semantics=sem, vmem_limit_bytes=VMEM_LIMIT)


def _sds(shape, dtype=F32):
    return jax.ShapeDtypeStruct(shape, dtype)


def _const(shape):
    n = len(shape)
    return pl.BlockSpec(shape, lambda *a: (0,) * n)


def _nt(a, b):
    return lax.dot_general(a, b, (((1,), (1,)), ((), ())), preferred_element_type=F32)


def _nn(a, b):
    return jnp.dot(a, b, preferred_element_type=F32)


def _sigmoid(v):
    return 1.0 / (1.0 + jnp.exp(-v))


def _split3(v):
    a1 = v.astype(BF16)
    r1 = v - a1.astype(F32)
    a2 = r1.astype(BF16)
    a3 = (r1 - a2.astype(F32)).astype(BF16)
    return a1, a2, a3


def _rowmean(v):
    return jnp.mean(v, axis=-1, keepdims=True)


def _colsum(v):
    return jnp.sum(v, axis=0, keepdims=True)


def _lanesum(v):
    return jnp.sum(v, axis=-1, keepdims=True)


def _coords():
    return lax.axis_index("x"), lax.axis_index("y"), lax.axis_index("c")


def _direct_copies(scatter, src_ref, dst_ref, send_sems, recv_sems, local_sem):
    mx, my, mc = _coords()
    me = 4 * mx + 2 * my + mc
    if scatter:
        local = pltpu.make_async_copy(src_ref.at[me], dst_ref.at[0], local_sem)
    else:
        local = pltpu.make_async_copy(src_ref, dst_ref.at[me], local_sem)
    remote = []
    for r in range(1, N_DEV):
        px = 1 - mx if r & 4 else mx
        py = 1 - my if r & 2 else my
        pcc = 1 - mc if r & 1 else mc
        remote.append(pltpu.make_async_remote_copy(
            src_ref=src_ref.at[4 * px + 2 * py + pcc] if scatter else src_ref,
            dst_ref=dst_ref.at[r] if scatter else dst_ref.at[me],
            send_sem=send_sems.at[r - 1], recv_sem=recv_sems.at[r - 1],
            device_id=(px, py, pcc), device_id_type=MESH))
    return [local] + remote


def _start_all(copies):
    for cp in copies:
        cp.start()


def _wait_all(copies):
    for cp in copies[1:]:
        cp.wait()
    copies[0].wait()


COMM_SEMS = [pltpu.SemaphoreType.DMA((7,)), pltpu.SemaphoreType.DMA((7,)), pltpu.SemaphoreType.DMA(())]


def gather_direct(arrays, name):
    NG = len(arrays)

    def body(*refs):
        sems = refs[2 * NG:]
        sets = [_direct_copies(False, refs[q], refs[NG + q], *sems[3 * q:3 * q + 3]) for q in range(NG)]
        for copies in sets:
            _start_all(copies)
        for copies in sets:
            _wait_all(copies)

    return _pc(body, name=name, out_shape=[_sds((N_DEV,) + a.shape, a.dtype) for a in arrays],
               in_specs=[ANY] * NG, out_specs=[ANY] * NG, scratch_shapes=COMM_SEMS * NG)(*arrays)


def gather_two_level(arrays, name):
    NG = len(arrays)

    def body(*refs):
        mx, my, mc = _coords()
        me, sibling = (mx, my, mc), (mx, my, 1 - mc)
        chips = [(1 - mx, my), (mx, 1 - my), (1 - mx, 1 - my)]
        plans = []
        for q in range(NG):
            x_ref, out_ref = refs[q], refs[NG + q]
            send_sems, recv_sems, local_sem = refs[2 * NG + 3 * q:2 * NG + 3 * q + 3]

            def rows(block, out_ref=out_ref):
                return out_ref.at[4 * block[0] + 2 * block[1] + block[2]]

            def copy(k, block, to, src=None, rows=rows, send_sems=send_sems, recv_sems=recv_sems):
                return pltpu.make_async_remote_copy(
                    src_ref=rows(block) if src is None else src, dst_ref=rows(block),
                    send_sem=send_sems.at[k], recv_sem=recv_sems.at[k], device_id=to, device_id_type=MESH)

            mine = pltpu.make_async_copy(x_ref, rows(me), local_sem)
            first = [copy(0, me, sibling, src=x_ref)]
            first += [copy(1 + t, me, (*chip, mc), src=x_ref) for t, chip in enumerate(chips)]
            passed = [copy(4 + t, (*chip, mc), sibling) for t, chip in enumerate(chips)]
            _start_all([mine] + first)
            plans.append((copy, mine, first, passed))
        for copy, mine, first, passed in plans:
            for t, chip in enumerate(chips):
                copy(1 + t, (*chip, mc), me).wait_recv()
                passed[t].start()
        for copy, mine, first, passed in plans:
            copy(0, sibling, me).wait_recv()
            for t, chip in enumerate(chips):
                copy(4 + t, (*chip, 1 - mc), me).wait_recv()
            for cp in first + passed:
                cp.wait_send()
            mine.wait()

    return _pc(body, name=name, out_shape=[_sds((N_DEV,) + a.shape, a.dtype) for a in arrays],
               in_specs=[ANY] * NG, out_specs=[ANY] * NG, scratch_shapes=COMM_SEMS * NG)(*arrays)


def mod_columns(c_all, w_shard, b_cols):
    NCOL = w_shard.shape[1]

    def body(c_ref, w_ref, b_ref, o_ref):
        cv = c_ref[...]
        o_ref[...] = _nn((cv * _sigmoid(cv)).astype(BF16), w_ref[...].astype(BF16)) + b_ref[...]

    return _pc(body, name="mod_columns", out_shape=_sds((N_DEV, NCOL)),
               in_specs=[_const((N_DEV, D)), _const((D, NCOL)), _const((1, NCOL))],
               out_specs=_const((N_DEV, NCOL)), grid=(1,), compiler_params=_cp())(c_all, w_shard, b_cols)


def _norm_mod(xv, g, sc, sh):
    r = lax.rsqrt(_rowmean(xv * xv) + EPS)
    xh = xv * r
    return r, xh, xh * (g * (1.0 + sc)) + sh


def _log_sigmoid(v):
    e = jnp.exp(-jnp.abs(v))
    l1p = jnp.where(e < 1e-4, e * (1.0 - 0.5 * e), jnp.log(1.0 + e))
    return jnp.minimum(v, 0.0) - l1p


def fwd_in(x, n1g, sc1, sh1, wT, wr, qg_col, kg_col, bf_col, tri):
    S = x.shape[0]
    B = BLK
    NB = S // B

    def body(x_ref, g_ref, sc_ref, sh_ref, wT_ref, wr_ref, qg_ref, kg_ref, bf_ref, tri_ref,
             qTa_ref, kT_ref, kaug_ref, vT_ref, vaug_ref, zqk_ref, fgT_ref, alg_ref, u0_ref, stat_ref,
             carry, tbuf, fs, nq, nk):
        i = pl.program_id(0)

        @pl.when(i == 0)
        def _():
            carry[...] = jnp.zeros_like(carry)

        _, _, h = _norm_mod(x_ref[...], g_ref[...], sc_ref[...], sh_ref[...])
        hb = h.astype(BF16)
        zT = _nt(wT_ref[...], hb)
        zr = _nn(hb, wr_ref[...])
        zqk_ref[...] = zT[0:2 * A]
        fgT = zT[3 * A:3 * A + 8]
        fgT_ref[...] = fgT
        alg_ref[...] = zr
        u0_ref[...] = zr[:, 0:CW] * _sigmoid(zr[:, CW:2 * CW])

        logf = _log_sigmoid(fgT + bf_ref[...])
        a1, a2, a3 = _split3(logf)
        tr = tri_ref[...]
        F = _nn(a1, tr) + _nn(a2, tr) + _nn(a3, tr) + carry[...]
        carry[...] = carry[...] + _lanesum(logf)
        p1, p2, p3 = _split3(F * LOG2E)
        n1, n2, n3 = _split3(F * (-LOG2E))
        for k, v in enumerate((p1, p2, p3, n1, n2, n3)):
            fs[k] = v.astype(F32)

        rowi = lax.broadcasted_iota(jnp.int32, (8, B), 0)
        zeros_tail = jnp.zeros((128 - DH - 8, B), F32)
        ones_row = jnp.where(lax.broadcasted_iota(jnp.int32, (KR - DH, B), 0) == 0, 1.0, 0.0).astype(BF16)
        for hh in range(H):
            sl = slice(hh * DH, (hh + 1) * DH)
            q = zT[sl]
            k = zT[A + hh * DH:A + (hh + 1) * DH]
            v = zT[2 * A + hh * DH:2 * A + (hh + 1) * DH]
            qh = q * lax.rsqrt(jnp.mean(q * q, axis=0, keepdims=True) + EPS) * qg_ref[...] * (0.125 * LOG2E)
            kh = k * lax.rsqrt(jnp.mean(k * k, axis=0, keepdims=True) + EPS) * kg_ref[...]
            frow = [fs[kk, hh:hh + 1, :] for kk in range(6)]
            qx = jnp.where(rowi < 3, 1.0, jnp.where(rowi == 3, frow[0], jnp.where(
                rowi == 4, frow[1], jnp.where(rowi == 5, frow[2], 0.0))))
            kx = jnp.where(rowi == 0, frow[3], jnp.where(rowi == 1, frow[4], jnp.where(
                rowi == 2, frow[5], jnp.where(rowi < 6, 1.0, 0.0))))
            tbuf[0:DH, :] = qh
            tbuf[DH:DH + 8, :] = qx
            tbuf[DH + 8:128, :] = zeros_tail
            qTa_ref[0, hh * 128:(hh + 1) * 128, :] = tbuf[...].astype(BF16)
            tbuf[0:DH, :] = kh
            tbuf[DH:DH + 8, :] = kx
            kaug_ref[hh] = tbuf[...].T.astype(BF16)
            kT_ref[0, hh * KR:hh * KR + DH, :] = kh.astype(BF16)
            kT_ref[0, hh * KR + DH:(hh + 1) * KR, :] = ones_row
            tbuf[0:DH, :] = v
            tbuf[DH:DH + 8, :] = jnp.zeros((8, B), F32)
            vaug_ref[hh] = tbuf[...].T.astype(BF16)
            vT_ref[0, sl, :] = v.astype(BF16)
            nq[hh:hh + 1, :] = jnp.max(_colsum(qh * qh), axis=1, keepdims=True)
            nk[hh:hh + 1, :] = jnp.max(_colsum(kh * kh), axis=1, keepdims=True)

        lane = lax.broadcasted_iota(jnp.int32, (8, 128), 1)
        stat_ref[0] = jnp.where(lane == 0, jnp.max(F, axis=1, keepdims=True), jnp.where(
            lane == 1, jnp.min(F, axis=1, keepdims=True), jnp.where(
                lane == 2, nq[...], jnp.where(lane == 3, nk[...], 0.0))))

    row = lambda w: pl.BlockSpec((B, w), lambda i: (i, 0))
    tsp = lambda r: pl.BlockSpec((1, r, B), lambda i: (i, 0, 0))
    return _pc(
        body, name="fwd_in", grid=(NB,),
        in_specs=[row(D), _const((1, D)), _const((1, D)), _const((1, D)), _const((ZT_ROWS, D)),
                  _const((D, 2 * CW)), _const((DH, 1)), _const((DH, 1)), _const((8, 1)), _const((B, B))],
        out_specs=[tsp(H * 128), tsp(H * KR), pl.BlockSpec((H, B, 128), lambda i: (0, i, 0)), tsp(A),
                   pl.BlockSpec((H, B, 128), lambda i: (0, i, 0)),
                   pl.BlockSpec((2 * A, B), lambda i: (0, i)), pl.BlockSpec((8, B), lambda i: (0, i)),
                   row(2 * CW), row(CW), pl.BlockSpec((1, 8, 128), lambda i: (i, 0, 0))],
        out_shape=[_sds((NB, H * 128, B), BF16), _sds((NB, H * KR, B), BF16), _sds((H, S, 128), BF16),
                   _sds((NB, A, B), BF16), _sds((H, S, 128), BF16), _sds((2 * A, S)), _sds((8, S)),
                   _sds((S, 2 * CW)), _sds((S, CW)), _sds((NB, 8, 128))],
        scratch_shapes=[pltpu.VMEM((8, 1), F32), pltpu.VMEM((128, B), F32), pltpu.VMEM((6, 8, B), F32),
                        pltpu.VMEM((8, 1), F32), pltpu.VMEM((8, 1), F32)],
        compiler_params=_cp(("arbitrary",)),
    )(x, n1g, sc1, sh1, wT, wr, qg_col, kg_col, bf_col, tri)


def _first_key_block(top, fmin_ref, cut, h, i):
    return lax.while_loop(
        lambda j: jnp.logical_and(j > 0, top - fmin_ref[h, jnp.maximum(j - 1, 0)] >= cut),
        lambda j: j - 1, i)


def _causal_keep(B):
    return lax.broadcasted_iota(jnp.int32, (B, B), 0) <= lax.broadcasted_iota(jnp.int32, (B, B), 1)


def _keep_right_half(B):
    hb = B // 2
    return lax.broadcasted_iota(jnp.int32, (B, hb), 0) <= lax.broadcasted_iota(jnp.int32, (B, hb), 1) + hb


def _keep_top_half(B):
    hb = B // 2
    return lax.broadcasted_iota(jnp.int32, (hb, B), 0) <= lax.broadcasted_iota(jnp.int32, (hb, B), 1)


def attn_fwd(fmax, fmin, thr, qTa, kaug, vT, late_w):
    NB, _, B = qTa.shape
    S = NB * B
    NG = len(late_w)
    HB = B // 2

    def body(fmax_ref, fmin_ref, thr_ref, q_ref, k_ref, v_ref, *rest):
        w_refs = rest[0:NG]
        o_ref, lse_ref = rest[NG:NG + 2]
        gw_refs = rest[NG + 2:2 * NG + 2]
        s0, s1, m_ref, l_ref, acc_ref = rest[2 * NG + 2:2 * NG + 7]
        sems = rest[2 * NG + 7:]
        h = pl.program_id(0)
        i = pl.program_id(1)

        def gathers():
            return [_direct_copies(False, w_refs[q], gw_refs[q], *sems[3 * q:3 * q + 3]) for q in range(NG)]

        @pl.when(jnp.logical_and(h == 0, i == 0))
        def _():
            for copies in gathers():
                _start_all(copies)

        jlo = _first_key_block(fmax_ref[h, i], fmin_ref, thr_ref[h], h, i)
        n = i - jlo

        def scores(j, s_ref):
            s_ref[...] = _nn(k_ref[0, pl.ds(pl.multiple_of(j * B, B), B), :], q_ref[0])

        def softmax_cols(s, j, cols, rows):
            m = m_ref[:, cols]
            mn = jnp.maximum(m, jnp.max(s, axis=0, keepdims=True))
            a = jnp.exp2(m - mn)
            p = jnp.exp2(s - mn)
            m_ref[:, cols] = mn
            l_ref[:, cols] = a * l_ref[:, cols] + _colsum(p)
            acc_ref[:, cols] = a * acc_ref[:, cols] + _nn(v_ref[j][:, rows], p.astype(BF16))

        def softmax_step(s_ref, j, masked):
            if not masked:
                softmax_cols(s_ref[...], j, slice(None), slice(None))
                return
            lo, hi = slice(0, HB), slice(HB, B)
            softmax_cols(jnp.where(_causal_keep(HB), s_ref[lo, lo], -jnp.inf), j, lo, lo)
            softmax_cols(jnp.where(_keep_right_half(B), s_ref[:, hi], -jnp.inf), j, hi, slice(None))

        m_ref[...] = jnp.full((1, B), -jnp.inf, F32)
        l_ref[...] = jnp.zeros((1, B), F32)
        acc_ref[...] = jnp.zeros((DH, B), F32)
        scores(jlo, s0)

        def pair(t, carry):
            j = jlo + 2 * t
            scores(j + 1, s1)
            softmax_step(s0, j, False)
            scores(j + 2, s0)
            softmax_step(s1, j + 1, False)
            return carry

        lax.fori_loop(0, n // 2, pair, 0)

        @pl.when(n % 2 == 1)
        def _():
            scores(i, s1)
            softmax_step(s0, i - 1, False)
            softmax_step(s1, i, True)

        @pl.when(n % 2 == 0)
        def _():
            softmax_step(s0, i, True)

        l = l_ref[...]
        o_ref[0] = acc_ref[...] / l
        lse_ref[0, 0] = m_ref[...] + jnp.log2(l)

        @pl.when(jnp.logical_and(h == H - 1, i == NB - 1))
        def _():
            for copies in gathers():
                _wait_all(copies)

    return _pc(
        body, name="attn_fwd", grid=(H, NB),
        in_specs=[SMEM, SMEM, SMEM, pl.BlockSpec((1, 128, B), lambda h, i: (i, h, 0)),
                  pl.BlockSpec((1, S, 128), lambda h, i: (h, 0, 0)),
                  pl.BlockSpec((NB, DH, B), lambda h, i: (0, h, 0))] + [ANY] * NG,
        out_specs=[pl.BlockSpec((1, DH, B), lambda h, i: (i, h, 0)),
                   pl.BlockSpec((1, 1, 1, B), lambda h, i: (h, i, 0, 0))] + [ANY] * NG,
        out_shape=[_sds((NB, A, B)), _sds((H, NB, 1, B))]
        + [_sds((N_DEV,) + w.shape, w.dtype) for w in late_w],
        scratch_shapes=[pltpu.VMEM((B, B), F32), pltpu.VMEM((B, B), F32), pltpu.VMEM((1, B), F32),
                        pltpu.VMEM((1, B), F32), pltpu.VMEM((DH, B), F32)] + COMM_SEMS * NG,
        compiler_params=_cp(("arbitrary", "arbitrary")),
    )(fmax, fmin, thr, qTa, kaug, vT, *late_w)


def attn_bwd(fmax, fmin, thr, qTa, kaug, kT, vaug, doTa, lse, delta, early_g):
    NG = len(early_g)
    NB, _, B = qTa.shape
    QR = 80
    CH = min(256, B)

    def body(fmax_ref, fmin_ref, thr_ref, q_ref, ka_ref, kt_ref, va_ref, do_ref, lse_ref, dl_ref, *rest):
        g_refs = rest[0:NG]
        dq_ref, dk_ref, dv_ref, dfk_ref = rest[NG:NG + 4]
        recv_refs = rest[NG + 4:2 * NG + 4]
        s0, d0, s1, d1, p0, e0, p1, e1, dk_acc, dv_acc = rest[2 * NG + 4:2 * NG + 14]
        sems = rest[2 * NG + 14:]
        h = pl.program_id(0)
        j = pl.program_id(1)

        def exchanges():
            return [_direct_copies(True, g_refs[q], recv_refs[q], *sems[3 * q:3 * q + 3]) for q in range(NG)]

        @pl.when(jnp.logical_and(h == 0, j == 0))
        def _():
            for copies in exchanges():
                _start_all(copies)

        bottom = fmin_ref[h, j]
        cut = thr_ref[h]
        ihi = lax.while_loop(
            lambda i: jnp.logical_and(i < NB - 1, fmax_ref[h, jnp.minimum(i + 1, NB - 1)] - bottom >= cut),
            lambda i: i + 1, j)

        @pl.when(j == 0)
        def _():
            dq_ref[...] = jnp.zeros_like(dq_ref)

        dk_acc[...] = jnp.zeros_like(dk_acc)
        dv_acc[...] = jnp.zeros_like(dv_acc)

        n = ihi - j + 1

        def blk(k):
            return jnp.minimum(j + k, ihi)

        def products(k, s_ref, d_ref):
            i = blk(k)
            s_ref[...] = _nn(ka_ref[0], q_ref[i])
            d_ref[...] = _nn(va_ref[0], do_ref[i])

        def elementwise(k, s_ref, d_ref, p_ref, ds_ref, masked):
            i = blk(k)
            p = jnp.exp2(s_ref[...] - lse_ref[0, i])
            if masked:
                p = jnp.where(_causal_keep(B), p, 0.0)
            p_ref[...] = p.astype(BF16)
            ds_ref[...] = (p * (d_ref[...] - dl_ref[0, i])).astype(BF16)

        def grads(k, p_ref, ds_ref):
            i = j + k
            dsb = ds_ref[...]
            dv_acc[...] = dv_acc[...] + _nt(do_ref[i][0:DH], p_ref[...])
            dk_acc[...] = dk_acc[...] + _nt(q_ref[i][0:QR], dsb)
            dq_ref[i] = dq_ref[i] + _nn(kt_ref[0], dsb)

        def trip(kp, s_p, d_p, ke, s_e, d_e, p_e, e_e, kg, p_g, e_g):
            ip, ie, ig = blk(kp), blk(ke), j + kg
            lse_e = lse_ref[0, ie]
            dl_e = dl_ref[0, ie]
            dq_part = jnp.zeros((KR, B), F32)
            for c in range(B // CH):
                rows = slice(c * CH, (c + 1) * CH)
                s_p[rows, :] = _nn(ka_ref[0, rows, :], q_ref[ip])
                d_p[rows, :] = _nn(va_ref[0, rows, :], do_ref[ip])
                p = jnp.exp2(s_e[rows, :] - lse_e)
                p_e[rows, :] = p.astype(BF16)
                e_e[rows, :] = (p * (d_e[rows, :] - dl_e)).astype(BF16)
                dsb = e_g[rows, :]
                dv_acc[:, rows] = dv_acc[:, rows] + _nt(do_ref[ig][0:DH], p_g[rows, :])
                dk_acc[:, rows] = dk_acc[:, rows] + _nt(q_ref[ig][0:QR], dsb)
                dq_part = dq_part + _nn(kt_ref[0, :, rows], dsb)
            dq_ref[ig] = dq_ref[ig] + dq_part

        HB = B // 2
        lo, hi = slice(0, HB), slice(HB, B)

        def diagonal_products(s_ref, d_ref):
            s_ref[lo, :] = _nn(ka_ref[0, lo, :], q_ref[j])
            d_ref[lo, :] = _nn(va_ref[0, lo, :], do_ref[j])
            s_ref[hi, hi] = _nn(ka_ref[0, hi, :], q_ref[j, :, hi])
            d_ref[hi, hi] = _nn(va_ref[0, hi, :], do_ref[j, :, hi])

        def diagonal_elementwise(s_ref, d_ref, p_ref, ds_ref):
            lse0 = lse_ref[0, j]
            dl0 = dl_ref[0, j]
            p = jnp.where(_keep_top_half(B), jnp.exp2(s_ref[lo, :] - lse0), 0.0)
            p_ref[lo, :] = p.astype(BF16)
            ds_ref[lo, :] = (p * (d_ref[lo, :] - dl0)).astype(BF16)
            p = jnp.where(_causal_keep(HB), jnp.exp2(s_ref[hi, hi] - lse0[:, hi]), 0.0)
            p_ref[hi, hi] = p.astype(BF16)
            ds_ref[hi, hi] = (p * (d_ref[hi, hi] - dl0[:, hi])).astype(BF16)
            p_ref[hi, lo] = jnp.zeros((HB, HB), BF16)
            ds_ref[hi, lo] = jnp.zeros((HB, HB), BF16)

        diagonal_products(s0, d0)
        products(1, s1, d1)
        diagonal_elementwise(s0, d0, p0, e0)

        def pair(t, carry):
            k = 2 * t
            trip(k + 2, s0, d0, k + 1, s1, d1, p1, e1, k, p0, e0)
            trip(k + 3, s1, d1, k + 2, s0, d0, p0, e0, k + 1, p1, e1)
            return carry

        n_pairs = n // 2
        lax.fori_loop(0, n_pairs - 1, pair, 0)
        k_last = 2 * (n_pairs - 1)

        @pl.when(jnp.logical_and(n_pairs >= 1, n % 2 == 0))
        def _():
            elementwise(k_last + 1, s1, d1, p1, e1, False)
            grads(k_last, p0, e0)
            grads(k_last + 1, p1, e1)

        @pl.when(jnp.logical_and(n_pairs >= 1, n % 2 == 1))
        def _():
            trip(k_last + 2, s0, d0, k_last + 1, s1, d1, p1, e1, k_last, p0, e0)
            elementwise(k_last + 2, s0, d0, p0, e0, False)
            grads(k_last + 1, p1, e1)
            grads(k_last + 2, p0, e0)

        @pl.when(n == 1)
        def _():
            grads(0, p0, e0)

        dk_ref[0] = dk_acc[0:DH, :]
        dfk_ref[0, 0] = dk_acc[DH:DH + 8, :]
        dv_ref[0] = dv_acc[...]

        @pl.when(jnp.logical_and(h == H - 1, j == NB - 1))
        def _():
            for copies in exchanges():
                _wait_all(copies)

    per_kv = lambda r: pl.BlockSpec((1, r, B), lambda h, j: (j, h, 0))
    head_all = lambda r: pl.BlockSpec((NB, r, B), lambda h, j: (0, h, 0))
    aug = pl.BlockSpec((1, B, 128), lambda h, j: (h, j, 0))
    stat = pl.BlockSpec((1, NB, 1, B), lambda h, j: (h, 0, 0, 0))
    return _pc(
        body, name="attn_bwd", grid=(H, NB),
        in_specs=[SMEM, SMEM, SMEM, head_all(128), aug, per_kv(KR), aug, head_all(128), stat, stat]
        + [ANY] * NG,
        out_specs=[head_all(KR), per_kv(DH), per_kv(DH),
                   pl.BlockSpec((1, 1, 8, B), lambda h, j: (h, j, 0, 0))] + [ANY] * NG,
        out_shape=[_sds((NB, H * KR, B)), _sds((NB, A, B)), _sds((NB, A, B)), _sds((H, NB, 8, B))]
        + [_sds(g.shape, g.dtype) for g in early_g],
        scratch_shapes=[pltpu.VMEM((B, B), F32)] * 4 + [pltpu.VMEM((B, B), BF16)] * 4
        + [pltpu.VMEM((QR, B), F32), pltpu.VMEM((DH, B), F32)] + COMM_SEMS * NG,
        compiler_params=_cp(("arbitrary", "arbitrary")),
    )(fmax, fmin, thr, qTa, kaug, kT, vaug, doTa, lse, delta, *early_g)


def _conv_tail(u1, lng, lnb, beta_c):
    mu = _rowmean(u1)
    d = u1 - mu
    rstd = lax.rsqrt(_rowmean(d * d) + EPS)
    xhat = d * rstd
    u2 = xhat * lng + lnb
    sg = _sigmoid(u2)
    u3 = u2 * sg
    rc = lax.rsqrt(_rowmean(u3 * u3) + EPS)
    n3 = u3 * rc
    return rstd, xhat, u2, sg, rc, n3, n3 * beta_c


def _attn_tail(oT, beta_a_col):
    ra = lax.rsqrt(jnp.mean(oT * oT, axis=0, keepdims=True) + EPS)
    ohat = oT * ra
    return ra, ohat, ohat * beta_a_col


CONV_ROWS = 32
CORR_ROWS = 16


def _fill_shifted(buf, sh):
    rows = sh.shape[1]
    for ph in range(8):
        sh[ph] = buf[pl.ds(ph, rows), :]


def _tap_rows(sh, r0, o, rows):
    return sh[o % 8, pl.ds(pl.multiple_of(r0 + 8 * (o // 8), 8), rows), :]


def _depthwise(sh, w_ref, offs, bias, out_ref, B):
    def chunk(ci, carry):
        r0 = pl.multiple_of(ci * CONV_ROWS, CONV_ROWS)
        acc = jnp.broadcast_to(bias, (CONV_ROWS, CW))
        for k, o in enumerate(offs):
            acc = acc + w_ref[k:k + 1, :] * _tap_rows(sh, r0, o, CONV_ROWS)
        out_ref[pl.ds(r0, CONV_ROWS), :] = acc
        return carry

    lax.fori_loop(0, B // CONV_ROWS, chunk, 0)


def _tap_gradients(sh, d_ref, offs, dw_ref, B):
    for g0 in range(0, len(offs), 8):
        ks = list(range(g0, min(g0 + 8, len(offs))))

        def chunk(ci, accs, ks=ks):
            r0 = pl.multiple_of(ci * CORR_ROWS, CORR_ROWS)
            d = d_ref[pl.ds(r0, CORR_ROWS), :]
            out = []
            for a, k in zip(accs, ks):
                pr = d * _tap_rows(sh, r0, offs[k], CORR_ROWS)
                out.append(a + pr[0:8] + pr[8:16])
            return tuple(out)

        accs = lax.fori_loop(0, B // CORR_ROWS, chunk, tuple(jnp.zeros((8, CW), F32) for _ in ks))
        for a, k in zip(accs, ks):
            dw_ref[k:k + 1, :] = dw_ref[k:k + 1, :] + _colsum(a)


def conv_merge_out(u0, x, oT, conv_w, conv_b, lng, lnb, beta_c, beta_a_col, wo, g1):
    S = x.shape[0]
    B = BLK
    NB = S // B

    def body(uc_ref, up_ref, x_ref, oT_ref, w_ref, cb_ref, lng_ref, lnb_ref, bc_ref, ba_ref, wo_ref,
             g1_ref, x1_ref, o_ref, u1_ref, ubuf, sh):
        i = pl.program_id(0)
        ubuf[0:HALO, :] = jnp.where(i > 0, up_ref[B - HALO:B, :], 0.0)
        ubuf[HALO:HALO + B, :] = uc_ref[...]
        ubuf[HALO + B:HALO + B + 8, :] = jnp.zeros((8, CW), F32)
        _fill_shifted(ubuf, sh)
        _depthwise(sh, w_ref, [HALO - (KC - 1) + k for k in range(KC)], cb_ref[...], u1_ref, B)
        mc = _conv_tail(u1_ref[...], lng_ref[...], lnb_ref[...], bc_ref[...])[-1]
        maT = _attn_tail(oT_ref[0], ba_ref[...])[-1]
        ma = maT.T
        o = _nn(ma.astype(BF16), wo_ref[0:A, :]) + _nn(mc.astype(BF16), wo_ref[A:D, :])
        o_ref[...] = o
        x1_ref[...] = x_ref[...] + g1_ref[...] * o

    row = lambda w: pl.BlockSpec((B, w), lambda i: (i, 0))
    return _pc(
        body, name="conv_merge_out", grid=(NB,),
        in_specs=[row(CW), pl.BlockSpec((B, CW), lambda i: (jnp.maximum(i - 1, 0), 0)), row(D),
                  pl.BlockSpec((1, A, B), lambda i: (i, 0, 0)), _const((32, CW)), _const((1, CW)),
                  _const((1, CW)), _const((1, CW)), _const((1, CW)), _const((A, 1)), _const((D, D)),
                  _const((1, D))],
        out_specs=[row(D), row(D), row(CW)],
        out_shape=[_sds((S, D)), _sds((S, D)), _sds((S, CW))],
        scratch_shapes=[pltpu.VMEM((B + HALO + 8, CW), F32), pltpu.VMEM((8, B + HALO, CW), F32)],
        compiler_params=_cp(("arbitrary",)),
    )(u0, u0, x, oT, conv_w, conv_b, lng, lnb, beta_c, beta_a_col, wo, g1)


def mlp_fwd_loss(x1, tgt, n2g, sc2, sh2, g2, w1s, w2):
    S = x1.shape[0]
    B = BLK
    NB = S // B
    NS, _, FS = w1s.shape

    def body(x1_ref, t_ref, g_ref, sc_ref, sh_ref, g2_ref, w1_hbm, w2_hbm, dy_ref, loss_ref, dg2_ref, ra_ref,
             w1_v, w2_v, sem):
        i = pl.program_id(0)

        @pl.when(i == 0)
        def _():
            for src, dst in ((w1_hbm, w1_v), (w2_hbm, w2_v)):
                cp = pltpu.make_async_copy(src, dst, sem)
                cp.start()
                cp.wait()
            loss_ref[...] = jnp.zeros_like(loss_ref)
            dg2_ref[...] = jnp.zeros_like(dg2_ref)

        x1 = x1_ref[...]
        hs = _norm_mod(x1, g_ref[...], sc_ref[...], sh_ref[...])[2].astype(BF16)
        y2 = jnp.zeros((B, D), F32)
        for q in range(NS):
            a = jnp.maximum(_nn(hs, w1_v[q]), 0.0)
            ra_ref[:, q * FS:(q + 1) * FS] = a.astype(BF16)
            y2 = y2 + _nn((a * a).astype(BF16), w2_v[q * FS:(q + 1) * FS, :])
        e = x1 + g2_ref[...] * y2 - t_ref[...]
        dy = e * (1.0 / D)
        dy_ref[...] = dy
        loss_ref[...] = loss_ref[...] + 0.5 * _colsum(_rowmean(e * e))
        dg2_ref[...] = dg2_ref[...] + _colsum(dy * y2)

    row = pl.BlockSpec((B, D), lambda i: (i, 0))
    return _pc(
        body, name="mlp_fwd_loss", grid=(NB,),
        in_specs=[row, row, _const((1, D)), _const((1, D)), _const((1, D)), _const((1, D)), ANY, ANY],
        out_specs=[row, _const((1, 1)), _const((1, D)), pl.BlockSpec((B, FF), lambda i: (i, 0))],
        out_shape=[_sds((S, D)), _sds((1, 1)), _sds((1, D)), _sds((S, FF), BF16)],
        scratch_shapes=[pltpu.VMEM(w1s.shape, BF16), pltpu.VMEM(w2.shape, BF16), pltpu.SemaphoreType.DMA(())],
        compiler_params=_cp(("arbitrary",)),
    )(x1, tgt, n2g, sc2, sh2, g2, w1s, w2)


def mlp_bwd(x1, dy, ra, n2g, sc2, sh2, g2, w1s, w2):
    S = x1.shape[0]
    B = BLK
    NB = S // B
    NC = FF // FCH
    NS, _, FS = w1s.shape
    SPC = NS // NC

    def body(x1_ref, dy_ref, ra_ref, g_ref, sc_ref, sh_ref, g2_ref, w1_ref, w2_ref, dw1_ref, dw2_ref, dh_ref):
        i = pl.program_id(1)

        @pl.when(i == 0)
        def _():
            dw1_ref[...] = jnp.zeros_like(dw1_ref)
            dw2_ref[...] = jnp.zeros_like(dw2_ref)

        h2 = _norm_mod(x1_ref[...], g_ref[...], sc_ref[...], sh_ref[...])[2]
        h2t = h2.T.astype(BF16)
        ra = ra_ref[...].astype(F32)
        dyb = (dy_ref[...] * g2_ref[...]).astype(BF16)
        db = _nt(dyb, w2_ref[...])
        dab = (db * (2.0 * ra)).astype(BF16)
        dh = jnp.zeros((B, D), F32)
        for q in range(SPC):
            part = dab[:, q * FS:(q + 1) * FS]
            dh = dh + _nt(part, w1_ref[q])
            dw1_ref[q] = dw1_ref[q] + _nn(h2t, part)
        dh_ref[0] = dh
        dw2_ref[...] = dw2_ref[...] + _nn((ra * ra).T.astype(BF16), dyb)

    row = pl.BlockSpec((B, D), lambda c, i: (i, 0))
    vec = pl.BlockSpec((1, D), lambda c, i: (0, 0))
    wsh = pl.BlockSpec((SPC, D, FS), lambda c, i: (c, 0, 0))
    w2b = pl.BlockSpec((FCH, D), lambda c, i: (c, 0))
    return _pc(
        body, name="mlp_bwd", grid=(NC, NB),
        in_specs=[row, row, pl.BlockSpec((B, FCH), lambda c, i: (i, c)), vec, vec, vec, vec, wsh, w2b],
        out_specs=[wsh, w2b, pl.BlockSpec((1, B, D), lambda c, i: (c, i, 0))],
        out_shape=[_sds(w1s.shape), _sds((FF, D)), _sds((NC, S, D))],
        compiler_params=_cp(("arbitrary", "arbitrary")),
    )(x1, dy, ra, n2g, sc2, sh2, g2, w1s, w2)


def merge_bwd(dh2p, x1, dy, o, oT, u1, n2g, sc2, g1, lng, lnb, beta_c, beta_a_col, wo):
    S = x1.shape[0]
    B = BLK
    NB = S // B
    NC = dh2p.shape[0]

    def body(dh_ref, x1_ref, dy_ref, o_ref, oT_ref, u1_ref, g_ref, sc_ref, g1_ref, lng_ref, lnb_ref,
             bc_ref, ba_ref, wo_ref,
             dx1_ref, doTa_ref, dl_ref, du1_ref, dwo_ref, v1_ref, v2_ref, dba_ref):
        i = pl.program_id(0)

        @pl.when(i == 0)
        def _():
            dwo_ref[...] = jnp.zeros_like(dwo_ref)
            v1_ref[...] = jnp.zeros_like(v1_ref)
            v2_ref[...] = jnp.zeros_like(v2_ref)
            dba_ref[...] = jnp.zeros_like(dba_ref)

        dh2 = dh_ref[0]
        for cc in range(1, NC):
            dh2 = dh2 + dh_ref[cc]
        x1 = x1_ref[...]
        g = g_ref[...]
        sc = sc_ref[...]
        r2 = lax.rsqrt(_rowmean(x1 * x1) + EPS)
        xh = x1 * r2
        dhx = dh2 * xh
        v1_ref[0:1, :] = v1_ref[0:1, :] + _colsum(dh2)
        v1_ref[1:2, :] = v1_ref[1:2, :] + _colsum(dhx) * g
        v1_ref[2:3, :] = v1_ref[2:3, :] + _colsum(dhx) * (1.0 + sc)
        dxh = dh2 * (g * (1.0 + sc))
        dx1 = dy_ref[...] + r2 * (dxh - xh * _rowmean(dxh * xh))
        dx1_ref[...] = dx1
        v1_ref[3:4, :] = v1_ref[3:4, :] + _colsum(dx1 * o_ref[...])
        dob = (dx1 * g1_ref[...]).astype(BF16)

        lng = lng_ref[...]
        bc = bc_ref[...]
        rstd, xhat, u2, sg, rc, n3, mc = _conv_tail(u1_ref[...], lng, lnb_ref[...], bc)
        ba = ba_ref[...]
        oT = oT_ref[0]
        ra, ohat, maT = _attn_tail(oT, ba)
        dwo_ref[0:A, :] = dwo_ref[0:A, :] + _nn(maT.astype(BF16), dob)
        dwo_ref[A:D, :] = dwo_ref[A:D, :] + _nn(mc.T.astype(BF16), dob)
        dmaT = _nt(wo_ref[0:A, :], dob)
        dmc = _nt(dob, wo_ref[A:D, :])

        dba_ref[...] = dba_ref[...] + _lanesum(dmaT * ohat)
        dohat = dmaT * ba
        doT = ra * (dohat - ohat * jnp.mean(dohat * ohat, axis=0, keepdims=True))
        prod = doT * oT
        zpad = jnp.zeros((128 - DH, B), BF16)
        for hh in range(H):
            sl = slice(hh * DH, (hh + 1) * DH)
            dl_ref[hh, 0] = _colsum(prod[sl])
            doTa_ref[0, hh * 128:hh * 128 + DH, :] = doT[sl].astype(BF16)
            doTa_ref[0, hh * 128 + DH:(hh + 1) * 128, :] = zpad

        v2_ref[0:1, :] = v2_ref[0:1, :] + _colsum(dmc * n3)
        dn3 = dmc * bc
        du3 = rc * (dn3 - n3 * _rowmean(dn3 * n3))
        du2 = du3 * (sg * (1.0 + u2 * (1.0 - sg)))
        v2_ref[1:2, :] = v2_ref[1:2, :] + _colsum(du2 * xhat)
        v2_ref[2:3, :] = v2_ref[2:3, :] + _colsum(du2)
        dxhat = du2 * lng
        du1_ref[...] = rstd * (dxhat - _rowmean(dxhat) - xhat * _rowmean(dxhat * xhat))

    row = lambda w: pl.BlockSpec((B, w), lambda i: (i, 0))
    return _pc(
        body, name="merge_bwd", grid=(NB,),
        in_specs=[pl.BlockSpec((NC, B, D), lambda i: (0, i, 0)), row(D), row(D), row(D),
                  pl.BlockSpec((1, A, B), lambda i: (i, 0, 0)), row(CW), _const((1, D)), _const((1, D)),
                  _const((1, D)), _const((1, CW)), _const((1, CW)), _const((1, CW)), _const((A, 1)),
                  _const((D, D))],
        out_specs=[row(D), pl.BlockSpec((1, H * 128, B), lambda i: (i, 0, 0)),
                   pl.BlockSpec((H, 1, 1, B), lambda i: (0, i, 0, 0)), row(CW), _const((D, D)),
                   _const((8, D)), _const((8, CW)), _const((A, 1))],
        out_shape=[_sds((S, D)), _sds((NB, H * 128, B), BF16), _sds((H, NB, 1, B)), _sds((S, CW)),
                   _sds((D, D)), _sds((8, D)), _sds((8, CW)), _sds((A, 1))],
        compiler_params=_cp(("arbitrary",)),
    )(dh2p, x1, dy, o, oT, u1, n2g, sc2, g1, lng, lnb, beta_c, beta_a_col, wo)


def head_bwd(du1, u0, alg, zqk, fgT, dqT, dkT, dvT, dfk, conv_w, qg_col, kg_col, bf_col, tri_lo):
    S = u0.shape[0]
    B = BLK
    NB = S // B

    def body(dc_ref, dn_ref, uc_ref, up_ref, alg_ref, zqk_ref, fg_ref, dq_ref, dk_ref, dv_ref, dfk_ref,
             w_ref, qg_ref, kg_ref, bf_ref, tri_ref,
             dzr_ref, dzT_ref, dcw_ref, vc_ref, dqg_ref, dkg_ref, dbf_ref,
             buf, sh, du0_ref, carry, fbuf):
        pid = pl.program_id(0)
        ri = NB - 1 - pid

        @pl.when(pid == 0)
        def _():
            carry[...] = jnp.zeros_like(carry)
            dcw_ref[...] = jnp.zeros_like(dcw_ref)
            vc_ref[...] = jnp.zeros_like(vc_ref)
            dqg_ref[...] = jnp.zeros_like(dqg_ref)
            dkg_ref[...] = jnp.zeros_like(dkg_ref)
            dbf_ref[...] = jnp.zeros_like(dbf_ref)

        zero8 = jnp.zeros((8, CW), F32)
        buf[0:B, :] = dc_ref[...]
        buf[B:B + HALO, :] = jnp.where(ri < NB - 1, dn_ref[0:HALO, :], 0.0)
        buf[B + HALO:B + HALO + 8, :] = zero8
        _fill_shifted(buf, sh)
        _depthwise(sh, w_ref, [KC - 1 - k for k in range(KC)], jnp.zeros((1, CW), F32), du0_ref, B)
        buf[0:HALO, :] = jnp.where(ri > 0, up_ref[B - HALO:B, :], 0.0)
        buf[HALO:HALO + B, :] = uc_ref[...]
        buf[HALO + B:HALO + B + 8, :] = zero8
        _fill_shifted(buf, sh)
        _tap_gradients(sh, dc_ref, [HALO - (KC - 1) + k for k in range(KC)], dcw_ref, B)
        vc_ref[0:1, :] = vc_ref[0:1, :] + _colsum(dc_ref[...])

        du0 = du0_ref[...]
        al = alg_ref[:, 0:CW]
        sg = _sigmoid(alg_ref[:, CW:2 * CW])
        dzr_ref[:, 0:CW] = (du0 * sg).astype(BF16)
        dzr_ref[:, CW:2 * CW] = (du0 * al * sg * (1.0 - sg)).astype(BF16)

        dqg = jnp.zeros((DH, B), F32)
        dkg = jnp.zeros((DH, B), F32)
        qg = qg_ref[...]
        kg = kg_ref[...]
        for hh in range(H):
            sl = slice(hh * DH, (hh + 1) * DH)
            q = zqk_ref[sl, :]
            rq = lax.rsqrt(jnp.mean(q * q, axis=0, keepdims=True) + EPS)
            qn = q * rq
            dqh = dq_ref[0, hh * KR:hh * KR + DH, :] * 0.125
            dqg = dqg + dqh * qn
            dqn = dqh * qg
            dzT_ref[0, sl, :] = (rq * (dqn - qn * jnp.mean(dqn * qn, axis=0, keepdims=True))).astype(BF16)
            k = zqk_ref[A + hh * DH:A + (hh + 1) * DH, :]
            rk = lax.rsqrt(jnp.mean(k * k, axis=0, keepdims=True) + EPS)
            kn = k * rk
            dkh = dk_ref[0, sl, :] * (1.0 / LOG2E)
            dkg = dkg + dkh * kn
            dkn = dkh * kg
            dzT_ref[0, A + hh * DH:A + (hh + 1) * DH, :] = (
                rk * (dkn - kn * jnp.mean(dkn * kn, axis=0, keepdims=True))).astype(BF16)
            fbuf[hh:hh + 1, :] = dq_ref[0, hh * KR + DH:hh * KR + DH + 1, :] - dfk_ref[hh, 0, 0:1, :]
        dqg_ref[...] = dqg_ref[...] + _lanesum(dqg)
        dkg_ref[...] = dkg_ref[...] + _lanesum(dkg)
        dzT_ref[0, 2 * A:3 * A, :] = dv_ref[0].astype(BF16)

        dF = fbuf[...]
        a1, a2, a3 = _split3(dF)
        tr = tri_ref[...]
        dlogf = _nn(a1, tr) + _nn(a2, tr) + _nn(a3, tr) + carry[...]
        carry[...] = carry[...] + _lanesum(dF)
        dfg = dlogf * _sigmoid(-(fg_ref[...] + bf_ref[...]))
        dbf_ref[...] = dbf_ref[...] + _lanesum(dfg)
        dzT_ref[0, 3 * A:ZT_ROWS, :] = jnp.concatenate([dfg, jnp.zeros((8, B), F32)], axis=0).astype(BF16)

    rrow = lambda w: pl.BlockSpec((B, w), lambda p: (NB - 1 - p, 0))
    rts = lambda r: pl.BlockSpec((1, r, B), lambda p: (NB - 1 - p, 0, 0))
    return _pc(
        body, name="head_bwd", grid=(NB,),
        in_specs=[rrow(CW), pl.BlockSpec((B, CW), lambda p: (jnp.minimum(NB - p, NB - 1), 0)),
                  rrow(CW), pl.BlockSpec((B, CW), lambda p: (jnp.maximum(NB - 2 - p, 0), 0)),
                  rrow(2 * CW), pl.BlockSpec((2 * A, B), lambda p: (0, NB - 1 - p)),
                  pl.BlockSpec((8, B), lambda p: (0, NB - 1 - p)), rts(H * KR), rts(A), rts(A),
                  pl.BlockSpec((H, 1, 8, B), lambda p: (0, NB - 1 - p, 0, 0)),
                  _const((32, CW)), _const((DH, 1)), _const((DH, 1)), _const((8, 1)), _const((B, B))],
        out_specs=[rrow(2 * CW), rts(ZT_ROWS), _const((32, CW)), _const((8, CW)), _const((DH, 1)),
                   _const((DH, 1)), _const((8, 1))],
        out_shape=[_sds((S, 2 * CW), BF16), _sds((NB, ZT_ROWS, B), BF16), _sds((32, CW)), _sds((8, CW)),
                   _sds((DH, 1)), _sds((DH, 1)), _sds((8, 1))],
        scratch_shapes=[pltpu.VMEM((B + HALO + 8, CW), F32), pltpu.VMEM((8, B + HALO, CW), F32),
                        pltpu.VMEM((B, CW), F32), pltpu.VMEM((8, 1), F32), pltpu.VMEM((8, B), F32)],
        compiler_params=_cp(("arbitrary",)),
    )(du1, du1, u0, u0, alg, zqk, fgT, dqT, dkT, dvT, dfk, conv_w, qg_col, kg_col, bf_col, tri_lo)


def in_bwd_w(x, dzr, dzT, n1g, sc1, sh1):
    S = x.shape[0]
    B = BLK
    NB = S // B

    def body(x_ref, dzr_ref, dzT_ref, g_ref, sc_ref, sh_ref, dwT_hbm, dwr_hbm, dwT_acc, dwr_acc, sem):
        i = pl.program_id(0)

        @pl.when(i == 0)
        def _():
            dwT_acc[...] = jnp.zeros_like(dwT_acc)
            dwr_acc[...] = jnp.zeros_like(dwr_acc)

        h1 = _norm_mod(x_ref[...], g_ref[...], sc_ref[...], sh_ref[...])[2]
        dwT_acc[...] = dwT_acc[...] + _nn(dzT_ref[0], h1.astype(BF16))
        dwr_acc[...] = dwr_acc[...] + _nn(h1.T.astype(BF16), dzr_ref[...])

        @pl.when(i == NB - 1)
        def _():
            for src, dst in ((dwT_acc, dwT_hbm), (dwr_acc, dwr_hbm)):
                cp = pltpu.make_async_copy(src, dst, sem)
                cp.start()
                cp.wait()

    row = lambda w: pl.BlockSpec((B, w), lambda i: (i, 0))
    return _pc(
        body, name="in_bwd_w", grid=(NB,),
        in_specs=[row(D), row(2 * CW), pl.BlockSpec((1, ZT_ROWS, B), lambda i: (i, 0, 0)),
                  _const((1, D)), _const((1, D)), _const((1, D))],
        out_specs=[ANY, ANY],
        out_shape=[_sds((ZT_ROWS, D)), _sds((D, 2 * CW))],
        scratch_shapes=[pltpu.VMEM((ZT_ROWS, D), F32), pltpu.VMEM((D, 2 * CW), F32),
                        pltpu.SemaphoreType.DMA(())],
        compiler_params=_cp(("arbitrary",)),
    )(x, dzr, dzT, n1g, sc1, sh1)


def in_bwd_x(x, dx1, dzr, dzT, n1g, sc1, sh1, w_qkvf, wr, late_g):
    S = x.shape[0]
    B = BLK
    NB = S // B

    def body(x_ref, dx1_ref, dzr_ref, dzT_ref, g_ref, sc_ref, sh_ref, w_hbm, wr_hbm, lg_ref,
             gx_ref, v_ref, recv_ref, w_v, wr_v, sem, send_sems, recv_sems, local_sem):
        i = pl.program_id(0)

        @pl.when(i == 0)
        def _():
            _start_all(_direct_copies(True, lg_ref, recv_ref, send_sems, recv_sems, local_sem))
            for src, dst in ((w_hbm, w_v), (wr_hbm, wr_v)):
                cp = pltpu.make_async_copy(src, dst, sem)
                cp.start()
                cp.wait()
            v_ref[...] = jnp.zeros_like(v_ref)

        g = g_ref[...]
        sc = sc_ref[...]
        r1, xh, _ = _norm_mod(x_ref[...], g, sc, sh_ref[...])
        dh1 = _nt(dzr_ref[...], wr_v[...]) + _nn(w_v[...], dzT_ref[0]).T
        dhx = dh1 * xh
        v_ref[0:1, :] = v_ref[0:1, :] + _colsum(dh1)
        v_ref[1:2, :] = v_ref[1:2, :] + _colsum(dhx) * g
        v_ref[2:3, :] = v_ref[2:3, :] + _colsum(dhx) * (1.0 + sc)
        dxh = dh1 * (g * (1.0 + sc))
        gx_ref[...] = dx1_ref[...] + r1 * (dxh - xh * _rowmean(dxh * xh))

        @pl.when(i == NB - 1)
        def _():
            _wait_all(_direct_copies(True, lg_ref, recv_ref, send_sems, recv_sems, local_sem))

    row = lambda w: pl.BlockSpec((B, w), lambda i: (i, 0))
    return _pc(
        body, name="in_bwd_x", grid=(NB,),
        in_specs=[row(D), row(D), row(2 * CW), pl.BlockSpec((1, ZT_ROWS, B), lambda i: (i, 0, 0)),
                  _const((1, D)), _const((1, D)), _const((1, D)), ANY, ANY, ANY],
        out_specs=[row(D), _const((8, D)), ANY],
        out_shape=[_sds((S, D)), _sds((8, D)), _sds(late_g.shape, late_g.dtype)],
        scratch_shapes=[pltpu.VMEM((D, ZT_ROWS), BF16), pltpu.VMEM((D, 2 * CW), BF16),
                        pltpu.SemaphoreType.DMA(())] + COMM_SEMS,
        compiler_params=_cp(("arbitrary",)),
    )(x, dx1, dzr, dzT, n1g, sc1, sh1, w_qkvf, wr, late_g)


def _adam_math(g, w, m, v):
    m = ADAM_B1 * m + (1.0 - ADAM_B1) * g
    v = ADAM_B2 * v + (1.0 - ADAM_B2) * (g * g)
    m_hat = m / (1.0 - ADAM_B1 ** ADAM_STEP)
    v_hat = v / (1.0 - ADAM_B2 ** ADAM_STEP)
    delta = -ADAM_LR * (m_hat / (jnp.sqrt(v_hat) + ADAM_EPS) + ADAM_WD * w)
    return delta, m, v


def _row_tile(R):
    for t in (1024, 512, 256, 128, 64, 32, 16, 8):
        if R % t == 0:
            return t
    return R


def sum_slots(parts, name):
    K, R, C = parts.shape
    T = _row_tile(R)

    def body(p_ref, o_ref):
        s = p_ref[0]
        for k in range(1, K):
            s = s + p_ref[k]
        o_ref[...] = s

    return _pc(body, name=name, grid=(R // T,),
               in_specs=[pl.BlockSpec((K, T, C), lambda i: (0, i, 0))],
               out_specs=pl.BlockSpec((T, C), lambda i: (i, 0)), out_shape=_sds((R, C)),
               compiler_params=_cp(("arbitrary",)))(parts)


def adamw_slots(parts, w, m, v, name):
    K, R, C = parts.shape
    T = R
    while K * T * C * 4 > (8 << 20) and T % 16 == 0:
        T //= 2

    def body(p_ref, w_ref, m_ref, v_ref, g_ref, d_ref, nm_ref, nv_ref):
        g = p_ref[0].astype(F32)
        for k in range(1, K):
            g = g + p_ref[k].astype(F32)
        g_ref[...] = g
        d_ref[...], nm_ref[...], nv_ref[...] = _adam_math(g, w_ref[...], m_ref[...], v_ref[...])

    t2 = pl.BlockSpec((T, C), lambda i: (i, 0))
    return _pc(body, name=name, grid=(R // T,),
               in_specs=[pl.BlockSpec((K, T, C), lambda i: (0, i, 0)), t2, t2, t2],
               out_specs=[t2, t2, t2, t2], out_shape=[_sds((R, C))] * 4,
               compiler_params=_cp(("arbitrary",)))(parts, w, m, v)


def adamw_many(gs, ws, ms, vs, name):
    n = len(ws)

    def body(*refs):
        outs = refs[4 * n:]
        for q in range(n):
            d, nm, nv = _adam_math(refs[q][...], refs[n + q][...], refs[2 * n + q][...], refs[3 * n + q][...])
            outs[q][...] = d
            outs[n + q][...] = nm
            outs[2 * n + q][...] = nv

    res = _pc(body, name=name, grid=(1,),
              in_specs=[_const(a.shape) for a in list(gs) + list(ws) + list(ms) + list(vs)],
              out_specs=[_const(w.shape) for w in ws] * 3, out_shape=[_sds(w.shape) for w in ws] * 3,
              compiler_params=_cp())(*gs, *ws, *ms, *vs)
    return res[0:n], res[n:2 * n], res[2 * n:3 * n]


def ada_grad_adamw(cT, dmod_cols, w, m, v):
    NCOL = w.shape[1]

    def body(cT_ref, dm_ref, w_ref, m_ref, v_ref, g_ref, d_ref, nm_ref, nv_ref):
        def term(b):
            cv = cT_ref[b]
            return (cv * _sigmoid(cv)) * dm_ref[b:b + 1, :]

        g = term(0)
        for b in range(1, N_DEV):
            g = g + term(b)
        g_ref[...] = g
        d_ref[...], nm_ref[...], nv_ref[...] = _adam_math(g, w_ref[...], m_ref[...], v_ref[...])

    full = _const((D, NCOL))
    return _pc(body, name="ada_grad_adamw", grid=(1,),
               in_specs=[_const((N_DEV, D, 1)), _const((N_DEV, NCOL)), full, full, full],
               out_specs=[full, full, full, full], out_shape=[_sds((D, NCOL))] * 4,
               compiler_params=_cp())(cT, dmod_cols, w, m, v)


def _pack_rows(vecs, rows=None):
    flat = jnp.concatenate([jnp.ravel(v) for v in vecs])
    n = flat.shape[0]
    r = -(-n // 1024) * 8 if rows is None else rows
    return jnp.pad(flat, (0, r * 128 - n)).reshape(r, 128)


def _unpack_rows(packed, shapes):
    flat = packed.reshape(-1)
    out, off = [], 0
    for s in shapes:
        n = 1
        for d in s:
            n *= d
        out.append(flat[off:off + n].reshape(s))
        off += n
    return out


def _cols_to_shards(w, cols):
    rows = w.shape[0]
    return w.reshape(rows, N_DEV, cols).transpose(1, 0, 2).reshape(N_DEV, rows * cols)


def kernel(x, c, w_ada, b_ada, norm1_g, w_in, q_norm_g, k_norm_g, b_f, conv_w, conv_b, conv_ln_g, conv_ln_b, beta_attn, beta_conv, w_out, norm2_g, w_ff1, w_ff2, loss_target, m_w_ada, m_b_ada, m_norm1_g, m_w_in, m_q_norm_g, m_k_norm_g, m_b_f, m_conv_w, m_conv_b, m_conv_ln_g, m_conv_ln_b, m_beta_attn, m_beta_conv, m_w_out, m_norm2_g, m_w_ff1, m_w_ff2, v_w_ada, v_b_ada, v_norm1_g, v_w_in, v_q_norm_g, v_k_norm_g, v_b_f, v_conv_w, v_conv_b, v_conv_ln_g, v_conv_ln_b, v_beta_attn, v_beta_conv, v_w_out, v_norm2_g, v_w_ff1, v_w_ff2):
    S = x.shape[1]
    B = BLK
    NB = S // B
    me = 4 * lax.axis_index("x") + 2 * lax.axis_index("y") + lax.axis_index("c")
    xs = x[0]
    tgt = loss_target[0]

    ADA_C, IN_C, FF_C, CV_C = w_ada.shape[2], w_in.shape[2], w_ff1.shape[2], conv_w.shape[2]
    OUT_R, FF_R = w_out.shape[1], w_ff2.shape[1]

    g_in, g_c, g_cw = gather_two_level([w_in[0].T.astype(BF16), c, conv_w[0]], "ag_weights_early")
    late_w = [w_out[0].astype(BF16), w_ff1[0].astype(BF16), w_ff2[0].astype(BF16)]

    w_in_t = g_in.reshape(N_DEV * IN_C, D)
    c_all = g_c.reshape(N_DEV, D)
    conv_w_full = g_cw.transpose(1, 0, 2).reshape(KC, N_DEV * CV_C)
    conv_w_pad = jnp.pad(conv_w_full, ((0, 32 - KC), (0, 0)))

    wT = jnp.pad(w_in_t[0:3 * A + H], ((0, ZT_ROWS - 3 * A - H), (0, 0)))
    w_qkvf = wT.T
    wr = w_in_t[3 * A + H:].T

    qg_col = q_norm_g.reshape(DH, 1)
    kg_col = k_norm_g.reshape(DH, 1)
    bf_col = b_f.reshape(H, 1)
    beta_a_col = beta_attn.reshape(A, 1)
    ii = lax.broadcasted_iota(jnp.int32, (B, B), 0)
    jj = lax.broadcasted_iota(jnp.int32, (B, B), 1)
    tri_up = (ii <= jj).astype(BF16)
    tri_lo = (ii >= jj).astype(BF16)

    modc = mod_columns(c_all, w_ada[0], lax.dynamic_slice(b_ada, (0, me * ADA_C), (1, ADA_C)))
    gm, = gather_direct([modc], "ag_mod")
    mod = lax.dynamic_index_in_dim(gm, me, axis=1, keepdims=False).reshape(1, N_DEV * ADA_C)
    sh1, sc1, g1, sh2, sc2, g2 = [mod[:, k * D:(k + 1) * D] for k in range(6)]
    qTa, kT, kaug, vT, vaug, zqk, fgT, alg, u0, stat = fwd_in(xs, norm1_g, sc1, sh1, wT, wr, qg_col,
                                                             kg_col, bf_col, tri_up)
    fmax = stat[:, :, 0].T
    fmin = stat[:, :, 1].T
    qk_max = jnp.sqrt(jnp.max(stat[:, :, 2], axis=0) * jnp.max(stat[:, :, 3], axis=0))
    thr = -(PRUNE + (2.02 / LOG2E) * qk_max)
    oT, lse, g_wo, w1_shards, g_w2 = attn_fwd(fmax, fmin, thr, qTa, kaug, vT, late_w)
    wo_full = g_wo.reshape(D, D)
    w2_full = g_w2.reshape(FF, D)
    x1, o, u1 = conv_merge_out(u0, xs, oT, conv_w_pad, conv_b, conv_ln_g, conv_ln_b, beta_conv,
                               beta_a_col, wo_full, g1)
    dy, loss_part, dg2, ra = mlp_fwd_loss(x1, tgt, norm2_g, sc2, sh2, g2, w1_shards, w2_full)

    dw1s, dw2, dh2p = mlp_bwd(x1, dy, ra, norm2_g, sc2, sh2, g2, w1_shards, w2_full)
    dx1, doTa, delta, du1, dwo, v1, v2, dba = merge_bwd(dh2p, x1, dy, o, oT, u1, norm2_g, sc2, g1,
                                                         conv_ln_g, conv_ln_b, beta_conv, beta_a_col, wo_full)
    early_g = [dwo.reshape(N_DEV, OUT_R, D), dw1s, dw2.reshape(N_DEV, FF_R, D)]
    dqT, dkT, dvT, dfk, r_out, r_f1, r_f2 = attn_bwd(fmax, fmin, thr, qTa, kaug, kT, vaug, doTa, lse,
                                                     delta, early_g)
    dzr, dzT, dcw, vc, dqg, dkg, dbf = head_bwd(du1, u0, alg, zqk, fgT, dqT, dkT, dvT, dfk, conv_w_pad,
                                                qg_col, kg_col, bf_col, tri_lo)
    dwT, dwr = in_bwd_w(xs, dzr, dzT, norm1_g, sc1, sh1)

    dw_in = jnp.concatenate([dwT.T[:, 0:3 * A + H], dwr], axis=1)
    late_g = _cols_to_shards(dw_in, IN_C)
    rows_b = -(-late_g.shape[1] // (128 * 256)) * 256
    late_g = jnp.pad(late_g, ((0, 0), (0, rows_b * 128 - late_g.shape[1]))).astype(BF16)
    grad_x, v0, recv_b = in_bwd_x(xs, dx1, dzr, dzT, norm1_g, sc1, sh1, w_qkvf, wr,
                                  late_g.reshape(N_DEV, rows_b, 128))

    dmod = jnp.concatenate([v0[0], v0[1], v1[3], v1[0], v1[1], dg2[0]])
    small1 = _pack_rows([dmod, v0[2], v1[2], dcw, vc[0], v2[1], v2[2], v2[0], dba, dqg, dkg,
                         jnp.pad(dbf.reshape(-1), (0, 120)), jnp.pad(loss_part.reshape(-1), (0, 127))])
    gs1, = gather_direct([small1], "ag_small_bwd")
    tot = sum_slots(gs1, "sum_small")
    (g_b_ada, g_n1, g_n2, g_cw, g_cb, g_lng, g_lnb, g_bc, g_ba, g_qg, g_kg, g_bf, loss_v) = _unpack_rows(
        tot, [(1, 6 * D), (1, D), (1, D), (32, CW), (1, CW), (1, CW), (1, CW), (1, CW), (1, A),
              (1, DH), (1, DH), (1, 128), (1, 128)])
    loss = loss_v[0, 0]
    g_bf = g_bf[:, 0:H]
    g_cw_mine = lax.dynamic_slice(g_cw[0:KC], (0, me * CV_C), (KC, CV_C)).reshape(1, KC, CV_C)

    small_names = [(b_ada, m_b_ada, v_b_ada, g_b_ada), (norm1_g, m_norm1_g, v_norm1_g, g_n1),
                   (q_norm_g, m_q_norm_g, v_q_norm_g, g_qg), (k_norm_g, m_k_norm_g, v_k_norm_g, g_kg),
                   (b_f, m_b_f, v_b_f, g_bf), (conv_w, m_conv_w, v_conv_w, g_cw_mine),
                   (conv_b, m_conv_b, v_conv_b, g_cb), (conv_ln_g, m_conv_ln_g, v_conv_ln_g, g_lng),
                   (conv_ln_b, m_conv_ln_b, v_conv_ln_b, g_lnb), (beta_attn, m_beta_attn, v_beta_attn, g_ba),
                   (beta_conv, m_beta_conv, v_beta_conv, g_bc), (norm2_g, m_norm2_g, v_norm2_g, g_n2)]
    sgs = [t[3].reshape(t[0].shape) for t in small_names]
    sds, sms, svs = adamw_many(sgs, *[[t[k] for t in small_names] for k in range(3)], "adamw_small")

    dmod_all = gs1[:, 0:48, :].reshape(N_DEV, N_DEV, ADA_C)
    dmod_cols = lax.dynamic_index_in_dim(dmod_all, me, axis=1, keepdims=False)
    ga, da, ma_, va_ = ada_grad_adamw(c_all.reshape(N_DEV, D, 1), dmod_cols, w_ada[0], m_w_ada[0], v_w_ada[0])

    res_b = adamw_slots(recv_b, *[_pack_rows(ws, rows=rows_b) for ws in ([w_in], [m_w_in], [v_w_in])],
                        "adamw_in")
    late = []
    for nm, parts, (w_, m_, v_) in (("adamw_out", r_out, (w_out, m_w_out, v_w_out)),
                                    ("adamw_ff1", r_f1, (w_ff1, m_w_ff1, v_w_ff1)),
                                    ("adamw_ff2", r_f2, (w_ff2, m_w_ff2, v_w_ff2))):
        late.append([r[None] for r in adamw_slots(parts, w_[0], m_[0], v_[0], nm)])
    bgs, bds, bms, bvs = [_unpack_rows(res_b[q], [(1, D, IN_C)]) + [late[0][q], late[1][q], late[2][q]]
                          for q in range(4)]

    def assemble(small, ada, bigs):
        (b_ada_, n1_, qg_, kg_, bf_, cw_, cb_, lng_, lnb_, ba_, bc_, n2_) = small
        return [ada.reshape(1, D, ADA_C), b_ada_, n1_, bigs[0], qg_, kg_, bf_, cw_, cb_, lng_, lnb_, ba_, bc_,
                bigs[1], n2_, bigs[2], bigs[3]]

    return (loss, grad_x.reshape(1, S, D), *assemble(sgs, ga, bgs), *assemble(sds, da, bds),
            *assemble(sms, ma_, bms), *assemble(svs, va_, bvs))
```

```python
import functools

import jax
import jax.numpy as jnp
from jax import lax
from jax.experimental import pallas as pl
from jax.experimental.pallas import tpu as pltpu

F32 = jnp.float32
BF16 = jnp.bfloat16

D = 1024
A = 512
CW = 512
H = 8
DH = 64
FF = 4096
KC = 31
HALO = 32
KR = 80
ZT_ROWS = 1552
FCH = 1024
EPS = 1e-6
BLK = 512
N_DEV = 8
VMEM_LIMIT = 56 * 1024 * 1024

ADAM_LR = 0.001
ADAM_B1 = 0.9
ADAM_B2 = 0.999
ADAM_EPS = 1e-08
ADAM_WD = 0.01
ADAM_STEP = 10

MESH = pl.DeviceIdType.MESH
ANY = pl.BlockSpec(memory_space=pl.ANY)
SMEM = pl.BlockSpec(memory_space=pltpu.SMEM)
PRUNE = 105.0
LOG2E = 1.4426950408889634


def _pc(body, **kw):
    return pl.pallas_call(body, **kw)


def _cp(sem=None):
    return pltpu.CompilerParams(dimension_semantics=sem, vmem_limit_bytes=VMEM_LIMIT)


def _sds(shape, dtype=F32):
    return jax.ShapeDtypeStruct(shape, dtype)


def _const(shape):
    n = len(shape)
    return pl.BlockSpec(shape, lambda *a: (0,) * n)


def _nt(a, b):
    return lax.dot_general(a, b, (((1,), (1,)), ((), ())), preferred_element_type=F32)


def _nn(a, b):
    return jnp.dot(a, b, preferred_element_type=F32)


def _sigmoid(v):
    return 1.0 / (1.0 + jnp.exp(-v))


def _split3(v):
    a1 = v.astype(BF16)
    r1 = v - a1.astype(F32)
    a2 = r1.astype(BF16)
    a3 = (r1 - a2.astype(F32)).astype(BF16)
    return a1, a2, a3


def _rowmean(v):
    return jnp.mean(v, axis=-1, keepdims=True)


def _colsum(v):
    return jnp.sum(v, axis=0, keepdims=True)


def _lanesum(v):
    return jnp.sum(v, axis=-1, keepdims=True)


def _coords():
    return lax.axis_index("x"), lax.axis_index("y"), lax.axis_index("c")


def _direct_copies(scatter, src_ref, dst_ref, send_sems, recv_sems, local_sem):
    mx, my, mc = _coords()
    me = 4 * mx + 2 * my + mc
    if scatter:
        local = pltpu.make_async_copy(src_ref.at[me], dst_ref.at[0], local_sem)
    else:
        local = pltpu.make_async_copy(src_ref, dst_ref.at[me], local_sem)
    remote = []
    for r in range(1, N_DEV):
        px = 1 - mx if r & 4 else mx
        py = 1 - my if r & 2 else my
        pcc = 1 - mc if r & 1 else mc
        remote.append(pltpu.make_async_remote_copy(
            src_ref=src_ref.at[4 * px + 2 * py + pcc] if scatter else src_ref,
            dst_ref=dst_ref.at[r] if scatter else dst_ref.at[me],
            send_sem=send_sems.at[r - 1], recv_sem=recv_sems.at[r - 1],
            device_id=(px, py, pcc), device_id_type=MESH))
    return [local] + remote


def _start_all(copies):
    for cp in copies:
        cp.start()


def _wait_all(copies):
    for cp in copies[1:]:
        cp.wait()
    copies[0].wait()


COMM_SEMS = [pltpu.SemaphoreType.DMA((7,)), pltpu.SemaphoreType.DMA((7,)), pltpu.SemaphoreType.DMA(())]


def gather_direct(arrays, name):
    NG = len(arrays)

    def body(*refs):
        sems = refs[2 * NG:]
        sets = [_direct_copies(False, refs[q], refs[NG + q], *sems[3 * q:3 * q + 3]) for q in range(NG)]
        for copies in sets:
            _start_all(copies)
        for copies in sets:
            _wait_all(copies)

    return _pc(body, name=name, out_shape=[_sds((N_DEV,) + a.shape, a.dtype) for a in arrays],
               in_specs=[ANY] * NG, out_specs=[ANY] * NG, scratch_shapes=COMM_SEMS * NG)(*arrays)


def gather_two_level(arrays, name):
    NG = len(arrays)

    def body(*refs):
        mx, my, mc = _coords()
        me, sibling = (mx, my, mc), (mx, my, 1 - mc)
        chips = [(1 - mx, my), (mx, 1 - my), (1 - mx, 1 - my)]
        plans = []
        for q in range(NG):
            x_ref, out_ref = refs[q], refs[NG + q]
            send_sems, recv_sems, local_sem = refs[2 * NG + 3 * q:2 * NG + 3 * q + 3]

            def rows(block, out_ref=out_ref):
                return out_ref.at[4 * block[0] + 2 * block[1] + block[2]]

            def copy(k, block, to, src=None, rows=rows, send_sems=send_sems, recv_sems=recv_sems):
                return pltpu.make_async_remote_copy(
                    src_ref=rows(block) if src is None else src, dst_ref=rows(block),
                    send_sem=send_sems.at[k], recv_sem=recv_sems.at[k], device_id=to, device_id_type=MESH)

            mine = pltpu.make_async_copy(x_ref, rows(me), local_sem)
            first = [copy(0, me, sibling, src=x_ref)]
            first += [copy(1 + t, me, (*chip, mc), src=x_ref) for t, chip in enumerate(chips)]
            passed = [copy(4 + t, (*chip, mc), sibling) for t, chip in enumerate(chips)]
            _start_all([mine] + first)
            plans.append((copy, mine, first, passed))
        for copy, mine, first, passed in plans:
            for t, chip in enumerate(chips):
                copy(1 + t, (*chip, mc), me).wait_recv()
                passed[t].start()
        for copy, mine, first, passed in plans:
            copy(0, sibling, me).wait_recv()
            for t, chip in enumerate(chips):
                copy(4 + t, (*chip, 1 - mc), me).wait_recv()
            for cp in first + passed:
                cp.wait_send()
            mine.wait()

    return _pc(body, name=name, out_shape=[_sds((N_DEV,) + a.shape, a.dtype) for a in arrays],
               in_specs=[ANY] * NG, out_specs=[ANY] * NG, scratch_shapes=COMM_SEMS * NG)(*arrays)


def mod_columns(c_all, w_shard, b_cols):
    NCOL = w_shard.shape[1]

    def body(c_ref, w_ref, b_ref, o_ref):
        cv = c_ref[...]
        o_ref[...] = _nn((cv * _sigmoid(cv)).astype(BF16), w_ref[...].astype(BF16)) + b_ref[...]

    return _pc(body, name="mod_columns", out_shape=_sds((N_DEV, NCOL)),
               in_specs=[_const((N_DEV, D)), _const((D, NCOL)), _const((1, NCOL))],
               out_specs=_const((N_DEV, NCOL)), grid=(1,), compiler_params=_cp())(c_all, w_shard, b_cols)


def _norm_mod(xv, g, sc, sh):
    r = lax.rsqrt(_rowmean(xv * xv) + EPS)
    xh = xv * r
    return r, xh, xh * (g * (1.0 + sc)) + sh


def _log_sigmoid(v):
    e = jnp.exp(-jnp.abs(v))
    l1p = jnp.where(e < 1e-4, e * (1.0 - 0.5 * e), jnp.log(1.0 + e))
    return jnp.minimum(v, 0.0) - l1p


def fwd_in(x, n1g, sc1, sh1, wT, wr, qg_col, kg_col, bf_col, tri):
    S = x.shape[0]
    B = BLK
    NB = S // B

    def body(x_ref, g_ref, sc_ref, sh_ref, wT_ref, wr_ref, qg_ref, kg_ref, bf_ref, tri_ref,
             qTa_ref, kT_ref, kaug_ref, vT_ref, vaug_ref, zqk_ref, fgT_ref, alg_ref, u0_ref, stat_ref,
             carry, tbuf, fs, nq, nk):
        i = pl.program_id(0)

        @pl.when(i == 0)
        def _():
            carry[...] = jnp.zeros_like(carry)

        _, _, h = _norm_mod(x_ref[...], g_ref[...], sc_ref[...], sh_ref[...])
        hb = h.astype(BF16)
        zT = _nt(wT_ref[...], hb)
        zr = _nn(hb, wr_ref[...])
        zqk_ref[...] = zT[0:2 * A]
        fgT = zT[3 * A:3 * A + 8]
        fgT_ref[...] = fgT
        alg_ref[...] = zr
        u0_ref[...] = zr[:, 0:CW] * _sigmoid(zr[:, CW:2 * CW])

        logf = _log_sigmoid(fgT + bf_ref[...])
        a1, a2, a3 = _split3(logf)
        tr = tri_ref[...]
        F = _nn(a1, tr) + _nn(a2, tr) + _nn(a3, tr) + carry[...]
        carry[...] = carry[...] + _lanesum(logf)
        p1, p2, p3 = _split3(F * LOG2E)
        n1, n2, n3 = _split3(F * (-LOG2E))
        for k, v in enumerate((p1, p2, p3, n1, n2, n3)):
            fs[k] = v.astype(F32)

        rowi = lax.broadcasted_iota(jnp.int32, (8, B), 0)
        zeros_tail = jnp.zeros((128 - DH - 8, B), F32)
        ones_row = jnp.where(lax.broadcasted_iota(jnp.int32, (KR - DH, B), 0) == 0, 1.0, 0.0).astype(BF16)
        for hh in range(H):
            sl = slice(hh * DH, (hh + 1) * DH)
            q = zT[sl]
            k = zT[A + hh * DH:A + (hh + 1) * DH]
            v = zT[2 * A + hh * DH:2 * A + (hh + 1) * DH]
            qh = q * lax.rsqrt(jnp.mean(q * q, axis=0, keepdims=True) + EPS) * qg_ref[...] * (0.125 * LOG2E)
            kh = k * lax.rsqrt(jnp.mean(k * k, axis=0, keepdims=True) + EPS) * kg_ref[...]
            frow = [fs[kk, hh:hh + 1, :] for kk in range(6)]
            qx = jnp.where(rowi < 3, 1.0, jnp.where(rowi == 3, frow[0], jnp.where(
                rowi == 4, frow[1], jnp.where(rowi == 5, frow[2], 0.0))))
            kx = jnp.where(rowi == 0, frow[3], jnp.where(rowi == 1, frow[4], jnp.where(
                rowi == 2, frow[5], jnp.where(rowi < 6, 1.0, 0.0))))
            tbuf[0:DH, :] = qh
            tbuf[DH:DH + 8, :] = qx
            tbuf[DH + 8:128, :] = zeros_tail
            qTa_ref[0, hh * 128:(hh + 1) * 128, :] = tbuf[...].astype(BF16)
            tbuf[0:DH, :] = kh
            tbuf[DH:DH + 8, :] = kx
            kaug_ref[hh] = tbuf[...].T.astype(BF16)
            kT_ref[0, hh * KR:hh * KR + DH, :] = kh.astype(BF16)
            kT_ref[0, hh * KR + DH:(hh + 1) * KR, :] = ones_row
            tbuf[0:DH, :] = v
            tbuf[DH:DH + 8, :] = jnp.zeros((8, B), F32)
            vaug_ref[hh] = tbuf[...].T.astype(BF16)
            vT_ref[0, sl, :] = v.astype(BF16)
            nq[hh:hh + 1, :] = jnp.max(_colsum(qh * qh), axis=1, keepdims=True)
            nk[hh:hh + 1, :] = jnp.max(_colsum(kh * kh), axis=1, keepdims=True)

        lane = lax.broadcasted_iota(jnp.int32, (8, 128), 1)
        stat_ref[0] = jnp.where(lane == 0, jnp.max(F, axis=1, keepdims=True), jnp.where(
            lane == 1, jnp.min(F, axis=1, keepdims=True), jnp.where(
                lane == 2, nq[...], jnp.where(lane == 3, nk[...], 0.0))))

    row = lambda w: pl.BlockSpec((B, w), lambda i: (i, 0))
    tsp = lambda r: pl.BlockSpec((1, r, B), lambda i: (i, 0, 0))
    return _pc(
        body, name="fwd_in", grid=(NB,),
        in_specs=[row(D), _const((1, D)), _const((1, D)), _const((1, D)), _const((ZT_ROWS, D)),
                  _const((D, 2 * CW)), _const((DH, 1)), _const((DH, 1)), _const((8, 1)), _const((B, B))],
        out_specs=[tsp(H * 128), tsp(H * KR), pl.BlockSpec((H, B, 128), lambda i: (0, i, 0)), tsp(A),
                   pl.BlockSpec((H, B, 128), lambda i: (0, i, 0)),
                   pl.BlockSpec((2 * A, B), lambda i: (0, i)), pl.BlockSpec((8, B), lambda i: (0, i)),
                   row(2 * CW), row(CW), pl.BlockSpec((1, 8, 128), lambda i: (i, 0, 0))],
        out_shape=[_sds((NB, H * 128, B), BF16), _sds((NB, H * KR, B), BF16), _sds((H, S, 128), BF16),
                   _sds((NB, A, B), BF16), _sds((H, S, 128), BF16), _sds((2 * A, S)), _sds((8, S)),
                   _sds((S, 2 * CW)), _sds((S, CW)), _sds((NB, 8, 128))],
        scratch_shapes=[pltpu.VMEM((8, 1), F32), pltpu.VMEM((128, B), F32), pltpu.VMEM((6, 8, B), F32),
                        pltpu.VMEM((8, 1), F32), pltpu.VMEM((8, 1), F32)],
        compiler_params=_cp(("arbitrary",)),
    )(x, n1g, sc1, sh1, wT, wr, qg_col, kg_col, bf_col, tri)


def _first_key_block(top, fmin_ref, cut, h, i):
    return lax.while_loop(
        lambda j: jnp.logical_and(j > 0, top - fmin_ref[h, jnp.maximum(j - 1, 0)] >= cut),
        lambda j: j - 1, i)


def _causal_keep(B):
    return lax.broadcasted_iota(jnp.int32, (B, B), 0) <= lax.broadcasted_iota(jnp.int32, (B, B), 1)


def _keep_right_half(B):
    hb = B // 2
    return lax.broadcasted_iota(jnp.int32, (B, hb), 0) <= lax.broadcasted_iota(jnp.int32, (B, hb), 1) + hb


def _keep_top_half(B):
    hb = B // 2
    return lax.broadcasted_iota(jnp.int32, (hb, B), 0) <= lax.broadcasted_iota(jnp.int32, (hb, B), 1)


def attn_fwd(fmax, fmin, thr, qTa, kaug, vT, late_w):
    NB, _, B = qTa.shape
    S = NB * B
    NG = len(late_w)
    HB = B // 2

    def body(fmax_ref, fmin_ref, thr_ref, q_ref, k_ref, v_ref, *rest):
        w_refs = rest[0:NG]
        o_ref, lse_ref = rest[NG:NG + 2]
        gw_refs = rest[NG + 2:2 * NG + 2]
        s0, s1, m_ref, l_ref, acc_ref = rest[2 * NG + 2:2 * NG + 7]
        sems = rest[2 * NG + 7:]
        h = pl.program_id(0)
        i = pl.program_id(1)

        def gathers():
            return [_direct_copies(False, w_refs[q], gw_refs[q], *sems[3 * q:3 * q + 3]) for q in range(NG)]

        @pl.when(jnp.logical_and(h == 0, i == 0))
        def _():
            for copies in gathers():
                _start_all(copies)

        jlo = _first_key_block(fmax_ref[h, i], fmin_ref, thr_ref[h], h, i)
        n = i - jlo

        def scores(j, s_ref):
            s_ref[...] = _nn(k_ref[0, pl.ds(pl.multiple_of(j * B, B), B), :], q_ref[0])

        def softmax_cols(s, j, cols, rows):
            m = m_ref[:, cols]
            mn = jnp.maximum(m, jnp.max(s, axis=0, keepdims=True))
            a = jnp.exp2(m - mn)
            p = jnp.exp2(s - mn)
            m_ref[:, cols] = mn
            l_ref[:, cols] = a * l_ref[:, cols] + _colsum(p)
            acc_ref[:, cols] = a * acc_ref[:, cols] + _nn(v_ref[j][:, rows], p.astype(BF16))

        def softmax_step(s_ref, j, masked):
            if not masked:
                softmax_cols(s_ref[...], j, slice(None), slice(None))
                return
            lo, hi = slice(0, HB), slice(HB, B)
            softmax_cols(jnp.where(_causal_keep(HB), s_ref[lo, lo], -jnp.inf), j, lo, lo)
            softmax_cols(jnp.where(_keep_right_half(B), s_ref[:, hi], -jnp.inf), j, hi, slice(None))

        m_ref[...] = jnp.full((1, B), -jnp.inf, F32)
        l_ref[...] = jnp.zeros((1, B), F32)
        acc_ref[...] = jnp.zeros((DH, B), F32)
        scores(jlo, s0)

        def pair(t, carry):
            j = jlo + 2 * t
            scores(j + 1, s1)
            softmax_step(s0, j, False)
            scores(j + 2, s0)
            softmax_step(s1, j + 1, False)
            return carry

        lax.fori_loop(0, n // 2, pair, 0)

        @pl.when(n % 2 == 1)
        def _():
            scores(i, s1)
            softmax_step(s0, i - 1, False)
            softmax_step(s1, i, True)

        @pl.when(n % 2 == 0)
        def _():
            softmax_step(s0, i, True)

        l = l_ref[...]
        o_ref[0] = acc_ref[...] / l
        lse_ref[0, 0] = m_ref[...] + jnp.log2(l)

        @pl.when(jnp.logical_and(h == H - 1, i == NB - 1))
        def _():
            for copies in gathers():
                _wait_all(copies)

    return _pc(
        body, name="attn_fwd", grid=(H, NB),
        in_specs=[SMEM, SMEM, SMEM, pl.BlockSpec((1, 128, B), lambda h, i: (i, h, 0)),
                  pl.BlockSpec((1, S, 128), lambda h, i: (h, 0, 0)),
                  pl.BlockSpec((NB, DH, B), lambda h, i: (0, h, 0))] + [ANY] * NG,
        out_specs=[pl.BlockSpec((1, DH, B), lambda h, i: (i, h, 0)),
                   pl.BlockSpec((1, 1, 1, B), lambda h, i: (h, i, 0, 0))] + [ANY] * NG,
        out_shape=[_sds((NB, A, B)), _sds((H, NB, 1, B))]
        + [_sds((N_DEV,) + w.shape, w.dtype) for w in late_w],
        scratch_shapes=[pltpu.VMEM((B, B), F32), pltpu.VMEM((B, B), F32), pltpu.VMEM((1, B), F32),
                        pltpu.VMEM((1, B), F32), pltpu.VMEM((DH, B), F32)] + COMM_SEMS * NG,
        compiler_params=_cp(("arbitrary", "arbitrary")),
    )(fmax, fmin, thr, qTa, kaug, vT, *late_w)


def attn_bwd(fmax, fmin, thr, qTa, kaug, kT, vaug, doTa, lse, delta, early_g):
    NG = len(early_g)
    NB, _, B = qTa.shape
    QR = 80
    CH = min(256, B)

    def body(fmax_ref, fmin_ref, thr_ref, q_ref, ka_ref, kt_ref, va_ref, do_ref, lse_ref, dl_ref, *rest):
        g_refs = rest[0:NG]
        dq_ref, dk_ref, dv_ref, dfk_ref = rest[NG:NG + 4]
        recv_refs = rest[NG + 4:2 * NG + 4]
        s0, d0, s1, d1, p0, e0, p1, e1, dk_acc, dv_acc = rest[2 * NG + 4:2 * NG + 14]
        sems = rest[2 * NG + 14:]
        h = pl.program_id(0)
        j = pl.program_id(1)

        def exchanges():
            return [_direct_copies(True, g_refs[q], recv_refs[q], *sems[3 * q:3 * q + 3]) for q in range(NG)]

        @pl.when(jnp.logical_and(h == 0, j == 0))
        def _():
            for copies in exchanges():
                _start_all(copies)

        bottom = fmin_ref[h, j]
        cut = thr_ref[h]
        ihi = lax.while_loop(
            lambda i: jnp.logical_and(i < NB - 1, fmax_ref[h, jnp.minimum(i + 1, NB - 1)] - bottom >= cut),
            lambda i: i + 1, j)

        @pl.when(j == 0)
        def _():
            dq_ref[...] = jnp.zeros_like(dq_ref)

        dk_acc[...] = jnp.zeros_like(dk_acc)
        dv_acc[...] = jnp.zeros_like(dv_acc)

        n = ihi - j + 1

        def blk(k):
            return jnp.minimum(j + k, ihi)

        def products(k, s_ref, d_ref):
            i = blk(k)
            s_ref[...] = _nn(ka_ref[0], q_ref[i])
            d_ref[...] = _nn(va_ref[0], do_ref[i])

        def elementwise(k, s_ref, d_ref, p_ref, ds_ref, masked):
            i = blk(k)
            p = jnp.exp2(s_ref[...] - lse_ref[0, i])
            if masked:
                p = jnp.where(_causal_keep(B), p, 0.0)
            p_ref[...] = p.astype(BF16)
            ds_ref[...] = (p * (d_ref[...] - dl_ref[0, i])).astype(BF16)

        def grads(k, p_ref, ds_ref):
            i = j + k
            dsb = ds_ref[...]
            dv_acc[...] = dv_acc[...] + _nt(do_ref[i][0:DH], p_ref[...])
            dk_acc[...] = dk_acc[...] + _nt(q_ref[i][0:QR], dsb)
            dq_ref[i] = dq_ref[i] + _nn(kt_ref[0], dsb)

        def trip(kp, s_p, d_p, ke, s_e, d_e, p_e, e_e, kg, p_g, e_g):
            ip, ie, ig = blk(kp), blk(ke), j + kg
            lse_e = lse_ref[0, ie]
            dl_e = dl_ref[0, ie]
            dq_part = jnp.zeros((KR, B), F32)
            for c in range(B // CH):
                rows = slice(c * CH, (c + 1) * CH)
                s_p[rows, :] = _nn(ka_ref[0, rows, :], q_ref[ip])
                d_p[rows, :] = _nn(va_ref[0, rows, :], do_ref[ip])
                p = jnp.exp2(s_e[rows, :] - lse_e)
                p_e[rows, :] = p.astype(BF16)
                e_e[rows, :] = (p * (d_e[rows, :] - dl_e)).astype(BF16)
                dsb = e_g[rows, :]
                dv_acc[:, rows] = dv_acc[:, rows] + _nt(do_ref[ig][0:DH], p_g[rows, :])
                dk_acc[:, rows] = dk_acc[:, rows] + _nt(q_ref[ig][0:QR], dsb)
                dq_part = dq_part + _nn(kt_ref[0, :, rows], dsb)
            dq_ref[ig] = dq_ref[ig] + dq_part

        HB = B // 2
        lo, hi = slice(0, HB), slice(HB, B)

        def diagonal_products(s_ref, d_ref):
            s_ref[lo, :] = _nn(ka_ref[0, lo, :], q_ref[j])
            d_ref[lo, :] = _nn(va_ref[0, lo, :], do_ref[j])
            s_ref[hi, hi] = _nn(ka_ref[0, hi, :], q_ref[j, :, hi])
            d_ref[hi, hi] = _nn(va_ref[0, hi, :], do_ref[j, :, hi])

        def diagonal_elementwise(s_ref, d_ref, p_ref, ds_ref):
            lse0 = lse_ref[0, j]
            dl0 = dl_ref[0, j]
            p = jnp.where(_keep_top_half(B), jnp.exp2(s_ref[lo, :] - lse0), 0.0)
            p_ref[lo, :] = p.astype(BF16)
            ds_ref[lo, :] = (p * (d_ref[lo, :] - dl0)).astype(BF16)
            p = jnp.where(_causal_keep(HB), jnp.exp2(s_ref[hi, hi] - lse0[:, hi]), 0.0)
            p_ref[hi, hi] = p.astype(BF16)
            ds_ref[hi, hi] = (p * (d_ref[hi, hi] - dl0[:, hi])).astype(BF16)
            p_ref[hi, lo] = jnp.zeros((HB, HB), BF16)
            ds_ref[hi, lo] = jnp.zeros((HB, HB), BF16)

        diagonal_products(s0, d0)
        products(1, s1, d1)
        diagonal_elementwise(s0, d0, p0, e0)

        def pair(t, carry):
            k = 2 * t
            trip(k + 2, s0, d0, k + 1, s1, d1, p1, e1, k, p0, e0)
            trip(k + 3, s1, d1, k + 2, s0, d0, p0, e0, k + 1, p1, e1)
            return carry

        n_pairs = n // 2
        lax.fori_loop(0, n_pairs - 1, pair, 0)
        k_last = 2 * (n_pairs - 1)

        @pl.when(jnp.logical_and(n_pairs >= 1, n % 2 == 0))
        def _():
            elementwise(k_last + 1, s1, d1, p1, e1, False)
            grads(k_last, p0, e0)
            grads(k_last + 1, p1, e1)

        @pl.when(jnp.logical_and(n_pairs >= 1, n % 2 == 1))
        def _():
            trip(k_last + 2, s0, d0, k_last + 1, s1, d1, p1, e1, k_last, p0, e0)
            elementwise(k_last + 2, s0, d0, p0, e0, False)
            grads(k_last + 1, p1, e1)
            grads(k_last + 2, p0, e0)

        @pl.when(n == 1)
        def _():
            grads(0, p0, e0)

        dk_ref[0] = dk_acc[0:DH, :]
        dfk_ref[0, 0] = dk_acc[DH:DH + 8, :]
        dv_ref[0] = dv_acc[...]

        @pl.when(jnp.logical_and(h == H - 1, j == NB - 1))
        def _():
            for copies in exchanges():
                _wait_all(copies)

    per_kv = lambda r: pl.BlockSpec((1, r, B), lambda h, j: (j, h, 0))
    head_all = lambda r: pl.BlockSpec((NB, r, B), lambda h, j: (0, h, 0))
    aug = pl.BlockSpec((1, B, 128), lambda h, j: (h, j, 0))
    stat = pl.BlockSpec((1, NB, 1, B), lambda h, j: (h, 0, 0, 0))
    return _pc(
        body, name="attn_bwd", grid=(H, NB),
        in_specs=[SMEM, SMEM, SMEM, head_all(128), aug, per_kv(KR), aug, head_all(128), stat, stat]
        + [ANY] * NG,
        out_specs=[head_all(KR), per_kv(DH), per_kv(DH),
                   pl.BlockSpec((1, 1, 8, B), lambda h, j: (h, j, 0, 0))] + [ANY] * NG,
        out_shape=[_sds((NB, H * KR, B)), _sds((NB, A, B)), _sds((NB, A, B)), _sds((H, NB, 8, B))]
        + [_sds(g.shape, g.dtype) for g in early_g],
        scratch_shapes=[pltpu.VMEM((B, B), F32)] * 4 + [pltpu.VMEM((B, B), BF16)] * 4
        + [pltpu.VMEM((QR, B), F32), pltpu.VMEM((DH, B), F32)] + COMM_SEMS * NG,
        compiler_params=_cp(("arbitrary", "arbitrary")),
    )(fmax, fmin, thr, qTa, kaug, kT, vaug, doTa, lse, delta, *early_g)


def _conv_tail(u1, lng, lnb, beta_c):
    mu = _rowmean(u1)
    d = u1 - mu
    rstd = lax.rsqrt(_rowmean(d * d) + EPS)
    xhat = d * rstd
    u2 = xhat * lng + lnb
    sg = _sigmoid(u2)
    u3 = u2 * sg
    rc = lax.rsqrt(_rowmean(u3 * u3) + EPS)
    n3 = u3 * rc
    return rstd, xhat, u2, sg, rc, n3, n3 * beta_c


def _attn_tail(oT, beta_a_col):
    ra = lax.rsqrt(jnp.mean(oT * oT, axis=0, keepdims=True) + EPS)
    ohat = oT * ra
    return ra, ohat, ohat * beta_a_col


CONV_ROWS = 64
CORR_ROWS = 32


def _fill_shifted(buf, sh):
    rows = sh.shape[1]
    for ph in range(8):
        sh[ph] = buf[pl.ds(ph, rows), :]


def _tap_rows(sh, r0, o, rows):
    return sh[o % 8, pl.ds(pl.multiple_of(r0 + 8 * (o // 8), 8), rows), :]


def _depthwise(sh, w_ref, offs, bias, out_ref, B):
    def chunk(ci, carry):
        r0 = pl.multiple_of(ci * CONV_ROWS, CONV_ROWS)
        acc = jnp.broadcast_to(bias, (CONV_ROWS, CW))
        for k, o in enumerate(offs):
            acc = acc + w_ref[k:k + 1, :] * _tap_rows(sh, r0, o, CONV_ROWS)
        out_ref[pl.ds(r0, CONV_ROWS), :] = acc
        return carry

    lax.fori_loop(0, B // CONV_ROWS, chunk, 0)


def _tap_gradients(sh, d_ref, offs, dw_ref, B):
    for g0 in range(0, len(offs), 8):
        ks = list(range(g0, min(g0 + 8, len(offs))))

        def chunk(ci, accs, ks=ks):
            r0 = pl.multiple_of(ci * CORR_ROWS, CORR_ROWS)
            d = d_ref[pl.ds(r0, CORR_ROWS), :]
            out = []
            for a, k in zip(accs, ks):
                pr = d * _tap_rows(sh, r0, offs[k], CORR_ROWS)
                for r in range(0, CORR_ROWS, 8):
                    a = a + pr[r:r + 8]
                out.append(a)
            return tuple(out)

        accs = lax.fori_loop(0, B // CORR_ROWS, chunk, tuple(jnp.zeros((8, CW), F32) for _ in ks))
        for a, k in zip(accs, ks):
            dw_ref[k:k + 1, :] = dw_ref[k:k + 1, :] + _colsum(a)


def conv_merge_out(u0, x, oT, conv_w, conv_b, lng, lnb, beta_c, beta_a_col, wo, g1):
    S = x.shape[0]
    B = BLK
    NB = S // B

    def body(uc_ref, up_ref, x_ref, oT_ref, w_ref, cb_ref, lng_ref, lnb_ref, bc_ref, ba_ref, wo_ref,
             g1_ref, x1_ref, o_ref, u1_ref, ubuf, sh):
        i = pl.program_id(0)
        ubuf[0:HALO, :] = jnp.where(i > 0, up_ref[B - HALO:B, :], 0.0)
        ubuf[HALO:HALO + B, :] = uc_ref[...]
        ubuf[HALO + B:HALO + B + 8, :] = jnp.zeros((8, CW), F32)
        _fill_shifted(ubuf, sh)
        _depthwise(sh, w_ref, [HALO - (KC - 1) + k for k in range(KC)], cb_ref[...], u1_ref, B)
        mc = _conv_tail(u1_ref[...], lng_ref[...], lnb_ref[...], bc_ref[...])[-1]
        maT = _attn_tail(oT_ref[0], ba_ref[...])[-1]
        ma = maT.T
        o = _nn(ma.astype(BF16), wo_ref[0:A, :]) + _nn(mc.astype(BF16), wo_ref[A:D, :])
        o_ref[...] = o
        x1_ref[...] = x_ref[...] + g1_ref[...] * o

    row = lambda w: pl.BlockSpec((B, w), lambda i: (i, 0))
    return _pc(
        body, name="conv_merge_out", grid=(NB,),
        in_specs=[row(CW), pl.BlockSpec((B, CW), lambda i: (jnp.maximum(i - 1, 0), 0)), row(D),
                  pl.BlockSpec((1, A, B), lambda i: (i, 0, 0)), _const((32, CW)), _const((1, CW)),
                  _const((1, CW)), _const((1, CW)), _const((1, CW)), _const((A, 1)), _const((D, D)),
                  _const((1, D))],
        out_specs=[row(D), row(D), row(CW)],
        out_shape=[_sds((S, D)), _sds((S, D)), _sds((S, CW))],
        scratch_shapes=[pltpu.VMEM((B + HALO + 8, CW), F32), pltpu.VMEM((8, B + HALO, CW), F32)],
        compiler_params=_cp(("arbitrary",)),
    )(u0, u0, x, oT, conv_w, conv_b, lng, lnb, beta_c, beta_a_col, wo, g1)


def mlp_fwd_loss(x1, tgt, n2g, sc2, sh2, g2, w1s, w2):
    S = x1.shape[0]
    B = BLK
    NB = S // B
    NS, _, FS = w1s.shape

    def body(x1_ref, t_ref, g_ref, sc_ref, sh_ref, g2_ref, w1_hbm, w2_hbm, dy_ref, loss_ref, dg2_ref, ra_ref,
             w1_v, w2_v, sem):
        i = pl.program_id(0)

        @pl.when(i == 0)
        def _():
            for src, dst in ((w1_hbm, w1_v), (w2_hbm, w2_v)):
                cp = pltpu.make_async_copy(src, dst, sem)
                cp.start()
                cp.wait()
            loss_ref[...] = jnp.zeros_like(loss_ref)
            dg2_ref[...] = jnp.zeros_like(dg2_ref)

        x1 = x1_ref[...]
        hs = _norm_mod(x1, g_ref[...], sc_ref[...], sh_ref[...])[2].astype(BF16)
        y2 = jnp.zeros((B, D), F32)
        for q in range(NS):
            a = jnp.maximum(_nn(hs, w1_v[q]), 0.0)
            ra_ref[:, q * FS:(q + 1) * FS] = a.astype(BF16)
            y2 = y2 + _nn((a * a).astype(BF16), w2_v[q * FS:(q + 1) * FS, :])
        e = x1 + g2_ref[...] * y2 - t_ref[...]
        dy = e * (1.0 / D)
        dy_ref[...] = dy
        loss_ref[...] = loss_ref[...] + 0.5 * _colsum(_rowmean(e * e))
        dg2_ref[...] = dg2_ref[...] + _colsum(dy * y2)

    row = pl.BlockSpec((B, D), lambda i: (i, 0))
    return _pc(
        body, name="mlp_fwd_loss", grid=(NB,),
        in_specs=[row, row, _const((1, D)), _const((1, D)), _const((1, D)), _const((1, D)), ANY, ANY],
        out_specs=[row, _const((1, 1)), _const((1, D)), pl.BlockSpec((B, FF), lambda i: (i, 0))],
        out_shape=[_sds((S, D)), _sds((1, 1)), _sds((1, D)), _sds((S, FF), BF16)],
        scratch_shapes=[pltpu.VMEM(w1s.shape, BF16), pltpu.VMEM(w2.shape, BF16), pltpu.SemaphoreType.DMA(())],
        compiler_params=_cp(("arbitrary",)),
    )(x1, tgt, n2g, sc2, sh2, g2, w1s, w2)


def mlp_bwd(x1, dy, ra, n2g, sc2, sh2, g2, w1s, w2):
    S = x1.shape[0]
    B = BLK
    NB = S // B
    NC = FF // FCH
    NS, _, FS = w1s.shape
    SPC = NS // NC

    def body(x1_ref, dy_ref, ra_ref, g_ref, sc_ref, sh_ref, g2_ref, w1_ref, w2_ref, dw1_ref, dw2_ref, dh_ref):
        i = pl.program_id(1)

        @pl.when(i == 0)
        def _():
            dw1_ref[...] = jnp.zeros_like(dw1_ref)
            dw2_ref[...] = jnp.zeros_like(dw2_ref)

        h2 = _norm_mod(x1_ref[...], g_ref[...], sc_ref[...], sh_ref[...])[2]
        h2t = h2.T.astype(BF16)
        ra = ra_ref[...].astype(F32)
        dyb = (dy_ref[...] * g2_ref[...]).astype(BF16)
        db = _nt(dyb, w2_ref[...])
        dab = (db * (2.0 * ra)).astype(BF16)
        dh = jnp.zeros((B, D), F32)
        for q in range(SPC):
            part = dab[:, q * FS:(q + 1) * FS]
            dh = dh + _nt(part, w1_ref[q])
            dw1_ref[q] = dw1_ref[q] + _nn(h2t, part)
        dh_ref[0] = dh
        dw2_ref[...] = dw2_ref[...] + _nn((ra * ra).T.astype(BF16), dyb)

    row = pl.BlockSpec((B, D), lambda c, i: (i, 0))
    vec = pl.BlockSpec((1, D), lambda c, i: (0, 0))
    wsh = pl.BlockSpec((SPC, D, FS), lambda c, i: (c, 0, 0))
    w2b = pl.BlockSpec((FCH, D), lambda c, i: (c, 0))
    return _pc(
        body, name="mlp_bwd", grid=(NC, NB),
        in_specs=[row, row, pl.BlockSpec((B, FCH), lambda c, i: (i, c)), vec, vec, vec, vec, wsh, w2b],
        out_specs=[wsh, w2b, pl.BlockSpec((1, B, D), lambda c, i: (c, i, 0))],
        out_shape=[_sds(w1s.shape), _sds((FF, D)), _sds((NC, S, D))],
        compiler_params=_cp(("arbitrary", "arbitrary")),
    )(x1, dy, ra, n2g, sc2, sh2, g2, w1s, w2)


def merge_bwd(dh2p, x1, dy, o, oT, u1, n2g, sc2, g1, lng, lnb, beta_c, beta_a_col, wo):
    S = x1.shape[0]
    B = BLK
    NB = S // B
    NC = dh2p.shape[0]

    def body(dh_ref, x1_ref, dy_ref, o_ref, oT_ref, u1_ref, g_ref, sc_ref, g1_ref, lng_ref, lnb_ref,
             bc_ref, ba_ref, wo_ref,
             dx1_ref, doTa_ref, dl_ref, du1_ref, dwo_ref, v1_ref, v2_ref, dba_ref):
        i = pl.program_id(0)

        @pl.when(i == 0)
        def _():
            dwo_ref[...] = jnp.zeros_like(dwo_ref)
            v1_ref[...] = jnp.zeros_like(v1_ref)
            v2_ref[...] = jnp.zeros_like(v2_ref)
            dba_ref[...] = jnp.zeros_like(dba_ref)

        dh2 = dh_ref[0]
        for cc in range(1, NC):
            dh2 = dh2 + dh_ref[cc]
        x1 = x1_ref[...]
        g = g_ref[...]
        sc = sc_ref[...]
        r2 = lax.rsqrt(_rowmean(x1 * x1) + EPS)
        xh = x1 * r2
        dhx = dh2 * xh
        v1_ref[0:1, :] = v1_ref[0:1, :] + _colsum(dh2)
        v1_ref[1:2, :] = v1_ref[1:2, :] + _colsum(dhx) * g
        v1_ref[2:3, :] = v1_ref[2:3, :] + _colsum(dhx) * (1.0 + sc)
        dxh = dh2 * (g * (1.0 + sc))
        dx1 = dy_ref[...] + r2 * (dxh - xh * _rowmean(dxh * xh))
        dx1_ref[...] = dx1
        v1_ref[3:4, :] = v1_ref[3:4, :] + _colsum(dx1 * o_ref[...])
        dob = (dx1 * g1_ref[...]).astype(BF16)

        lng = lng_ref[...]
        bc = bc_ref[...]
        rstd, xhat, u2, sg, rc, n3, mc = _conv_tail(u1_ref[...], lng, lnb_ref[...], bc)
        ba = ba_ref[...]
        oT = oT_ref[0]
        ra, ohat, maT = _attn_tail(oT, ba)
        dwo_ref[0:A, :] = dwo_ref[0:A, :] + _nn(maT.astype(BF16), dob)
        dwo_ref[A:D, :] = dwo_ref[A:D, :] + _nn(mc.T.astype(BF16), dob)
        dmaT = _nt(wo_ref[0:A, :], dob)
        dmc = _nt(dob, wo_ref[A:D, :])

        dba_ref[...] = dba_ref[...] + _lanesum(dmaT * ohat)
        dohat = dmaT * ba
        doT = ra * (dohat - ohat * jnp.mean(dohat * ohat, axis=0, keepdims=True))
        prod = doT * oT
        zpad = jnp.zeros((128 - DH, B), BF16)
        for hh in range(H):
            sl = slice(hh * DH, (hh + 1) * DH)
            dl_ref[hh, 0] = _colsum(prod[sl])
            doTa_ref[0, hh * 128:hh * 128 + DH, :] = doT[sl].astype(BF16)
            doTa_ref[0, hh * 128 + DH:(hh + 1) * 128, :] = zpad

        v2_ref[0:1, :] = v2_ref[0:1, :] + _colsum(dmc * n3)
        dn3 = dmc * bc
        du3 = rc * (dn3 - n3 * _rowmean(dn3 * n3))
        du2 = du3 * (sg * (1.0 + u2 * (1.0 - sg)))
        v2_ref[1:2, :] = v2_ref[1:2, :] + _colsum(du2 * xhat)
        v2_ref[2:3, :] = v2_ref[2:3, :] + _colsum(du2)
        dxhat = du2 * lng
        du1_ref[...] = rstd * (dxhat - _rowmean(dxhat) - xhat * _rowmean(dxhat * xhat))

    row = lambda w: pl.BlockSpec((B, w), lambda i: (i, 0))
    return _pc(
        body, name="merge_bwd", grid=(NB,),
        in_specs=[pl.BlockSpec((NC, B, D), lambda i: (0, i, 0)), row(D), row(D), row(D),
                  pl.BlockSpec((1, A, B), lambda i: (i, 0, 0)), row(CW), _const((1, D)), _const((1, D)),
                  _const((1, D)), _const((1, CW)), _const((1, CW)), _const((1, CW)), _const((A, 1)),
                  _const((D, D))],
        out_specs=[row(D), pl.BlockSpec((1, H * 128, B), lambda i: (i, 0, 0)),
                   pl.BlockSpec((H, 1, 1, B), lambda i: (0, i, 0, 0)), row(CW), _const((D, D)),
                   _const((8, D)), _const((8, CW)), _const((A, 1))],
        out_shape=[_sds((S, D)), _sds((NB, H * 128, B), BF16), _sds((H, NB, 1, B)), _sds((S, CW)),
                   _sds((D, D)), _sds((8, D)), _sds((8, CW)), _sds((A, 1))],
        compiler_params=_cp(("arbitrary",)),
    )(dh2p, x1, dy, o, oT, u1, n2g, sc2, g1, lng, lnb, beta_c, beta_a_col, wo)


def head_bwd(du1, u0, alg, zqk, fgT, dqT, dkT, dvT, dfk, conv_w, qg_col, kg_col, bf_col, tri_lo):
    S = u0.shape[0]
    B = BLK
    NB = S // B

    def body(dc_ref, dn_ref, uc_ref, up_ref, alg_ref, zqk_ref, fg_ref, dq_ref, dk_ref, dv_ref, dfk_ref,
             w_ref, qg_ref, kg_ref, bf_ref, tri_ref,
             dzr_ref, dzT_ref, dcw_ref, vc_ref, dqg_ref, dkg_ref, dbf_ref,
             buf, sh, du0_ref, carry, fbuf):
        pid = pl.program_id(0)
        ri = NB - 1 - pid

        @pl.when(pid == 0)
        def _():
            carry[...] = jnp.zeros_like(carry)
            dcw_ref[...] = jnp.zeros_like(dcw_ref)
            vc_ref[...] = jnp.zeros_like(vc_ref)
            dqg_ref[...] = jnp.zeros_like(dqg_ref)
            dkg_ref[...] = jnp.zeros_like(dkg_ref)
            dbf_ref[...] = jnp.zeros_like(dbf_ref)

        zero8 = jnp.zeros((8, CW), F32)
        buf[0:B, :] = dc_ref[...]
        buf[B:B + HALO, :] = jnp.where(ri < NB - 1, dn_ref[0:HALO, :], 0.0)
        buf[B + HALO:B + HALO + 8, :] = zero8
        _fill_shifted(buf, sh)
        _depthwise(sh, w_ref, [KC - 1 - k for k in range(KC)], jnp.zeros((1, CW), F32), du0_ref, B)
        buf[0:HALO, :] = jnp.where(ri > 0, up_ref[B - HALO:B, :], 0.0)
        buf[HALO:HALO + B, :] = uc_ref[...]
        buf[HALO + B:HALO + B + 8, :] = zero8
        _fill_shifted(buf, sh)
        _tap_gradients(sh, dc_ref, [HALO - (KC - 1) + k for k in range(KC)], dcw_ref, B)
        vc_ref[0:1, :] = vc_ref[0:1, :] + _colsum(dc_ref[...])

        du0 = du0_ref[...]
        al = alg_ref[:, 0:CW]
        sg = _sigmoid(alg_ref[:, CW:2 * CW])
        dzr_ref[:, 0:CW] = (du0 * sg).astype(BF16)
        dzr_ref[:, CW:2 * CW] = (du0 * al * sg * (1.0 - sg)).astype(BF16)

        dqg = jnp.zeros((DH, B), F32)
        dkg = jnp.zeros((DH, B), F32)
        qg = qg_ref[...]
        kg = kg_ref[...]
        for hh in range(H):
            sl = slice(hh * DH, (hh + 1) * DH)
            q = zqk_ref[sl, :]
            rq = lax.rsqrt(jnp.mean(q * q, axis=0, keepdims=True) + EPS)
            qn = q * rq
            dqh = dq_ref[0, hh * KR:hh * KR + DH, :] * 0.125
            dqg = dqg + dqh * qn
            dqn = dqh * qg
            dzT_ref[0, sl, :] = (rq * (dqn - qn * jnp.mean(dqn * qn, axis=0, keepdims=True))).astype(BF16)
            k = zqk_ref[A + hh * DH:A + (hh + 1) * DH, :]
            rk = lax.rsqrt(jnp.mean(k * k, axis=0, keepdims=True) + EPS)
            kn = k * rk
            dkh = dk_ref[0, sl, :] * (1.0 / LOG2E)
            dkg = dkg + dkh * kn
            dkn = dkh * kg
            dzT_ref[0, A + hh * DH:A + (hh + 1) * DH, :] = (
                rk * (dkn - kn * jnp.mean(dkn * kn, axis=0, keepdims=True))).astype(BF16)
            fbuf[hh:hh + 1, :] = dq_ref[0, hh * KR + DH:hh * KR + DH + 1, :] - dfk_ref[hh, 0, 0:1, :]
        dqg_ref[...] = dqg_ref[...] + _lanesum(dqg)
        dkg_ref[...] = dkg_ref[...] + _lanesum(dkg)
        dzT_ref[0, 2 * A:3 * A, :] = dv_ref[0].astype(BF16)

        dF = fbuf[...]
        a1, a2, a3 = _split3(dF)
        tr = tri_ref[...]
        dlogf = _nn(a1, tr) + _nn(a2, tr) + _nn(a3, tr) + carry[...]
        carry[...] = carry[...] + _lanesum(dF)
        dfg = dlogf * _sigmoid(-(fg_ref[...] + bf_ref[...]))
        dbf_ref[...] = dbf_ref[...] + _lanesum(dfg)
        dzT_ref[0, 3 * A:ZT_ROWS, :] = jnp.concatenate([dfg, jnp.zeros((8, B), F32)], axis=0).astype(BF16)

    rrow = lambda w: pl.BlockSpec((B, w), lambda p: (NB - 1 - p, 0))
    rts = lambda r: pl.BlockSpec((1, r, B), lambda p: (NB - 1 - p, 0, 0))
    return _pc(
        body, name="head_bwd", grid=(NB,),
        in_specs=[rrow(CW), pl.BlockSpec((B, CW), lambda p: (jnp.minimum(NB - p, NB - 1), 0)),
                  rrow(CW), pl.BlockSpec((B, CW), lambda p: (jnp.maximum(NB - 2 - p, 0), 0)),
                  rrow(2 * CW), pl.BlockSpec((2 * A, B), lambda p: (0, NB - 1 - p)),
                  pl.BlockSpec((8, B), lambda p: (0, NB - 1 - p)), rts(H * KR), rts(A), rts(A),
                  pl.BlockSpec((H, 1, 8, B), lambda p: (0, NB - 1 - p, 0, 0)),
                  _const((32, CW)), _const((DH, 1)), _const((DH, 1)), _const((8, 1)), _const((B, B))],
        out_specs=[rrow(2 * CW), rts(ZT_ROWS), _const((32, CW)), _const((8, CW)), _const((DH, 1)),
                   _const((DH, 1)), _const((8, 1))],
        out_shape=[_sds((S, 2 * CW), BF16), _sds((NB, ZT_ROWS, B), BF16), _sds((32, CW)), _sds((8, CW)),
                   _sds((DH, 1)), _sds((DH, 1)), _sds((8, 1))],
        scratch_shapes=[pltpu.VMEM((B + HALO + 8, CW), F32), pltpu.VMEM((8, B + HALO, CW), F32),
                        pltpu.VMEM((B, CW), F32), pltpu.VMEM((8, 1), F32), pltpu.VMEM((8, B), F32)],
        compiler_params=_cp(("arbitrary",)),
    )(du1, du1, u0, u0, alg, zqk, fgT, dqT, dkT, dvT, dfk, conv_w, qg_col, kg_col, bf_col, tri_lo)


def in_bwd_w(x, dzr, dzT, n1g, sc1, sh1):
    S = x.shape[0]
    B = BLK
    NB = S // B

    def body(x_ref, dzr_ref, dzT_ref, g_ref, sc_ref, sh_ref, dwT_hbm, dwr_hbm, dwT_acc, dwr_acc, sem):
        i = pl.program_id(0)

        @pl.when(i == 0)
        def _():
            dwT_acc[...] = jnp.zeros_like(dwT_acc)
            dwr_acc[...] = jnp.zeros_like(dwr_acc)

        h1 = _norm_mod(x_ref[...], g_ref[...], sc_ref[...], sh_ref[...])[2]
        dwT_acc[...] = dwT_acc[...] + _nn(dzT_ref[0], h1.astype(BF16))
        dwr_acc[...] = dwr_acc[...] + _nn(h1.T.astype(BF16), dzr_ref[...])

        @pl.when(i == NB - 1)
        def _():
            for src, dst in ((dwT_acc, dwT_hbm), (dwr_acc, dwr_hbm)):
                cp = pltpu.make_async_copy(src, dst, sem)
                cp.start()
                cp.wait()

    row = lambda w: pl.BlockSpec((B, w), lambda i: (i, 0))
    return _pc(
        body, name="in_bwd_w", grid=(NB,),
        in_specs=[row(D), row(2 * CW), pl.BlockSpec((1, ZT_ROWS, B), lambda i: (i, 0, 0)),
                  _const((1, D)), _const((1, D)), _const((1, D))],
        out_specs=[ANY, ANY],
        out_shape=[_sds((ZT_ROWS, D)), _sds((D, 2 * CW))],
        scratch_shapes=[pltpu.VMEM((ZT_ROWS, D), F32), pltpu.VMEM((D, 2 * CW), F32),
                        pltpu.SemaphoreType.DMA(())],
        compiler_params=_cp(("arbitrary",)),
    )(x, dzr, dzT, n1g, sc1, sh1)


def in_bwd_x(x, dx1, dzr, dzT, n1g, sc1, sh1, w_qkvf, wr, late_g):
    S = x.shape[0]
    B = BLK
    NB = S // B

    def body(x_ref, dx1_ref, dzr_ref, dzT_ref, g_ref, sc_ref, sh_ref, w_hbm, wr_hbm, lg_ref,
             gx_ref, v_ref, recv_ref, w_v, wr_v, sem, send_sems, recv_sems, local_sem):
        i = pl.program_id(0)

        @pl.when(i == 0)
        def _():
            _start_all(_direct_copies(True, lg_ref, recv_ref, send_sems, recv_sems, local_sem))
            for src, dst in ((w_hbm, w_v), (wr_hbm, wr_v)):
                cp = pltpu.make_async_copy(src, dst, sem)
                cp.start()
                cp.wait()
            v_ref[...] = jnp.zeros_like(v_ref)

        g = g_ref[...]
        sc = sc_ref[...]
        r1, xh, _ = _norm_mod(x_ref[...], g, sc, sh_ref[...])
        dh1 = _nt(dzr_ref[...], wr_v[...]) + _nn(w_v[...], dzT_ref[0]).T
        dhx = dh1 * xh
        v_ref[0:1, :] = v_ref[0:1, :] + _colsum(dh1)
        v_ref[1:2, :] = v_ref[1:2, :] + _colsum(dhx) * g
        v_ref[2:3, :] = v_ref[2:3, :] + _colsum(dhx) * (1.0 + sc)
        dxh = dh1 * (g * (1.0 + sc))
        gx_ref[...] = dx1_ref[...] + r1 * (dxh - xh * _rowmean(dxh * xh))

        @pl.when(i == NB - 1)
        def _():
            _wait_all(_direct_copies(True, lg_ref, recv_ref, send_sems, recv_sems, local_sem))

    row = lambda w: pl.BlockSpec((B, w), lambda i: (i, 0))
    return _pc(
        body, name="in_bwd_x", grid=(NB,),
        in_specs=[row(D), row(D), row(2 * CW), pl.BlockSpec((1, ZT_ROWS, B), lambda i: (i, 0, 0)),
                  _const((1, D)), _const((1, D)), _const((1, D)), ANY, ANY, ANY],
        out_specs=[row(D), _const((8, D)), ANY],
        out_shape=[_sds((S, D)), _sds((8, D)), _sds(late_g.shape, late_g.dtype)],
        scratch_shapes=[pltpu.VMEM((D, ZT_ROWS), BF16), pltpu.VMEM((D, 2 * CW), BF16),
                        pltpu.SemaphoreType.DMA(())] + COMM_SEMS,
        compiler_params=_cp(("arbitrary",)),
    )(x, dx1, dzr, dzT, n1g, sc1, sh1, w_qkvf, wr, late_g)


def _adam_math(g, w, m, v):
    m = ADAM_B1 * m + (1.0 - ADAM_B1) * g
    v = ADAM_B2 * v + (1.0 - ADAM_B2) * (g * g)
    m_hat = m / (1.0 - ADAM_B1 ** ADAM_STEP)
    v_hat = v / (1.0 - ADAM_B2 ** ADAM_STEP)
    delta = -ADAM_LR * (m_hat / (jnp.sqrt(v_hat) + ADAM_EPS) + ADAM_WD * w)
    return delta, m, v


def _row_tile(R):
    for t in (1024, 512, 256, 128, 64, 32, 16, 8):
        if R % t == 0:
            return t
    return R


def sum_slots(parts, name):
    K, R, C = parts.shape
    T = _row_tile(R)

    def body(p_ref, o_ref):
        s = p_ref[0]
        for k in range(1, K):
            s = s + p_ref[k]
        o_ref[...] = s

    return _pc(body, name=name, grid=(R // T,),
               in_specs=[pl.BlockSpec((K, T, C), lambda i: (0, i, 0))],
               out_specs=pl.BlockSpec((T, C), lambda i: (i, 0)), out_shape=_sds((R, C)),
               compiler_params=_cp(("arbitrary",)))(parts)


def adamw_slots(parts, w, m, v, name):
    K, R, C = parts.shape
    T = R
    while K * T * C * 4 > (8 << 20) and T % 16 == 0:
        T //= 2

    def body(p_ref, w_ref, m_ref, v_ref, g_ref, d_ref, nm_ref, nv_ref):
        g = p_ref[0].astype(F32)
        for k in range(1, K):
            g = g + p_ref[k].astype(F32)
        g_ref[...] = g
        d_ref[...], nm_ref[...], nv_ref[...] = _adam_math(g, w_ref[...], m_ref[...], v_ref[...])

    t2 = pl.BlockSpec((T, C), lambda i: (i, 0))
    return _pc(body, name=name, grid=(R // T,),
               in_specs=[pl.BlockSpec((K, T, C), lambda i: (0, i, 0)), t2, t2, t2],
               out_specs=[t2, t2, t2, t2], out_shape=[_sds((R, C))] * 4,
               compiler_params=_cp(("arbitrary",)))(parts, w, m, v)


def adamw_many(gs, ws, ms, vs, name):
    n = len(ws)

    def body(*refs):
        outs = refs[4 * n:]
        for q in range(n):
            d, nm, nv = _adam_math(refs[q][...], refs[n + q][...], refs[2 * n + q][...], refs[3 * n + q][...])
            outs[q][...] = d
            outs[n + q][...] = nm
            outs[2 * n + q][...] = nv

    res = _pc(body, name=name, grid=(1,),
              in_specs=[_const(a.shape) for a in list(gs) + list(ws) + list(ms) + list(vs)],
              out_specs=[_const(w.shape) for w in ws] * 3, out_shape=[_sds(w.shape) for w in ws] * 3,
              compiler_params=_cp())(*gs, *ws, *ms, *vs)
    return res[0:n], res[n:2 * n], res[2 * n:3 * n]


def ada_grad_adamw(cT, dmod_cols, w, m, v):
    NCOL = w.shape[1]

    def body(cT_ref, dm_ref, w_ref, m_ref, v_ref, g_ref, d_ref, nm_ref, nv_ref):
        def term(b):
            cv = cT_ref[b]
            return (cv * _sigmoid(cv)) * dm_ref[b:b + 1, :]

        g = term(0)
        for b in range(1, N_DEV):
            g = g + term(b)
        g_ref[...] = g
        d_ref[...], nm_ref[...], nv_ref[...] = _adam_math(g, w_ref[...], m_ref[...], v_ref[...])

    full = _const((D, NCOL))
    return _pc(body, name="ada_grad_adamw", grid=(1,),
               in_specs=[_const((N_DEV, D, 1)), _const((N_DEV, NCOL)), full, full, full],
               out_specs=[full, full, full, full], out_shape=[_sds((D, NCOL))] * 4,
               compiler_params=_cp())(cT, dmod_cols, w, m, v)


def _pack_rows(vecs, rows=None):
    flat = jnp.concatenate([jnp.ravel(v) for v in vecs])
    n = flat.shape[0]
    r = -(-n // 1024) * 8 if rows is None else rows
    return jnp.pad(flat, (0, r * 128 - n)).reshape(r, 128)


def _unpack_rows(packed, shapes):
    flat = packed.reshape(-1)
    out, off = [], 0
    for s in shapes:
        n = 1
        for d in s:
            n *= d
        out.append(flat[off:off + n].reshape(s))
        off += n
    return out


def _cols_to_shards(w, cols):
    rows = w.shape[0]
    return w.reshape(rows, N_DEV, cols).transpose(1, 0, 2).reshape(N_DEV, rows * cols)


def kernel(x, c, w_ada, b_ada, norm1_g, w_in, q_norm_g, k_norm_g, b_f, conv_w, conv_b, conv_ln_g, conv_ln_b, beta_attn, beta_conv, w_out, norm2_g, w_ff1, w_ff2, loss_target, m_w_ada, m_b_ada, m_norm1_g, m_w_in, m_q_norm_g, m_k_norm_g, m_b_f, m_conv_w, m_conv_b, m_conv_ln_g, m_conv_ln_b, m_beta_attn, m_beta_conv, m_w_out, m_norm2_g, m_w_ff1, m_w_ff2, v_w_ada, v_b_ada, v_norm1_g, v_w_in, v_q_norm_g, v_k_norm_g, v_b_f, v_conv_w, v_conv_b, v_conv_ln_g, v_conv_ln_b, v_beta_attn, v_beta_conv, v_w_out, v_norm2_g, v_w_ff1, v_w_ff2):
    S = x.shape[1]
    B = BLK
    NB = S // B
    me = 4 * lax.axis_index("x") + 2 * lax.axis_index("y") + lax.axis_index("c")
    xs = x[0]
    tgt = loss_target[0]

    ADA_C, IN_C, FF_C, CV_C = w_ada.shape[2], w_in.shape[2], w_ff1.shape[2], conv_w.shape[2]
    OUT_R, FF_R = w_out.shape[1], w_ff2.shape[1]

    g_in, g_c, g_cw = gather_two_level([w_in[0].T.astype(BF16), c, conv_w[0]], "ag_weights_early")
    late_w = [w_out[0].astype(BF16), w_ff1[0].astype(BF16), w_ff2[0].astype(BF16)]

    w_in_t = g_in.reshape(N_DEV * IN_C, D)
    c_all = g_c.reshape(N_DEV, D)
    conv_w_full = g_cw.transpose(1, 0, 2).reshape(KC, N_DEV * CV_C)
    conv_w_pad = jnp.pad(conv_w_full, ((0, 32 - KC), (0, 0)))

    wT = jnp.pad(w_in_t[0:3 * A + H], ((0, ZT_ROWS - 3 * A - H), (0, 0)))
    w_qkvf = wT.T
    wr = w_in_t[3 * A + H:].T

    qg_col = q_norm_g.reshape(DH, 1)
    kg_col = k_norm_g.reshape(DH, 1)
    bf_col = b_f.reshape(H, 1)
    beta_a_col = beta_attn.reshape(A, 1)
    ii = lax.broadcasted_iota(jnp.int32, (B, B), 0)
    jj = lax.broadcasted_iota(jnp.int32, (B, B), 1)
    tri_up = (ii <= jj).astype(BF16)
    tri_lo = (ii >= jj).astype(BF16)

    modc = mod_columns(c_all, w_ada[0], lax.dynamic_slice(b_ada, (0, me * ADA_C), (1, ADA_C)))
    gm, = gather_direct([modc], "ag_mod")
    mod = lax.dynamic_index_in_dim(gm, me, axis=1, keepdims=False).reshape(1, N_DEV * ADA_C)
    sh1, sc1, g1, sh2, sc2, g2 = [mod[:, k * D:(k + 1) * D] for k in range(6)]
    qTa, kT, kaug, vT, vaug, zqk, fgT, alg, u0, stat = fwd_in(xs, norm1_g, sc1, sh1, wT, wr, qg_col,
                                                             kg_col, bf_col, tri_up)
    fmax = stat[:, :, 0].T
    fmin = stat[:, :, 1].T
    qk_max = jnp.sqrt(jnp.max(stat[:, :, 2], axis=0) * jnp.max(stat[:, :, 3], axis=0))
    thr = -(PRUNE + (2.02 / LOG2E) * qk_max)
    oT, lse, g_wo, w1_shards, g_w2 = attn_fwd(fmax, fmin, thr, qTa, kaug, vT, late_w)
    wo_full = g_wo.reshape(D, D)
    w2_full = g_w2.reshape(FF, D)
    x1, o, u1 = conv_merge_out(u0, xs, oT, conv_w_pad, conv_b, conv_ln_g, conv_ln_b, beta_conv,
                               beta_a_col, wo_full, g1)
    dy, loss_part, dg2, ra = mlp_fwd_loss(x1, tgt, norm2_g, sc2, sh2, g2, w1_shards, w2_full)

    dw1s, dw2, dh2p = mlp_bwd(x1, dy, ra, norm2_g, sc2, sh2, g2, w1_shards, w2_full)
    dx1, doTa, delta, du1, dwo, v1, v2, dba = merge_bwd(dh2p, x1, dy, o, oT, u1, norm2_g, sc2, g1,
                                                         conv_ln_g, conv_ln_b, beta_conv, beta_a_col, wo_full)
    early_g = [dwo.reshape(N_DEV, OUT_R, D), dw1s, dw2.reshape(N_DEV, FF_R, D)]
    dqT, dkT, dvT, dfk, r_out, r_f1, r_f2 = attn_bwd(fmax, fmin, thr, qTa, kaug, kT, vaug, doTa, lse,
                                                     delta, early_g)
    dzr, dzT, dcw, vc, dqg, dkg, dbf = head_bwd(du1, u0, alg, zqk, fgT, dqT, dkT, dvT, dfk, conv_w_pad,
                                                qg_col, kg_col, bf_col, tri_lo)
    dwT, dwr = in_bwd_w(xs, dzr, dzT, norm1_g, sc1, sh1)

    dw_in = jnp.concatenate([dwT.T[:, 0:3 * A + H], dwr], axis=1)
    late_g = _cols_to_shards(dw_in, IN_C)
    rows_b = -(-late_g.shape[1] // (128 * 256)) * 256
    late_g = jnp.pad(late_g, ((0, 0), (0, rows_b * 128 - late_g.shape[1]))).astype(BF16)
    grad_x, v0, recv_b = in_bwd_x(xs, dx1, dzr, dzT, norm1_g, sc1, sh1, w_qkvf, wr,
                                  late_g.reshape(N_DEV, rows_b, 128))

    dmod = jnp.concatenate([v0[0], v0[1], v1[3], v1[0], v1[1], dg2[0]])
    small1 = _pack_rows([dmod, v0[2], v1[2], dcw, vc[0], v2[1], v2[2], v2[0], dba, dqg, dkg,
                         jnp.pad(dbf.reshape(-1), (0, 120)), jnp.pad(loss_part.reshape(-1), (0, 127))])
    gs1, = gather_direct([small1], "ag_small_bwd")
    tot = sum_slots(gs1, "sum_small")
    (g_b_ada, g_n1, g_n2, g_cw, g_cb, g_lng, g_lnb, g_bc, g_ba, g_qg, g_kg, g_bf, loss_v) = _unpack_rows(
        tot, [(1, 6 * D), (1, D), (1, D), (32, CW), (1, CW), (1, CW), (1, CW), (1, CW), (1, A),
              (1, DH), (1, DH), (1, 128), (1, 128)])
    loss = loss_v[0, 0]
    g_bf = g_bf[:, 0:H]
    g_cw_mine = lax.dynamic_slice(g_cw[0:KC], (0, me * CV_C), (KC, CV_C)).reshape(1, KC, CV_C)

    small_names = [(b_ada, m_b_ada, v_b_ada, g_b_ada), (norm1_g, m_norm1_g, v_norm1_g, g_n1),
                   (q_norm_g, m_q_norm_g, v_q_norm_g, g_qg), (k_norm_g, m_k_norm_g, v_k_norm_g, g_kg),
                   (b_f, m_b_f, v_b_f, g_bf), (conv_w, m_conv_w, v_conv_w, g_cw_mine),
                   (conv_b, m_conv_b, v_conv_b, g_cb), (conv_ln_g, m_conv_ln_g, v_conv_ln_g, g_lng),
                   (conv_ln_b, m_conv_ln_b, v_conv_ln_b, g_lnb), (beta_attn, m_beta_attn, v_beta_attn, g_ba),
                   (beta_conv, m_beta_conv, v_beta_conv, g_bc), (norm2_g, m_norm2_g, v_norm2_g, g_n2)]
    sgs = [t[3].reshape(t[0].shape) for t in small_names]
    sds, sms, svs = adamw_many(sgs, *[[t[k] for t in small_names] for k in range(3)], "adamw_small")

    dmod_all = gs1[:, 0:48, :].reshape(N_DEV, N_DEV, ADA_C)
    dmod_cols = lax.dynamic_index_in_dim(dmod_all, me, axis=1, keepdims=False)
    ga, da, ma_, va_ = ada_grad_adamw(c_all.reshape(N_DEV, D, 1), dmod_cols, w_ada[0], m_w_ada[0], v_w_ada[0])

    res_b = adamw_slots(recv_b, *[_pack_rows(ws, rows=rows_b) for ws in ([w_in], [m_w_in], [v_w_in])],
                        "adamw_in")
    late = []
    for nm, parts, (w_, m_, v_) in (("adamw_out", r_out, (w_out, m_w_out, v_w_out)),
                                    ("adamw_ff1", r_f1, (w_ff1, m_w_ff1, v_w_ff1)),
                                    ("adamw_ff2", r_f2, (w_ff2, m_w_ff2, v_w_ff2))):
        late.append([r[None] for r in adamw_slots(parts, w_[0], m_[0], v_[0], nm)])
    bgs, bds, bms, bvs = [_unpack_rows(res_b[q], [(1, D, IN_C)]) + [late[0][q], late[1][q], late[2][q]]
                          for q in range(4)]

    def assemble(small, ada, bigs):
        (b_ada_, n1_, qg_, kg_, bf_, cw_, cb_, lng_, lnb_, ba_, bc_, n2_) = small
        return [ada.reshape(1, D, ADA_C), b_ada_, n1_, bigs[0], qg_, kg_, bf_, cw_, cb_, lng_, lnb_, ba_, bc_,
                bigs[1], n2_, bigs[2], bigs[3]]

    return (loss, grad_x.reshape(1, S, D), *assemble(sgs, ga, bgs), *assemble(sds, da, bds),
            *assemble(sms, ma_, bms), *assemble(svs, va_, bvs))
```

```python
import functools

import jax
import jax.numpy as jnp
from jax import lax
from jax.experimental import pallas as pl
from jax.experimental.pallas import tpu as pltpu

F32 = jnp.float32
BF16 = jnp.bfloat16

D = 1024
A = 512
CW = 512
H = 8
DH = 64
FF = 4096
KC = 31
HALO = 32
KR = 80
ZT_ROWS = 1552
FCH = 1024
EPS = 1e-6
BLK = 512
N_DEV = 8
VMEM_LIMIT = 56 * 1024 * 1024

ADAM_LR = 0.001
ADAM_B1 = 0.9
ADAM_B2 = 0.999
ADAM_EPS = 1e-08
ADAM_WD = 0.01
ADAM_STEP = 10

MESH = pl.DeviceIdType.MESH
ANY = pl.BlockSpec(memory_space=pl.ANY)
SMEM = pl.BlockSpec(memory_space=pltpu.SMEM)
PRUNE = 105.0
LOG2E = 1.4426950408889634


def _pc(body, **kw):
    return pl.pallas_call(body, **kw)


def _cp(sem=None):
    return pltpu.CompilerParams(dimension_semantics=sem, vmem_limit_bytes=VMEM_LIMIT)


def _sds(shape, dtype=F32):
    return jax.ShapeDtypeStruct(shape, dtype)


def _const(shape):
    n = len(shape)
    return pl.BlockSpec(shape, lambda *a: (0,) * n)


def _nt(a, b):
    return lax.dot_general(a, b, (((1,), (1,)), ((), ())), preferred_element_type=F32)


def _nn(a, b):
    return jnp.dot(a, b, preferred_element_type=F32)


def _sigmoid(v):
    return 1.0 / (1.0 + jnp.exp(-v))


def _split3(v):
    a1 = v.astype(BF16)
    r1 = v - a1.astype(F32)
    a2 = r1.astype(BF16)
    a3 = (r1 - a2.astype(F32)).astype(BF16)
    return a1, a2, a3


def _rowmean(v):
    return jnp.mean(v, axis=-1, keepdims=True)


def _colsum(v):
    return jnp.sum(v, axis=0, keepdims=True)


def _lanesum(v):
    return jnp.sum(v, axis=-1, keepdims=True)


def _coords():
    return lax.axis_index("x"), lax.axis_index("y"), lax.axis_index("c")


def _direct_copies(scatter, src_ref, dst_ref, send_sems, recv_sems, local_sem):
    mx, my, mc = _coords()
    me = 4 * mx + 2 * my + mc
    if scatter:
        local = pltpu.make_async_copy(src_ref.at[me], dst_ref.at[0], local_sem)
    else:
        local = pltpu.make_async_copy(src_ref, dst_ref.at[me], local_sem)
    remote = []
    for r in range(1, N_DEV):
        px = 1 - mx if r & 4 else mx
        py = 1 - my if r & 2 else my
        pcc = 1 - mc if r & 1 else mc
        remote.append(pltpu.make_async_remote_copy(
            src_ref=src_ref.at[4 * px + 2 * py + pcc] if scatter else src_ref,
            dst_ref=dst_ref.at[r] if scatter else dst_ref.at[me],
            send_sem=send_sems.at[r - 1], recv_sem=recv_sems.at[r - 1],
            device_id=(px, py, pcc), device_id_type=MESH))
    return [local] + remote


def _start_all(copies):
    for cp in copies:
        cp.start()


def _wait_all(copies):
    for cp in copies[1:]:
        cp.wait()
    copies[0].wait()


COMM_SEMS = [pltpu.SemaphoreType.DMA((7,)), pltpu.SemaphoreType.DMA((7,)), pltpu.SemaphoreType.DMA(())]


def gather_direct(arrays, name):
    NG = len(arrays)

    def body(*refs):
        sems = refs[2 * NG:]
        sets = [_direct_copies(False, refs[q], refs[NG + q], *sems[3 * q:3 * q + 3]) for q in range(NG)]
        for copies in sets:
            _start_all(copies)
        for copies in sets:
            _wait_all(copies)

    return _pc(body, name=name, out_shape=[_sds((N_DEV,) + a.shape, a.dtype) for a in arrays],
               in_specs=[ANY] * NG, out_specs=[ANY] * NG, scratch_shapes=COMM_SEMS * NG)(*arrays)


def gather_two_level(arrays, name):
    NG = len(arrays)

    def body(*refs):
        mx, my, mc = _coords()
        me, sibling = (mx, my, mc), (mx, my, 1 - mc)
        chips = [(1 - mx, my), (mx, 1 - my), (1 - mx, 1 - my)]
        plans = []
        for q in range(NG):
            x_ref, out_ref = refs[q], refs[NG + q]
            send_sems, recv_sems, local_sem = refs[2 * NG + 3 * q:2 * NG + 3 * q + 3]

            def rows(block, out_ref=out_ref):
                return out_ref.at[4 * block[0] + 2 * block[1] + block[2]]

            def copy(k, block, to, src=None, rows=rows, send_sems=send_sems, recv_sems=recv_sems):
                return pltpu.make_async_remote_copy(
                    src_ref=rows(block) if src is None else src, dst_ref=rows(block),
                    send_sem=send_sems.at[k], recv_sem=recv_sems.at[k], device_id=to, device_id_type=MESH)

            mine = pltpu.make_async_copy(x_ref, rows(me), local_sem)
            first = [copy(0, me, sibling, src=x_ref)]
            first += [copy(1 + t, me, (*chip, mc), src=x_ref) for t, chip in enumerate(chips)]
            passed = [copy(4 + t, (*chip, mc), sibling) for t, chip in enumerate(chips)]
            _start_all([mine] + first)
            plans.append((copy, mine, first, passed))
        for copy, mine, first, passed in plans:
            for t, chip in enumerate(chips):
                copy(1 + t, (*chip, mc), me).wait_recv()
                passed[t].start()
        for copy, mine, first, passed in plans:
            copy(0, sibling, me).wait_recv()
            for t, chip in enumerate(chips):
                copy(4 + t, (*chip, 1 - mc), me).wait_recv()
            for cp in first + passed:
                cp.wait_send()
            mine.wait()

    return _pc(body, name=name, out_shape=[_sds((N_DEV,) + a.shape, a.dtype) for a in arrays],
               in_specs=[ANY] * NG, out_specs=[ANY] * NG, scratch_shapes=COMM_SEMS * NG)(*arrays)


def mod_columns(c_all, w_shard, b_cols):
    NCOL = w_shard.shape[1]

    def body(c_ref, w_ref, b_ref, o_ref):
        cv = c_ref[...]
        o_ref[...] = _nn((cv * _sigmoid(cv)).astype(BF16), w_ref[...].astype(BF16)) + b_ref[...]

    return _pc(body, name="mod_columns", out_shape=_sds((N_DEV, NCOL)),
               in_specs=[_const((N_DEV, D)), _const((D, NCOL)), _const((1, NCOL))],
               out_specs=_const((N_DEV, NCOL)), grid=(1,), compiler_params=_cp())(c_all, w_shard, b_cols)


def _norm_mod(xv, g, sc, sh):
    r = lax.rsqrt(_rowmean(xv * xv) + EPS)
    xh = xv * r
    return r, xh, xh * (g * (1.0 + sc)) + sh


def _log_sigmoid(v):
    e = jnp.exp(-jnp.abs(v))
    l1p = jnp.where(e < 1e-4, e * (1.0 - 0.5 * e), jnp.log(1.0 + e))
    return jnp.minimum(v, 0.0) - l1p


def fwd_in(x, n1g, sc1, sh1, wT, wr, qg_col, kg_col, bf_col, tri):
    S = x.shape[0]
    B = BLK
    NB = S // B

    def body(x_ref, g_ref, sc_ref, sh_ref, wT_ref, wr_ref, qg_ref, kg_ref, bf_ref, tri_ref,
             qTa_ref, kT_ref, kaug_ref, vT_ref, vaug_ref, zqk_ref, fgT_ref, alg_ref, u0_ref, stat_ref,
             carry, tbuf, fs, nq, nk):
        i = pl.program_id(0)

        @pl.when(i == 0)
        def _():
            carry[...] = jnp.zeros_like(carry)

        _, _, h = _norm_mod(x_ref[...], g_ref[...], sc_ref[...], sh_ref[...])
        hb = h.astype(BF16)
        zT = _nt(wT_ref[...], hb)
        zr = _nn(hb, wr_ref[...])
        zqk_ref[...] = zT[0:2 * A]
        fgT = zT[3 * A:3 * A + 8]
        fgT_ref[...] = fgT
        alg_ref[...] = zr
        u0_ref[...] = zr[:, 0:CW] * _sigmoid(zr[:, CW:2 * CW])

        logf = _log_sigmoid(fgT + bf_ref[...])
        a1, a2, a3 = _split3(logf)
        tr = tri_ref[...]
        F = _nn(a1, tr) + _nn(a2, tr) + _nn(a3, tr) + carry[...]
        carry[...] = carry[...] + _lanesum(logf)
        p1, p2, p3 = _split3(F * LOG2E)
        n1, n2, n3 = _split3(F * (-LOG2E))
        for k, v in enumerate((p1, p2, p3, n1, n2, n3)):
            fs[k] = v.astype(F32)

        rowi = lax.broadcasted_iota(jnp.int32, (8, B), 0)
        zeros_tail = jnp.zeros((128 - DH - 8, B), F32)
        ones_row = jnp.where(lax.broadcasted_iota(jnp.int32, (KR - DH, B), 0) == 0, 1.0, 0.0).astype(BF16)
        for hh in range(H):
            sl = slice(hh * DH, (hh + 1) * DH)
            q = zT[sl]
            k = zT[A + hh * DH:A + (hh + 1) * DH]
            v = zT[2 * A + hh * DH:2 * A + (hh + 1) * DH]
            qh = q * lax.rsqrt(jnp.mean(q * q, axis=0, keepdims=True) + EPS) * qg_ref[...] * (0.125 * LOG2E)
            kh = k * lax.rsqrt(jnp.mean(k * k, axis=0, keepdims=True) + EPS) * kg_ref[...]
            frow = [fs[kk, hh:hh + 1, :] for kk in range(6)]
            qx = jnp.where(rowi < 3, 1.0, jnp.where(rowi == 3, frow[0], jnp.where(
                rowi == 4, frow[1], jnp.where(rowi == 5, frow[2], 0.0))))
            kx = jnp.where(rowi == 0, frow[3], jnp.where(rowi == 1, frow[4], jnp.where(
                rowi == 2, frow[5], jnp.where(rowi < 6, 1.0, 0.0))))
            tbuf[0:DH, :] = qh
            tbuf[DH:DH + 8, :] = qx
            tbuf[DH + 8:128, :] = zeros_tail
            qTa_ref[0, hh * 128:(hh + 1) * 128, :] = tbuf[...].astype(BF16)
            tbuf[0:DH, :] = kh
            tbuf[DH:DH + 8, :] = kx
            kaug_ref[hh] = tbuf[...].T.astype(BF16)
            kT_ref[0, hh * KR:hh * KR + DH, :] = kh.astype(BF16)
            kT_ref[0, hh * KR + DH:(hh + 1) * KR, :] = ones_row
            tbuf[0:DH, :] = v
            tbuf[DH:DH + 8, :] = jnp.zeros((8, B), F32)
            vaug_ref[hh] = tbuf[...].T.astype(BF16)
            vT_ref[0, sl, :] = v.astype(BF16)
            nq[hh:hh + 1, :] = jnp.max(_colsum(qh * qh), axis=1, keepdims=True)
            nk[hh:hh + 1, :] = jnp.max(_colsum(kh * kh), axis=1, keepdims=True)

        lane = lax.broadcasted_iota(jnp.int32, (8, 128), 1)
        stat_ref[0] = jnp.where(lane == 0, jnp.max(F, axis=1, keepdims=True), jnp.where(
            lane == 1, jnp.min(F, axis=1, keepdims=True), jnp.where(
                lane == 2, nq[...], jnp.where(lane == 3, nk[...], 0.0))))

    row = lambda w: pl.BlockSpec((B, w), lambda i: (i, 0))
    tsp = lambda r: pl.BlockSpec((1, r, B), lambda i: (i, 0, 0))
    return _pc(
        body, name="fwd_in", grid=(NB,),
        in_specs=[row(D), _const((1, D)), _const((1, D)), _const((1, D)), _const((ZT_ROWS, D)),
                  _const((D, 2 * CW)), _const((DH, 1)), _const((DH, 1)), _const((8, 1)), _const((B, B))],
        out_specs=[tsp(H * 128), tsp(H * KR), pl.BlockSpec((H, B, 128), lambda i: (0, i, 0)), tsp(A),
                   pl.BlockSpec((H, B, 128), lambda i: (0, i, 0)),
                   pl.BlockSpec((2 * A, B), lambda i: (0, i)), pl.BlockSpec((8, B), lambda i: (0, i)),
                   row(2 * CW), row(CW), pl.BlockSpec((1, 8, 128), lambda i: (i, 0, 0))],
        out_shape=[_sds((NB, H * 128, B), BF16), _sds((NB, H * KR, B), BF16), _sds((H, S, 128), BF16),
                   _sds((NB, A, B), BF16), _sds((H, S, 128), BF16), _sds((2 * A, S)), _sds((8, S)),
                   _sds((S, 2 * CW)), _sds((S, CW)), _sds((NB, 8, 128))],
        scratch_shapes=[pltpu.VMEM((8, 1), F32), pltpu.VMEM((128, B), F32), pltpu.VMEM((6, 8, B), F32),
                        pltpu.VMEM((8, 1), F32), pltpu.VMEM((8, 1), F32)],
        compiler_params=_cp(("arbitrary",)),
    )(x, n1g, sc1, sh1, wT, wr, qg_col, kg_col, bf_col, tri)


def _first_key_block(top, fmin_ref, cut, h, i):
    return lax.while_loop(
        lambda j: jnp.logical_and(j > 0, top - fmin_ref[h, jnp.maximum(j - 1, 0)] >= cut),
        lambda j: j - 1, i)


def _causal_keep(B):
    return lax.broadcasted_iota(jnp.int32, (B, B), 0) <= lax.broadcasted_iota(jnp.int32, (B, B), 1)


def _keep_right_half(B):
    hb = B // 2
    return lax.broadcasted_iota(jnp.int32, (B, hb), 0) <= lax.broadcasted_iota(jnp.int32, (B, hb), 1) + hb


def _keep_top_half(B):
    hb = B // 2
    return lax.broadcasted_iota(jnp.int32, (hb, B), 0) <= lax.broadcasted_iota(jnp.int32, (hb, B), 1)


def attn_fwd(fmax, fmin, thr, qTa, kaug, vT, late_w):
    NB, _, B = qTa.shape
    S = NB * B
    NG = len(late_w)
    HB = B // 2

    def body(fmax_ref, fmin_ref, thr_ref, q_ref, k_ref, v_ref, *rest):
        w_refs = rest[0:NG]
        o_ref, lse_ref = rest[NG:NG + 2]
        gw_refs = rest[NG + 2:2 * NG + 2]
        s0, s1, m_ref, l_ref, acc_ref = rest[2 * NG + 2:2 * NG + 7]
        sems = rest[2 * NG + 7:]
        h = pl.program_id(0)
        i = pl.program_id(1)

        def gathers():
            return [_direct_copies(False, w_refs[q], gw_refs[q], *sems[3 * q:3 * q + 3]) for q in range(NG)]

        @pl.when(jnp.logical_and(h == 0, i == 0))
        def _():
            for copies in gathers():
                _start_all(copies)

        jlo = _first_key_block(fmax_ref[h, i], fmin_ref, thr_ref[h], h, i)
        n = i - jlo

        def scores(j, s_ref):
            s_ref[...] = _nn(k_ref[0, pl.ds(pl.multiple_of(j * B, B), B), :], q_ref[0])

        def softmax_cols(s, j, cols, rows):
            m = m_ref[:, cols]
            mn = jnp.maximum(m, jnp.max(s, axis=0, keepdims=True))
            a = jnp.exp2(m - mn)
            p = jnp.exp2(s - mn)
            m_ref[:, cols] = mn
            l_ref[:, cols] = a * l_ref[:, cols] + _colsum(p)
            acc_ref[:, cols] = a * acc_ref[:, cols] + _nn(v_ref[j][:, rows], p.astype(BF16))

        def softmax_step(s_ref, j, masked):
            if not masked:
                softmax_cols(s_ref[...], j, slice(None), slice(None))
                return
            lo, hi = slice(0, HB), slice(HB, B)
            softmax_cols(jnp.where(_causal_keep(HB), s_ref[lo, lo], -jnp.inf), j, lo, lo)
            softmax_cols(jnp.where(_keep_right_half(B), s_ref[:, hi], -jnp.inf), j, hi, slice(None))

        m_ref[...] = jnp.full((1, B), -jnp.inf, F32)
        l_ref[...] = jnp.zeros((1, B), F32)
        acc_ref[...] = jnp.zeros((DH, B), F32)
        scores(jlo, s0)

        def pair(t, carry):
            j = jlo + 2 * t
            scores(j + 1, s1)
            softmax_step(s0, j, False)
            scores(j + 2, s0)
            softmax_step(s1, j + 1, False)
            return carry

        lax.fori_loop(0, n // 2, pair, 0)

        @pl.when(n % 2 == 1)
        def _():
            scores(i, s1)
            softmax_step(s0, i - 1, False)
            softmax_step(s1, i, True)

        @pl.when(n % 2 == 0)
        def _():
            softmax_step(s0, i, True)

        l = l_ref[...]
        o_ref[0] = acc_ref[...] / l
        lse_ref[0, 0] = m_ref[...] + jnp.log2(l)

        @pl.when(jnp.logical_and(h == H - 1, i == NB - 1))
        def _():
            for copies in gathers():
                _wait_all(copies)

    return _pc(
        body, name="attn_fwd", grid=(H, NB),
        in_specs=[SMEM, SMEM, SMEM, pl.BlockSpec((1, 128, B), lambda h, i: (i, h, 0)),
                  pl.BlockSpec((1, S, 128), lambda h, i: (h, 0, 0)),
                  pl.BlockSpec((NB, DH, B), lambda h, i: (0, h, 0))] + [ANY] * NG,
        out_specs=[pl.BlockSpec((1, DH, B), lambda h, i: (i, h, 0)),
                   pl.BlockSpec((1, 1, 1, B), lambda h, i: (h, i, 0, 0))] + [ANY] * NG,
        out_shape=[_sds((NB, A, B)), _sds((H, NB, 1, B))]
        + [_sds((N_DEV,) + w.shape, w.dtype) for w in late_w],
        scratch_shapes=[pltpu.VMEM((B, B), F32), pltpu.VMEM((B, B), F32), pltpu.VMEM((1, B), F32),
                        pltpu.VMEM((1, B), F32), pltpu.VMEM((DH, B), F32)] + COMM_SEMS * NG,
        compiler_params=_cp(("arbitrary", "arbitrary")),
    )(fmax, fmin, thr, qTa, kaug, vT, *late_w)


def attn_bwd(fmax, fmin, thr, qTa, kaug, kT, vaug, doTa, lse, delta, early_g):
    NG = len(early_g)
    NB, _, B = qTa.shape
    QR = 80
    CH = min(256, B)

    def body(fmax_ref, fmin_ref, thr_ref, q_ref, ka_ref, kt_ref, va_ref, do_ref, lse_ref, dl_ref, *rest):
        g_refs = rest[0:NG]
        dq_ref, dk_ref, dv_ref, dfk_ref = rest[NG:NG + 4]
        recv_refs = rest[NG + 4:2 * NG + 4]
        s0, d0, s1, d1, p0, e0, p1, e1, dk_acc, dv_acc = rest[2 * NG + 4:2 * NG + 14]
        sems = rest[2 * NG + 14:]
        h = pl.program_id(0)
        j = pl.program_id(1)

        def exchanges():
            return [_direct_copies(True, g_refs[q], recv_refs[q], *sems[3 * q:3 * q + 3]) for q in range(NG)]

        @pl.when(jnp.logical_and(h == 0, j == 0))
        def _():
            for copies in exchanges():
                _start_all(copies)

        bottom = fmin_ref[h, j]
        cut = thr_ref[h]
        ihi = lax.while_loop(
            lambda i: jnp.logical_and(i < NB - 1, fmax_ref[h, jnp.minimum(i + 1, NB - 1)] - bottom >= cut),
            lambda i: i + 1, j)

        @pl.when(j == 0)
        def _():
            dq_ref[...] = jnp.zeros_like(dq_ref)

        dk_acc[...] = jnp.zeros_like(dk_acc)
        dv_acc[...] = jnp.zeros_like(dv_acc)

        n = ihi - j + 1

        def blk(k):
            return jnp.minimum(j + k, ihi)

        def products(k, s_ref, d_ref):
            i = blk(k)
            s_ref[...] = _nn(ka_ref[0], q_ref[i])
            d_ref[...] = _nn(va_ref[0], do_ref[i])

        def elementwise(k, s_ref, d_ref, p_ref, ds_ref, masked):
            i = blk(k)
            p = jnp.exp2(s_ref[...] - lse_ref[0, i])
            if masked:
                p = jnp.where(_causal_keep(B), p, 0.0)
            p_ref[...] = p.astype(BF16)
            ds_ref[...] = (p * (d_ref[...] - dl_ref[0, i])).astype(BF16)

        def grads(k, p_ref, ds_ref):
            i = j + k
            dsb = ds_ref[...]
            dv_acc[...] = dv_acc[...] + _nt(do_ref[i][0:DH], p_ref[...])
            dk_acc[...] = dk_acc[...] + _nt(q_ref[i][0:QR], dsb)
            dq_ref[i] = dq_ref[i] + _nn(kt_ref[0], dsb)

        def trip(kp, s_p, d_p, ke, s_e, d_e, p_e, e_e, kg, p_g, e_g):
            ip, ie, ig = blk(kp), blk(ke), j + kg
            lse_e = lse_ref[0, ie]
            dl_e = dl_ref[0, ie]
            dq_part = jnp.zeros((KR, B), F32)
            for c in range(B // CH):
                rows = slice(c * CH, (c + 1) * CH)
                s_p[rows, :] = _nn(ka_ref[0, rows, :], q_ref[ip])
                d_p[rows, :] = _nn(va_ref[0, rows, :], do_ref[ip])
                p = jnp.exp2(s_e[rows, :] - lse_e)
                p_e[rows, :] = p.astype(BF16)
                e_e[rows, :] = (p * (d_e[rows, :] - dl_e)).astype(BF16)
                dsb = e_g[rows, :]
                dv_acc[:, rows] = dv_acc[:, rows] + _nt(do_ref[ig][0:DH], p_g[rows, :])
                dk_acc[:, rows] = dk_acc[:, rows] + _nt(q_ref[ig][0:QR], dsb)
                dq_part = dq_part + _nn(kt_ref[0, :, rows], dsb)
            dq_ref[ig] = dq_ref[ig] + dq_part

        HB = B // 2
        lo, hi = slice(0, HB), slice(HB, B)

        def diagonal_products(s_ref, d_ref):
            s_ref[lo, :] = _nn(ka_ref[0, lo, :], q_ref[j])
            d_ref[lo, :] = _nn(va_ref[0, lo, :], do_ref[j])
            s_ref[hi, hi] = _nn(ka_ref[0, hi, :], q_ref[j, :, hi])
            d_ref[hi, hi] = _nn(va_ref[0, hi, :], do_ref[j, :, hi])

        def diagonal_elementwise(s_ref, d_ref, p_ref, ds_ref):
            lse0 = lse_ref[0, j]
            dl0 = dl_ref[0, j]
            p = jnp.where(_keep_top_half(B), jnp.exp2(s_ref[lo, :] - lse0), 0.0)
            p_ref[lo, :] = p.astype(BF16)
            ds_ref[lo, :] = (p * (d_ref[lo, :] - dl0)).astype(BF16)
            p = jnp.where(_causal_keep(HB), jnp.exp2(s_ref[hi, hi] - lse0[:, hi]), 0.0)
            p_ref[hi, hi] = p.astype(BF16)
            ds_ref[hi, hi] = (p * (d_ref[hi, hi] - dl0[:, hi])).astype(BF16)
            p_ref[hi, lo] = jnp.zeros((HB, HB), BF16)
            ds_ref[hi, lo] = jnp.zeros((HB, HB), BF16)

        diagonal_products(s0, d0)
        products(1, s1, d1)
        diagonal_elementwise(s0, d0, p0, e0)

        def pair(t, carry):
            k = 2 * t
            trip(k + 2, s0, d0, k + 1, s1, d1, p1, e1, k, p0, e0)
            trip(k + 3, s1, d1, k + 2, s0, d0, p0, e0, k + 1, p1, e1)
            return carry

        n_pairs = n // 2
        lax.fori_loop(0, n_pairs - 1, pair, 0)
        k_last = 2 * (n_pairs - 1)

        @pl.when(jnp.logical_and(n_pairs >= 1, n % 2 == 0))
        def _():
            elementwise(k_last + 1, s1, d1, p1, e1, False)
            grads(k_last, p0, e0)
            grads(k_last + 1, p1, e1)

        @pl.when(jnp.logical_and(n_pairs >= 1, n % 2 == 1))
        def _():
            trip(k_last + 2, s0, d0, k_last + 1, s1, d1, p1, e1, k_last, p0, e0)
            elementwise(k_last + 2, s0, d0, p0, e0, False)
            grads(k_last + 1, p1, e1)
            grads(k_last + 2, p0, e0)

        @pl.when(n == 1)
        def _():
            grads(0, p0, e0)

        dk_ref[0] = dk_acc[0:DH, :]
        dfk_ref[0, 0] = dk_acc[DH:DH + 8, :]
        dv_ref[0] = dv_acc[...]

        @pl.when(jnp.logical_and(h == H - 1, j == NB - 1))
        def _():
            for copies in exchanges():
                _wait_all(copies)

    per_kv = lambda r: pl.BlockSpec((1, r, B), lambda h, j: (j, h, 0))
    head_all = lambda r: pl.BlockSpec((NB, r, B), lambda h, j: (0, h, 0))
    aug = pl.BlockSpec((1, B, 128), lambda h, j: (h, j, 0))
    stat = pl.BlockSpec((1, NB, 1, B), lambda h, j: (h, 0, 0, 0))
    return _pc(
        body, name="attn_bwd", grid=(H, NB),
        in_specs=[SMEM, SMEM, SMEM, head_all(128), aug, per_kv(KR), aug, head_all(128), stat, stat]
        + [ANY] * NG,
        out_specs=[head_all(KR), per_kv(DH), per_kv(DH),
                   pl.BlockSpec((1, 1, 8, B), lambda h, j: (h, j, 0, 0))] + [ANY] * NG,
        out_shape=[_sds((NB, H * KR, B)), _sds((NB, A, B)), _sds((NB, A, B)), _sds((H, NB, 8, B))]
        + [_sds(g.shape, g.dtype) for g in early_g],
        scratch_shapes=[pltpu.VMEM((B, B), F32)] * 4 + [pltpu.VMEM((B, B), BF16)] * 4
        + [pltpu.VMEM((QR, B), F32), pltpu.VMEM((DH, B), F32)] + COMM_SEMS * NG,
        compiler_params=_cp(("arbitrary", "arbitrary")),
    )(fmax, fmin, thr, qTa, kaug, kT, vaug, doTa, lse, delta, *early_g)


def _conv_tail(u1, lng, lnb, beta_c):
    mu = _rowmean(u1)
    d = u1 - mu
    rstd = lax.rsqrt(_rowmean(d * d) + EPS)
    xhat = d * rstd
    u2 = xhat * lng + lnb
    sg = _sigmoid(u2)
    u3 = u2 * sg
    rc = lax.rsqrt(_rowmean(u3 * u3) + EPS)
    n3 = u3 * rc
    return rstd, xhat, u2, sg, rc, n3, n3 * beta_c


def _attn_tail(oT, beta_a_col):
    ra = lax.rsqrt(jnp.mean(oT * oT, axis=0, keepdims=True) + EPS)
    ohat = oT * ra
    return ra, ohat, ohat * beta_a_col


CONV_ROWS = 64
CORR_ROWS = 64


def _fill_shifted(buf, sh):
    rows = sh.shape[1]
    for ph in range(8):
        sh[ph] = buf[pl.ds(ph, rows), :]


def _tap_rows(sh, r0, o, rows):
    return sh[o % 8, pl.ds(pl.multiple_of(r0 + 8 * (o // 8), 8), rows), :]


def _depthwise(sh, w_ref, offs, bias, out_ref, B):
    def chunk(ci, carry):
        r0 = pl.multiple_of(ci * CONV_ROWS, CONV_ROWS)
        acc = jnp.broadcast_to(bias, (CONV_ROWS, CW))
        for k, o in enumerate(offs):
            acc = acc + w_ref[k:k + 1, :] * _tap_rows(sh, r0, o, CONV_ROWS)
        out_ref[pl.ds(r0, CONV_ROWS), :] = acc
        return carry

    lax.fori_loop(0, B // CONV_ROWS, chunk, 0)


def _tap_gradients(sh, d_ref, offs, dw_ref, B):
    for g0 in range(0, len(offs), 8):
        ks = list(range(g0, min(g0 + 8, len(offs))))

        def chunk(ci, accs, ks=ks):
            r0 = pl.multiple_of(ci * CORR_ROWS, CORR_ROWS)
            d = d_ref[pl.ds(r0, CORR_ROWS), :]
            out = []
            for a, k in zip(accs, ks):
                pr = d * _tap_rows(sh, r0, offs[k], CORR_ROWS)
                for r in range(0, CORR_ROWS, 8):
                    a = a + pr[r:r + 8]
                out.append(a)
            return tuple(out)

        accs = lax.fori_loop(0, B // CORR_ROWS, chunk, tuple(jnp.zeros((8, CW), F32) for _ in ks))
        for a, k in zip(accs, ks):
            dw_ref[k:k + 1, :] = dw_ref[k:k + 1, :] + _colsum(a)


def conv_merge_out(u0, x, oT, conv_w, conv_b, lng, lnb, beta_c, beta_a_col, wo, g1):
    S = x.shape[0]
    B = BLK
    NB = S // B

    def body(uc_ref, up_ref, x_ref, oT_ref, w_ref, cb_ref, lng_ref, lnb_ref, bc_ref, ba_ref, wo_ref,
             g1_ref, x1_ref, o_ref, u1_ref, ubuf, sh):
        i = pl.program_id(0)
        ubuf[0:HALO, :] = jnp.where(i > 0, up_ref[B - HALO:B, :], 0.0)
        ubuf[HALO:HALO + B, :] = uc_ref[...]
        ubuf[HALO + B:HALO + B + 8, :] = jnp.zeros((8, CW), F32)
        _fill_shifted(ubuf, sh)
        _depthwise(sh, w_ref, [HALO - (KC - 1) + k for k in range(KC)], cb_ref[...], u1_ref, B)
        mc = _conv_tail(u1_ref[...], lng_ref[...], lnb_ref[...], bc_ref[...])[-1]
        maT = _attn_tail(oT_ref[0], ba_ref[...])[-1]
        ma = maT.T
        o = _nn(ma.astype(BF16), wo_ref[0:A, :]) + _nn(mc.astype(BF16), wo_ref[A:D, :])
        o_ref[...] = o
        x1_ref[...] = x_ref[...] + g1_ref[...] * o

    row = lambda w: pl.BlockSpec((B, w), lambda i: (i, 0))
    return _pc(
        body, name="conv_merge_out", grid=(NB,),
        in_specs=[row(CW), pl.BlockSpec((B, CW), lambda i: (jnp.maximum(i - 1, 0), 0)), row(D),
                  pl.BlockSpec((1, A, B), lambda i: (i, 0, 0)), _const((32, CW)), _const((1, CW)),
                  _const((1, CW)), _const((1, CW)), _const((1, CW)), _const((A, 1)), _const((D, D)),
                  _const((1, D))],
        out_specs=[row(D), row(D), row(CW)],
        out_shape=[_sds((S, D)), _sds((S, D)), _sds((S, CW))],
        scratch_shapes=[pltpu.VMEM((B + HALO + 8, CW), F32), pltpu.VMEM((8, B + HALO, CW), F32)],
        compiler_params=_cp(("arbitrary",)),
    )(u0, u0, x, oT, conv_w, conv_b, lng, lnb, beta_c, beta_a_col, wo, g1)


def mlp_fwd_loss(x1, tgt, n2g, sc2, sh2, g2, w1s, w2):
    S = x1.shape[0]
    B = BLK
    NB = S // B
    NS, _, FS = w1s.shape

    def body(x1_ref, t_ref, g_ref, sc_ref, sh_ref, g2_ref, w1_hbm, w2_hbm, dy_ref, loss_ref, dg2_ref, ra_ref,
             ht_ref, w1_v, w2_v, sem):
        i = pl.program_id(0)

        @pl.when(i == 0)
        def _():
            for src, dst in ((w1_hbm, w1_v), (w2_hbm, w2_v)):
                cp = pltpu.make_async_copy(src, dst, sem)
                cp.start()
                cp.wait()
            loss_ref[...] = jnp.zeros_like(loss_ref)
            dg2_ref[...] = jnp.zeros_like(dg2_ref)

        x1 = x1_ref[...]
        h2 = _norm_mod(x1, g_ref[...], sc_ref[...], sh_ref[...])[2]
        hs = h2.astype(BF16)
        ht_ref[...] = h2.T.astype(BF16)
        y2 = jnp.zeros((B, D), F32)
        for q in range(NS):
            a = jnp.maximum(_nn(hs, w1_v[q]), 0.0)
            ra_ref[:, q * FS:(q + 1) * FS] = a.astype(BF16)
            y2 = y2 + _nn((a * a).astype(BF16), w2_v[q * FS:(q + 1) * FS, :])
        e = x1 + g2_ref[...] * y2 - t_ref[...]
        dy = e * (1.0 / D)
        dy_ref[...] = dy
        loss_ref[...] = loss_ref[...] + 0.5 * _colsum(_rowmean(e * e))
        dg2_ref[...] = dg2_ref[...] + _colsum(dy * y2)

    row = pl.BlockSpec((B, D), lambda i: (i, 0))
    return _pc(
        body, name="mlp_fwd_loss", grid=(NB,),
        in_specs=[row, row, _const((1, D)), _const((1, D)), _const((1, D)), _const((1, D)), ANY, ANY],
        out_specs=[row, _const((1, 1)), _const((1, D)), pl.BlockSpec((B, FF), lambda i: (i, 0)),
                   pl.BlockSpec((D, B), lambda i: (0, i))],
        out_shape=[_sds((S, D)), _sds((1, 1)), _sds((1, D)), _sds((S, FF), BF16), _sds((D, S), BF16)],
        scratch_shapes=[pltpu.VMEM(w1s.shape, BF16), pltpu.VMEM(w2.shape, BF16), pltpu.SemaphoreType.DMA(())],
        compiler_params=_cp(("arbitrary",)),
    )(x1, tgt, n2g, sc2, sh2, g2, w1s, w2)


def mlp_bwd(h2t, dy, ra, g2, w1s, w2):
    S = dy.shape[0]
    B = BLK
    NB = S // B
    NC = FF // FCH
    NS, _, FS = w1s.shape
    SPC = NS // NC

    def body(ht_ref, dy_ref, ra_ref, g2_ref, w1_ref, w2_ref, dw1_ref, dw2_ref, dh_ref):
        i = pl.program_id(1)

        @pl.when(i == 0)
        def _():
            dw1_ref[...] = jnp.zeros_like(dw1_ref)
            dw2_ref[...] = jnp.zeros_like(dw2_ref)

        h2t = ht_ref[...]
        ra = ra_ref[...].astype(F32)
        dyb = (dy_ref[...] * g2_ref[...]).astype(BF16)
        db = _nt(dyb, w2_ref[...])
        dab = (db * (2.0 * ra)).astype(BF16)
        dh = jnp.zeros((B, D), F32)
        for q in range(SPC):
            part = dab[:, q * FS:(q + 1) * FS]
            dh = dh + _nt(part, w1_ref[q])
            dw1_ref[q] = dw1_ref[q] + _nn(h2t, part)
        dh_ref[0] = dh
        dw2_ref[...] = dw2_ref[...] + _nn((ra * ra).T.astype(BF16), dyb)

    row = pl.BlockSpec((B, D), lambda c, i: (i, 0))
    vec = pl.BlockSpec((1, D), lambda c, i: (0, 0))
    wsh = pl.BlockSpec((SPC, D, FS), lambda c, i: (c, 0, 0))
    w2b = pl.BlockSpec((FCH, D), lambda c, i: (c, 0))
    return _pc(
        body, name="mlp_bwd", grid=(NC, NB),
        in_specs=[pl.BlockSpec((D, B), lambda c, i: (0, i)), row, pl.BlockSpec((B, FCH), lambda c, i: (i, c)),
                  vec, wsh, w2b],
        out_specs=[wsh, w2b, pl.BlockSpec((1, B, D), lambda c, i: (c, i, 0))],
        out_shape=[_sds(w1s.shape), _sds((FF, D)), _sds((NC, S, D))],
        compiler_params=_cp(("arbitrary", "arbitrary")),
    )(h2t, dy, ra, g2, w1s, w2)


def merge_bwd(dh2p, x1, dy, o, oT, u1, n2g, sc2, g1, lng, lnb, beta_c, beta_a_col, wo):
    S = x1.shape[0]
    B = BLK
    NB = S // B
    NC = dh2p.shape[0]

    def body(dh_ref, x1_ref, dy_ref, o_ref, oT_ref, u1_ref, g_ref, sc_ref, g1_ref, lng_ref, lnb_ref,
             bc_ref, ba_ref, wo_ref,
             dx1_ref, doTa_ref, dl_ref, du1_ref, dwo_ref, v1_ref, v2_ref, dba_ref):
        i = pl.program_id(0)

        @pl.when(i == 0)
        def _():
            dwo_ref[...] = jnp.zeros_like(dwo_ref)
            v1_ref[...] = jnp.zeros_like(v1_ref)
            v2_ref[...] = jnp.zeros_like(v2_ref)
            dba_ref[...] = jnp.zeros_like(dba_ref)

        dh2 = dh_ref[0]
        for cc in range(1, NC):
            dh2 = dh2 + dh_ref[cc]
        x1 = x1_ref[...]
        g = g_ref[...]
        sc = sc_ref[...]
        r2 = lax.rsqrt(_rowmean(x1 * x1) + EPS)
        xh = x1 * r2
        dhx = dh2 * xh
        v1_ref[0:1, :] = v1_ref[0:1, :] + _colsum(dh2)
        v1_ref[1:2, :] = v1_ref[1:2, :] + _colsum(dhx) * g
        v1_ref[2:3, :] = v1_ref[2:3, :] + _colsum(dhx) * (1.0 + sc)
        dxh = dh2 * (g * (1.0 + sc))
        dx1 = dy_ref[...] + r2 * (dxh - xh * _rowmean(dxh * xh))
        dx1_ref[...] = dx1
        v1_ref[3:4, :] = v1_ref[3:4, :] + _colsum(dx1 * o_ref[...])
        dob = (dx1 * g1_ref[...]).astype(BF16)

        lng = lng_ref[...]
        bc = bc_ref[...]
        rstd, xhat, u2, sg, rc, n3, mc = _conv_tail(u1_ref[...], lng, lnb_ref[...], bc)
        ba = ba_ref[...]
        oT = oT_ref[0]
        ra, ohat, maT = _attn_tail(oT, ba)
        dwo_ref[0:A, :] = dwo_ref[0:A, :] + _nn(maT.astype(BF16), dob)
        dwo_ref[A:D, :] = dwo_ref[A:D, :] + _nn(mc.T.astype(BF16), dob)
        dmaT = _nt(wo_ref[0:A, :], dob)
        dmc = _nt(dob, wo_ref[A:D, :])

        dba_ref[...] = dba_ref[...] + _lanesum(dmaT * ohat)
        dohat = dmaT * ba
        doT = ra * (dohat - ohat * jnp.mean(dohat * ohat, axis=0, keepdims=True))
        prod = doT * oT
        zpad = jnp.zeros((128 - DH, B), BF16)
        for hh in range(H):
            sl = slice(hh * DH, (hh + 1) * DH)
            dl_ref[hh, 0] = _colsum(prod[sl])
            doTa_ref[0, hh * 128:hh * 128 + DH, :] = doT[sl].astype(BF16)
            doTa_ref[0, hh * 128 + DH:(hh + 1) * 128, :] = zpad

        v2_ref[0:1, :] = v2_ref[0:1, :] + _colsum(dmc * n3)
        dn3 = dmc * bc
        du3 = rc * (dn3 - n3 * _rowmean(dn3 * n3))
        du2 = du3 * (sg * (1.0 + u2 * (1.0 - sg)))
        v2_ref[1:2, :] = v2_ref[1:2, :] + _colsum(du2 * xhat)
        v2_ref[2:3, :] = v2_ref[2:3, :] + _colsum(du2)
        dxhat = du2 * lng
        du1_ref[...] = rstd * (dxhat - _rowmean(dxhat) - xhat * _rowmean(dxhat * xhat))

    row = lambda w: pl.BlockSpec((B, w), lambda i: (i, 0))
    return _pc(
        body, name="merge_bwd", grid=(NB,),
        in_specs=[pl.BlockSpec((NC, B, D), lambda i: (0, i, 0)), row(D), row(D), row(D),
                  pl.BlockSpec((1, A, B), lambda i: (i, 0, 0)), row(CW), _const((1, D)), _const((1, D)),
                  _const((1, D)), _const((1, CW)), _const((1, CW)), _const((1, CW)), _const((A, 1)),
                  _const((D, D))],
        out_specs=[row(D), pl.BlockSpec((1, H * 128, B), lambda i: (i, 0, 0)),
                   pl.BlockSpec((H, 1, 1, B), lambda i: (0, i, 0, 0)), row(CW), _const((D, D)),
                   _const((8, D)), _const((8, CW)), _const((A, 1))],
        out_shape=[_sds((S, D)), _sds((NB, H * 128, B), BF16), _sds((H, NB, 1, B)), _sds((S, CW)),
                   _sds((D, D)), _sds((8, D)), _sds((8, CW)), _sds((A, 1))],
        compiler_params=_cp(("arbitrary",)),
    )(dh2p, x1, dy, o, oT, u1, n2g, sc2, g1, lng, lnb, beta_c, beta_a_col, wo)


def head_bwd(du1, u0, alg, zqk, fgT, dqT, dkT, dvT, dfk, conv_w, qg_col, kg_col, bf_col, tri_lo):
    S = u0.shape[0]
    B = BLK
    NB = S // B

    def body(dc_ref, dn_ref, uc_ref, up_ref, alg_ref, zqk_ref, fg_ref, dq_ref, dk_ref, dv_ref, dfk_ref,
             w_ref, qg_ref, kg_ref, bf_ref, tri_ref,
             dzr_ref, dzT_ref, dcw_ref, vc_ref, dqg_ref, dkg_ref, dbf_ref,
             buf, sh, du0_ref, carry, fbuf):
        pid = pl.program_id(0)
        ri = NB - 1 - pid

        @pl.when(pid == 0)
        def _():
            carry[...] = jnp.zeros_like(carry)
            dcw_ref[...] = jnp.zeros_like(dcw_ref)
            vc_ref[...] = jnp.zeros_like(vc_ref)
            dqg_ref[...] = jnp.zeros_like(dqg_ref)
            dkg_ref[...] = jnp.zeros_like(dkg_ref)
            dbf_ref[...] = jnp.zeros_like(dbf_ref)

        zero8 = jnp.zeros((8, CW), F32)
        buf[0:B, :] = dc_ref[...]
        buf[B:B + HALO, :] = jnp.where(ri < NB - 1, dn_ref[0:HALO, :], 0.0)
        buf[B + HALO:B + HALO + 8, :] = zero8
        _fill_shifted(buf, sh)
        _depthwise(sh, w_ref, [KC - 1 - k for k in range(KC)], jnp.zeros((1, CW), F32), du0_ref, B)
        buf[0:HALO, :] = jnp.where(ri > 0, up_ref[B - HALO:B, :], 0.0)
        buf[HALO:HALO + B, :] = uc_ref[...]
        buf[HALO + B:HALO + B + 8, :] = zero8
        _fill_shifted(buf, sh)
        _tap_gradients(sh, dc_ref, [HALO - (KC - 1) + k for k in range(KC)], dcw_ref, B)
        vc_ref[0:1, :] = vc_ref[0:1, :] + _colsum(dc_ref[...])

        du0 = du0_ref[...]
        al = alg_ref[:, 0:CW]
        sg = _sigmoid(alg_ref[:, CW:2 * CW])
        dzr_ref[:, 0:CW] = (du0 * sg).astype(BF16)
        dzr_ref[:, CW:2 * CW] = (du0 * al * sg * (1.0 - sg)).astype(BF16)

        dqg = jnp.zeros((DH, B), F32)
        dkg = jnp.zeros((DH, B), F32)
        qg = qg_ref[...]
        kg = kg_ref[...]
        for hh in range(H):
            sl = slice(hh * DH, (hh + 1) * DH)
            q = zqk_ref[sl, :]
            rq = lax.rsqrt(jnp.mean(q * q, axis=0, keepdims=True) + EPS)
            qn = q * rq
            dqh = dq_ref[0, hh * KR:hh * KR + DH, :] * 0.125
            dqg = dqg + dqh * qn
            dqn = dqh * qg
            dzT_ref[0, sl, :] = (rq * (dqn - qn * jnp.mean(dqn * qn, axis=0, keepdims=True))).astype(BF16)
            k = zqk_ref[A + hh * DH:A + (hh + 1) * DH, :]
            rk = lax.rsqrt(jnp.mean(k * k, axis=0, keepdims=True) + EPS)
            kn = k * rk
            dkh = dk_ref[0, sl, :] * (1.0 / LOG2E)
            dkg = dkg + dkh * kn
            dkn = dkh * kg
            dzT_ref[0, A + hh * DH:A + (hh + 1) * DH, :] = (
                rk * (dkn - kn * jnp.mean(dkn * kn, axis=0, keepdims=True))).astype(BF16)
            fbuf[hh:hh + 1, :] = dq_ref[0, hh * KR + DH:hh * KR + DH + 1, :] - dfk_ref[hh, 0, 0:1, :]
        dqg_ref[...] = dqg_ref[...] + _lanesum(dqg)
        dkg_ref[...] = dkg_ref[...] + _lanesum(dkg)
        dzT_ref[0, 2 * A:3 * A, :] = dv_ref[0].astype(BF16)

        dF = fbuf[...]
        a1, a2, a3 = _split3(dF)
        tr = tri_ref[...]
        dlogf = _nn(a1, tr) + _nn(a2, tr) + _nn(a3, tr) + carry[...]
        carry[...] = carry[...] + _lanesum(dF)
        dfg = dlogf * _sigmoid(-(fg_ref[...] + bf_ref[...]))
        dbf_ref[...] = dbf_ref[...] + _lanesum(dfg)
        dzT_ref[0, 3 * A:ZT_ROWS, :] = jnp.concatenate([dfg, jnp.zeros((8, B), F32)], axis=0).astype(BF16)

    rrow = lambda w: pl.BlockSpec((B, w), lambda p: (NB - 1 - p, 0))
    rts = lambda r: pl.BlockSpec((1, r, B), lambda p: (NB - 1 - p, 0, 0))
    return _pc(
        body, name="head_bwd", grid=(NB,),
        in_specs=[rrow(CW), pl.BlockSpec((B, CW), lambda p: (jnp.minimum(NB - p, NB - 1), 0)),
                  rrow(CW), pl.BlockSpec((B, CW), lambda p: (jnp.maximum(NB - 2 - p, 0), 0)),
                  rrow(2 * CW), pl.BlockSpec((2 * A, B), lambda p: (0, NB - 1 - p)),
                  pl.BlockSpec((8, B), lambda p: (0, NB - 1 - p)), rts(H * KR), rts(A), rts(A),
                  pl.BlockSpec((H, 1, 8, B), lambda p: (0, NB - 1 - p, 0, 0)),
                  _const((32, CW)), _const((DH, 1)), _const((DH, 1)), _const((8, 1)), _const((B, B))],
        out_specs=[rrow(2 * CW), rts(ZT_ROWS), _const((32, CW)), _const((8, CW)), _const((DH, 1)),
                   _const((DH, 1)), _const((8, 1))],
        out_shape=[_sds((S, 2 * CW), BF16), _sds((NB, ZT_ROWS, B), BF16), _sds((32, CW)), _sds((8, CW)),
                   _sds((DH, 1)), _sds((DH, 1)), _sds((8, 1))],
        scratch_shapes=[pltpu.VMEM((B + HALO + 8, CW), F32), pltpu.VMEM((8, B + HALO, CW), F32),
                        pltpu.VMEM((B, CW), F32), pltpu.VMEM((8, 1), F32), pltpu.VMEM((8, B), F32)],
        compiler_params=_cp(("arbitrary",)),
    )(du1, du1, u0, u0, alg, zqk, fgT, dqT, dkT, dvT, dfk, conv_w, qg_col, kg_col, bf_col, tri_lo)


def in_bwd_w(x, dzr, dzT, n1g, sc1, sh1):
    S = x.shape[0]
    B = BLK
    NB = S // B

    def body(x_ref, dzr_ref, dzT_ref, g_ref, sc_ref, sh_ref, dwT_hbm, dwr_hbm, dwT_acc, dwr_acc, sem):
        i = pl.program_id(0)

        @pl.when(i == 0)
        def _():
            dwT_acc[...] = jnp.zeros_like(dwT_acc)
            dwr_acc[...] = jnp.zeros_like(dwr_acc)

        h1 = _norm_mod(x_ref[...], g_ref[...], sc_ref[...], sh_ref[...])[2]
        dwT_acc[...] = dwT_acc[...] + _nn(dzT_ref[0], h1.astype(BF16))
        dwr_acc[...] = dwr_acc[...] + _nn(h1.T.astype(BF16), dzr_ref[...])

        @pl.when(i == NB - 1)
        def _():
            for src, dst in ((dwT_acc, dwT_hbm), (dwr_acc, dwr_hbm)):
                cp = pltpu.make_async_copy(src, dst, sem)
                cp.start()
                cp.wait()

    row = lambda w: pl.BlockSpec((B, w), lambda i: (i, 0))
    return _pc(
        body, name="in_bwd_w", grid=(NB,),
        in_specs=[row(D), row(2 * CW), pl.BlockSpec((1, ZT_ROWS, B), lambda i: (i, 0, 0)),
                  _const((1, D)), _const((1, D)), _const((1, D))],
        out_specs=[ANY, ANY],
        out_shape=[_sds((ZT_ROWS, D)), _sds((D, 2 * CW))],
        scratch_shapes=[pltpu.VMEM((ZT_ROWS, D), F32), pltpu.VMEM((D, 2 * CW), F32),
                        pltpu.SemaphoreType.DMA(())],
        compiler_params=_cp(("arbitrary",)),
    )(x, dzr, dzT, n1g, sc1, sh1)


def in_bwd_x(x, dx1, dzr, dzT, n1g, sc1, sh1, w_qkvf, wr, late_g):
    S = x.shape[0]
    B = BLK
    NB = S // B

    def body(x_ref, dx1_ref, dzr_ref, dzT_ref, g_ref, sc_ref, sh_ref, w_hbm, wr_hbm, lg_ref,
             gx_ref, v_ref, recv_ref, w_v, wr_v, sem, send_sems, recv_sems, local_sem):
        i = pl.program_id(0)

        @pl.when(i == 0)
        def _():
            _start_all(_direct_copies(True, lg_ref, recv_ref, send_sems, recv_sems, local_sem))
            for src, dst in ((w_hbm, w_v), (wr_hbm, wr_v)):
                cp = pltpu.make_async_copy(src, dst, sem)
                cp.start()
                cp.wait()
            v_ref[...] = jnp.zeros_like(v_ref)

        g = g_ref[...]
        sc = sc_ref[...]
        r1, xh, _ = _norm_mod(x_ref[...], g, sc, sh_ref[...])
        dh1 = _nt(dzr_ref[...], wr_v[...]) + _nn(w_v[...], dzT_ref[0]).T
        dhx = dh1 * xh
        v_ref[0:1, :] = v_ref[0:1, :] + _colsum(dh1)
        v_ref[1:2, :] = v_ref[1:2, :] + _colsum(dhx) * g
        v_ref[2:3, :] = v_ref[2:3, :] + _colsum(dhx) * (1.0 + sc)
        dxh = dh1 * (g * (1.0 + sc))
        gx_ref[...] = dx1_ref[...] + r1 * (dxh - xh * _rowmean(dxh * xh))

        @pl.when(i == NB - 1)
        def _():
            _wait_all(_direct_copies(True, lg_ref, recv_ref, send_sems, recv_sems, local_sem))

    row = lambda w: pl.BlockSpec((B, w), lambda i: (i, 0))
    return _pc(
        body, name="in_bwd_x", grid=(NB,),
        in_specs=[row(D), row(D), row(2 * CW), pl.BlockSpec((1, ZT_ROWS, B), lambda i: (i, 0, 0)),
                  _const((1, D)), _const((1, D)), _const((1, D)), ANY, ANY, ANY],
        out_specs=[row(D), _const((8, D)), ANY],
        out_shape=[_sds((S, D)), _sds((8, D)), _sds(late_g.shape, late_g.dtype)],
        scratch_shapes=[pltpu.VMEM((D, ZT_ROWS), BF16), pltpu.VMEM((D, 2 * CW), BF16),
                        pltpu.SemaphoreType.DMA(())] + COMM_SEMS,
        compiler_params=_cp(("arbitrary",)),
    )(x, dx1, dzr, dzT, n1g, sc1, sh1, w_qkvf, wr, late_g)


def _adam_math(g, w, m, v):
    m = ADAM_B1 * m + (1.0 - ADAM_B1) * g
    v = ADAM_B2 * v + (1.0 - ADAM_B2) * (g * g)
    m_hat = m / (1.0 - ADAM_B1 ** ADAM_STEP)
    v_hat = v / (1.0 - ADAM_B2 ** ADAM_STEP)
    delta = -ADAM_LR * (m_hat / (jnp.sqrt(v_hat) + ADAM_EPS) + ADAM_WD * w)
    return delta, m, v


def _row_tile(R):
    for t in (1024, 512, 256, 128, 64, 32, 16, 8):
        if R % t == 0:
            return t
    return R


def sum_slots(parts, name):
    K, R, C = parts.shape
    T = _row_tile(R)

    def body(p_ref, o_ref):
        s = p_ref[0]
        for k in range(1, K):
            s = s + p_ref[k]
        o_ref[...] = s

    return _pc(body, name=name, grid=(R // T,),
               in_specs=[pl.BlockSpec((K, T, C), lambda i: (0, i, 0))],
               out_specs=pl.BlockSpec((T, C), lambda i: (i, 0)), out_shape=_sds((R, C)),
               compiler_params=_cp(("arbitrary",)))(parts)


def adamw_slots(parts, w, m, v, name):
    K, R, C = parts.shape
    T = R
    while K * T * C * 4 > (8 << 20) and T % 16 == 0:
        T //= 2

    def body(p_ref, w_ref, m_ref, v_ref, g_ref, d_ref, nm_ref, nv_ref):
        g = p_ref[0].astype(F32)
        for k in range(1, K):
            g = g + p_ref[k].astype(F32)
        g_ref[...] = g
        d_ref[...], nm_ref[...], nv_ref[...] = _adam_math(g, w_ref[...], m_ref[...], v_ref[...])

    t2 = pl.BlockSpec((T, C), lambda i: (i, 0))
    return _pc(body, name=name, grid=(R // T,),
               in_specs=[pl.BlockSpec((K, T, C), lambda i: (0, i, 0)), t2, t2, t2],
               out_specs=[t2, t2, t2, t2], out_shape=[_sds((R, C))] * 4,
               compiler_params=_cp(("arbitrary",)))(parts, w, m, v)


def adamw_many(gs, ws, ms, vs, name):
    n = len(ws)

    def body(*refs):
        outs = refs[4 * n:]
        for q in range(n):
            d, nm, nv = _adam_math(refs[q][...], refs[n + q][...], refs[2 * n + q][...], refs[3 * n + q][...])
            outs[q][...] = d
            outs[n + q][...] = nm
            outs[2 * n + q][...] = nv

    res = _pc(body, name=name, grid=(1,),
              in_specs=[_const(a.shape) for a in list(gs) + list(ws) + list(ms) + list(vs)],
              out_specs=[_const(w.shape) for w in ws] * 3, out_shape=[_sds(w.shape) for w in ws] * 3,
              compiler_params=_cp())(*gs, *ws, *ms, *vs)
    return res[0:n], res[n:2 * n], res[2 * n:3 * n]


def ada_grad_adamw(cT, dmod_cols, w, m, v):
    NCOL = w.shape[1]

    def body(cT_ref, dm_ref, w_ref, m_ref, v_ref, g_ref, d_ref, nm_ref, nv_ref):
        def term(b):
            cv = cT_ref[b]
            return (cv * _sigmoid(cv)) * dm_ref[b:b + 1, :]

        g = term(0)
        for b in range(1, N_DEV):
            g = g + term(b)
        g_ref[...] = g
        d_ref[...], nm_ref[...], nv_ref[...] = _adam_math(g, w_ref[...], m_ref[...], v_ref[...])

    full = _const((D, NCOL))
    return _pc(body, name="ada_grad_adamw", grid=(1,),
               in_specs=[_const((N_DEV, D, 1)), _const((N_DEV, NCOL)), full, full, full],
               out_specs=[full, full, full, full], out_shape=[_sds((D, NCOL))] * 4,
               compiler_params=_cp())(cT, dmod_cols, w, m, v)


def _pack_rows(vecs, rows=None):
    flat = jnp.concatenate([jnp.ravel(v) for v in vecs])
    n = flat.shape[0]
    r = -(-n // 1024) * 8 if rows is None else rows
    return jnp.pad(flat, (0, r * 128 - n)).reshape(r, 128)


def _unpack_rows(packed, shapes):
    flat = packed.reshape(-1)
    out, off = [], 0
    for s in shapes:
        n = 1
        for d in s:
            n *= d
        out.append(flat[off:off + n].reshape(s))
        off += n
    return out


def _cols_to_shards(w, cols):
    rows = w.shape[0]
    return w.reshape(rows, N_DEV, cols).transpose(1, 0, 2).reshape(N_DEV, rows * cols)


def kernel(x, c, w_ada, b_ada, norm1_g, w_in, q_norm_g, k_norm_g, b_f, conv_w, conv_b, conv_ln_g, conv_ln_b, beta_attn, beta_conv, w_out, norm2_g, w_ff1, w_ff2, loss_target, m_w_ada, m_b_ada, m_norm1_g, m_w_in, m_q_norm_g, m_k_norm_g, m_b_f, m_conv_w, m_conv_b, m_conv_ln_g, m_conv_ln_b, m_beta_attn, m_beta_conv, m_w_out, m_norm2_g, m_w_ff1, m_w_ff2, v_w_ada, v_b_ada, v_norm1_g, v_w_in, v_q_norm_g, v_k_norm_g, v_b_f, v_conv_w, v_conv_b, v_conv_ln_g, v_conv_ln_b, v_beta_attn, v_beta_conv, v_w_out, v_norm2_g, v_w_ff1, v_w_ff2):
    S = x.shape[1]
    B = BLK
    NB = S // B
    me = 4 * lax.axis_index("x") + 2 * lax.axis_index("y") + lax.axis_index("c")
    xs = x[0]
    tgt = loss_target[0]

    ADA_C, IN_C, FF_C, CV_C = w_ada.shape[2], w_in.shape[2], w_ff1.shape[2], conv_w.shape[2]
    OUT_R, FF_R = w_out.shape[1], w_ff2.shape[1]

    g_in, g_c, g_cw = gather_two_level([w_in[0].T.astype(BF16), c, conv_w[0]], "ag_weights_early")
    late_w = [w_out[0].astype(BF16), w_ff1[0].astype(BF16), w_ff2[0].astype(BF16)]

    w_in_t = g_in.reshape(N_DEV * IN_C, D)
    c_all = g_c.reshape(N_DEV, D)
    conv_w_full = g_cw.transpose(1, 0, 2).reshape(KC, N_DEV * CV_C)
    conv_w_pad = jnp.pad(conv_w_full, ((0, 32 - KC), (0, 0)))

    wT = jnp.pad(w_in_t[0:3 * A + H], ((0, ZT_ROWS - 3 * A - H), (0, 0)))
    w_qkvf = wT.T
    wr = w_in_t[3 * A + H:].T

    qg_col = q_norm_g.reshape(DH, 1)
    kg_col = k_norm_g.reshape(DH, 1)
    bf_col = b_f.reshape(H, 1)
    beta_a_col = beta_attn.reshape(A, 1)
    ii = lax.broadcasted_iota(jnp.int32, (B, B), 0)
    jj = lax.broadcasted_iota(jnp.int32, (B, B), 1)
    tri_up = (ii <= jj).astype(BF16)
    tri_lo = (ii >= jj).astype(BF16)

    modc = mod_columns(c_all, w_ada[0], lax.dynamic_slice(b_ada, (0, me * ADA_C), (1, ADA_C)))
    gm, = gather_direct([modc], "ag_mod")
    mod = lax.dynamic_index_in_dim(gm, me, axis=1, keepdims=False).reshape(1, N_DEV * ADA_C)
    sh1, sc1, g1, sh2, sc2, g2 = [mod[:, k * D:(k + 1) * D] for k in range(6)]
    qTa, kT, kaug, vT, vaug, zqk, fgT, alg, u0, stat = fwd_in(xs, norm1_g, sc1, sh1, wT, wr, qg_col,
                                                             kg_col, bf_col, tri_up)
    fmax = stat[:, :, 0].T
    fmin = stat[:, :, 1].T
    qk_max = jnp.sqrt(jnp.max(stat[:, :, 2], axis=0) * jnp.max(stat[:, :, 3], axis=0))
    thr = -(PRUNE + (2.02 / LOG2E) * qk_max)
    oT, lse, g_wo, w1_shards, g_w2 = attn_fwd(fmax, fmin, thr, qTa, kaug, vT, late_w)
    wo_full = g_wo.reshape(D, D)
    w2_full = g_w2.reshape(FF, D)
    x1, o, u1 = conv_merge_out(u0, xs, oT, conv_w_pad, conv_b, conv_ln_g, conv_ln_b, beta_conv,
                               beta_a_col, wo_full, g1)
    dy, loss_part, dg2, ra, h2t = mlp_fwd_loss(x1, tgt, norm2_g, sc2, sh2, g2, w1_shards, w2_full)

    dw1s, dw2, dh2p = mlp_bwd(h2t, dy, ra, g2, w1_shards, w2_full)
    dx1, doTa, delta, du1, dwo, v1, v2, dba = merge_bwd(dh2p, x1, dy, o, oT, u1, norm2_g, sc2, g1,
                                                         conv_ln_g, conv_ln_b, beta_conv, beta_a_col, wo_full)
    early_g = [dwo.reshape(N_DEV, OUT_R, D), dw1s, dw2.reshape(N_DEV, FF_R, D)]
    dqT, dkT, dvT, dfk, r_out, r_f1, r_f2 = attn_bwd(fmax, fmin, thr, qTa, kaug, kT, vaug, doTa, lse,
                                                     delta, early_g)
    dzr, dzT, dcw, vc, dqg, dkg, dbf = head_bwd(du1, u0, alg, zqk, fgT, dqT, dkT, dvT, dfk, conv_w_pad,
                                                qg_col, kg_col, bf_col, tri_lo)
    dwT, dwr = in_bwd_w(xs, dzr, dzT, norm1_g, sc1, sh1)

    dw_in = jnp.concatenate([dwT.T[:, 0:3 * A + H], dwr], axis=1)
    late_g = _cols_to_shards(dw_in, IN_C)
    rows_b = -(-late_g.shape[1] // (128 * 256)) * 256
    late_g = jnp.pad(late_g, ((0, 0), (0, rows_b * 128 - late_g.shape[1]))).astype(BF16)
    grad_x, v0, recv_b = in_bwd_x(xs, dx1, dzr, dzT, norm1_g, sc1, sh1, w_qkvf, wr,
                                  late_g.reshape(N_DEV, rows_b, 128))

    dmod = jnp.concatenate([v0[0], v0[1], v1[3], v1[0], v1[1], dg2[0]])
    small1 = _pack_rows([dmod, v0[2], v1[2], dcw, vc[0], v2[1], v2[2], v2[0], dba, dqg, dkg,
                         jnp.pad(dbf.reshape(-1), (0, 120)), jnp.pad(loss_part.reshape(-1), (0, 127))])
    gs1, = gather_direct([small1], "ag_small_bwd")
    tot = sum_slots(gs1, "sum_small")
    (g_b_ada, g_n1, g_n2, g_cw, g_cb, g_lng, g_lnb, g_bc, g_ba, g_qg, g_kg, g_bf, loss_v) = _unpack_rows(
        tot, [(1, 6 * D), (1, D), (1, D), (32, CW), (1, CW), (1, CW), (1, CW), (1, CW), (1, A),
              (1, DH), (1, DH), (1, 128), (1, 128)])
    loss = loss_v[0, 0]
    g_bf = g_bf[:, 0:H]
    g_cw_mine = lax.dynamic_slice(g_cw[0:KC], (0, me * CV_C), (KC, CV_C)).reshape(1, KC, CV_C)

    small_names = [(b_ada, m_b_ada, v_b_ada, g_b_ada), (norm1_g, m_norm1_g, v_norm1_g, g_n1),
                   (q_norm_g, m_q_norm_g, v_q_norm_g, g_qg), (k_norm_g, m_k_norm_g, v_k_norm_g, g_kg),
                   (b_f, m_b_f, v_b_f, g_bf), (conv_w, m_conv_w, v_conv_w, g_cw_mine),
                   (conv_b, m_conv_b, v_conv_b, g_cb), (conv_ln_g, m_conv_ln_g, v_conv_ln_g, g_lng),
                   (conv_ln_b, m_conv_ln_b, v_conv_ln_b, g_lnb), (beta_attn, m_beta_attn, v_beta_attn, g_ba),
                   (beta_conv, m_beta_conv, v_beta_conv, g_bc), (norm2_g, m_norm2_g, v_norm2_g, g_n2)]
    sgs = [t[3].reshape(t[0].shape) for t in small_names]
    sds, sms, svs = adamw_many(sgs, *[[t[k] for t in small_names] for k in range(3)], "adamw_small")

    dmod_all = gs1[:, 0:48, :].reshape(N_DEV, N_DEV, ADA_C)
    dmod_cols = lax.dynamic_index_in_dim(dmod_all, me, axis=1, keepdims=False)
    ga, da, ma_, va_ = ada_grad_adamw(c_all.reshape(N_DEV, D, 1), dmod_cols, w_ada[0], m_w_ada[0], v_w_ada[0])

    res_b = adamw_slots(recv_b, *[_pack_rows(ws, rows=rows_b) for ws in ([w_in], [m_w_in], [v_w_in])],
                        "adamw_in")
    late = []
    for nm, parts, (w_, m_, v_) in (("adamw_out", r_out, (w_out, m_w_out, v_w_out)),
                                    ("adamw_ff1", r_f1, (w_ff1, m_w_ff1, v_w_ff1)),
                                    ("adamw_ff2", r_f2, (w_ff2, m_w_ff2, v_w_ff2))):
        late.append([r[None] for r in adamw_slots(parts, w_[0], m_[0], v_[0], nm)])
    bgs, bds, bms, bvs = [_unpack_rows(res_b[q], [(1, D, IN_C)]) + [late[0][q], late[1][q], late[2][q]]
                          for q in range(4)]

    def assemble(small, ada, bigs):
        (b_ada_, n1_, qg_, kg_, bf_, cw_, cb_, lng_, lnb_, ba_, bc_, n2_) = small
        return [ada.reshape(1, D, ADA_C), b_ada_, n1_, bigs[0], qg_, kg_, bf_, cw_, cb_, lng_, lnb_, ba_, bc_,
                bigs[1], n2_, bigs[2], bigs[3]]

    return (loss, grad_x.reshape(1, S, D), *assemble(sgs, ga, bgs), *assemble(sds, da, bds),
            *assemble(sms, ma_, bms), *assemble(svs, va_, bvs))
```

```python
import functools

import jax
import jax.numpy as jnp
from jax import lax
from jax.experimental import pallas as pl
from jax.experimental.pallas import tpu as pltpu

F32 = jnp.float32
BF16 = jnp.bfloat16

D = 1024
A = 512
CW = 512
H = 8
DH = 64
FF = 4096
KC = 31
HALO = 32
KR = 80
ZT_ROWS = 1552
FCH = 1024
EPS = 1e-6
BLK = 512
N_DEV = 8
VMEM_LIMIT = 56 * 1024 * 1024

ADAM_LR = 0.001
ADAM_B1 = 0.9
ADAM_B2 = 0.999
ADAM_EPS = 1e-08
ADAM_WD = 0.01
ADAM_STEP = 10

MESH = pl.DeviceIdType.MESH
ANY = pl.BlockSpec(memory_space=pl.ANY)
SMEM = pl.BlockSpec(memory_space=pltpu.SMEM)
PRUNE = 105.0
LOG2E = 1.4426950408889634


def _pc(body, **kw):
    return pl.pallas_call(body, **kw)


def _cp(sem=None):
    return pltpu.CompilerParams(dimension_semantics=sem, vmem_limit_bytes=VMEM_LIMIT)


def _sds(shape, dtype=F32):
    return jax.ShapeDtypeStruct(shape, dtype)


def _const(shape):
    n = len(shape)
    return pl.BlockSpec(shape, lambda *a: (0,) * n)


def _nt(a, b):
    return lax.dot_general(a, b, (((1,), (1,)), ((), ())), preferred_element_type=F32)


def _nn(a, b):
    return jnp.dot(a, b, preferred_element_type=F32)


def _sigmoid(v):
    return 1.0 / (1.0 + jnp.exp(-v))


def _split3(v):
    a1 = v.astype(BF16)
    r1 = v - a1.astype(F32)
    a2 = r1.astype(BF16)
    a3 = (r1 - a2.astype(F32)).astype(BF16)
    return a1, a2, a3


def _rowmean(v):
    return jnp.mean(v, axis=-1, keepdims=True)


def _colsum(v):
    return jnp.sum(v, axis=0, keepdims=True)


def _lanesum(v):
    return jnp.sum(v, axis=-1, keepdims=True)


def _coords():
    return lax.axis_index("x"), lax.axis_index("y"), lax.axis_index("c")


def _direct_copies(scatter, src_ref, dst_ref, send_sems, recv_sems, local_sem):
    mx, my, mc = _coords()
    me = 4 * mx + 2 * my + mc
    if scatter:
        local = pltpu.make_async_copy(src_ref.at[me], dst_ref.at[0], local_sem)
    else:
        local = pltpu.make_async_copy(src_ref, dst_ref.at[me], local_sem)
    remote = []
    for r in range(1, N_DEV):
        px = 1 - mx if r & 4 else mx
        py = 1 - my if r & 2 else my
        pcc = 1 - mc if r & 1 else mc
        remote.append(pltpu.make_async_remote_copy(
            src_ref=src_ref.at[4 * px + 2 * py + pcc] if scatter else src_ref,
            dst_ref=dst_ref.at[r] if scatter else dst_ref.at[me],
            send_sem=send_sems.at[r - 1], recv_sem=recv_sems.at[r - 1],
            device_id=(px, py, pcc), device_id_type=MESH))
    return [local] + remote


def _start_all(copies):
    for cp in copies:
        cp.start()


def _wait_all(copies):
    for cp in copies[1:]:
        cp.wait()
    copies[0].wait()


COMM_SEMS = [pltpu.SemaphoreType.DMA((7,)), pltpu.SemaphoreType.DMA((7,)), pltpu.SemaphoreType.DMA(())]


def gather_direct(arrays, name):
    NG = len(arrays)

    def body(*refs):
        sems = refs[2 * NG:]
        sets = [_direct_copies(False, refs[q], refs[NG + q], *sems[3 * q:3 * q + 3]) for q in range(NG)]
        for copies in sets:
            _start_all(copies)
        for copies in sets:
            _wait_all(copies)

    return _pc(body, name=name, out_shape=[_sds((N_DEV,) + a.shape, a.dtype) for a in arrays],
               in_specs=[ANY] * NG, out_specs=[ANY] * NG, scratch_shapes=COMM_SEMS * NG)(*arrays)


def gather_two_level(arrays, name):
    NG = len(arrays)

    def body(*refs):
        mx, my, mc = _coords()
        me, sibling = (mx, my, mc), (mx, my, 1 - mc)
        chips = [(1 - mx, my), (mx, 1 - my), (1 - mx, 1 - my)]
        plans = []
        for q in range(NG):
            x_ref, out_ref = refs[q], refs[NG + q]
            send_sems, recv_sems, local_sem = refs[2 * NG + 3 * q:2 * NG + 3 * q + 3]

            def rows(block, out_ref=out_ref):
                return out_ref.at[4 * block[0] + 2 * block[1] + block[2]]

            def copy(k, block, to, src=None, rows=rows, send_sems=send_sems, recv_sems=recv_sems):
                return pltpu.make_async_remote_copy(
                    src_ref=rows(block) if src is None else src, dst_ref=rows(block),
                    send_sem=send_sems.at[k], recv_sem=recv_sems.at[k], device_id=to, device_id_type=MESH)

            mine = pltpu.make_async_copy(x_ref, rows(me), local_sem)
            first = [copy(0, me, sibling, src=x_ref)]
            first += [copy(1 + t, me, (*chip, mc), src=x_ref) for t, chip in enumerate(chips)]
            passed = [copy(4 + t, (*chip, mc), sibling) for t, chip in enumerate(chips)]
            _start_all([mine] + first)
            plans.append((copy, mine, first, passed))
        for copy, mine, first, passed in plans:
            for t, chip in enumerate(chips):
                copy(1 + t, (*chip, mc), me).wait_recv()
                passed[t].start()
        for copy, mine, first, passed in plans:
            copy(0, sibling, me).wait_recv()
            for t, chip in enumerate(chips):
                copy(4 + t, (*chip, 1 - mc), me).wait_recv()
            for cp in first + passed:
                cp.wait_send()
            mine.wait()

    return _pc(body, name=name, out_shape=[_sds((N_DEV,) + a.shape, a.dtype) for a in arrays],
               in_specs=[ANY] * NG, out_specs=[ANY] * NG, scratch_shapes=COMM_SEMS * NG)(*arrays)


def mod_columns(c_all, w_shard, b_cols):
    NCOL = w_shard.shape[1]

    def body(c_ref, w_ref, b_ref, o_ref):
        cv = c_ref[...]
        o_ref[...] = _nn((cv * _sigmoid(cv)).astype(BF16), w_ref[...].astype(BF16)) + b_ref[...]

    return _pc(body, name="mod_columns", out_shape=_sds((N_DEV, NCOL)),
               in_specs=[_const((N_DEV, D)), _const((D, NCOL)), _const((1, NCOL))],
               out_specs=_const((N_DEV, NCOL)), grid=(1,), compiler_params=_cp())(c_all, w_shard, b_cols)


def _norm_mod(xv, g, sc, sh):
    r = lax.rsqrt(_rowmean(xv * xv) + EPS)
    xh = xv * r
    return r, xh, xh * (g * (1.0 + sc)) + sh


def _log_sigmoid(v):
    e = jnp.exp(-jnp.abs(v))
    l1p = jnp.where(e < 1e-4, e * (1.0 - 0.5 * e), jnp.log(1.0 + e))
    return jnp.minimum(v, 0.0) - l1p


def fwd_in(x, n1g, sc1, sh1, wT, wr, qg_col, kg_col, bf_col, tri):
    S = x.shape[0]
    B = BLK
    NB = S // B

    def body(x_ref, g_ref, sc_ref, sh_ref, wT_ref, wr_ref, qg_ref, kg_ref, bf_ref, tri_ref,
             qTa_ref, kT_ref, kaug_ref, vT_ref, vaug_ref, zqk_ref, fgT_ref, alg_ref, u0_ref, stat_ref,
             carry, tbuf, fs, nq, nk):
        i = pl.program_id(0)

        @pl.when(i == 0)
        def _():
            carry[...] = jnp.zeros_like(carry)

        _, _, h = _norm_mod(x_ref[...], g_ref[...], sc_ref[...], sh_ref[...])
        hb = h.astype(BF16)
        zT = _nt(wT_ref[...], hb)
        zr = _nn(hb, wr_ref[...])
        zqk_ref[...] = zT[0:2 * A]
        fgT = zT[3 * A:3 * A + 8]
        fgT_ref[...] = fgT
        alg_ref[...] = zr
        u0_ref[...] = zr[:, 0:CW] * _sigmoid(zr[:, CW:2 * CW])

        logf = _log_sigmoid(fgT + bf_ref[...])
        a1, a2, a3 = _split3(logf)
        tr = tri_ref[...]
        F = _nn(a1, tr) + _nn(a2, tr) + _nn(a3, tr) + carry[...]
        carry[...] = carry[...] + _lanesum(logf)
        p1, p2, p3 = _split3(F * LOG2E)
        n1, n2, n3 = _split3(F * (-LOG2E))
        for k, v in enumerate((p1, p2, p3, n1, n2, n3)):
            fs[k] = v.astype(F32)

        rowi = lax.broadcasted_iota(jnp.int32, (8, B), 0)
        zeros_tail = jnp.zeros((128 - DH - 8, B), F32)
        ones_row = jnp.where(lax.broadcasted_iota(jnp.int32, (KR - DH, B), 0) == 0, 1.0, 0.0).astype(BF16)
        for hh in range(H):
            sl = slice(hh * DH, (hh + 1) * DH)
            q = zT[sl]
            k = zT[A + hh * DH:A + (hh + 1) * DH]
            v = zT[2 * A + hh * DH:2 * A + (hh + 1) * DH]
            qh = q * lax.rsqrt(jnp.mean(q * q, axis=0, keepdims=True) + EPS) * qg_ref[...] * (0.125 * LOG2E)
            kh = k * lax.rsqrt(jnp.mean(k * k, axis=0, keepdims=True) + EPS) * kg_ref[...]
            frow = [fs[kk, hh:hh + 1, :] for kk in range(6)]
            qx = jnp.where(rowi < 3, 1.0, jnp.where(rowi == 3, frow[0], jnp.where(
                rowi == 4, frow[1], jnp.where(rowi == 5, frow[2], 0.0))))
            kx = jnp.where(rowi == 0, frow[3], jnp.where(rowi == 1, frow[4], jnp.where(
                rowi == 2, frow[5], jnp.where(rowi < 6, 1.0, 0.0))))
            tbuf[0:DH, :] = qh
            tbuf[DH:DH + 8, :] = qx
            tbuf[DH + 8:128, :] = zeros_tail
            qTa_ref[0, hh * 128:(hh + 1) * 128, :] = tbuf[...].astype(BF16)
            tbuf[0:DH, :] = kh
            tbuf[DH:DH + 8, :] = kx
            kaug_ref[hh] = tbuf[...].T.astype(BF16)
            kT_ref[0, hh * KR:hh * KR + DH, :] = kh.astype(BF16)
            kT_ref[0, hh * KR + DH:(hh + 1) * KR, :] = ones_row
            tbuf[0:DH, :] = v
            tbuf[DH:DH + 8, :] = jnp.zeros((8, B), F32)
            vaug_ref[hh] = tbuf[...].T.astype(BF16)
            vT_ref[0, sl, :] = v.astype(BF16)
            nq[hh:hh + 1, :] = jnp.max(_colsum(qh * qh), axis=1, keepdims=True)
            nk[hh:hh + 1, :] = jnp.max(_colsum(kh * kh), axis=1, keepdims=True)

        lane = lax.broadcasted_iota(jnp.int32, (8, 128), 1)
        stat_ref[0] = jnp.where(lane == 0, jnp.max(F, axis=1, keepdims=True), jnp.where(
            lane == 1, jnp.min(F, axis=1, keepdims=True), jnp.where(
                lane == 2, nq[...], jnp.where(lane == 3, nk[...], 0.0))))

    row = lambda w: pl.BlockSpec((B, w), lambda i: (i, 0))
    tsp = lambda r: pl.BlockSpec((1, r, B), lambda i: (i, 0, 0))
    return _pc(
        body, name="fwd_in", grid=(NB,),
        in_specs=[row(D), _const((1, D)), _const((1, D)), _const((1, D)), _const((ZT_ROWS, D)),
                  _const((D, 2 * CW)), _const((DH, 1)), _const((DH, 1)), _const((8, 1)), _const((B, B))],
        out_specs=[tsp(H * 128), tsp(H * KR), pl.BlockSpec((H, B, 128), lambda i: (0, i, 0)), tsp(A),
                   pl.BlockSpec((H, B, 128), lambda i: (0, i, 0)),
                   pl.BlockSpec((2 * A, B), lambda i: (0, i)), pl.BlockSpec((8, B), lambda i: (0, i)),
                   row(2 * CW), row(CW), pl.BlockSpec((1, 8, 128), lambda i: (i, 0, 0))],
        out_shape=[_sds((NB, H * 128, B), BF16), _sds((NB, H * KR, B), BF16), _sds((H, S, 128), BF16),
                   _sds((NB, A, B), BF16), _sds((H, S, 128), BF16), _sds((2 * A, S)), _sds((8, S)),
                   _sds((S, 2 * CW)), _sds((S, CW)), _sds((NB, 8, 128))],
        scratch_shapes=[pltpu.VMEM((8, 1), F32), pltpu.VMEM((128, B), F32), pltpu.VMEM((6, 8, B), F32),
                        pltpu.VMEM((8, 1), F32), pltpu.VMEM((8, 1), F32)],
        compiler_params=_cp(("arbitrary",)),
    )(x, n1g, sc1, sh1, wT, wr, qg_col, kg_col, bf_col, tri)


def _first_key_block(top, fmin_ref, cut, h, i):
    return lax.while_loop(
        lambda j: jnp.logical_and(j > 0, top - fmin_ref[h, jnp.maximum(j - 1, 0)] >= cut),
        lambda j: j - 1, i)


def _causal_keep(B):
    return lax.broadcasted_iota(jnp.int32, (B, B), 0) <= lax.broadcasted_iota(jnp.int32, (B, B), 1)


def _keep_right_half(B):
    hb = B // 2
    return lax.broadcasted_iota(jnp.int32, (B, hb), 0) <= lax.broadcasted_iota(jnp.int32, (B, hb), 1) + hb


def _keep_top_half(B):
    hb = B // 2
    return lax.broadcasted_iota(jnp.int32, (hb, B), 0) <= lax.broadcasted_iota(jnp.int32, (hb, B), 1)


def attn_fwd(fmax, fmin, thr, qTa, kaug, vT, late_w):
    NB, _, B = qTa.shape
    S = NB * B
    NG = len(late_w)
    HB = B // 2

    def body(fmax_ref, fmin_ref, thr_ref, q_ref, k_ref, v_ref, *rest):
        w_refs = rest[0:NG]
        o_ref, lse_ref = rest[NG:NG + 2]
        gw_refs = rest[NG + 2:2 * NG + 2]
        s0, s1, m_ref, l_ref, acc_ref = rest[2 * NG + 2:2 * NG + 7]
        sems = rest[2 * NG + 7:]
        h = pl.program_id(0)
        i = pl.program_id(1)

        def gathers():
            return [_direct_copies(False, w_refs[q], gw_refs[q], *sems[3 * q:3 * q + 3]) for q in range(NG)]

        @pl.when(jnp.logical_and(h == 0, i == 0))
        def _():
            for copies in gathers():
                _start_all(copies)

        jlo = _first_key_block(fmax_ref[h, i], fmin_ref, thr_ref[h], h, i)
        n = i - jlo

        def scores(j, s_ref):
            s_ref[...] = _nn(k_ref[0, pl.ds(pl.multiple_of(j * B, B), B), :], q_ref[0])

        def softmax_cols(s, j, cols, rows):
            m = m_ref[:, cols]
            mn = jnp.maximum(m, jnp.max(s, axis=0, keepdims=True))
            a = jnp.exp2(m - mn)
            p = jnp.exp2(s - mn)
            m_ref[:, cols] = mn
            l_ref[:, cols] = a * l_ref[:, cols] + _colsum(p)
            acc_ref[:, cols] = a * acc_ref[:, cols] + _nn(v_ref[j][:, rows], p.astype(BF16))

        def softmax_step(s_ref, j, masked):
            if not masked:
                softmax_cols(s_ref[...], j, slice(None), slice(None))
                return
            lo, hi = slice(0, HB), slice(HB, B)
            softmax_cols(jnp.where(_causal_keep(HB), s_ref[lo, lo], -jnp.inf), j, lo, lo)
            softmax_cols(jnp.where(_keep_right_half(B), s_ref[:, hi], -jnp.inf), j, hi, slice(None))

        m_ref[...] = jnp.full((1, B), -jnp.inf, F32)
        l_ref[...] = jnp.zeros((1, B), F32)
        acc_ref[...] = jnp.zeros((DH, B), F32)
        scores(jlo, s0)

        def pair(t, carry):
            j = jlo + 2 * t
            scores(j + 1, s1)
            softmax_step(s0, j, False)
            scores(j + 2, s0)
            softmax_step(s1, j + 1, False)
            return carry

        lax.fori_loop(0, n // 2, pair, 0)

        @pl.when(n % 2 == 1)
        def _():
            scores(i, s1)
            softmax_step(s0, i - 1, False)
            softmax_step(s1, i, True)

        @pl.when(n % 2 == 0)
        def _():
            softmax_step(s0, i, True)

        l = l_ref[...]
        o_ref[0] = acc_ref[...] / l
        lse_ref[0, 0] = m_ref[...] + jnp.log2(l)

        @pl.when(jnp.logical_and(h == H - 1, i == NB - 1))
        def _():
            for copies in gathers():
                _wait_all(copies)

    return _pc(
        body, name="attn_fwd", grid=(H, NB),
        in_specs=[SMEM, SMEM, SMEM, pl.BlockSpec((1, 128, B), lambda h, i: (i, h, 0)),
                  pl.BlockSpec((1, S, 128), lambda h, i: (h, 0, 0)),
                  pl.BlockSpec((NB, DH, B), lambda h, i: (0, h, 0))] + [ANY] * NG,
        out_specs=[pl.BlockSpec((1, DH, B), lambda h, i: (i, h, 0)),
                   pl.BlockSpec((1, 1, 1, B), lambda h, i: (h, i, 0, 0))] + [ANY] * NG,
        out_shape=[_sds((NB, A, B)), _sds((H, NB, 1, B))]
        + [_sds((N_DEV,) + w.shape, w.dtype) for w in late_w],
        scratch_shapes=[pltpu.VMEM((B, B), F32), pltpu.VMEM((B, B), F32), pltpu.VMEM((1, B), F32),
                        pltpu.VMEM((1, B), F32), pltpu.VMEM((DH, B), F32)] + COMM_SEMS * NG,
        compiler_params=_cp(("arbitrary", "arbitrary")),
    )(fmax, fmin, thr, qTa, kaug, vT, *late_w)


def attn_bwd(fmax, fmin, thr, qTa, kaug, kT, vaug, doTa, lse, delta, early_g):
    NG = len(early_g)
    NB, _, B = qTa.shape
    QR = 80
    CH = min(256, B)

    def body(fmax_ref, fmin_ref, thr_ref, q_ref, ka_ref, kt_ref, va_ref, do_ref, lse_ref, dl_ref, *rest):
        g_refs = rest[0:NG]
        dq_ref, dk_ref, dv_ref, dfk_ref = rest[NG:NG + 4]
        recv_refs = rest[NG + 4:2 * NG + 4]
        s0, d0, s1, d1, p0, e0, p1, e1, dk_acc, dv_acc = rest[2 * NG + 4:2 * NG + 14]
        sems = rest[2 * NG + 14:]
        h = pl.program_id(0)
        step = pl.program_id(1)
        j = NB - 1 - step

        def exchanges():
            return [_direct_copies(True, g_refs[q], recv_refs[q], *sems[3 * q:3 * q + 3]) for q in range(NG)]

        @pl.when(jnp.logical_and(h == 0, step == 0))
        def _():
            for copies in exchanges():
                _start_all(copies)

        bottom = fmin_ref[h, j]
        cut = thr_ref[h]
        ihi = lax.while_loop(
            lambda i: jnp.logical_and(i < NB - 1, fmax_ref[h, jnp.minimum(i + 1, NB - 1)] - bottom >= cut),
            lambda i: i + 1, j)

        @pl.when(step == 0)
        def _():
            dq_ref[...] = jnp.zeros_like(dq_ref)

        dk_acc[...] = jnp.zeros_like(dk_acc)
        dv_acc[...] = jnp.zeros_like(dv_acc)

        n = ihi - j + 1

        def blk(k):
            return jnp.minimum(j + k, ihi)

        def products(k, s_ref, d_ref):
            i = blk(k)
            s_ref[...] = _nn(ka_ref[0], q_ref[i])
            d_ref[...] = _nn(va_ref[0], do_ref[i])

        def elementwise(k, s_ref, d_ref, p_ref, ds_ref, masked):
            i = blk(k)
            p = jnp.exp2(s_ref[...] - lse_ref[0, i])
            if masked:
                p = jnp.where(_causal_keep(B), p, 0.0)
            p_ref[...] = p.astype(BF16)
            ds_ref[...] = (p * (d_ref[...] - dl_ref[0, i])).astype(BF16)

        def grads(k, p_ref, ds_ref):
            i = j + k
            dsb = ds_ref[...]
            dv_acc[...] = dv_acc[...] + _nt(do_ref[i][0:DH], p_ref[...])
            dk_acc[...] = dk_acc[...] + _nt(q_ref[i][0:QR], dsb)
            dq_ref[i] = dq_ref[i] + _nn(kt_ref[0], dsb)

        def trip(kp, s_p, d_p, ke, s_e, d_e, p_e, e_e, kg, p_g, e_g):
            ip, ie, ig = blk(kp), blk(ke), j + kg
            lse_e = lse_ref[0, ie]
            dl_e = dl_ref[0, ie]
            dq_part = jnp.zeros((KR, B), F32)
            for c in range(B // CH):
                rows = slice(c * CH, (c + 1) * CH)
                s_p[rows, :] = _nn(ka_ref[0, rows, :], q_ref[ip])
                d_p[rows, :] = _nn(va_ref[0, rows, :], do_ref[ip])
                p = jnp.exp2(s_e[rows, :] - lse_e)
                p_e[rows, :] = p.astype(BF16)
                e_e[rows, :] = (p * (d_e[rows, :] - dl_e)).astype(BF16)
                dsb = e_g[rows, :]
                dv_acc[:, rows] = dv_acc[:, rows] + _nt(do_ref[ig][0:DH], p_g[rows, :])
                dk_acc[:, rows] = dk_acc[:, rows] + _nt(q_ref[ig][0:QR], dsb)
                dq_part = dq_part + _nn(kt_ref[0, :, rows], dsb)
            dq_ref[ig] = dq_ref[ig] + dq_part

        HB = B // 2
        lo, hi = slice(0, HB), slice(HB, B)

        def diagonal_products(s_ref, d_ref):
            s_ref[lo, :] = _nn(ka_ref[0, lo, :], q_ref[j])
            d_ref[lo, :] = _nn(va_ref[0, lo, :], do_ref[j])
            s_ref[hi, hi] = _nn(ka_ref[0, hi, :], q_ref[j, :, hi])
            d_ref[hi, hi] = _nn(va_ref[0, hi, :], do_ref[j, :, hi])

        def diagonal_elementwise(s_ref, d_ref, p_ref, ds_ref):
            lse0 = lse_ref[0, j]
            dl0 = dl_ref[0, j]
            p = jnp.where(_keep_top_half(B), jnp.exp2(s_ref[lo, :] - lse0), 0.0)
            p_ref[lo, :] = p.astype(BF16)
            ds_ref[lo, :] = (p * (d_ref[lo, :] - dl0)).astype(BF16)
            p = jnp.where(_causal_keep(HB), jnp.exp2(s_ref[hi, hi] - lse0[:, hi]), 0.0)
            p_ref[hi, hi] = p.astype(BF16)
            ds_ref[hi, hi] = (p * (d_ref[hi, hi] - dl0[:, hi])).astype(BF16)
            p_ref[hi, lo] = jnp.zeros((HB, HB), BF16)
            ds_ref[hi, lo] = jnp.zeros((HB, HB), BF16)

        diagonal_products(s0, d0)
        products(1, s1, d1)
        diagonal_elementwise(s0, d0, p0, e0)

        def pair(t, carry):
            k = 2 * t
            trip(k + 2, s0, d0, k + 1, s1, d1, p1, e1, k, p0, e0)
            trip(k + 3, s1, d1, k + 2, s0, d0, p0, e0, k + 1, p1, e1)
            return carry

        n_pairs = n // 2
        lax.fori_loop(0, n_pairs - 1, pair, 0)
        k_last = 2 * (n_pairs - 1)

        @pl.when(jnp.logical_and(n_pairs >= 1, n % 2 == 0))
        def _():
            elementwise(k_last + 1, s1, d1, p1, e1, False)
            grads(k_last, p0, e0)
            grads(k_last + 1, p1, e1)

        @pl.when(jnp.logical_and(n_pairs >= 1, n % 2 == 1))
        def _():
            trip(k_last + 2, s0, d0, k_last + 1, s1, d1, p1, e1, k_last, p0, e0)
            elementwise(k_last + 2, s0, d0, p0, e0, False)
            grads(k_last + 1, p1, e1)
            grads(k_last + 2, p0, e0)

        @pl.when(n == 1)
        def _():
            grads(0, p0, e0)

        dk_ref[0] = dk_acc[0:DH, :]
        dfk_ref[0, 0] = dk_acc[DH:DH + 8, :]
        dv_ref[0] = dv_acc[...]

        @pl.when(jnp.logical_and(h == H - 1, step == NB - 1))
        def _():
            for copies in exchanges():
                _wait_all(copies)

    per_kv = lambda r: pl.BlockSpec((1, r, B), lambda h, t: (NB - 1 - t, h, 0))
    head_all = lambda r: pl.BlockSpec((NB, r, B), lambda h, j: (0, h, 0))
    aug = pl.BlockSpec((1, B, 128), lambda h, t: (h, NB - 1 - t, 0))
    stat = pl.BlockSpec((1, NB, 1, B), lambda h, j: (h, 0, 0, 0))
    return _pc(
        body, name="attn_bwd", grid=(H, NB),
        in_specs=[SMEM, SMEM, SMEM, head_all(128), aug, per_kv(KR), aug, head_all(128), stat, stat]
        + [ANY] * NG,
        out_specs=[head_all(KR), per_kv(DH), per_kv(DH),
                   pl.BlockSpec((1, 1, 8, B), lambda h, t: (h, NB - 1 - t, 0, 0))] + [ANY] * NG,
        out_shape=[_sds((NB, H * KR, B)), _sds((NB, A, B)), _sds((NB, A, B)), _sds((H, NB, 8, B))]
        + [_sds(g.shape, g.dtype) for g in early_g],
        scratch_shapes=[pltpu.VMEM((B, B), F32)] * 4 + [pltpu.VMEM((B, B), BF16)] * 4
        + [pltpu.VMEM((QR, B), F32), pltpu.VMEM((DH, B), F32)] + COMM_SEMS * NG,
        compiler_params=_cp(("arbitrary", "arbitrary")),
    )(fmax, fmin, thr, qTa, kaug, kT, vaug, doTa, lse, delta, *early_g)


def _conv_tail(u1, lng, lnb, beta_c):
    mu = _rowmean(u1)
    d = u1 - mu
    rstd = lax.rsqrt(_rowmean(d * d) + EPS)
    xhat = d * rstd
    u2 = xhat * lng + lnb
    sg = _sigmoid(u2)
    u3 = u2 * sg
    rc = lax.rsqrt(_rowmean(u3 * u3) + EPS)
    n3 = u3 * rc
    return rstd, xhat, u2, sg, rc, n3, n3 * beta_c


def _attn_tail(oT, beta_a_col):
    ra = lax.rsqrt(jnp.mean(oT * oT, axis=0, keepdims=True) + EPS)
    ohat = oT * ra
    return ra, ohat, ohat * beta_a_col


CONV_ROWS = 64
CORR_ROWS = 64


def _fill_shifted(buf, sh):
    rows = sh.shape[1]
    for ph in range(8):
        sh[ph] = buf[pl.ds(ph, rows), :]


def _tap_rows(sh, r0, o, rows):
    return sh[o % 8, pl.ds(pl.multiple_of(r0 + 8 * (o // 8), 8), rows), :]


def _depthwise(sh, w_ref, offs, bias, out_ref, B):
    def chunk(ci, carry):
        r0 = pl.multiple_of(ci * CONV_ROWS, CONV_ROWS)
        acc = jnp.broadcast_to(bias, (CONV_ROWS, CW))
        for k, o in enumerate(offs):
            acc = acc + w_ref[k:k + 1, :] * _tap_rows(sh, r0, o, CONV_ROWS)
        out_ref[pl.ds(r0, CONV_ROWS), :] = acc
        return carry

    lax.fori_loop(0, B // CONV_ROWS, chunk, 0)


def _tap_gradients(sh, d_ref, offs, dw_ref, B):
    for g0 in range(0, len(offs), 8):
        ks = list(range(g0, min(g0 + 8, len(offs))))

        def chunk(ci, accs, ks=ks):
            r0 = pl.multiple_of(ci * CORR_ROWS, CORR_ROWS)
            d = d_ref[pl.ds(r0, CORR_ROWS), :]
            out = []
            for a, k in zip(accs, ks):
                pr = d * _tap_rows(sh, r0, offs[k], CORR_ROWS)
                for r in range(0, CORR_ROWS, 8):
                    a = a + pr[r:r + 8]
                out.append(a)
            return tuple(out)

        accs = lax.fori_loop(0, B // CORR_ROWS, chunk, tuple(jnp.zeros((8, CW), F32) for _ in ks))
        for a, k in zip(accs, ks):
            dw_ref[k:k + 1, :] = dw_ref[k:k + 1, :] + _colsum(a)


def conv_merge_out(u0, x, oT, conv_w, conv_b, lng, lnb, beta_c, beta_a_col, wo, g1):
    S = x.shape[0]
    B = BLK
    NB = S // B

    def body(uc_ref, up_ref, x_ref, oT_ref, w_ref, cb_ref, lng_ref, lnb_ref, bc_ref, ba_ref, wo_ref,
             g1_ref, x1_ref, o_ref, u1_ref, ubuf, sh):
        i = pl.program_id(0)
        ubuf[0:HALO, :] = jnp.where(i > 0, up_ref[B - HALO:B, :], 0.0)
        ubuf[HALO:HALO + B, :] = uc_ref[...]
        ubuf[HALO + B:HALO + B + 8, :] = jnp.zeros((8, CW), F32)
        _fill_shifted(ubuf, sh)
        _depthwise(sh, w_ref, [HALO - (KC - 1) + k for k in range(KC)], cb_ref[...], u1_ref, B)
        mc = _conv_tail(u1_ref[...], lng_ref[...], lnb_ref[...], bc_ref[...])[-1]
        maT = _attn_tail(oT_ref[0], ba_ref[...])[-1]
        ma = maT.T
        o = _nn(ma.astype(BF16), wo_ref[0:A, :]) + _nn(mc.astype(BF16), wo_ref[A:D, :])
        o_ref[...] = o
        x1_ref[...] = x_ref[...] + g1_ref[...] * o

    row = lambda w: pl.BlockSpec((B, w), lambda i: (i, 0))
    return _pc(
        body, name="conv_merge_out", grid=(NB,),
        in_specs=[row(CW), pl.BlockSpec((B, CW), lambda i: (jnp.maximum(i - 1, 0), 0)), row(D),
                  pl.BlockSpec((1, A, B), lambda i: (i, 0, 0)), _const((32, CW)), _const((1, CW)),
                  _const((1, CW)), _const((1, CW)), _const((1, CW)), _const((A, 1)), _const((D, D)),
                  _const((1, D))],
        out_specs=[row(D), row(D), row(CW)],
        out_shape=[_sds((S, D)), _sds((S, D)), _sds((S, CW))],
        scratch_shapes=[pltpu.VMEM((B + HALO + 8, CW), F32), pltpu.VMEM((8, B + HALO, CW), F32)],
        compiler_params=_cp(("arbitrary",)),
    )(u0, u0, x, oT, conv_w, conv_b, lng, lnb, beta_c, beta_a_col, wo, g1)


def mlp_fwd_loss(x1, tgt, n2g, sc2, sh2, g2, w1s, w2):
    S = x1.shape[0]
    B = BLK
    NB = S // B
    NS, _, FS = w1s.shape

    def body(x1_ref, t_ref, g_ref, sc_ref, sh_ref, g2_ref, w1_hbm, w2_hbm, dy_ref, loss_ref, dg2_ref, ra_ref,
             ht_ref, w1_v, w2_v, sem):
        i = pl.program_id(0)

        @pl.when(i == 0)
        def _():
            for src, dst in ((w1_hbm, w1_v), (w2_hbm, w2_v)):
                cp = pltpu.make_async_copy(src, dst, sem)
                cp.start()
                cp.wait()
            loss_ref[...] = jnp.zeros_like(loss_ref)
            dg2_ref[...] = jnp.zeros_like(dg2_ref)

        x1 = x1_ref[...]
        h2 = _norm_mod(x1, g_ref[...], sc_ref[...], sh_ref[...])[2]
        hs = h2.astype(BF16)
        ht_ref[...] = h2.T.astype(BF16)
        y2 = jnp.zeros((B, D), F32)
        for q in range(NS):
            a = jnp.maximum(_nn(hs, w1_v[q]), 0.0)
            ra_ref[:, q * FS:(q + 1) * FS] = a.astype(BF16)
            y2 = y2 + _nn((a * a).astype(BF16), w2_v[q * FS:(q + 1) * FS, :])
        e = x1 + g2_ref[...] * y2 - t_ref[...]
        dy = e * (1.0 / D)
        dy_ref[...] = dy
        loss_ref[...] = loss_ref[...] + 0.5 * _colsum(_rowmean(e * e))
        dg2_ref[...] = dg2_ref[...] + _colsum(dy * y2)

    row = pl.BlockSpec((B, D), lambda i: (i, 0))
    return _pc(
        body, name="mlp_fwd_loss", grid=(NB,),
        in_specs=[row, row, _const((1, D)), _const((1, D)), _const((1, D)), _const((1, D)), ANY, ANY],
        out_specs=[row, _const((1, 1)), _const((1, D)), pl.BlockSpec((B, FF), lambda i: (i, 0)),
                   pl.BlockSpec((D, B), lambda i: (0, i))],
        out_shape=[_sds((S, D)), _sds((1, 1)), _sds((1, D)), _sds((S, FF), BF16), _sds((D, S), BF16)],
        scratch_shapes=[pltpu.VMEM(w1s.shape, BF16), pltpu.VMEM(w2.shape, BF16), pltpu.SemaphoreType.DMA(())],
        compiler_params=_cp(("arbitrary",)),
    )(x1, tgt, n2g, sc2, sh2, g2, w1s, w2)


def mlp_bwd(h2t, dy, ra, g2, w1s, w2):
    S = dy.shape[0]
    B = BLK
    NB = S // B
    NC = FF // FCH
    NS, _, FS = w1s.shape
    SPC = NS // NC

    def body(ht_ref, dy_ref, ra_ref, g2_ref, w1_ref, w2_ref, dw1_ref, dw2_ref, dh_ref):
        i = pl.program_id(1)

        @pl.when(i == 0)
        def _():
            dw1_ref[...] = jnp.zeros_like(dw1_ref)
            dw2_ref[...] = jnp.zeros_like(dw2_ref)

        h2t = ht_ref[...]
        ra = ra_ref[...].astype(F32)
        dyb = (dy_ref[...] * g2_ref[...]).astype(BF16)
        db = _nt(dyb, w2_ref[...])
        dab = (db * (2.0 * ra)).astype(BF16)
        dh = jnp.zeros((B, D), F32)
        for q in range(SPC):
            part = dab[:, q * FS:(q + 1) * FS]
            dh = dh + _nt(part, w1_ref[q])
            dw1_ref[q] = dw1_ref[q] + _nn(h2t, part)
        dh_ref[0] = dh
        dw2_ref[...] = dw2_ref[...] + _nn((ra * ra).T.astype(BF16), dyb)

    row = pl.BlockSpec((B, D), lambda c, i: (i, 0))
    vec = pl.BlockSpec((1, D), lambda c, i: (0, 0))
    wsh = pl.BlockSpec((SPC, D, FS), lambda c, i: (c, 0, 0))
    w2b = pl.BlockSpec((FCH, D), lambda c, i: (c, 0))
    return _pc(
        body, name="mlp_bwd", grid=(NC, NB),
        in_specs=[pl.BlockSpec((D, B), lambda c, i: (0, i)), row, pl.BlockSpec((B, FCH), lambda c, i: (i, c)),
                  vec, wsh, w2b],
        out_specs=[wsh, w2b, pl.BlockSpec((1, B, D), lambda c, i: (c, i, 0))],
        out_shape=[_sds(w1s.shape), _sds((FF, D)), _sds((NC, S, D))],
        compiler_params=_cp(("arbitrary", "arbitrary")),
    )(h2t, dy, ra, g2, w1s, w2)


def merge_bwd(dh2p, x1, dy, o, oT, u1, n2g, sc2, g1, lng, lnb, beta_c, beta_a_col, wo):
    S = x1.shape[0]
    B = BLK
    NB = S // B
    NC = dh2p.shape[0]

    def body(dh_ref, x1_ref, dy_ref, o_ref, oT_ref, u1_ref, g_ref, sc_ref, g1_ref, lng_ref, lnb_ref,
             bc_ref, ba_ref, wo_ref,
             dx1_ref, doTa_ref, dl_ref, du1_ref, dwo_ref, v1_ref, v2_ref, dba_ref):
        i = pl.program_id(0)

        @pl.when(i == 0)
        def _():
            dwo_ref[...] = jnp.zeros_like(dwo_ref)
            v1_ref[...] = jnp.zeros_like(v1_ref)
            v2_ref[...] = jnp.zeros_like(v2_ref)
            dba_ref[...] = jnp.zeros_like(dba_ref)

        dh2 = dh_ref[0]
        for cc in range(1, NC):
            dh2 = dh2 + dh_ref[cc]
        x1 = x1_ref[...]
        g = g_ref[...]
        sc = sc_ref[...]
        r2 = lax.rsqrt(_rowmean(x1 * x1) + EPS)
        xh = x1 * r2
        dhx = dh2 * xh
        v1_ref[0:1, :] = v1_ref[0:1, :] + _colsum(dh2)
        v1_ref[1:2, :] = v1_ref[1:2, :] + _colsum(dhx) * g
        v1_ref[2:3, :] = v1_ref[2:3, :] + _colsum(dhx) * (1.0 + sc)
        dxh = dh2 * (g * (1.0 + sc))
        dx1 = dy_ref[...] + r2 * (dxh - xh * _rowmean(dxh * xh))
        dx1_ref[...] = dx1
        v1_ref[3:4, :] = v1_ref[3:4, :] + _colsum(dx1 * o_ref[...])
        dob = (dx1 * g1_ref[...]).astype(BF16)

        lng = lng_ref[...]
        bc = bc_ref[...]
        rstd, xhat, u2, sg, rc, n3, mc = _conv_tail(u1_ref[...], lng, lnb_ref[...], bc)
        ba = ba_ref[...]
        oT = oT_ref[0]
        ra, ohat, maT = _attn_tail(oT, ba)
        dwo_ref[0:A, :] = dwo_ref[0:A, :] + _nn(maT.astype(BF16), dob)
        dwo_ref[A:D, :] = dwo_ref[A:D, :] + _nn(mc.T.astype(BF16), dob)
        dmaT = _nt(wo_ref[0:A, :], dob)
        dmc = _nt(dob, wo_ref[A:D, :])

        dba_ref[...] = dba_ref[...] + _lanesum(dmaT * ohat)
        dohat = dmaT * ba
        doT = ra * (dohat - ohat * jnp.mean(dohat * ohat, axis=0, keepdims=True))
        prod = doT * oT
        zpad = jnp.zeros((128 - DH, B), BF16)
        for hh in range(H):
            sl = slice(hh * DH, (hh + 1) * DH)
            dl_ref[hh, 0] = _colsum(prod[sl])
            doTa_ref[0, hh * 128:hh * 128 + DH, :] = doT[sl].astype(BF16)
            doTa_ref[0, hh * 128 + DH:(hh + 1) * 128, :] = zpad

        v2_ref[0:1, :] = v2_ref[0:1, :] + _colsum(dmc * n3)
        dn3 = dmc * bc
        du3 = rc * (dn3 - n3 * _rowmean(dn3 * n3))
        du2 = du3 * (sg * (1.0 + u2 * (1.0 - sg)))
        v2_ref[1:2, :] = v2_ref[1:2, :] + _colsum(du2 * xhat)
        v2_ref[2:3, :] = v2_ref[2:3, :] + _colsum(du2)
        dxhat = du2 * lng
        du1_ref[...] = rstd * (dxhat - _rowmean(dxhat) - xhat * _rowmean(dxhat * xhat))

    row = lambda w: pl.BlockSpec((B, w), lambda i: (i, 0))
    return _pc(
        body, name="merge_bwd", grid=(NB,),
        in_specs=[pl.BlockSpec((NC, B, D), lambda i: (0, i, 0)), row(D), row(D), row(D),
                  pl.BlockSpec((1, A, B), lambda i: (i, 0, 0)), row(CW), _const((1, D)), _const((1, D)),
                  _const((1, D)), _const((1, CW)), _const((1, CW)), _const((1, CW)), _const((A, 1)),
                  _const((D, D))],
        out_specs=[row(D), pl.BlockSpec((1, H * 128, B), lambda i: (i, 0, 0)),
                   pl.BlockSpec((H, 1, 1, B), lambda i: (0, i, 0, 0)), row(CW), _const((D, D)),
                   _const((8, D)), _const((8, CW)), _const((A, 1))],
        out_shape=[_sds((S, D)), _sds((NB, H * 128, B), BF16), _sds((H, NB, 1, B)), _sds((S, CW)),
                   _sds((D, D)), _sds((8, D)), _sds((8, CW)), _sds((A, 1))],
        compiler_params=_cp(("arbitrary",)),
    )(dh2p, x1, dy, o, oT, u1, n2g, sc2, g1, lng, lnb, beta_c, beta_a_col, wo)


def head_bwd(du1, u0, alg, zqk, fgT, dqT, dkT, dvT, dfk, conv_w, qg_col, kg_col, bf_col, tri_lo):
    S = u0.shape[0]
    B = BLK
    NB = S // B

    def body(dc_ref, dn_ref, uc_ref, up_ref, alg_ref, zqk_ref, fg_ref, dq_ref, dk_ref, dv_ref, dfk_ref,
             w_ref, qg_ref, kg_ref, bf_ref, tri_ref,
             dzr_ref, dzT_ref, dcw_ref, vc_ref, dqg_ref, dkg_ref, dbf_ref,
             buf, sh, du0_ref, carry, fbuf):
        pid = pl.program_id(0)
        ri = NB - 1 - pid

        @pl.when(pid == 0)
        def _():
            carry[...] = jnp.zeros_like(carry)
            dcw_ref[...] = jnp.zeros_like(dcw_ref)
            vc_ref[...] = jnp.zeros_like(vc_ref)
            dqg_ref[...] = jnp.zeros_like(dqg_ref)
            dkg_ref[...] = jnp.zeros_like(dkg_ref)
            dbf_ref[...] = jnp.zeros_like(dbf_ref)

        zero8 = jnp.zeros((8, CW), F32)
        buf[0:B, :] = dc_ref[...]
        buf[B:B + HALO, :] = jnp.where(ri < NB - 1, dn_ref[0:HALO, :], 0.0)
        buf[B + HALO:B + HALO + 8, :] = zero8
        _fill_shifted(buf, sh)
        _depthwise(sh, w_ref, [KC - 1 - k for k in range(KC)], jnp.zeros((1, CW), F32), du0_ref, B)
        buf[0:HALO, :] = jnp.where(ri > 0, up_ref[B - HALO:B, :], 0.0)
        buf[HALO:HALO + B, :] = uc_ref[...]
        buf[HALO + B:HALO + B + 8, :] = zero8
        _fill_shifted(buf, sh)
        _tap_gradients(sh, dc_ref, [HALO - (KC - 1) + k for k in range(KC)], dcw_ref, B)
        vc_ref[0:1, :] = vc_ref[0:1, :] + _colsum(dc_ref[...])

        du0 = du0_ref[...]
        al = alg_ref[:, 0:CW]
        sg = _sigmoid(alg_ref[:, CW:2 * CW])
        dzr_ref[:, 0:CW] = (du0 * sg).astype(BF16)
        dzr_ref[:, CW:2 * CW] = (du0 * al * sg * (1.0 - sg)).astype(BF16)

        dqg = jnp.zeros((DH, B), F32)
        dkg = jnp.zeros((DH, B), F32)
        qg = qg_ref[...]
        kg = kg_ref[...]
        for hh in range(H):
            sl = slice(hh * DH, (hh + 1) * DH)
            q = zqk_ref[sl, :]
            rq = lax.rsqrt(jnp.mean(q * q, axis=0, keepdims=True) + EPS)
            qn = q * rq
            dqh = dq_ref[0, hh * KR:hh * KR + DH, :] * 0.125
            dqg = dqg + dqh * qn
            dqn = dqh * qg
            dzT_ref[0, sl, :] = (rq * (dqn - qn * jnp.mean(dqn * qn, axis=0, keepdims=True))).astype(BF16)
            k = zqk_ref[A + hh * DH:A + (hh + 1) * DH, :]
            rk = lax.rsqrt(jnp.mean(k * k, axis=0, keepdims=True) + EPS)
            kn = k * rk
            dkh = dk_ref[0, sl, :] * (1.0 / LOG2E)
            dkg = dkg + dkh * kn
            dkn = dkh * kg
            dzT_ref[0, A + hh * DH:A + (hh + 1) * DH, :] = (
                rk * (dkn - kn * jnp.mean(dkn * kn, axis=0, keepdims=True))).astype(BF16)
            fbuf[hh:hh + 1, :] = dq_ref[0, hh * KR + DH:hh * KR + DH + 1, :] - dfk_ref[hh, 0, 0:1, :]
        dqg_ref[...] = dqg_ref[...] + _lanesum(dqg)
        dkg_ref[...] = dkg_ref[...] + _lanesum(dkg)
        dzT_ref[0, 2 * A:3 * A, :] = dv_ref[0].astype(BF16)

        dF = fbuf[...]
        a1, a2, a3 = _split3(dF)
        tr = tri_ref[...]
        dlogf = _nn(a1, tr) + _nn(a2, tr) + _nn(a3, tr) + carry[...]
        carry[...] = carry[...] + _lanesum(dF)
        dfg = dlogf * _sigmoid(-(fg_ref[...] + bf_ref[...]))
        dbf_ref[...] = dbf_ref[...] + _lanesum(dfg)
        dzT_ref[0, 3 * A:ZT_ROWS, :] = jnp.concatenate([dfg, jnp.zeros((8, B), F32)], axis=0).astype(BF16)

    rrow = lambda w: pl.BlockSpec((B, w), lambda p: (NB - 1 - p, 0))
    rts = lambda r: pl.BlockSpec((1, r, B), lambda p: (NB - 1 - p, 0, 0))
    return _pc(
        body, name="head_bwd", grid=(NB,),
        in_specs=[rrow(CW), pl.BlockSpec((B, CW), lambda p: (jnp.minimum(NB - p, NB - 1), 0)),
                  rrow(CW), pl.BlockSpec((B, CW), lambda p: (jnp.maximum(NB - 2 - p, 0), 0)),
                  rrow(2 * CW), pl.BlockSpec((2 * A, B), lambda p: (0, NB - 1 - p)),
                  pl.BlockSpec((8, B), lambda p: (0, NB - 1 - p)), rts(H * KR), rts(A), rts(A),
                  pl.BlockSpec((H, 1, 8, B), lambda p: (0, NB - 1 - p, 0, 0)),
                  _const((32, CW)), _const((DH, 1)), _const((DH, 1)), _const((8, 1)), _const((B, B))],
        out_specs=[rrow(2 * CW), rts(ZT_ROWS), _const((32, CW)), _const((8, CW)), _const((DH, 1)),
                   _const((DH, 1)), _const((8, 1))],
        out_shape=[_sds((S, 2 * CW), BF16), _sds((NB, ZT_ROWS, B), BF16), _sds((32, CW)), _sds((8, CW)),
                   _sds((DH, 1)), _sds((DH, 1)), _sds((8, 1))],
        scratch_shapes=[pltpu.VMEM((B + HALO + 8, CW), F32), pltpu.VMEM((8, B + HALO, CW), F32),
                        pltpu.VMEM((B, CW), F32), pltpu.VMEM((8, 1), F32), pltpu.VMEM((8, B), F32)],
        compiler_params=_cp(("arbitrary",)),
    )(du1, du1, u0, u0, alg, zqk, fgT, dqT, dkT, dvT, dfk, conv_w, qg_col, kg_col, bf_col, tri_lo)


def in_bwd_w(x, dzr, dzT, n1g, sc1, sh1):
    S = x.shape[0]
    B = BLK
    NB = S // B

    def body(x_ref, dzr_ref, dzT_ref, g_ref, sc_ref, sh_ref, dwT_hbm, dwr_hbm, dwT_acc, dwr_acc, sem):
        i = pl.program_id(0)

        @pl.when(i == 0)
        def _():
            dwT_acc[...] = jnp.zeros_like(dwT_acc)
            dwr_acc[...] = jnp.zeros_like(dwr_acc)

        h1 = _norm_mod(x_ref[...], g_ref[...], sc_ref[...], sh_ref[...])[2]
        dwT_acc[...] = dwT_acc[...] + _nn(dzT_ref[0], h1.astype(BF16))
        dwr_acc[...] = dwr_acc[...] + _nn(h1.T.astype(BF16), dzr_ref[...])

        @pl.when(i == NB - 1)
        def _():
            for src, dst in ((dwT_acc, dwT_hbm), (dwr_acc, dwr_hbm)):
                cp = pltpu.make_async_copy(src, dst, sem)
                cp.start()
                cp.wait()

    row = lambda w: pl.BlockSpec((B, w), lambda i: (i, 0))
    return _pc(
        body, name="in_bwd_w", grid=(NB,),
        in_specs=[row(D), row(2 * CW), pl.BlockSpec((1, ZT_ROWS, B), lambda i: (i, 0, 0)),
                  _const((1, D)), _const((1, D)), _const((1, D))],
        out_specs=[ANY, ANY],
        out_shape=[_sds((ZT_ROWS, D)), _sds((D, 2 * CW))],
        scratch_shapes=[pltpu.VMEM((ZT_ROWS, D), F32), pltpu.VMEM((D, 2 * CW), F32),
                        pltpu.SemaphoreType.DMA(())],
        compiler_params=_cp(("arbitrary",)),
    )(x, dzr, dzT, n1g, sc1, sh1)


def in_bwd_x(x, dx1, dzr, dzT, n1g, sc1, sh1, w_qkvf, wr, late_g):
    S = x.shape[0]
    B = BLK
    NB = S // B

    def body(x_ref, dx1_ref, dzr_ref, dzT_ref, g_ref, sc_ref, sh_ref, w_hbm, wr_hbm, lg_ref,
             gx_ref, v_ref, recv_ref, w_v, wr_v, sem, send_sems, recv_sems, local_sem):
        i = pl.program_id(0)

        @pl.when(i == 0)
        def _():
            _start_all(_direct_copies(True, lg_ref, recv_ref, send_sems, recv_sems, local_sem))
            for src, dst in ((w_hbm, w_v), (wr_hbm, wr_v)):
                cp = pltpu.make_async_copy(src, dst, sem)
                cp.start()
                cp.wait()
            v_ref[...] = jnp.zeros_like(v_ref)

        g = g_ref[...]
        sc = sc_ref[...]
        r1, xh, _ = _norm_mod(x_ref[...], g, sc, sh_ref[...])
        dh1 = _nt(dzr_ref[...], wr_v[...]) + _nn(w_v[...], dzT_ref[0]).T
        dhx = dh1 * xh
        v_ref[0:1, :] = v_ref[0:1, :] + _colsum(dh1)
        v_ref[1:2, :] = v_ref[1:2, :] + _colsum(dhx) * g
        v_ref[2:3, :] = v_ref[2:3, :] + _colsum(dhx) * (1.0 + sc)
        dxh = dh1 * (g * (1.0 + sc))
        gx_ref[...] = dx1_ref[...] + r1 * (dxh - xh * _rowmean(dxh * xh))

        @pl.when(i == NB - 1)
        def _():
            _wait_all(_direct_copies(True, lg_ref, recv_ref, send_sems, recv_sems, local_sem))

    row = lambda w: pl.BlockSpec((B, w), lambda i: (i, 0))
    return _pc(
        body, name="in_bwd_x", grid=(NB,),
        in_specs=[row(D), row(D), row(2 * CW), pl.BlockSpec((1, ZT_ROWS, B), lambda i: (i, 0, 0)),
                  _const((1, D)), _const((1, D)), _const((1, D)), ANY, ANY, ANY],
        out_specs=[row(D), _const((8, D)), ANY],
        out_shape=[_sds((S, D)), _sds((8, D)), _sds(late_g.shape, late_g.dtype)],
        scratch_shapes=[pltpu.VMEM((D, ZT_ROWS), BF16), pltpu.VMEM((D, 2 * CW), BF16),
                        pltpu.SemaphoreType.DMA(())] + COMM_SEMS,
        compiler_params=_cp(("arbitrary",)),
    )(x, dx1, dzr, dzT, n1g, sc1, sh1, w_qkvf, wr, late_g)


def _adam_math(g, w, m, v):
    m = ADAM_B1 * m + (1.0 - ADAM_B1) * g
    v = ADAM_B2 * v + (1.0 - ADAM_B2) * (g * g)
    m_hat = m / (1.0 - ADAM_B1 ** ADAM_STEP)
    v_hat = v / (1.0 - ADAM_B2 ** ADAM_STEP)
    delta = -ADAM_LR * (m_hat / (jnp.sqrt(v_hat) + ADAM_EPS) + ADAM_WD * w)
    return delta, m, v


def _row_tile(R):
    for t in (1024, 512, 256, 128, 64, 32, 16, 8):
        if R % t == 0:
            return t
    return R


def sum_slots(parts, name):
    K, R, C = parts.shape
    T = _row_tile(R)

    def body(p_ref, o_ref):
        s = p_ref[0]
        for k in range(1, K):
            s = s + p_ref[k]
        o_ref[...] = s

    return _pc(body, name=name, grid=(R // T,),
               in_specs=[pl.BlockSpec((K, T, C), lambda i: (0, i, 0))],
               out_specs=pl.BlockSpec((T, C), lambda i: (i, 0)), out_shape=_sds((R, C)),
               compiler_params=_cp(("arbitrary",)))(parts)


def adamw_slots(parts, w, m, v, name):
    K, R, C = parts.shape
    T = R
    while K * T * C * 4 > (8 << 20) and T % 16 == 0:
        T //= 2

    def body(p_ref, w_ref, m_ref, v_ref, g_ref, d_ref, nm_ref, nv_ref):
        g = p_ref[0].astype(F32)
        for k in range(1, K):
            g = g + p_ref[k].astype(F32)
        g_ref[...] = g
        d_ref[...], nm_ref[...], nv_ref[...] = _adam_math(g, w_ref[...], m_ref[...], v_ref[...])

    t2 = pl.BlockSpec((T, C), lambda i: (i, 0))
    return _pc(body, name=name, grid=(R // T,),
               in_specs=[pl.BlockSpec((K, T, C), lambda i: (0, i, 0)), t2, t2, t2],
               out_specs=[t2, t2, t2, t2], out_shape=[_sds((R, C))] * 4,
               compiler_params=_cp(("arbitrary",)))(parts, w, m, v)


def adamw_many(gs, ws, ms, vs, name):
    n = len(ws)

    def body(*refs):
        outs = refs[4 * n:]
        for q in range(n):
            d, nm, nv = _adam_math(refs[q][...], refs[n + q][...], refs[2 * n + q][...], refs[3 * n + q][...])
            outs[q][...] = d
            outs[n + q][...] = nm
            outs[2 * n + q][...] = nv

    res = _pc(body, name=name, grid=(1,),
              in_specs=[_const(a.shape) for a in list(gs) + list(ws) + list(ms) + list(vs)],
              out_specs=[_const(w.shape) for w in ws] * 3, out_shape=[_sds(w.shape) for w in ws] * 3,
              compiler_params=_cp())(*gs, *ws, *ms, *vs)
    return res[0:n], res[n:2 * n], res[2 * n:3 * n]


def ada_grad_adamw(cT, dmod_cols, w, m, v):
    NCOL = w.shape[1]

    def body(cT_ref, dm_ref, w_ref, m_ref, v_ref, g_ref, d_ref, nm_ref, nv_ref):
        def term(b):
            cv = cT_ref[b]
            return (cv * _sigmoid(cv)) * dm_ref[b:b + 1, :]

        g = term(0)
        for b in range(1, N_DEV):
            g = g + term(b)
        g_ref[...] = g
        d_ref[...], nm_ref[...], nv_ref[...] = _adam_math(g, w_ref[...], m_ref[...], v_ref[...])

    full = _const((D, NCOL))
    return _pc(body, name="ada_grad_adamw", grid=(1,),
               in_specs=[_const((N_DEV, D, 1)), _const((N_DEV, NCOL)), full, full, full],
               out_specs=[full, full, full, full], out_shape=[_sds((D, NCOL))] * 4,
               compiler_params=_cp())(cT, dmod_cols, w, m, v)


def _pack_rows(vecs, rows=None):
    flat = jnp.concatenate([jnp.ravel(v) for v in vecs])
    n = flat.shape[0]
    r = -(-n // 1024) * 8 if rows is None else rows
    return jnp.pad(flat, (0, r * 128 - n)).reshape(r, 128)


def _unpack_rows(packed, shapes):
    flat = packed.reshape(-1)
    out, off = [], 0
    for s in shapes:
        n = 1
        for d in s:
            n *= d
        out.append(flat[off:off + n].reshape(s))
        off += n
    return out


def _cols_to_shards(w, cols):
    rows = w.shape[0]
    return w.reshape(rows, N_DEV, cols).transpose(1, 0, 2).reshape(N_DEV, rows * cols)


def kernel(x, c, w_ada, b_ada, norm1_g, w_in, q_norm_g, k_norm_g, b_f, conv_w, conv_b, conv_ln_g, conv_ln_b, beta_attn, beta_conv, w_out, norm2_g, w_ff1, w_ff2, loss_target, m_w_ada, m_b_ada, m_norm1_g, m_w_in, m_q_norm_g, m_k_norm_g, m_b_f, m_conv_w, m_conv_b, m_conv_ln_g, m_conv_ln_b, m_beta_attn, m_beta_conv, m_w_out, m_norm2_g, m_w_ff1, m_w_ff2, v_w_ada, v_b_ada, v_norm1_g, v_w_in, v_q_norm_g, v_k_norm_g, v_b_f, v_conv_w, v_conv_b, v_conv_ln_g, v_conv_ln_b, v_beta_attn, v_beta_conv, v_w_out, v_norm2_g, v_w_ff1, v_w_ff2):
    S = x.shape[1]
    B = BLK
    NB = S // B
    me = 4 * lax.axis_index("x") + 2 * lax.axis_index("y") + lax.axis_index("c")
    xs = x[0]
    tgt = loss_target[0]

    ADA_C, IN_C, FF_C, CV_C = w_ada.shape[2], w_in.shape[2], w_ff1.shape[2], conv_w.shape[2]
    OUT_R, FF_R = w_out.shape[1], w_ff2.shape[1]

    g_in, g_c, g_cw = gather_two_level([w_in[0].T.astype(BF16), c, conv_w[0]], "ag_weights_early")
    late_w = [w_out[0].astype(BF16), w_ff1[0].astype(BF16), w_ff2[0].astype(BF16)]

    w_in_t = g_in.reshape(N_DEV * IN_C, D)
    c_all = g_c.reshape(N_DEV, D)
    conv_w_full = g_cw.transpose(1, 0, 2).reshape(KC, N_DEV * CV_C)
    conv_w_pad = jnp.pad(conv_w_full, ((0, 32 - KC), (0, 0)))

    wT = jnp.pad(w_in_t[0:3 * A + H], ((0, ZT_ROWS - 3 * A - H), (0, 0)))
    w_qkvf = wT.T
    wr = w_in_t[3 * A + H:].T

    qg_col = q_norm_g.reshape(DH, 1)
    kg_col = k_norm_g.reshape(DH, 1)
    bf_col = b_f.reshape(H, 1)
    beta_a_col = beta_attn.reshape(A, 1)
    ii = lax.broadcasted_iota(jnp.int32, (B, B), 0)
    jj = lax.broadcasted_iota(jnp.int32, (B, B), 1)
    tri_up = (ii <= jj).astype(BF16)
    tri_lo = (ii >= jj).astype(BF16)

    modc = mod_columns(c_all, w_ada[0], lax.dynamic_slice(b_ada, (0, me * ADA_C), (1, ADA_C)))
    gm, = gather_direct([modc], "ag_mod")
    mod = lax.dynamic_index_in_dim(gm, me, axis=1, keepdims=False).reshape(1, N_DEV * ADA_C)
    sh1, sc1, g1, sh2, sc2, g2 = [mod[:, k * D:(k + 1) * D] for k in range(6)]
    qTa, kT, kaug, vT, vaug, zqk, fgT, alg, u0, stat = fwd_in(xs, norm1_g, sc1, sh1, wT, wr, qg_col,
                                                             kg_col, bf_col, tri_up)
    fmax = stat[:, :, 0].T
    fmin = stat[:, :, 1].T
    qk_max = jnp.sqrt(jnp.max(stat[:, :, 2], axis=0) * jnp.max(stat[:, :, 3], axis=0))
    thr = -(PRUNE + (2.02 / LOG2E) * qk_max)
    oT, lse, g_wo, w1_shards, g_w2 = attn_fwd(fmax, fmin, thr, qTa, kaug, vT, late_w)
    wo_full = g_wo.reshape(D, D)
    w2_full = g_w2.reshape(FF, D)
    x1, o, u1 = conv_merge_out(u0, xs, oT, conv_w_pad, conv_b, conv_ln_g, conv_ln_b, beta_conv,
                               beta_a_col, wo_full, g1)
    dy, loss_part, dg2, ra, h2t = mlp_fwd_loss(x1, tgt, norm2_g, sc2, sh2, g2, w1_shards, w2_full)

    dw1s, dw2, dh2p = mlp_bwd(h2t, dy, ra, g2, w1_shards, w2_full)
    dx1, doTa, delta, du1, dwo, v1, v2, dba = merge_bwd(dh2p, x1, dy, o, oT, u1, norm2_g, sc2, g1,
                                                         conv_ln_g, conv_ln_b, beta_conv, beta_a_col, wo_full)
    early_g = [dwo.reshape(N_DEV, OUT_R, D), dw1s, dw2.reshape(N_DEV, FF_R, D)]
    dqT, dkT, dvT, dfk, r_out, r_f1, r_f2 = attn_bwd(fmax, fmin, thr, qTa, kaug, kT, vaug, doTa, lse,
                                                     delta, early_g)
    dzr, dzT, dcw, vc, dqg, dkg, dbf = head_bwd(du1, u0, alg, zqk, fgT, dqT, dkT, dvT, dfk, conv_w_pad,
                                                qg_col, kg_col, bf_col, tri_lo)
    dwT, dwr = in_bwd_w(xs, dzr, dzT, norm1_g, sc1, sh1)

    dw_in = jnp.concatenate([dwT.T[:, 0:3 * A + H], dwr], axis=1)
    late_g = _cols_to_shards(dw_in, IN_C)
    rows_b = -(-late_g.shape[1] // (128 * 256)) * 256
    late_g = jnp.pad(late_g, ((0, 0), (0, rows_b * 128 - late_g.shape[1]))).astype(BF16)
    grad_x, v0, recv_b = in_bwd_x(xs, dx1, dzr, dzT, norm1_g, sc1, sh1, w_qkvf, wr,
                                  late_g.reshape(N_DEV, rows_b, 128))

    dmod = jnp.concatenate([v0[0], v0[1], v1[3], v1[0], v1[1], dg2[0]])
    small1 = _pack_rows([dmod, v0[2], v1[2], dcw, vc[0], v2[1], v2[2], v2[0], dba, dqg, dkg,
                         jnp.pad(dbf.reshape(-1), (0, 120)), jnp.pad(loss_part.reshape(-1), (0, 127))])
    gs1, = gather_direct([small1], "ag_small_bwd")
    tot = sum_slots(gs1, "sum_small")
    (g_b_ada, g_n1, g_n2, g_cw, g_cb, g_lng, g_lnb, g_bc, g_ba, g_qg, g_kg, g_bf, loss_v) = _unpack_rows(
        tot, [(1, 6 * D), (1, D), (1, D), (32, CW), (1, CW), (1, CW), (1, CW), (1, CW), (1, A),
              (1, DH), (1, DH), (1, 128), (1, 128)])
    loss = loss_v[0, 0]
    g_bf = g_bf[:, 0:H]
    g_cw_mine = lax.dynamic_slice(g_cw[0:KC], (0, me * CV_C), (KC, CV_C)).reshape(1, KC, CV_C)

    small_names = [(b_ada, m_b_ada, v_b_ada, g_b_ada), (norm1_g, m_norm1_g, v_norm1_g, g_n1),
                   (q_norm_g, m_q_norm_g, v_q_norm_g, g_qg), (k_norm_g, m_k_norm_g, v_k_norm_g, g_kg),
                   (b_f, m_b_f, v_b_f, g_bf), (conv_w, m_conv_w, v_conv_w, g_cw_mine),
                   (conv_b, m_conv_b, v_conv_b, g_cb), (conv_ln_g, m_conv_ln_g, v_conv_ln_g, g_lng),
                   (conv_ln_b, m_conv_ln_b, v_conv_ln_b, g_lnb), (beta_attn, m_beta_attn, v_beta_attn, g_ba),
                   (beta_conv, m_beta_conv, v_beta_conv, g_bc), (norm2_g, m_norm2_g, v_norm2_g, g_n2)]
    sgs = [t[3].reshape(t[0].shape) for t in small_names]
    sds, sms, svs = adamw_many(sgs, *[[t[k] for t in small_names] for k in range(3)], "adamw_small")

    dmod_all = gs1[:, 0:48, :].reshape(N_DEV, N_DEV, ADA_C)
    dmod_cols = lax.dynamic_index_in_dim(dmod_all, me, axis=1, keepdims=False)
    ga, da, ma_, va_ = ada_grad_adamw(c_all.reshape(N_DEV, D, 1), dmod_cols, w_ada[0], m_w_ada[0], v_w_ada[0])

    res_b = adamw_slots(recv_b, *[_pack_rows(ws, rows=rows_b) for ws in ([w_in], [m_w_in], [v_w_in])],
                        "adamw_in")
    late = []
    for nm, parts, (w_, m_, v_) in (("adamw_out", r_out, (w_out, m_w_out, v_w_out)),
                                    ("adamw_ff1", r_f1, (w_ff1, m_w_ff1, v_w_ff1)),
                                    ("adamw_ff2", r_f2, (w_ff2, m_w_ff2, v_w_ff2))):
        late.append([r[None] for r in adamw_slots(parts, w_[0], m_[0], v_[0], nm)])
    bgs, bds, bms, bvs = [_unpack_rows(res_b[q], [(1, D, IN_C)]) + [late[0][q], late[1][q], late[2][q]]
                          for q in range(4)]

    def assemble(small, ada, bigs):
        (b_ada_, n1_, qg_, kg_, bf_, cw_, cb_, lng_, lnb_, ba_, bc_, n2_) = small
        return [ada.reshape(1, D, ADA_C), b_ada_, n1_, bigs[0], qg_, kg_, bf_, cw_, cb_, lng_, lnb_, ba_, bc_,
                bigs[1], n2_, bigs[2], bigs[3]]

    return (loss, grad_x.reshape(1, S, D), *assemble(sgs, ga, bgs), *assemble(sds, da, bds),
            *assemble(sms, ma_, bms), *assemble(svs, va_, bvs))
```

```python
import functools

import jax
import jax.numpy as jnp
from jax import lax
from jax.experimental import pallas as pl
from jax.experimental.pallas import tpu as pltpu

F32 = jnp.float32
BF16 = jnp.bfloat16

D = 1024
A = 512
CW = 512
H = 8
DH = 64
FF = 4096
KC = 31
HALO = 32
KR = 80
ZT_ROWS = 1552
FCH = 1024
EPS = 1e-6
BLK = 512
N_DEV = 8
VMEM_LIMIT = 56 * 1024 * 1024

ADAM_LR = 0.001
ADAM_B1 = 0.9
ADAM_B2 = 0.999
ADAM_EPS = 1e-08
ADAM_WD = 0.01
ADAM_STEP = 10

MESH = pl.DeviceIdType.MESH
ANY = pl.BlockSpec(memory_space=pl.ANY)
SMEM = pl.BlockSpec(memory_space=pltpu.SMEM)
PRUNE = 105.0
LOG2E = 1.4426950408889634


def _pc(body, **kw):
    return pl.pallas_call(body, **kw)


def _cp(sem=None):
    return pltpu.CompilerParams(dimension_semantics=sem, vmem_limit_bytes=VMEM_LIMIT)


def _sds(shape, dtype=F32):
    return jax.ShapeDtypeStruct(shape, dtype)


def _const(shape):
    n = len(shape)
    return pl.BlockSpec(shape, lambda *a: (0,) * n)


def _nt(a, b):
    return lax.dot_general(a, b, (((1,), (1,)), ((), ())), preferred_element_type=F32)


def _nn(a, b):
    return jnp.dot(a, b, preferred_element_type=F32)


def _sigmoid(v):
    return 1.0 / (1.0 + jnp.exp(-v))


def _split3(v):
    a1 = v.astype(BF16)
    r1 = v - a1.astype(F32)
    a2 = r1.astype(BF16)
    a3 = (r1 - a2.astype(F32)).astype(BF16)
    return a1, a2, a3


def _rowmean(v):
    return jnp.mean(v, axis=-1, keepdims=True)


def _colsum(v):
    return jnp.sum(v, axis=0, keepdims=True)


def _lanesum(v):
    return jnp.sum(v, axis=-1, keepdims=True)


def _coords():
    return lax.axis_index("x"), lax.axis_index("y"), lax.axis_index("c")


def _direct_copies(scatter, src_ref, dst_ref, send_sems, recv_sems, local_sem):
    mx, my, mc = _coords()
    me = 4 * mx + 2 * my + mc
    if scatter:
        local = pltpu.make_async_copy(src_ref.at[me], dst_ref.at[0], local_sem)
    else:
        local = pltpu.make_async_copy(src_ref, dst_ref.at[me], local_sem)
    remote = []
    for r in range(1, N_DEV):
        px = 1 - mx if r & 4 else mx
        py = 1 - my if r & 2 else my
        pcc = 1 - mc if r & 1 else mc
        remote.append(pltpu.make_async_remote_copy(
            src_ref=src_ref.at[4 * px + 2 * py + pcc] if scatter else src_ref,
            dst_ref=dst_ref.at[r] if scatter else dst_ref.at[me],
            send_sem=send_sems.at[r - 1], recv_sem=recv_sems.at[r - 1],
            device_id=(px, py, pcc), device_id_type=MESH))
    return [local] + remote


def _start_all(copies):
    for cp in copies:
        cp.start()


def _wait_all(copies):
    for cp in copies[1:]:
        cp.wait()
    copies[0].wait()


COMM_SEMS = [pltpu.SemaphoreType.DMA((7,)), pltpu.SemaphoreType.DMA((7,)), pltpu.SemaphoreType.DMA(())]


def gather_direct(arrays, name):
    NG = len(arrays)

    def body(*refs):
        sems = refs[2 * NG:]
        sets = [_direct_copies(False, refs[q], refs[NG + q], *sems[3 * q:3 * q + 3]) for q in range(NG)]
        for copies in sets:
            _start_all(copies)
        for copies in sets:
            _wait_all(copies)

    return _pc(body, name=name, out_shape=[_sds((N_DEV,) + a.shape, a.dtype) for a in arrays],
               in_specs=[ANY] * NG, out_specs=[ANY] * NG, scratch_shapes=COMM_SEMS * NG)(*arrays)


def gather_two_level(arrays, name):
    NG = len(arrays)

    def body(*refs):
        mx, my, mc = _coords()
        me, sibling = (mx, my, mc), (mx, my, 1 - mc)
        chips = [(1 - mx, my), (mx, 1 - my), (1 - mx, 1 - my)]
        plans = []
        for q in range(NG):
            x_ref, out_ref = refs[q], refs[NG + q]
            send_sems, recv_sems, local_sem = refs[2 * NG + 3 * q:2 * NG + 3 * q + 3]

            def rows(block, out_ref=out_ref):
                return out_ref.at[4 * block[0] + 2 * block[1] + block[2]]

            def copy(k, block, to, src=None, rows=rows, send_sems=send_sems, recv_sems=recv_sems):
                return pltpu.make_async_remote_copy(
                    src_ref=rows(block) if src is None else src, dst_ref=rows(block),
                    send_sem=send_sems.at[k], recv_sem=recv_sems.at[k], device_id=to, device_id_type=MESH)

            mine = pltpu.make_async_copy(x_ref, rows(me), local_sem)
            first = [copy(0, me, sibling, src=x_ref)]
            first += [copy(1 + t, me, (*chip, mc), src=x_ref) for t, chip in enumerate(chips)]
            passed = [copy(4 + t, (*chip, mc), sibling) for t, chip in enumerate(chips)]
            _start_all([mine] + first)
            plans.append((copy, mine, first, passed))
        for copy, mine, first, passed in plans:
            for t, chip in enumerate(chips):
                copy(1 + t, (*chip, mc), me).wait_recv()
                passed[t].start()
        for copy, mine, first, passed in plans:
            copy(0, sibling, me).wait_recv()
            for t, chip in enumerate(chips):
                copy(4 + t, (*chip, 1 - mc), me).wait_recv()
            for cp in first + passed:
                cp.wait_send()
            mine.wait()

    return _pc(body, name=name, out_shape=[_sds((N_DEV,) + a.shape, a.dtype) for a in arrays],
               in_specs=[ANY] * NG, out_specs=[ANY] * NG, scratch_shapes=COMM_SEMS * NG)(*arrays)


def mod_columns(c_all, w_shard, b_cols):
    NCOL = w_shard.shape[1]

    def body(c_ref, w_ref, b_ref, o_ref):
        cv = c_ref[...]
        o_ref[...] = _nn((cv * _sigmoid(cv)).astype(BF16), w_ref[...].astype(BF16)) + b_ref[...]

    return _pc(body, name="mod_columns", out_shape=_sds((N_DEV, NCOL)),
               in_specs=[_const((N_DEV, D)), _const((D, NCOL)), _const((1, NCOL))],
               out_specs=_const((N_DEV, NCOL)), grid=(1,), compiler_params=_cp())(c_all, w_shard, b_cols)


def _norm_mod(xv, g, sc, sh):
    r = lax.rsqrt(_rowmean(xv * xv) + EPS)
    xh = xv * r
    return r, xh, xh * (g * (1.0 + sc)) + sh


def _log_sigmoid(v):
    e = jnp.exp(-jnp.abs(v))
    l1p = jnp.where(e < 1e-4, e * (1.0 - 0.5 * e), jnp.log(1.0 + e))
    return jnp.minimum(v, 0.0) - l1p


def fwd_in(x, n1g, sc1, sh1, wT, wr, qg_col, kg_col, bf_col, tri):
    S = x.shape[0]
    B = BLK
    NB = S // B

    def body(x_ref, g_ref, sc_ref, sh_ref, wT_ref, wr_ref, qg_ref, kg_ref, bf_ref, tri_ref,
             qTa_ref, kT_ref, kaug_ref, vT_ref, vaug_ref, zqk_ref, fgT_ref, alg_ref, u0_ref, stat_ref,
             carry, tbuf, fs, nq, nk):
        i = pl.program_id(0)

        @pl.when(i == 0)
        def _():
            carry[...] = jnp.zeros_like(carry)

        _, _, h = _norm_mod(x_ref[...], g_ref[...], sc_ref[...], sh_ref[...])
        hb = h.astype(BF16)
        zT = _nt(wT_ref[...], hb)
        zr = _nn(hb, wr_ref[...])
        zqk_ref[...] = zT[0:2 * A]
        fgT = zT[3 * A:3 * A + 8]
        fgT_ref[...] = fgT
        alg_ref[...] = zr
        u0_ref[...] = zr[:, 0:CW] * _sigmoid(zr[:, CW:2 * CW])

        logf = _log_sigmoid(fgT + bf_ref[...])
        a1, a2, a3 = _split3(logf)
        tr = tri_ref[...]
        F = _nn(a1, tr) + _nn(a2, tr) + _nn(a3, tr) + carry[...]
        carry[...] = carry[...] + _lanesum(logf)
        p1, p2, p3 = _split3(F * LOG2E)
        n1, n2, n3 = _split3(F * (-LOG2E))
        for k, v in enumerate((p1, p2, p3, n1, n2, n3)):
            fs[k] = v.astype(F32)

        rowi = lax.broadcasted_iota(jnp.int32, (8, B), 0)
        zeros_tail = jnp.zeros((128 - DH - 8, B), F32)
        ones_row = jnp.where(lax.broadcasted_iota(jnp.int32, (KR - DH, B), 0) == 0, 1.0, 0.0).astype(BF16)
        for hh in range(H):
            sl = slice(hh * DH, (hh + 1) * DH)
            q = zT[sl]
            k = zT[A + hh * DH:A + (hh + 1) * DH]
            v = zT[2 * A + hh * DH:2 * A + (hh + 1) * DH]
            qh = q * lax.rsqrt(jnp.mean(q * q, axis=0, keepdims=True) + EPS) * qg_ref[...] * (0.125 * LOG2E)
            kh = k * lax.rsqrt(jnp.mean(k * k, axis=0, keepdims=True) + EPS) * kg_ref[...]
            frow = [fs[kk, hh:hh + 1, :] for kk in range(6)]
            qx = jnp.where(rowi < 3, 1.0, jnp.where(rowi == 3, frow[0], jnp.where(
                rowi == 4, frow[1], jnp.where(rowi == 5, frow[2], 0.0))))
            kx = jnp.where(rowi == 0, frow[3], jnp.where(rowi == 1, frow[4], jnp.where(
                rowi == 2, frow[5], jnp.where(rowi < 6, 1.0, 0.0))))
            tbuf[0:DH, :] = qh
            tbuf[DH:DH + 8, :] = qx
            tbuf[DH + 8:128, :] = zeros_tail
            qTa_ref[0, hh * 128:(hh + 1) * 128, :] = tbuf[...].astype(BF16)
            tbuf[0:DH, :] = kh
            tbuf[DH:DH + 8, :] = kx
            kaug_ref[hh] = tbuf[...].T.astype(BF16)
            kT_ref[0, hh * KR:hh * KR + DH, :] = kh.astype(BF16)
            kT_ref[0, hh * KR + DH:(hh + 1) * KR, :] = ones_row
            tbuf[0:DH, :] = v
            tbuf[DH:DH + 8, :] = jnp.zeros((8, B), F32)
            vaug_ref[hh] = tbuf[...].T.astype(BF16)
            vT_ref[0, sl, :] = v.astype(BF16)
            nq[hh:hh + 1, :] = jnp.max(_colsum(qh * qh), axis=1, keepdims=True)
            nk[hh:hh + 1, :] = jnp.max(_colsum(kh * kh), axis=1, keepdims=True)

        lane = lax.broadcasted_iota(jnp.int32, (8, 128), 1)
        stat_ref[0] = jnp.where(lane == 0, jnp.max(F, axis=1, keepdims=True), jnp.where(
            lane == 1, jnp.min(F, axis=1, keepdims=True), jnp.where(
                lane == 2, nq[...], jnp.where(lane == 3, nk[...], 0.0))))

    row = lambda w: pl.BlockSpec((B, w), lambda i: (i, 0))
    tsp = lambda r: pl.BlockSpec((1, r, B), lambda i: (i, 0, 0))
    return _pc(
        body, name="fwd_in", grid=(NB,),
        in_specs=[row(D), _const((1, D)), _const((1, D)), _const((1, D)), _const((ZT_ROWS, D)),
                  _const((D, 2 * CW)), _const((DH, 1)), _const((DH, 1)), _const((8, 1)), _const((B, B))],
        out_specs=[tsp(H * 128), tsp(H * KR), pl.BlockSpec((H, B, 128), lambda i: (0, i, 0)), tsp(A),
                   pl.BlockSpec((H, B, 128), lambda i: (0, i, 0)),
                   pl.BlockSpec((2 * A, B), lambda i: (0, i)), pl.BlockSpec((8, B), lambda i: (0, i)),
                   row(2 * CW), row(CW), pl.BlockSpec((1, 8, 128), lambda i: (i, 0, 0))],
        out_shape=[_sds((NB, H * 128, B), BF16), _sds((NB, H * KR, B), BF16), _sds((H, S, 128), BF16),
                   _sds((NB, A, B), BF16), _sds((H, S, 128), BF16), _sds((2 * A, S)), _sds((8, S)),
                   _sds((S, 2 * CW)), _sds((S, CW)), _sds((NB, 8, 128))],
        scratch_shapes=[pltpu.VMEM((8, 1), F32), pltpu.VMEM((128, B), F32), pltpu.VMEM((6, 8, B), F32),
                        pltpu.VMEM((8, 1), F32), pltpu.VMEM((8, 1), F32)],
        compiler_params=_cp(("arbitrary",)),
    )(x, n1g, sc1, sh1, wT, wr, qg_col, kg_col, bf_col, tri)


def _first_key_block(top, fmin_ref, cut, h, i):
    return lax.while_loop(
        lambda j: jnp.logical_and(j > 0, top - fmin_ref[h, jnp.maximum(j - 1, 0)] >= cut),
        lambda j: j - 1, i)


def _causal_keep(B):
    return lax.broadcasted_iota(jnp.int32, (B, B), 0) <= lax.broadcasted_iota(jnp.int32, (B, B), 1)


def _keep_right_half(B):
    hb = B // 2
    return lax.broadcasted_iota(jnp.int32, (B, hb), 0) <= lax.broadcasted_iota(jnp.int32, (B, hb), 1) + hb


def _keep_top_half(B):
    hb = B // 2
    return lax.broadcasted_iota(jnp.int32, (hb, B), 0) <= lax.broadcasted_iota(jnp.int32, (hb, B), 1)


def attn_fwd(fmax, fmin, thr, qTa, kaug, vT, late_w):
    NB, _, B = qTa.shape
    S = NB * B
    NG = len(late_w)
    HB = B // 2

    def body(fmax_ref, fmin_ref, thr_ref, q_ref, k_ref, v_ref, *rest):
        w_refs = rest[0:NG]
        o_ref, lse_ref = rest[NG:NG + 2]
        gw_refs = rest[NG + 2:2 * NG + 2]
        s0, s1, m_ref, l_ref, acc_ref = rest[2 * NG + 2:2 * NG + 7]
        sems = rest[2 * NG + 7:]
        h = pl.program_id(0)
        i = pl.program_id(1)

        def gathers():
            return [_direct_copies(False, w_refs[q], gw_refs[q], *sems[3 * q:3 * q + 3]) for q in range(NG)]

        @pl.when(jnp.logical_and(h == 0, i == 0))
        def _():
            for copies in gathers():
                _start_all(copies)

        jlo = _first_key_block(fmax_ref[h, i], fmin_ref, thr_ref[h], h, i)
        n = i - jlo

        def scores(j, s_ref):
            s_ref[...] = _nn(k_ref[0, pl.ds(pl.multiple_of(j * B, B), B), :], q_ref[0])

        def softmax_cols(s, j, cols, rows):
            m = m_ref[:, cols]
            mn = jnp.maximum(m, jnp.max(s, axis=0, keepdims=True))
            a = jnp.exp2(m - mn)
            p = jnp.exp2(s - mn)
            m_ref[:, cols] = mn
            l_ref[:, cols] = a * l_ref[:, cols] + _colsum(p)
            acc_ref[:, cols] = a * acc_ref[:, cols] + _nn(v_ref[j][:, rows], p.astype(BF16))

        def softmax_step(s_ref, j, masked):
            if not masked:
                softmax_cols(s_ref[...], j, slice(None), slice(None))
                return
            lo, hi = slice(0, HB), slice(HB, B)
            softmax_cols(jnp.where(_causal_keep(HB), s_ref[lo, lo], -jnp.inf), j, lo, lo)
            softmax_cols(jnp.where(_keep_right_half(B), s_ref[:, hi], -jnp.inf), j, hi, slice(None))

        m_ref[...] = jnp.full((1, B), -jnp.inf, F32)
        l_ref[...] = jnp.zeros((1, B), F32)
        acc_ref[...] = jnp.zeros((DH, B), F32)
        scores(jlo, s0)

        def pair(t, carry):
            j = jlo + 2 * t
            scores(j + 1, s1)
            softmax_step(s0, j, False)
            scores(j + 2, s0)
            softmax_step(s1, j + 1, False)
            return carry

        lax.fori_loop(0, n // 2, pair, 0)

        @pl.when(n % 2 == 1)
        def _():
            scores(i, s1)
            softmax_step(s0, i - 1, False)
            softmax_step(s1, i, True)

        @pl.when(n % 2 == 0)
        def _():
            softmax_step(s0, i, True)

        l = l_ref[...]
        o_ref[0] = acc_ref[...] / l
        lse_ref[0, 0] = m_ref[...] + jnp.log2(l)

        @pl.when(jnp.logical_and(h == H - 1, i == NB - 1))
        def _():
            for copies in gathers():
                _wait_all(copies)

    return _pc(
        body, name="attn_fwd", grid=(H, NB),
        in_specs=[SMEM, SMEM, SMEM, pl.BlockSpec((1, 128, B), lambda h, i: (i, h, 0)),
                  pl.BlockSpec((1, S, 128), lambda h, i: (h, 0, 0)),
                  pl.BlockSpec((NB, DH, B), lambda h, i: (0, h, 0))] + [ANY] * NG,
        out_specs=[pl.BlockSpec((1, DH, B), lambda h, i: (i, h, 0)),
                   pl.BlockSpec((1, 1, 1, B), lambda h, i: (h, i, 0, 0))] + [ANY] * NG,
        out_shape=[_sds((NB, A, B)), _sds((H, NB, 1, B))]
        + [_sds((N_DEV,) + w.shape, w.dtype) for w in late_w],
        scratch_shapes=[pltpu.VMEM((B, B), F32), pltpu.VMEM((B, B), F32), pltpu.VMEM((1, B), F32),
                        pltpu.VMEM((1, B), F32), pltpu.VMEM((DH, B), F32)] + COMM_SEMS * NG,
        compiler_params=_cp(("arbitrary", "arbitrary")),
    )(fmax, fmin, thr, qTa, kaug, vT, *late_w)


def attn_bwd(fmax, fmin, thr, qTa, kaug, kT, vaug, doTa, lse, delta, early_g):
    NG = len(early_g)
    NB, _, B = qTa.shape
    QR = 80
    CH = min(256, B)

    def body(fmax_ref, fmin_ref, thr_ref, q_ref, ka_ref, kt_ref, va_ref, do_ref, lse_ref, dl_ref, *rest):
        g_refs = rest[0:NG]
        dq_ref, dk_ref, dv_ref, dfk_ref = rest[NG:NG + 4]
        recv_refs = rest[NG + 4:2 * NG + 4]
        s0, d0, s1, d1, p0, e0, p1, e1, dk_acc, dv_acc = rest[2 * NG + 4:2 * NG + 14]
        sems = rest[2 * NG + 14:]
        h = pl.program_id(0)
        step = pl.program_id(1)
        j = NB - 1 - step

        def exchanges():
            return [_direct_copies(True, g_refs[q], recv_refs[q], *sems[3 * q:3 * q + 3]) for q in range(NG)]

        @pl.when(jnp.logical_and(h == 0, step == 0))
        def _():
            for copies in exchanges():
                _start_all(copies)

        bottom = fmin_ref[h, j]
        cut = thr_ref[h]
        ihi = lax.while_loop(
            lambda i: jnp.logical_and(i < NB - 1, fmax_ref[h, jnp.minimum(i + 1, NB - 1)] - bottom >= cut),
            lambda i: i + 1, j)

        @pl.when(step == 0)
        def _():
            dq_ref[...] = jnp.zeros_like(dq_ref)

        dk_acc[...] = jnp.zeros_like(dk_acc)
        dv_acc[...] = jnp.zeros_like(dv_acc)

        n = ihi - j + 1

        def blk(k):
            return jnp.minimum(j + k, ihi)

        def products(k, s_ref, d_ref):
            i = blk(k)
            s_ref[...] = _nn(ka_ref[0], q_ref[i])
            d_ref[...] = _nn(va_ref[0], do_ref[i])

        def elementwise(k, s_ref, d_ref, p_ref, ds_ref, masked):
            i = blk(k)
            p = jnp.exp2(s_ref[...] - lse_ref[0, i])
            if masked:
                p = jnp.where(_causal_keep(B), p, 0.0)
            p_ref[...] = p.astype(BF16)
            ds_ref[...] = (p * (d_ref[...] - dl_ref[0, i])).astype(BF16)

        def grads(k, p_ref, ds_ref):
            i = j + k
            dsb = ds_ref[...]
            dv_acc[...] = dv_acc[...] + _nt(do_ref[i][0:DH], p_ref[...])
            dk_acc[...] = dk_acc[...] + _nt(q_ref[i][0:QR], dsb)
            dq_ref[i] = dq_ref[i] + _nn(kt_ref[0], dsb)

        def trip(kp, s_p, d_p, ke, s_e, d_e, p_e, e_e, kg, p_g, e_g):
            ip, ie, ig = blk(kp), blk(ke), j + kg
            lse_e = lse_ref[0, ie]
            dl_e = dl_ref[0, ie]
            dq_part = jnp.zeros((KR, B), F32)
            for c in range(B // CH):
                rows = slice(c * CH, (c + 1) * CH)
                s_p[rows, :] = _nn(ka_ref[0, rows, :], q_ref[ip])
                d_p[rows, :] = _nn(va_ref[0, rows, :], do_ref[ip])
                p = jnp.exp2(s_e[rows, :] - lse_e)
                p_e[rows, :] = p.astype(BF16)
                e_e[rows, :] = (p * (d_e[rows, :] - dl_e)).astype(BF16)
                dsb = e_g[rows, :]
                dv_acc[:, rows] = dv_acc[:, rows] + _nt(do_ref[ig][0:DH], p_g[rows, :])
                dk_acc[:, rows] = dk_acc[:, rows] + _nt(q_ref[ig][0:QR], dsb)
                dq_part = dq_part + _nn(kt_ref[0, :, rows], dsb)
            dq_ref[ig] = dq_ref[ig] + dq_part

        HB = B // 2
        lo, hi = slice(0, HB), slice(HB, B)

        def diagonal_products(s_ref, d_ref):
            s_ref[lo, :] = _nn(ka_ref[0, lo, :], q_ref[j])
            d_ref[lo, :] = _nn(va_ref[0, lo, :], do_ref[j])
            s_ref[hi, hi] = _nn(ka_ref[0, hi, :], q_ref[j, :, hi])
            d_ref[hi, hi] = _nn(va_ref[0, hi, :], do_ref[j, :, hi])

        def diagonal_elementwise(s_ref, d_ref, p_ref, ds_ref):
            lse0 = lse_ref[0, j]
            dl0 = dl_ref[0, j]
            p = jnp.where(_keep_top_half(B), jnp.exp2(s_ref[lo, :] - lse0), 0.0)
            p_ref[lo, :] = p.astype(BF16)
            ds_ref[lo, :] = (p * (d_ref[lo, :] - dl0)).astype(BF16)
            p = jnp.where(_causal_keep(HB), jnp.exp2(s_ref[hi, hi] - lse0[:, hi]), 0.0)
            p_ref[hi, hi] = p.astype(BF16)
            ds_ref[hi, hi] = (p * (d_ref[hi, hi] - dl0[:, hi])).astype(BF16)
            p_ref[hi, lo] = jnp.zeros((HB, HB), BF16)
            ds_ref[hi, lo] = jnp.zeros((HB, HB), BF16)

        diagonal_products(s0, d0)
        products(1, s1, d1)
        diagonal_elementwise(s0, d0, p0, e0)

        def pair(t, carry):
            k = 2 * t
            trip(k + 2, s0, d0, k + 1, s1, d1, p1, e1, k, p0, e0)
            trip(k + 3, s1, d1, k + 2, s0, d0, p0, e0, k + 1, p1, e1)
            return carry

        n_pairs = n // 2
        lax.fori_loop(0, n_pairs - 1, pair, 0)
        k_last = 2 * (n_pairs - 1)

        @pl.when(jnp.logical_and(n_pairs >= 1, n % 2 == 0))
        def _():
            elementwise(k_last + 1, s1, d1, p1, e1, False)
            grads(k_last, p0, e0)
            grads(k_last + 1, p1, e1)

        @pl.when(jnp.logical_and(n_pairs >= 1, n % 2 == 1))
        def _():
            trip(k_last + 2, s0, d0, k_last + 1, s1, d1, p1, e1, k_last, p0, e0)
            elementwise(k_last + 2, s0, d0, p0, e0, False)
            grads(k_last + 1, p1, e1)
            grads(k_last + 2, p0, e0)

        @pl.when(n == 1)
        def _():
            grads(0, p0, e0)

        dk_ref[0] = dk_acc[0:DH, :]
        dfk_ref[0, 0] = dk_acc[DH:DH + 8, :]
        dv_ref[0] = dv_acc[...]

        @pl.when(jnp.logical_and(h == H - 1, step == NB - 1))
        def _():
            for copies in exchanges():
                _wait_all(copies)

    per_kv = lambda r: pl.BlockSpec((1, r, B), lambda h, t: (NB - 1 - t, h, 0))
    head_all = lambda r: pl.BlockSpec((NB, r, B), lambda h, j: (0, h, 0))
    aug = pl.BlockSpec((1, B, 128), lambda h, t: (h, NB - 1 - t, 0))
    stat = pl.BlockSpec((1, NB, 1, B), lambda h, j: (h, 0, 0, 0))
    return _pc(
        body, name="attn_bwd", grid=(H, NB),
        in_specs=[SMEM, SMEM, SMEM, head_all(128), aug, per_kv(KR), aug, head_all(128), stat, stat]
        + [ANY] * NG,
        out_specs=[head_all(KR), per_kv(DH), per_kv(DH),
                   pl.BlockSpec((1, 1, 8, B), lambda h, t: (h, NB - 1 - t, 0, 0))] + [ANY] * NG,
        out_shape=[_sds((NB, H * KR, B)), _sds((NB, A, B)), _sds((NB, A, B)), _sds((H, NB, 8, B))]
        + [_sds(g.shape, g.dtype) for g in early_g],
        scratch_shapes=[pltpu.VMEM((B, B), F32)] * 4 + [pltpu.VMEM((B, B), BF16)] * 4
        + [pltpu.VMEM((QR, B), F32), pltpu.VMEM((DH, B), F32)] + COMM_SEMS * NG,
        compiler_params=_cp(("arbitrary", "arbitrary")),
    )(fmax, fmin, thr, qTa, kaug, kT, vaug, doTa, lse, delta, *early_g)


def _conv_tail(u1, lng, lnb, beta_c):
    mu = _rowmean(u1)
    d = u1 - mu
    rstd = lax.rsqrt(_rowmean(d * d) + EPS)
    xhat = d * rstd
    u2 = xhat * lng + lnb
    sg = _sigmoid(u2)
    u3 = u2 * sg
    rc = lax.rsqrt(_rowmean(u3 * u3) + EPS)
    n3 = u3 * rc
    return rstd, xhat, u2, sg, rc, n3, n3 * beta_c


def _attn_tail(oT, beta_a_col):
    ra = lax.rsqrt(jnp.mean(oT * oT, axis=0, keepdims=True) + EPS)
    ohat = oT * ra
    return ra, ohat, ohat * beta_a_col


CONV_ROWS = 64
CORR_ROWS = 64


def _fill_shifted(buf, sh):
    rows = sh.shape[1]
    for ph in range(8):
        sh[ph] = buf[pl.ds(ph, rows), :]


def _tap_rows(sh, r0, o, rows):
    return sh[o % 8, pl.ds(pl.multiple_of(r0 + 8 * (o // 8), 8), rows), :]


def _depthwise(sh, w_ref, offs, bias, out_ref, B):
    def chunk(ci, carry):
        r0 = pl.multiple_of(ci * CONV_ROWS, CONV_ROWS)
        acc = jnp.broadcast_to(bias, (CONV_ROWS, CW))
        for k, o in enumerate(offs):
            acc = acc + w_ref[k:k + 1, :] * _tap_rows(sh, r0, o, CONV_ROWS)
        out_ref[pl.ds(r0, CONV_ROWS), :] = acc
        return carry

    lax.fori_loop(0, B // CONV_ROWS, chunk, 0)


def _tap_gradients(sh, d_ref, offs, dw_ref, B):
    for g0 in range(0, len(offs), 8):
        ks = list(range(g0, min(g0 + 8, len(offs))))

        def chunk(ci, accs, ks=ks):
            r0 = pl.multiple_of(ci * CORR_ROWS, CORR_ROWS)
            d = d_ref[pl.ds(r0, CORR_ROWS), :]
            out = []
            for a, k in zip(accs, ks):
                pr = d * _tap_rows(sh, r0, offs[k], CORR_ROWS)
                for r in range(0, CORR_ROWS, 8):
                    a = a + pr[r:r + 8]
                out.append(a)
            return tuple(out)

        accs = lax.fori_loop(0, B // CORR_ROWS, chunk, tuple(jnp.zeros((8, CW), F32) for _ in ks))
        for a, k in zip(accs, ks):
            dw_ref[k:k + 1, :] = dw_ref[k:k + 1, :] + _colsum(a)


def conv_merge_out(u0, x, oT, conv_w, conv_b, lng, lnb, beta_c, beta_a_col, wo, g1):
    S = x.shape[0]
    B = BLK
    NB = S // B

    def body(uc_ref, up_ref, x_ref, oT_ref, w_ref, cb_ref, lng_ref, lnb_ref, bc_ref, ba_ref, wo_ref,
             g1_ref, x1_ref, o_ref, u1_ref, ubuf, sh):
        i = pl.program_id(0)
        ubuf[0:HALO, :] = jnp.where(i > 0, up_ref[B - HALO:B, :], 0.0)
        ubuf[HALO:HALO + B, :] = uc_ref[...]
        ubuf[HALO + B:HALO + B + 8, :] = jnp.zeros((8, CW), F32)
        _fill_shifted(ubuf, sh)
        _depthwise(sh, w_ref, [HALO - (KC - 1) + k for k in range(KC)], cb_ref[...], u1_ref, B)
        mc = _conv_tail(u1_ref[...], lng_ref[...], lnb_ref[...], bc_ref[...])[-1]
        maT = _attn_tail(oT_ref[0], ba_ref[...])[-1]
        ma = maT.T
        o = _nn(ma.astype(BF16), wo_ref[0:A, :]) + _nn(mc.astype(BF16), wo_ref[A:D, :])
        o_ref[...] = o
        x1_ref[...] = x_ref[...] + g1_ref[...] * o

    row = lambda w: pl.BlockSpec((B, w), lambda i: (i, 0))
    return _pc(
        body, name="conv_merge_out", grid=(NB,),
        in_specs=[row(CW), pl.BlockSpec((B, CW), lambda i: (jnp.maximum(i - 1, 0), 0)), row(D),
                  pl.BlockSpec((1, A, B), lambda i: (i, 0, 0)), _const((32, CW)), _const((1, CW)),
                  _const((1, CW)), _const((1, CW)), _const((1, CW)), _const((A, 1)), _const((D, D)),
                  _const((1, D))],
        out_specs=[row(D), row(D), row(CW)],
        out_shape=[_sds((S, D)), _sds((S, D)), _sds((S, CW))],
        scratch_shapes=[pltpu.VMEM((B + HALO + 8, CW), F32), pltpu.VMEM((8, B + HALO, CW), F32)],
        compiler_params=_cp(("arbitrary",)),
    )(u0, u0, x, oT, conv_w, conv_b, lng, lnb, beta_c, beta_a_col, wo, g1)


def mlp_fwd_loss(x1, tgt, n2g, sc2, sh2, g2, w1s, w2):
    S = x1.shape[0]
    B = BLK
    NB = S // B
    NS, _, FS = w1s.shape

    def body(x1_ref, t_ref, g_ref, sc_ref, sh_ref, g2_ref, w1_hbm, w2_hbm, dy_ref, loss_ref, dg2_ref, ra_ref,
             ht_ref, w1_v, w2_v, sem):
        i = pl.program_id(0)

        def weight_loads():
            return [pltpu.make_async_copy(w1_hbm, w1_v, sem.at[0]), pltpu.make_async_copy(w2_hbm, w2_v, sem.at[1])]

        @pl.when(i == 0)
        def _():
            for cp in weight_loads():
                cp.start()
            loss_ref[...] = jnp.zeros_like(loss_ref)
            dg2_ref[...] = jnp.zeros_like(dg2_ref)

        x1 = x1_ref[...]
        h2 = _norm_mod(x1, g_ref[...], sc_ref[...], sh_ref[...])[2]
        hs = h2.astype(BF16)
        ht_ref[...] = h2.T.astype(BF16)

        @pl.when(i == 0)
        def _():
            for cp in weight_loads():
                cp.wait()

        y2 = jnp.zeros((B, D), F32)
        for q in range(NS):
            a = jnp.maximum(_nn(hs, w1_v[q]), 0.0)
            ra_ref[:, q * FS:(q + 1) * FS] = a.astype(BF16)
            y2 = y2 + _nn((a * a).astype(BF16), w2_v[q * FS:(q + 1) * FS, :])
        e = x1 + g2_ref[...] * y2 - t_ref[...]
        dy = e * (1.0 / D)
        dy_ref[...] = dy
        loss_ref[...] = loss_ref[...] + 0.5 * _colsum(_rowmean(e * e))
        dg2_ref[...] = dg2_ref[...] + _colsum(dy * y2)

    row = pl.BlockSpec((B, D), lambda i: (i, 0))
    return _pc(
        body, name="mlp_fwd_loss", grid=(NB,),
        in_specs=[row, row, _const((1, D)), _const((1, D)), _const((1, D)), _const((1, D)), ANY, ANY],
        out_specs=[row, _const((1, 1)), _const((1, D)), pl.BlockSpec((B, FF), lambda i: (i, 0)),
                   pl.BlockSpec((D, B), lambda i: (0, i))],
        out_shape=[_sds((S, D)), _sds((1, 1)), _sds((1, D)), _sds((S, FF), BF16), _sds((D, S), BF16)],
        scratch_shapes=[pltpu.VMEM(w1s.shape, BF16), pltpu.VMEM(w2.shape, BF16), pltpu.SemaphoreType.DMA((2,))],
        compiler_params=_cp(("arbitrary",)),
    )(x1, tgt, n2g, sc2, sh2, g2, w1s, w2)


def mlp_bwd(h2t, dy, ra, g2, w1s, w2):
    S = dy.shape[0]
    B = BLK
    NB = S // B
    NC = FF // FCH
    NS, _, FS = w1s.shape
    SPC = NS // NC

    def body(ht_ref, dy_ref, ra_ref, g2_ref, w1_ref, w2_ref, dw1_ref, dw2_ref, dh_ref):
        i = pl.program_id(1)

        @pl.when(i == 0)
        def _():
            dw1_ref[...] = jnp.zeros_like(dw1_ref)
            dw2_ref[...] = jnp.zeros_like(dw2_ref)

        h2t = ht_ref[...]
        ra = ra_ref[...].astype(F32)
        dyb = (dy_ref[...] * g2_ref[...]).astype(BF16)
        db = _nt(dyb, w2_ref[...])
        dab = (db * (2.0 * ra)).astype(BF16)
        dh = jnp.zeros((B, D), F32)
        for q in range(SPC):
            part = dab[:, q * FS:(q + 1) * FS]
            dh = dh + _nt(part, w1_ref[q])
            dw1_ref[q] = dw1_ref[q] + _nn(h2t, part)
        dh_ref[0] = dh
        dw2_ref[...] = dw2_ref[...] + _nn((ra * ra).T.astype(BF16), dyb)

    row = pl.BlockSpec((B, D), lambda c, i: (i, 0))
    vec = pl.BlockSpec((1, D), lambda c, i: (0, 0))
    wsh = pl.BlockSpec((SPC, D, FS), lambda c, i: (c, 0, 0))
    w2b = pl.BlockSpec((FCH, D), lambda c, i: (c, 0))
    return _pc(
        body, name="mlp_bwd", grid=(NC, NB),
        in_specs=[pl.BlockSpec((D, B), lambda c, i: (0, i)), row, pl.BlockSpec((B, FCH), lambda c, i: (i, c)),
                  vec, wsh, w2b],
        out_specs=[wsh, w2b, pl.BlockSpec((1, B, D), lambda c, i: (c, i, 0))],
        out_shape=[_sds(w1s.shape), _sds((FF, D)), _sds((NC, S, D))],
        compiler_params=_cp(("arbitrary", "arbitrary")),
    )(h2t, dy, ra, g2, w1s, w2)


def merge_bwd(dh2p, x1, dy, o, oT, u1, n2g, sc2, g1, lng, lnb, beta_c, beta_a_col, wo):
    S = x1.shape[0]
    B = BLK
    NB = S // B
    NC = dh2p.shape[0]

    def body(dh_ref, x1_ref, dy_ref, o_ref, oT_ref, u1_ref, g_ref, sc_ref, g1_ref, lng_ref, lnb_ref,
             bc_ref, ba_ref, wo_ref,
             dx1_ref, doTa_ref, dl_ref, du1_ref, dwo_ref, v1_ref, v2_ref, dba_ref):
        i = pl.program_id(0)

        @pl.when(i == 0)
        def _():
            dwo_ref[...] = jnp.zeros_like(dwo_ref)
            v1_ref[...] = jnp.zeros_like(v1_ref)
            v2_ref[...] = jnp.zeros_like(v2_ref)
            dba_ref[...] = jnp.zeros_like(dba_ref)

        dh2 = dh_ref[0]
        for cc in range(1, NC):
            dh2 = dh2 + dh_ref[cc]
        x1 = x1_ref[...]
        g = g_ref[...]
        sc = sc_ref[...]
        r2 = lax.rsqrt(_rowmean(x1 * x1) + EPS)
        xh = x1 * r2
        dhx = dh2 * xh
        v1_ref[0:1, :] = v1_ref[0:1, :] + _colsum(dh2)
        v1_ref[1:2, :] = v1_ref[1:2, :] + _colsum(dhx) * g
        v1_ref[2:3, :] = v1_ref[2:3, :] + _colsum(dhx) * (1.0 + sc)
        dxh = dh2 * (g * (1.0 + sc))
        dx1 = dy_ref[...] + r2 * (dxh - xh * _rowmean(dxh * xh))
        dx1_ref[...] = dx1
        v1_ref[3:4, :] = v1_ref[3:4, :] + _colsum(dx1 * o_ref[...])
        dob = (dx1 * g1_ref[...]).astype(BF16)

        lng = lng_ref[...]
        bc = bc_ref[...]
        rstd, xhat, u2, sg, rc, n3, mc = _conv_tail(u1_ref[...], lng, lnb_ref[...], bc)
        ba = ba_ref[...]
        oT = oT_ref[0]
        ra, ohat, maT = _attn_tail(oT, ba)
        dwo_ref[0:A, :] = dwo_ref[0:A, :] + _nn(maT.astype(BF16), dob)
        dwo_ref[A:D, :] = dwo_ref[A:D, :] + _nn(mc.T.astype(BF16), dob)
        dmaT = _nt(wo_ref[0:A, :], dob)
        dmc = _nt(dob, wo_ref[A:D, :])

        dba_ref[...] = dba_ref[...] + _lanesum(dmaT * ohat)
        dohat = dmaT * ba
        doT = ra * (dohat - ohat * jnp.mean(dohat * ohat, axis=0, keepdims=True))
        prod = doT * oT
        zpad = jnp.zeros((128 - DH, B), BF16)
        for hh in range(H):
            sl = slice(hh * DH, (hh + 1) * DH)
            dl_ref[hh, 0] = _colsum(prod[sl])
            doTa_ref[0, hh * 128:hh * 128 + DH, :] = doT[sl].astype(BF16)
            doTa_ref[0, hh * 128 + DH:(hh + 1) * 128, :] = zpad

        v2_ref[0:1, :] = v2_ref[0:1, :] + _colsum(dmc * n3)
        dn3 = dmc * bc
        du3 = rc * (dn3 - n3 * _rowmean(dn3 * n3))
        du2 = du3 * (sg * (1.0 + u2 * (1.0 - sg)))
        v2_ref[1:2, :] = v2_ref[1:2, :] + _colsum(du2 * xhat)
        v2_ref[2:3, :] = v2_ref[2:3, :] + _colsum(du2)
        dxhat = du2 * lng
        du1_ref[...] = rstd * (dxhat - _rowmean(dxhat) - xhat * _rowmean(dxhat * xhat))

    row = lambda w: pl.BlockSpec((B, w), lambda i: (i, 0))
    return _pc(
        body, name="merge_bwd", grid=(NB,),
        in_specs=[pl.BlockSpec((NC, B, D), lambda i: (0, i, 0)), row(D), row(D), row(D),
                  pl.BlockSpec((1, A, B), lambda i: (i, 0, 0)), row(CW), _const((1, D)), _const((1, D)),
                  _const((1, D)), _const((1, CW)), _const((1, CW)), _const((1, CW)), _const((A, 1)),
                  _const((D, D))],
        out_specs=[row(D), pl.BlockSpec((1, H * 128, B), lambda i: (i, 0, 0)),
                   pl.BlockSpec((H, 1, 1, B), lambda i: (0, i, 0, 0)), row(CW), _const((D, D)),
                   _const((8, D)), _const((8, CW)), _const((A, 1))],
        out_shape=[_sds((S, D)), _sds((NB, H * 128, B), BF16), _sds((H, NB, 1, B)), _sds((S, CW)),
                   _sds((D, D)), _sds((8, D)), _sds((8, CW)), _sds((A, 1))],
        compiler_params=_cp(("arbitrary",)),
    )(dh2p, x1, dy, o, oT, u1, n2g, sc2, g1, lng, lnb, beta_c, beta_a_col, wo)


def head_bwd(du1, u0, alg, zqk, fgT, dqT, dkT, dvT, dfk, conv_w, qg_col, kg_col, bf_col, tri_lo):
    S = u0.shape[0]
    B = BLK
    NB = S // B

    def body(dc_ref, dn_ref, uc_ref, up_ref, alg_ref, zqk_ref, fg_ref, dq_ref, dk_ref, dv_ref, dfk_ref,
             w_ref, qg_ref, kg_ref, bf_ref, tri_ref,
             dzr_ref, dzT_ref, dcw_ref, vc_ref, dqg_ref, dkg_ref, dbf_ref,
             buf, sh, du0_ref, carry, fbuf):
        pid = pl.program_id(0)
        ri = NB - 1 - pid

        @pl.when(pid == 0)
        def _():
            carry[...] = jnp.zeros_like(carry)
            dcw_ref[...] = jnp.zeros_like(dcw_ref)
            vc_ref[...] = jnp.zeros_like(vc_ref)
            dqg_ref[...] = jnp.zeros_like(dqg_ref)
            dkg_ref[...] = jnp.zeros_like(dkg_ref)
            dbf_ref[...] = jnp.zeros_like(dbf_ref)

        zero8 = jnp.zeros((8, CW), F32)
        buf[0:B, :] = dc_ref[...]
        buf[B:B + HALO, :] = jnp.where(ri < NB - 1, dn_ref[0:HALO, :], 0.0)
        buf[B + HALO:B + HALO + 8, :] = zero8
        _fill_shifted(buf, sh)
        _depthwise(sh, w_ref, [KC - 1 - k for k in range(KC)], jnp.zeros((1, CW), F32), du0_ref, B)
        buf[0:HALO, :] = jnp.where(ri > 0, up_ref[B - HALO:B, :], 0.0)
        buf[HALO:HALO + B, :] = uc_ref[...]
        buf[HALO + B:HALO + B + 8, :] = zero8
        _fill_shifted(buf, sh)
        _tap_gradients(sh, dc_ref, [HALO - (KC - 1) + k for k in range(KC)], dcw_ref, B)
        vc_ref[0:1, :] = vc_ref[0:1, :] + _colsum(dc_ref[...])

        du0 = du0_ref[...]
        al = alg_ref[:, 0:CW]
        sg = _sigmoid(alg_ref[:, CW:2 * CW])
        dzr_ref[:, 0:CW] = (du0 * sg).astype(BF16)
        dzr_ref[:, CW:2 * CW] = (du0 * al * sg * (1.0 - sg)).astype(BF16)

        dqg = jnp.zeros((DH, B), F32)
        dkg = jnp.zeros((DH, B), F32)
        qg = qg_ref[...]
        kg = kg_ref[...]
        for hh in range(H):
            sl = slice(hh * DH, (hh + 1) * DH)
            q = zqk_ref[sl, :]
            rq = lax.rsqrt(jnp.mean(q * q, axis=0, keepdims=True) + EPS)
            qn = q * rq
            dqh = dq_ref[0, hh * KR:hh * KR + DH, :] * 0.125
            dqg = dqg + dqh * qn
            dqn = dqh * qg
            dzT_ref[0, sl, :] = (rq * (dqn - qn * jnp.mean(dqn * qn, axis=0, keepdims=True))).astype(BF16)
            k = zqk_ref[A + hh * DH:A + (hh + 1) * DH, :]
            rk = lax.rsqrt(jnp.mean(k * k, axis=0, keepdims=True) + EPS)
            kn = k * rk
            dkh = dk_ref[0, sl, :] * (1.0 / LOG2E)
            dkg = dkg + dkh * kn
            dkn = dkh * kg
            dzT_ref[0, A + hh * DH:A + (hh + 1) * DH, :] = (
                rk * (dkn - kn * jnp.mean(dkn * kn, axis=0, keepdims=True))).astype(BF16)
            fbuf[hh:hh + 1, :] = dq_ref[0, hh * KR + DH:hh * KR + DH + 1, :] - dfk_ref[hh, 0, 0:1, :]
        dqg_ref[...] = dqg_ref[...] + _lanesum(dqg)
        dkg_ref[...] = dkg_ref[...] + _lanesum(dkg)
        dzT_ref[0, 2 * A:3 * A, :] = dv_ref[0].astype(BF16)

        dF = fbuf[...]
        a1, a2, a3 = _split3(dF)
        tr = tri_ref[...]
        dlogf = _nn(a1, tr) + _nn(a2, tr) + _nn(a3, tr) + carry[...]
        carry[...] = carry[...] + _lanesum(dF)
        dfg = dlogf * _sigmoid(-(fg_ref[...] + bf_ref[...]))
        dbf_ref[...] = dbf_ref[...] + _lanesum(dfg)
        dzT_ref[0, 3 * A:ZT_ROWS, :] = jnp.concatenate([dfg, jnp.zeros((8, B), F32)], axis=0).astype(BF16)

    rrow = lambda w: pl.BlockSpec((B, w), lambda p: (NB - 1 - p, 0))
    rts = lambda r: pl.BlockSpec((1, r, B), lambda p: (NB - 1 - p, 0, 0))
    return _pc(
        body, name="head_bwd", grid=(NB,),
        in_specs=[rrow(CW), pl.BlockSpec((B, CW), lambda p: (jnp.minimum(NB - p, NB - 1), 0)),
                  rrow(CW), pl.BlockSpec((B, CW), lambda p: (jnp.maximum(NB - 2 - p, 0), 0)),
                  rrow(2 * CW), pl.BlockSpec((2 * A, B), lambda p: (0, NB - 1 - p)),
                  pl.BlockSpec((8, B), lambda p: (0, NB - 1 - p)), rts(H * KR), rts(A), rts(A),
                  pl.BlockSpec((H, 1, 8, B), lambda p: (0, NB - 1 - p, 0, 0)),
                  _const((32, CW)), _const((DH, 1)), _const((DH, 1)), _const((8, 1)), _const((B, B))],
        out_specs=[rrow(2 * CW), rts(ZT_ROWS), _const((32, CW)), _const((8, CW)), _const((DH, 1)),
                   _const((DH, 1)), _const((8, 1))],
        out_shape=[_sds((S, 2 * CW), BF16), _sds((NB, ZT_ROWS, B), BF16), _sds((32, CW)), _sds((8, CW)),
                   _sds((DH, 1)), _sds((DH, 1)), _sds((8, 1))],
        scratch_shapes=[pltpu.VMEM((B + HALO + 8, CW), F32), pltpu.VMEM((8, B + HALO, CW), F32),
                        pltpu.VMEM((B, CW), F32), pltpu.VMEM((8, 1), F32), pltpu.VMEM((8, B), F32)],
        compiler_params=_cp(("arbitrary",)),
    )(du1, du1, u0, u0, alg, zqk, fgT, dqT, dkT, dvT, dfk, conv_w, qg_col, kg_col, bf_col, tri_lo)


def in_bwd_w(x, dzr, dzT, n1g, sc1, sh1):
    S = x.shape[0]
    B = BLK
    NB = S // B

    def body(x_ref, dzr_ref, dzT_ref, g_ref, sc_ref, sh_ref, dwT_hbm, dwr_hbm, dwT_acc, dwr_acc, sem):
        i = pl.program_id(0)

        @pl.when(i == 0)
        def _():
            dwT_acc[...] = jnp.zeros_like(dwT_acc)
            dwr_acc[...] = jnp.zeros_like(dwr_acc)

        h1 = _norm_mod(x_ref[...], g_ref[...], sc_ref[...], sh_ref[...])[2]
        dwT_acc[...] = dwT_acc[...] + _nn(dzT_ref[0], h1.astype(BF16))
        dwr_acc[...] = dwr_acc[...] + _nn(h1.T.astype(BF16), dzr_ref[...])

        @pl.when(i == NB - 1)
        def _():
            for src, dst in ((dwT_acc, dwT_hbm), (dwr_acc, dwr_hbm)):
                cp = pltpu.make_async_copy(src, dst, sem)
                cp.start()
                cp.wait()

    row = lambda w: pl.BlockSpec((B, w), lambda i: (i, 0))
    return _pc(
        body, name="in_bwd_w", grid=(NB,),
        in_specs=[row(D), row(2 * CW), pl.BlockSpec((1, ZT_ROWS, B), lambda i: (i, 0, 0)),
                  _const((1, D)), _const((1, D)), _const((1, D))],
        out_specs=[ANY, ANY],
        out_shape=[_sds((ZT_ROWS, D)), _sds((D, 2 * CW))],
        scratch_shapes=[pltpu.VMEM((ZT_ROWS, D), F32), pltpu.VMEM((D, 2 * CW), F32),
                        pltpu.SemaphoreType.DMA(())],
        compiler_params=_cp(("arbitrary",)),
    )(x, dzr, dzT, n1g, sc1, sh1)


def in_bwd_x(x, dx1, dzr, dzT, n1g, sc1, sh1, w_qkvf, wr, late_g):
    S = x.shape[0]
    B = BLK
    NB = S // B

    def body(x_ref, dx1_ref, dzr_ref, dzT_ref, g_ref, sc_ref, sh_ref, w_hbm, wr_hbm, lg_ref,
             gx_ref, v_ref, recv_ref, w_v, wr_v, sem, send_sems, recv_sems, local_sem):
        i = pl.program_id(0)

        @pl.when(i == 0)
        def _():
            _start_all(_direct_copies(True, lg_ref, recv_ref, send_sems, recv_sems, local_sem))
            for src, dst in ((w_hbm, w_v), (wr_hbm, wr_v)):
                cp = pltpu.make_async_copy(src, dst, sem)
                cp.start()
                cp.wait()
            v_ref[...] = jnp.zeros_like(v_ref)

        g = g_ref[...]
        sc = sc_ref[...]
        r1, xh, _ = _norm_mod(x_ref[...], g, sc, sh_ref[...])
        dh1 = _nt(dzr_ref[...], wr_v[...]) + _nn(w_v[...], dzT_ref[0]).T
        dhx = dh1 * xh
        v_ref[0:1, :] = v_ref[0:1, :] + _colsum(dh1)
        v_ref[1:2, :] = v_ref[1:2, :] + _colsum(dhx) * g
        v_ref[2:3, :] = v_ref[2:3, :] + _colsum(dhx) * (1.0 + sc)
        dxh = dh1 * (g * (1.0 + sc))
        gx_ref[...] = dx1_ref[...] + r1 * (dxh - xh * _rowmean(dxh * xh))

        @pl.when(i == NB - 1)
        def _():
            _wait_all(_direct_copies(True, lg_ref, recv_ref, send_sems, recv_sems, local_sem))

    row = lambda w: pl.BlockSpec((B, w), lambda i: (i, 0))
    return _pc(
        body, name="in_bwd_x", grid=(NB,),
        in_specs=[row(D), row(D), row(2 * CW), pl.BlockSpec((1, ZT_ROWS, B), lambda i: (i, 0, 0)),
                  _const((1, D)), _const((1, D)), _const((1, D)), ANY, ANY, ANY],
        out_specs=[row(D), _const((8, D)), ANY],
        out_shape=[_sds((S, D)), _sds((8, D)), _sds(late_g.shape, late_g.dtype)],
        scratch_shapes=[pltpu.VMEM((D, ZT_ROWS), BF16), pltpu.VMEM((D, 2 * CW), BF16),
                        pltpu.SemaphoreType.DMA(())] + COMM_SEMS,
        compiler_params=_cp(("arbitrary",)),
    )(x, dx1, dzr, dzT, n1g, sc1, sh1, w_qkvf, wr, late_g)


def _adam_math(g, w, m, v):
    m = ADAM_B1 * m + (1.0 - ADAM_B1) * g
    v = ADAM_B2 * v + (1.0 - ADAM_B2) * (g * g)
    m_hat = m / (1.0 - ADAM_B1 ** ADAM_STEP)
    v_hat = v / (1.0 - ADAM_B2 ** ADAM_STEP)
    delta = -ADAM_LR * (m_hat / (jnp.sqrt(v_hat) + ADAM_EPS) + ADAM_WD * w)
    return delta, m, v


def _row_tile(R):
    for t in (1024, 512, 256, 128, 64, 32, 16, 8):
        if R % t == 0:
            return t
    return R


def sum_slots(parts, name):
    K, R, C = parts.shape
    T = _row_tile(R)

    def body(p_ref, o_ref):
        s = p_ref[0]
        for k in range(1, K):
            s = s + p_ref[k]
        o_ref[...] = s

    return _pc(body, name=name, grid=(R // T,),
               in_specs=[pl.BlockSpec((K, T, C), lambda i: (0, i, 0))],
               out_specs=pl.BlockSpec((T, C), lambda i: (i, 0)), out_shape=_sds((R, C)),
               compiler_params=_cp(("arbitrary",)))(parts)


def adamw_slots(parts, w, m, v, name):
    K, R, C = parts.shape
    T = R
    while K * T * C * 4 > (8 << 20) and T % 16 == 0:
        T //= 2

    def body(p_ref, w_ref, m_ref, v_ref, g_ref, d_ref, nm_ref, nv_ref):
        g = p_ref[0].astype(F32)
        for k in range(1, K):
            g = g + p_ref[k].astype(F32)
        g_ref[...] = g
        d_ref[...], nm_ref[...], nv_ref[...] = _adam_math(g, w_ref[...], m_ref[...], v_ref[...])

    t2 = pl.BlockSpec((T, C), lambda i: (i, 0))
    return _pc(body, name=name, grid=(R // T,),
               in_specs=[pl.BlockSpec((K, T, C), lambda i: (0, i, 0)), t2, t2, t2],
               out_specs=[t2, t2, t2, t2], out_shape=[_sds((R, C))] * 4,
               compiler_params=_cp(("arbitrary",)))(parts, w, m, v)


def adamw_many(gs, ws, ms, vs, name):
    n = len(ws)

    def body(*refs):
        outs = refs[4 * n:]
        for q in range(n):
            d, nm, nv = _adam_math(refs[q][...], refs[n + q][...], refs[2 * n + q][...], refs[3 * n + q][...])
            outs[q][...] = d
            outs[n + q][...] = nm
            outs[2 * n + q][...] = nv

    res = _pc(body, name=name, grid=(1,),
              in_specs=[_const(a.shape) for a in list(gs) + list(ws) + list(ms) + list(vs)],
              out_specs=[_const(w.shape) for w in ws] * 3, out_shape=[_sds(w.shape) for w in ws] * 3,
              compiler_params=_cp())(*gs, *ws, *ms, *vs)
    return res[0:n], res[n:2 * n], res[2 * n:3 * n]


def ada_grad_adamw(cT, dmod_cols, w, m, v):
    NCOL = w.shape[1]

    def body(cT_ref, dm_ref, w_ref, m_ref, v_ref, g_ref, d_ref, nm_ref, nv_ref):
        def term(b):
            cv = cT_ref[b]
            return (cv * _sigmoid(cv)) * dm_ref[b:b + 1, :]

        g = term(0)
        for b in range(1, N_DEV):
            g = g + term(b)
        g_ref[...] = g
        d_ref[...], nm_ref[...], nv_ref[...] = _adam_math(g, w_ref[...], m_ref[...], v_ref[...])

    full = _const((D, NCOL))
    return _pc(body, name="ada_grad_adamw", grid=(1,),
               in_specs=[_const((N_DEV, D, 1)), _const((N_DEV, NCOL)), full, full, full],
               out_specs=[full, full, full, full], out_shape=[_sds((D, NCOL))] * 4,
               compiler_params=_cp())(cT, dmod_cols, w, m, v)


def _pack_rows(vecs, rows=None):
    flat = jnp.concatenate([jnp.ravel(v) for v in vecs])
    n = flat.shape[0]
    r = -(-n // 1024) * 8 if rows is None else rows
    return jnp.pad(flat, (0, r * 128 - n)).reshape(r, 128)


def _unpack_rows(packed, shapes):
    flat = packed.reshape(-1)
    out, off = [], 0
    for s in shapes:
        n = 1
        for d in s:
            n *= d
        out.append(flat[off:off + n].reshape(s))
        off += n
    return out


def _cols_to_shards(w, cols):
    rows = w.shape[0]
    return w.reshape(rows, N_DEV, cols).transpose(1, 0, 2).reshape(N_DEV, rows * cols)


def kernel(x, c, w_ada, b_ada, norm1_g, w_in, q_norm_g, k_norm_g, b_f, conv_w, conv_b, conv_ln_g, conv_ln_b, beta_attn, beta_conv, w_out, norm2_g, w_ff1, w_ff2, loss_target, m_w_ada, m_b_ada, m_norm1_g, m_w_in, m_q_norm_g, m_k_norm_g, m_b_f, m_conv_w, m_conv_b, m_conv_ln_g, m_conv_ln_b, m_beta_attn, m_beta_conv, m_w_out, m_norm2_g, m_w_ff1, m_w_ff2, v_w_ada, v_b_ada, v_norm1_g, v_w_in, v_q_norm_g, v_k_norm_g, v_b_f, v_conv_w, v_conv_b, v_conv_ln_g, v_conv_ln_b, v_beta_attn, v_beta_conv, v_w_out, v_norm2_g, v_w_ff1, v_w_ff2):
    S = x.shape[1]
    B = BLK
    NB = S // B
    me = 4 * lax.axis_index("x") + 2 * lax.axis_index("y") + lax.axis_index("c")
    xs = x[0]
    tgt = loss_target[0]

    ADA_C, IN_C, FF_C, CV_C = w_ada.shape[2], w_in.shape[2], w_ff1.shape[2], conv_w.shape[2]
    OUT_R, FF_R = w_out.shape[1], w_ff2.shape[1]

    g_in, g_c, g_cw = gather_two_level([w_in[0].T.astype(BF16), c, conv_w[0]], "ag_weights_early")
    late_w = [w_out[0].astype(BF16), w_ff1[0].astype(BF16), w_ff2[0].astype(BF16)]

    w_in_t = g_in.reshape(N_DEV * IN_C, D)
    c_all = g_c.reshape(N_DEV, D)
    conv_w_full = g_cw.transpose(1, 0, 2).reshape(KC, N_DEV * CV_C)
    conv_w_pad = jnp.pad(conv_w_full, ((0, 32 - KC), (0, 0)))

    wT = jnp.pad(w_in_t[0:3 * A + H], ((0, ZT_ROWS - 3 * A - H), (0, 0)))
    w_qkvf = wT.T
    wr = w_in_t[3 * A + H:].T

    qg_col = q_norm_g.reshape(DH, 1)
    kg_col = k_norm_g.reshape(DH, 1)
    bf_col = b_f.reshape(H, 1)
    beta_a_col = beta_attn.reshape(A, 1)
    ii = lax.broadcasted_iota(jnp.int32, (B, B), 0)
    jj = lax.broadcasted_iota(jnp.int32, (B, B), 1)
    tri_up = (ii <= jj).astype(BF16)
    tri_lo = (ii >= jj).astype(BF16)

    modc = mod_columns(c_all, w_ada[0], lax.dynamic_slice(b_ada, (0, me * ADA_C), (1, ADA_C)))
    gm, = gather_direct([modc], "ag_mod")
    mod = lax.dynamic_index_in_dim(gm, me, axis=1, keepdims=False).reshape(1, N_DEV * ADA_C)
    sh1, sc1, g1, sh2, sc2, g2 = [mod[:, k * D:(k + 1) * D] for k in range(6)]
    qTa, kT, kaug, vT, vaug, zqk, fgT, alg, u0, stat = fwd_in(xs, norm1_g, sc1, sh1, wT, wr, qg_col,
                                                             kg_col, bf_col, tri_up)
    fmax = stat[:, :, 0].T
    fmin = stat[:, :, 1].T
    qk_max = jnp.sqrt(jnp.max(stat[:, :, 2], axis=0) * jnp.max(stat[:, :, 3], axis=0))
    thr = -(PRUNE + (2.02 / LOG2E) * qk_max)
    oT, lse, g_wo, w1_shards, g_w2 = attn_fwd(fmax, fmin, thr, qTa, kaug, vT, late_w)
    wo_full = g_wo.reshape(D, D)
    w2_full = g_w2.reshape(FF, D)
    x1, o, u1 = conv_merge_out(u0, xs, oT, conv_w_pad, conv_b, conv_ln_g, conv_ln_b, beta_conv,
                               beta_a_col, wo_full, g1)
    dy, loss_part, dg2, ra, h2t = mlp_fwd_loss(x1, tgt, norm2_g, sc2, sh2, g2, w1_shards, w2_full)

    dw1s, dw2, dh2p = mlp_bwd(h2t, dy, ra, g2, w1_shards, w2_full)
    dx1, doTa, delta, du1, dwo, v1, v2, dba = merge_bwd(dh2p, x1, dy, o, oT, u1, norm2_g, sc2, g1,
                                                         conv_ln_g, conv_ln_b, beta_conv, beta_a_col, wo_full)
    early_g = [dwo.reshape(N_DEV, OUT_R, D), dw1s, dw2.reshape(N_DEV, FF_R, D)]
    dqT, dkT, dvT, dfk, r_out, r_f1, r_f2 = attn_bwd(fmax, fmin, thr, qTa, kaug, kT, vaug, doTa, lse,
                                                     delta, early_g)
    dzr, dzT, dcw, vc, dqg, dkg, dbf = head_bwd(du1, u0, alg, zqk, fgT, dqT, dkT, dvT, dfk, conv_w_pad,
                                                qg_col, kg_col, bf_col, tri_lo)
    dwT, dwr = in_bwd_w(xs, dzr, dzT, norm1_g, sc1, sh1)

    dw_in = jnp.concatenate([dwT.T[:, 0:3 * A + H], dwr], axis=1)
    late_g = _cols_to_shards(dw_in, IN_C)
    rows_b = -(-late_g.shape[1] // (128 * 256)) * 256
    late_g = jnp.pad(late_g, ((0, 0), (0, rows_b * 128 - late_g.shape[1]))).astype(BF16)
    grad_x, v0, recv_b = in_bwd_x(xs, dx1, dzr, dzT, norm1_g, sc1, sh1, w_qkvf, wr,
                                  late_g.reshape(N_DEV, rows_b, 128))

    dmod = jnp.concatenate([v0[0], v0[1], v1[3], v1[0], v1[1], dg2[0]])
    small1 = _pack_rows([dmod, v0[2], v1[2], dcw, vc[0], v2[1], v2[2], v2[0], dba, dqg, dkg,
                         jnp.pad(dbf.reshape(-1), (0, 120)), jnp.pad(loss_part.reshape(-1), (0, 127))])
    gs1, = gather_direct([small1], "ag_small_bwd")
    tot = sum_slots(gs1, "sum_small")
    (g_b_ada, g_n1, g_n2, g_cw, g_cb, g_lng, g_lnb, g_bc, g_ba, g_qg, g_kg, g_bf, loss_v) = _unpack_rows(
        tot, [(1, 6 * D), (1, D), (1, D), (32, CW), (1, CW), (1, CW), (1, CW), (1, CW), (1, A),
              (1, DH), (1, DH), (1, 128), (1, 128)])
    loss = loss_v[0, 0]
    g_bf = g_bf[:, 0:H]
    g_cw_mine = lax.dynamic_slice(g_cw[0:KC], (0, me * CV_C), (KC, CV_C)).reshape(1, KC, CV_C)

    small_names = [(b_ada, m_b_ada, v_b_ada, g_b_ada), (norm1_g, m_norm1_g, v_norm1_g, g_n1),
                   (q_norm_g, m_q_norm_g, v_q_norm_g, g_qg), (k_norm_g, m_k_norm_g, v_k_norm_g, g_kg),
                   (b_f, m_b_f, v_b_f, g_bf), (conv_w, m_conv_w, v_conv_w, g_cw_mine),
                   (conv_b, m_conv_b, v_conv_b, g_cb), (conv_ln_g, m_conv_ln_g, v_conv_ln_g, g_lng),
                   (conv_ln_b, m_conv_ln_b, v_conv_ln_b, g_lnb), (beta_attn, m_beta_attn, v_beta_attn, g_ba),
                   (beta_conv, m_beta_conv, v_beta_conv, g_bc), (norm2_g, m_norm2_g, v_norm2_g, g_n2)]
    sgs = [t[3].reshape(t[0].shape) for t in small_names]
    sds, sms, svs = adamw_many(sgs, *[[t[k] for t in small_names] for k in range(3)], "adamw_small")

    dmod_all = gs1[:, 0:48, :].reshape(N_DEV, N_DEV, ADA_C)
    dmod_cols = lax.dynamic_index_in_dim(dmod_all, me, axis=1, keepdims=False)
    ga, da, ma_, va_ = ada_grad_adamw(c_all.reshape(N_DEV, D, 1), dmod_cols, w_ada[0], m_w_ada[0], v_w_ada[0])

    res_b = adamw_slots(recv_b, *[_pack_rows(ws, rows=rows_b) for ws in ([w_in], [m_w_in], [v_w_in])],
                        "adamw_in")
    late = []
    for nm, parts, (w_, m_, v_) in (("adamw_out", r_out, (w_out, m_w_out, v_w_out)),
                                    ("adamw_ff1", r_f1, (w_ff1, m_w_ff1, v_w_ff1)),
                                    ("adamw_ff2", r_f2, (w_ff2, m_w_ff2, v_w_ff2))):
        late.append([r[None] for r in adamw_slots(parts, w_[0], m_[0], v_[0], nm)])
    bgs, bds, bms, bvs = [_unpack_rows(res_b[q], [(1, D, IN_C)]) + [late[0][q], late[1][q], late[2][q]]
                          for q in range(4)]

    def assemble(small, ada, bigs):
        (b_ada_, n1_, qg_, kg_, bf_, cw_, cb_, lng_, lnb_, ba_, bc_, n2_) = small
        return [ada.reshape(1, D, ADA_C), b_ada_, n1_, bigs[0], qg_, kg_, bf_, cw_, cb_, lng_, lnb_, ba_, bc_,
                bigs[1], n2_, bigs[2], bigs[3]]

    return (loss, grad_x.reshape(1, S, D), *assemble(sgs, ga, bgs), *assemble(sds, da, bds),
            *assemble(sms, ma_, bms), *assemble(svs, va_, bvs))
```
